```python
import math
import jax, jax.numpy as jnp
from jax import lax
import numpy as np

D_MODEL = 1024
BATCH = 8
SEQ = 2048
DEPTH = 1
DEC_BATCH = 128
DEC_SEQ = 1
PAST_LEN = 16384
PAGE_SIZE = 128

D_MIX = D_MODEL
D_CONV = D_MIX // 2
D_RET = D_MIX - D_CONV
RET_HEADS = 4
RET_HD = D_RET // RET_HEADS
CONV_WIDTH = 31
CONV_BUF = CONV_WIDTH - 1
RET_CHUNK = 128
ROPE_BASE = 10000.0
N_EXPERTS = 64
TOP_K = 6
N_GROUPS = 8
TOPK_GROUPS = 4
D_EXPERT = D_MODEL // 4
D_SHARED = D_EXPERT
ROUTED_SCALE = 2.5
MOE_BLOCK = 128
EPS = 1e-6
D_IN = 2 * D_CONV + 4 * D_RET

kernel_name = 'hybrid_conv_retention_moe_step'

F32 = jnp.float32


def rmsnorm(x, g):
    xf = x.astype(F32)
    y = xf * lax.rsqrt(jnp.mean(xf * xf, axis=-1, keepdims=True) + EPS)
    return (y * g.astype(F32)).astype(x.dtype)


def layernorm(x, g, b):
    xf = x.astype(F32)
    mu = jnp.mean(xf, axis=-1, keepdims=True)
    var = jnp.mean(jnp.square(xf - mu), axis=-1, keepdims=True)
    return ((xf - mu) * lax.rsqrt(var + EPS) * g.astype(F32) + b.astype(F32)).astype(x.dtype)


def rotary(x, pos):
    half = x.shape[-1] // 2
    inv = ROPE_BASE ** (-jnp.arange(half, dtype=F32) / half)
    ang = pos[:, None] * inv[None, :]
    cos = jnp.cos(ang)[None, :, None, :]
    sin = jnp.sin(ang)[None, :, None, :]
    xf = x.astype(F32)
    x1, x2 = xf[..., :half], xf[..., half:]
    return jnp.concatenate([x1 * cos - x2 * sin, x1 * sin + x2 * cos], axis=-1)


def retention(q, k, v, s0):
    B, L, H, D = q.shape
    C = math.gcd(L, RET_CHUNK)
    n = L // C
    log_g = jnp.log(1.0 - 2.0 ** (-5.0 - jnp.arange(H, dtype=F32)))
    idx = jnp.arange(C, dtype=F32)
    rel = idx[:, None] - idx[None, :]
    mask = jnp.where(rel >= 0, jnp.exp(log_g[:, None, None] * jnp.maximum(rel, 0.0)), 0.0)
    q_decay = jnp.exp(log_g[None, :] * (idx[:, None] + 1.0))
    k_decay = jnp.exp(log_g[None, :] * (C - 1.0 - idx[:, None]))
    chunk_decay = jnp.exp(log_g * C)

    def split(t):
        return t.reshape(B, n, C, H, D).swapaxes(0, 1)

    def step(s, qkv):
        qc, kc, vc = qkv
        scores = jnp.einsum('bchd,bshd->bhcs', qc, kc) * mask[None]
        inner = jnp.einsum('bhcs,bshe->bche', scores, vc)
        cross = jnp.einsum('bchd,bhde->bche', qc * q_decay[None, :, :, None], s)
        s_new = chunk_decay[None, :, None, None] * s + jnp.einsum(
            'bshd,bshe->bhde', kc * k_decay[None, :, :, None], vc)
        return s_new, inner + cross

    s_fin, out = lax.scan(step, s0, (split(q), split(k), split(v)))
    return out.swapaxes(0, 1).reshape(B, L, H, D), s_fin


def mixer(h, conv_buf, ret_s, pos0, w_in, conv_w, conv_b, conv_norm_g, conv_norm_b,
          ret_norm_g, ret_norm_b, w_out):
    B, L, _ = h.shape
    u = h @ w_in
    a, b, q, k, v, g = jnp.split(
        u, [D_CONV, 2 * D_CONV, 2 * D_CONV + D_RET, 2 * D_CONV + 2 * D_RET, 2 * D_CONV + 3 * D_RET], axis=-1)
    glu = a * jax.nn.sigmoid(b)
    xpad = jnp.concatenate([conv_buf.astype(glu.dtype), glu], axis=1)
    conv = lax.conv_general_dilated(
        xpad, conv_w[:, None, :].astype(glu.dtype), window_strides=(1,), padding='VALID',
        dimension_numbers=('NWC', 'WIO', 'NWC'), feature_group_count=D_CONV) + conv_b.astype(glu.dtype)
    new_conv_buf = xpad[:, -CONV_BUF:]
    conv_out = jax.nn.silu(layernorm(conv, conv_norm_g, conv_norm_b))
    pos = jnp.arange(L, dtype=F32) + pos0
    qh = rotary(q.reshape(B, L, RET_HEADS, RET_HD), pos) * (RET_HD ** -0.5)
    kh = rotary(k.reshape(B, L, RET_HEADS, RET_HD), pos)
    vh = v.reshape(B, L, RET_HEADS, RET_HD).astype(F32)
    o, s_new = retention(qh, kh, vh, ret_s.astype(F32))
    mu = jnp.mean(o, axis=-1, keepdims=True)
    var = jnp.mean(jnp.square(o - mu), axis=-1, keepdims=True)
    o = ((o - mu) * lax.rsqrt(var + EPS)).reshape(B, L, D_RET)
    o = o * ret_norm_g.astype(F32) + ret_norm_b.astype(F32)
    ret_out = o.astype(h.dtype) * jax.nn.silu(g)
    y = jnp.concatenate([conv_out, ret_out], axis=-1) @ w_out
    return y, new_conv_buf, s_new.astype(ret_s.dtype)


def moe(h, w_router, router_bias, w1, w3, w2, ws1, ws3, ws2):
    B, L, Dm = h.shape
    T = B * L
    t = h.reshape(T, Dm)
    scores = jax.nn.sigmoid((t @ w_router).astype(F32))
    sel = scores + router_bias.astype(F32)
    per = N_EXPERTS // N_GROUPS
    grp_score = lax.top_k(sel.reshape(T, N_GROUPS, per), 2)[0].sum(-1)
    _, gidx = lax.top_k(grp_score, TOPK_GROUPS)
    keep = jnp.zeros((T, N_GROUPS), bool).at[jnp.arange(T)[:, None], gidx].set(True)
    sel = jnp.where(jnp.repeat(keep, per, axis=1), sel, -jnp.inf)
    _, eidx = lax.top_k(sel, TOP_K)
    wts = jnp.take_along_axis(scores, eidx, axis=1)
    wts = wts / jnp.sum(wts, axis=-1, keepdims=True) * ROUTED_SCALE
    A = T * TOP_K
    flat_e = eidx.reshape(A)
    order = jnp.argsort(flat_e)
    se = flat_e[order]
    stok = (order // TOP_K).astype(jnp.int32)
    sw = wts.reshape(A)[order]
    counts = jnp.bincount(flat_e, length=N_EXPERTS)
    padded = (counts + MOE_BLOCK - 1) // MOE_BLOCK * MOE_BLOCK
    pend = jnp.cumsum(padded)
    pstart = pend - padded
    start = jnp.cumsum(counts) - counts
    dest = pstart[se] + jnp.arange(A) - start[se]
    n_blocks = -(-(A + N_EXPERTS * (MOE_BLOCK - 1)) // MOE_BLOCK)
    cap = n_blocks * MOE_BLOCK
    tok_buf = jnp.zeros((cap,), jnp.int32).at[dest].set(stok)
    w_buf = jnp.zeros((cap,), F32).at[dest].set(sw)
    blk_e = jnp.minimum(jnp.searchsorted(pend, jnp.arange(n_blocks) * MOE_BLOCK, side='right'),
                        N_EXPERTS - 1)
    xb = t[tok_buf].reshape(n_blocks, MOE_BLOCK, Dm)

    def run_block(args):
        xblk, e = args
        hid = jax.nn.silu(xblk @ w1[e]) * (xblk @ w3[e])
        return hid @ w2[e]

    yb = lax.map(run_block, (xb, blk_e))
    routed = jnp.zeros((T, Dm), F32).at[tok_buf].add(yb.reshape(cap, Dm).astype(F32) * w_buf[:, None])
    shared = (jax.nn.silu(t @ ws1) * (t @ ws3)) @ ws2
    return (routed + shared.astype(F32)).astype(h.dtype).reshape(B, L, Dm)


def layer(x, c, conv_buf, ret_s, pos0, w_ada, b_ada, g_mix, g_ffn, w_in, conv_w, conv_b,
          conv_norm_g, conv_norm_b, ret_norm_g, ret_norm_b, w_out, w_router, router_bias,
          w1, w3, w2, ws1, ws3, ws2):
    mod = (jax.nn.silu(c) @ w_ada + b_ada)[:, None, :]
    sh_m, sc_m, gt_m, sh_f, sc_f, gt_f = jnp.split(mod, 6, axis=-1)
    h = rmsnorm(x, g_mix) * (1 + sc_m) + sh_m
    y, new_buf, new_s = mixer(h, conv_buf, ret_s, pos0, w_in, conv_w, conv_b, conv_norm_g,
                              conv_norm_b, ret_norm_g, ret_norm_b, w_out)
    x = x + gt_m * y
    h = rmsnorm(x, g_ffn) * (1 + sc_f) + sh_f
    x = x + gt_f * moe(h, w_router, router_bias, w1, w3, w2, ws1, ws3, ws2)
    return x, new_buf, new_s


def setup_inputs(seed: int = 0) -> dict:
    key = jax.random.key(seed)
    ks = jax.random.split(key, 32)
    nrm = jax.random.normal
    D = D_MODEL
    return {
        'x_prompt': nrm(ks[0], (BATCH, SEQ, D), F32),
        'x_sample': nrm(ks[1], (DEC_BATCH, DEC_SEQ, D), F32),
        'c_prompt': nrm(ks[2], (BATCH, D), F32),
        'c_sample': nrm(ks[3], (DEC_BATCH, D), F32),
        'state_conv': 0.5 * nrm(ks[4], (DEPTH, DEC_BATCH, CONV_BUF, D_CONV), F32),
        'state_ret': 4.0 * nrm(ks[5], (DEPTH, DEC_BATCH, RET_HEADS, RET_HD, RET_HD), F32),
        'w_ada': 0.5 * D ** -0.5 * nrm(ks[6], (DEPTH, D, 6 * D), F32),
        'b_ada': 0.02 * nrm(ks[7], (DEPTH, 6 * D), F32),
        'g_mix': 1.0 + 0.02 * nrm(ks[8], (DEPTH, D), F32),
        'g_ffn': 1.0 + 0.02 * nrm(ks[9], (DEPTH, D), F32),
        'w_in': D ** -0.5 * nrm(ks[10], (DEPTH, D, D_IN), F32),
        'conv_w': CONV_WIDTH ** -0.5 * nrm(ks[11], (DEPTH, CONV_WIDTH, D_CONV), F32),
        'conv_b': 0.02 * nrm(ks[12], (DEPTH, D_CONV), F32),
        'conv_norm_g': 1.0 + 0.02 * nrm(ks[13], (DEPTH, D_CONV), F32),
        'conv_norm_b': 0.02 * nrm(ks[14], (DEPTH, D_CONV), F32),
        'ret_norm_g': 1.0 + 0.02 * nrm(ks[15], (DEPTH, D_RET), F32),
        'ret_norm_b': 0.02 * nrm(ks[16], (DEPTH, D_RET), F32),
        'w_out': D_MIX ** -0.5 * nrm(ks[17], (DEPTH, D_MIX, D), F32),
        'w_router': D ** -0.5 * nrm(ks[18], (DEPTH, D, N_EXPERTS), F32),
        'router_bias': 0.01 * nrm(ks[19], (DEPTH, N_EXPERTS), F32),
        'w1': D ** -0.5 * nrm(ks[20], (DEPTH, N_EXPERTS, D, D_EXPERT), F32),
        'w3': D ** -0.5 * nrm(ks[21], (DEPTH, N_EXPERTS, D, D_EXPERT), F32),
        'w2': D_EXPERT ** -0.5 * nrm(ks[22], (DEPTH, N_EXPERTS, D_EXPERT, D), F32),
        'ws1': D ** -0.5 * nrm(ks[23], (DEPTH, D, D_SHARED), F32),
        'ws3': D ** -0.5 * nrm(ks[24], (DEPTH, D, D_SHARED), F32),
        'ws2': D_SHARED ** -0.5 * nrm(ks[25], (DEPTH, D_SHARED, D), F32),
        'g_final': 1.0 + 0.02 * nrm(ks[26], (D,), F32),
    }


def reference(x_prompt, x_sample, c_prompt, c_sample, state_conv, state_ret, w_ada, b_ada,
              g_mix, g_ffn, w_in, conv_w, conv_b, conv_norm_g, conv_norm_b, ret_norm_g,
              ret_norm_b, w_out, w_router, router_bias, w1, w3, w2, ws1, ws3, ws2, g_final):
    bp = x_prompt.shape[0]
    xp, xs = x_prompt, x_sample
    conv_p, ret_p, conv_s, ret_s = [], [], [], []
    for l in range(DEPTH):
        params = (w_ada[l], b_ada[l], g_mix[l], g_ffn[l], w_in[l], conv_w[l], conv_b[l],
                  conv_norm_g[l], conv_norm_b[l], ret_norm_g[l], ret_norm_b[l], w_out[l],
                  w_router[l], router_bias[l], w1[l], w3[l], w2[l], ws1[l], ws3[l], ws2[l])
        zero_conv = jnp.zeros((bp, CONV_BUF, D_CONV), x_prompt.dtype)
        zero_ret = jnp.zeros((bp, RET_HEADS, RET_HD, RET_HD), x_prompt.dtype)
        xp, cbp, rsp = layer(xp, c_prompt, zero_conv, zero_ret, 0, *params)
        xs, cbs, rss = layer(xs, c_sample, state_conv[l], state_ret[l], PAST_LEN, *params)
        conv_p.append(cbp)
        ret_p.append(rsp)
        conv_s.append(cbs)
        ret_s.append(rss)
    y_prompt = rmsnorm(xp, g_final)
    y_sample = rmsnorm(xs, g_final)
    return (y_prompt, y_sample, jnp.stack(conv_p), jnp.stack(ret_p), jnp.stack(conv_s), jnp.stack(ret_s))
```

```python
import functools
import math

import jax
import jax.numpy as jnp
from jax import lax
from jax.experimental import pallas as pl
from jax.experimental.pallas import tpu as pltpu

F32 = jnp.float32
BF16 = jnp.bfloat16
U32 = jnp.uint32
I32 = jnp.int32

EPS = 1e-6
PAST_LEN = 16384
RET_HEADS = 4
RET_CHUNK = 128
CONV_WIDTH = 31
CONV_BUF = CONV_WIDTH - 1
ROPE_BASE = 10000.0
N_EXPERTS = 64
TOP_K = 6
N_GROUPS = 8
TOPK_GROUPS = 4
ROUTED_SCALE = 2.5

LANES = 128
SUBLANES = 8
CONV_PAD = 32
EXPERT_ROWS = 256
ROUTE_TILE = 384
TOKEN_TILE = 128
VMEM_LIMIT = 56 * 1024 * 1024

HI_MASK = 0xFFFF0000


def _sigmoid(x):
    return jax.nn.sigmoid(x)


def _silu(x):
    return x * jax.nn.sigmoid(x)


def _pack_halves(lo, hi):
    lo_u = lax.bitcast_convert_type(lo.astype(BF16).astype(F32), U32) >> 16
    hi_u = lax.bitcast_convert_type(hi.astype(BF16).astype(F32), U32) & jnp.uint32(HI_MASK)
    return hi_u | lo_u


def _unpack_halves(p):
    lo = lax.bitcast_convert_type(p << 16, F32)
    hi = lax.bitcast_convert_type(p & jnp.uint32(HI_MASK), F32)
    return lo, hi


def _ada_body(c_ref, w_ref, b_ref, o_ref):
    s = _silu(c_ref[...]).astype(BF16)
    o_ref[...] = jnp.dot(s, w_ref[...].astype(BF16), preferred_element_type=F32) + b_ref[...]


def _ada(c_all, w_ada, b_ada):
    rows, d = c_all.shape
    n = w_ada.shape[1]
    tn = 512
    return pl.pallas_call(
        _ada_body,
        grid=(n // tn,),
        in_specs=[
            pl.BlockSpec((rows, d), lambda j: (0, 0)),
            pl.BlockSpec((d, tn), lambda j: (0, j)),
            pl.BlockSpec((1, tn), lambda j: (0, j)),
        ],
        out_specs=pl.BlockSpec((rows, tn), lambda j: (0, j)),
        out_shape=jax.ShapeDtypeStruct((rows, n), F32),
        name="ada",
    )(c_all, w_ada, b_ada)


def _rope_body(inv_ref, cos_ref, sin_ref, *, pos0, tl, half):
    row = lax.broadcasted_iota(I32, (tl, LANES), 0) + pl.program_id(0) * tl
    ang = (row.astype(F32) + pos0) * inv_ref[...]
    lane = lax.broadcasted_iota(I32, (tl, LANES), 1)
    s = jnp.sin(ang)
    cos_ref[...] = jnp.cos(ang)
    sin_ref[...] = jnp.where(lane < half, -s, s)


def _rope_tables(inv2, rows, pos0):
    tl = min(rows, 256)
    return pl.pallas_call(
        functools.partial(_rope_body, pos0=float(pos0), tl=tl, half=LANES // 2),
        grid=(rows // tl,),
        in_specs=[pl.BlockSpec((1, LANES), lambda i: (0, 0))],
        out_specs=[pl.BlockSpec((tl, LANES), lambda i: (i, 0))] * 2,
        out_shape=[jax.ShapeDtypeStruct((rows, LANES), F32)] * 2,
        name="rope",
    )(inv2)


def _modulated_rmsnorm(x, g, sc, sh):
    ms = jnp.mean(x * x, axis=-1, keepdims=True)
    h = x * lax.rsqrt(ms + EPS) * g
    return h * (1.0 + sc) + sh


def _proj_body(x_ref, sh_ref, sc_ref, g_ref, w_ref, cos_ref, sin_ref,
               glu_ref, q_ref, k_ref, v_ref, sg_ref, *, dc, dr, hd, rope_rows):
    hb = _modulated_rmsnorm(x_ref[...], g_ref[...], sc_ref[...], sh_ref[...]).astype(BF16)

    def proj(lo, n):
        return jnp.dot(hb, w_ref[:, lo:lo + n], preferred_element_type=F32)

    glu_ref[...] = proj(0, dc) * _sigmoid(proj(dc, dc))
    cos = cos_ref[...] if rope_rows else cos_ref[0:1, :]
    sin = sin_ref[...] if rope_rows else sin_ref[0:1, :]
    for ref, lo, scale in ((q_ref, 2 * dc, hd ** -0.5), (k_ref, 2 * dc + dr, None)):
        t = proj(lo, dr)
        for hh in range(dr // hd):
            th = t[:, hh * hd:(hh + 1) * hd]
            r = th * cos + pltpu.roll(th, hd // 2, 1) * sin
            if scale is not None:
                r = r * scale
            ref[:, hh * hd:(hh + 1) * hd] = r.astype(BF16)
    v_ref[...] = proj(2 * dc + 2 * dr, dr).astype(BF16)
    sg_ref[...] = _silu(proj(2 * dc + 3 * dr, dr)).astype(BF16)


def _proj(x, mod, g_mix, w_in_b, cos2, sin2, *, dc, dr, hd, per_row_mod):
    nb, length, d = x.shape
    tl = min(length, 256)
    x2 = x.reshape(nb * length, d)
    nl = length // tl
    if per_row_mod:
        mod_spec = lambda j: pl.BlockSpec((tl, d), lambda b, l: (l, j))
        rope_spec = pl.BlockSpec((SUBLANES, LANES), lambda b, l: (0, 0))
    else:
        mod_spec = lambda j: pl.BlockSpec((None, None, 1, d), lambda b, l: (b, j, 0, 0))
        rope_spec = pl.BlockSpec((tl, LANES), lambda b, l: (l, 0))
    row_spec = lambda w: pl.BlockSpec((tl, w), lambda b, l: (b * nl + l, 0))
    t = nb * length
    outs = pl.pallas_call(
        functools.partial(_proj_body, dc=dc, dr=dr, hd=hd, rope_rows=not per_row_mod),
        grid=(nb, nl),
        in_specs=[
            row_spec(d), mod_spec(0), mod_spec(1),
            pl.BlockSpec((1, d), lambda b, l: (0, 0)),
            pl.BlockSpec(w_in_b.shape, lambda b, l: (0, 0)),
            rope_spec, rope_spec,
        ],
        out_specs=[row_spec(dc), row_spec(dr), row_spec(dr), row_spec(dr), row_spec(dr)],
        out_shape=[
            jax.ShapeDtypeStruct((t, dc), F32),
            jax.ShapeDtypeStruct((t, dr), BF16),
            jax.ShapeDtypeStruct((t, dr), BF16),
            jax.ShapeDtypeStruct((t, dr), BF16),
            jax.ShapeDtypeStruct((t, dr), BF16),
        ],
        compiler_params=pltpu.CompilerParams(vmem_limit_bytes=VMEM_LIMIT),
        name="proj",
    )(x2, mod, mod, g_mix, w_in_b, cos2, sin2)
    return outs


def _layernorm_silu(c, g, b):
    mu = jnp.mean(c, axis=-1, keepdims=True)
    d = c - mu
    var = jnp.mean(d * d, axis=-1, keepdims=True)
    return _silu(d * lax.rsqrt(var + EPS) * g + b)


def _groupnorm(o, g, b):
    mu = jnp.mean(o, axis=-1, keepdims=True)
    d = o - mu
    var = jnp.mean(d * d, axis=-1, keepdims=True)
    return d * lax.rsqrt(var + EPS) * g + b


def _mix_body(glu_ref, q_ref, k_ref, v_ref, sg_ref, cw_ref, cb_ref, lng_ref, lnb_ref, rg_ref, rb_ref,
              mask_ref, qd_ref, kd_ref, cd_ref, cat_ref, st_ref, buf, cscr, *, tl, dc, hd, nh, chunk):
    nslab = dc // LANES

    @pl.when(pl.program_id(1) == 0)
    def _():
        buf[:, 0:CONV_PAD, :] = jnp.zeros((nslab, CONV_PAD, LANES), F32)
        st_ref[...] = jnp.zeros(st_ref.shape, F32)

    for j in range(nslab):
        buf[j, CONV_PAD:CONV_PAD + tl, :] = glu_ref[:, LANES * j:LANES * (j + 1)]
    first = CONV_PAD - CONV_BUF
    rows_per_iter = 4 * SUBLANES
    for j in range(nslab):
        cols = slice(LANES * j, LANES * (j + 1))
        wv = [jnp.broadcast_to(cw_ref[t:t + 1, cols], (SUBLANES, LANES)) for t in range(CONV_WIDTH)]
        bias = jnp.broadcast_to(cb_ref[0:1, cols], (SUBLANES, LANES))

        def body(r, carry, j=j, cols=cols, wv=wv, bias=bias):
            base = pl.multiple_of(r * rows_per_iter, rows_per_iter)
            for u in range(rows_per_iter // SUBLANES):
                acc = bias
                for t in range(CONV_WIDTH):
                    acc = acc + wv[t] * buf[j, pl.ds(base + (u * SUBLANES + first + t), SUBLANES), :]
                cscr[pl.ds(base + u * SUBLANES, SUBLANES), cols] = acc
            return carry

        lax.fori_loop(0, tl // rows_per_iter, body, 0)
    for j in range(nslab):
        buf[j, 0:CONV_PAD, :] = buf[j, tl:tl + CONV_PAD, :]
    cat_ref[:, 0:dc] = _layernorm_silu(cscr[...], lng_ref[...], lnb_ref[...]).astype(BF16)

    nt = (((1,), (1,)), ((), ()))
    tn = (((0,), (0,)), ((), ()))
    for c in range(tl // chunk):
        rows = slice(c * chunk, (c + 1) * chunk)
        for hh in range(nh):
            cols = slice(hh * hd, (hh + 1) * hd)
            qh = q_ref[rows, cols]
            kh = k_ref[rows, cols]
            vh = v_ref[rows, cols]
            s = st_ref[0, hh]
            scores = lax.dot_general(qh, kh, nt, preferred_element_type=F32) * mask_ref[hh]
            inner = jnp.dot(scores.astype(BF16), vh, preferred_element_type=F32)
            qd = (qh.astype(F32) * qd_ref[hh]).astype(BF16)
            cross = jnp.dot(qd, s.astype(BF16), preferred_element_type=F32)
            kd = (kh.astype(F32) * kd_ref[hh]).astype(BF16)
            st_ref[0, hh] = cd_ref[hh] * s + lax.dot_general(kd, vh, tn, preferred_element_type=F32)
            o = _groupnorm(inner + cross, rg_ref[0:1, cols], rb_ref[0:1, cols])
            cat_ref[rows, dc + hh * hd:dc + (hh + 1) * hd] = (o * sg_ref[rows, cols].astype(F32)).astype(BF16)


def _mix(glu, q, k, v, sg, conv_w, conv_b, ln_g, ln_b, rg, rb, tables, *, nb, length, dc, dr, hd):
    nh = dr // hd
    chunk = math.gcd(length, RET_CHUNK)
    tl = min(length, 256)
    nl = length // tl
    mask, qd, kd, cd = tables
    row_spec = lambda w: pl.BlockSpec((tl, w), lambda b, l: (b * nl + l, 0))
    full = lambda a: pl.BlockSpec(a.shape, lambda b, l: (0,) * a.ndim)
    cat, st = pl.pallas_call(
        functools.partial(_mix_body, tl=tl, dc=dc, hd=hd, nh=nh, chunk=chunk),
        grid=(nb, nl),
        in_specs=[row_spec(dc), row_spec(dr), row_spec(dr), row_spec(dr), row_spec(dr),
                  full(conv_w), full(conv_b), full(ln_g), full(ln_b), full(rg), full(rb),
                  full(mask), full(qd), full(kd), full(cd)],
        out_specs=[row_spec(dc + dr), pl.BlockSpec((1, nh, hd, hd), lambda b, l: (b, 0, 0, 0))],
        out_shape=[jax.ShapeDtypeStruct((nb * length, dc + dr), BF16),
                   jax.ShapeDtypeStruct((nb, nh, hd, hd), F32)],
        scratch_shapes=[pltpu.VMEM((dc // LANES, tl + CONV_PAD, LANES), F32),
                        pltpu.VMEM((tl, dc), F32)],
        compiler_params=pltpu.CompilerParams(dimension_semantics=("arbitrary", "arbitrary")),
        name="mix",
    )(glu, q, k, v, sg, conv_w, conv_b, ln_g, ln_b, rg, rb, mask, qd, kd, cd)
    return cat, st


def _mix1_body(glu_ref, q_ref, k_ref, v_ref, sg_ref, sc_ref, s0_ref, cw_ref, cb_ref, lng_ref, lnb_ref,
               rg_ref, rb_ref, gam_ref, cat_ref, st_ref, cscr, oscr, qf, kf, vf, *, tb, dc, hd, nh):
    w_hist = cw_ref[0:CONV_BUF, :]
    w_last = cw_ref[CONV_BUF:CONV_WIDTH, :]
    for bb in range(tb):
        hist = jnp.sum(sc_ref[bb] * w_hist, axis=0, keepdims=True)
        cscr[bb:bb + 1, :] = hist + glu_ref[bb:bb + 1, :] * w_last + cb_ref[...]
    cat_ref[:, 0:dc] = _layernorm_silu(cscr[...], lng_ref[...], lnb_ref[...]).astype(BF16)

    tn = (((0,), (0,)), ((), ()))
    rowid = lax.broadcasted_iota(I32, (tb, hd), 0)
    qf[...] = q_ref[...].astype(F32)
    kf[...] = k_ref[...].astype(F32)
    vf[...] = v_ref[...].astype(F32)
    for hh in range(nh):
        cols = slice(hh * hd, (hh + 1) * hd)
        qa = q_ref[:, cols]
        ka = k_ref[:, cols]
        gam = gam_ref[hh, 0:1, :]
        for bb in range(tb):
            onehot = (rowid == bb).astype(BF16)
            qcol = lax.dot_general(qa, onehot, tn, preferred_element_type=F32)
            kcol = lax.dot_general(ka, onehot, tn, preferred_element_type=F32)
            s0 = s0_ref[bb, hh]
            qrow = qf[bb:bb + 1, cols]
            krow = kf[bb:bb + 1, cols]
            vrow = vf[bb:bb + 1, cols]
            qk = jnp.sum(qrow * krow, axis=-1, keepdims=True)
            cross = gam * jnp.sum(qcol * s0, axis=0, keepdims=True)
            st_ref[bb, hh] = gam * s0 + kcol * vrow
            oscr[bb:bb + 1, cols] = qk * vrow + cross
    for hh in range(nh):
        cols = slice(hh * hd, (hh + 1) * hd)
        o = _groupnorm(oscr[:, cols], rg_ref[0:1, cols], rb_ref[0:1, cols])
        cat_ref[:, dc + hh * hd:dc + (hh + 1) * hd] = (o * sg_ref[:, cols].astype(F32)).astype(BF16)


def _mix1(glu, q, k, v, sg, state_conv, state_ret, conv_w, conv_b, ln_g, ln_b, rg, rb, gam, *, dc, dr, hd):
    nb = glu.shape[0]
    nh = dr // hd
    tb = 16
    row_spec = lambda w: pl.BlockSpec((tb, w), lambda i: (i, 0))
    full = lambda a: pl.BlockSpec(a.shape, lambda i: (0,) * a.ndim)
    st_spec = pl.BlockSpec((tb, nh, hd, hd), lambda i: (i, 0, 0, 0))
    cat, st = pl.pallas_call(
        functools.partial(_mix1_body, tb=tb, dc=dc, hd=hd, nh=nh),
        grid=(nb // tb,),
        in_specs=[row_spec(dc), row_spec(dr), row_spec(dr), row_spec(dr), row_spec(dr),
                  pl.BlockSpec((tb, CONV_BUF, dc), lambda i: (i, 0, 0)), st_spec,
                  full(conv_w), full(conv_b), full(ln_g), full(ln_b), full(rg), full(rb), full(gam)],
        out_specs=[row_spec(dc + dr), st_spec],
        out_shape=[jax.ShapeDtypeStruct((nb, dc + dr), BF16),
                   jax.ShapeDtypeStruct((nb, nh, hd, hd), F32)],
        scratch_shapes=[pltpu.VMEM((tb, dc), F32)] + [pltpu.VMEM((tb, dr), F32)] * 4,
        compiler_params=pltpu.CompilerParams(vmem_limit_bytes=VMEM_LIMIT),
        name="mix1",
    )(glu, q, k, v, sg, state_conv, state_ret, conv_w, conv_b, ln_g, ln_b, rg, rb, gam)
    return cat, st


def _post_body(cat_ref, x_ref, gtm_ref, scf_ref, shf_ref, gtf_ref, g_ref, wo_ref, wrh_ref, wrl_ref,
               ws1_ref, ws3_ref, ws2_ref, x2_ref, hp_ref, lg_ref):
    d = x_ref.shape[1]
    y = jnp.dot(cat_ref[...], wo_ref[...], preferred_element_type=F32)
    x1 = x_ref[...] + gtm_ref[...] * y
    h = _modulated_rmsnorm(x1, g_ref[...], scf_ref[...], shf_ref[...])
    hb = h.astype(BF16)
    hp_ref[...] = _pack_halves(h[:, 0:d // 2], h[:, d // 2:d])
    hl = (h - hb.astype(F32)).astype(BF16)
    nt = (((1,), (1,)), ((), ()))
    lg_ref[...] = (lax.dot_general(wrh_ref[...], hb, nt, preferred_element_type=F32)
                   + lax.dot_general(wrh_ref[...], hl, nt, preferred_element_type=F32)
                   + lax.dot_general(wrl_ref[...], hb, nt, preferred_element_type=F32))
    s1 = jnp.dot(hb, ws1_ref[...], preferred_element_type=F32)
    s3 = jnp.dot(hb, ws3_ref[...], preferred_element_type=F32)
    shared = jnp.dot((_silu(s1) * s3).astype(BF16), ws2_ref[...], preferred_element_type=F32)
    x2_ref[...] = x1 + gtf_ref[...] * shared


def _post(cat, x, mod, g_ffn, wo_b, wrh, wrl, ws1_b, ws3_b, ws2_b, *, per_row_mod):
    nb, length, d = x.shape
    tl = min(length, 512)
    nl = length // tl
    t = nb * length
    ne = wrh.shape[0]
    x2d = x.reshape(t, d)
    if per_row_mod:
        mod_spec = lambda j: pl.BlockSpec((tl, d), lambda b, l: (l, j))
    else:
        mod_spec = lambda j: pl.BlockSpec((None, None, 1, d), lambda b, l: (b, j, 0, 0))
    row_spec = lambda w: pl.BlockSpec((tl, w), lambda b, l: (b * nl + l, 0))
    full = lambda a: pl.BlockSpec(a.shape, lambda b, l: (0,) * a.ndim)
    return pl.pallas_call(
        _post_body,
        grid=(nb, nl),
        in_specs=[row_spec(d), row_spec(d), mod_spec(2), mod_spec(4), mod_spec(3), mod_spec(5),
                  full(g_ffn), full(wo_b), full(wrh), full(wrl), full(ws1_b), full(ws3_b), full(ws2_b)],
        out_specs=[row_spec(d), row_spec(d // 2), pl.BlockSpec((ne, tl), lambda b, l: (0, b * nl + l))],
        out_shape=[jax.ShapeDtypeStruct((t, d), F32),
                   jax.ShapeDtypeStruct((t, d // 2), U32),
                   jax.ShapeDtypeStruct((ne, t), F32)],
        compiler_params=pltpu.CompilerParams(vmem_limit_bytes=VMEM_LIMIT),
        name="post",
    )(cat, x2d, mod, mod, mod, mod, g_ffn, wo_b, wrh, wrl, ws1_b, ws3_b, ws2_b)


def _first_max(x, idx, sentinel):
    m = jnp.max(x, axis=0, keepdims=True)
    f = jnp.min(jnp.where(x == m, idx, sentinel), axis=0, keepdims=True)
    return m, f


def _route_body(lg_ref, bias_ref, e_ref, w_ref, r_ref, cnt_ref, cnt_scr, *, tr, ne, ng, topk, topg):
    @pl.when(pl.program_id(0) == 0)
    def _():
        cnt_scr[...] = jnp.zeros(cnt_scr.shape, F32)

    per = ne // ng
    neg = -jnp.inf
    scores = _sigmoid(lg_ref[...])
    sel = scores + bias_ref[...]
    sub = lax.broadcasted_iota(I32, (per, tr), 0)
    gs = []
    for g in range(ng):
        s_g = sel[g * per:(g + 1) * per, :]
        m1, f1 = _first_max(s_g, sub, per)
        m2 = jnp.max(jnp.where(sub == f1, neg, s_g), axis=0, keepdims=True)
        gs.append(m1 + m2)
    gsc = jnp.concatenate(gs, axis=0)
    gi = lax.broadcasted_iota(I32, (ng, tr), 0)
    keep = jnp.zeros((ng, tr), F32)
    for _ in range(topg):
        _, f = _first_max(gsc, gi, ng)
        pick = gi == f
        keep = jnp.where(pick, 1.0, keep)
        gsc = jnp.where(pick, neg, gsc)
    work = jnp.concatenate(
        [jnp.where(keep[g:g + 1, :] > 0.5, sel[g * per:(g + 1) * per, :], neg) for g in range(ng)], axis=0)
    ei = lax.broadcasted_iota(I32, (ne, tr), 0)
    picks, es, ws = [], [], []
    for _ in range(topk):
        _, f = _first_max(work, ei, ne)
        pick = ei == f
        picks.append(pick)
        es.append(f)
        ws.append(jnp.sum(jnp.where(pick, scores, 0.0), axis=0, keepdims=True))
        work = jnp.where(pick, neg, work)
    wsum = ws[0]
    for w in ws[1:]:
        wsum = wsum + w
    scale = ROUTED_SCALE / wsum
    chosen = picks[0]
    for p in picks[1:]:
        chosen = jnp.logical_or(chosen, p)
    chosen_f = chosen.astype(F32)
    t_row = lax.broadcasted_iota(I32, (tr, tr), 0)
    t_col = lax.broadcasted_iota(I32, (tr, tr), 1)
    before = (t_row < t_col).astype(BF16)
    prior = cnt_scr[:, 0:1] + jnp.dot(chosen_f.astype(BF16), before, preferred_element_type=F32)
    rs = [jnp.sum(jnp.where(p, prior, 0.0), axis=0, keepdims=True).astype(I32) for p in picks]
    pad_i = jnp.zeros((SUBLANES - topk, tr), I32)
    pad_f = jnp.zeros((SUBLANES - topk, tr), F32)
    e_ref[...] = jnp.concatenate(es + [pad_i], axis=0)
    w_ref[...] = jnp.concatenate([w * scale for w in ws] + [pad_f], axis=0)
    r_ref[...] = jnp.concatenate(rs + [pad_i], axis=0)
    total = cnt_scr[:, 0:1] + jnp.sum(chosen_f, axis=1, keepdims=True)
    cnt_scr[...] = jnp.broadcast_to(total, cnt_scr.shape)
    cnt_ref[...] = jnp.broadcast_to(total, cnt_ref.shape)


def _route(logits_t, bias_col):
    ne, t = logits_t.shape
    tr = ROUTE_TILE if t % ROUTE_TILE == 0 else LANES
    tok = lambda dt: jax.ShapeDtypeStruct((SUBLANES, t), dt)
    tok_spec = pl.BlockSpec((SUBLANES, tr), lambda i: (0, i))
    return pl.pallas_call(
        functools.partial(_route_body, tr=tr, ne=ne, ng=N_GROUPS, topk=TOP_K, topg=TOPK_GROUPS),
        grid=(t // tr,),
        in_specs=[pl.BlockSpec((ne, tr), lambda i: (0, i)), pl.BlockSpec((ne, 1), lambda i: (0, 0))],
        out_specs=[tok_spec, tok_spec, tok_spec, pl.BlockSpec((ne, LANES), lambda i: (0, 0))],
        out_shape=[tok(I32), tok(F32), tok(I32), jax.ShapeDtypeStruct((ne, LANES), F32)],
        scratch_shapes=[pltpu.VMEM((ne, LANES), F32)],
        compiler_params=pltpu.CompilerParams(dimension_semantics=("arbitrary",)),
        name="route",
    )(logits_t, bias_col)


def _row_copy(src_ref, src_row, dst_ref, dst_row, sem):
    return pltpu.make_async_copy(src_ref.at[pl.ds(src_row, 1), :], dst_ref.at[pl.ds(dst_row, 1), :], sem)


def _dispatch_body(ps_ref, e_ref, r_ref, h_ref, xs_in_ref, xs_ref, sem, *, td, topk):
    del xs_in_ref

    def issue(i, carry):
        for k in range(topk):
            dest = ps_ref[e_ref[k, i]] + r_ref[k, i]
            _row_copy(h_ref, i, xs_ref, dest, sem).start()
        return carry

    lax.fori_loop(0, td, issue, 0)

    def drain(i, carry):
        for k in range(topk):
            _row_copy(h_ref, 0, xs_ref, 0, sem).wait()
        return carry

    lax.fori_loop(0, td, drain, 0)


def _dispatch(pstart, eidx, rank, hp, xs):
    t, w = hp.shape
    td = TOKEN_TILE
    smem_spec = pl.BlockSpec((SUBLANES, td), lambda i, ps: (0, i), memory_space=pltpu.SMEM)
    return pl.pallas_call(
        functools.partial(_dispatch_body, td=td, topk=TOP_K),
        grid_spec=pltpu.PrefetchScalarGridSpec(
            num_scalar_prefetch=1,
            grid=(t // td,),
            in_specs=[smem_spec, smem_spec,
                      pl.BlockSpec((td, w), lambda i, ps: (i, 0)),
                      pl.BlockSpec(memory_space=pl.ANY)],
            out_specs=pl.BlockSpec(memory_space=pl.ANY),
            scratch_shapes=[pltpu.SemaphoreType.DMA(())]),
        out_shape=jax.ShapeDtypeStruct(xs.shape, xs.dtype),
        input_output_aliases={4: 0},
        compiler_params=pltpu.CompilerParams(dimension_semantics=("arbitrary",), has_side_effects=True),
        name="dispatch",
    )(pstart, eidx, rank, hp, xs)


def _expert_body(be_ref, nu_ref, x_ref, w1_ref, w3_ref, w2_ref, y_ref):
    del be_ref

    @pl.when(pl.program_id(0) < nu_ref[0])
    def _():
        half = x_ref.shape[1]
        lo, hi = _unpack_halves(x_ref[...])
        lo = lo.astype(BF16)
        hi = hi.astype(BF16)

        def up(w_ref):
            return (jnp.dot(lo, w_ref[0, 0:half, :], preferred_element_type=F32)
                    + jnp.dot(hi, w_ref[0, half:2 * half, :], preferred_element_type=F32))

        hid = (_silu(up(w1_ref)) * up(w3_ref)).astype(BF16)
        y = jnp.dot(hid, w2_ref[0], preferred_element_type=F32)
        y_ref[...] = _pack_halves(y[:, 0:half], y[:, half:2 * half])

    @pl.when(pl.program_id(0) >= nu_ref[0])
    def _():
        y_ref[...] = jnp.zeros(y_ref.shape, U32)


def _experts(blk_e, n_used, xs, w1b, w3b, w2b):
    rows, half = xs.shape
    tm = EXPERT_ROWS
    d, de = w1b.shape[1], w1b.shape[2]
    blk = lambda i, be, nu: (jnp.minimum(i, nu[0] - 1), 0)
    return pl.pallas_call(
        _expert_body,
        grid_spec=pltpu.PrefetchScalarGridSpec(
            num_scalar_prefetch=2,
            grid=(rows // tm,),
            in_specs=[pl.BlockSpec((tm, half), blk),
                      pl.BlockSpec((1, d, de), lambda i, be, nu: (be[i], 0, 0)),
                      pl.BlockSpec((1, d, de), lambda i, be, nu: (be[i], 0, 0)),
                      pl.BlockSpec((1, de, d), lambda i, be, nu: (be[i], 0, 0))],
            out_specs=pl.BlockSpec((tm, half), lambda i, be, nu: (i, 0))),
        out_shape=jax.ShapeDtypeStruct((rows, half), U32),
        compiler_params=pltpu.CompilerParams(dimension_semantics=("arbitrary",)),
        name="experts",
    )(blk_e, n_used, xs, w1b, w3b, w2b)


def _combine_body(ps_ref, e_ref, r_ref, x2_ref, gtf_ref, wt_ref, gfin_ref, ys_ref, y_ref, buf, sem, *, td, topk):
    def issue(i, carry):
        for k in range(topk):
            src = ps_ref[e_ref[k, i]] + r_ref[k, i]
            pltpu.make_async_copy(ys_ref.at[pl.ds(src, 1), :], buf.at[k, pl.ds(i, 1), :], sem).start()
        return carry

    lax.fori_loop(0, td, issue, 0)

    def drain(i, carry):
        for k in range(topk):
            pltpu.make_async_copy(ys_ref.at[pl.ds(0, 1), :], buf.at[0, pl.ds(0, 1), :], sem).wait()
        return carry

    lax.fori_loop(0, td, drain, 0)

    half = buf.shape[2]
    acc_lo = jnp.zeros((td, half), F32)
    acc_hi = jnp.zeros((td, half), F32)
    for k in range(topk):
        lo, hi = _unpack_halves(buf[k])
        w = wt_ref[:, k:k + 1]
        acc_lo = acc_lo + w * lo
        acc_hi = acc_hi + w * hi
    x_lo = x2_ref[:, 0:half] + gtf_ref[:, 0:half] * acc_lo
    x_hi = x2_ref[:, half:2 * half] + gtf_ref[:, half:2 * half] * acc_hi
    ms = (jnp.sum(x_lo * x_lo, axis=-1, keepdims=True)
          + jnp.sum(x_hi * x_hi, axis=-1, keepdims=True)) / (2 * half)
    rs = lax.rsqrt(ms + EPS)
    y_ref[:, 0:half] = x_lo * rs * gfin_ref[:, 0:half]
    y_ref[:, half:2 * half] = x_hi * rs * gfin_ref[:, half:2 * half]


def _combine(pstart, eidx, rank, wt_rows, x2, mod, g_final, ys, *, rows_per_mod, per_row_mod):
    t, d = x2.shape
    td = TOKEN_TILE
    smem_spec = pl.BlockSpec((SUBLANES, td), lambda i, ps: (0, i), memory_space=pltpu.SMEM)
    if per_row_mod:
        gtf_spec = pl.BlockSpec((td, d), lambda i, ps: (i, 5))
    else:
        tiles_per_mod = rows_per_mod // td
        gtf_spec = pl.BlockSpec((None, None, 1, d), lambda i, ps: (i // tiles_per_mod, 5, 0, 0))
    return pl.pallas_call(
        functools.partial(_combine_body, td=td, topk=TOP_K),
        grid_spec=pltpu.PrefetchScalarGridSpec(
            num_scalar_prefetch=1,
            grid=(t // td,),
            in_specs=[smem_spec, smem_spec,
                      pl.BlockSpec((td, d), lambda i, ps: (i, 0)),
                      gtf_spec,
                      pl.BlockSpec((td, SUBLANES), lambda i, ps: (i, 0)),
                      pl.BlockSpec((1, d), lambda i, ps: (0, 0)),
                      pl.BlockSpec(memory_space=pl.ANY)],
            out_specs=pl.BlockSpec((td, d), lambda i, ps: (i, 0)),
            scratch_shapes=[pltpu.VMEM((TOP_K, td, d // 2), U32), pltpu.SemaphoreType.DMA(())]),
        out_shape=jax.ShapeDtypeStruct((t, d), F32),
        compiler_params=pltpu.CompilerParams(dimension_semantics=("arbitrary",)),
        name="combine",
    )(pstart, eidx, rank, x2, mod, wt_rows, g_final, ys)


def _retention_tables(length, nh, hd):
    c = math.gcd(length, RET_CHUNK)
    log_g = jnp.log(1.0 - 2.0 ** (-5.0 - jnp.arange(nh, dtype=F32)))
    idx = jnp.arange(c, dtype=F32)
    rel = idx[:, None] - idx[None, :]
    mask = jnp.where(rel >= 0, jnp.exp(log_g[:, None, None] * jnp.maximum(rel, 0.0)), 0.0)
    q_decay = jnp.exp(log_g[None, :] * (idx[:, None] + 1.0))
    k_decay = jnp.exp(log_g[None, :] * (c - 1.0 - idx[:, None]))
    chunk_decay = jnp.exp(log_g * c)
    qd = jnp.broadcast_to(q_decay.T[:, :, None], (nh, c, hd))
    kd = jnp.broadcast_to(k_decay.T[:, :, None], (nh, c, hd))
    cd = jnp.broadcast_to(chunk_decay[:, None, None], (nh, hd, hd))
    return mask, qd, kd, cd


def kernel(x_prompt, x_sample, c_prompt, c_sample, state_conv, state_ret, w_ada, b_ada, g_mix, g_ffn, w_in,
           conv_w, conv_b, conv_norm_g, conv_norm_b, ret_norm_g, ret_norm_b, w_out, w_router, router_bias,
           w1, w3, w2, ws1, ws3, ws2, g_final):
    depth = w_ada.shape[0]
    assert depth == 1, "single-layer trunk"
    bp, lp, d = x_prompt.shape
    bs, ls, _ = x_sample.shape
    assert ls == 1
    dc = conv_w.shape[2]
    dr = ret_norm_g.shape[1]
    nh = RET_HEADS
    hd = dr // nh
    assert hd == LANES and lp % 256 == 0 and bs % 16 == 0
    ne = w_router.shape[2]
    row = lambda a: a.reshape(1, -1)

    mod = _ada(jnp.concatenate([c_prompt, c_sample], axis=0), w_ada[0], row(b_ada[0]))
    mod_p = mod[:bp].reshape(bp, 6, 1, d)
    mod_s = mod[bp:]

    half = hd // 2
    inv = ROPE_BASE ** (-jnp.arange(half, dtype=F32) / half)
    inv2 = jnp.concatenate([inv, inv]).reshape(1, hd)
    cos_p, sin_p = _rope_tables(inv2, lp, 0)
    cos_s, sin_s = _rope_tables(inv2, SUBLANES, PAST_LEN)

    w_in_b = w_in[0].astype(BF16)
    wo_b = w_out[0].astype(BF16)
    wr_t = w_router[0].T
    wrh = wr_t.astype(BF16)
    wrl = (wr_t - wrh.astype(F32)).astype(BF16)
    ws1_b, ws3_b, ws2_b = ws1[0].astype(BF16), ws3[0].astype(BF16), ws2[0].astype(BF16)
    w1b, w3b, w2b = w1[0].astype(BF16), w3[0].astype(BF16), w2[0].astype(BF16)
    dims = dict(dc=dc, dr=dr, hd=hd)

    glu_p, q_p, k_p, v_p, sg_p = _proj(x_prompt, mod_p, row(g_mix[0]), w_in_b, cos_p, sin_p,
                                       per_row_mod=False, **dims)
    cat_p, ret_p = _mix(glu_p, q_p, k_p, v_p, sg_p, conv_w[0], row(conv_b[0]), row(conv_norm_g[0]),
                        row(conv_norm_b[0]), row(ret_norm_g[0]), row(ret_norm_b[0]),
                        _retention_tables(lp, nh, hd), nb=bp, length=lp, **dims)
    x2_p, hp_p, lg_p = _post(cat_p, x_prompt, mod_p, row(g_ffn[0]), wo_b, wrh, wrl, ws1_b, ws3_b, ws2_b,
                             per_row_mod=False)

    xs3 = x_sample.reshape(1, bs, d)
    glu_s, q_s, k_s, v_s, sg_s = _proj(xs3, mod_s, row(g_mix[0]), w_in_b, cos_s, sin_s,
                                       per_row_mod=True, **dims)
    log_g = jnp.log(1.0 - 2.0 ** (-5.0 - jnp.arange(nh, dtype=F32)))
    gam = jnp.broadcast_to(jnp.exp(log_g)[:, None, None], (nh, SUBLANES, hd))
    cat_s, ret_s = _mix1(glu_s, q_s, k_s, v_s, sg_s, state_conv[0], state_ret[0], conv_w[0], row(conv_b[0]),
                         row(conv_norm_g[0]), row(conv_norm_b[0]), row(ret_norm_g[0]), row(ret_norm_b[0]),
                         gam, **dims)
    x2_s, hp_s, lg_s = _post(cat_s, xs3, mod_s, row(g_ffn[0]), wo_b, wrh, wrl, ws1_b, ws3_b, ws2_b,
                             per_row_mod=True)

    tp = bp * lp
    eidx, wts, rank, cnt = _route(jnp.concatenate([lg_p, lg_s], axis=1), router_bias[0].reshape(ne, 1))
    counts = cnt[:, 0].astype(I32)
    tm = EXPERT_ROWS
    padded = (counts + tm - 1) // tm * tm
    pend = jnp.cumsum(padded)
    pstart = (pend - padded).astype(I32)
    n_assign = (tp + bs) * TOP_K
    n_blocks = -(-(n_assign + ne * (tm - 1)) // tm)
    n_used = (pend[-1] // tm).astype(I32).reshape(1)
    blk_e = jnp.minimum(jnp.searchsorted(pend, jnp.arange(n_blocks, dtype=I32) * tm, side="right"),
                        ne - 1).astype(I32)
    wt_rows = wts.T

    xs = jnp.zeros((n_blocks * tm, d // 2), U32)
    xs = _dispatch(pstart, eidx[:, :tp], rank[:, :tp], hp_p, xs)
    xs = _dispatch(pstart, eidx[:, tp:], rank[:, tp:], hp_s, xs)
    ys = _experts(blk_e, n_used, xs, w1b, w3b, w2b)
    y_p = _combine(pstart, eidx[:, :tp], rank[:, :tp], wt_rows[:tp], x2_p, mod_p, row(g_final), ys,
                   rows_per_mod=lp, per_row_mod=False)
    y_s = _combine(pstart, eidx[:, tp:], rank[:, tp:], wt_rows[tp:], x2_s, mod_s, row(g_final), ys,
                   rows_per_mod=bs, per_row_mod=True)

    new_conv_p = glu_p.reshape(bp, lp, dc)[:, lp - CONV_BUF:, :]
    new_conv_s = jnp.concatenate([state_conv[0][:, 1:, :], glu_s[:, None, :]], axis=1)
    return (y_p.reshape(bp, lp, d), y_s.reshape(bs, ls, d), new_conv_p[None], ret_p[None],
            new_conv_s[None], ret_s[None])
```

```python
import functools
import math

import jax
import jax.numpy as jnp
from jax import lax
from jax.experimental import pallas as pl
from jax.experimental.pallas import tpu as pltpu

F32 = jnp.float32
BF16 = jnp.bfloat16
U32 = jnp.uint32
I32 = jnp.int32

EPS = 1e-6
PAST_LEN = 16384
RET_HEADS = 4
RET_CHUNK = 128
CONV_WIDTH = 31
CONV_BUF = CONV_WIDTH - 1
ROPE_BASE = 10000.0
N_EXPERTS = 64
TOP_K = 6
N_GROUPS = 8
TOPK_GROUPS = 4
ROUTED_SCALE = 2.5

LANES = 128
SUBLANES = 8
CONV_PAD = 32
EXPERT_ROWS = 256
ROUTE_TILE = 384
TOKEN_TILE = 128
VMEM_LIMIT = 56 * 1024 * 1024

HI_MASK = 0xFFFF0000


def _sigmoid(x):
    return jax.nn.sigmoid(x)


def _silu(x):
    return x * jax.nn.sigmoid(x)


def _pack_halves(lo, hi):
    lo_u = lax.bitcast_convert_type(lo.astype(BF16).astype(F32), U32) >> 16
    hi_u = lax.bitcast_convert_type(hi.astype(BF16).astype(F32), U32) & jnp.uint32(HI_MASK)
    return hi_u | lo_u


def _unpack_halves(p):
    lo = lax.bitcast_convert_type(p << 16, F32)
    hi = lax.bitcast_convert_type(p & jnp.uint32(HI_MASK), F32)
    return lo, hi


def _ada_body(c_ref, w_ref, b_ref, o_ref):
    s = _silu(c_ref[...]).astype(BF16)
    o_ref[...] = jnp.dot(s, w_ref[...].astype(BF16), preferred_element_type=F32) + b_ref[...]


def _ada(c_all, w_ada, b_ada):
    rows, d = c_all.shape
    n = w_ada.shape[1]
    tn = 512
    return pl.pallas_call(
        _ada_body,
        grid=(n // tn,),
        in_specs=[
            pl.BlockSpec((rows, d), lambda j: (0, 0)),
            pl.BlockSpec((d, tn), lambda j: (0, j)),
            pl.BlockSpec((1, tn), lambda j: (0, j)),
        ],
        out_specs=pl.BlockSpec((rows, tn), lambda j: (0, j)),
        out_shape=jax.ShapeDtypeStruct((rows, n), F32),
        name="ada",
    )(c_all, w_ada, b_ada)


def _rope_body(inv_ref, cos_ref, sin_ref, *, pos0, tl, half):
    row = lax.broadcasted_iota(I32, (tl, LANES), 0) + pl.program_id(0) * tl
    ang = (row.astype(F32) + pos0) * inv_ref[...]
    lane = lax.broadcasted_iota(I32, (tl, LANES), 1)
    s = jnp.sin(ang)
    cos_ref[...] = jnp.cos(ang)
    sin_ref[...] = jnp.where(lane < half, -s, s)


def _rope_tables(inv2, rows, pos0):
    tl = min(rows, 256)
    return pl.pallas_call(
        functools.partial(_rope_body, pos0=float(pos0), tl=tl, half=LANES // 2),
        grid=(rows // tl,),
        in_specs=[pl.BlockSpec((1, LANES), lambda i: (0, 0))],
        out_specs=[pl.BlockSpec((tl, LANES), lambda i: (i, 0))] * 2,
        out_shape=[jax.ShapeDtypeStruct((rows, LANES), F32)] * 2,
        name="rope",
    )(inv2)


def _modulated_rmsnorm(x, g, sc, sh):
    ms = jnp.mean(x * x, axis=-1, keepdims=True)
    h = x * lax.rsqrt(ms + EPS) * g
    return h * (1.0 + sc) + sh


def _proj_body(x_ref, sh_ref, sc_ref, g_ref, w_ref, cos_ref, sin_ref,
               glu_ref, q_ref, k_ref, v_ref, sg_ref, *, dc, dr, hd, rope_rows):
    hb = _modulated_rmsnorm(x_ref[...], g_ref[...], sc_ref[...], sh_ref[...]).astype(BF16)

    def proj(lo, n):
        return jnp.dot(hb, w_ref[:, lo:lo + n], preferred_element_type=F32)

    glu_ref[...] = proj(0, dc) * _sigmoid(proj(dc, dc))
    cos = cos_ref[...] if rope_rows else cos_ref[0:1, :]
    sin = sin_ref[...] if rope_rows else sin_ref[0:1, :]
    for ref, lo, scale in ((q_ref, 2 * dc, hd ** -0.5), (k_ref, 2 * dc + dr, None)):
        t = proj(lo, dr)
        for hh in range(dr // hd):
            th = t[:, hh * hd:(hh + 1) * hd]
            r = th * cos + pltpu.roll(th, hd // 2, 1) * sin
            if scale is not None:
                r = r * scale
            ref[:, hh * hd:(hh + 1) * hd] = r.astype(BF16)
    v_ref[...] = proj(2 * dc + 2 * dr, dr).astype(BF16)
    sg_ref[...] = _silu(proj(2 * dc + 3 * dr, dr)).astype(BF16)


def _proj(x, mod, g_mix, w_in_b, cos2, sin2, *, dc, dr, hd, per_row_mod):
    nb, length, d = x.shape
    tl = min(length, 256)
    x2 = x.reshape(nb * length, d)
    nl = length // tl
    if per_row_mod:
        mod_spec = lambda j: pl.BlockSpec((tl, d), lambda b, l: (l, j))
        rope_spec = pl.BlockSpec((SUBLANES, LANES), lambda b, l: (0, 0))
    else:
        mod_spec = lambda j: pl.BlockSpec((None, None, 1, d), lambda b, l: (b, j, 0, 0))
        rope_spec = pl.BlockSpec((tl, LANES), lambda b, l: (l, 0))
    row_spec = lambda w: pl.BlockSpec((tl, w), lambda b, l: (b * nl + l, 0))
    t = nb * length
    outs = pl.pallas_call(
        functools.partial(_proj_body, dc=dc, dr=dr, hd=hd, rope_rows=not per_row_mod),
        grid=(nb, nl),
        in_specs=[
            row_spec(d), mod_spec(0), mod_spec(1),
            pl.BlockSpec((1, d), lambda b, l: (0, 0)),
            pl.BlockSpec(w_in_b.shape, lambda b, l: (0, 0)),
            rope_spec, rope_spec,
        ],
        out_specs=[row_spec(dc), row_spec(dr), row_spec(dr), row_spec(dr), row_spec(dr)],
        out_shape=[
            jax.ShapeDtypeStruct((t, dc), F32),
            jax.ShapeDtypeStruct((t, dr), BF16),
            jax.ShapeDtypeStruct((t, dr), BF16),
            jax.ShapeDtypeStruct((t, dr), BF16),
            jax.ShapeDtypeStruct((t, dr), BF16),
        ],
        compiler_params=pltpu.CompilerParams(vmem_limit_bytes=VMEM_LIMIT),
        name="proj",
    )(x2, mod, mod, g_mix, w_in_b, cos2, sin2)
    return outs


def _layernorm_silu(c, g, b):
    mu = jnp.mean(c, axis=-1, keepdims=True)
    d = c - mu
    var = jnp.mean(d * d, axis=-1, keepdims=True)
    return _silu(d * lax.rsqrt(var + EPS) * g + b)


def _groupnorm(o, g, b):
    mu = jnp.mean(o, axis=-1, keepdims=True)
    d = o - mu
    var = jnp.mean(d * d, axis=-1, keepdims=True)
    return d * lax.rsqrt(var + EPS) * g + b


def _mix_body(glu_ref, q_ref, k_ref, v_ref, sg_ref, cw_ref, cb_ref, lng_ref, lnb_ref, rg_ref, rb_ref,
              mask_ref, qd_ref, kd_ref, cd_ref, cat_ref, st_ref, buf, cscr, *, tl, dc, hd, nh, chunk):
    nslab = dc // LANES

    @pl.when(pl.program_id(1) == 0)
    def _():
        buf[:, 0:CONV_PAD, :] = jnp.zeros((nslab, CONV_PAD, LANES), F32)
        st_ref[...] = jnp.zeros(st_ref.shape, F32)

    for j in range(nslab):
        buf[j, CONV_PAD:CONV_PAD + tl, :] = glu_ref[:, LANES * j:LANES * (j + 1)]
    first = CONV_PAD - CONV_BUF
    rows_per_iter = 4 * SUBLANES
    for j in range(nslab):
        cols = slice(LANES * j, LANES * (j + 1))
        wv = [jnp.broadcast_to(cw_ref[t:t + 1, cols], (SUBLANES, LANES)) for t in range(CONV_WIDTH)]
        bias = jnp.broadcast_to(cb_ref[0:1, cols], (SUBLANES, LANES))

        def body(r, carry, j=j, cols=cols, wv=wv, bias=bias):
            base = pl.multiple_of(r * rows_per_iter, rows_per_iter)
            for u in range(rows_per_iter // SUBLANES):
                acc = bias
                for t in range(CONV_WIDTH):
                    acc = acc + wv[t] * buf[j, pl.ds(base + (u * SUBLANES + first + t), SUBLANES), :]
                cscr[pl.ds(base + u * SUBLANES, SUBLANES), cols] = acc
            return carry

        lax.fori_loop(0, tl // rows_per_iter, body, 0)
    for j in range(nslab):
        buf[j, 0:CONV_PAD, :] = buf[j, tl:tl + CONV_PAD, :]
    cat_ref[:, 0:dc] = _layernorm_silu(cscr[...], lng_ref[...], lnb_ref[...]).astype(BF16)

    nt = (((1,), (1,)), ((), ()))
    tn = (((0,), (0,)), ((), ()))
    for c in range(tl // chunk):
        rows = slice(c * chunk, (c + 1) * chunk)
        for hh in range(nh):
            cols = slice(hh * hd, (hh + 1) * hd)
            qh = q_ref[rows, cols]
            kh = k_ref[rows, cols]
            vh = v_ref[rows, cols]
            s = st_ref[0, hh]
            scores = lax.dot_general(qh, kh, nt, preferred_element_type=F32) * mask_ref[hh]
            inner = jnp.dot(scores.astype(BF16), vh, preferred_element_type=F32)
            qd = (qh.astype(F32) * qd_ref[hh]).astype(BF16)
            cross = jnp.dot(qd, s.astype(BF16), preferred_element_type=F32)
            kd = (kh.astype(F32) * kd_ref[hh]).astype(BF16)
            st_ref[0, hh] = cd_ref[hh] * s + lax.dot_general(kd, vh, tn, preferred_element_type=F32)
            o = _groupnorm(inner + cross, rg_ref[0:1, cols], rb_ref[0:1, cols])
            cat_ref[rows, dc + hh * hd:dc + (hh + 1) * hd] = (o * sg_ref[rows, cols].astype(F32)).astype(BF16)


def _mix(glu, q, k, v, sg, conv_w, conv_b, ln_g, ln_b, rg, rb, tables, *, nb, length, dc, dr, hd):
    nh = dr // hd
    chunk = math.gcd(length, RET_CHUNK)
    tl = min(length, 256)
    nl = length // tl
    mask, qd, kd, cd = tables
    row_spec = lambda w: pl.BlockSpec((tl, w), lambda b, l: (b * nl + l, 0))
    full = lambda a: pl.BlockSpec(a.shape, lambda b, l: (0,) * a.ndim)
    cat, st = pl.pallas_call(
        functools.partial(_mix_body, tl=tl, dc=dc, hd=hd, nh=nh, chunk=chunk),
        grid=(nb, nl),
        in_specs=[row_spec(dc), row_spec(dr), row_spec(dr), row_spec(dr), row_spec(dr),
                  full(conv_w), full(conv_b), full(ln_g), full(ln_b), full(rg), full(rb),
                  full(mask), full(qd), full(kd), full(cd)],
        out_specs=[row_spec(dc + dr), pl.BlockSpec((1, nh, hd, hd), lambda b, l: (b, 0, 0, 0))],
        out_shape=[jax.ShapeDtypeStruct((nb * length, dc + dr), BF16),
                   jax.ShapeDtypeStruct((nb, nh, hd, hd), F32)],
        scratch_shapes=[pltpu.VMEM((dc // LANES, tl + CONV_PAD, LANES), F32),
                        pltpu.VMEM((tl, dc), F32)],
        compiler_params=pltpu.CompilerParams(dimension_semantics=("arbitrary", "arbitrary")),
        name="mix",
    )(glu, q, k, v, sg, conv_w, conv_b, ln_g, ln_b, rg, rb, mask, qd, kd, cd)
    return cat, st


def _mix1_body(glu_ref, q_ref, k_ref, v_ref, sg_ref, sc_ref, s0_ref, cw_ref, cb_ref, lng_ref, lnb_ref,
               rg_ref, rb_ref, gam_ref, cat_ref, st_ref, cscr, oscr, qf, kf, vf, *, tb, dc, hd, nh):
    w_hist = cw_ref[0:CONV_BUF, :]
    w_last = cw_ref[CONV_BUF:CONV_WIDTH, :]
    for bb in range(tb):
        hist = jnp.sum(sc_ref[bb] * w_hist, axis=0, keepdims=True)
        cscr[bb:bb + 1, :] = hist + glu_ref[bb:bb + 1, :] * w_last + cb_ref[...]
    cat_ref[:, 0:dc] = _layernorm_silu(cscr[...], lng_ref[...], lnb_ref[...]).astype(BF16)

    tn = (((0,), (0,)), ((), ()))
    rowid = lax.broadcasted_iota(I32, (tb, hd), 0)
    qf[...] = q_ref[...].astype(F32)
    kf[...] = k_ref[...].astype(F32)
    vf[...] = v_ref[...].astype(F32)
    for hh in range(nh):
        cols = slice(hh * hd, (hh + 1) * hd)
        qa = q_ref[:, cols]
        ka = k_ref[:, cols]
        gam = gam_ref[hh, 0:1, :]
        for bb in range(tb):
            onehot = (rowid == bb).astype(BF16)
            qcol = lax.dot_general(qa, onehot, tn, preferred_element_type=F32)
            kcol = lax.dot_general(ka, onehot, tn, preferred_element_type=F32)
            s0 = s0_ref[bb, hh]
            qrow = qf[bb:bb + 1, cols]
            krow = kf[bb:bb + 1, cols]
            vrow = vf[bb:bb + 1, cols]
            qk = jnp.sum(qrow * krow, axis=-1, keepdims=True)
            cross = gam * jnp.sum(qcol * s0, axis=0, keepdims=True)
            st_ref[bb, hh] = gam * s0 + kcol * vrow
            oscr[bb:bb + 1, cols] = qk * vrow + cross
    for hh in range(nh):
        cols = slice(hh * hd, (hh + 1) * hd)
        o = _groupnorm(oscr[:, cols], rg_ref[0:1, cols], rb_ref[0:1, cols])
        cat_ref[:, dc + hh * hd:dc + (hh + 1) * hd] = (o * sg_ref[:, cols].astype(F32)).astype(BF16)


def _mix1(glu, q, k, v, sg, state_conv, state_ret, conv_w, conv_b, ln_g, ln_b, rg, rb, gam, *, dc, dr, hd):
    nb = glu.shape[0]
    nh = dr // hd
    tb = 16
    row_spec = lambda w: pl.BlockSpec((tb, w), lambda i: (i, 0))
    full = lambda a: pl.BlockSpec(a.shape, lambda i: (0,) * a.ndim)
    st_spec = pl.BlockSpec((tb, nh, hd, hd), lambda i: (i, 0, 0, 0))
    cat, st = pl.pallas_call(
        functools.partial(_mix1_body, tb=tb, dc=dc, hd=hd, nh=nh),
        grid=(nb // tb,),
        in_specs=[row_spec(dc), row_spec(dr), row_spec(dr), row_spec(dr), row_spec(dr),
                  pl.BlockSpec((tb, CONV_BUF, dc), lambda i: (i, 0, 0)), st_spec,
                  full(conv_w), full(conv_b), full(ln_g), full(ln_b), full(rg), full(rb), full(gam)],
        out_specs=[row_spec(dc + dr), st_spec],
        out_shape=[jax.ShapeDtypeStruct((nb, dc + dr), BF16),
                   jax.ShapeDtypeStruct((nb, nh, hd, hd), F32)],
        scratch_shapes=[pltpu.VMEM((tb, dc), F32)] + [pltpu.VMEM((tb, dr), F32)] * 4,
        compiler_params=pltpu.CompilerParams(vmem_limit_bytes=VMEM_LIMIT),
        name="mix1",
    )(glu, q, k, v, sg, state_conv, state_ret, conv_w, conv_b, ln_g, ln_b, rg, rb, gam)
    return cat, st


def _post_body(cat_ref, x_ref, gtm_ref, scf_ref, shf_ref, gtf_ref, g_ref, wo_ref, wrh_ref, wrl_ref,
               ws1_ref, ws3_ref, ws2_ref, x2_ref, hp_ref, lg_ref):
    d = x_ref.shape[1]
    y = jnp.dot(cat_ref[...], wo_ref[...], preferred_element_type=F32)
    x1 = x_ref[...] + gtm_ref[...] * y
    h = _modulated_rmsnorm(x1, g_ref[...], scf_ref[...], shf_ref[...])
    hb = h.astype(BF16)
    hp_ref[...] = _pack_halves(h[:, 0:d // 2], h[:, d // 2:d])
    hl = (h - hb.astype(F32)).astype(BF16)
    nt = (((1,), (1,)), ((), ()))
    lg_ref[...] = (lax.dot_general(wrh_ref[...], hb, nt, preferred_element_type=F32)
                   + lax.dot_general(wrh_ref[...], hl, nt, preferred_element_type=F32)
                   + lax.dot_general(wrl_ref[...], hb, nt, preferred_element_type=F32))
    s1 = jnp.dot(hb, ws1_ref[...], preferred_element_type=F32)
    s3 = jnp.dot(hb, ws3_ref[...], preferred_element_type=F32)
    shared = jnp.dot((_silu(s1) * s3).astype(BF16), ws2_ref[...], preferred_element_type=F32)
    x2_ref[...] = x1 + gtf_ref[...] * shared


def _post(cat, x, mod, g_ffn, wo_b, wrh, wrl, ws1_b, ws3_b, ws2_b, *, per_row_mod):
    nb, length, d = x.shape
    tl = min(length, 512)
    nl = length // tl
    t = nb * length
    ne = wrh.shape[0]
    x2d = x.reshape(t, d)
    if per_row_mod:
        mod_spec = lambda j: pl.BlockSpec((tl, d), lambda b, l: (l, j))
    else:
        mod_spec = lambda j: pl.BlockSpec((None, None, 1, d), lambda b, l: (b, j, 0, 0))
    row_spec = lambda w: pl.BlockSpec((tl, w), lambda b, l: (b * nl + l, 0))
    full = lambda a: pl.BlockSpec(a.shape, lambda b, l: (0,) * a.ndim)
    return pl.pallas_call(
        _post_body,
        grid=(nb, nl),
        in_specs=[row_spec(d), row_spec(d), mod_spec(2), mod_spec(4), mod_spec(3), mod_spec(5),
                  full(g_ffn), full(wo_b), full(wrh), full(wrl), full(ws1_b), full(ws3_b), full(ws2_b)],
        out_specs=[row_spec(d), row_spec(d // 2), pl.BlockSpec((ne, tl), lambda b, l: (0, b * nl + l))],
        out_shape=[jax.ShapeDtypeStruct((t, d), F32),
                   jax.ShapeDtypeStruct((t, d // 2), U32),
                   jax.ShapeDtypeStruct((ne, t), F32)],
        compiler_params=pltpu.CompilerParams(vmem_limit_bytes=VMEM_LIMIT),
        name="post",
    )(cat, x2d, mod, mod, mod, mod, g_ffn, wo_b, wrh, wrl, ws1_b, ws3_b, ws2_b)


def _first_max(x, idx, sentinel):
    m = jnp.max(x, axis=0, keepdims=True)
    f = jnp.min(jnp.where(x == m, idx, sentinel), axis=0, keepdims=True)
    return m, f


def _route_body(lg_ref, bias_ref, e_ref, w_ref, r_ref, cnt_ref, cnt_scr, *, tr, ne, ng, topk, topg):
    @pl.when(pl.program_id(0) == 0)
    def _():
        cnt_scr[...] = jnp.zeros(cnt_scr.shape, F32)

    per = ne // ng
    neg = -jnp.inf
    scores = _sigmoid(lg_ref[...])
    sel = scores + bias_ref[...]
    sub = lax.broadcasted_iota(I32, (per, tr), 0)
    gs = []
    for g in range(ng):
        s_g = sel[g * per:(g + 1) * per, :]
        m1, f1 = _first_max(s_g, sub, per)
        m2 = jnp.max(jnp.where(sub == f1, neg, s_g), axis=0, keepdims=True)
        gs.append(m1 + m2)
    gsc = jnp.concatenate(gs, axis=0)
    gi = lax.broadcasted_iota(I32, (ng, tr), 0)
    keep = jnp.zeros((ng, tr), F32)
    for _ in range(topg):
        _, f = _first_max(gsc, gi, ng)
        pick = gi == f
        keep = jnp.where(pick, 1.0, keep)
        gsc = jnp.where(pick, neg, gsc)
    work = jnp.concatenate(
        [jnp.where(keep[g:g + 1, :] > 0.5, sel[g * per:(g + 1) * per, :], neg) for g in range(ng)], axis=0)
    ei = lax.broadcasted_iota(I32, (ne, tr), 0)
    picks, es, ws = [], [], []
    for _ in range(topk):
        _, f = _first_max(work, ei, ne)
        pick = ei == f
        picks.append(pick)
        es.append(f)
        ws.append(jnp.sum(jnp.where(pick, scores, 0.0), axis=0, keepdims=True))
        work = jnp.where(pick, neg, work)
    wsum = ws[0]
    for w in ws[1:]:
        wsum = wsum + w
    scale = ROUTED_SCALE / wsum
    chosen = picks[0]
    for p in picks[1:]:
        chosen = jnp.logical_or(chosen, p)
    chosen_f = chosen.astype(F32)
    t_row = lax.broadcasted_iota(I32, (tr, tr), 0)
    t_col = lax.broadcasted_iota(I32, (tr, tr), 1)
    before = (t_row < t_col).astype(BF16)
    prior = cnt_scr[:, 0:1] + jnp.dot(chosen_f.astype(BF16), before, preferred_element_type=F32)
    rs = [jnp.sum(jnp.where(p, prior, 0.0), axis=0, keepdims=True).astype(I32) for p in picks]
    pad_i = jnp.zeros((SUBLANES - topk, tr), I32)
    pad_f = jnp.zeros((SUBLANES - topk, tr), F32)
    e_ref[...] = jnp.concatenate(es + [pad_i], axis=0)
    w_ref[...] = jnp.concatenate([w * scale for w in ws] + [pad_f], axis=0)
    r_ref[...] = jnp.concatenate(rs + [pad_i], axis=0)
    total = cnt_scr[:, 0:1] + jnp.sum(chosen_f, axis=1, keepdims=True)
    cnt_scr[...] = jnp.broadcast_to(total, cnt_scr.shape)
    cnt_ref[...] = jnp.broadcast_to(total, cnt_ref.shape)


def _route(logits_t, bias_col):
    ne, t = logits_t.shape
    tr = ROUTE_TILE if t % ROUTE_TILE == 0 else LANES
    tok = lambda dt: jax.ShapeDtypeStruct((SUBLANES, t), dt)
    tok_spec = pl.BlockSpec((SUBLANES, tr), lambda i: (0, i))
    return pl.pallas_call(
        functools.partial(_route_body, tr=tr, ne=ne, ng=N_GROUPS, topk=TOP_K, topg=TOPK_GROUPS),
        grid=(t // tr,),
        in_specs=[pl.BlockSpec((ne, tr), lambda i: (0, i)), pl.BlockSpec((ne, 1), lambda i: (0, 0))],
        out_specs=[tok_spec, tok_spec, tok_spec, pl.BlockSpec((ne, LANES), lambda i: (0, 0))],
        out_shape=[tok(I32), tok(F32), tok(I32), jax.ShapeDtypeStruct((ne, LANES), F32)],
        scratch_shapes=[pltpu.VMEM((ne, LANES), F32)],
        compiler_params=pltpu.CompilerParams(dimension_semantics=("arbitrary",)),
        name="route",
    )(logits_t, bias_col)


def _dest_body(ps_ref, e_ref, r_ref, d_ref, *, ne):
    e = e_ref[...]
    base = jnp.zeros(e.shape, I32)
    for j in range(ne):
        base = jnp.where(e == j, ps_ref[j], base)
    d_ref[...] = base + r_ref[...]


def _dest_rows(pstart, eidx, rank):
    rows, t = eidx.shape
    tr = ROUTE_TILE if t % ROUTE_TILE == 0 else LANES
    spec = pl.BlockSpec((rows, tr), lambda i, ps: (0, i))
    return pl.pallas_call(
        functools.partial(_dest_body, ne=pstart.shape[0]),
        grid_spec=pltpu.PrefetchScalarGridSpec(
            num_scalar_prefetch=1, grid=(t // tr,), in_specs=[spec, spec], out_specs=spec),
        out_shape=jax.ShapeDtypeStruct((rows, t), I32),
        name="dest",
    )(pstart, eidx, rank)


def _row_copy(src_ref, src_row, dst_ref, dst_row, sem):
    return pltpu.make_async_copy(src_ref.at[pl.ds(src_row, 1), :], dst_ref.at[pl.ds(dst_row, 1), :], sem)


def _dispatch_body(d_ref, h_ref, xs_in_ref, xs_ref, sem, *, td, topk):
    del xs_in_ref

    def issue(i, carry):
        for k in range(topk):
            _row_copy(h_ref, i, xs_ref, d_ref[k, i], sem).start()
        return carry

    lax.fori_loop(0, td, issue, 0)

    def drain(i, carry):
        for k in range(topk):
            _row_copy(h_ref, 0, xs_ref, 0, sem).wait()
        return carry

    lax.fori_loop(0, td, drain, 0)


def _dispatch(dest, hp, xs):
    t, w = hp.shape
    td = TOKEN_TILE
    return pl.pallas_call(
        functools.partial(_dispatch_body, td=td, topk=TOP_K),
        grid=(t // td,),
        in_specs=[pl.BlockSpec((SUBLANES, td), lambda i: (0, i), memory_space=pltpu.SMEM),
                  pl.BlockSpec((td, w), lambda i: (i, 0)),
                  pl.BlockSpec(memory_space=pl.ANY)],
        out_specs=pl.BlockSpec(memory_space=pl.ANY),
        scratch_shapes=[pltpu.SemaphoreType.DMA(())],
        out_shape=jax.ShapeDtypeStruct(xs.shape, xs.dtype),
        input_output_aliases={2: 0},
        compiler_params=pltpu.CompilerParams(dimension_semantics=("arbitrary",), has_side_effects=True),
        name="dispatch",
    )(dest, hp, xs)


def _expert_body(be_ref, nu_ref, x_ref, w1_ref, w3_ref, w2_ref, y_ref, w1s, w3s, w2s):
    i = pl.program_id(0)
    used = i < nu_ref[0]
    new_expert = jnp.logical_or(i == 0, be_ref[i] != be_ref[jnp.maximum(i - 1, 0)])

    @pl.when(jnp.logical_and(used, new_expert))
    def _():
        w1s[...] = w1_ref[0].astype(BF16)
        w3s[...] = w3_ref[0].astype(BF16)
        w2s[...] = w2_ref[0].astype(BF16)

    @pl.when(used)
    def _():
        half = x_ref.shape[1]
        lo, hi = _unpack_halves(x_ref[...])
        lo = lo.astype(BF16)
        hi = hi.astype(BF16)

        def up(ws):
            return (jnp.dot(lo, ws[0:half, :], preferred_element_type=F32)
                    + jnp.dot(hi, ws[half:2 * half, :], preferred_element_type=F32))

        hid = (_silu(up(w1s)) * up(w3s)).astype(BF16)
        y = jnp.dot(hid, w2s[...], preferred_element_type=F32)
        y_ref[...] = _pack_halves(y[:, 0:half], y[:, half:2 * half])

    @pl.when(pl.program_id(0) >= nu_ref[0])
    def _():
        y_ref[...] = jnp.zeros(y_ref.shape, U32)


def _experts(blk_e, n_used, xs, w1e, w3e, w2e):
    rows, half = xs.shape
    tm = EXPERT_ROWS
    d, de = w1e.shape[1], w1e.shape[2]
    blk = lambda i, be, nu: (jnp.minimum(i, nu[0] - 1), 0)
    return pl.pallas_call(
        _expert_body,
        grid_spec=pltpu.PrefetchScalarGridSpec(
            num_scalar_prefetch=2,
            grid=(rows // tm,),
            in_specs=[pl.BlockSpec((tm, half), blk),
                      pl.BlockSpec((1, d, de), lambda i, be, nu: (be[i], 0, 0)),
                      pl.BlockSpec((1, d, de), lambda i, be, nu: (be[i], 0, 0)),
                      pl.BlockSpec((1, de, d), lambda i, be, nu: (be[i], 0, 0))],
            out_specs=pl.BlockSpec((tm, half), lambda i, be, nu: (i, 0)),
            scratch_shapes=[pltpu.VMEM((d, de), BF16), pltpu.VMEM((d, de), BF16), pltpu.VMEM((de, d), BF16)]),
        out_shape=jax.ShapeDtypeStruct((rows, half), U32),
        compiler_params=pltpu.CompilerParams(dimension_semantics=("arbitrary",), vmem_limit_bytes=VMEM_LIMIT),
        name="experts",
    )(blk_e, n_used, xs, w1e, w3e, w2e)


def _combine_body(d_ref, x2_ref, gtf_ref, wt_ref, gfin_ref, ys_ref, y_ref, buf, sem, *, td, topk):
    def issue(i, carry):
        for k in range(topk):
            pltpu.make_async_copy(ys_ref.at[pl.ds(d_ref[k, i], 1), :], buf.at[k, pl.ds(i, 1), :], sem).start()
        return carry

    lax.fori_loop(0, td, issue, 0)

    def drain(i, carry):
        for k in range(topk):
            pltpu.make_async_copy(ys_ref.at[pl.ds(0, 1), :], buf.at[0, pl.ds(0, 1), :], sem).wait()
        return carry

    lax.fori_loop(0, td, drain, 0)

    half = buf.shape[2]
    acc_lo = jnp.zeros((td, half), F32)
    acc_hi = jnp.zeros((td, half), F32)
    for k in range(topk):
        lo, hi = _unpack_halves(buf[k])
        w = wt_ref[:, k:k + 1]
        acc_lo = acc_lo + w * lo
        acc_hi = acc_hi + w * hi
    x_lo = x2_ref[:, 0:half] + gtf_ref[:, 0:half] * acc_lo
    x_hi = x2_ref[:, half:2 * half] + gtf_ref[:, half:2 * half] * acc_hi
    ms = (jnp.sum(x_lo * x_lo, axis=-1, keepdims=True)
          + jnp.sum(x_hi * x_hi, axis=-1, keepdims=True)) / (2 * half)
    rs = lax.rsqrt(ms + EPS)
    y_ref[:, 0:half] = x_lo * rs * gfin_ref[:, 0:half]
    y_ref[:, half:2 * half] = x_hi * rs * gfin_ref[:, half:2 * half]


def _combine(dest, wt_rows, x2, mod, g_final, ys, *, rows_per_mod, per_row_mod):
    t, d = x2.shape
    td = TOKEN_TILE
    if per_row_mod:
        gtf_spec = pl.BlockSpec((td, d), lambda i: (i, 5))
    else:
        tiles_per_mod = rows_per_mod // td
        gtf_spec = pl.BlockSpec((None, None, 1, d), lambda i: (i // tiles_per_mod, 5, 0, 0))
    return pl.pallas_call(
        functools.partial(_combine_body, td=td, topk=TOP_K),
        grid=(t // td,),
        in_specs=[pl.BlockSpec((SUBLANES, td), lambda i: (0, i), memory_space=pltpu.SMEM),
                  pl.BlockSpec((td, d), lambda i: (i, 0)),
                  gtf_spec,
                  pl.BlockSpec((td, SUBLANES), lambda i: (i, 0)),
                  pl.BlockSpec((1, d), lambda i: (0, 0)),
                  pl.BlockSpec(memory_space=pl.ANY)],
        out_specs=pl.BlockSpec((td, d), lambda i: (i, 0)),
        scratch_shapes=[pltpu.VMEM((TOP_K, td, d // 2), U32), pltpu.SemaphoreType.DMA(())],
        out_shape=jax.ShapeDtypeStruct((t, d), F32),
        compiler_params=pltpu.CompilerParams(dimension_semantics=("arbitrary",)),
        name="combine",
    )(dest, x2, mod, wt_rows, g_final, ys)


def _retention_tables(length, nh, hd):
    c = math.gcd(length, RET_CHUNK)
    log_g = jnp.log(1.0 - 2.0 ** (-5.0 - jnp.arange(nh, dtype=F32)))
    idx = jnp.arange(c, dtype=F32)
    rel = idx[:, None] - idx[None, :]
    mask = jnp.where(rel >= 0, jnp.exp(log_g[:, None, None] * jnp.maximum(rel, 0.0)), 0.0)
    q_decay = jnp.exp(log_g[None, :] * (idx[:, None] + 1.0))
    k_decay = jnp.exp(log_g[None, :] * (c - 1.0 - idx[:, None]))
    chunk_decay = jnp.exp(log_g * c)
    qd = jnp.broadcast_to(q_decay.T[:, :, None], (nh, c, hd))
    kd = jnp.broadcast_to(k_decay.T[:, :, None], (nh, c, hd))
    cd = jnp.broadcast_to(chunk_decay[:, None, None], (nh, hd, hd))
    return mask, qd, kd, cd


def kernel(x_prompt, x_sample, c_prompt, c_sample, state_conv, state_ret, w_ada, b_ada, g_mix, g_ffn, w_in,
           conv_w, conv_b, conv_norm_g, conv_norm_b, ret_norm_g, ret_norm_b, w_out, w_router, router_bias,
           w1, w3, w2, ws1, ws3, ws2, g_final):
    depth = w_ada.shape[0]
    assert depth == 1, "single-layer trunk"
    bp, lp, d = x_prompt.shape
    bs, ls, _ = x_sample.shape
    assert ls == 1
    dc = conv_w.shape[2]
    dr = ret_norm_g.shape[1]
    nh = RET_HEADS
    hd = dr // nh
    assert hd == LANES and lp % 256 == 0 and bs % 16 == 0
    ne = w_router.shape[2]
    row = lambda a: a.reshape(1, -1)

    mod = _ada(jnp.concatenate([c_prompt, c_sample], axis=0), w_ada[0], row(b_ada[0]))
    mod_p = mod[:bp].reshape(bp, 6, 1, d)
    mod_s = mod[bp:]

    half = hd // 2
    inv = ROPE_BASE ** (-jnp.arange(half, dtype=F32) / half)
    inv2 = jnp.concatenate([inv, inv]).reshape(1, hd)
    cos_p, sin_p = _rope_tables(inv2, lp, 0)
    cos_s, sin_s = _rope_tables(inv2, SUBLANES, PAST_LEN)

    w_in_b = w_in[0].astype(BF16)
    wo_b = w_out[0].astype(BF16)
    wr_t = w_router[0].T
    wrh = wr_t.astype(BF16)
    wrl = (wr_t - wrh.astype(F32)).astype(BF16)
    ws1_b, ws3_b, ws2_b = ws1[0].astype(BF16), ws3[0].astype(BF16), ws2[0].astype(BF16)
    dims = dict(dc=dc, dr=dr, hd=hd)

    glu_p, q_p, k_p, v_p, sg_p = _proj(x_prompt, mod_p, row(g_mix[0]), w_in_b, cos_p, sin_p,
                                       per_row_mod=False, **dims)
    cat_p, ret_p = _mix(glu_p, q_p, k_p, v_p, sg_p, conv_w[0], row(conv_b[0]), row(conv_norm_g[0]),
                        row(conv_norm_b[0]), row(ret_norm_g[0]), row(ret_norm_b[0]),
                        _retention_tables(lp, nh, hd), nb=bp, length=lp, **dims)
    x2_p, hp_p, lg_p = _post(cat_p, x_prompt, mod_p, row(g_ffn[0]), wo_b, wrh, wrl, ws1_b, ws3_b, ws2_b,
                             per_row_mod=False)

    xs3 = x_sample.reshape(1, bs, d)
    glu_s, q_s, k_s, v_s, sg_s = _proj(xs3, mod_s, row(g_mix[0]), w_in_b, cos_s, sin_s,
                                       per_row_mod=True, **dims)
    log_g = jnp.log(1.0 - 2.0 ** (-5.0 - jnp.arange(nh, dtype=F32)))
    gam = jnp.broadcast_to(jnp.exp(log_g)[:, None, None], (nh, SUBLANES, hd))
    cat_s, ret_s = _mix1(glu_s, q_s, k_s, v_s, sg_s, state_conv[0], state_ret[0], conv_w[0], row(conv_b[0]),
                         row(conv_norm_g[0]), row(conv_norm_b[0]), row(ret_norm_g[0]), row(ret_norm_b[0]),
                         gam, **dims)
    x2_s, hp_s, lg_s = _post(cat_s, xs3, mod_s, row(g_ffn[0]), wo_b, wrh, wrl, ws1_b, ws3_b, ws2_b,
                             per_row_mod=True)

    tp = bp * lp
    eidx, wts, rank, cnt = _route(jnp.concatenate([lg_p, lg_s], axis=1), router_bias[0].reshape(ne, 1))
    counts = cnt[:, 0].astype(I32)
    tm = EXPERT_ROWS
    padded = (counts + tm - 1) // tm * tm
    pend = jnp.cumsum(padded)
    pstart = (pend - padded).astype(I32)
    n_assign = (tp + bs) * TOP_K
    n_blocks = -(-(n_assign + ne * (tm - 1)) // tm)
    n_used = (pend[-1] // tm).astype(I32).reshape(1)
    blk_row = jnp.arange(n_blocks, dtype=I32) * tm
    blk_e = jnp.minimum(jnp.sum((pend[None, :] <= blk_row[:, None]).astype(I32), axis=1), ne - 1)
    wt_rows = wts.T
    dest = _dest_rows(pstart, eidx, rank)

    xs = jnp.zeros((n_blocks * tm, d // 2), U32)
    xs = _dispatch(dest[:, :tp], hp_p, xs)
    xs = _dispatch(dest[:, tp:], hp_s, xs)
    ys = _experts(blk_e, n_used, xs, w1[0], w3[0], w2[0])
    y_p = _combine(dest[:, :tp], wt_rows[:tp], x2_p, mod_p, row(g_final), ys, rows_per_mod=lp, per_row_mod=False)
    y_s = _combine(dest[:, tp:], wt_rows[tp:], x2_s, mod_s, row(g_final), ys, rows_per_mod=bs, per_row_mod=True)

    new_conv_p = glu_p.reshape(bp, lp, dc)[:, lp - CONV_BUF:, :]
    new_conv_s = jnp.concatenate([state_conv[0][:, 1:, :], glu_s[:, None, :]], axis=1)
    return (y_p.reshape(bp, lp, d), y_s.reshape(bs, ls, d), new_conv_p[None], ret_p[None],
            new_conv_s[None], ret_s[None])
```

```python
import functools
import math

import jax
import jax.numpy as jnp
from jax import lax
from jax.experimental import pallas as pl
from jax.experimental.pallas import tpu as pltpu
from jax.experimental.pallas import tpu_sc as plsc

F32 = jnp.float32
BF16 = jnp.bfloat16
U32 = jnp.uint32
I32 = jnp.int32

EPS = 1e-6
PAST_LEN = 16384
RET_HEADS = 4
RET_CHUNK = 128
CONV_WIDTH = 31
CONV_BUF = CONV_WIDTH - 1
ROPE_BASE = 10000.0
N_EXPERTS = 64
TOP_K = 6
N_GROUPS = 8
TOPK_GROUPS = 4
ROUTED_SCALE = 2.5

LANES = 128
SUBLANES = 8
CONV_PAD = 32
EXPERT_ROWS = 256
ROUTE_TILE = 384
TOKEN_TILE = 128
VMEM_LIMIT = 56 * 1024 * 1024
SC_CORES = 2
SC_SUBCORES = 16
SC_WORKERS = SC_CORES * SC_SUBCORES
SCATTER_ROWS = 64
GATHER_ROWS = 32
ROW_WORDS = 4

HI_MASK = 0xFFFF0000


def _sigmoid(x):
    return jax.nn.sigmoid(x)


def _silu(x):
    return x * jax.nn.sigmoid(x)


def _pack_halves(lo, hi):
    lo_u = lax.bitcast_convert_type(lo.astype(BF16).astype(F32), U32) >> 16
    hi_u = lax.bitcast_convert_type(hi.astype(BF16).astype(F32), U32) & jnp.uint32(HI_MASK)
    return hi_u | lo_u


def _unpack_halves(p):
    lo = lax.bitcast_convert_type(p << 16, F32)
    hi = lax.bitcast_convert_type(p & jnp.uint32(HI_MASK), F32)
    return lo, hi


def _store_rows(ref, x):
    rows = x.shape[0]
    for j in range(ROW_WORDS):
        ref[pl.ds(j, rows, stride=ROW_WORDS), :] = x[:, LANES * j:LANES * (j + 1)]


def _load_row_word(ref, j, rows):
    return ref[pl.ds(j, rows, stride=ROW_WORDS), :]


def _ada_body(c_ref, w_ref, b_ref, o_ref):
    s = _silu(c_ref[...]).astype(BF16)
    o_ref[...] = jnp.dot(s, w_ref[...].astype(BF16), preferred_element_type=F32) + b_ref[...]


def _ada(c_all, w_ada, b_ada):
    rows, d = c_all.shape
    n = w_ada.shape[1]
    tn = 512
    return pl.pallas_call(
        _ada_body,
        grid=(n // tn,),
        in_specs=[
            pl.BlockSpec((rows, d), lambda j: (0, 0)),
            pl.BlockSpec((d, tn), lambda j: (0, j)),
            pl.BlockSpec((1, tn), lambda j: (0, j)),
        ],
        out_specs=pl.BlockSpec((rows, tn), lambda j: (0, j)),
        out_shape=jax.ShapeDtypeStruct((rows, n), F32),
        name="ada",
    )(c_all, w_ada, b_ada)


def _rope_body(inv_ref, cos_ref, sin_ref, *, pos0, tl, half):
    row = lax.broadcasted_iota(I32, (tl, LANES), 0) + pl.program_id(0) * tl
    ang = (row.astype(F32) + pos0) * inv_ref[...]
    lane = lax.broadcasted_iota(I32, (tl, LANES), 1)
    s = jnp.sin(ang)
    cos_ref[...] = jnp.cos(ang)
    sin_ref[...] = jnp.where(lane < half, -s, s)


def _rope_tables(inv2, rows, pos0):
    tl = min(rows, 256)
    return pl.pallas_call(
        functools.partial(_rope_body, pos0=float(pos0), tl=tl, half=LANES // 2),
        grid=(rows // tl,),
        in_specs=[pl.BlockSpec((1, LANES), lambda i: (0, 0))],
        out_specs=[pl.BlockSpec((tl, LANES), lambda i: (i, 0))] * 2,
        out_shape=[jax.ShapeDtypeStruct((rows, LANES), F32)] * 2,
        name="rope",
    )(inv2)


def _modulated_rmsnorm(x, g, sc, sh):
    ms = jnp.mean(x * x, axis=-1, keepdims=True)
    h = x * lax.rsqrt(ms + EPS) * g
    return h * (1.0 + sc) + sh


def _proj_body(x_ref, sh_ref, sc_ref, g_ref, w_ref, cos_ref, sin_ref,
               glu_ref, q_ref, k_ref, v_ref, sg_ref, *, dc, dr, hd, rope_rows):
    hb = _modulated_rmsnorm(x_ref[...], g_ref[...], sc_ref[...], sh_ref[...]).astype(BF16)

    def proj(lo, n):
        return jnp.dot(hb, w_ref[:, lo:lo + n], preferred_element_type=F32)

    glu_ref[...] = proj(0, dc) * _sigmoid(proj(dc, dc))
    cos = cos_ref[...] if rope_rows else cos_ref[0:1, :]
    sin = sin_ref[...] if rope_rows else sin_ref[0:1, :]
    for ref, lo, scale in ((q_ref, 2 * dc, hd ** -0.5), (k_ref, 2 * dc + dr, None)):
        t = proj(lo, dr)
        for hh in range(dr // hd):
            th = t[:, hh * hd:(hh + 1) * hd]
            r = th * cos + pltpu.roll(th, hd // 2, 1) * sin
            if scale is not None:
                r = r * scale
            ref[:, hh * hd:(hh + 1) * hd] = r.astype(BF16)
    v_ref[...] = proj(2 * dc + 2 * dr, dr).astype(BF16)
    sg_ref[...] = _silu(proj(2 * dc + 3 * dr, dr)).astype(BF16)


def _proj(x, mod, g_mix, w_in_b, cos2, sin2, *, dc, dr, hd, per_row_mod):
    nb, length, d = x.shape
    tl = min(length, 256)
    x2 = x.reshape(nb * length, d)
    nl = length // tl
    if per_row_mod:
        mod_spec = lambda j: pl.BlockSpec((tl, d), lambda b, l: (l, j))
        rope_spec = pl.BlockSpec((SUBLANES, LANES), lambda b, l: (0, 0))
    else:
        mod_spec = lambda j: pl.BlockSpec((None, None, 1, d), lambda b, l: (b, j, 0, 0))
        rope_spec = pl.BlockSpec((tl, LANES), lambda b, l: (l, 0))
    row_spec = lambda w: pl.BlockSpec((tl, w), lambda b, l: (b * nl + l, 0))
    t = nb * length
    outs = pl.pallas_call(
        functools.partial(_proj_body, dc=dc, dr=dr, hd=hd, rope_rows=not per_row_mod),
        grid=(nb, nl),
        in_specs=[
            row_spec(d), mod_spec(0), mod_spec(1),
            pl.BlockSpec((1, d), lambda b, l: (0, 0)),
            pl.BlockSpec(w_in_b.shape, lambda b, l: (0, 0)),
            rope_spec, rope_spec,
        ],
        out_specs=[row_spec(dc), row_spec(dr), row_spec(dr), row_spec(dr), row_spec(dr)],
        out_shape=[
            jax.ShapeDtypeStruct((t, dc), F32),
            jax.ShapeDtypeStruct((t, dr), BF16),
            jax.ShapeDtypeStruct((t, dr), BF16),
            jax.ShapeDtypeStruct((t, dr), BF16),
            jax.ShapeDtypeStruct((t, dr), BF16),
        ],
        compiler_params=pltpu.CompilerParams(vmem_limit_bytes=VMEM_LIMIT),
        name="proj",
    )(x2, mod, mod, g_mix, w_in_b, cos2, sin2)
    return outs


def _layernorm_silu(c, g, b):
    mu = jnp.mean(c, axis=-1, keepdims=True)
    d = c - mu
    var = jnp.mean(d * d, axis=-1, keepdims=True)
    return _silu(d * lax.rsqrt(var + EPS) * g + b)


def _groupnorm(o, g, b):
    mu = jnp.mean(o, axis=-1, keepdims=True)
    d = o - mu
    var = jnp.mean(d * d, axis=-1, keepdims=True)
    return d * lax.rsqrt(var + EPS) * g + b


def _mix_body(glu_ref, q_ref, k_ref, v_ref, sg_ref, cw_ref, cb_ref, lng_ref, lnb_ref, rg_ref, rb_ref,
              mask_ref, qd_ref, kd_ref, cd_ref, cat_ref, st_ref, buf, cscr, *, tl, dc, hd, nh, chunk):
    nslab = dc // LANES

    @pl.when(pl.program_id(1) == 0)
    def _():
        buf[:, 0:CONV_PAD, :] = jnp.zeros((nslab, CONV_PAD, LANES), F32)
        st_ref[...] = jnp.zeros(st_ref.shape, F32)

    for j in range(nslab):
        buf[j, CONV_PAD:CONV_PAD + tl, :] = glu_ref[:, LANES * j:LANES * (j + 1)]
    first = CONV_PAD - CONV_BUF
    rows_per_iter = 4 * SUBLANES
    for j in range(nslab):
        cols = slice(LANES * j, LANES * (j + 1))
        wv = [jnp.broadcast_to(cw_ref[t:t + 1, cols], (SUBLANES, LANES)) for t in range(CONV_WIDTH)]
        bias = jnp.broadcast_to(cb_ref[0:1, cols], (SUBLANES, LANES))

        def body(r, carry, j=j, cols=cols, wv=wv, bias=bias):
            base = pl.multiple_of(r * rows_per_iter, rows_per_iter)
            for u in range(rows_per_iter // SUBLANES):
                acc = bias
                for t in range(CONV_WIDTH):
                    acc = acc + wv[t] * buf[j, pl.ds(base + (u * SUBLANES + first + t), SUBLANES), :]
                cscr[pl.ds(base + u * SUBLANES, SUBLANES), cols] = acc
            return carry

        lax.fori_loop(0, tl // rows_per_iter, body, 0)
    for j in range(nslab):
        buf[j, 0:CONV_PAD, :] = buf[j, tl:tl + CONV_PAD, :]
    cat_ref[:, 0:dc] = _layernorm_silu(cscr[...], lng_ref[...], lnb_ref[...]).astype(BF16)

    nt = (((1,), (1,)), ((), ()))
    tn = (((0,), (0,)), ((), ()))
    for c in range(tl // chunk):
        rows = slice(c * chunk, (c + 1) * chunk)
        for hh in range(nh):
            cols = slice(hh * hd, (hh + 1) * hd)
            qh = q_ref[rows, cols]
            kh = k_ref[rows, cols]
            vh = v_ref[rows, cols]
            s = st_ref[0, hh]
            scores = lax.dot_general(qh, kh, nt, preferred_element_type=F32) * mask_ref[hh]
            inner = jnp.dot(scores.astype(BF16), vh, preferred_element_type=F32)
            qd = (qh.astype(F32) * qd_ref[hh]).astype(BF16)
            cross = jnp.dot(qd, s.astype(BF16), preferred_element_type=F32)
            kd = (kh.astype(F32) * kd_ref[hh]).astype(BF16)
            st_ref[0, hh] = cd_ref[hh] * s + lax.dot_general(kd, vh, tn, preferred_element_type=F32)
            o = _groupnorm(inner + cross, rg_ref[0:1, cols], rb_ref[0:1, cols])
            cat_ref[rows, dc + hh * hd:dc + (hh + 1) * hd] = (o * sg_ref[rows, cols].astype(F32)).astype(BF16)


def _mix(glu, q, k, v, sg, conv_w, conv_b, ln_g, ln_b, rg, rb, tables, *, nb, length, dc, dr, hd):
    nh = dr // hd
    chunk = math.gcd(length, RET_CHUNK)
    tl = min(length, 256)
    nl = length // tl
    mask, qd, kd, cd = tables
    row_spec = lambda w: pl.BlockSpec((tl, w), lambda b, l: (b * nl + l, 0))
    full = lambda a: pl.BlockSpec(a.shape, lambda b, l: (0,) * a.ndim)
    cat, st = pl.pallas_call(
        functools.partial(_mix_body, tl=tl, dc=dc, hd=hd, nh=nh, chunk=chunk),
        grid=(nb, nl),
        in_specs=[row_spec(dc), row_spec(dr), row_spec(dr), row_spec(dr), row_spec(dr),
                  full(conv_w), full(conv_b), full(ln_g), full(ln_b), full(rg), full(rb),
                  full(mask), full(qd), full(kd), full(cd)],
        out_specs=[row_spec(dc + dr), pl.BlockSpec((1, nh, hd, hd), lambda b, l: (b, 0, 0, 0))],
        out_shape=[jax.ShapeDtypeStruct((nb * length, dc + dr), BF16),
                   jax.ShapeDtypeStruct((nb, nh, hd, hd), F32)],
        scratch_shapes=[pltpu.VMEM((dc // LANES, tl + CONV_PAD, LANES), F32),
                        pltpu.VMEM((tl, dc), F32)],
        compiler_params=pltpu.CompilerParams(dimension_semantics=("arbitrary", "arbitrary")),
        name="mix",
    )(glu, q, k, v, sg, conv_w, conv_b, ln_g, ln_b, rg, rb, mask, qd, kd, cd)
    return cat, st


def _mix1_body(glu_ref, q_ref, k_ref, v_ref, sg_ref, sc_ref, s0_ref, cw_ref, cb_ref, lng_ref, lnb_ref,
               rg_ref, rb_ref, gam_ref, cat_ref, st_ref, cscr, oscr, qf, kf, vf, *, tb, dc, hd, nh):
    w_hist = cw_ref[0:CONV_BUF, :]
    w_last = cw_ref[CONV_BUF:CONV_WIDTH, :]
    for bb in range(tb):
        hist = jnp.sum(sc_ref[bb] * w_hist, axis=0, keepdims=True)
        cscr[bb:bb + 1, :] = hist + glu_ref[bb:bb + 1, :] * w_last + cb_ref[...]
    cat_ref[:, 0:dc] = _layernorm_silu(cscr[...], lng_ref[...], lnb_ref[...]).astype(BF16)

    tn = (((0,), (0,)), ((), ()))
    rowid = lax.broadcasted_iota(I32, (tb, hd), 0)
    qf[...] = q_ref[...].astype(F32)
    kf[...] = k_ref[...].astype(F32)
    vf[...] = v_ref[...].astype(F32)
    for hh in range(nh):
        cols = slice(hh * hd, (hh + 1) * hd)
        qa = q_ref[:, cols]
        ka = k_ref[:, cols]
        gam = gam_ref[hh, 0:1, :]
        for bb in range(tb):
            onehot = (rowid == bb).astype(BF16)
            qcol = lax.dot_general(qa, onehot, tn, preferred_element_type=F32)
            kcol = lax.dot_general(ka, onehot, tn, preferred_element_type=F32)
            s0 = s0_ref[bb, hh]
            qrow = qf[bb:bb + 1, cols]
            krow = kf[bb:bb + 1, cols]
            vrow = vf[bb:bb + 1, cols]
            qk = jnp.sum(qrow * krow, axis=-1, keepdims=True)
            cross = gam * jnp.sum(qcol * s0, axis=0, keepdims=True)
            st_ref[bb, hh] = gam * s0 + kcol * vrow
            oscr[bb:bb + 1, cols] = qk * vrow + cross
    for hh in range(nh):
        cols = slice(hh * hd, (hh + 1) * hd)
        o = _groupnorm(oscr[:, cols], rg_ref[0:1, cols], rb_ref[0:1, cols])
        cat_ref[:, dc + hh * hd:dc + (hh + 1) * hd] = (o * sg_ref[:, cols].astype(F32)).astype(BF16)


def _mix1(glu, q, k, v, sg, state_conv, state_ret, conv_w, conv_b, ln_g, ln_b, rg, rb, gam, *, dc, dr, hd):
    nb = glu.shape[0]
    nh = dr // hd
    tb = 16
    row_spec = lambda w: pl.BlockSpec((tb, w), lambda i: (i, 0))
    full = lambda a: pl.BlockSpec(a.shape, lambda i: (0,) * a.ndim)
    st_spec = pl.BlockSpec((tb, nh, hd, hd), lambda i: (i, 0, 0, 0))
    cat, st = pl.pallas_call(
        functools.partial(_mix1_body, tb=tb, dc=dc, hd=hd, nh=nh),
        grid=(nb // tb,),
        in_specs=[row_spec(dc), row_spec(dr), row_spec(dr), row_spec(dr), row_spec(dr),
                  pl.BlockSpec((tb, CONV_BUF, dc), lambda i: (i, 0, 0)), st_spec,
                  full(conv_w), full(conv_b), full(ln_g), full(ln_b), full(rg), full(rb), full(gam)],
        out_specs=[row_spec(dc + dr), st_spec],
        out_shape=[jax.ShapeDtypeStruct((nb, dc + dr), BF16),
                   jax.ShapeDtypeStruct((nb, nh, hd, hd), F32)],
        scratch_shapes=[pltpu.VMEM((tb, dc), F32)] + [pltpu.VMEM((tb, dr), F32)] * 4,
        compiler_params=pltpu.CompilerParams(vmem_limit_bytes=VMEM_LIMIT),
        name="mix1",
    )(glu, q, k, v, sg, state_conv, state_ret, conv_w, conv_b, ln_g, ln_b, rg, rb, gam)
    return cat, st


def _post_body(cat_ref, x_ref, gtm_ref, scf_ref, shf_ref, gtf_ref, g_ref, wo_ref, wrh_ref, wrl_ref,
               ws1_ref, ws3_ref, ws2_ref, x2_ref, hp_ref, lg_ref):
    d = x_ref.shape[1]
    y = jnp.dot(cat_ref[...], wo_ref[...], preferred_element_type=F32)
    x1 = x_ref[...] + gtm_ref[...] * y
    h = _modulated_rmsnorm(x1, g_ref[...], scf_ref[...], shf_ref[...])
    hb = h.astype(BF16)
    _store_rows(hp_ref, _pack_halves(h[:, 0:d // 2], h[:, d // 2:d]))
    hl = (h - hb.astype(F32)).astype(BF16)
    nt = (((1,), (1,)), ((), ()))
    lg_ref[...] = (lax.dot_general(wrh_ref[...], hb, nt, preferred_element_type=F32)
                   + lax.dot_general(wrh_ref[...], hl, nt, preferred_element_type=F32)
                   + lax.dot_general(wrl_ref[...], hb, nt, preferred_element_type=F32))
    s1 = jnp.dot(hb, ws1_ref[...], preferred_element_type=F32)
    s3 = jnp.dot(hb, ws3_ref[...], preferred_element_type=F32)
    shared = jnp.dot((_silu(s1) * s3).astype(BF16), ws2_ref[...], preferred_element_type=F32)
    x2_ref[...] = x1 + gtf_ref[...] * shared


def _post(cat, x, mod, g_ffn, wo_b, wrh, wrl, ws1_b, ws3_b, ws2_b, *, per_row_mod):
    nb, length, d = x.shape
    tl = min(length, 512)
    nl = length // tl
    t = nb * length
    ne = wrh.shape[0]
    x2d = x.reshape(t, d)
    if per_row_mod:
        mod_spec = lambda j: pl.BlockSpec((tl, d), lambda b, l: (l, j))
    else:
        mod_spec = lambda j: pl.BlockSpec((None, None, 1, d), lambda b, l: (b, j, 0, 0))
    row_spec = lambda w: pl.BlockSpec((tl, w), lambda b, l: (b * nl + l, 0))
    full = lambda a: pl.BlockSpec(a.shape, lambda b, l: (0,) * a.ndim)
    return pl.pallas_call(
        _post_body,
        grid=(nb, nl),
        in_specs=[row_spec(d), row_spec(d), mod_spec(2), mod_spec(4), mod_spec(3), mod_spec(5),
                  full(g_ffn), full(wo_b), full(wrh), full(wrl), full(ws1_b), full(ws3_b), full(ws2_b)],
        out_specs=[row_spec(d),
                   pl.BlockSpec((tl * ROW_WORDS, LANES), lambda b, l: (b * nl + l, 0)),
                   pl.BlockSpec((ne, tl), lambda b, l: (0, b * nl + l))],
        out_shape=[jax.ShapeDtypeStruct((t, d), F32),
                   jax.ShapeDtypeStruct((t * ROW_WORDS, LANES), U32),
                   jax.ShapeDtypeStruct((ne, t), F32)],
        compiler_params=pltpu.CompilerParams(vmem_limit_bytes=VMEM_LIMIT),
        name="post",
    )(cat, x2d, mod, mod, mod, mod, g_ffn, wo_b, wrh, wrl, ws1_b, ws3_b, ws2_b)


def _first_max(x, idx, sentinel):
    m = jnp.max(x, axis=0, keepdims=True)
    f = jnp.min(jnp.where(x == m, idx, sentinel), axis=0, keepdims=True)
    return m, f


def _route_body(lg_ref, bias_ref, e_ref, w_ref, r_ref, cnt_ref, cnt_scr, *, tr, ne, ng, topk, topg):
    @pl.when(pl.program_id(0) == 0)
    def _():
        cnt_scr[...] = jnp.zeros(cnt_scr.shape, F32)

    per = ne // ng
    neg = -jnp.inf
    scores = _sigmoid(lg_ref[...])
    sel = scores + bias_ref[...]
    sub = lax.broadcasted_iota(I32, (per, tr), 0)
    gs = []
    for g in range(ng):
        s_g = sel[g * per:(g + 1) * per, :]
        m1, f1 = _first_max(s_g, sub, per)
        m2 = jnp.max(jnp.where(sub == f1, neg, s_g), axis=0, keepdims=True)
        gs.append(m1 + m2)
    gsc = jnp.concatenate(gs, axis=0)
    gi = lax.broadcasted_iota(I32, (ng, tr), 0)
    keep = jnp.zeros((ng, tr), F32)
    for _ in range(topg):
        _, f = _first_max(gsc, gi, ng)
        pick = gi == f
        keep = jnp.where(pick, 1.0, keep)
        gsc = jnp.where(pick, neg, gsc)
    work = jnp.concatenate(
        [jnp.where(keep[g:g + 1, :] > 0.5, sel[g * per:(g + 1) * per, :], neg) for g in range(ng)], axis=0)
    ei = lax.broadcasted_iota(I32, (ne, tr), 0)
    picks, es, ws = [], [], []
    for _ in range(topk):
        _, f = _first_max(work, ei, ne)
        pick = ei == f
        picks.append(pick)
        es.append(f)
        ws.append(jnp.sum(jnp.where(pick, scores, 0.0), axis=0, keepdims=True))
        work = jnp.where(pick, neg, work)
    wsum = ws[0]
    for w in ws[1:]:
        wsum = wsum + w
    scale = ROUTED_SCALE / wsum
    chosen = picks[0]
    for p in picks[1:]:
        chosen = jnp.logical_or(chosen, p)
    chosen_f = chosen.astype(F32)
    t_row = lax.broadcasted_iota(I32, (tr, tr), 0)
    t_col = lax.broadcasted_iota(I32, (tr, tr), 1)
    before = (t_row < t_col).astype(BF16)
    prior = cnt_scr[:, 0:1] + jnp.dot(chosen_f.astype(BF16), before, preferred_element_type=F32)
    rs = [jnp.sum(jnp.where(p, prior, 0.0), axis=0, keepdims=True).astype(I32) for p in picks]
    pad_i = jnp.zeros((SUBLANES - topk, tr), I32)
    pad_f = jnp.zeros((SUBLANES - topk, tr), F32)
    e_ref[...] = jnp.concatenate(es + [pad_i], axis=0)
    w_ref[...] = jnp.concatenate([w * scale for w in ws] + [pad_f], axis=0)
    r_ref[...] = jnp.concatenate(rs + [pad_i], axis=0)
    total = cnt_scr[:, 0:1] + jnp.sum(chosen_f, axis=1, keepdims=True)
    cnt_scr[...] = jnp.broadcast_to(total, cnt_scr.shape)
    cnt_ref[...] = jnp.broadcast_to(total, cnt_ref.shape)


def _route(logits_t, bias_col):
    ne, t = logits_t.shape
    tr = ROUTE_TILE if t % ROUTE_TILE == 0 else LANES
    tok = lambda dt: jax.ShapeDtypeStruct((SUBLANES, t), dt)
    tok_spec = pl.BlockSpec((SUBLANES, tr), lambda i: (0, i))
    return pl.pallas_call(
        functools.partial(_route_body, tr=tr, ne=ne, ng=N_GROUPS, topk=TOP_K, topg=TOPK_GROUPS),
        grid=(t // tr,),
        in_specs=[pl.BlockSpec((ne, tr), lambda i: (0, i)), pl.BlockSpec((ne, 1), lambda i: (0, 0))],
        out_specs=[tok_spec, tok_spec, tok_spec, pl.BlockSpec((ne, LANES), lambda i: (0, 0))],
        out_shape=[tok(I32), tok(F32), tok(I32), jax.ShapeDtypeStruct((ne, LANES), F32)],
        scratch_shapes=[pltpu.VMEM((ne, LANES), F32)],
        compiler_params=pltpu.CompilerParams(dimension_semantics=("arbitrary",)),
        name="route",
    )(logits_t, bias_col)


def _dest_body(ps_ref, e_ref, r_ref, d_ref, *, ne, tr):
    e = e_ref[...]
    base = jnp.zeros(e.shape, I32)
    for j in range(ne):
        base = jnp.where(e == j, ps_ref[j], base)
    dest = base + r_ref[...]
    for m in range(tr // TOKEN_TILE):
        d_ref[SUBLANES * m:SUBLANES * (m + 1), :] = dest[:, TOKEN_TILE * m:TOKEN_TILE * (m + 1)]


def _dest_rows(pstart, eidx, rank):
    rows, t = eidx.shape
    tr = ROUTE_TILE if t % ROUTE_TILE == 0 else LANES
    spec = pl.BlockSpec((rows, tr), lambda i, ps: (0, i))
    return pl.pallas_call(
        functools.partial(_dest_body, ne=pstart.shape[0], tr=tr),
        grid_spec=pltpu.PrefetchScalarGridSpec(
            num_scalar_prefetch=1, grid=(t // tr,), in_specs=[spec, spec],
            out_specs=pl.BlockSpec((tr // TOKEN_TILE * rows, TOKEN_TILE), lambda i, ps: (i, 0))),
        out_shape=jax.ShapeDtypeStruct((t // TOKEN_TILE * rows, TOKEN_TILE), I32),
        name="dest",
    )(pstart, eidx, rank)


def _sc_mesh():
    return plsc.VectorSubcoreMesh(core_axis_name="c", subcore_axis_name="s")


def _sc_worker_id():
    return lax.axis_index("s") * SC_CORES + lax.axis_index("c")


def _index_block(dest_ref, chunk, width):
    per_tile = TOKEN_TILE // width
    return dest_ref.at[chunk // per_tile, :, pl.ds((chunk % per_tile) * width, width)]


def _dispatch(hp_p, hp_s, dest3, n_rows):
    w = SCATTER_ROWS
    width = hp_p.shape[1]
    ncp = hp_p.shape[0] // w
    nch = ncp + hp_s.shape[0] // w

    @functools.partial(
        pl.kernel, mesh=_sc_mesh(),
        out_type=jax.ShapeDtypeStruct((n_rows, width), U32),
        scratch_types=[pltpu.VMEM((SUBLANES, w), I32), pltpu.VMEM((w, width), U32), pltpu.SemaphoreType.DMA],
        compiler_params=pltpu.CompilerParams(use_tc_tiling_on_sc=False),
        name="dispatch",
    )
    def run(hpp_ref, hps_ref, dest_ref, xs_ref, idx_v, rows_v, sem):
        wid = _sc_worker_id()

        @pl.loop(0, pl.cdiv(nch, SC_WORKERS))
        def _(r):
            c = r * SC_WORKERS + wid

            @pl.when(c < nch)
            def _():
                pltpu.sync_copy(_index_block(dest_ref, c, w), idx_v)

                @pl.when(c < ncp)
                def _():
                    pltpu.sync_copy(hpp_ref.at[pl.ds(c * w, w)], rows_v)

                @pl.when(c >= ncp)
                def _():
                    pltpu.sync_copy(hps_ref.at[pl.ds((c - ncp) * w, w)], rows_v)

                copies = [pltpu.async_copy(rows_v, xs_ref.at[idx_v.at[k]], sem) for k in range(TOP_K)]
                for cp in copies:
                    cp.wait()

    return run(hp_p, hp_s, dest3)


def _undispatch(ys, dest3, n_tokens):
    w = GATHER_ROWS
    width = ys.shape[1]
    nch = n_tokens // w

    @functools.partial(
        pl.kernel, mesh=_sc_mesh(),
        out_type=jax.ShapeDtypeStruct((TOP_K, n_tokens, width), U32),
        scratch_types=[pltpu.VMEM((SUBLANES, w), I32), pltpu.VMEM((TOP_K, w, width), U32),
                       pltpu.SemaphoreType.DMA],
        compiler_params=pltpu.CompilerParams(use_tc_tiling_on_sc=False),
        name="undispatch",
    )
    def run(ys_ref, dest_ref, z_ref, idx_v, bufs, sem):
        wid = _sc_worker_id()

        @pl.loop(0, pl.cdiv(nch, SC_WORKERS))
        def _(r):
            c = r * SC_WORKERS + wid

            @pl.when(c < nch)
            def _():
                pltpu.sync_copy(_index_block(dest_ref, c, w), idx_v)
                gathers = [pltpu.async_copy(ys_ref.at[idx_v.at[k]], bufs.at[k], sem) for k in range(TOP_K)]
                for cp in gathers:
                    cp.wait()
                stores = [pltpu.async_copy(bufs.at[k], z_ref.at[k, pl.ds(c * w, w)], sem) for k in range(TOP_K)]
                for cp in stores:
                    cp.wait()

    return run(ys, dest3)


def _expert_body(be_ref, nu_ref, x_ref, w1_ref, w3_ref, w2_ref, y_ref, w1s, w3s, w2s, xlo, xhi, *, tm):
    i = pl.program_id(0)
    used = i < nu_ref[0]
    new_expert = jnp.logical_or(i == 0, be_ref[i] != be_ref[jnp.maximum(i - 1, 0)])

    @pl.when(jnp.logical_and(used, new_expert))
    def _():
        w1s[...] = w1_ref[0].astype(BF16)
        w3s[...] = w3_ref[0].astype(BF16)
        w2s[...] = w2_ref[0].astype(BF16)

    @pl.when(used)
    def _():
        half = ROW_WORDS * LANES
        for j in range(ROW_WORDS):
            lo, hi = _unpack_halves(_load_row_word(x_ref, j, tm))
            xlo[:, LANES * j:LANES * (j + 1)] = lo.astype(BF16)
            xhi[:, LANES * j:LANES * (j + 1)] = hi.astype(BF16)

        def up(ws):
            return (jnp.dot(xlo[...], ws[0:half, :], preferred_element_type=F32)
                    + jnp.dot(xhi[...], ws[half:2 * half, :], preferred_element_type=F32))

        hid = (_silu(up(w1s)) * up(w3s)).astype(BF16)
        y = jnp.dot(hid, w2s[...], preferred_element_type=F32)
        _store_rows(y_ref, _pack_halves(y[:, 0:half], y[:, half:2 * half]))

    @pl.when(pl.program_id(0) >= nu_ref[0])
    def _():
        y_ref[...] = jnp.zeros(y_ref.shape, U32)


def _experts(blk_e, n_used, xs, w1e, w3e, w2e):
    tm = EXPERT_ROWS
    n_blocks = xs.shape[0] // (tm * ROW_WORDS)
    d, de = w1e.shape[1], w1e.shape[2]
    half = ROW_WORDS * LANES
    blk = lambda i, be, nu: (jnp.minimum(i, nu[0] - 1), 0)
    return pl.pallas_call(
        functools.partial(_expert_body, tm=tm),
        grid_spec=pltpu.PrefetchScalarGridSpec(
            num_scalar_prefetch=2,
            grid=(n_blocks,),
            in_specs=[pl.BlockSpec((tm * ROW_WORDS, LANES), blk),
                      pl.BlockSpec((1, d, de), lambda i, be, nu: (be[i], 0, 0)),
                      pl.BlockSpec((1, d, de), lambda i, be, nu: (be[i], 0, 0)),
                      pl.BlockSpec((1, de, d), lambda i, be, nu: (be[i], 0, 0))],
            out_specs=pl.BlockSpec((tm * ROW_WORDS, LANES), lambda i, be, nu: (i, 0)),
            scratch_shapes=[pltpu.VMEM((d, de), BF16), pltpu.VMEM((d, de), BF16), pltpu.VMEM((de, d), BF16),
                            pltpu.VMEM((tm, half), BF16), pltpu.VMEM((tm, half), BF16)]),
        out_shape=jax.ShapeDtypeStruct(xs.shape, U32),
        compiler_params=pltpu.CompilerParams(dimension_semantics=("arbitrary",), vmem_limit_bytes=VMEM_LIMIT),
        name="experts",
    )(blk_e, n_used, xs, w1e, w3e, w2e)


def _combine_body(z_ref, x2_ref, gtf_ref, wt_ref, gfin_ref, y_ref, xo, *, td, topk):
    half = ROW_WORDS * LANES
    ws = [wt_ref[:, k:k + 1] for k in range(topk)]
    sq = jnp.zeros((td, 1), F32)
    for j in range(ROW_WORDS):
        acc_lo = jnp.zeros((td, LANES), F32)
        acc_hi = jnp.zeros((td, LANES), F32)
        for k in range(topk):
            lo, hi = _unpack_halves(_load_row_word(z_ref.at[k], j, td))
            acc_lo = acc_lo + ws[k] * lo
            acc_hi = acc_hi + ws[k] * hi
        for base, acc in ((0, acc_lo), (half, acc_hi)):
            cols = slice(base + LANES * j, base + LANES * (j + 1))
            x = x2_ref[:, cols] + gtf_ref[:, cols] * acc
            xo[:, cols] = x
            sq = sq + jnp.sum(x * x, axis=-1, keepdims=True)
    rs = lax.rsqrt(sq / (2 * half) + EPS)
    y_ref[...] = xo[...] * rs * gfin_ref[...]


def _combine(z, tile0, wt_rows, x2, mod, g_final, *, rows_per_mod, per_row_mod):
    t, d = x2.shape
    td = TOKEN_TILE
    if per_row_mod:
        gtf_spec = pl.BlockSpec((td, d), lambda i: (i, 5))
    else:
        tiles_per_mod = rows_per_mod // td
        gtf_spec = pl.BlockSpec((None, None, 1, d), lambda i: (i // tiles_per_mod, 5, 0, 0))
    return pl.pallas_call(
        functools.partial(_combine_body, td=td, topk=TOP_K),
        grid=(t // td,),
        in_specs=[pl.BlockSpec((TOP_K, td * ROW_WORDS, LANES), lambda i: (0, i + tile0, 0)),
                  pl.BlockSpec((td, d), lambda i: (i, 0)),
                  gtf_spec,
                  pl.BlockSpec((td, SUBLANES), lambda i: (i, 0)),
                  pl.BlockSpec((1, d), lambda i: (0, 0))],
        out_specs=pl.BlockSpec((td, d), lambda i: (i, 0)),
        scratch_shapes=[pltpu.VMEM((td, d), F32)],
        out_shape=jax.ShapeDtypeStruct((t, d), F32),
        name="combine",
    )(z, x2, mod, wt_rows, g_final)


def _retention_tables(length, nh, hd):
    c = math.gcd(length, RET_CHUNK)
    log_g = jnp.log(1.0 - 2.0 ** (-5.0 - jnp.arange(nh, dtype=F32)))
    idx = jnp.arange(c, dtype=F32)
    rel = idx[:, None] - idx[None, :]
    mask = jnp.where(rel >= 0, jnp.exp(log_g[:, None, None] * jnp.maximum(rel, 0.0)), 0.0)
    q_decay = jnp.exp(log_g[None, :] * (idx[:, None] + 1.0))
    k_decay = jnp.exp(log_g[None, :] * (c - 1.0 - idx[:, None]))
    chunk_decay = jnp.exp(log_g * c)
    qd = jnp.broadcast_to(q_decay.T[:, :, None], (nh, c, hd))
    kd = jnp.broadcast_to(k_decay.T[:, :, None], (nh, c, hd))
    cd = jnp.broadcast_to(chunk_decay[:, None, None], (nh, hd, hd))
    return mask, qd, kd, cd


def kernel(x_prompt, x_sample, c_prompt, c_sample, state_conv, state_ret, w_ada, b_ada, g_mix, g_ffn, w_in,
           conv_w, conv_b, conv_norm_g, conv_norm_b, ret_norm_g, ret_norm_b, w_out, w_router, router_bias,
           w1, w3, w2, ws1, ws3, ws2, g_final):
    depth = w_ada.shape[0]
    assert depth == 1, "single-layer trunk"
    bp, lp, d = x_prompt.shape
    bs, ls, _ = x_sample.shape
    assert ls == 1
    dc = conv_w.shape[2]
    dr = ret_norm_g.shape[1]
    nh = RET_HEADS
    hd = dr // nh
    assert hd == LANES and lp % 256 == 0 and bs % TOKEN_TILE == 0 and d // 2 == ROW_WORDS * LANES
    ne = w_router.shape[2]
    row = lambda a: a.reshape(1, -1)

    mod = _ada(jnp.concatenate([c_prompt, c_sample], axis=0), w_ada[0], row(b_ada[0]))
    mod_p = mod[:bp].reshape(bp, 6, 1, d)
    mod_s = mod[bp:]

    half = hd // 2
    inv = ROPE_BASE ** (-jnp.arange(half, dtype=F32) / half)
    inv2 = jnp.concatenate([inv, inv]).reshape(1, hd)
    cos_p, sin_p = _rope_tables(inv2, lp, 0)
    cos_s, sin_s = _rope_tables(inv2, SUBLANES, PAST_LEN)

    w_in_b = w_in[0].astype(BF16)
    wo_b = w_out[0].astype(BF16)
    wr_t = w_router[0].T
    wrh = wr_t.astype(BF16)
    wrl = (wr_t - wrh.astype(F32)).astype(BF16)
    ws1_b, ws3_b, ws2_b = ws1[0].astype(BF16), ws3[0].astype(BF16), ws2[0].astype(BF16)
    dims = dict(dc=dc, dr=dr, hd=hd)

    glu_p, q_p, k_p, v_p, sg_p = _proj(x_prompt, mod_p, row(g_mix[0]), w_in_b, cos_p, sin_p,
                                       per_row_mod=False, **dims)
    cat_p, ret_p = _mix(glu_p, q_p, k_p, v_p, sg_p, conv_w[0], row(conv_b[0]), row(conv_norm_g[0]),
                        row(conv_norm_b[0]), row(ret_norm_g[0]), row(ret_norm_b[0]),
                        _retention_tables(lp, nh, hd), nb=bp, length=lp, **dims)
    x2_p, hp_p, lg_p = _post(cat_p, x_prompt, mod_p, row(g_ffn[0]), wo_b, wrh, wrl, ws1_b, ws3_b, ws2_b,
                             per_row_mod=False)

    xs3 = x_sample.reshape(1, bs, d)
    glu_s, q_s, k_s, v_s, sg_s = _proj(xs3, mod_s, row(g_mix[0]), w_in_b, cos_s, sin_s,
                                       per_row_mod=True, **dims)
    log_g = jnp.log(1.0 - 2.0 ** (-5.0 - jnp.arange(nh, dtype=F32)))
    gam = jnp.broadcast_to(jnp.exp(log_g)[:, None, None], (nh, SUBLANES, hd))
    cat_s, ret_s = _mix1(glu_s, q_s, k_s, v_s, sg_s, state_conv[0], state_ret[0], conv_w[0], row(conv_b[0]),
                         row(conv_norm_g[0]), row(conv_norm_b[0]), row(ret_norm_g[0]), row(ret_norm_b[0]),
                         gam, **dims)
    x2_s, hp_s, lg_s = _post(cat_s, xs3, mod_s, row(g_ffn[0]), wo_b, wrh, wrl, ws1_b, ws3_b, ws2_b,
                             per_row_mod=True)

    tp = bp * lp
    eidx, wts, rank, cnt = _route(jnp.concatenate([lg_p, lg_s], axis=1), router_bias[0].reshape(ne, 1))
    counts = cnt[:, 0].astype(I32)
    tm = EXPERT_ROWS
    padded = (counts + tm - 1) // tm * tm
    pend = jnp.cumsum(padded)
    pstart = (pend - padded).astype(I32)
    n_assign = (tp + bs) * TOP_K
    n_blocks = -(-(n_assign + ne * (tm - 1)) // tm)
    n_used = (pend[-1] // tm).astype(I32).reshape(1)
    blk_row = jnp.arange(n_blocks, dtype=I32) * tm
    blk_e = jnp.minimum(jnp.sum((pend[None, :] <= blk_row[:, None]).astype(I32), axis=1), ne - 1)
    wt_rows = wts.T
    tall = tp + bs
    n_rows = n_blocks * tm
    hw = d // 2
    dest3 = _dest_rows(pstart, eidx, rank).reshape(tall // TOKEN_TILE, SUBLANES, TOKEN_TILE)

    xs = _dispatch(hp_p.reshape(tp, hw), hp_s.reshape(bs, hw), dest3, n_rows)
    ys = _experts(blk_e, n_used, xs.reshape(n_rows * ROW_WORDS, LANES), w1[0], w3[0], w2[0])
    z = _undispatch(ys.reshape(n_rows, hw), dest3, tall).reshape(TOP_K, tall * ROW_WORDS, LANES)
    y_p = _combine(z, 0, wt_rows[:tp], x2_p, mod_p, row(g_final), rows_per_mod=lp, per_row_mod=False)
    y_s = _combine(z, tp // TOKEN_TILE, wt_rows[tp:], x2_s, mod_s, row(g_final), rows_per_mod=bs, per_row_mod=True)

    new_conv_p = glu_p.reshape(bp, lp, dc)[:, lp - CONV_BUF:, :]
    new_conv_s = jnp.concatenate([state_conv[0][:, 1:, :], glu_s[:, None, :]], axis=1)
    return (y_p.reshape(bp, lp, d), y_s.reshape(bs, ls, d), new_conv_p[None], ret_p[None],
            new_conv_s[None], ret_s[None])
```

```python
import functools
import math

import jax
import jax.numpy as jnp
from jax import lax
from jax.experimental import pallas as pl
from jax.experimental.pallas import tpu as pltpu
from jax.experimental.pallas import tpu_sc as plsc

F32 = jnp.float32
BF16 = jnp.bfloat16
U32 = jnp.uint32
I32 = jnp.int32

EPS = 1e-6
PAST_LEN = 16384
RET_HEADS = 4
RET_CHUNK = 128
CONV_WIDTH = 31
CONV_BUF = CONV_WIDTH - 1
ROPE_BASE = 10000.0
N_EXPERTS = 64
TOP_K = 6
N_GROUPS = 8
TOPK_GROUPS = 4
ROUTED_SCALE = 2.5

LANES = 128
SUBLANES = 8
CONV_PAD = 32
EXPERT_ROWS = 256
ROUTE_TILE = 384
TOKEN_TILE = 128
VMEM_LIMIT = 56 * 1024 * 1024
SC_CORES = 2
SC_SUBCORES = 16
SC_WORKERS = SC_CORES * SC_SUBCORES
SCATTER_ROWS = 64
GATHER_ROWS = 32
ROW_WORDS = 4

HI_MASK = 0xFFFF0000


def _sigmoid(x):
    return jax.nn.sigmoid(x)


def _silu(x):
    return x * jax.nn.sigmoid(x)


def _pack_halves(lo, hi):
    lo_u = lax.bitcast_convert_type(lo.astype(BF16).astype(F32), U32) >> 16
    hi_u = lax.bitcast_convert_type(hi.astype(BF16).astype(F32), U32) & jnp.uint32(HI_MASK)
    return hi_u | lo_u


def _unpack_halves(p):
    lo = lax.bitcast_convert_type(p << 16, F32)
    hi = lax.bitcast_convert_type(p & jnp.uint32(HI_MASK), F32)
    return lo, hi


def _store_rows(ref, x):
    rows = x.shape[0]
    for j in range(ROW_WORDS):
        ref[pl.ds(j, rows, stride=ROW_WORDS), :] = x[:, LANES * j:LANES * (j + 1)]


def _load_row_word(ref, j, rows):
    return ref[pl.ds(j, rows, stride=ROW_WORDS), :]


def _ada_body(c_ref, w_ref, b_ref, o_ref):
    s = _silu(c_ref[...]).astype(BF16)
    o_ref[...] = jnp.dot(s, w_ref[...].astype(BF16), preferred_element_type=F32) + b_ref[...]


def _ada(c_all, w_ada, b_ada):
    rows, d = c_all.shape
    n = w_ada.shape[1]
    tn = 512
    return pl.pallas_call(
        _ada_body,
        grid=(n // tn,),
        in_specs=[
            pl.BlockSpec((rows, d), lambda j: (0, 0)),
            pl.BlockSpec((d, tn), lambda j: (0, j)),
            pl.BlockSpec((1, tn), lambda j: (0, j)),
        ],
        out_specs=pl.BlockSpec((rows, tn), lambda j: (0, j)),
        out_shape=jax.ShapeDtypeStruct((rows, n), F32),
        name="ada",
    )(c_all, w_ada, b_ada)


def _rope_body(inv_ref, cos_ref, sin_ref, *, pos0, tl, half):
    row = lax.broadcasted_iota(I32, (tl, LANES), 0) + pl.program_id(0) * tl
    ang = (row.astype(F32) + pos0) * inv_ref[...]
    lane = lax.broadcasted_iota(I32, (tl, LANES), 1)
    s = jnp.sin(ang)
    cos_ref[...] = jnp.cos(ang)
    sin_ref[...] = jnp.where(lane < half, -s, s)


def _rope_tables(inv2, rows, pos0):
    tl = min(rows, 256)
    return pl.pallas_call(
        functools.partial(_rope_body, pos0=float(pos0), tl=tl, half=LANES // 2),
        grid=(rows // tl,),
        in_specs=[pl.BlockSpec((1, LANES), lambda i: (0, 0))],
        out_specs=[pl.BlockSpec((tl, LANES), lambda i: (i, 0))] * 2,
        out_shape=[jax.ShapeDtypeStruct((rows, LANES), F32)] * 2,
        name="rope",
    )(inv2)


def _modulated_rmsnorm(x, g, sc, sh):
    ms = jnp.mean(x * x, axis=-1, keepdims=True)
    h = x * lax.rsqrt(ms + EPS) * g
    return h * (1.0 + sc) + sh


def _proj_body(x_ref, sh_ref, sc_ref, g_ref, w_ref, cos_ref, sin_ref,
               glu_ref, q_ref, k_ref, v_ref, sg_ref, *, dc, dr, hd, rope_rows):
    hb = _modulated_rmsnorm(x_ref[...], g_ref[...], sc_ref[...], sh_ref[...]).astype(BF16)

    def proj(lo, n):
        return jnp.dot(hb, w_ref[:, lo:lo + n], preferred_element_type=F32)

    glu_ref[...] = proj(0, dc) * _sigmoid(proj(dc, dc))
    cos = cos_ref[...] if rope_rows else cos_ref[0:1, :]
    sin = sin_ref[...] if rope_rows else sin_ref[0:1, :]
    for ref, lo, scale in ((q_ref, 2 * dc, hd ** -0.5), (k_ref, 2 * dc + dr, None)):
        t = proj(lo, dr)
        for hh in range(dr // hd):
            th = t[:, hh * hd:(hh + 1) * hd]
            r = th * cos + pltpu.roll(th, hd // 2, 1) * sin
            if scale is not None:
                r = r * scale
            ref[:, hh * hd:(hh + 1) * hd] = r.astype(BF16)
    v_ref[...] = proj(2 * dc + 2 * dr, dr).astype(BF16)
    sg_ref[...] = _silu(proj(2 * dc + 3 * dr, dr)).astype(BF16)


def _proj(x, mod, g_mix, w_in_b, cos2, sin2, *, dc, dr, hd, per_row_mod):
    nb, length, d = x.shape
    tl = min(length, 256)
    x2 = x.reshape(nb * length, d)
    nl = length // tl
    if per_row_mod:
        mod_spec = lambda j: pl.BlockSpec((tl, d), lambda b, l: (l, j))
        rope_spec = pl.BlockSpec((SUBLANES, LANES), lambda b, l: (0, 0))
    else:
        mod_spec = lambda j: pl.BlockSpec((None, None, 1, d), lambda b, l: (b, j, 0, 0))
        rope_spec = pl.BlockSpec((tl, LANES), lambda b, l: (l, 0))
    row_spec = lambda w: pl.BlockSpec((tl, w), lambda b, l: (b * nl + l, 0))
    t = nb * length
    outs = pl.pallas_call(
        functools.partial(_proj_body, dc=dc, dr=dr, hd=hd, rope_rows=not per_row_mod),
        grid=(nb, nl),
        in_specs=[
            row_spec(d), mod_spec(0), mod_spec(1),
            pl.BlockSpec((1, d), lambda b, l: (0, 0)),
            pl.BlockSpec(w_in_b.shape, lambda b, l: (0, 0)),
            rope_spec, rope_spec,
        ],
        out_specs=[row_spec(dc), row_spec(dr), row_spec(dr), row_spec(dr), row_spec(dr)],
        out_shape=[
            jax.ShapeDtypeStruct((t, dc), F32),
            jax.ShapeDtypeStruct((t, dr), BF16),
            jax.ShapeDtypeStruct((t, dr), BF16),
            jax.ShapeDtypeStruct((t, dr), BF16),
            jax.ShapeDtypeStruct((t, dr), BF16),
        ],
        compiler_params=pltpu.CompilerParams(vmem_limit_bytes=VMEM_LIMIT),
        name="proj",
    )(x2, mod, mod, g_mix, w_in_b, cos2, sin2)
    return outs


def _layernorm_silu(c, g, b):
    mu = jnp.mean(c, axis=-1, keepdims=True)
    d = c - mu
    var = jnp.mean(d * d, axis=-1, keepdims=True)
    return _silu(d * lax.rsqrt(var + EPS) * g + b)


def _groupnorm(o, g, b):
    mu = jnp.mean(o, axis=-1, keepdims=True)
    d = o - mu
    var = jnp.mean(d * d, axis=-1, keepdims=True)
    return d * lax.rsqrt(var + EPS) * g + b


def _mix_body(glu_ref, q_ref, k_ref, v_ref, sg_ref, cw_ref, cb_ref, lng_ref, lnb_ref, rg_ref, rb_ref,
              mask_ref, qd_ref, kd_ref, cd_ref, cat_ref, st_ref, buf, cscr, *, tl, dc, hd, nh, chunk):
    nslab = dc // LANES

    @pl.when(pl.program_id(1) == 0)
    def _():
        buf[:, 0:CONV_PAD, :] = jnp.zeros((nslab, CONV_PAD, LANES), F32)
        st_ref[...] = jnp.zeros(st_ref.shape, F32)

    for j in range(nslab):
        buf[j, CONV_PAD:CONV_PAD + tl, :] = glu_ref[:, LANES * j:LANES * (j + 1)]
    first = CONV_PAD - CONV_BUF
    rows_per_iter = 4 * SUBLANES
    for j in range(nslab):
        cols = slice(LANES * j, LANES * (j + 1))
        wv = [jnp.broadcast_to(cw_ref[t:t + 1, cols], (SUBLANES, LANES)) for t in range(CONV_WIDTH)]
        bias = jnp.broadcast_to(cb_ref[0:1, cols], (SUBLANES, LANES))

        def body(r, carry, j=j, cols=cols, wv=wv, bias=bias):
            base = pl.multiple_of(r * rows_per_iter, rows_per_iter)
            for u in range(rows_per_iter // SUBLANES):
                acc = bias
                for t in range(CONV_WIDTH):
                    acc = acc + wv[t] * buf[j, pl.ds(base + (u * SUBLANES + first + t), SUBLANES), :]
                cscr[pl.ds(base + u * SUBLANES, SUBLANES), cols] = acc
            return carry

        lax.fori_loop(0, tl // rows_per_iter, body, 0)
    for j in range(nslab):
        buf[j, 0:CONV_PAD, :] = buf[j, tl:tl + CONV_PAD, :]
    cat_ref[:, 0:dc] = _layernorm_silu(cscr[...], lng_ref[...], lnb_ref[...]).astype(BF16)

    nt = (((1,), (1,)), ((), ()))
    tn = (((0,), (0,)), ((), ()))
    for c in range(tl // chunk):
        rows = slice(c * chunk, (c + 1) * chunk)
        for hh in range(nh):
            cols = slice(hh * hd, (hh + 1) * hd)
            qh = q_ref[rows, cols]
            kh = k_ref[rows, cols]
            vh = v_ref[rows, cols]
            s = st_ref[0, hh]
            scores = lax.dot_general(qh, kh, nt, preferred_element_type=F32) * mask_ref[hh]
            inner = jnp.dot(scores.astype(BF16), vh, preferred_element_type=F32)
            qd = (qh.astype(F32) * qd_ref[hh]).astype(BF16)
            cross = jnp.dot(qd, s.astype(BF16), preferred_element_type=F32)
            kd = (kh.astype(F32) * kd_ref[hh]).astype(BF16)
            st_ref[0, hh] = cd_ref[hh] * s + lax.dot_general(kd, vh, tn, preferred_element_type=F32)
            o = _groupnorm(inner + cross, rg_ref[0:1, cols], rb_ref[0:1, cols])
            cat_ref[rows, dc + hh * hd:dc + (hh + 1) * hd] = (o * sg_ref[rows, cols].astype(F32)).astype(BF16)


def _mix(glu, q, k, v, sg, conv_w, conv_b, ln_g, ln_b, rg, rb, tables, *, nb, length, dc, dr, hd):
    nh = dr // hd
    chunk = math.gcd(length, RET_CHUNK)
    tl = min(length, 256)
    nl = length // tl
    mask, qd, kd, cd = tables
    row_spec = lambda w: pl.BlockSpec((tl, w), lambda b, l: (b * nl + l, 0))
    full = lambda a: pl.BlockSpec(a.shape, lambda b, l: (0,) * a.ndim)
    cat, st = pl.pallas_call(
        functools.partial(_mix_body, tl=tl, dc=dc, hd=hd, nh=nh, chunk=chunk),
        grid=(nb, nl),
        in_specs=[row_spec(dc), row_spec(dr), row_spec(dr), row_spec(dr), row_spec(dr),
                  full(conv_w), full(conv_b), full(ln_g), full(ln_b), full(rg), full(rb),
                  full(mask), full(qd), full(kd), full(cd)],
        out_specs=[row_spec(dc + dr), pl.BlockSpec((1, nh, hd, hd), lambda b, l: (b, 0, 0, 0))],
        out_shape=[jax.ShapeDtypeStruct((nb * length, dc + dr), BF16),
                   jax.ShapeDtypeStruct((nb, nh, hd, hd), F32)],
        scratch_shapes=[pltpu.VMEM((dc // LANES, tl + CONV_PAD, LANES), F32),
                        pltpu.VMEM((tl, dc), F32)],
        compiler_params=pltpu.CompilerParams(dimension_semantics=("arbitrary", "arbitrary")),
        name="mix",
    )(glu, q, k, v, sg, conv_w, conv_b, ln_g, ln_b, rg, rb, mask, qd, kd, cd)
    return cat, st


def _mix1_body(glu_ref, q_ref, k_ref, v_ref, sg_ref, sc_ref, s0_ref, cw_ref, cb_ref, lng_ref, lnb_ref,
               rg_ref, rb_ref, gam_ref, cat_ref, st_ref, cscr, oscr, qf, kf, vf, *, tb, dc, hd, nh):
    w_hist = cw_ref[0:CONV_BUF, :]
    w_last = cw_ref[CONV_BUF:CONV_WIDTH, :]
    for bb in range(tb):
        hist = jnp.sum(sc_ref[bb] * w_hist, axis=0, keepdims=True)
        cscr[bb:bb + 1, :] = hist + glu_ref[bb:bb + 1, :] * w_last + cb_ref[...]
    cat_ref[:, 0:dc] = _layernorm_silu(cscr[...], lng_ref[...], lnb_ref[...]).astype(BF16)

    tn = (((0,), (0,)), ((), ()))
    rowid = lax.broadcasted_iota(I32, (tb, hd), 0)
    qf[...] = q_ref[...].astype(F32)
    kf[...] = k_ref[...].astype(F32)
    vf[...] = v_ref[...].astype(F32)
    for hh in range(nh):
        cols = slice(hh * hd, (hh + 1) * hd)
        qa = q_ref[:, cols]
        ka = k_ref[:, cols]
        gam = gam_ref[hh, 0:1, :]
        for bb in range(tb):
            onehot = (rowid == bb).astype(BF16)
            qcol = lax.dot_general(qa, onehot, tn, preferred_element_type=F32)
            kcol = lax.dot_general(ka, onehot, tn, preferred_element_type=F32)
            s0 = s0_ref[bb, hh]
            qrow = qf[bb:bb + 1, cols]
            krow = kf[bb:bb + 1, cols]
            vrow = vf[bb:bb + 1, cols]
            qk = jnp.sum(qrow * krow, axis=-1, keepdims=True)
            cross = gam * jnp.sum(qcol * s0, axis=0, keepdims=True)
            st_ref[bb, hh] = gam * s0 + kcol * vrow
            oscr[bb:bb + 1, cols] = qk * vrow + cross
    for hh in range(nh):
        cols = slice(hh * hd, (hh + 1) * hd)
        o = _groupnorm(oscr[:, cols], rg_ref[0:1, cols], rb_ref[0:1, cols])
        cat_ref[:, dc + hh * hd:dc + (hh + 1) * hd] = (o * sg_ref[:, cols].astype(F32)).astype(BF16)


def _mix1(glu, q, k, v, sg, state_conv, state_ret, conv_w, conv_b, ln_g, ln_b, rg, rb, gam, *, dc, dr, hd):
    nb = glu.shape[0]
    nh = dr // hd
    tb = 16
    row_spec = lambda w: pl.BlockSpec((tb, w), lambda i: (i, 0))
    full = lambda a: pl.BlockSpec(a.shape, lambda i: (0,) * a.ndim)
    st_spec = pl.BlockSpec((tb, nh, hd, hd), lambda i: (i, 0, 0, 0))
    cat, st = pl.pallas_call(
        functools.partial(_mix1_body, tb=tb, dc=dc, hd=hd, nh=nh),
        grid=(nb // tb,),
        in_specs=[row_spec(dc), row_spec(dr), row_spec(dr), row_spec(dr), row_spec(dr),
                  pl.BlockSpec((tb, CONV_BUF, dc), lambda i: (i, 0, 0)), st_spec,
                  full(conv_w), full(conv_b), full(ln_g), full(ln_b), full(rg), full(rb), full(gam)],
        out_specs=[row_spec(dc + dr), st_spec],
        out_shape=[jax.ShapeDtypeStruct((nb, dc + dr), BF16),
                   jax.ShapeDtypeStruct((nb, nh, hd, hd), F32)],
        scratch_shapes=[pltpu.VMEM((tb, dc), F32)] + [pltpu.VMEM((tb, dr), F32)] * 4,
        compiler_params=pltpu.CompilerParams(vmem_limit_bytes=VMEM_LIMIT),
        name="mix1",
    )(glu, q, k, v, sg, state_conv, state_ret, conv_w, conv_b, ln_g, ln_b, rg, rb, gam)
    return cat, st


def _post_body(cat_ref, x_ref, gtm_ref, scf_ref, shf_ref, gtf_ref, g_ref, wo_ref, wrh_ref, wrl_ref,
               ws1_ref, ws3_ref, ws2_ref, x2_ref, hp_ref, lg_ref):
    d = x_ref.shape[1]
    y = jnp.dot(cat_ref[...], wo_ref[...], preferred_element_type=F32)
    x1 = x_ref[...] + gtm_ref[...] * y
    h = _modulated_rmsnorm(x1, g_ref[...], scf_ref[...], shf_ref[...])
    hb = h.astype(BF16)
    _store_rows(hp_ref, _pack_halves(h[:, 0:d // 2], h[:, d // 2:d]))
    hl = (h - hb.astype(F32)).astype(BF16)
    nt = (((1,), (1,)), ((), ()))
    lg_ref[...] = (lax.dot_general(wrh_ref[...], hb, nt, preferred_element_type=F32)
                   + lax.dot_general(wrh_ref[...], hl, nt, preferred_element_type=F32)
                   + lax.dot_general(wrl_ref[...], hb, nt, preferred_element_type=F32))
    s1 = jnp.dot(hb, ws1_ref[...], preferred_element_type=F32)
    s3 = jnp.dot(hb, ws3_ref[...], preferred_element_type=F32)
    shared = jnp.dot((_silu(s1) * s3).astype(BF16), ws2_ref[...], preferred_element_type=F32)
    x2_ref[...] = x1 + gtf_ref[...] * shared


def _post(cat, x, mod, g_ffn, wo_b, wrh, wrl, ws1_b, ws3_b, ws2_b, *, per_row_mod):
    nb, length, d = x.shape
    tl = min(length, 512)
    nl = length // tl
    t = nb * length
    ne = wrh.shape[0]
    x2d = x.reshape(t, d)
    if per_row_mod:
        mod_spec = lambda j: pl.BlockSpec((tl, d), lambda b, l: (l, j))
    else:
        mod_spec = lambda j: pl.BlockSpec((None, None, 1, d), lambda b, l: (b, j, 0, 0))
    row_spec = lambda w: pl.BlockSpec((tl, w), lambda b, l: (b * nl + l, 0))
    full = lambda a: pl.BlockSpec(a.shape, lambda b, l: (0,) * a.ndim)
    return pl.pallas_call(
        _post_body,
        grid=(nb, nl),
        in_specs=[row_spec(d), row_spec(d), mod_spec(2), mod_spec(4), mod_spec(3), mod_spec(5),
                  full(g_ffn), full(wo_b), full(wrh), full(wrl), full(ws1_b), full(ws3_b), full(ws2_b)],
        out_specs=[row_spec(d),
                   pl.BlockSpec((tl * ROW_WORDS, LANES), lambda b, l: (b * nl + l, 0)),
                   pl.BlockSpec((ne, tl), lambda b, l: (0, b * nl + l))],
        out_shape=[jax.ShapeDtypeStruct((t, d), F32),
                   jax.ShapeDtypeStruct((t * ROW_WORDS, LANES), U32),
                   jax.ShapeDtypeStruct((ne, t), F32)],
        compiler_params=pltpu.CompilerParams(vmem_limit_bytes=VMEM_LIMIT),
        name="post",
    )(cat, x2d, mod, mod, mod, mod, g_ffn, wo_b, wrh, wrl, ws1_b, ws3_b, ws2_b)


def _first_max(x, idx, sentinel):
    m = jnp.max(x, axis=0, keepdims=True)
    f = jnp.min(jnp.where(x == m, idx, sentinel), axis=0, keepdims=True)
    return m, f


def _route_body(lg_ref, bias_ref, e_ref, w_ref, r_ref, cnt_ref, cnt_scr, *, tr, ne, ng, topk, topg):
    @pl.when(pl.program_id(0) == 0)
    def _():
        cnt_scr[...] = jnp.zeros(cnt_scr.shape, F32)

    per = ne // ng
    neg = -jnp.inf
    scores = _sigmoid(lg_ref[...])
    sel = scores + bias_ref[...]
    sub = lax.broadcasted_iota(I32, (per, tr), 0)
    gs = []
    for g in range(ng):
        s_g = sel[g * per:(g + 1) * per, :]
        m1, f1 = _first_max(s_g, sub, per)
        m2 = jnp.max(jnp.where(sub == f1, neg, s_g), axis=0, keepdims=True)
        gs.append(m1 + m2)
    gsc = jnp.concatenate(gs, axis=0)
    gi = lax.broadcasted_iota(I32, (ng, tr), 0)
    keep = jnp.zeros((ng, tr), F32)
    for _ in range(topg):
        _, f = _first_max(gsc, gi, ng)
        pick = gi == f
        keep = jnp.where(pick, 1.0, keep)
        gsc = jnp.where(pick, neg, gsc)
    work = jnp.concatenate(
        [jnp.where(keep[g:g + 1, :] > 0.5, sel[g * per:(g + 1) * per, :], neg) for g in range(ng)], axis=0)
    ei = lax.broadcasted_iota(I32, (ne, tr), 0)
    picks, es, ws = [], [], []
    for _ in range(topk):
        _, f = _first_max(work, ei, ne)
        pick = ei == f
        picks.append(pick)
        es.append(f)
        ws.append(jnp.sum(jnp.where(pick, scores, 0.0), axis=0, keepdims=True))
        work = jnp.where(pick, neg, work)
    wsum = ws[0]
    for w in ws[1:]:
        wsum = wsum + w
    scale = ROUTED_SCALE / wsum
    chosen = picks[0]
    for p in picks[1:]:
        chosen = jnp.logical_or(chosen, p)
    chosen_f = chosen.astype(F32)
    t_row = lax.broadcasted_iota(I32, (tr, tr), 0)
    t_col = lax.broadcasted_iota(I32, (tr, tr), 1)
    before = (t_row < t_col).astype(BF16)
    prior = cnt_scr[:, 0:1] + jnp.dot(chosen_f.astype(BF16), before, preferred_element_type=F32)
    rs = [jnp.sum(jnp.where(p, prior, 0.0), axis=0, keepdims=True).astype(I32) for p in picks]
    pad_i = jnp.zeros((SUBLANES - topk, tr), I32)
    pad_f = jnp.zeros((SUBLANES - topk, tr), F32)
    e_ref[...] = jnp.concatenate(es + [pad_i], axis=0)
    w_ref[...] = jnp.concatenate([w * scale for w in ws] + [pad_f], axis=0)
    r_ref[...] = jnp.concatenate(rs + [pad_i], axis=0)
    total = cnt_scr[:, 0:1] + jnp.sum(chosen_f, axis=1, keepdims=True)
    cnt_scr[...] = jnp.broadcast_to(total, cnt_scr.shape)
    cnt_ref[...] = jnp.broadcast_to(total, cnt_ref.shape)


def _route(logits_t, bias_col):
    ne, t = logits_t.shape
    tr = ROUTE_TILE if t % ROUTE_TILE == 0 else LANES
    tok = lambda dt: jax.ShapeDtypeStruct((SUBLANES, t), dt)
    tok_spec = pl.BlockSpec((SUBLANES, tr), lambda i: (0, i))
    return pl.pallas_call(
        functools.partial(_route_body, tr=tr, ne=ne, ng=N_GROUPS, topk=TOP_K, topg=TOPK_GROUPS),
        grid=(t // tr,),
        in_specs=[pl.BlockSpec((ne, tr), lambda i: (0, i)), pl.BlockSpec((ne, 1), lambda i: (0, 0))],
        out_specs=[tok_spec, tok_spec, tok_spec, pl.BlockSpec((ne, LANES), lambda i: (0, 0))],
        out_shape=[tok(I32), tok(F32), tok(I32), jax.ShapeDtypeStruct((ne, LANES), F32)],
        scratch_shapes=[pltpu.VMEM((ne, LANES), F32)],
        compiler_params=pltpu.CompilerParams(dimension_semantics=("arbitrary",)),
        name="route",
    )(logits_t, bias_col)


def _dest_body(ps_ref, e_ref, r_ref, d_ref, *, ne, tr):
    e = e_ref[...]
    base = jnp.zeros(e.shape, I32)
    for j in range(ne):
        base = jnp.where(e == j, ps_ref[j], base)
    dest = base + r_ref[...]
    for m in range(tr // TOKEN_TILE):
        d_ref[SUBLANES * m:SUBLANES * (m + 1), :] = dest[:, TOKEN_TILE * m:TOKEN_TILE * (m + 1)]


def _dest_rows(pstart, eidx, rank):
    rows, t = eidx.shape
    tr = ROUTE_TILE if t % ROUTE_TILE == 0 else LANES
    spec = pl.BlockSpec((rows, tr), lambda i, ps: (0, i))
    return pl.pallas_call(
        functools.partial(_dest_body, ne=pstart.shape[0], tr=tr),
        grid_spec=pltpu.PrefetchScalarGridSpec(
            num_scalar_prefetch=1, grid=(t // tr,), in_specs=[spec, spec],
            out_specs=pl.BlockSpec((tr // TOKEN_TILE * rows, TOKEN_TILE), lambda i, ps: (i, 0))),
        out_shape=jax.ShapeDtypeStruct((t // TOKEN_TILE * rows, TOKEN_TILE), I32),
        name="dest",
    )(pstart, eidx, rank)


def _sc_mesh():
    return plsc.VectorSubcoreMesh(core_axis_name="c", subcore_axis_name="s")


def _sc_worker_id():
    return lax.axis_index("s") * SC_CORES + lax.axis_index("c")


def _index_block(dest_ref, chunk, width):
    per_tile = TOKEN_TILE // width
    return dest_ref.at[chunk // per_tile, :, pl.ds((chunk % per_tile) * width, width)]


def _dispatch(hp_p, hp_s, dest3, n_rows):
    w = SCATTER_ROWS
    width = hp_p.shape[1]
    ncp = hp_p.shape[0] // w
    nch = ncp + hp_s.shape[0] // w

    @functools.partial(
        pl.kernel, mesh=_sc_mesh(),
        out_type=jax.ShapeDtypeStruct((n_rows, width), U32),
        scratch_types=[pltpu.VMEM((SUBLANES, w), I32), pltpu.VMEM((w, width), U32), pltpu.SemaphoreType.DMA],
        compiler_params=pltpu.CompilerParams(use_tc_tiling_on_sc=False),
        name="dispatch",
    )
    def run(hpp_ref, hps_ref, dest_ref, xs_ref, idx_v, rows_v, sem):
        wid = _sc_worker_id()

        @pl.loop(0, pl.cdiv(nch, SC_WORKERS))
        def _(r):
            c = r * SC_WORKERS + wid

            @pl.when(c < nch)
            def _():
                pltpu.sync_copy(_index_block(dest_ref, c, w), idx_v)

                @pl.when(c < ncp)
                def _():
                    pltpu.sync_copy(hpp_ref.at[pl.ds(c * w, w)], rows_v)

                @pl.when(c >= ncp)
                def _():
                    pltpu.sync_copy(hps_ref.at[pl.ds((c - ncp) * w, w)], rows_v)

                copies = [pltpu.async_copy(rows_v, xs_ref.at[idx_v.at[k]], sem) for k in range(TOP_K)]
                for cp in copies:
                    cp.wait()

    return run(hp_p, hp_s, dest3)


def _undispatch(ys, dest3, n_tokens):
    w = GATHER_ROWS
    width = ys.shape[1]
    nch = n_tokens // w

    @functools.partial(
        pl.kernel, mesh=_sc_mesh(),
        out_type=jax.ShapeDtypeStruct((TOP_K, n_tokens, width), U32),
        scratch_types=[pltpu.VMEM((SUBLANES, w), I32), pltpu.VMEM((TOP_K, w, width), U32),
                       pltpu.SemaphoreType.DMA],
        compiler_params=pltpu.CompilerParams(use_tc_tiling_on_sc=False),
        name="undispatch",
    )
    def run(ys_ref, dest_ref, z_ref, idx_v, bufs, sem):
        wid = _sc_worker_id()

        @pl.loop(0, pl.cdiv(nch, SC_WORKERS))
        def _(r):
            c = r * SC_WORKERS + wid

            @pl.when(c < nch)
            def _():
                pltpu.sync_copy(_index_block(dest_ref, c, w), idx_v)
                gathers = [pltpu.async_copy(ys_ref.at[idx_v.at[k]], bufs.at[k], sem) for k in range(TOP_K)]
                for cp in gathers:
                    cp.wait()
                stores = [pltpu.async_copy(bufs.at[k], z_ref.at[k, pl.ds(c * w, w)], sem) for k in range(TOP_K)]
                for cp in stores:
                    cp.wait()

    return run(ys, dest3)


def _expert_body(ps_ref, nb_ref, xs_ref, w1_ref, w3_ref, w2_ref, ys_ref,
                 w1s, w3s, w2s, xbuf, ybuf, xlo, xhi, sem_in, sem_out, *, tm):
    e = pl.program_id(0)
    nb = nb_ref[e]
    row0 = ps_ref[e]
    blk_words = tm * ROW_WORDS
    half = ROW_WORDS * LANES

    def x_copy(j, slot):
        src = xs_ref.at[pl.ds(pl.multiple_of((row0 + j * tm) * ROW_WORDS, blk_words), blk_words), :]
        return pltpu.make_async_copy(src, xbuf.at[slot], sem_in.at[slot])

    def y_copy(j, slot):
        dst = ys_ref.at[pl.ds(pl.multiple_of((row0 + j * tm) * ROW_WORDS, blk_words), blk_words), :]
        return pltpu.make_async_copy(ybuf.at[slot], dst, sem_out.at[slot])

    @pl.when(nb > 0)
    def _():
        x_copy(0, 0).start()
        w1s[...] = w1_ref[0].astype(BF16)
        w3s[...] = w3_ref[0].astype(BF16)
        w2s[...] = w2_ref[0].astype(BF16)

    def block(j, carry):
        slot = j % 2
        x_copy(j, slot).wait()

        @pl.when(j + 1 < nb)
        def _():
            x_copy(j + 1, 1 - slot).start()

        @pl.when(j >= 2)
        def _():
            y_copy(j - 2, slot).wait()

        xin = xbuf.at[slot]
        for w in range(ROW_WORDS):
            lo, hi = _unpack_halves(_load_row_word(xin, w, tm))
            xlo[:, LANES * w:LANES * (w + 1)] = lo.astype(BF16)
            xhi[:, LANES * w:LANES * (w + 1)] = hi.astype(BF16)

        def up(ws):
            return (jnp.dot(xlo[...], ws[0:half, :], preferred_element_type=F32)
                    + jnp.dot(xhi[...], ws[half:2 * half, :], preferred_element_type=F32))

        hid = (_silu(up(w1s)) * up(w3s)).astype(BF16)
        y = jnp.dot(hid, w2s[...], preferred_element_type=F32)
        _store_rows(ybuf.at[slot], _pack_halves(y[:, 0:half], y[:, half:2 * half]))
        y_copy(j, slot).start()
        return carry

    lax.fori_loop(0, nb, block, 0)

    @pl.when(nb >= 2)
    def _():
        y_copy(nb - 2, nb % 2).wait()

    @pl.when(nb >= 1)
    def _():
        y_copy(nb - 1, (nb - 1) % 2).wait()


def _experts(pstart, n_blk, xs, w1e, w3e, w2e):
    tm = EXPERT_ROWS
    ne, d, de = w1e.shape
    half = ROW_WORDS * LANES
    wspec = lambda a, b: pl.BlockSpec((1, a, b), lambda e, ps, nb: (e, 0, 0))
    return pl.pallas_call(
        functools.partial(_expert_body, tm=tm),
        grid_spec=pltpu.PrefetchScalarGridSpec(
            num_scalar_prefetch=2,
            grid=(ne,),
            in_specs=[pl.BlockSpec(memory_space=pl.ANY), wspec(d, de), wspec(d, de), wspec(de, d)],
            out_specs=pl.BlockSpec(memory_space=pl.ANY),
            scratch_shapes=[pltpu.VMEM((d, de), BF16), pltpu.VMEM((d, de), BF16), pltpu.VMEM((de, d), BF16),
                            pltpu.VMEM((2, tm * ROW_WORDS, LANES), U32),
                            pltpu.VMEM((2, tm * ROW_WORDS, LANES), U32),
                            pltpu.VMEM((tm, half), BF16), pltpu.VMEM((tm, half), BF16),
                            pltpu.SemaphoreType.DMA((2,)), pltpu.SemaphoreType.DMA((2,))]),
        out_shape=jax.ShapeDtypeStruct(xs.shape, U32),
        compiler_params=pltpu.CompilerParams(dimension_semantics=("arbitrary",), vmem_limit_bytes=VMEM_LIMIT),
        name="experts",
    )(pstart, n_blk, xs, w1e, w3e, w2e)


def _combine_body(z_ref, x2_ref, gtf_ref, wt_ref, gfin_ref, y_ref, xo, *, td, topk):
    half = ROW_WORDS * LANES
    ws = [wt_ref[:, k:k + 1] for k in range(topk)]
    sq = jnp.zeros((td, 1), F32)
    for j in range(ROW_WORDS):
        acc_lo = jnp.zeros((td, LANES), F32)
        acc_hi = jnp.zeros((td, LANES), F32)
        for k in range(topk):
            lo, hi = _unpack_halves(_load_row_word(z_ref.at[k], j, td))
            acc_lo = acc_lo + ws[k] * lo
            acc_hi = acc_hi + ws[k] * hi
        for base, acc in ((0, acc_lo), (half, acc_hi)):
            cols = slice(base + LANES * j, base + LANES * (j + 1))
            x = x2_ref[:, cols] + gtf_ref[:, cols] * acc
            xo[:, cols] = x
            sq = sq + jnp.sum(x * x, axis=-1, keepdims=True)
    rs = lax.rsqrt(sq / (2 * half) + EPS)
    y_ref[...] = xo[...] * rs * gfin_ref[...]


def _combine(z, tile0, wt_rows, x2, mod, g_final, *, rows_per_mod, per_row_mod):
    t, d = x2.shape
    td = TOKEN_TILE
    if per_row_mod:
        gtf_spec = pl.BlockSpec((td, d), lambda i: (i, 5))
    else:
        tiles_per_mod = rows_per_mod // td
        gtf_spec = pl.BlockSpec((None, None, 1, d), lambda i: (i // tiles_per_mod, 5, 0, 0))
    return pl.pallas_call(
        functools.partial(_combine_body, td=td, topk=TOP_K),
        grid=(t // td,),
        in_specs=[pl.BlockSpec((TOP_K, td * ROW_WORDS, LANES), lambda i: (0, i + tile0, 0)),
                  pl.BlockSpec((td, d), lambda i: (i, 0)),
                  gtf_spec,
                  pl.BlockSpec((td, SUBLANES), lambda i: (i, 0)),
                  pl.BlockSpec((1, d), lambda i: (0, 0))],
        out_specs=pl.BlockSpec((td, d), lambda i: (i, 0)),
        scratch_shapes=[pltpu.VMEM((td, d), F32)],
        out_shape=jax.ShapeDtypeStruct((t, d), F32),
        name="combine",
    )(z, x2, mod, wt_rows, g_final)


def _retention_tables(length, nh, hd):
    c = math.gcd(length, RET_CHUNK)
    log_g = jnp.log(1.0 - 2.0 ** (-5.0 - jnp.arange(nh, dtype=F32)))
    idx = jnp.arange(c, dtype=F32)
    rel = idx[:, None] - idx[None, :]
    mask = jnp.where(rel >= 0, jnp.exp(log_g[:, None, None] * jnp.maximum(rel, 0.0)), 0.0)
    q_decay = jnp.exp(log_g[None, :] * (idx[:, None] + 1.0))
    k_decay = jnp.exp(log_g[None, :] * (c - 1.0 - idx[:, None]))
    chunk_decay = jnp.exp(log_g * c)
    qd = jnp.broadcast_to(q_decay.T[:, :, None], (nh, c, hd))
    kd = jnp.broadcast_to(k_decay.T[:, :, None], (nh, c, hd))
    cd = jnp.broadcast_to(chunk_decay[:, None, None], (nh, hd, hd))
    return mask, qd, kd, cd


def kernel(x_prompt, x_sample, c_prompt, c_sample, state_conv, state_ret, w_ada, b_ada, g_mix, g_ffn, w_in,
           conv_w, conv_b, conv_norm_g, conv_norm_b, ret_norm_g, ret_norm_b, w_out, w_router, router_bias,
           w1, w3, w2, ws1, ws3, ws2, g_final):
    depth = w_ada.shape[0]
    assert depth == 1, "single-layer trunk"
    bp, lp, d = x_prompt.shape
    bs, ls, _ = x_sample.shape
    assert ls == 1
    dc = conv_w.shape[2]
    dr = ret_norm_g.shape[1]
    nh = RET_HEADS
    hd = dr // nh
    assert hd == LANES and lp % 256 == 0 and bs % TOKEN_TILE == 0 and d // 2 == ROW_WORDS * LANES
    ne = w_router.shape[2]
    row = lambda a: a.reshape(1, -1)

    mod = _ada(jnp.concatenate([c_prompt, c_sample], axis=0), w_ada[0], row(b_ada[0]))
    mod_p = mod[:bp].reshape(bp, 6, 1, d)
    mod_s = mod[bp:]

    half = hd // 2
    inv = ROPE_BASE ** (-jnp.arange(half, dtype=F32) / half)
    inv2 = jnp.concatenate([inv, inv]).reshape(1, hd)
    cos_p, sin_p = _rope_tables(inv2, lp, 0)
    cos_s, sin_s = _rope_tables(inv2, SUBLANES, PAST_LEN)

    w_in_b = w_in[0].astype(BF16)
    wo_b = w_out[0].astype(BF16)
    wr_t = w_router[0].T
    wrh = wr_t.astype(BF16)
    wrl = (wr_t - wrh.astype(F32)).astype(BF16)
    ws1_b, ws3_b, ws2_b = ws1[0].astype(BF16), ws3[0].astype(BF16), ws2[0].astype(BF16)
    dims = dict(dc=dc, dr=dr, hd=hd)

    glu_p, q_p, k_p, v_p, sg_p = _proj(x_prompt, mod_p, row(g_mix[0]), w_in_b, cos_p, sin_p,
                                       per_row_mod=False, **dims)
    cat_p, ret_p = _mix(glu_p, q_p, k_p, v_p, sg_p, conv_w[0], row(conv_b[0]), row(conv_norm_g[0]),
                        row(conv_norm_b[0]), row(ret_norm_g[0]), row(ret_norm_b[0]),
                        _retention_tables(lp, nh, hd), nb=bp, length=lp, **dims)
    x2_p, hp_p, lg_p = _post(cat_p, x_prompt, mod_p, row(g_ffn[0]), wo_b, wrh, wrl, ws1_b, ws3_b, ws2_b,
                             per_row_mod=False)

    xs3 = x_sample.reshape(1, bs, d)
    glu_s, q_s, k_s, v_s, sg_s = _proj(xs3, mod_s, row(g_mix[0]), w_in_b, cos_s, sin_s,
                                       per_row_mod=True, **dims)
    log_g = jnp.log(1.0 - 2.0 ** (-5.0 - jnp.arange(nh, dtype=F32)))
    gam = jnp.broadcast_to(jnp.exp(log_g)[:, None, None], (nh, SUBLANES, hd))
    cat_s, ret_s = _mix1(glu_s, q_s, k_s, v_s, sg_s, state_conv[0], state_ret[0], conv_w[0], row(conv_b[0]),
                         row(conv_norm_g[0]), row(conv_norm_b[0]), row(ret_norm_g[0]), row(ret_norm_b[0]),
                         gam, **dims)
    x2_s, hp_s, lg_s = _post(cat_s, xs3, mod_s, row(g_ffn[0]), wo_b, wrh, wrl, ws1_b, ws3_b, ws2_b,
                             per_row_mod=True)

    tp = bp * lp
    eidx, wts, rank, cnt = _route(jnp.concatenate([lg_p, lg_s], axis=1), router_bias[0].reshape(ne, 1))
    counts = cnt[:, 0].astype(I32)
    tm = EXPERT_ROWS
    n_blk = (counts + tm - 1) // tm
    padded = n_blk * tm
    pstart = (jnp.cumsum(padded) - padded).astype(I32)
    n_assign = (tp + bs) * TOP_K
    n_blocks = -(-(n_assign + ne * (tm - 1)) // tm)
    wt_rows = wts.T
    tall = tp + bs
    n_rows = n_blocks * tm
    hw = d // 2
    dest3 = _dest_rows(pstart, eidx, rank).reshape(tall // TOKEN_TILE, SUBLANES, TOKEN_TILE)

    xs = _dispatch(hp_p.reshape(tp, hw), hp_s.reshape(bs, hw), dest3, n_rows)
    ys = _experts(pstart, n_blk, xs.reshape(n_rows * ROW_WORDS, LANES), w1[0], w3[0], w2[0])
    z = _undispatch(ys.reshape(n_rows, hw), dest3, tall).reshape(TOP_K, tall * ROW_WORDS, LANES)
    y_p = _combine(z, 0, wt_rows[:tp], x2_p, mod_p, row(g_final), rows_per_mod=lp, per_row_mod=False)
    y_s = _combine(z, tp // TOKEN_TILE, wt_rows[tp:], x2_s, mod_s, row(g_final), rows_per_mod=bs, per_row_mod=True)

    new_conv_p = glu_p.reshape(bp, lp, dc)[:, lp - CONV_BUF:, :]
    new_conv_s = jnp.concatenate([state_conv[0][:, 1:, :], glu_s[:, None, :]], axis=1)
    return (y_p.reshape(bp, lp, d), y_s.reshape(bs, ls, d), new_conv_p[None], ret_p[None],
            new_conv_s[None], ret_s[None])
```

```python
import functools
import math

import jax
import jax.numpy as jnp
from jax import lax
from jax.experimental import pallas as pl
from jax.experimental.pallas import tpu as pltpu
from jax.experimental.pallas import tpu_sc as plsc

F32 = jnp.float32
BF16 = jnp.bfloat16
U32 = jnp.uint32
I32 = jnp.int32

EPS = 1e-6
PAST_LEN = 16384
RET_HEADS = 4
RET_CHUNK = 128
CONV_WIDTH = 31
CONV_BUF = CONV_WIDTH - 1
ROPE_BASE = 10000.0
N_EXPERTS = 64
TOP_K = 6
N_GROUPS = 8
TOPK_GROUPS = 4
ROUTED_SCALE = 2.5

LANES = 128
SUBLANES = 8
CONV_PAD = 32
EXPERT_ROWS = 256
ROUTE_TILE = 384
TOKEN_TILE = 128
COMBINE_TILE = 256
VMEM_LIMIT = 56 * 1024 * 1024
SC_CORES = 2
SC_SUBCORES = 16
SC_WORKERS = SC_CORES * SC_SUBCORES
SCATTER_ROWS = 64
GATHER_ROWS = 32
ROW_WORDS = 4
X_SLOTS = 4
Y_SLOTS = 3

HI_MASK = 0xFFFF0000


def _sigmoid(x):
    return jax.nn.sigmoid(x)


def _silu(x):
    return x * jax.nn.sigmoid(x)


def _pack_halves(lo, hi):
    lo_u = lax.bitcast_convert_type(lo.astype(BF16).astype(F32), U32) >> 16
    hi_u = lax.bitcast_convert_type(hi.astype(BF16).astype(F32), U32) & jnp.uint32(HI_MASK)
    return hi_u | lo_u


def _unpack_halves(p):
    lo = lax.bitcast_convert_type(p << 16, F32)
    hi = lax.bitcast_convert_type(p & jnp.uint32(HI_MASK), F32)
    return lo, hi


def _store_rows(ref, x):
    rows = x.shape[0]
    for j in range(ROW_WORDS):
        ref[pl.ds(j, rows, stride=ROW_WORDS), :] = x[:, LANES * j:LANES * (j + 1)]


def _load_row_word(ref, j, rows):
    return ref[pl.ds(j, rows, stride=ROW_WORDS), :]


def _ada_body(c_ref, w_ref, b_ref, o_ref):
    s = _silu(c_ref[...]).astype(BF16)
    o_ref[...] = jnp.dot(s, w_ref[...].astype(BF16), preferred_element_type=F32) + b_ref[...]


def _ada(c_all, w_ada, b_ada):
    rows, d = c_all.shape
    n = w_ada.shape[1]
    tn = 512
    return pl.pallas_call(
        _ada_body,
        grid=(n // tn,),
        in_specs=[
            pl.BlockSpec((rows, d), lambda j: (0, 0)),
            pl.BlockSpec((d, tn), lambda j: (0, j)),
            pl.BlockSpec((1, tn), lambda j: (0, j)),
        ],
        out_specs=pl.BlockSpec((rows, tn), lambda j: (0, j)),
        out_shape=jax.ShapeDtypeStruct((rows, n), F32),
        name="ada",
    )(c_all, w_ada, b_ada)


def _rope_body(inv_ref, cos_ref, sin_ref, *, pos0, tl, half):
    row = lax.broadcasted_iota(I32, (tl, LANES), 0) + pl.program_id(0) * tl
    ang = (row.astype(F32) + pos0) * inv_ref[...]
    lane = lax.broadcasted_iota(I32, (tl, LANES), 1)
    s = jnp.sin(ang)
    cos_ref[...] = jnp.cos(ang)
    sin_ref[...] = jnp.where(lane < half, -s, s)


def _rope_tables(inv2, rows, pos0):
    tl = min(rows, 256)
    return pl.pallas_call(
        functools.partial(_rope_body, pos0=float(pos0), tl=tl, half=LANES // 2),
        grid=(rows // tl,),
        in_specs=[pl.BlockSpec((1, LANES), lambda i: (0, 0))],
        out_specs=[pl.BlockSpec((tl, LANES), lambda i: (i, 0))] * 2,
        out_shape=[jax.ShapeDtypeStruct((rows, LANES), F32)] * 2,
        name="rope",
    )(inv2)


def _modulated_rmsnorm(x, g, sc, sh):
    ms = jnp.mean(x * x, axis=-1, keepdims=True)
    h = x * lax.rsqrt(ms + EPS) * g
    return h * (1.0 + sc) + sh


def _proj_body(x_ref, sh_ref, sc_ref, g_ref, w_ref, cos_ref, sin_ref,
               glu_ref, q_ref, k_ref, v_ref, sg_ref, *, dc, dr, hd, rope_rows):
    hb = _modulated_rmsnorm(x_ref[...], g_ref[...], sc_ref[...], sh_ref[...]).astype(BF16)

    def proj(lo, n):
        return jnp.dot(hb, w_ref[:, lo:lo + n], preferred_element_type=F32)

    glu_ref[...] = proj(0, dc) * _sigmoid(proj(dc, dc))
    cos = cos_ref[...] if rope_rows else cos_ref[0:1, :]
    sin = sin_ref[...] if rope_rows else sin_ref[0:1, :]
    for ref, lo, scale in ((q_ref, 2 * dc, hd ** -0.5), (k_ref, 2 * dc + dr, None)):
        t = proj(lo, dr)
        for hh in range(dr // hd):
            th = t[:, hh * hd:(hh + 1) * hd]
            r = th * cos + pltpu.roll(th, hd // 2, 1) * sin
            if scale is not None:
                r = r * scale
            ref[:, hh * hd:(hh + 1) * hd] = r.astype(BF16)
    v_ref[...] = proj(2 * dc + 2 * dr, dr).astype(BF16)
    sg_ref[...] = _silu(proj(2 * dc + 3 * dr, dr)).astype(BF16)


def _proj(x, mod, g_mix, w_in_b, cos2, sin2, *, dc, dr, hd, per_row_mod):
    nb, length, d = x.shape
    tl = min(length, 256)
    x2 = x.reshape(nb * length, d)
    nl = length // tl
    if per_row_mod:
        mod_spec = lambda j: pl.BlockSpec((tl, d), lambda b, l: (l, j))
        rope_spec = pl.BlockSpec((SUBLANES, LANES), lambda b, l: (0, 0))
    else:
        mod_spec = lambda j: pl.BlockSpec((None, None, 1, d), lambda b, l: (b, j, 0, 0))
        rope_spec = pl.BlockSpec((tl, LANES), lambda b, l: (l, 0))
    row_spec = lambda w: pl.BlockSpec((tl, w), lambda b, l: (b * nl + l, 0))
    t = nb * length
    outs = pl.pallas_call(
        functools.partial(_proj_body, dc=dc, dr=dr, hd=hd, rope_rows=not per_row_mod),
        grid=(nb, nl),
        in_specs=[
            row_spec(d), mod_spec(0), mod_spec(1),
            pl.BlockSpec((1, d), lambda b, l: (0, 0)),
            pl.BlockSpec(w_in_b.shape, lambda b, l: (0, 0)),
            rope_spec, rope_spec,
        ],
        out_specs=[row_spec(dc), row_spec(dr), row_spec(dr), row_spec(dr), row_spec(dr)],
        out_shape=[
            jax.ShapeDtypeStruct((t, dc), F32),
            jax.ShapeDtypeStruct((t, dr), BF16),
            jax.ShapeDtypeStruct((t, dr), BF16),
            jax.ShapeDtypeStruct((t, dr), BF16),
            jax.ShapeDtypeStruct((t, dr), BF16),
        ],
        compiler_params=pltpu.CompilerParams(vmem_limit_bytes=VMEM_LIMIT),
        name="proj",
    )(x2, mod, mod, g_mix, w_in_b, cos2, sin2)
    return outs


def _layernorm_silu(c, g, b):
    mu = jnp.mean(c, axis=-1, keepdims=True)
    d = c - mu
    var = jnp.mean(d * d, axis=-1, keepdims=True)
    return _silu(d * lax.rsqrt(var + EPS) * g + b)


def _groupnorm(o, g, b):
    mu = jnp.mean(o, axis=-1, keepdims=True)
    d = o - mu
    var = jnp.mean(d * d, axis=-1, keepdims=True)
    return d * lax.rsqrt(var + EPS) * g + b


def _mix_body(glu_ref, q_ref, k_ref, v_ref, sg_ref, cw_ref, cb_ref, lng_ref, lnb_ref, rg_ref, rb_ref,
              mask_ref, qd_ref, kd_ref, cd_ref, cat_ref, st_ref, buf, cscr, *, tl, dc, hd, nh, chunk):
    nslab = dc // LANES

    @pl.when(pl.program_id(1) == 0)
    def _():
        buf[:, 0:CONV_PAD, :] = jnp.zeros((nslab, CONV_PAD, LANES), F32)
        st_ref[...] = jnp.zeros(st_ref.shape, F32)

    for j in range(nslab):
        buf[j, CONV_PAD:CONV_PAD + tl, :] = glu_ref[:, LANES * j:LANES * (j + 1)]
    first = CONV_PAD - CONV_BUF
    rows_per_iter = 4 * SUBLANES
    for j in range(nslab):
        cols = slice(LANES * j, LANES * (j + 1))
        wv = [jnp.broadcast_to(cw_ref[t:t + 1, cols], (SUBLANES, LANES)) for t in range(CONV_WIDTH)]
        bias = jnp.broadcast_to(cb_ref[0:1, cols], (SUBLANES, LANES))

        def body(r, carry, j=j, cols=cols, wv=wv, bias=bias):
            base = pl.multiple_of(r * rows_per_iter, rows_per_iter)
            for u in range(rows_per_iter // SUBLANES):
                acc = bias
                for t in range(CONV_WIDTH):
                    acc = acc + wv[t] * buf[j, pl.ds(base + (u * SUBLANES + first + t), SUBLANES), :]
                cscr[pl.ds(base + u * SUBLANES, SUBLANES), cols] = acc
            return carry

        lax.fori_loop(0, tl // rows_per_iter, body, 0)
    for j in range(nslab):
        buf[j, 0:CONV_PAD, :] = buf[j, tl:tl + CONV_PAD, :]
    cat_ref[:, 0:dc] = _layernorm_silu(cscr[...], lng_ref[...], lnb_ref[...]).astype(BF16)

    nt = (((1,), (1,)), ((), ()))
    tn = (((0,), (0,)), ((), ()))
    for c in range(tl // chunk):
        rows = slice(c * chunk, (c + 1) * chunk)
        for hh in range(nh):
            cols = slice(hh * hd, (hh + 1) * hd)
            qh = q_ref[rows, cols]
            kh = k_ref[rows, cols]
            vh = v_ref[rows, cols]
            s = st_ref[0, hh]
            scores = lax.dot_general(qh, kh, nt, preferred_element_type=F32) * mask_ref[hh]
            inner = jnp.dot(scores.astype(BF16), vh, preferred_element_type=F32)
            qd = (qh.astype(F32) * qd_ref[hh]).astype(BF16)
            cross = jnp.dot(qd, s.astype(BF16), preferred_element_type=F32)
            kd = (kh.astype(F32) * kd_ref[hh]).astype(BF16)
            st_ref[0, hh] = cd_ref[hh] * s + lax.dot_general(kd, vh, tn, preferred_element_type=F32)
            o = _groupnorm(inner + cross, rg_ref[0:1, cols], rb_ref[0:1, cols])
            cat_ref[rows, dc + hh * hd:dc + (hh + 1) * hd] = (o * sg_ref[rows, cols].astype(F32)).astype(BF16)


def _mix(glu, q, k, v, sg, conv_w, conv_b, ln_g, ln_b, rg, rb, tables, *, nb, length, dc, dr, hd):
    nh = dr // hd
    chunk = math.gcd(length, RET_CHUNK)
    tl = min(length, 256)
    nl = length // tl
    mask, qd, kd, cd = tables
    row_spec = lambda w: pl.BlockSpec((tl, w), lambda b, l: (b * nl + l, 0))
    full = lambda a: pl.BlockSpec(a.shape, lambda b, l: (0,) * a.ndim)
    cat, st = pl.pallas_call(
        functools.partial(_mix_body, tl=tl, dc=dc, hd=hd, nh=nh, chunk=chunk),
        grid=(nb, nl),
        in_specs=[row_spec(dc), row_spec(dr), row_spec(dr), row_spec(dr), row_spec(dr),
                  full(conv_w), full(conv_b), full(ln_g), full(ln_b), full(rg), full(rb),
                  full(mask), full(qd), full(kd), full(cd)],
        out_specs=[row_spec(dc + dr), pl.BlockSpec((1, nh, hd, hd), lambda b, l: (b, 0, 0, 0))],
        out_shape=[jax.ShapeDtypeStruct((nb * length, dc + dr), BF16),
                   jax.ShapeDtypeStruct((nb, nh, hd, hd), F32)],
        scratch_shapes=[pltpu.VMEM((dc // LANES, tl + CONV_PAD, LANES), F32),
                        pltpu.VMEM((tl, dc), F32)],
        compiler_params=pltpu.CompilerParams(dimension_semantics=("arbitrary", "arbitrary")),
        name="mix",
    )(glu, q, k, v, sg, conv_w, conv_b, ln_g, ln_b, rg, rb, mask, qd, kd, cd)
    return cat, st


def _mix1_body(glu_ref, q_ref, k_ref, v_ref, sg_ref, sc_ref, s0_ref, cw_ref, cb_ref, lng_ref, lnb_ref,
               rg_ref, rb_ref, gam_ref, cat_ref, st_ref, cscr, oscr, qf, kf, vf, *, tb, dc, hd, nh):
    w_hist = cw_ref[0:CONV_BUF, :]
    w_last = cw_ref[CONV_BUF:CONV_WIDTH, :]
    for bb in range(tb):
        hist = jnp.sum(sc_ref[bb] * w_hist, axis=0, keepdims=True)
        cscr[bb:bb + 1, :] = hist + glu_ref[bb:bb + 1, :] * w_last + cb_ref[...]
    cat_ref[:, 0:dc] = _layernorm_silu(cscr[...], lng_ref[...], lnb_ref[...]).astype(BF16)

    tn = (((0,), (0,)), ((), ()))
    rowid = lax.broadcasted_iota(I32, (tb, hd), 0)
    qf[...] = q_ref[...].astype(F32)
    kf[...] = k_ref[...].astype(F32)
    vf[...] = v_ref[...].astype(F32)
    for hh in range(nh):
        cols = slice(hh * hd, (hh + 1) * hd)
        qa = q_ref[:, cols]
        ka = k_ref[:, cols]
        gam = gam_ref[hh, 0:1, :]
        for bb in range(tb):
            onehot = (rowid == bb).astype(BF16)
            qcol = lax.dot_general(qa, onehot, tn, preferred_element_type=F32)
            kcol = lax.dot_general(ka, onehot, tn, preferred_element_type=F32)
            s0 = s0_ref[bb, hh]
            qrow = qf[bb:bb + 1, cols]
            krow = kf[bb:bb + 1, cols]
            vrow = vf[bb:bb + 1, cols]
            qk = jnp.sum(qrow * krow, axis=-1, keepdims=True)
            cross = gam * jnp.sum(qcol * s0, axis=0, keepdims=True)
            st_ref[bb, hh] = gam * s0 + kcol * vrow
            oscr[bb:bb + 1, cols] = qk * vrow + cross
    for hh in range(nh):
        cols = slice(hh * hd, (hh + 1) * hd)
        o = _groupnorm(oscr[:, cols], rg_ref[0:1, cols], rb_ref[0:1, cols])
        cat_ref[:, dc + hh * hd:dc + (hh + 1) * hd] = (o * sg_ref[:, cols].astype(F32)).astype(BF16)


def _mix1(glu, q, k, v, sg, state_conv, state_ret, conv_w, conv_b, ln_g, ln_b, rg, rb, gam, *, dc, dr, hd):
    nb = glu.shape[0]
    nh = dr // hd
    tb = 16
    row_spec = lambda w: pl.BlockSpec((tb, w), lambda i: (i, 0))
    full = lambda a: pl.BlockSpec(a.shape, lambda i: (0,) * a.ndim)
    st_spec = pl.BlockSpec((tb, nh, hd, hd), lambda i: (i, 0, 0, 0))
    cat, st = pl.pallas_call(
        functools.partial(_mix1_body, tb=tb, dc=dc, hd=hd, nh=nh),
        grid=(nb // tb,),
        in_specs=[row_spec(dc), row_spec(dr), row_spec(dr), row_spec(dr), row_spec(dr),
                  pl.BlockSpec((tb, CONV_BUF, dc), lambda i: (i, 0, 0)), st_spec,
                  full(conv_w), full(conv_b), full(ln_g), full(ln_b), full(rg), full(rb), full(gam)],
        out_specs=[row_spec(dc + dr), st_spec],
        out_shape=[jax.ShapeDtypeStruct((nb, dc + dr), BF16),
                   jax.ShapeDtypeStruct((nb, nh, hd, hd), F32)],
        scratch_shapes=[pltpu.VMEM((tb, dc), F32)] + [pltpu.VMEM((tb, dr), F32)] * 4,
        compiler_params=pltpu.CompilerParams(vmem_limit_bytes=VMEM_LIMIT),
        name="mix1",
    )(glu, q, k, v, sg, state_conv, state_ret, conv_w, conv_b, ln_g, ln_b, rg, rb, gam)
    return cat, st


def _post_body(cat_ref, x_ref, gtm_ref, scf_ref, shf_ref, gtf_ref, g_ref, wo_ref, wrh_ref, wrl_ref,
               ws1_ref, ws3_ref, ws2_ref, x2_ref, hp_ref, lg_ref):
    d = x_ref.shape[1]
    y = jnp.dot(cat_ref[...], wo_ref[...], preferred_element_type=F32)
    x1 = x_ref[...] + gtm_ref[...] * y
    h = _modulated_rmsnorm(x1, g_ref[...], scf_ref[...], shf_ref[...])
    hb = h.astype(BF16)
    _store_rows(hp_ref, _pack_halves(h[:, 0:d // 2], h[:, d // 2:d]))
    hl = (h - hb.astype(F32)).astype(BF16)
    nt = (((1,), (1,)), ((), ()))
    lg_ref[...] = (lax.dot_general(wrh_ref[...], hb, nt, preferred_element_type=F32)
                   + lax.dot_general(wrh_ref[...], hl, nt, preferred_element_type=F32)
                   + lax.dot_general(wrl_ref[...], hb, nt, preferred_element_type=F32))
    s1 = jnp.dot(hb, ws1_ref[...], preferred_element_type=F32)
    s3 = jnp.dot(hb, ws3_ref[...], preferred_element_type=F32)
    shared = jnp.dot((_silu(s1) * s3).astype(BF16), ws2_ref[...], preferred_element_type=F32)
    x2_ref[...] = x1 + gtf_ref[...] * shared


def _post(cat, x, mod, g_ffn, wo_b, wrh, wrl, ws1_b, ws3_b, ws2_b, *, per_row_mod):
    nb, length, d = x.shape
    tl = min(length, 512)
    nl = length // tl
    t = nb * length
    ne = wrh.shape[0]
    x2d = x.reshape(t, d)
    if per_row_mod:
        mod_spec = lambda j: pl.BlockSpec((tl, d), lambda b, l: (l, j))
    else:
        mod_spec = lambda j: pl.BlockSpec((None, None, 1, d), lambda b, l: (b, j, 0, 0))
    row_spec = lambda w: pl.BlockSpec((tl, w), lambda b, l: (b * nl + l, 0))
    full = lambda a: pl.BlockSpec(a.shape, lambda b, l: (0,) * a.ndim)
    return pl.pallas_call(
        _post_body,
        grid=(nb, nl),
        in_specs=[row_spec(d), row_spec(d), mod_spec(2), mod_spec(4), mod_spec(3), mod_spec(5),
                  full(g_ffn), full(wo_b), full(wrh), full(wrl), full(ws1_b), full(ws3_b), full(ws2_b)],
        out_specs=[row_spec(d),
                   pl.BlockSpec((tl * ROW_WORDS, LANES), lambda b, l: (b * nl + l, 0)),
                   pl.BlockSpec((ne, tl), lambda b, l: (0, b * nl + l))],
        out_shape=[jax.ShapeDtypeStruct((t, d), F32),
                   jax.ShapeDtypeStruct((t * ROW_WORDS, LANES), U32),
                   jax.ShapeDtypeStruct((ne, t), F32)],
        compiler_params=pltpu.CompilerParams(vmem_limit_bytes=VMEM_LIMIT),
        name="post",
    )(cat, x2d, mod, mod, mod, mod, g_ffn, wo_b, wrh, wrl, ws1_b, ws3_b, ws2_b)


def _first_max(x, idx, sentinel):
    m = jnp.max(x, axis=0, keepdims=True)
    f = jnp.min(jnp.where(x == m, idx, sentinel), axis=0, keepdims=True)
    return m, f


def _route_body(lg_ref, bias_ref, e_ref, w_ref, r_ref, cnt_ref, cnt_scr, *, tr, ne, ng, topk, topg):
    @pl.when(pl.program_id(0) == 0)
    def _():
        cnt_scr[...] = jnp.zeros(cnt_scr.shape, F32)

    per = ne // ng
    neg = -jnp.inf
    scores = _sigmoid(lg_ref[...])
    sel = scores + bias_ref[...]
    sub = lax.broadcasted_iota(I32, (per, tr), 0)
    gs = []
    for g in range(ng):
        s_g = sel[g * per:(g + 1) * per, :]
        m1, f1 = _first_max(s_g, sub, per)
        m2 = jnp.max(jnp.where(sub == f1, neg, s_g), axis=0, keepdims=True)
        gs.append(m1 + m2)
    gsc = jnp.concatenate(gs, axis=0)
    gi = lax.broadcasted_iota(I32, (ng, tr), 0)
    keep = jnp.zeros((ng, tr), F32)
    for _ in range(topg):
        _, f = _first_max(gsc, gi, ng)
        pick = gi == f
        keep = jnp.where(pick, 1.0, keep)
        gsc = jnp.where(pick, neg, gsc)
    work = jnp.concatenate(
        [jnp.where(keep[g:g + 1, :] > 0.5, sel[g * per:(g + 1) * per, :], neg) for g in range(ng)], axis=0)
    ei = lax.broadcasted_iota(I32, (ne, tr), 0)
    picks, es, ws = [], [], []
    for _ in range(topk):
        _, f = _first_max(work, ei, ne)
        pick = ei == f
        picks.append(pick)
        es.append(f)
        ws.append(jnp.sum(jnp.where(pick, scores, 0.0), axis=0, keepdims=True))
        work = jnp.where(pick, neg, work)
    wsum = ws[0]
    for w in ws[1:]:
        wsum = wsum + w
    scale = ROUTED_SCALE / wsum
    chosen = picks[0]
    for p in picks[1:]:
        chosen = jnp.logical_or(chosen, p)
    chosen_f = chosen.astype(F32)
    t_row = lax.broadcasted_iota(I32, (tr, tr), 0)
    t_col = lax.broadcasted_iota(I32, (tr, tr), 1)
    before = (t_row < t_col).astype(BF16)
    prior = cnt_scr[:, 0:1] + jnp.dot(chosen_f.astype(BF16), before, preferred_element_type=F32)
    rs = [jnp.sum(jnp.where(p, prior, 0.0), axis=0, keepdims=True).astype(I32) for p in picks]
    pad_i = jnp.zeros((SUBLANES - topk, tr), I32)
    pad_f = jnp.zeros((SUBLANES - topk, tr), F32)
    e_ref[...] = jnp.concatenate(es + [pad_i], axis=0)
    w_ref[...] = jnp.concatenate([w * scale for w in ws] + [pad_f], axis=0)
    r_ref[...] = jnp.concatenate(rs + [pad_i], axis=0)
    total = cnt_scr[:, 0:1] + jnp.sum(chosen_f, axis=1, keepdims=True)
    cnt_scr[...] = jnp.broadcast_to(total, cnt_scr.shape)
    cnt_ref[...] = jnp.broadcast_to(total, cnt_ref.shape)


def _route(logits_t, bias_col):
    ne, t = logits_t.shape
    tr = ROUTE_TILE if t % ROUTE_TILE == 0 else LANES
    tok = lambda dt: jax.ShapeDtypeStruct((SUBLANES, t), dt)
    tok_spec = pl.BlockSpec((SUBLANES, tr), lambda i: (0, i))
    return pl.pallas_call(
        functools.partial(_route_body, tr=tr, ne=ne, ng=N_GROUPS, topk=TOP_K, topg=TOPK_GROUPS),
        grid=(t // tr,),
        in_specs=[pl.BlockSpec((ne, tr), lambda i: (0, i)), pl.BlockSpec((ne, 1), lambda i: (0, 0))],
        out_specs=[tok_spec, tok_spec, tok_spec, pl.BlockSpec((ne, LANES), lambda i: (0, 0))],
        out_shape=[tok(I32), tok(F32), tok(I32), jax.ShapeDtypeStruct((ne, LANES), F32)],
        scratch_shapes=[pltpu.VMEM((ne, LANES), F32)],
        compiler_params=pltpu.CompilerParams(dimension_semantics=("arbitrary",)),
        name="route",
    )(logits_t, bias_col)


def _dest_body(ps_ref, e_ref, r_ref, d_ref, *, ne, tr):
    e = e_ref[...]
    base = jnp.zeros(e.shape, I32)
    for j in range(ne):
        base = jnp.where(e == j, ps_ref[j], base)
    dest = base + r_ref[...]
    for m in range(tr // TOKEN_TILE):
        d_ref[SUBLANES * m:SUBLANES * (m + 1), :] = dest[:, TOKEN_TILE * m:TOKEN_TILE * (m + 1)]


def _dest_rows(pstart, eidx, rank):
    rows, t = eidx.shape
    tr = ROUTE_TILE if t % ROUTE_TILE == 0 else LANES
    spec = pl.BlockSpec((rows, tr), lambda i, ps: (0, i))
    return pl.pallas_call(
        functools.partial(_dest_body, ne=pstart.shape[0], tr=tr),
        grid_spec=pltpu.PrefetchScalarGridSpec(
            num_scalar_prefetch=1, grid=(t // tr,), in_specs=[spec, spec],
            out_specs=pl.BlockSpec((tr // TOKEN_TILE * rows, TOKEN_TILE), lambda i, ps: (i, 0))),
        out_shape=jax.ShapeDtypeStruct((t // TOKEN_TILE * rows, TOKEN_TILE), I32),
        name="dest",
    )(pstart, eidx, rank)


def _sc_mesh():
    return plsc.VectorSubcoreMesh(core_axis_name="c", subcore_axis_name="s")


def _sc_worker_id():
    return lax.axis_index("s") * SC_CORES + lax.axis_index("c")


def _index_block(dest_ref, chunk, width):
    per_tile = TOKEN_TILE // width
    return dest_ref.at[chunk // per_tile, :, pl.ds((chunk % per_tile) * width, width)]


def _dispatch(hp_p, hp_s, dest3, n_rows):
    w = SCATTER_ROWS
    width = hp_p.shape[1]
    ncp = hp_p.shape[0] // w
    nch = ncp + hp_s.shape[0] // w

    @functools.partial(
        pl.kernel, mesh=_sc_mesh(),
        out_type=jax.ShapeDtypeStruct((n_rows, width), U32),
        scratch_types=[pltpu.VMEM((SUBLANES, w), I32), pltpu.VMEM((w, width), U32), pltpu.SemaphoreType.DMA],
        compiler_params=pltpu.CompilerParams(use_tc_tiling_on_sc=False),
        name="dispatch",
    )
    def run(hpp_ref, hps_ref, dest_ref, xs_ref, idx_v, rows_v, sem):
        wid = _sc_worker_id()

        @pl.loop(0, pl.cdiv(nch, SC_WORKERS))
        def _(r):
            c = r * SC_WORKERS + wid

            @pl.when(c < nch)
            def _():
                pltpu.sync_copy(_index_block(dest_ref, c, w), idx_v)

                @pl.when(c < ncp)
                def _():
                    pltpu.sync_copy(hpp_ref.at[pl.ds(c * w, w)], rows_v)

                @pl.when(c >= ncp)
                def _():
                    pltpu.sync_copy(hps_ref.at[pl.ds((c - ncp) * w, w)], rows_v)

                copies = [pltpu.async_copy(rows_v, xs_ref.at[idx_v.at[k]], sem) for k in range(TOP_K)]
                for cp in copies:
                    cp.wait()

    return run(hp_p, hp_s, dest3)


def _undispatch(ys, dest3, n_tokens):
    w = GATHER_ROWS
    width = ys.shape[1]
    nch = n_tokens // w

    @functools.partial(
        pl.kernel, mesh=_sc_mesh(),
        out_type=jax.ShapeDtypeStruct((TOP_K, n_tokens, width), U32),
        scratch_types=[pltpu.VMEM((SUBLANES, w), I32), pltpu.VMEM((TOP_K, w, width), U32),
                       pltpu.SemaphoreType.DMA],
        compiler_params=pltpu.CompilerParams(use_tc_tiling_on_sc=False),
        name="undispatch",
    )
    def run(ys_ref, dest_ref, z_ref, idx_v, bufs, sem):
        wid = _sc_worker_id()

        @pl.loop(0, pl.cdiv(nch, SC_WORKERS))
        def _(r):
            c = r * SC_WORKERS + wid

            @pl.when(c < nch)
            def _():
                pltpu.sync_copy(_index_block(dest_ref, c, w), idx_v)
                gathers = [pltpu.async_copy(ys_ref.at[idx_v.at[k]], bufs.at[k], sem) for k in range(TOP_K)]
                for cp in gathers:
                    cp.wait()
                stores = [pltpu.async_copy(bufs.at[k], z_ref.at[k, pl.ds(c * w, w)], sem) for k in range(TOP_K)]
                for cp in stores:
                    cp.wait()

    return run(ys, dest3)


def _expert_schedule(counts, n_blocks):
    tm = EXPERT_ROWS
    ne = counts.shape[0]
    n_blk = (counts + tm - 1) // tm
    pend_blk = jnp.cumsum(n_blk)
    pstart = ((pend_blk - n_blk) * tm).astype(I32)
    n_used = pend_blk[-1].astype(I32)
    blk = jnp.arange(n_blocks, dtype=I32)
    blk_e = jnp.minimum(jnp.sum((pend_blk[None, :] <= blk[:, None]).astype(I32), axis=1), ne - 1)
    prev_e = jnp.concatenate([jnp.full((1,), -1, I32), blk_e[:-1]])
    first = (blk_e != prev_e).astype(I32)
    wslot = ((jnp.cumsum(first) - 1) % 2).astype(I32)
    nxt_blk = pend_blk[blk_e].astype(I32)
    has_next = (nxt_blk < n_used).astype(I32)
    nxt_e = blk_e[jnp.minimum(nxt_blk, n_blocks - 1)]
    return pstart, (blk_e, first, wslot, nxt_e, has_next, n_used.reshape(1))


def _expert_body(be_ref, first_ref, wslot_ref, nxt_ref, hasn_ref, nu_ref, xs_ref, w1_ref, w3_ref, w2_ref, ys_ref,
                 w1f, w3f, w2f, w1s, w3s, w2s, xbuf, ybuf, xlo, xhi, sem_x, sem_y, sem_w, *, tm):
    nu = nu_ref[0]
    blk_words = tm * ROW_WORDS
    half = ROW_WORDS * LANES

    def x_copy(i, slot):
        src = xs_ref.at[pl.ds(pl.multiple_of(i * blk_words, blk_words), blk_words), :]
        return pltpu.make_async_copy(src, xbuf.at[slot], sem_x.at[slot])

    def y_copy(i, slot):
        dst = ys_ref.at[pl.ds(pl.multiple_of(i * blk_words, blk_words), blk_words), :]
        return pltpu.make_async_copy(ybuf.at[slot], dst, sem_y.at[slot])

    def w_copies(e, ws):
        return [pltpu.make_async_copy(src.at[e], dst.at[ws], sem_w.at[ws])
                for src, dst in ((w1_ref, w1f), (w3_ref, w3f), (w2_ref, w2f))]

    for p in range(X_SLOTS - 1):
        @pl.when(p < nu)
        def _(p=p):
            x_copy(p, p).start()

    @pl.when(nu > 0)
    def _():
        for cp in w_copies(be_ref[0], 0):
            cp.start()

    def block(i, carry):
        slot = i % X_SLOTS
        yslot = i % Y_SLOTS
        ws = wslot_ref[i]

        @pl.when(first_ref[i] == 1)
        def _():
            for cp in w_copies(be_ref[i], ws):
                cp.wait()
            w1s[...] = w1f[ws].astype(BF16)
            w3s[...] = w3f[ws].astype(BF16)
            w2s[...] = w2f[ws].astype(BF16)

            @pl.when(hasn_ref[i] == 1)
            def _():
                for cp in w_copies(nxt_ref[i], 1 - ws):
                    cp.start()

        x_copy(i, slot).wait()

        @pl.when(i + (X_SLOTS - 1) < nu)
        def _():
            x_copy(i + (X_SLOTS - 1), (i + (X_SLOTS - 1)) % X_SLOTS).start()

        @pl.when(i >= Y_SLOTS)
        def _():
            y_copy(i - Y_SLOTS, yslot).wait()

        xin = xbuf.at[slot]
        for w in range(ROW_WORDS):
            lo, hi = _unpack_halves(_load_row_word(xin, w, tm))
            xlo[:, LANES * w:LANES * (w + 1)] = lo.astype(BF16)
            xhi[:, LANES * w:LANES * (w + 1)] = hi.astype(BF16)

        def up(wsc):
            return (jnp.dot(xlo[...], wsc[0:half, :], preferred_element_type=F32)
                    + jnp.dot(xhi[...], wsc[half:2 * half, :], preferred_element_type=F32))

        hid = (_silu(up(w1s)) * up(w3s)).astype(BF16)
        y = jnp.dot(hid, w2s[...], preferred_element_type=F32)
        _store_rows(ybuf.at[yslot], _pack_halves(y[:, 0:half], y[:, half:2 * half]))
        y_copy(i, yslot).start()
        return carry

    lax.fori_loop(0, nu, block, 0)

    for q in range(Y_SLOTS):
        @pl.when(nu > q)
        def _(q=q):
            y_copy(nu - 1 - q, (nu - 1 - q) % Y_SLOTS).wait()


def _experts(sched, xs, w1e, w3e, w2e):
    tm = EXPERT_ROWS
    ne, d, de = w1e.shape
    half = ROW_WORDS * LANES
    anyspec = pl.BlockSpec(memory_space=pl.ANY)
    blk_buf = lambda n: pltpu.VMEM((n, tm * ROW_WORDS, LANES), U32)
    return pl.pallas_call(
        functools.partial(_expert_body, tm=tm),
        grid_spec=pltpu.PrefetchScalarGridSpec(
            num_scalar_prefetch=len(sched),
            grid=(1,),
            in_specs=[anyspec, anyspec, anyspec, anyspec],
            out_specs=anyspec,
            scratch_shapes=[pltpu.VMEM((2, d, de), F32), pltpu.VMEM((2, d, de), F32), pltpu.VMEM((2, de, d), F32),
                            pltpu.VMEM((d, de), BF16), pltpu.VMEM((d, de), BF16), pltpu.VMEM((de, d), BF16),
                            blk_buf(X_SLOTS), blk_buf(Y_SLOTS),
                            pltpu.VMEM((tm, half), BF16), pltpu.VMEM((tm, half), BF16),
                            pltpu.SemaphoreType.DMA((X_SLOTS,)), pltpu.SemaphoreType.DMA((Y_SLOTS,)),
                            pltpu.SemaphoreType.DMA((2,))]),
        out_shape=jax.ShapeDtypeStruct(xs.shape, U32),
        compiler_params=pltpu.CompilerParams(dimension_semantics=("arbitrary",), vmem_limit_bytes=VMEM_LIMIT),
        name="experts",
    )(*sched, xs, w1e, w3e, w2e)


def _combine_body(z_ref, x2_ref, gtf_ref, wt_ref, gfin_ref, y_ref, xo, *, td, topk):
    half = ROW_WORDS * LANES
    ws = [wt_ref[:, k:k + 1] for k in range(topk)]
    sq = jnp.zeros((td, 1), F32)
    for j in range(ROW_WORDS):
        acc_lo = jnp.zeros((td, LANES), F32)
        acc_hi = jnp.zeros((td, LANES), F32)
        for k in range(topk):
            lo, hi = _unpack_halves(_load_row_word(z_ref.at[k], j, td))
            acc_lo = acc_lo + ws[k] * lo
            acc_hi = acc_hi + ws[k] * hi
        for base, acc in ((0, acc_lo), (half, acc_hi)):
            cols = slice(base + LANES * j, base + LANES * (j + 1))
            x = x2_ref[:, cols] + gtf_ref[:, cols] * acc
            xo[:, cols] = x
            sq = sq + jnp.sum(x * x, axis=-1, keepdims=True)
    rs = lax.rsqrt(sq / (2 * half) + EPS)
    y_ref[...] = xo[...] * rs * gfin_ref[...]


def _combine(z, token0, wt_rows, x2, mod, g_final, *, rows_per_mod, per_row_mod):
    t, d = x2.shape
    td = min(t, COMBINE_TILE)
    tile0 = token0 // td
    if per_row_mod:
        gtf_spec = pl.BlockSpec((td, d), lambda i: (i, 5))
    else:
        tiles_per_mod = rows_per_mod // td
        gtf_spec = pl.BlockSpec((None, None, 1, d), lambda i: (i // tiles_per_mod, 5, 0, 0))
    return pl.pallas_call(
        functools.partial(_combine_body, td=td, topk=TOP_K),
        grid=(t // td,),
        in_specs=[pl.BlockSpec((TOP_K, td * ROW_WORDS, LANES), lambda i: (0, i + tile0, 0)),
                  pl.BlockSpec((td, d), lambda i: (i, 0)),
                  gtf_spec,
                  pl.BlockSpec((td, SUBLANES), lambda i: (i, 0)),
                  pl.BlockSpec((1, d), lambda i: (0, 0))],
        out_specs=pl.BlockSpec((td, d), lambda i: (i, 0)),
        scratch_shapes=[pltpu.VMEM((td, d), F32)],
        out_shape=jax.ShapeDtypeStruct((t, d), F32),
        name="combine",
    )(z, x2, mod, wt_rows, g_final)


def _retention_tables(length, nh, hd):
    c = math.gcd(length, RET_CHUNK)
    log_g = jnp.log(1.0 - 2.0 ** (-5.0 - jnp.arange(nh, dtype=F32)))
    idx = jnp.arange(c, dtype=F32)
    rel = idx[:, None] - idx[None, :]
    mask = jnp.where(rel >= 0, jnp.exp(log_g[:, None, None] * jnp.maximum(rel, 0.0)), 0.0)
    q_decay = jnp.exp(log_g[None, :] * (idx[:, None] + 1.0))
    k_decay = jnp.exp(log_g[None, :] * (c - 1.0 - idx[:, None]))
    chunk_decay = jnp.exp(log_g * c)
    qd = jnp.broadcast_to(q_decay.T[:, :, None], (nh, c, hd))
    kd = jnp.broadcast_to(k_decay.T[:, :, None], (nh, c, hd))
    cd = jnp.broadcast_to(chunk_decay[:, None, None], (nh, hd, hd))
    return mask, qd, kd, cd


def kernel(x_prompt, x_sample, c_prompt, c_sample, state_conv, state_ret, w_ada, b_ada, g_mix, g_ffn, w_in,
           conv_w, conv_b, conv_norm_g, conv_norm_b, ret_norm_g, ret_norm_b, w_out, w_router, router_bias,
           w1, w3, w2, ws1, ws3, ws2, g_final):
    depth = w_ada.shape[0]
    assert depth == 1, "single-layer trunk"
    bp, lp, d = x_prompt.shape
    bs, ls, _ = x_sample.shape
    assert ls == 1
    dc = conv_w.shape[2]
    dr = ret_norm_g.shape[1]
    nh = RET_HEADS
    hd = dr // nh
    assert hd == LANES and lp % 256 == 0 and bs % TOKEN_TILE == 0 and d // 2 == ROW_WORDS * LANES
    ne = w_router.shape[2]
    row = lambda a: a.reshape(1, -1)

    mod = _ada(jnp.concatenate([c_prompt, c_sample], axis=0), w_ada[0], row(b_ada[0]))
    mod_p = mod[:bp].reshape(bp, 6, 1, d)
    mod_s = mod[bp:]

    half = hd // 2
    inv = ROPE_BASE ** (-jnp.arange(half, dtype=F32) / half)
    inv2 = jnp.concatenate([inv, inv]).reshape(1, hd)
    cos_p, sin_p = _rope_tables(inv2, lp, 0)
    cos_s, sin_s = _rope_tables(inv2, SUBLANES, PAST_LEN)

    w_in_b = w_in[0].astype(BF16)
    wo_b = w_out[0].astype(BF16)
    wr_t = w_router[0].T
    wrh = wr_t.astype(BF16)
    wrl = (wr_t - wrh.astype(F32)).astype(BF16)
    ws1_b, ws3_b, ws2_b = ws1[0].astype(BF16), ws3[0].astype(BF16), ws2[0].astype(BF16)
    dims = dict(dc=dc, dr=dr, hd=hd)

    glu_p, q_p, k_p, v_p, sg_p = _proj(x_prompt, mod_p, row(g_mix[0]), w_in_b, cos_p, sin_p,
                                       per_row_mod=False, **dims)
    cat_p, ret_p = _mix(glu_p, q_p, k_p, v_p, sg_p, conv_w[0], row(conv_b[0]), row(conv_norm_g[0]),
                        row(conv_norm_b[0]), row(ret_norm_g[0]), row(ret_norm_b[0]),
                        _retention_tables(lp, nh, hd), nb=bp, length=lp, **dims)
    x2_p, hp_p, lg_p = _post(cat_p, x_prompt, mod_p, row(g_ffn[0]), wo_b, wrh, wrl, ws1_b, ws3_b, ws2_b,
                             per_row_mod=False)

    xs3 = x_sample.reshape(1, bs, d)
    glu_s, q_s, k_s, v_s, sg_s = _proj(xs3, mod_s, row(g_mix[0]), w_in_b, cos_s, sin_s,
                                       per_row_mod=True, **dims)
    log_g = jnp.log(1.0 - 2.0 ** (-5.0 - jnp.arange(nh, dtype=F32)))
    gam = jnp.broadcast_to(jnp.exp(log_g)[:, None, None], (nh, SUBLANES, hd))
    cat_s, ret_s = _mix1(glu_s, q_s, k_s, v_s, sg_s, state_conv[0], state_ret[0], conv_w[0], row(conv_b[0]),
                         row(conv_norm_g[0]), row(conv_norm_b[0]), row(ret_norm_g[0]), row(ret_norm_b[0]),
                         gam, **dims)
    x2_s, hp_s, lg_s = _post(cat_s, xs3, mod_s, row(g_ffn[0]), wo_b, wrh, wrl, ws1_b, ws3_b, ws2_b,
                             per_row_mod=True)

    tp = bp * lp
    eidx, wts, rank, cnt = _route(jnp.concatenate([lg_p, lg_s], axis=1), router_bias[0].reshape(ne, 1))
    counts = cnt[:, 0].astype(I32)
    tm = EXPERT_ROWS
    n_assign = (tp + bs) * TOP_K
    n_blocks = -(-(n_assign + ne * (tm - 1)) // tm)
    pstart, sched = _expert_schedule(counts, n_blocks)
    wt_rows = wts.T
    tall = tp + bs
    n_rows = n_blocks * tm
    hw = d // 2
    dest3 = _dest_rows(pstart, eidx, rank).reshape(tall // TOKEN_TILE, SUBLANES, TOKEN_TILE)

    xs = _dispatch(hp_p.reshape(tp, hw), hp_s.reshape(bs, hw), dest3, n_rows)
    ys = _experts(sched, xs.reshape(n_rows * ROW_WORDS, LANES), w1[0], w3[0], w2[0])
    z = _undispatch(ys.reshape(n_rows, hw), dest3, tall).reshape(TOP_K, tall * ROW_WORDS, LANES)
    y_p = _combine(z, 0, wt_rows[:tp], x2_p, mod_p, row(g_final), rows_per_mod=lp, per_row_mod=False)
    y_s = _combine(z, tp, wt_rows[tp:], x2_s, mod_s, row(g_final), rows_per_mod=bs, per_row_mod=True)

    new_conv_p = glu_p.reshape(bp, lp, dc)[:, lp - CONV_BUF:, :]
    new_conv_s = jnp.concatenate([state_conv[0][:, 1:, :], glu_s[:, None, :]], axis=1)
    return (y_p.reshape(bp, lp, d), y_s.reshape(bs, ls, d), new_conv_p[None], ret_p[None],
            new_conv_s[None], ret_s[None])
```

```python
import functools
import math

import jax
import jax.numpy as jnp
from jax import lax
from jax.experimental import pallas as pl
from jax.experimental.pallas import tpu as pltpu
from jax.experimental.pallas import tpu_sc as plsc

F32 = jnp.float32
BF16 = jnp.bfloat16
U32 = jnp.uint32
I32 = jnp.int32

EPS = 1e-6
PAST_LEN = 16384
RET_HEADS = 4
RET_CHUNK = 128
CONV_WIDTH = 31
CONV_BUF = CONV_WIDTH - 1
ROPE_BASE = 10000.0
N_EXPERTS = 64
TOP_K = 6
N_GROUPS = 8
TOPK_GROUPS = 4
ROUTED_SCALE = 2.5

LANES = 128
SUBLANES = 8
CONV_PAD = 32
EXPERT_ROWS = 256
ROUTE_TILE = 384
TOKEN_TILE = 128
COMBINE_TILE = 256
VMEM_LIMIT = 56 * 1024 * 1024
SC_CORES = 2
SC_SUBCORES = 16
SC_WORKERS = SC_CORES * SC_SUBCORES
SCATTER_ROWS = 64
GATHER_ROWS = 32
ROW_WORDS = 4
X_SLOTS = 4
Y_SLOTS = 3

HI_MASK = 0xFFFF0000


def _sigmoid(x):
    return jax.nn.sigmoid(x)


def _silu(x):
    return x * jax.nn.sigmoid(x)


def _pack_halves(lo, hi):
    lo_u = lax.bitcast_convert_type(lo.astype(BF16).astype(F32), U32) >> 16
    hi_u = lax.bitcast_convert_type(hi.astype(BF16).astype(F32), U32) & jnp.uint32(HI_MASK)
    return hi_u | lo_u


def _unpack_halves(p):
    lo = lax.bitcast_convert_type(p << 16, F32)
    hi = lax.bitcast_convert_type(p & jnp.uint32(HI_MASK), F32)
    return lo, hi


def _store_rows(ref, x):
    rows = x.shape[0]
    for j in range(ROW_WORDS):
        ref[pl.ds(j, rows, stride=ROW_WORDS), :] = x[:, LANES * j:LANES * (j + 1)]


def _load_row_word(ref, j, rows):
    return ref[pl.ds(j, rows, stride=ROW_WORDS), :]


def _ada_body(c_ref, w_ref, b_ref, o_ref):
    s = _silu(c_ref[...]).astype(BF16)
    o_ref[...] = jnp.dot(s, w_ref[...].astype(BF16), preferred_element_type=F32) + b_ref[...]


def _ada(c_all, w_ada, b_ada):
    rows, d = c_all.shape
    n = w_ada.shape[1]
    tn = 512
    return pl.pallas_call(
        _ada_body,
        grid=(n // tn,),
        in_specs=[
            pl.BlockSpec((rows, d), lambda j: (0, 0)),
            pl.BlockSpec((d, tn), lambda j: (0, j)),
            pl.BlockSpec((1, tn), lambda j: (0, j)),
        ],
        out_specs=pl.BlockSpec((rows, tn), lambda j: (0, j)),
        out_shape=jax.ShapeDtypeStruct((rows, n), F32),
        name="ada",
    )(c_all, w_ada, b_ada)


def _rope_body(inv_ref, cos_ref, sin_ref, *, pos0, tl, half):
    row = lax.broadcasted_iota(I32, (tl, LANES), 0) + pl.program_id(0) * tl
    ang = (row.astype(F32) + pos0) * inv_ref[...]
    lane = lax.broadcasted_iota(I32, (tl, LANES), 1)
    s = jnp.sin(ang)
    cos_ref[...] = jnp.cos(ang)
    sin_ref[...] = jnp.where(lane < half, -s, s)


def _rope_tables(inv2, rows, pos0):
    tl = min(rows, 256)
    return pl.pallas_call(
        functools.partial(_rope_body, pos0=float(pos0), tl=tl, half=LANES // 2),
        grid=(rows // tl,),
        in_specs=[pl.BlockSpec((1, LANES), lambda i: (0, 0))],
        out_specs=[pl.BlockSpec((tl, LANES), lambda i: (i, 0))] * 2,
        out_shape=[jax.ShapeDtypeStruct((rows, LANES), F32)] * 2,
        name="rope",
    )(inv2)


def _modulated_rmsnorm(x, g, sc, sh):
    ms = jnp.mean(x * x, axis=-1, keepdims=True)
    h = x * lax.rsqrt(ms + EPS) * g
    return h * (1.0 + sc) + sh


def _proj_body(x_ref, sh_ref, sc_ref, g_ref, w_ref, cos_ref, sin_ref,
               glu_ref, q_ref, k_ref, v_ref, sg_ref, *, dc, dr, hd, rope_rows):
    hb = _modulated_rmsnorm(x_ref[...], g_ref[...], sc_ref[...], sh_ref[...]).astype(BF16)

    def proj(lo, n):
        return jnp.dot(hb, w_ref[:, lo:lo + n], preferred_element_type=F32)

    glu_ref[...] = proj(0, dc) * _sigmoid(proj(dc, dc))
    cos = cos_ref[...] if rope_rows else cos_ref[0:1, :]
    sin = sin_ref[...] if rope_rows else sin_ref[0:1, :]
    for ref, lo, scale in ((q_ref, 2 * dc, hd ** -0.5), (k_ref, 2 * dc + dr, None)):
        t = proj(lo, dr)
        for hh in range(dr // hd):
            th = t[:, hh * hd:(hh + 1) * hd]
            r = th * cos + pltpu.roll(th, hd // 2, 1) * sin
            if scale is not None:
                r = r * scale
            ref[:, hh * hd:(hh + 1) * hd] = r.astype(BF16)
    v_ref[...] = proj(2 * dc + 2 * dr, dr).astype(BF16)
    sg_ref[...] = _silu(proj(2 * dc + 3 * dr, dr)).astype(BF16)


def _proj(x, mod, g_mix, w_in_b, cos2, sin2, *, dc, dr, hd, per_row_mod):
    nb, length, d = x.shape
    tl = min(length, 256)
    x2 = x.reshape(nb * length, d)
    nl = length // tl
    if per_row_mod:
        mod_spec = lambda j: pl.BlockSpec((tl, d), lambda b, l: (l, j))
        rope_spec = pl.BlockSpec((SUBLANES, LANES), lambda b, l: (0, 0))
    else:
        mod_spec = lambda j: pl.BlockSpec((None, None, 1, d), lambda b, l: (b, j, 0, 0))
        rope_spec = pl.BlockSpec((tl, LANES), lambda b, l: (l, 0))
    row_spec = lambda w: pl.BlockSpec((tl, w), lambda b, l: (b * nl + l, 0))
    t = nb * length
    outs = pl.pallas_call(
        functools.partial(_proj_body, dc=dc, dr=dr, hd=hd, rope_rows=not per_row_mod),
        grid=(nb, nl),
        in_specs=[
            row_spec(d), mod_spec(0), mod_spec(1),
            pl.BlockSpec((1, d), lambda b, l: (0, 0)),
            pl.BlockSpec(w_in_b.shape, lambda b, l: (0, 0)),
            rope_spec, rope_spec,
        ],
        out_specs=[row_spec(dc), row_spec(dr), row_spec(dr), row_spec(dr), row_spec(dr)],
        out_shape=[
            jax.ShapeDtypeStruct((t, dc), F32),
            jax.ShapeDtypeStruct((t, dr), BF16),
            jax.ShapeDtypeStruct((t, dr), BF16),
            jax.ShapeDtypeStruct((t, dr), BF16),
            jax.ShapeDtypeStruct((t, dr), BF16),
        ],
        compiler_params=pltpu.CompilerParams(vmem_limit_bytes=VMEM_LIMIT),
        name="proj",
    )(x2, mod, mod, g_mix, w_in_b, cos2, sin2)
    return outs


def _layernorm_silu(c, g, b):
    mu = jnp.mean(c, axis=-1, keepdims=True)
    d = c - mu
    var = jnp.mean(d * d, axis=-1, keepdims=True)
    return _silu(d * lax.rsqrt(var + EPS) * g + b)


def _groupnorm(o, g, b):
    mu = jnp.mean(o, axis=-1, keepdims=True)
    d = o - mu
    var = jnp.mean(d * d, axis=-1, keepdims=True)
    return d * lax.rsqrt(var + EPS) * g + b


def _mix_body(glu_ref, q_ref, k_ref, v_ref, sg_ref, cw_ref, cb_ref, lng_ref, lnb_ref, rg_ref, rb_ref,
              mask_ref, qd_ref, kd_ref, cd_ref, cat_ref, st_ref, buf, cscr, *, tl, dc, hd, nh, chunk):
    nslab = dc // LANES

    @pl.when(pl.program_id(1) == 0)
    def _():
        buf[:, 0:CONV_PAD, :] = jnp.zeros((nslab, CONV_PAD, LANES), F32)
        st_ref[...] = jnp.zeros(st_ref.shape, F32)

    for j in range(nslab):
        buf[j, CONV_PAD:CONV_PAD + tl, :] = glu_ref[:, LANES * j:LANES * (j + 1)]
    first = CONV_PAD - CONV_BUF
    rows_per_iter = 4 * SUBLANES
    for j in range(nslab):
        cols = slice(LANES * j, LANES * (j + 1))
        wv = [jnp.broadcast_to(cw_ref[t:t + 1, cols], (SUBLANES, LANES)) for t in range(CONV_WIDTH)]
        bias = jnp.broadcast_to(cb_ref[0:1, cols], (SUBLANES, LANES))

        def body(r, carry, j=j, cols=cols, wv=wv, bias=bias):
            base = pl.multiple_of(r * rows_per_iter, rows_per_iter)
            for u in range(rows_per_iter // SUBLANES):
                acc = bias
                for t in range(CONV_WIDTH):
                    acc = acc + wv[t] * buf[j, pl.ds(base + (u * SUBLANES + first + t), SUBLANES), :]
                cscr[pl.ds(base + u * SUBLANES, SUBLANES), cols] = acc
            return carry

        lax.fori_loop(0, tl // rows_per_iter, body, 0)
    for j in range(nslab):
        buf[j, 0:CONV_PAD, :] = buf[j, tl:tl + CONV_PAD, :]
    cat_ref[:, 0:dc] = _layernorm_silu(cscr[...], lng_ref[...], lnb_ref[...]).astype(BF16)

    nt = (((1,), (1,)), ((), ()))
    tn = (((0,), (0,)), ((), ()))
    for c in range(tl // chunk):
        rows = slice(c * chunk, (c + 1) * chunk)
        for hh in range(nh):
            cols = slice(hh * hd, (hh + 1) * hd)
            qh = q_ref[rows, cols]
            kh = k_ref[rows, cols]
            vh = v_ref[rows, cols]
            s = st_ref[0, hh]
            scores = lax.dot_general(qh, kh, nt, preferred_element_type=F32) * mask_ref[hh]
            inner = jnp.dot(scores.astype(BF16), vh, preferred_element_type=F32)
            qd = (qh.astype(F32) * qd_ref[hh]).astype(BF16)
            cross = jnp.dot(qd, s.astype(BF16), preferred_element_type=F32)
            kd = (kh.astype(F32) * kd_ref[hh]).astype(BF16)
            st_ref[0, hh] = cd_ref[hh] * s + lax.dot_general(kd, vh, tn, preferred_element_type=F32)
            o = _groupnorm(inner + cross, rg_ref[0:1, cols], rb_ref[0:1, cols])
            cat_ref[rows, dc + hh * hd:dc + (hh + 1) * hd] = (o * sg_ref[rows, cols].astype(F32)).astype(BF16)


def _mix(glu, q, k, v, sg, conv_w, conv_b, ln_g, ln_b, rg, rb, tables, *, nb, length, dc, dr, hd):
    nh = dr // hd
    chunk = math.gcd(length, RET_CHUNK)
    tl = min(length, 256)
    nl = length // tl
    mask, qd, kd, cd = tables
    row_spec = lambda w: pl.BlockSpec((tl, w), lambda b, l: (b * nl + l, 0))
    full = lambda a: pl.BlockSpec(a.shape, lambda b, l: (0,) * a.ndim)
    cat, st = pl.pallas_call(
        functools.partial(_mix_body, tl=tl, dc=dc, hd=hd, nh=nh, chunk=chunk),
        grid=(nb, nl),
        in_specs=[row_spec(dc), row_spec(dr), row_spec(dr), row_spec(dr), row_spec(dr),
                  full(conv_w), full(conv_b), full(ln_g), full(ln_b), full(rg), full(rb),
                  full(mask), full(qd), full(kd), full(cd)],
        out_specs=[row_spec(dc + dr), pl.BlockSpec((1, nh, hd, hd), lambda b, l: (b, 0, 0, 0))],
        out_shape=[jax.ShapeDtypeStruct((nb * length, dc + dr), BF16),
                   jax.ShapeDtypeStruct((nb, nh, hd, hd), F32)],
        scratch_shapes=[pltpu.VMEM((dc // LANES, tl + CONV_PAD, LANES), F32),
                        pltpu.VMEM((tl, dc), F32)],
        compiler_params=pltpu.CompilerParams(dimension_semantics=("arbitrary", "arbitrary")),
        name="mix",
    )(glu, q, k, v, sg, conv_w, conv_b, ln_g, ln_b, rg, rb, mask, qd, kd, cd)
    return cat, st


def _mix1_body(glu_ref, q_ref, k_ref, v_ref, sg_ref, sc_ref, s0_ref, cw_ref, cb_ref, lng_ref, lnb_ref,
               rg_ref, rb_ref, gam_ref, cat_ref, st_ref, cscr, oscr, qf, kf, vf, *, tb, dc, hd, nh):
    w_hist = cw_ref[0:CONV_BUF, :]
    w_last = cw_ref[CONV_BUF:CONV_WIDTH, :]
    for bb in range(tb):
        hist = jnp.sum(sc_ref[bb] * w_hist, axis=0, keepdims=True)
        cscr[bb:bb + 1, :] = hist + glu_ref[bb:bb + 1, :] * w_last + cb_ref[...]
    cat_ref[:, 0:dc] = _layernorm_silu(cscr[...], lng_ref[...], lnb_ref[...]).astype(BF16)

    tn = (((0,), (0,)), ((), ()))
    rowid = lax.broadcasted_iota(I32, (tb, hd), 0)
    qf[...] = q_ref[...].astype(F32)
    kf[...] = k_ref[...].astype(F32)
    vf[...] = v_ref[...].astype(F32)
    for hh in range(nh):
        cols = slice(hh * hd, (hh + 1) * hd)
        qa = q_ref[:, cols]
        ka = k_ref[:, cols]
        gam = gam_ref[hh, 0:1, :]
        for bb in range(tb):
            onehot = (rowid == bb).astype(BF16)
            qcol = lax.dot_general(qa, onehot, tn, preferred_element_type=F32)
            kcol = lax.dot_general(ka, onehot, tn, preferred_element_type=F32)
            s0 = s0_ref[bb, hh]
            qrow = qf[bb:bb + 1, cols]
            krow = kf[bb:bb + 1, cols]
            vrow = vf[bb:bb + 1, cols]
            qk = jnp.sum(qrow * krow, axis=-1, keepdims=True)
            cross = gam * jnp.sum(qcol * s0, axis=0, keepdims=True)
            st_ref[bb, hh] = gam * s0 + kcol * vrow
            oscr[bb:bb + 1, cols] = qk * vrow + cross
    for hh in range(nh):
        cols = slice(hh * hd, (hh + 1) * hd)
        o = _groupnorm(oscr[:, cols], rg_ref[0:1, cols], rb_ref[0:1, cols])
        cat_ref[:, dc + hh * hd:dc + (hh + 1) * hd] = (o * sg_ref[:, cols].astype(F32)).astype(BF16)


def _mix1(glu, q, k, v, sg, state_conv, state_ret, conv_w, conv_b, ln_g, ln_b, rg, rb, gam, *, dc, dr, hd):
    nb = glu.shape[0]
    nh = dr // hd
    tb = 16
    row_spec = lambda w: pl.BlockSpec((tb, w), lambda i: (i, 0))
    full = lambda a: pl.BlockSpec(a.shape, lambda i: (0,) * a.ndim)
    st_spec = pl.BlockSpec((tb, nh, hd, hd), lambda i: (i, 0, 0, 0))
    cat, st = pl.pallas_call(
        functools.partial(_mix1_body, tb=tb, dc=dc, hd=hd, nh=nh),
        grid=(nb // tb,),
        in_specs=[row_spec(dc), row_spec(dr), row_spec(dr), row_spec(dr), row_spec(dr),
                  pl.BlockSpec((tb, CONV_BUF, dc), lambda i: (i, 0, 0)), st_spec,
                  full(conv_w), full(conv_b), full(ln_g), full(ln_b), full(rg), full(rb), full(gam)],
        out_specs=[row_spec(dc + dr), st_spec],
        out_shape=[jax.ShapeDtypeStruct((nb, dc + dr), BF16),
                   jax.ShapeDtypeStruct((nb, nh, hd, hd), F32)],
        scratch_shapes=[pltpu.VMEM((tb, dc), F32)] + [pltpu.VMEM((tb, dr), F32)] * 4,
        compiler_params=pltpu.CompilerParams(vmem_limit_bytes=VMEM_LIMIT),
        name="mix1",
    )(glu, q, k, v, sg, state_conv, state_ret, conv_w, conv_b, ln_g, ln_b, rg, rb, gam)
    return cat, st


def _post_body(cat_ref, x_ref, gtm_ref, scf_ref, shf_ref, gtf_ref, g_ref, wo_ref, wrh_ref, wrl_ref,
               ws1_ref, ws3_ref, ws2_ref, x2_ref, hp_ref, lg_ref):
    d = x_ref.shape[1]
    y = jnp.dot(cat_ref[...], wo_ref[...], preferred_element_type=F32)
    x1 = x_ref[...] + gtm_ref[...] * y
    h = _modulated_rmsnorm(x1, g_ref[...], scf_ref[...], shf_ref[...])
    hb = h.astype(BF16)
    _store_rows(hp_ref, _pack_halves(h[:, 0:d // 2], h[:, d // 2:d]))
    hl = (h - hb.astype(F32)).astype(BF16)
    nt = (((1,), (1,)), ((), ()))
    lg_ref[...] = (lax.dot_general(wrh_ref[...], hb, nt, preferred_element_type=F32)
                   + lax.dot_general(wrh_ref[...], hl, nt, preferred_element_type=F32)
                   + lax.dot_general(wrl_ref[...], hb, nt, preferred_element_type=F32))
    s1 = jnp.dot(hb, ws1_ref[...], preferred_element_type=F32)
    s3 = jnp.dot(hb, ws3_ref[...], preferred_element_type=F32)
    shared = jnp.dot((_silu(s1) * s3).astype(BF16), ws2_ref[...], preferred_element_type=F32)
    x2_ref[...] = x1 + gtf_ref[...] * shared


def _post(cat, x, mod, g_ffn, wo_b, wrh, wrl, ws1_b, ws3_b, ws2_b, *, per_row_mod):
    nb, length, d = x.shape
    tl = min(length, 512)
    nl = length // tl
    t = nb * length
    ne = wrh.shape[0]
    x2d = x.reshape(t, d)
    if per_row_mod:
        mod_spec = lambda j: pl.BlockSpec((tl, d), lambda b, l: (l, j))
    else:
        mod_spec = lambda j: pl.BlockSpec((None, None, 1, d), lambda b, l: (b, j, 0, 0))
    row_spec = lambda w: pl.BlockSpec((tl, w), lambda b, l: (b * nl + l, 0))
    full = lambda a: pl.BlockSpec(a.shape, lambda b, l: (0,) * a.ndim)
    return pl.pallas_call(
        _post_body,
        grid=(nb, nl),
        in_specs=[row_spec(d), row_spec(d), mod_spec(2), mod_spec(4), mod_spec(3), mod_spec(5),
                  full(g_ffn), full(wo_b), full(wrh), full(wrl), full(ws1_b), full(ws3_b), full(ws2_b)],
        out_specs=[row_spec(d),
                   pl.BlockSpec((tl * ROW_WORDS, LANES), lambda b, l: (b * nl + l, 0)),
                   pl.BlockSpec((ne, tl), lambda b, l: (0, b * nl + l))],
        out_shape=[jax.ShapeDtypeStruct((t, d), F32),
                   jax.ShapeDtypeStruct((t * ROW_WORDS, LANES), U32),
                   jax.ShapeDtypeStruct((ne, t), F32)],
        compiler_params=pltpu.CompilerParams(vmem_limit_bytes=VMEM_LIMIT),
        name="post",
    )(cat, x2d, mod, mod, mod, mod, g_ffn, wo_b, wrh, wrl, ws1_b, ws3_b, ws2_b)


def _first_max(x, idx, sentinel):
    m = jnp.max(x, axis=0, keepdims=True)
    f = jnp.min(jnp.where(x == m, idx, sentinel), axis=0, keepdims=True)
    return m, f


def _route_body(lg_ref, bias_ref, e_ref, w_ref, r_ref, cnt_ref, cnt_scr, *, tr, ne, ng, topk, topg):
    @pl.when(pl.program_id(0) == 0)
    def _():
        cnt_scr[...] = jnp.zeros(cnt_scr.shape, F32)

    per = ne // ng
    neg = -jnp.inf
    scores = _sigmoid(lg_ref[...])
    sel = scores + bias_ref[...]
    sub = lax.broadcasted_iota(I32, (per, tr), 0)
    gs = []
    for g in range(ng):
        s_g = sel[g * per:(g + 1) * per, :]
        m1, f1 = _first_max(s_g, sub, per)
        m2 = jnp.max(jnp.where(sub == f1, neg, s_g), axis=0, keepdims=True)
        gs.append(m1 + m2)
    gsc = jnp.concatenate(gs, axis=0)
    gi = lax.broadcasted_iota(I32, (ng, tr), 0)
    keep = jnp.zeros((ng, tr), F32)
    for _ in range(topg):
        _, f = _first_max(gsc, gi, ng)
        pick = gi == f
        keep = jnp.where(pick, 1.0, keep)
        gsc = jnp.where(pick, neg, gsc)
    work = jnp.concatenate(
        [jnp.where(keep[g:g + 1, :] > 0.5, sel[g * per:(g + 1) * per, :], neg) for g in range(ng)], axis=0)
    ei = lax.broadcasted_iota(I32, (ne, tr), 0)
    picks, es, ws = [], [], []
    for _ in range(topk):
        _, f = _first_max(work, ei, ne)
        pick = ei == f
        picks.append(pick)
        es.append(f)
        ws.append(jnp.sum(jnp.where(pick, scores, 0.0), axis=0, keepdims=True))
        work = jnp.where(pick, neg, work)
    wsum = ws[0]
    for w in ws[1:]:
        wsum = wsum + w
    scale = ROUTED_SCALE / wsum
    chosen = picks[0]
    for p in picks[1:]:
        chosen = jnp.logical_or(chosen, p)
    chosen_f = chosen.astype(F32)
    t_row = lax.broadcasted_iota(I32, (tr, tr), 0)
    t_col = lax.broadcasted_iota(I32, (tr, tr), 1)
    before = (t_row < t_col).astype(BF16)
    prior = cnt_scr[:, 0:1] + jnp.dot(chosen_f.astype(BF16), before, preferred_element_type=F32)
    rs = [jnp.sum(jnp.where(p, prior, 0.0), axis=0, keepdims=True).astype(I32) for p in picks]
    pad_i = jnp.zeros((SUBLANES - topk, tr), I32)
    pad_f = jnp.zeros((SUBLANES - topk, tr), F32)
    e_ref[...] = jnp.concatenate(es + [pad_i], axis=0)
    w_ref[...] = jnp.concatenate([w * scale for w in ws] + [pad_f], axis=0)
    r_ref[...] = jnp.concatenate(rs + [pad_i], axis=0)
    total = cnt_scr[:, 0:1] + jnp.sum(chosen_f, axis=1, keepdims=True)
    cnt_scr[...] = jnp.broadcast_to(total, cnt_scr.shape)
    cnt_ref[...] = jnp.broadcast_to(total, cnt_ref.shape)


def _route(logits_t, bias_col):
    ne, t = logits_t.shape
    tr = ROUTE_TILE if t % ROUTE_TILE == 0 else LANES
    tok = lambda dt: jax.ShapeDtypeStruct((SUBLANES, t), dt)
    tok_spec = pl.BlockSpec((SUBLANES, tr), lambda i: (0, i))
    return pl.pallas_call(
        functools.partial(_route_body, tr=tr, ne=ne, ng=N_GROUPS, topk=TOP_K, topg=TOPK_GROUPS),
        grid=(t // tr,),
        in_specs=[pl.BlockSpec((ne, tr), lambda i: (0, i)), pl.BlockSpec((ne, 1), lambda i: (0, 0))],
        out_specs=[tok_spec, tok_spec, tok_spec, pl.BlockSpec((ne, LANES), lambda i: (0, 0))],
        out_shape=[tok(I32), tok(F32), tok(I32), jax.ShapeDtypeStruct((ne, LANES), F32)],
        scratch_shapes=[pltpu.VMEM((ne, LANES), F32)],
        compiler_params=pltpu.CompilerParams(dimension_semantics=("arbitrary",)),
        name="route",
    )(logits_t, bias_col)


def _dest_body(cnt_ref, e_ref, r_ref, d_ref, ps, *, ne, tr):
    shift = EXPERT_ROWS.bit_length() - 1

    @pl.when(pl.program_id(0) == 0)
    def _():
        def step(j, start):
            ps[j] = start
            return start + lax.shift_left(lax.shift_right_logical(cnt_ref[j] + (EXPERT_ROWS - 1), shift), shift)

        lax.fori_loop(0, ne, step, jnp.int32(0))

    e = e_ref[...]
    base = jnp.zeros(e.shape, I32)
    for j in range(ne):
        base = jnp.where(e == j, ps[j], base)
    dest = base + r_ref[...]
    for m in range(tr // TOKEN_TILE):
        d_ref[SUBLANES * m:SUBLANES * (m + 1), :] = dest[:, TOKEN_TILE * m:TOKEN_TILE * (m + 1)]


def _dest_rows(counts, eidx, rank):
    rows, t = eidx.shape
    ne = counts.shape[0]
    tr = ROUTE_TILE if t % ROUTE_TILE == 0 else LANES
    spec = pl.BlockSpec((rows, tr), lambda i, cnt: (0, i))
    return pl.pallas_call(
        functools.partial(_dest_body, ne=ne, tr=tr),
        grid_spec=pltpu.PrefetchScalarGridSpec(
            num_scalar_prefetch=1, grid=(t // tr,), in_specs=[spec, spec],
            out_specs=pl.BlockSpec((tr // TOKEN_TILE * rows, TOKEN_TILE), lambda i, cnt: (i, 0)),
            scratch_shapes=[pltpu.SMEM((ne,), I32)]),
        out_shape=jax.ShapeDtypeStruct((t // TOKEN_TILE * rows, TOKEN_TILE), I32),
        compiler_params=pltpu.CompilerParams(dimension_semantics=("arbitrary",)),
        name="dest",
    )(counts, eidx, rank)


def _sc_mesh():
    return plsc.VectorSubcoreMesh(core_axis_name="c", subcore_axis_name="s")


def _sc_worker_id():
    return lax.axis_index("s") * SC_CORES + lax.axis_index("c")


def _index_block(dest_ref, chunk, width):
    per_tile = TOKEN_TILE // width
    return dest_ref.at[chunk // per_tile, :, pl.ds((chunk % per_tile) * width, width)]


def _dispatch(hp_p, hp_s, dest3, n_rows):
    w = SCATTER_ROWS
    width = hp_p.shape[1]
    ncp = hp_p.shape[0] // w
    nch = ncp + hp_s.shape[0] // w

    @functools.partial(
        pl.kernel, mesh=_sc_mesh(),
        out_type=jax.ShapeDtypeStruct((n_rows, width), U32),
        scratch_types=[pltpu.VMEM((SUBLANES, w), I32), pltpu.VMEM((w, width), U32), pltpu.SemaphoreType.DMA],
        compiler_params=pltpu.CompilerParams(use_tc_tiling_on_sc=False),
        name="dispatch",
    )
    def run(hpp_ref, hps_ref, dest_ref, xs_ref, idx_v, rows_v, sem):
        wid = _sc_worker_id()

        @pl.loop(0, pl.cdiv(nch, SC_WORKERS))
        def _(r):
            c = r * SC_WORKERS + wid

            @pl.when(c < nch)
            def _():
                pltpu.sync_copy(_index_block(dest_ref, c, w), idx_v)

                @pl.when(c < ncp)
                def _():
                    pltpu.sync_copy(hpp_ref.at[pl.ds(c * w, w)], rows_v)

                @pl.when(c >= ncp)
                def _():
                    pltpu.sync_copy(hps_ref.at[pl.ds((c - ncp) * w, w)], rows_v)

                copies = [pltpu.async_copy(rows_v, xs_ref.at[idx_v.at[k]], sem) for k in range(TOP_K)]
                for cp in copies:
                    cp.wait()

    return run(hp_p, hp_s, dest3)


def _undispatch(ys, dest3, n_tokens):
    w = GATHER_ROWS
    width = ys.shape[1]
    nch = n_tokens // w

    @functools.partial(
        pl.kernel, mesh=_sc_mesh(),
        out_type=jax.ShapeDtypeStruct((TOP_K, n_tokens, width), U32),
        scratch_types=[pltpu.VMEM((SUBLANES, w), I32), pltpu.VMEM((TOP_K, w, width), U32),
                       pltpu.SemaphoreType.DMA],
        compiler_params=pltpu.CompilerParams(use_tc_tiling_on_sc=False),
        name="undispatch",
    )
    def run(ys_ref, dest_ref, z_ref, idx_v, bufs, sem):
        wid = _sc_worker_id()

        @pl.loop(0, pl.cdiv(nch, SC_WORKERS))
        def _(r):
            c = r * SC_WORKERS + wid

            @pl.when(c < nch)
            def _():
                pltpu.sync_copy(_index_block(dest_ref, c, w), idx_v)
                gathers = [pltpu.async_copy(ys_ref.at[idx_v.at[k]], bufs.at[k], sem) for k in range(TOP_K)]
                for cp in gathers:
                    cp.wait()
                stores = [pltpu.async_copy(bufs.at[k], z_ref.at[k, pl.ds(c * w, w)], sem) for k in range(TOP_K)]
                for cp in stores:
                    cp.wait()

    return run(ys, dest3)


def _expert_body(cnt_ref, xs_ref, w1_ref, w3_ref, w2_ref, ys_ref,
                 w1f, w3f, w2f, w1s, w3s, w2s, xbuf, ybuf, xlo, xhi, sem_x, sem_y, sem_w, *, tm, ne):
    blk_words = tm * ROW_WORDS
    half = ROW_WORDS * LANES
    shift = tm.bit_length() - 1

    def n_blocks_of(e):
        return lax.shift_right_logical(cnt_ref[e] + (tm - 1), shift)

    def next_nonempty(e):
        return lax.while_loop(
            lambda c: jnp.logical_and(c < ne, n_blocks_of(jnp.minimum(c, ne - 1)) == 0), lambda c: c + 1, e)

    nu = lax.fori_loop(0, ne, lambda e, acc: acc + n_blocks_of(e), jnp.int32(0))

    def x_copy(i, slot):
        src = xs_ref.at[pl.ds(pl.multiple_of(i * blk_words, blk_words), blk_words), :]
        return pltpu.make_async_copy(src, xbuf.at[slot], sem_x.at[slot])

    def y_copy(i, slot):
        dst = ys_ref.at[pl.ds(pl.multiple_of(i * blk_words, blk_words), blk_words), :]
        return pltpu.make_async_copy(ybuf.at[slot], dst, sem_y.at[slot])

    def w_copies(e, ws):
        return [pltpu.make_async_copy(src.at[e], dst.at[ws], sem_w.at[ws])
                for src, dst in ((w1_ref, w1f), (w3_ref, w3f), (w2_ref, w2f))]

    for p in range(X_SLOTS - 1):
        @pl.when(p < nu)
        def _(p=p):
            x_copy(p, p).start()

    e_first = next_nonempty(jnp.int32(0))

    @pl.when(e_first < ne)
    def _():
        for cp in w_copies(e_first, 0):
            cp.start()

    def block(i):
        slot = i % X_SLOTS
        yslot = i % Y_SLOTS
        x_copy(i, slot).wait()

        @pl.when(i + (X_SLOTS - 1) < nu)
        def _():
            x_copy(i + (X_SLOTS - 1), (i + (X_SLOTS - 1)) % X_SLOTS).start()

        @pl.when(i >= Y_SLOTS)
        def _():
            y_copy(i - Y_SLOTS, yslot).wait()

        xin = xbuf.at[slot]
        for w in range(ROW_WORDS):
            lo, hi = _unpack_halves(_load_row_word(xin, w, tm))
            xlo[:, LANES * w:LANES * (w + 1)] = lo.astype(BF16)
            xhi[:, LANES * w:LANES * (w + 1)] = hi.astype(BF16)

        def up(wsc):
            return (jnp.dot(xlo[...], wsc[0:half, :], preferred_element_type=F32)
                    + jnp.dot(xhi[...], wsc[half:2 * half, :], preferred_element_type=F32))

        hid = (_silu(up(w1s)) * up(w3s)).astype(BF16)
        y = jnp.dot(hid, w2s[...], preferred_element_type=F32)
        _store_rows(ybuf.at[yslot], _pack_halves(y[:, 0:half], y[:, half:2 * half]))
        y_copy(i, yslot).start()

    def per_expert(e, carry):
        i0, ws = carry
        n = n_blocks_of(e)

        @pl.when(n > 0)
        def _():
            for cp in w_copies(e, ws):
                cp.wait()
            w1s[...] = w1f[ws].astype(BF16)
            w3s[...] = w3f[ws].astype(BF16)
            w2s[...] = w2f[ws].astype(BF16)
            e_next = next_nonempty(e + 1)

            @pl.when(e_next < ne)
            def _():
                for cp in w_copies(e_next, 1 - ws):
                    cp.start()

            def body(j, c):
                block(i0 + j)
                return c

            lax.fori_loop(0, n, body, 0)

        return i0 + n, jnp.where(n > 0, 1 - ws, ws)

    lax.fori_loop(0, ne, per_expert, (jnp.int32(0), jnp.int32(0)))

    for q in range(Y_SLOTS):
        @pl.when(nu > q)
        def _(q=q):
            y_copy(nu - 1 - q, (nu - 1 - q) % Y_SLOTS).wait()


def _experts(counts, xs, w1e, w3e, w2e):
    tm = EXPERT_ROWS
    ne, d, de = w1e.shape
    half = ROW_WORDS * LANES
    anyspec = pl.BlockSpec(memory_space=pl.ANY)
    blk_buf = lambda n: pltpu.VMEM((n, tm * ROW_WORDS, LANES), U32)
    return pl.pallas_call(
        functools.partial(_expert_body, tm=tm, ne=ne),
        grid_spec=pltpu.PrefetchScalarGridSpec(
            num_scalar_prefetch=1,
            grid=(1,),
            in_specs=[anyspec, anyspec, anyspec, anyspec],
            out_specs=anyspec,
            scratch_shapes=[pltpu.VMEM((2, d, de), F32), pltpu.VMEM((2, d, de), F32), pltpu.VMEM((2, de, d), F32),
                            pltpu.VMEM((d, de), BF16), pltpu.VMEM((d, de), BF16), pltpu.VMEM((de, d), BF16),
                            blk_buf(X_SLOTS), blk_buf(Y_SLOTS),
                            pltpu.VMEM((tm, half), BF16), pltpu.VMEM((tm, half), BF16),
                            pltpu.SemaphoreType.DMA((X_SLOTS,)), pltpu.SemaphoreType.DMA((Y_SLOTS,)),
                            pltpu.SemaphoreType.DMA((2,))]),
        out_shape=jax.ShapeDtypeStruct(xs.shape, U32),
        compiler_params=pltpu.CompilerParams(dimension_semantics=("arbitrary",), vmem_limit_bytes=VMEM_LIMIT),
        name="experts",
    )(counts, xs, w1e, w3e, w2e)


def _combine_body(z_ref, x2_ref, gtf_ref, wt_ref, gfin_ref, y_ref, xo, *, td, topk):
    half = ROW_WORDS * LANES
    wt = wt_ref[...].T
    ws = [wt[:, k:k + 1] for k in range(topk)]
    sq = jnp.zeros((td, 1), F32)
    for j in range(ROW_WORDS):
        acc_lo = jnp.zeros((td, LANES), F32)
        acc_hi = jnp.zeros((td, LANES), F32)
        for k in range(topk):
            lo, hi = _unpack_halves(_load_row_word(z_ref.at[k], j, td))
            acc_lo = acc_lo + ws[k] * lo
            acc_hi = acc_hi + ws[k] * hi
        for base, acc in ((0, acc_lo), (half, acc_hi)):
            cols = slice(base + LANES * j, base + LANES * (j + 1))
            x = x2_ref[:, cols] + gtf_ref[:, cols] * acc
            xo[:, cols] = x
            sq = sq + jnp.sum(x * x, axis=-1, keepdims=True)
    rs = lax.rsqrt(sq / (2 * half) + EPS)
    y_ref[...] = xo[...] * rs * gfin_ref[...]


def _combine(z, token0, wts, x2, mod, g_final, *, rows_per_mod, per_row_mod):
    t, d = x2.shape
    td = min(t, COMBINE_TILE)
    tile0 = token0 // td
    if per_row_mod:
        gtf_spec = pl.BlockSpec((td, d), lambda i: (i, 5))
    else:
        tiles_per_mod = rows_per_mod // td
        gtf_spec = pl.BlockSpec((None, None, 1, d), lambda i: (i // tiles_per_mod, 5, 0, 0))
    return pl.pallas_call(
        functools.partial(_combine_body, td=td, topk=TOP_K),
        grid=(t // td,),
        in_specs=[pl.BlockSpec((TOP_K, td * ROW_WORDS, LANES), lambda i: (0, i + tile0, 0)),
                  pl.BlockSpec((td, d), lambda i: (i, 0)),
                  gtf_spec,
                  pl.BlockSpec((SUBLANES, td), lambda i: (0, i + tile0)),
                  pl.BlockSpec((1, d), lambda i: (0, 0))],
        out_specs=pl.BlockSpec((td, d), lambda i: (i, 0)),
        scratch_shapes=[pltpu.VMEM((td, d), F32)],
        out_shape=jax.ShapeDtypeStruct((t, d), F32),
        name="combine",
    )(z, x2, mod, wts, g_final)


def _retention_tables(length, nh, hd):
    c = math.gcd(length, RET_CHUNK)
    log_g = jnp.log(1.0 - 2.0 ** (-5.0 - jnp.arange(nh, dtype=F32)))
    idx = jnp.arange(c, dtype=F32)
    rel = idx[:, None] - idx[None, :]
    mask = jnp.where(rel >= 0, jnp.exp(log_g[:, None, None] * jnp.maximum(rel, 0.0)), 0.0)
    q_decay = jnp.exp(log_g[None, :] * (idx[:, None] + 1.0))
    k_decay = jnp.exp(log_g[None, :] * (c - 1.0 - idx[:, None]))
    chunk_decay = jnp.exp(log_g * c)
    qd = jnp.broadcast_to(q_decay.T[:, :, None], (nh, c, hd))
    kd = jnp.broadcast_to(k_decay.T[:, :, None], (nh, c, hd))
    cd = jnp.broadcast_to(chunk_decay[:, None, None], (nh, hd, hd))
    return mask, qd, kd, cd


def kernel(x_prompt, x_sample, c_prompt, c_sample, state_conv, state_ret, w_ada, b_ada, g_mix, g_ffn, w_in,
           conv_w, conv_b, conv_norm_g, conv_norm_b, ret_norm_g, ret_norm_b, w_out, w_router, router_bias,
           w1, w3, w2, ws1, ws3, ws2, g_final):
    depth = w_ada.shape[0]
    assert depth == 1, "single-layer trunk"
    bp, lp, d = x_prompt.shape
    bs, ls, _ = x_sample.shape
    assert ls == 1
    dc = conv_w.shape[2]
    dr = ret_norm_g.shape[1]
    nh = RET_HEADS
    hd = dr // nh
    assert hd == LANES and lp % 256 == 0 and bs % TOKEN_TILE == 0 and d // 2 == ROW_WORDS * LANES
    ne = w_router.shape[2]
    row = lambda a: a.reshape(1, -1)

    mod = _ada(jnp.concatenate([c_prompt, c_sample], axis=0), w_ada[0], row(b_ada[0]))
    mod_p = mod[:bp].reshape(bp, 6, 1, d)
    mod_s = mod[bp:]

    half = hd // 2
    inv = ROPE_BASE ** (-jnp.arange(half, dtype=F32) / half)
    inv2 = jnp.concatenate([inv, inv]).reshape(1, hd)
    cos_p, sin_p = _rope_tables(inv2, lp, 0)
    cos_s, sin_s = _rope_tables(inv2, SUBLANES, PAST_LEN)

    w_in_b = w_in[0].astype(BF16)
    wo_b = w_out[0].astype(BF16)
    wr_t = w_router[0].T
    wrh = wr_t.astype(BF16)
    wrl = (wr_t - wrh.astype(F32)).astype(BF16)
    ws1_b, ws3_b, ws2_b = ws1[0].astype(BF16), ws3[0].astype(BF16), ws2[0].astype(BF16)
    dims = dict(dc=dc, dr=dr, hd=hd)

    glu_p, q_p, k_p, v_p, sg_p = _proj(x_prompt, mod_p, row(g_mix[0]), w_in_b, cos_p, sin_p,
                                       per_row_mod=False, **dims)
    cat_p, ret_p = _mix(glu_p, q_p, k_p, v_p, sg_p, conv_w[0], row(conv_b[0]), row(conv_norm_g[0]),
                        row(conv_norm_b[0]), row(ret_norm_g[0]), row(ret_norm_b[0]),
                        _retention_tables(lp, nh, hd), nb=bp, length=lp, **dims)
    x2_p, hp_p, lg_p = _post(cat_p, x_prompt, mod_p, row(g_ffn[0]), wo_b, wrh, wrl, ws1_b, ws3_b, ws2_b,
                             per_row_mod=False)

    xs3 = x_sample.reshape(1, bs, d)
    glu_s, q_s, k_s, v_s, sg_s = _proj(xs3, mod_s, row(g_mix[0]), w_in_b, cos_s, sin_s,
                                       per_row_mod=True, **dims)
    log_g = jnp.log(1.0 - 2.0 ** (-5.0 - jnp.arange(nh, dtype=F32)))
    gam = jnp.broadcast_to(jnp.exp(log_g)[:, None, None], (nh, SUBLANES, hd))
    cat_s, ret_s = _mix1(glu_s, q_s, k_s, v_s, sg_s, state_conv[0], state_ret[0], conv_w[0], row(conv_b[0]),
                         row(conv_norm_g[0]), row(conv_norm_b[0]), row(ret_norm_g[0]), row(ret_norm_b[0]),
                         gam, **dims)
    x2_s, hp_s, lg_s = _post(cat_s, xs3, mod_s, row(g_ffn[0]), wo_b, wrh, wrl, ws1_b, ws3_b, ws2_b,
                             per_row_mod=True)

    tp = bp * lp
    eidx, wts, rank, cnt = _route(jnp.concatenate([lg_p, lg_s], axis=1), router_bias[0].reshape(ne, 1))
    counts = cnt[:, 0].astype(I32)
    tm = EXPERT_ROWS
    n_assign = (tp + bs) * TOP_K
    n_blocks = -(-(n_assign + ne * (tm - 1)) // tm)
    tall = tp + bs
    n_rows = n_blocks * tm
    hw = d // 2
    dest3 = _dest_rows(counts, eidx, rank).reshape(tall // TOKEN_TILE, SUBLANES, TOKEN_TILE)

    xs = _dispatch(hp_p.reshape(tp, hw), hp_s.reshape(bs, hw), dest3, n_rows)
    ys = _experts(counts, xs.reshape(n_rows * ROW_WORDS, LANES), w1[0], w3[0], w2[0])
    z = _undispatch(ys.reshape(n_rows, hw), dest3, tall).reshape(TOP_K, tall * ROW_WORDS, LANES)
    y_p = _combine(z, 0, wts, x2_p, mod_p, row(g_final), rows_per_mod=lp, per_row_mod=False)
    y_s = _combine(z, tp, wts, x2_s, mod_s, row(g_final), rows_per_mod=bs, per_row_mod=True)

    new_conv_p = glu_p.reshape(bp, lp, dc)[:, lp - CONV_BUF:, :]
    new_conv_s = jnp.concatenate([state_conv[0][:, 1:, :], glu_s[:, None, :]], axis=1)
    return (y_p.reshape(bp, lp, d), y_s.reshape(bs, ls, d), new_conv_p[None], ret_p[None],
            new_conv_s[None], ret_s[None])
```

```python
import functools
import math

import jax
import jax.numpy as jnp
from jax import lax
from jax.experimental import pallas as pl
from jax.experimental.pallas import tpu as pltpu
from jax.experimental.pallas import tpu_sc as plsc

F32 = jnp.float32
BF16 = jnp.bfloat16
U32 = jnp.uint32
I32 = jnp.int32

EPS = 1e-6
PAST_LEN = 16384
RET_HEADS = 4
RET_CHUNK = 128
CONV_WIDTH = 31
CONV_BUF = CONV_WIDTH - 1
ROPE_BASE = 10000.0
N_EXPERTS = 64
TOP_K = 6
N_GROUPS = 8
TOPK_GROUPS = 4
ROUTED_SCALE = 2.5

LANES = 128
SUBLANES = 8
CONV_PAD = 32
EXPERT_ROWS = 256
ROUTE_TILE = 640
TOKEN_TILE = 128
COMBINE_TILE = 256
VMEM_LIMIT = 56 * 1024 * 1024
SC_CORES = 2
SC_SUBCORES = 16
SC_WORKERS = SC_CORES * SC_SUBCORES
SCATTER_ROWS = 64
GATHER_ROWS = 32
ROW_WORDS = 4
X_SLOTS = 4
Y_SLOTS = 3

HI_MASK = 0xFFFF0000


def _sigmoid(x):
    return jax.nn.sigmoid(x)


def _silu(x):
    return x * jax.nn.sigmoid(x)


def _pack_halves(lo, hi):
    lo_u = lax.bitcast_convert_type(lo.astype(BF16).astype(F32), U32) >> 16
    hi_u = lax.bitcast_convert_type(hi.astype(BF16).astype(F32), U32) & jnp.uint32(HI_MASK)
    return hi_u | lo_u


def _unpack_halves(p):
    lo = lax.bitcast_convert_type(p << 16, F32)
    hi = lax.bitcast_convert_type(p & jnp.uint32(HI_MASK), F32)
    return lo, hi


def _store_rows(ref, x):
    rows = x.shape[0]
    for j in range(ROW_WORDS):
        ref[pl.ds(j, rows, stride=ROW_WORDS), :] = x[:, LANES * j:LANES * (j + 1)]


def _load_row_word(ref, j, rows):
    return ref[pl.ds(j, rows, stride=ROW_WORDS), :]


def _ada_body(c_ref, w_ref, b_ref, o_ref):
    s = _silu(c_ref[...]).astype(BF16)
    o_ref[...] = jnp.dot(s, w_ref[...].astype(BF16), preferred_element_type=F32) + b_ref[...]


def _ada(c_all, w_ada, b_ada):
    rows, d = c_all.shape
    n = w_ada.shape[1]
    tn = 512
    return pl.pallas_call(
        _ada_body,
        grid=(n // tn,),
        in_specs=[
            pl.BlockSpec((rows, d), lambda j: (0, 0)),
            pl.BlockSpec((d, tn), lambda j: (0, j)),
            pl.BlockSpec((1, tn), lambda j: (0, j)),
        ],
        out_specs=pl.BlockSpec((rows, tn), lambda j: (0, j)),
        out_shape=jax.ShapeDtypeStruct((rows, n), F32),
        name="ada",
    )(c_all, w_ada, b_ada)


def _rope_body(inv_ref, cos_ref, sin_ref, *, pos0, tl, half):
    row = lax.broadcasted_iota(I32, (tl, LANES), 0) + pl.program_id(0) * tl
    ang = (row.astype(F32) + pos0) * inv_ref[...]
    lane = lax.broadcasted_iota(I32, (tl, LANES), 1)
    s = jnp.sin(ang)
    cos_ref[...] = jnp.cos(ang)
    sin_ref[...] = jnp.where(lane < half, -s, s)


def _rope_tables(inv2, rows, pos0):
    tl = min(rows, 256)
    return pl.pallas_call(
        functools.partial(_rope_body, pos0=float(pos0), tl=tl, half=LANES // 2),
        grid=(rows // tl,),
        in_specs=[pl.BlockSpec((1, LANES), lambda i: (0, 0))],
        out_specs=[pl.BlockSpec((tl, LANES), lambda i: (i, 0))] * 2,
        out_shape=[jax.ShapeDtypeStruct((rows, LANES), F32)] * 2,
        name="rope",
    )(inv2)


def _modulated_rmsnorm(x, g, sc, sh):
    ms = jnp.mean(x * x, axis=-1, keepdims=True)
    h = x * lax.rsqrt(ms + EPS) * g
    return h * (1.0 + sc) + sh


def _proj_body(x_ref, sh_ref, sc_ref, g_ref, w_ref, cos_ref, sin_ref,
               glu_ref, q_ref, k_ref, v_ref, sg_ref, *, dc, dr, hd, rope_rows):
    hb = _modulated_rmsnorm(x_ref[...], g_ref[...], sc_ref[...], sh_ref[...]).astype(BF16)

    def proj(lo, n):
        return jnp.dot(hb, w_ref[:, lo:lo + n], preferred_element_type=F32)

    glu_ref[...] = proj(0, dc) * _sigmoid(proj(dc, dc))
    cos = cos_ref[...] if rope_rows else cos_ref[0:1, :]
    sin = sin_ref[...] if rope_rows else sin_ref[0:1, :]
    for ref, lo, scale in ((q_ref, 2 * dc, hd ** -0.5), (k_ref, 2 * dc + dr, None)):
        t = proj(lo, dr)
        for hh in range(dr // hd):
            th = t[:, hh * hd:(hh + 1) * hd]
            r = th * cos + pltpu.roll(th, hd // 2, 1) * sin
            if scale is not None:
                r = r * scale
            ref[:, hh * hd:(hh + 1) * hd] = r.astype(BF16)
    v_ref[...] = proj(2 * dc + 2 * dr, dr).astype(BF16)
    sg_ref[...] = _silu(proj(2 * dc + 3 * dr, dr)).astype(BF16)


def _proj(x, mod, g_mix, w_in_b, cos2, sin2, *, dc, dr, hd, per_row_mod, b0=0, nb=None):
    nb_all, length, d = x.shape
    nb = nb_all if nb is None else nb
    tl = min(length, 256)
    x2 = x.reshape(nb_all * length, d)
    nl = length // tl
    if per_row_mod:
        mod_spec = lambda j: pl.BlockSpec((tl, d), lambda b, l: (l, j))
        rope_spec = pl.BlockSpec((SUBLANES, LANES), lambda b, l: (0, 0))
    else:
        mod_spec = lambda j: pl.BlockSpec((None, None, 1, d), lambda b, l: (b + b0, j, 0, 0))
        rope_spec = pl.BlockSpec((tl, LANES), lambda b, l: (l, 0))
    row_spec = lambda w: pl.BlockSpec((tl, w), lambda b, l: (b * nl + l, 0))
    t = nb * length
    outs = pl.pallas_call(
        functools.partial(_proj_body, dc=dc, dr=dr, hd=hd, rope_rows=not per_row_mod),
        grid=(nb, nl),
        in_specs=[
            pl.BlockSpec((tl, d), lambda b, l: ((b + b0) * nl + l, 0)), mod_spec(0), mod_spec(1),
            pl.BlockSpec((1, d), lambda b, l: (0, 0)),
            pl.BlockSpec(w_in_b.shape, lambda b, l: (0, 0)),
            rope_spec, rope_spec,
        ],
        out_specs=[row_spec(dc), row_spec(dr), row_spec(dr), row_spec(dr), row_spec(dr)],
        out_shape=[
            jax.ShapeDtypeStruct((t, dc), F32),
            jax.ShapeDtypeStruct((t, dr), BF16),
            jax.ShapeDtypeStruct((t, dr), BF16),
            jax.ShapeDtypeStruct((t, dr), BF16),
            jax.ShapeDtypeStruct((t, dr), BF16),
        ],
        compiler_params=pltpu.CompilerParams(vmem_limit_bytes=VMEM_LIMIT),
        name="proj",
    )(x2, mod, mod, g_mix, w_in_b, cos2, sin2)
    return outs


def _layernorm_silu(c, g, b):
    mu = jnp.mean(c, axis=-1, keepdims=True)
    d = c - mu
    var = jnp.mean(d * d, axis=-1, keepdims=True)
    return _silu(d * lax.rsqrt(var + EPS) * g + b)


def _groupnorm(o, g, b):
    mu = jnp.mean(o, axis=-1, keepdims=True)
    d = o - mu
    var = jnp.mean(d * d, axis=-1, keepdims=True)
    return d * lax.rsqrt(var + EPS) * g + b


def _mix_body(glu_ref, q_ref, k_ref, v_ref, sg_ref, cw_ref, cb_ref, lng_ref, lnb_ref, rg_ref, rb_ref,
              mask_ref, qd_ref, kd_ref, cd_ref, cat_ref, st_ref, buf, cscr, *, tl, dc, hd, nh, chunk):
    nslab = dc // LANES

    @pl.when(pl.program_id(1) == 0)
    def _():
        buf[:, 0:CONV_PAD, :] = jnp.zeros((nslab, CONV_PAD, LANES), F32)
        st_ref[...] = jnp.zeros(st_ref.shape, F32)

    for j in range(nslab):
        buf[j, CONV_PAD:CONV_PAD + tl, :] = glu_ref[:, LANES * j:LANES * (j + 1)]
    first = CONV_PAD - CONV_BUF
    rows_per_iter = 8 * SUBLANES
    for j in range(nslab):
        cols = slice(LANES * j, LANES * (j + 1))
        wv = [jnp.broadcast_to(cw_ref[t:t + 1, cols], (SUBLANES, LANES)) for t in range(CONV_WIDTH)]
        bias = jnp.broadcast_to(cb_ref[0:1, cols], (SUBLANES, LANES))

        def body(r, carry, j=j, cols=cols, wv=wv, bias=bias):
            base = pl.multiple_of(r * rows_per_iter, rows_per_iter)
            for u in range(rows_per_iter // SUBLANES):
                acc = bias
                for t in range(CONV_WIDTH):
                    acc = acc + wv[t] * buf[j, pl.ds(base + (u * SUBLANES + first + t), SUBLANES), :]
                cscr[pl.ds(base + u * SUBLANES, SUBLANES), cols] = acc
            return carry

        lax.fori_loop(0, tl // rows_per_iter, body, 0)
    for j in range(nslab):
        buf[j, 0:CONV_PAD, :] = buf[j, tl:tl + CONV_PAD, :]
    cat_ref[:, 0:dc] = _layernorm_silu(cscr[...], lng_ref[...], lnb_ref[...]).astype(BF16)

    nt = (((1,), (1,)), ((), ()))
    tn = (((0,), (0,)), ((), ()))
    for c in range(tl // chunk):
        rows = slice(c * chunk, (c + 1) * chunk)
        for hh in range(nh):
            cols = slice(hh * hd, (hh + 1) * hd)
            qh = q_ref[rows, cols]
            kh = k_ref[rows, cols]
            vh = v_ref[rows, cols]
            s = st_ref[0, hh]
            scores = lax.dot_general(qh, kh, nt, preferred_element_type=F32) * mask_ref[hh]
            inner = jnp.dot(scores.astype(BF16), vh, preferred_element_type=F32)
            qd = (qh.astype(F32) * qd_ref[hh]).astype(BF16)
            cross = jnp.dot(qd, s.astype(BF16), preferred_element_type=F32)
            kd = (kh.astype(F32) * kd_ref[hh]).astype(BF16)
            st_ref[0, hh] = cd_ref[hh] * s + lax.dot_general(kd, vh, tn, preferred_element_type=F32)
            o = _groupnorm(inner + cross, rg_ref[0:1, cols], rb_ref[0:1, cols])
            cat_ref[rows, dc + hh * hd:dc + (hh + 1) * hd] = (o * sg_ref[rows, cols].astype(F32)).astype(BF16)


def _mix(glu, q, k, v, sg, conv_w, conv_b, ln_g, ln_b, rg, rb, tables, *, nb, length, dc, dr, hd):
    nh = dr // hd
    chunk = math.gcd(length, RET_CHUNK)
    tl = min(length, 256)
    nl = length // tl
    mask, qd, kd, cd = tables
    row_spec = lambda w: pl.BlockSpec((tl, w), lambda b, l: (b * nl + l, 0))
    full = lambda a: pl.BlockSpec(a.shape, lambda b, l: (0,) * a.ndim)
    cat, st = pl.pallas_call(
        functools.partial(_mix_body, tl=tl, dc=dc, hd=hd, nh=nh, chunk=chunk),
        grid=(nb, nl),
        in_specs=[row_spec(dc), row_spec(dr), row_spec(dr), row_spec(dr), row_spec(dr),
                  full(conv_w), full(conv_b), full(ln_g), full(ln_b), full(rg), full(rb),
                  full(mask), full(qd), full(kd), full(cd)],
        out_specs=[row_spec(dc + dr), pl.BlockSpec((1, nh, hd, hd), lambda b, l: (b, 0, 0, 0))],
        out_shape=[jax.ShapeDtypeStruct((nb * length, dc + dr), BF16),
                   jax.ShapeDtypeStruct((nb, nh, hd, hd), F32)],
        scratch_shapes=[pltpu.VMEM((dc // LANES, tl + CONV_PAD, LANES), F32),
                        pltpu.VMEM((tl, dc), F32)],
        compiler_params=pltpu.CompilerParams(dimension_semantics=("arbitrary", "arbitrary")),
        name="mix",
    )(glu, q, k, v, sg, conv_w, conv_b, ln_g, ln_b, rg, rb, mask, qd, kd, cd)
    return cat, st


def _mix1_body(glu_ref, q_ref, k_ref, v_ref, sg_ref, sc_ref, s0_ref, cw_ref, cb_ref, lng_ref, lnb_ref,
               rg_ref, rb_ref, gam_ref, cat_ref, st_ref, cscr, oscr, qf, kf, vf, *, tb, dc, hd, nh):
    w_hist = cw_ref[0:CONV_BUF, :]
    w_last = cw_ref[CONV_BUF:CONV_WIDTH, :]
    for bb in range(tb):
        hist = jnp.sum(sc_ref[bb] * w_hist, axis=0, keepdims=True)
        cscr[bb:bb + 1, :] = hist + glu_ref[bb:bb + 1, :] * w_last + cb_ref[...]
    cat_ref[:, 0:dc] = _layernorm_silu(cscr[...], lng_ref[...], lnb_ref[...]).astype(BF16)

    tn = (((0,), (0,)), ((), ()))
    rowid = lax.broadcasted_iota(I32, (tb, hd), 0)
    qf[...] = q_ref[...].astype(F32)
    kf[...] = k_ref[...].astype(F32)
    vf[...] = v_ref[...].astype(F32)
    for hh in range(nh):
        cols = slice(hh * hd, (hh + 1) * hd)
        qa = q_ref[:, cols]
        ka = k_ref[:, cols]
        gam = gam_ref[hh, 0:1, :]
        for bb in range(tb):
            onehot = (rowid == bb).astype(BF16)
            qcol = lax.dot_general(qa, onehot, tn, preferred_element_type=F32)
            kcol = lax.dot_general(ka, onehot, tn, preferred_element_type=F32)
            s0 = s0_ref[bb, hh]
            qrow = qf[bb:bb + 1, cols]
            krow = kf[bb:bb + 1, cols]
            vrow = vf[bb:bb + 1, cols]
            qk = jnp.sum(qrow * krow, axis=-1, keepdims=True)
            cross = gam * jnp.sum(qcol * s0, axis=0, keepdims=True)
            st_ref[bb, hh] = gam * s0 + kcol * vrow
            oscr[bb:bb + 1, cols] = qk * vrow + cross
    for hh in range(nh):
        cols = slice(hh * hd, (hh + 1) * hd)
        o = _groupnorm(oscr[:, cols], rg_ref[0:1, cols], rb_ref[0:1, cols])
        cat_ref[:, dc + hh * hd:dc + (hh + 1) * hd] = (o * sg_ref[:, cols].astype(F32)).astype(BF16)


def _mix1(glu, q, k, v, sg, state_conv, state_ret, conv_w, conv_b, ln_g, ln_b, rg, rb, gam, *, dc, dr, hd):
    nb = glu.shape[0]
    nh = dr // hd
    tb = 16
    row_spec = lambda w: pl.BlockSpec((tb, w), lambda i: (i, 0))
    full = lambda a: pl.BlockSpec(a.shape, lambda i: (0,) * a.ndim)
    st_spec = pl.BlockSpec((tb, nh, hd, hd), lambda i: (i, 0, 0, 0))
    cat, st = pl.pallas_call(
        functools.partial(_mix1_body, tb=tb, dc=dc, hd=hd, nh=nh),
        grid=(nb // tb,),
        in_specs=[row_spec(dc), row_spec(dr), row_spec(dr), row_spec(dr), row_spec(dr),
                  pl.BlockSpec((tb, CONV_BUF, dc), lambda i: (i, 0, 0)), st_spec,
                  full(conv_w), full(conv_b), full(ln_g), full(ln_b), full(rg), full(rb), full(gam)],
        out_specs=[row_spec(dc + dr), st_spec],
        out_shape=[jax.ShapeDtypeStruct((nb, dc + dr), BF16),
                   jax.ShapeDtypeStruct((nb, nh, hd, hd), F32)],
        scratch_shapes=[pltpu.VMEM((tb, dc), F32)] + [pltpu.VMEM((tb, dr), F32)] * 4,
        compiler_params=pltpu.CompilerParams(vmem_limit_bytes=VMEM_LIMIT),
        name="mix1",
    )(glu, q, k, v, sg, state_conv, state_ret, conv_w, conv_b, ln_g, ln_b, rg, rb, gam)
    return cat, st


def _post_body(cat_ref, x_ref, gtm_ref, scf_ref, shf_ref, gtf_ref, g_ref, wo_ref, wrh_ref, wrl_ref,
               ws1_ref, ws3_ref, ws2_ref, x2_ref, hp_ref, lg_ref):
    d = x_ref.shape[1]
    y = jnp.dot(cat_ref[...], wo_ref[...], preferred_element_type=F32)
    x1 = x_ref[...] + gtm_ref[...] * y
    h = _modulated_rmsnorm(x1, g_ref[...], scf_ref[...], shf_ref[...])
    hb = h.astype(BF16)
    _store_rows(hp_ref, _pack_halves(h[:, 0:d // 2], h[:, d // 2:d]))
    hl = (h - hb.astype(F32)).astype(BF16)
    nt = (((1,), (1,)), ((), ()))
    lg_ref[...] = (lax.dot_general(wrh_ref[...], hb, nt, preferred_element_type=F32)
                   + lax.dot_general(wrh_ref[...], hl, nt, preferred_element_type=F32)
                   + lax.dot_general(wrl_ref[...], hb, nt, preferred_element_type=F32))
    s1 = jnp.dot(hb, ws1_ref[...], preferred_element_type=F32)
    s3 = jnp.dot(hb, ws3_ref[...], preferred_element_type=F32)
    shared = jnp.dot((_silu(s1) * s3).astype(BF16), ws2_ref[...], preferred_element_type=F32)
    x2_ref[...] = x1 + gtf_ref[...] * shared


def _post(cat, x, mod, g_ffn, wo_b, wrh, wrl, ws1_b, ws3_b, ws2_b, *, per_row_mod, b0=0, nb=None):
    nb_all, length, d = x.shape
    nb = nb_all if nb is None else nb
    tl = min(length, 512)
    nl = length // tl
    t = nb * length
    ne = wrh.shape[0]
    x2d = x.reshape(nb_all * length, d)
    if per_row_mod:
        mod_spec = lambda j: pl.BlockSpec((tl, d), lambda b, l: (l, j))
    else:
        mod_spec = lambda j: pl.BlockSpec((None, None, 1, d), lambda b, l: (b + b0, j, 0, 0))
    row_spec = lambda w: pl.BlockSpec((tl, w), lambda b, l: (b * nl + l, 0))
    full = lambda a: pl.BlockSpec(a.shape, lambda b, l: (0,) * a.ndim)
    return pl.pallas_call(
        _post_body,
        grid=(nb, nl),
        in_specs=[row_spec(d), pl.BlockSpec((tl, d), lambda b, l: ((b + b0) * nl + l, 0)),
                  mod_spec(2), mod_spec(4), mod_spec(3), mod_spec(5),
                  full(g_ffn), full(wo_b), full(wrh), full(wrl), full(ws1_b), full(ws3_b), full(ws2_b)],
        out_specs=[row_spec(d),
                   pl.BlockSpec((tl * ROW_WORDS, LANES), lambda b, l: (b * nl + l, 0)),
                   pl.BlockSpec((ne, tl), lambda b, l: (0, b * nl + l))],
        out_shape=[jax.ShapeDtypeStruct((t, d), F32),
                   jax.ShapeDtypeStruct((t * ROW_WORDS, LANES), U32),
                   jax.ShapeDtypeStruct((ne, t), F32)],
        compiler_params=pltpu.CompilerParams(vmem_limit_bytes=VMEM_LIMIT),
        name="post",
    )(cat, x2d, mod, mod, mod, mod, g_ffn, wo_b, wrh, wrl, ws1_b, ws3_b, ws2_b)


def _first_max(x, idx, sentinel):
    m = jnp.max(x, axis=0, keepdims=True)
    f = jnp.min(jnp.where(x == m, idx, sentinel), axis=0, keepdims=True)
    return m, f


def _route_body(lg_ref, bias_ref, e_ref, w_ref, r_ref, cnt_ref, cnt_scr, *, tr, ne, ng, topk, topg):
    @pl.when(pl.program_id(0) == 0)
    def _():
        cnt_scr[...] = jnp.zeros(cnt_scr.shape, F32)

    per = ne // ng
    neg = -jnp.inf
    scores = _sigmoid(lg_ref[...])
    sel = scores + bias_ref[...]
    sub = lax.broadcasted_iota(I32, (per, tr), 0)
    gs = []
    for g in range(ng):
        s_g = sel[g * per:(g + 1) * per, :]
        m1, f1 = _first_max(s_g, sub, per)
        m2 = jnp.max(jnp.where(sub == f1, neg, s_g), axis=0, keepdims=True)
        gs.append(m1 + m2)
    gsc = jnp.concatenate(gs, axis=0)
    gi = lax.broadcasted_iota(I32, (ng, tr), 0)
    keep = jnp.zeros((ng, tr), F32)
    for _ in range(topg):
        _, f = _first_max(gsc, gi, ng)
        pick = gi == f
        keep = jnp.where(pick, 1.0, keep)
        gsc = jnp.where(pick, neg, gsc)
    work = jnp.concatenate(
        [jnp.where(keep[g:g + 1, :] > 0.5, sel[g * per:(g + 1) * per, :], neg) for g in range(ng)], axis=0)
    ei = lax.broadcasted_iota(I32, (ne, tr), 0)
    picks, es, ws = [], [], []
    for _ in range(topk):
        _, f = _first_max(work, ei, ne)
        pick = ei == f
        picks.append(pick)
        es.append(f)
        ws.append(jnp.sum(jnp.where(pick, scores, 0.0), axis=0, keepdims=True))
        work = jnp.where(pick, neg, work)
    wsum = ws[0]
    for w in ws[1:]:
        wsum = wsum + w
    scale = ROUTED_SCALE / wsum
    chosen = picks[0]
    for p in picks[1:]:
        chosen = jnp.logical_or(chosen, p)
    chosen_f = chosen.astype(F32)
    t_row = lax.broadcasted_iota(I32, (tr, tr), 0)
    t_col = lax.broadcasted_iota(I32, (tr, tr), 1)
    before = (t_row < t_col).astype(BF16)
    prior = cnt_scr[:, 0:1] + jnp.dot(chosen_f.astype(BF16), before, preferred_element_type=F32)
    rs = [jnp.sum(jnp.where(p, prior, 0.0), axis=0, keepdims=True).astype(I32) for p in picks]
    pad_i = jnp.zeros((SUBLANES - topk, tr), I32)
    pad_f = jnp.zeros((SUBLANES - topk, tr), F32)
    e_ref[...] = jnp.concatenate(es + [pad_i], axis=0)
    w_ref[...] = jnp.concatenate([w * scale for w in ws] + [pad_f], axis=0)
    r_ref[...] = jnp.concatenate(rs + [pad_i], axis=0)
    total = cnt_scr[:, 0:1] + jnp.sum(chosen_f, axis=1, keepdims=True)
    cnt_scr[...] = jnp.broadcast_to(total, cnt_scr.shape)
    cnt_ref[...] = jnp.broadcast_to(total, cnt_ref.shape)


def _route_tile(t):
    return max(m for m in range(LANES, ROUTE_TILE + 1, LANES) if t % m == 0)


def _route(logits_t, bias_col):
    ne, t = logits_t.shape
    tr = _route_tile(t)
    tok = lambda dt: jax.ShapeDtypeStruct((SUBLANES, t), dt)
    tok_spec = pl.BlockSpec((SUBLANES, tr), lambda i: (0, i))
    return pl.pallas_call(
        functools.partial(_route_body, tr=tr, ne=ne, ng=N_GROUPS, topk=TOP_K, topg=TOPK_GROUPS),
        grid=(t // tr,),
        in_specs=[pl.BlockSpec((ne, tr), lambda i: (0, i)), pl.BlockSpec((ne, 1), lambda i: (0, 0))],
        out_specs=[tok_spec, tok_spec, tok_spec, pl.BlockSpec((ne, LANES), lambda i: (0, 0))],
        out_shape=[tok(I32), tok(F32), tok(I32), jax.ShapeDtypeStruct((ne, LANES), F32)],
        scratch_shapes=[pltpu.VMEM((ne, LANES), F32)],
        compiler_params=pltpu.CompilerParams(dimension_semantics=("arbitrary",)),
        name="route",
    )(logits_t, bias_col)


def _dest_body(cnt_ref, e_ref, r_ref, d_ref, ps, *, ne, tr):
    shift = EXPERT_ROWS.bit_length() - 1

    @pl.when(pl.program_id(0) == 0)
    def _():
        def step(j, start):
            ps[j] = start
            return start + lax.shift_left(lax.shift_right_logical(cnt_ref[j] + (EXPERT_ROWS - 1), shift), shift)

        lax.fori_loop(0, ne, step, jnp.int32(0))

    e = e_ref[...]
    base = jnp.zeros(e.shape, I32)
    for j in range(ne):
        base = jnp.where(e == j, ps[j], base)
    dest = base + r_ref[...]
    for m in range(tr // TOKEN_TILE):
        d_ref[SUBLANES * m:SUBLANES * (m + 1), :] = dest[:, TOKEN_TILE * m:TOKEN_TILE * (m + 1)]


def _dest_rows(counts, eidx, rank):
    rows, t = eidx.shape
    ne = counts.shape[0]
    tr = _route_tile(t)
    spec = pl.BlockSpec((rows, tr), lambda i, cnt: (0, i))
    return pl.pallas_call(
        functools.partial(_dest_body, ne=ne, tr=tr),
        grid_spec=pltpu.PrefetchScalarGridSpec(
            num_scalar_prefetch=1, grid=(t // tr,), in_specs=[spec, spec],
            out_specs=pl.BlockSpec((tr // TOKEN_TILE * rows, TOKEN_TILE), lambda i, cnt: (i, 0)),
            scratch_shapes=[pltpu.SMEM((ne,), I32)]),
        out_shape=jax.ShapeDtypeStruct((t // TOKEN_TILE * rows, TOKEN_TILE), I32),
        compiler_params=pltpu.CompilerParams(dimension_semantics=("arbitrary",)),
        name="dest",
    )(counts, eidx, rank)


def _sc_mesh():
    return plsc.VectorSubcoreMesh(core_axis_name="c", subcore_axis_name="s")


def _sc_worker_id():
    return lax.axis_index("s") * SC_CORES + lax.axis_index("c")


def _index_block(dest_ref, chunk, width):
    per_tile = TOKEN_TILE // width
    return dest_ref.at[chunk // per_tile, :, pl.ds((chunk % per_tile) * width, width)]


def _dispatch(hps, dest3, n_rows):
    w = SCATTER_ROWS
    width = hps[0].shape[1]
    bounds = [0]
    for h in hps:
        bounds.append(bounds[-1] + h.shape[0] // w)
    nch = bounds[-1]
    nsrc = len(hps)

    @functools.partial(
        pl.kernel, mesh=_sc_mesh(),
        out_type=jax.ShapeDtypeStruct((n_rows, width), U32),
        scratch_types=[pltpu.VMEM((SUBLANES, w), I32), pltpu.VMEM((w, width), U32), pltpu.SemaphoreType.DMA],
        compiler_params=pltpu.CompilerParams(use_tc_tiling_on_sc=False),
        name="dispatch",
    )
    def run(*refs):
        src_refs, dest_ref, xs_ref = refs[:nsrc], refs[nsrc], refs[nsrc + 1]
        idx_v, rows_v, sem = refs[nsrc + 2:]
        wid = _sc_worker_id()

        @pl.loop(0, pl.cdiv(nch, SC_WORKERS))
        def _(r):
            c = r * SC_WORKERS + wid

            @pl.when(c < nch)
            def _():
                pltpu.sync_copy(_index_block(dest_ref, c, w), idx_v)
                for i, src in enumerate(src_refs):
                    @pl.when(jnp.logical_and(c >= bounds[i], c < bounds[i + 1]))
                    def _(src=src, lo=bounds[i]):
                        pltpu.sync_copy(src.at[pl.ds((c - lo) * w, w)], rows_v)

                copies = [pltpu.async_copy(rows_v, xs_ref.at[idx_v.at[k]], sem) for k in range(TOP_K)]
                for cp in copies:
                    cp.wait()

    return run(*hps, dest3)


def _undispatch(ys, dest3, n_tokens):
    w = GATHER_ROWS
    width = ys.shape[1]
    nch = n_tokens // w

    @functools.partial(
        pl.kernel, mesh=_sc_mesh(),
        out_type=jax.ShapeDtypeStruct((TOP_K, n_tokens, width), U32),
        scratch_types=[pltpu.VMEM((SUBLANES, w), I32), pltpu.VMEM((TOP_K, w, width), U32),
                       pltpu.SemaphoreType.DMA],
        compiler_params=pltpu.CompilerParams(use_tc_tiling_on_sc=False),
        name="undispatch",
    )
    def run(ys_ref, dest_ref, z_ref, idx_v, bufs, sem):
        wid = _sc_worker_id()

        @pl.loop(0, pl.cdiv(nch, SC_WORKERS))
        def _(r):
            c = r * SC_WORKERS + wid

            @pl.when(c < nch)
            def _():
                pltpu.sync_copy(_index_block(dest_ref, c, w), idx_v)
                gathers = [pltpu.async_copy(ys_ref.at[idx_v.at[k]], bufs.at[k], sem) for k in range(TOP_K)]
                for cp in gathers:
                    cp.wait()
                stores = [pltpu.async_copy(bufs.at[k], z_ref.at[k, pl.ds(c * w, w)], sem) for k in range(TOP_K)]
                for cp in stores:
                    cp.wait()

    return run(ys, dest3)


def _expert_body(cnt_ref, xs_ref, w1_ref, w3_ref, w2_ref, ys_ref,
                 w1f, w3f, w2f, w1s, w3s, w2s, xbuf, ybuf, xlo, xhi, sem_x, sem_y, sem_w, *, tm, ne):
    blk_words = tm * ROW_WORDS
    half = ROW_WORDS * LANES
    shift = tm.bit_length() - 1

    def n_blocks_of(e):
        return lax.shift_right_logical(cnt_ref[e] + (tm - 1), shift)

    def next_nonempty(e):
        return lax.while_loop(
            lambda c: jnp.logical_and(c < ne, n_blocks_of(jnp.minimum(c, ne - 1)) == 0), lambda c: c + 1, e)

    nu = lax.fori_loop(0, ne, lambda e, acc: acc + n_blocks_of(e), jnp.int32(0))

    def x_copy(i, slot):
        src = xs_ref.at[pl.ds(pl.multiple_of(i * blk_words, blk_words), blk_words), :]
        return pltpu.make_async_copy(src, xbuf.at[slot], sem_x.at[slot])

    def y_copy(i, slot):
        dst = ys_ref.at[pl.ds(pl.multiple_of(i * blk_words, blk_words), blk_words), :]
        return pltpu.make_async_copy(ybuf.at[slot], dst, sem_y.at[slot])

    def w_copies(e, ws):
        return [pltpu.make_async_copy(src.at[e], dst.at[ws], sem_w.at[ws])
                for src, dst in ((w1_ref, w1f), (w3_ref, w3f), (w2_ref, w2f))]

    for p in range(X_SLOTS - 1):
        @pl.when(p < nu)
        def _(p=p):
            x_copy(p, p).start()

    e_first = next_nonempty(jnp.int32(0))

    @pl.when(e_first < ne)
    def _():
        for cp in w_copies(e_first, 0):
            cp.start()

    def block(i):
        slot = i % X_SLOTS
        yslot = i % Y_SLOTS
        x_copy(i, slot).wait()

        @pl.when(i + (X_SLOTS - 1) < nu)
        def _():
            x_copy(i + (X_SLOTS - 1), (i + (X_SLOTS - 1)) % X_SLOTS).start()

        @pl.when(i >= Y_SLOTS)
        def _():
            y_copy(i - Y_SLOTS, yslot).wait()

        xin = xbuf.at[slot]
        for w in range(ROW_WORDS):
            lo, hi = _unpack_halves(_load_row_word(xin, w, tm))
            xlo[:, LANES * w:LANES * (w + 1)] = lo.astype(BF16)
            xhi[:, LANES * w:LANES * (w + 1)] = hi.astype(BF16)

        def up(wsc):
            return (jnp.dot(xlo[...], wsc[0:half, :], preferred_element_type=F32)
                    + jnp.dot(xhi[...], wsc[half:2 * half, :], preferred_element_type=F32))

        hid = (_silu(up(w1s)) * up(w3s)).astype(BF16)
        y = jnp.dot(hid, w2s[...], preferred_element_type=F32)
        _store_rows(ybuf.at[yslot], _pack_halves(y[:, 0:half], y[:, half:2 * half]))
        y_copy(i, yslot).start()

    def per_expert(e, carry):
        i0, ws = carry
        n = n_blocks_of(e)

        @pl.when(n > 0)
        def _():
            for cp in w_copies(e, ws):
                cp.wait()
            w1s[...] = w1f[ws].astype(BF16)
            w3s[...] = w3f[ws].astype(BF16)
            w2s[...] = w2f[ws].astype(BF16)
            e_next = next_nonempty(e + 1)

            @pl.when(e_next < ne)
            def _():
                for cp in w_copies(e_next, 1 - ws):
                    cp.start()

            def body(j, c):
                block(i0 + j)
                return c

            lax.fori_loop(0, n, body, 0)

        return i0 + n, jnp.where(n > 0, 1 - ws, ws)

    lax.fori_loop(0, ne, per_expert, (jnp.int32(0), jnp.int32(0)))

    for q in range(Y_SLOTS):
        @pl.when(nu > q)
        def _(q=q):
            y_copy(nu - 1 - q, (nu - 1 - q) % Y_SLOTS).wait()


def _experts(counts, xs, w1e, w3e, w2e):
    tm = EXPERT_ROWS
    ne, d, de = w1e.shape
    half = ROW_WORDS * LANES
    anyspec = pl.BlockSpec(memory_space=pl.ANY)
    blk_buf = lambda n: pltpu.VMEM((n, tm * ROW_WORDS, LANES), U32)
    return pl.pallas_call(
        functools.partial(_expert_body, tm=tm, ne=ne),
        grid_spec=pltpu.PrefetchScalarGridSpec(
            num_scalar_prefetch=1,
            grid=(1,),
            in_specs=[anyspec, anyspec, anyspec, anyspec],
            out_specs=anyspec,
            scratch_shapes=[pltpu.VMEM((2, d, de), F32), pltpu.VMEM((2, d, de), F32), pltpu.VMEM((2, de, d), F32),
                            pltpu.VMEM((d, de), BF16), pltpu.VMEM((d, de), BF16), pltpu.VMEM((de, d), BF16),
                            blk_buf(X_SLOTS), blk_buf(Y_SLOTS),
                            pltpu.VMEM((tm, half), BF16), pltpu.VMEM((tm, half), BF16),
                            pltpu.SemaphoreType.DMA((X_SLOTS,)), pltpu.SemaphoreType.DMA((Y_SLOTS,)),
                            pltpu.SemaphoreType.DMA((2,))]),
        out_shape=jax.ShapeDtypeStruct(xs.shape, U32),
        compiler_params=pltpu.CompilerParams(dimension_semantics=("arbitrary",), vmem_limit_bytes=VMEM_LIMIT),
        name="experts",
    )(counts, xs, w1e, w3e, w2e)


def _combine_body(z_ref, x2_ref, gtf_ref, wt_ref, gfin_ref, y_ref, xo, *, td, topk):
    half = ROW_WORDS * LANES
    wt = wt_ref[...].T
    ws = [wt[:, k:k + 1] for k in range(topk)]
    sq = jnp.zeros((td, 1), F32)
    for j in range(ROW_WORDS):
        acc_lo = jnp.zeros((td, LANES), F32)
        acc_hi = jnp.zeros((td, LANES), F32)
        for k in range(topk):
            lo, hi = _unpack_halves(_load_row_word(z_ref.at[k], j, td))
            acc_lo = acc_lo + ws[k] * lo
            acc_hi = acc_hi + ws[k] * hi
        for base, acc in ((0, acc_lo), (half, acc_hi)):
            cols = slice(base + LANES * j, base + LANES * (j + 1))
            x = x2_ref[:, cols] + gtf_ref[:, cols] * acc
            xo[:, cols] = x
            sq = sq + jnp.sum(x * x, axis=-1, keepdims=True)
    rs = lax.rsqrt(sq / (2 * half) + EPS)
    y_ref[...] = xo[...] * rs * gfin_ref[...]


def _combine_body_into(z_ref, x2_ref, gtf_ref, wt_ref, gfin_ref, prev_ref, y_ref, xo, *, td, topk):
    del prev_ref
    _combine_body(z_ref, x2_ref, gtf_ref, wt_ref, gfin_ref, y_ref, xo, td=td, topk=topk)


def _combine(z, token0, wts, x2, mod, g_final, *, rows_per_mod, per_row_mod, b0=0, out_rows=None, into=None):
    t, d = x2.shape
    td = min(t, COMBINE_TILE)
    tile0 = token0 // td
    out_rows = t if out_rows is None else out_rows
    if per_row_mod:
        gtf_spec = pl.BlockSpec((td, d), lambda i: (i, 5))
        out0 = 0
    else:
        tiles_per_mod = rows_per_mod // td
        gtf_spec = pl.BlockSpec((None, None, 1, d), lambda i: (i // tiles_per_mod + b0, 5, 0, 0))
        out0 = b0 * tiles_per_mod
    in_specs = [pl.BlockSpec((TOP_K, td * ROW_WORDS, LANES), lambda i: (0, i + tile0, 0)),
                pl.BlockSpec((td, d), lambda i: (i, 0)),
                gtf_spec,
                pl.BlockSpec((SUBLANES, td), lambda i: (0, i + tile0)),
                pl.BlockSpec((1, d), lambda i: (0, 0))]
    args = [z, x2, mod, wts, g_final]
    body, aliases = _combine_body, {}
    if into is not None:
        in_specs.append(pl.BlockSpec(memory_space=pl.ANY))
        args.append(into)
        body, aliases = _combine_body_into, {len(args) - 1: 0}
    return pl.pallas_call(
        functools.partial(body, td=td, topk=TOP_K),
        grid=(t // td,),
        in_specs=in_specs,
        out_specs=pl.BlockSpec((td, d), lambda i: (i + out0, 0)),
        scratch_shapes=[pltpu.VMEM((td, d), F32)],
        out_shape=jax.ShapeDtypeStruct((out_rows, d), F32),
        input_output_aliases=aliases,
        name="combine",
    )(*args)


def _retention_tables(length, nh, hd):
    c = math.gcd(length, RET_CHUNK)
    log_g = jnp.log(1.0 - 2.0 ** (-5.0 - jnp.arange(nh, dtype=F32)))
    idx = jnp.arange(c, dtype=F32)
    rel = idx[:, None] - idx[None, :]
    mask = jnp.where(rel >= 0, jnp.exp(log_g[:, None, None] * jnp.maximum(rel, 0.0)), 0.0)
    q_decay = jnp.exp(log_g[None, :] * (idx[:, None] + 1.0))
    k_decay = jnp.exp(log_g[None, :] * (c - 1.0 - idx[:, None]))
    chunk_decay = jnp.exp(log_g * c)
    qd = jnp.broadcast_to(q_decay.T[:, :, None], (nh, c, hd))
    kd = jnp.broadcast_to(k_decay.T[:, :, None], (nh, c, hd))
    cd = jnp.broadcast_to(chunk_decay[:, None, None], (nh, hd, hd))
    return mask, qd, kd, cd


def kernel(x_prompt, x_sample, c_prompt, c_sample, state_conv, state_ret, w_ada, b_ada, g_mix, g_ffn, w_in,
           conv_w, conv_b, conv_norm_g, conv_norm_b, ret_norm_g, ret_norm_b, w_out, w_router, router_bias,
           w1, w3, w2, ws1, ws3, ws2, g_final):
    depth = w_ada.shape[0]
    assert depth == 1, "single-layer trunk"
    bp, lp, d = x_prompt.shape
    bs, ls, _ = x_sample.shape
    assert ls == 1
    dc = conv_w.shape[2]
    dr = ret_norm_g.shape[1]
    nh = RET_HEADS
    hd = dr // nh
    assert hd == LANES and lp % 256 == 0 and bs % TOKEN_TILE == 0 and d // 2 == ROW_WORDS * LANES
    ne = w_router.shape[2]
    row = lambda a: a.reshape(1, -1)

    mod = _ada(jnp.concatenate([c_prompt, c_sample], axis=0), w_ada[0], row(b_ada[0]))
    mod_p = mod[:bp].reshape(bp, 6, 1, d)
    mod_s = mod[bp:]

    half = hd // 2
    inv = ROPE_BASE ** (-jnp.arange(half, dtype=F32) / half)
    inv2 = jnp.concatenate([inv, inv]).reshape(1, hd)
    cos_p, sin_p = _rope_tables(inv2, lp, 0)
    cos_s, sin_s = _rope_tables(inv2, SUBLANES, PAST_LEN)

    w_in_b = w_in[0].astype(BF16)
    wo_b = w_out[0].astype(BF16)
    wr_t = w_router[0].T
    wrh = wr_t.astype(BF16)
    wrl = (wr_t - wrh.astype(F32)).astype(BF16)
    ws1_b, ws3_b, ws2_b = ws1[0].astype(BF16), ws3[0].astype(BF16), ws2[0].astype(BF16)
    dims = dict(dc=dc, dr=dr, hd=hd)

    tables = _retention_tables(lp, nh, hd)
    log_g = jnp.log(1.0 - 2.0 ** (-5.0 - jnp.arange(nh, dtype=F32)))
    gam = jnp.broadcast_to(jnp.exp(log_g)[:, None, None], (nh, SUBLANES, hd))
    norm_rows = (row(conv_b[0]), row(conv_norm_g[0]), row(conv_norm_b[0]), row(ret_norm_g[0]), row(ret_norm_b[0]))
    post_w = (row(g_ffn[0]), wo_b, wrh, wrl, ws1_b, ws3_b, ws2_b)
    bias_col = router_bias[0].reshape(ne, 1)
    hw = d // 2
    tm = EXPERT_ROWS

    def pre_prompt(b0, nb):
        glu, q, k, v, sg = _proj(x_prompt, mod_p, row(g_mix[0]), w_in_b, cos_p, sin_p,
                                 per_row_mod=False, b0=b0, nb=nb, **dims)
        cat, ret = _mix(glu, q, k, v, sg, conv_w[0], *norm_rows, tables, nb=nb, length=lp, **dims)
        x2, hp, lg = _post(cat, x_prompt, mod_p, *post_w, per_row_mod=False, b0=b0, nb=nb)
        return glu, ret, x2, hp, lg

    def pre_sample():
        xs3 = x_sample.reshape(1, bs, d)
        glu, q, k, v, sg = _proj(xs3, mod_s, row(g_mix[0]), w_in_b, cos_s, sin_s, per_row_mod=True, **dims)
        cat, ret = _mix1(glu, q, k, v, sg, state_conv[0], state_ret[0], conv_w[0], *norm_rows, gam, **dims)
        x2, hp, lg = _post(cat, xs3, mod_s, *post_w, per_row_mod=True)
        return glu, ret, x2, hp, lg

    def moe(hps, lgs):
        lg = lgs[0] if len(lgs) == 1 else jnp.concatenate(lgs, axis=1)
        tokens = lg.shape[1]
        eidx, wts, rank, cnt = _route(lg, bias_col)
        counts = cnt[:, 0].astype(I32)
        n_rows = -(-(tokens * TOP_K + ne * (tm - 1)) // tm) * tm
        dest3 = _dest_rows(counts, eidx, rank).reshape(tokens // TOKEN_TILE, SUBLANES, TOKEN_TILE)
        xs = _dispatch([h.reshape(-1, hw) for h in hps], dest3, n_rows)
        ys = _experts(counts, xs.reshape(n_rows * ROW_WORDS, LANES), w1[0], w3[0], w2[0])
        z = _undispatch(ys.reshape(n_rows, hw), dest3, tokens).reshape(TOP_K, tokens * ROW_WORDS, LANES)
        return z, wts

    nb0 = bp // 2
    nb1 = bp - nb0
    glu_0, ret_0, x2_0, hp_0, lg_0 = pre_prompt(0, nb0)
    z_0, wts_0 = moe([hp_0], [lg_0])
    glu_1, ret_1, x2_1, hp_1, lg_1 = pre_prompt(nb0, nb1)
    glu_s, ret_s, x2_s, hp_s, lg_s = pre_sample()
    z_1, wts_1 = moe([hp_1, hp_s], [lg_1, lg_s])
    y_p = _combine(z_0, 0, wts_0, x2_0, mod_p, row(g_final), rows_per_mod=lp, per_row_mod=False,
                   out_rows=bp * lp)
    y_p = _combine(z_1, 0, wts_1, x2_1, mod_p, row(g_final), rows_per_mod=lp, per_row_mod=False,
                   b0=nb0, out_rows=bp * lp, into=y_p)
    y_s = _combine(z_1, nb1 * lp, wts_1, x2_s, mod_s, row(g_final), rows_per_mod=bs, per_row_mod=True)
    ret_p = jnp.concatenate([ret_0, ret_1], axis=0)

    tail = lambda g, n: g.reshape(n, lp, dc)[:, lp - CONV_BUF:, :]
    new_conv_p = jnp.concatenate([tail(glu_0, nb0), tail(glu_1, nb1)], axis=0)
    new_conv_s = jnp.concatenate([state_conv[0][:, 1:, :], glu_s[:, None, :]], axis=1)
    return (y_p.reshape(bp, lp, d), y_s.reshape(bs, ls, d), new_conv_p[None], ret_p[None],
            new_conv_s[None], ret_s[None])
```

```python
import functools
import math

import jax
import jax.numpy as jnp
from jax import lax
from jax.experimental import pallas as pl
from jax.experimental.pallas import tpu as pltpu
from jax.experimental.pallas import tpu_sc as plsc

F32 = jnp.float32
BF16 = jnp.bfloat16
U32 = jnp.uint32
I32 = jnp.int32

EPS = 1e-6
PAST_LEN = 16384
RET_HEADS = 4
RET_CHUNK = 128
CONV_WIDTH = 31
CONV_BUF = CONV_WIDTH - 1
ROPE_BASE = 10000.0
N_EXPERTS = 64
TOP_K = 6
N_GROUPS = 8
TOPK_GROUPS = 4
ROUTED_SCALE = 2.5

LANES = 128
SUBLANES = 8
CONV_PAD = 32
EXPERT_ROWS = 256
ROUTE_TILE = 640
TOKEN_TILE = 128
COMBINE_TILE = 256
COMBINE_ROWS = 32
VMEM_LIMIT = 56 * 1024 * 1024
SC_CORES = 2
SC_SUBCORES = 16
SC_WORKERS = SC_CORES * SC_SUBCORES
SCATTER_ROWS = 64
GATHER_ROWS = 32
ROW_WORDS = 4
X_SLOTS = 4
Y_SLOTS = 3

HI_MASK = 0xFFFF0000


def _sigmoid(x):
    return jax.nn.sigmoid(x)


def _silu(x):
    return x * jax.nn.sigmoid(x)


def _pack_halves(lo, hi):
    lo_u = lax.bitcast_convert_type(lo.astype(BF16).astype(F32), U32) >> 16
    hi_u = lax.bitcast_convert_type(hi.astype(BF16).astype(F32), U32) & jnp.uint32(HI_MASK)
    return hi_u | lo_u


def _unpack_halves(p):
    lo = lax.bitcast_convert_type(p << 16, F32)
    hi = lax.bitcast_convert_type(p & jnp.uint32(HI_MASK), F32)
    return lo, hi


def _store_rows(ref, x):
    rows = x.shape[0]
    for j in range(ROW_WORDS):
        ref[pl.ds(j, rows, stride=ROW_WORDS), :] = x[:, LANES * j:LANES * (j + 1)]


def _load_row_word(ref, j, rows):
    return ref[pl.ds(j, rows, stride=ROW_WORDS), :]


def _ada_body(c_ref, w_ref, b_ref, o_ref):
    s = _silu(c_ref[...]).astype(BF16)
    o_ref[...] = jnp.dot(s, w_ref[...].astype(BF16), preferred_element_type=F32) + b_ref[...]


def _ada(c_all, w_ada, b_ada):
    rows, d = c_all.shape
    n = w_ada.shape[1]
    tn = 2048
    return pl.pallas_call(
        _ada_body,
        grid=(n // tn,),
        in_specs=[
            pl.BlockSpec((rows, d), lambda j: (0, 0)),
            pl.BlockSpec((d, tn), lambda j: (0, j)),
            pl.BlockSpec((1, tn), lambda j: (0, j)),
        ],
        out_specs=pl.BlockSpec((rows, tn), lambda j: (0, j)),
        out_shape=jax.ShapeDtypeStruct((rows, n), F32),
        compiler_params=pltpu.CompilerParams(vmem_limit_bytes=VMEM_LIMIT),
        name="ada",
    )(c_all, w_ada, b_ada)


def _rope_body(inv_ref, cos_ref, sin_ref, *, pos0, tl, half):
    row = lax.broadcasted_iota(I32, (tl, LANES), 0) + pl.program_id(0) * tl
    ang = (row.astype(F32) + pos0) * inv_ref[...]
    lane = lax.broadcasted_iota(I32, (tl, LANES), 1)
    s = jnp.sin(ang)
    cos_ref[...] = jnp.cos(ang)
    sin_ref[...] = jnp.where(lane < half, -s, s)


def _rope_tables(inv2, rows, pos0):
    tl = min(rows, 256)
    return pl.pallas_call(
        functools.partial(_rope_body, pos0=float(pos0), tl=tl, half=LANES // 2),
        grid=(rows // tl,),
        in_specs=[pl.BlockSpec((1, LANES), lambda i: (0, 0))],
        out_specs=[pl.BlockSpec((tl, LANES), lambda i: (i, 0))] * 2,
        out_shape=[jax.ShapeDtypeStruct((rows, LANES), F32)] * 2,
        name="rope",
    )(inv2)


def _modulated_rmsnorm(x, g, sc, sh):
    ms = jnp.mean(x * x, axis=-1, keepdims=True)
    h = x * lax.rsqrt(ms + EPS) * g
    return h * (1.0 + sc) + sh


def _proj_body(x_ref, sh_ref, sc_ref, g_ref, w_ref, cos_ref, sin_ref,
               glu_ref, q_ref, k_ref, v_ref, sg_ref, *, dc, dr, hd, rope_rows):
    hb = _modulated_rmsnorm(x_ref[...], g_ref[...], sc_ref[...], sh_ref[...]).astype(BF16)

    def proj(lo, n):
        return jnp.dot(hb, w_ref[:, lo:lo + n], preferred_element_type=F32)

    glu_ref[...] = proj(0, dc) * _sigmoid(proj(dc, dc))
    cos = cos_ref[...] if rope_rows else cos_ref[0:1, :]
    sin = sin_ref[...] if rope_rows else sin_ref[0:1, :]
    for ref, lo, scale in ((q_ref, 2 * dc, hd ** -0.5), (k_ref, 2 * dc + dr, None)):
        t = proj(lo, dr)
        for hh in range(dr // hd):
            th = t[:, hh * hd:(hh + 1) * hd]
            r = th * cos + pltpu.roll(th, hd // 2, 1) * sin
            if scale is not None:
                r = r * scale
            ref[:, hh * hd:(hh + 1) * hd] = r.astype(BF16)
    v_ref[...] = proj(2 * dc + 2 * dr, dr).astype(BF16)
    sg_ref[...] = _silu(proj(2 * dc + 3 * dr, dr)).astype(BF16)


def _proj(x, mod, g_mix, w_in_b, cos2, sin2, *, dc, dr, hd, per_row_mod, b0=0, nb=None):
    nb_all, length, d = x.shape
    nb = nb_all if nb is None else nb
    tl = min(length, 256)
    x2 = x.reshape(nb_all * length, d)
    nl = length // tl
    if per_row_mod:
        mod_spec = lambda j: pl.BlockSpec((tl, d), lambda b, l: (l, j))
        rope_spec = pl.BlockSpec((SUBLANES, LANES), lambda b, l: (0, 0))
    else:
        mod_spec = lambda j: pl.BlockSpec((None, None, 1, d), lambda b, l: (b + b0, j, 0, 0))
        rope_spec = pl.BlockSpec((tl, LANES), lambda b, l: (l, 0))
    row_spec = lambda w: pl.BlockSpec((tl, w), lambda b, l: (b * nl + l, 0))
    t = nb * length
    outs = pl.pallas_call(
        functools.partial(_proj_body, dc=dc, dr=dr, hd=hd, rope_rows=not per_row_mod),
        grid=(nb, nl),
        in_specs=[
            pl.BlockSpec((tl, d), lambda b, l: ((b + b0) * nl + l, 0)), mod_spec(0), mod_spec(1),
            pl.BlockSpec((1, d), lambda b, l: (0, 0)),
            pl.BlockSpec(w_in_b.shape, lambda b, l: (0, 0)),
            rope_spec, rope_spec,
        ],
        out_specs=[row_spec(dc), row_spec(dr), row_spec(dr), row_spec(dr), row_spec(dr)],
        out_shape=[
            jax.ShapeDtypeStruct((t, dc), F32),
            jax.ShapeDtypeStruct((t, dr), BF16),
            jax.ShapeDtypeStruct((t, dr), BF16),
            jax.ShapeDtypeStruct((t, dr), BF16),
            jax.ShapeDtypeStruct((t, dr), BF16),
        ],
        compiler_params=pltpu.CompilerParams(vmem_limit_bytes=VMEM_LIMIT),
        name="proj",
    )(x2, mod, mod, g_mix, w_in_b, cos2, sin2)
    return outs


def _layernorm_silu(c, g, b):
    mu = jnp.mean(c, axis=-1, keepdims=True)
    d = c - mu
    var = jnp.mean(d * d, axis=-1, keepdims=True)
    return _silu(d * lax.rsqrt(var + EPS) * g + b)


def _groupnorm(o, g, b):
    mu = jnp.mean(o, axis=-1, keepdims=True)
    d = o - mu
    var = jnp.mean(d * d, axis=-1, keepdims=True)
    return d * lax.rsqrt(var + EPS) * g + b


def _mix_body(glu_ref, q_ref, k_ref, v_ref, sg_ref, cw_ref, cb_ref, lng_ref, lnb_ref, rg_ref, rb_ref,
              mask_ref, qd_ref, kd_ref, cd_ref, cat_ref, st_ref, buf, cscr, *, tl, dc, hd, nh, chunk):
    nslab = dc // LANES

    @pl.when(pl.program_id(1) == 0)
    def _():
        buf[:, 0:CONV_PAD, :] = jnp.zeros((nslab, CONV_PAD, LANES), F32)
        st_ref[...] = jnp.zeros(st_ref.shape, F32)

    for j in range(nslab):
        buf[j, CONV_PAD:CONV_PAD + tl, :] = glu_ref[:, LANES * j:LANES * (j + 1)]
    first = CONV_PAD - CONV_BUF
    rows_per_iter = 8 * SUBLANES
    for j in range(nslab):
        cols = slice(LANES * j, LANES * (j + 1))
        wv = [jnp.broadcast_to(cw_ref[t:t + 1, cols], (SUBLANES, LANES)) for t in range(CONV_WIDTH)]
        bias = jnp.broadcast_to(cb_ref[0:1, cols], (SUBLANES, LANES))

        def body(r, carry, j=j, cols=cols, wv=wv, bias=bias):
            base = pl.multiple_of(r * rows_per_iter, rows_per_iter)
            for u in range(rows_per_iter // SUBLANES):
                acc = bias
                for t in range(CONV_WIDTH):
                    acc = acc + wv[t] * buf[j, pl.ds(base + (u * SUBLANES + first + t), SUBLANES), :]
                cscr[pl.ds(base + u * SUBLANES, SUBLANES), cols] = acc
            return carry

        lax.fori_loop(0, tl // rows_per_iter, body, 0)
    for j in range(nslab):
        buf[j, 0:CONV_PAD, :] = buf[j, tl:tl + CONV_PAD, :]
    cat_ref[:, 0:dc] = _layernorm_silu(cscr[...], lng_ref[...], lnb_ref[...]).astype(BF16)

    nt = (((1,), (1,)), ((), ()))
    tn = (((0,), (0,)), ((), ()))
    for c in range(tl // chunk):
        rows = slice(c * chunk, (c + 1) * chunk)
        for hh in range(nh):
            cols = slice(hh * hd, (hh + 1) * hd)
            qh = q_ref[rows, cols]
            kh = k_ref[rows, cols]
            vh = v_ref[rows, cols]
            s = st_ref[0, hh]
            scores = lax.dot_general(qh, kh, nt, preferred_element_type=F32) * mask_ref[hh]
            inner = jnp.dot(scores.astype(BF16), vh, preferred_element_type=F32)
            qd = (qh.astype(F32) * qd_ref[hh]).astype(BF16)
            cross = jnp.dot(qd, s.astype(BF16), preferred_element_type=F32)
            kd = (kh.astype(F32) * kd_ref[hh]).astype(BF16)
            st_ref[0, hh] = cd_ref[hh] * s + lax.dot_general(kd, vh, tn, preferred_element_type=F32)
            o = _groupnorm(inner + cross, rg_ref[0:1, cols], rb_ref[0:1, cols])
            cat_ref[rows, dc + hh * hd:dc + (hh + 1) * hd] = (o * sg_ref[rows, cols].astype(F32)).astype(BF16)


def _mix(glu, q, k, v, sg, conv_w, conv_b, ln_g, ln_b, rg, rb, tables, *, nb, length, dc, dr, hd):
    nh = dr // hd
    chunk = math.gcd(length, RET_CHUNK)
    tl = min(length, 256)
    nl = length // tl
    mask, qd, kd, cd = tables
    row_spec = lambda w: pl.BlockSpec((tl, w), lambda b, l: (b * nl + l, 0))
    full = lambda a: pl.BlockSpec(a.shape, lambda b, l: (0,) * a.ndim)
    cat, st = pl.pallas_call(
        functools.partial(_mix_body, tl=tl, dc=dc, hd=hd, nh=nh, chunk=chunk),
        grid=(nb, nl),
        in_specs=[row_spec(dc), row_spec(dr), row_spec(dr), row_spec(dr), row_spec(dr),
                  full(conv_w), full(conv_b), full(ln_g), full(ln_b), full(rg), full(rb),
                  full(mask), full(qd), full(kd), full(cd)],
        out_specs=[row_spec(dc + dr), pl.BlockSpec((1, nh, hd, hd), lambda b, l: (b, 0, 0, 0))],
        out_shape=[jax.ShapeDtypeStruct((nb * length, dc + dr), BF16),
                   jax.ShapeDtypeStruct((nb, nh, hd, hd), F32)],
        scratch_shapes=[pltpu.VMEM((dc // LANES, tl + CONV_PAD, LANES), F32),
                        pltpu.VMEM((tl, dc), F32)],
        compiler_params=pltpu.CompilerParams(dimension_semantics=("arbitrary", "arbitrary")),
        name="mix",
    )(glu, q, k, v, sg, conv_w, conv_b, ln_g, ln_b, rg, rb, mask, qd, kd, cd)
    return cat, st


def _mix1_body(glu_ref, q_ref, k_ref, v_ref, sg_ref, sc_ref, s0_ref, cw_ref, cb_ref, lng_ref, lnb_ref,
               rg_ref, rb_ref, gam_ref, cat_ref, st_ref, nc_ref, cscr, oscr, qf, kf, vf, *, tb, dc, hd, nh):
    w_hist = cw_ref[0:CONV_BUF, :]
    w_last = cw_ref[CONV_BUF:CONV_WIDTH, :]
    for bb in range(tb):
        hist = jnp.sum(sc_ref[bb] * w_hist, axis=0, keepdims=True)
        cscr[bb:bb + 1, :] = hist + glu_ref[bb:bb + 1, :] * w_last + cb_ref[...]
        nc_ref[bb, 0:CONV_BUF - 1, :] = sc_ref[bb, 1:CONV_BUF, :]
        nc_ref[bb, CONV_BUF - 1:CONV_BUF, :] = glu_ref[bb:bb + 1, :]
    cat_ref[:, 0:dc] = _layernorm_silu(cscr[...], lng_ref[...], lnb_ref[...]).astype(BF16)

    tn = (((0,), (0,)), ((), ()))
    rowid = lax.broadcasted_iota(I32, (tb, hd), 0)
    qf[...] = q_ref[...].astype(F32)
    kf[...] = k_ref[...].astype(F32)
    vf[...] = v_ref[...].astype(F32)
    for hh in range(nh):
        cols = slice(hh * hd, (hh + 1) * hd)
        qa = q_ref[:, cols]
        ka = k_ref[:, cols]
        gam = gam_ref[hh, 0:1, :]
        for bb in range(tb):
            onehot = (rowid == bb).astype(BF16)
            qcol = lax.dot_general(qa, onehot, tn, preferred_element_type=F32)
            kcol = lax.dot_general(ka, onehot, tn, preferred_element_type=F32)
            s0 = s0_ref[bb, hh]
            qrow = qf[bb:bb + 1, cols]
            krow = kf[bb:bb + 1, cols]
            vrow = vf[bb:bb + 1, cols]
            qk = jnp.sum(qrow * krow, axis=-1, keepdims=True)
            cross = gam * jnp.sum(qcol * s0, axis=0, keepdims=True)
            st_ref[bb, hh] = gam * s0 + kcol * vrow
            oscr[bb:bb + 1, cols] = qk * vrow + cross
    for hh in range(nh):
        cols = slice(hh * hd, (hh + 1) * hd)
        o = _groupnorm(oscr[:, cols], rg_ref[0:1, cols], rb_ref[0:1, cols])
        cat_ref[:, dc + hh * hd:dc + (hh + 1) * hd] = (o * sg_ref[:, cols].astype(F32)).astype(BF16)


def _mix1(glu, q, k, v, sg, state_conv, state_ret, conv_w, conv_b, ln_g, ln_b, rg, rb, gam, *, dc, dr, hd):
    nb = glu.shape[0]
    nh = dr // hd
    tb = 16
    row_spec = lambda w: pl.BlockSpec((tb, w), lambda i: (i, 0))
    full = lambda a: pl.BlockSpec(a.shape, lambda i: (0,) * a.ndim)
    st_spec = pl.BlockSpec((tb, nh, hd, hd), lambda i: (i, 0, 0, 0))
    conv_spec = pl.BlockSpec((tb, CONV_BUF, dc), lambda i: (i, 0, 0))
    cat, st, new_conv = pl.pallas_call(
        functools.partial(_mix1_body, tb=tb, dc=dc, hd=hd, nh=nh),
        grid=(nb // tb,),
        in_specs=[row_spec(dc), row_spec(dr), row_spec(dr), row_spec(dr), row_spec(dr), conv_spec, st_spec,
                  full(conv_w), full(conv_b), full(ln_g), full(ln_b), full(rg), full(rb), full(gam)],
        out_specs=[row_spec(dc + dr), st_spec, conv_spec],
        out_shape=[jax.ShapeDtypeStruct((nb, dc + dr), BF16),
                   jax.ShapeDtypeStruct((nb, nh, hd, hd), F32),
                   jax.ShapeDtypeStruct((nb, CONV_BUF, dc), F32)],
        scratch_shapes=[pltpu.VMEM((tb, dc), F32)] + [pltpu.VMEM((tb, dr), F32)] * 4,
        compiler_params=pltpu.CompilerParams(vmem_limit_bytes=VMEM_LIMIT),
        name="mix1",
    )(glu, q, k, v, sg, state_conv, state_ret, conv_w, conv_b, ln_g, ln_b, rg, rb, gam)
    return cat, st, new_conv


def _post_body(cat_ref, x_ref, gtm_ref, scf_ref, shf_ref, gtf_ref, g_ref, wo_ref, wrh_ref, wrl_ref,
               ws1_ref, ws3_ref, ws2_ref, x2_ref, hp_ref, lg_ref):
    d = x_ref.shape[1]
    y = jnp.dot(cat_ref[...], wo_ref[...], preferred_element_type=F32)
    x1 = x_ref[...] + gtm_ref[...] * y
    h = _modulated_rmsnorm(x1, g_ref[...], scf_ref[...], shf_ref[...])
    hb = h.astype(BF16)
    _store_rows(hp_ref, _pack_halves(h[:, 0:d // 2], h[:, d // 2:d]))
    hl = (h - hb.astype(F32)).astype(BF16)
    nt = (((1,), (1,)), ((), ()))
    lg_ref[...] = (lax.dot_general(wrh_ref[...], hb, nt, preferred_element_type=F32)
                   + lax.dot_general(wrh_ref[...], hl, nt, preferred_element_type=F32)
                   + lax.dot_general(wrl_ref[...], hb, nt, preferred_element_type=F32))
    s1 = jnp.dot(hb, ws1_ref[...], preferred_element_type=F32)
    s3 = jnp.dot(hb, ws3_ref[...], preferred_element_type=F32)
    shared = jnp.dot((_silu(s1) * s3).astype(BF16), ws2_ref[...], preferred_element_type=F32)
    x2_ref[...] = x1 + gtf_ref[...] * shared


def _post(cat, x, mod, g_ffn, wo_b, wrh, wrl, ws1_b, ws3_b, ws2_b, *, per_row_mod, b0=0, nb=None):
    nb_all, length, d = x.shape
    nb = nb_all if nb is None else nb
    tl = min(length, 512)
    nl = length // tl
    t = nb * length
    ne = wrh.shape[0]
    x2d = x.reshape(nb_all * length, d)
    if per_row_mod:
        mod_spec = lambda j: pl.BlockSpec((tl, d), lambda b, l: (l, j))
    else:
        mod_spec = lambda j: pl.BlockSpec((None, None, 1, d), lambda b, l: (b + b0, j, 0, 0))
    row_spec = lambda w: pl.BlockSpec((tl, w), lambda b, l: (b * nl + l, 0))
    full = lambda a: pl.BlockSpec(a.shape, lambda b, l: (0,) * a.ndim)
    return pl.pallas_call(
        _post_body,
        grid=(nb, nl),
        in_specs=[row_spec(d), pl.BlockSpec((tl, d), lambda b, l: ((b + b0) * nl + l, 0)),
                  mod_spec(2), mod_spec(4), mod_spec(3), mod_spec(5),
                  full(g_ffn), full(wo_b), full(wrh), full(wrl), full(ws1_b), full(ws3_b), full(ws2_b)],
        out_specs=[row_spec(d),
                   pl.BlockSpec((tl * ROW_WORDS, LANES), lambda b, l: (b * nl + l, 0)),
                   pl.BlockSpec((ne, tl), lambda b, l: (0, b * nl + l))],
        out_shape=[jax.ShapeDtypeStruct((t, d), F32),
                   jax.ShapeDtypeStruct((t * ROW_WORDS, LANES), U32),
                   jax.ShapeDtypeStruct((ne, t), F32)],
        compiler_params=pltpu.CompilerParams(vmem_limit_bytes=VMEM_LIMIT),
        name="post",
    )(cat, x2d, mod, mod, mod, mod, g_ffn, wo_b, wrh, wrl, ws1_b, ws3_b, ws2_b)


def _first_max(x, idx, sentinel):
    m = jnp.max(x, axis=0, keepdims=True)
    f = jnp.min(jnp.where(x == m, idx, sentinel), axis=0, keepdims=True)
    return m, f


def _route_body(lg_ref, bias_ref, e_ref, w_ref, r_ref, cnt_ref, cnt_scr, *, tr, ne, ng, topk, topg):
    @pl.when(pl.program_id(0) == 0)
    def _():
        cnt_scr[...] = jnp.zeros(cnt_scr.shape, F32)

    per = ne // ng
    neg = -jnp.inf
    scores = _sigmoid(lg_ref[...])
    sel = scores + bias_ref[...]
    sub = lax.broadcasted_iota(I32, (per, tr), 0)
    gs = []
    for g in range(ng):
        s_g = sel[g * per:(g + 1) * per, :]
        m1, f1 = _first_max(s_g, sub, per)
        m2 = jnp.max(jnp.where(sub == f1, neg, s_g), axis=0, keepdims=True)
        gs.append(m1 + m2)
    gsc = jnp.concatenate(gs, axis=0)
    gi = lax.broadcasted_iota(I32, (ng, tr), 0)
    keep = jnp.zeros((ng, tr), F32)
    for _ in range(topg):
        _, f = _first_max(gsc, gi, ng)
        pick = gi == f
        keep = jnp.where(pick, 1.0, keep)
        gsc = jnp.where(pick, neg, gsc)
    work = jnp.concatenate(
        [jnp.where(keep[g:g + 1, :] > 0.5, sel[g * per:(g + 1) * per, :], neg) for g in range(ng)], axis=0)
    ei = lax.broadcasted_iota(I32, (ne, tr), 0)
    picks, es, ws = [], [], []
    for _ in range(topk):
        _, f = _first_max(work, ei, ne)
        pick = ei == f
        picks.append(pick)
        es.append(f)
        ws.append(jnp.sum(jnp.where(pick, scores, 0.0), axis=0, keepdims=True))
        work = jnp.where(pick, neg, work)
    wsum = ws[0]
    for w in ws[1:]:
        wsum = wsum + w
    scale = ROUTED_SCALE / wsum
    chosen = picks[0]
    for p in picks[1:]:
        chosen = jnp.logical_or(chosen, p)
    chosen_f = chosen.astype(F32)
    t_row = lax.broadcasted_iota(I32, (tr, tr), 0)
    t_col = lax.broadcasted_iota(I32, (tr, tr), 1)
    before = (t_row < t_col).astype(BF16)
    prior = cnt_scr[:, 0:1] + jnp.dot(chosen_f.astype(BF16), before, preferred_element_type=F32)
    rs = [jnp.sum(jnp.where(p, prior, 0.0), axis=0, keepdims=True).astype(I32) for p in picks]
    pad_i = jnp.zeros((SUBLANES - topk, tr), I32)
    pad_f = jnp.zeros((SUBLANES - topk, tr), F32)
    e_ref[...] = jnp.concatenate(es + [pad_i], axis=0)
    w_ref[...] = jnp.concatenate([w * scale for w in ws] + [pad_f], axis=0)
    r_ref[...] = jnp.concatenate(rs + [pad_i], axis=0)
    total = cnt_scr[:, 0:1] + jnp.sum(chosen_f, axis=1, keepdims=True)
    cnt_scr[...] = jnp.broadcast_to(total, cnt_scr.shape)
    cnt_ref[...] = jnp.broadcast_to(total, cnt_ref.shape)


def _route_tile(t):
    return max(m for m in range(LANES, ROUTE_TILE + 1, LANES) if t % m == 0)


def _route(logits_t, bias_col):
    ne, t = logits_t.shape
    tr = _route_tile(t)
    tok = lambda dt: jax.ShapeDtypeStruct((SUBLANES, t), dt)
    tok_spec = pl.BlockSpec((SUBLANES, tr), lambda i: (0, i))
    return pl.pallas_call(
        functools.partial(_route_body, tr=tr, ne=ne, ng=N_GROUPS, topk=TOP_K, topg=TOPK_GROUPS),
        grid=(t // tr,),
        in_specs=[pl.BlockSpec((ne, tr), lambda i: (0, i)), pl.BlockSpec((ne, 1), lambda i: (0, 0))],
        out_specs=[tok_spec, tok_spec, tok_spec, pl.BlockSpec((ne, LANES), lambda i: (0, 0))],
        out_shape=[tok(I32), tok(F32), tok(I32), jax.ShapeDtypeStruct((ne, LANES), F32)],
        scratch_shapes=[pltpu.VMEM((ne, LANES), F32)],
        compiler_params=pltpu.CompilerParams(dimension_semantics=("arbitrary",)),
        name="route",
    )(logits_t, bias_col)


def _dest_body(cnt_ref, e_ref, r_ref, d_ref, ps, *, ne, tr):
    shift = EXPERT_ROWS.bit_length() - 1

    @pl.when(pl.program_id(0) == 0)
    def _():
        def step(j, start):
            ps[j] = start
            return start + lax.shift_left(lax.shift_right_logical(cnt_ref[j] + (EXPERT_ROWS - 1), shift), shift)

        lax.fori_loop(0, ne, step, jnp.int32(0))

    e = e_ref[...]
    base = jnp.zeros(e.shape, I32)
    for j in range(ne):
        base = jnp.where(e == j, ps[j], base)
    dest = base + r_ref[...]
    for m in range(tr // TOKEN_TILE):
        d_ref[SUBLANES * m:SUBLANES * (m + 1), :] = dest[:, TOKEN_TILE * m:TOKEN_TILE * (m + 1)]


def _dest_rows(counts, eidx, rank):
    rows, t = eidx.shape
    ne = counts.shape[0]
    tr = _route_tile(t)
    spec = pl.BlockSpec((rows, tr), lambda i, cnt: (0, i))
    return pl.pallas_call(
        functools.partial(_dest_body, ne=ne, tr=tr),
        grid_spec=pltpu.PrefetchScalarGridSpec(
            num_scalar_prefetch=1, grid=(t // tr,), in_specs=[spec, spec],
            out_specs=pl.BlockSpec((tr // TOKEN_TILE * rows, TOKEN_TILE), lambda i, cnt: (i, 0)),
            scratch_shapes=[pltpu.SMEM((ne,), I32)]),
        out_shape=jax.ShapeDtypeStruct((t // TOKEN_TILE * rows, TOKEN_TILE), I32),
        compiler_params=pltpu.CompilerParams(dimension_semantics=("arbitrary",)),
        name="dest",
    )(counts, eidx, rank)


def _sc_mesh():
    return plsc.VectorSubcoreMesh(core_axis_name="c", subcore_axis_name="s")


def _sc_worker_id():
    return lax.axis_index("s") * SC_CORES + lax.axis_index("c")


def _index_block(dest_ref, chunk, width):
    per_tile = TOKEN_TILE // width
    return dest_ref.at[chunk // per_tile, :, pl.ds((chunk % per_tile) * width, width)]


def _dispatch(hps, dest3, n_rows):
    w = SCATTER_ROWS
    width = hps[0].shape[1]
    bounds = [0]
    for h in hps:
        bounds.append(bounds[-1] + h.shape[0] // w)
    nch = bounds[-1]
    nsrc = len(hps)

    @functools.partial(
        pl.kernel, mesh=_sc_mesh(),
        out_type=jax.ShapeDtypeStruct((n_rows, width), U32),
        scratch_types=[pltpu.VMEM((SUBLANES, w), I32), pltpu.VMEM((w, width), U32), pltpu.SemaphoreType.DMA],
        compiler_params=pltpu.CompilerParams(use_tc_tiling_on_sc=False),
        name="dispatch",
    )
    def run(*refs):
        src_refs, dest_ref, xs_ref = refs[:nsrc], refs[nsrc], refs[nsrc + 1]
        idx_v, rows_v, sem = refs[nsrc + 2:]
        wid = _sc_worker_id()

        @pl.loop(0, pl.cdiv(nch, SC_WORKERS))
        def _(r):
            c = r * SC_WORKERS + wid

            @pl.when(c < nch)
            def _():
                pltpu.sync_copy(_index_block(dest_ref, c, w), idx_v)
                for i, src in enumerate(src_refs):
                    @pl.when(jnp.logical_and(c >= bounds[i], c < bounds[i + 1]))
                    def _(src=src, lo=bounds[i]):
                        pltpu.sync_copy(src.at[pl.ds((c - lo) * w, w)], rows_v)

                copies = [pltpu.async_copy(rows_v, xs_ref.at[idx_v.at[k]], sem) for k in range(TOP_K)]
                for cp in copies:
                    cp.wait()

    return run(*hps, dest3)


def _undispatch(ys, dest3, n_tokens):
    w = GATHER_ROWS
    width = ys.shape[1]
    nch = n_tokens // w

    @functools.partial(
        pl.kernel, mesh=_sc_mesh(),
        out_type=jax.ShapeDtypeStruct((TOP_K, n_tokens, width), U32),
        scratch_types=[pltpu.VMEM((SUBLANES, w), I32), pltpu.VMEM((TOP_K, w, width), U32),
                       pltpu.SemaphoreType.DMA],
        compiler_params=pltpu.CompilerParams(use_tc_tiling_on_sc=False),
        name="undispatch",
    )
    def run(ys_ref, dest_ref, z_ref, idx_v, bufs, sem):
        wid = _sc_worker_id()

        @pl.loop(0, pl.cdiv(nch, SC_WORKERS))
        def _(r):
            c = r * SC_WORKERS + wid

            @pl.when(c < nch)
            def _():
                pltpu.sync_copy(_index_block(dest_ref, c, w), idx_v)
                gathers = [pltpu.async_copy(ys_ref.at[idx_v.at[k]], bufs.at[k], sem) for k in range(TOP_K)]
                for cp in gathers:
                    cp.wait()
                stores = [pltpu.async_copy(bufs.at[k], z_ref.at[k, pl.ds(c * w, w)], sem) for k in range(TOP_K)]
                for cp in stores:
                    cp.wait()

    return run(ys, dest3)


def _expert_body(cnt_ref, xs_ref, w1_ref, w3_ref, w2_ref, ys_ref,
                 w1f, w3f, w2f, w1s, w3s, w2s, xbuf, ybuf, xlo, xhi, sem_x, sem_y, sem_w, *, tm, ne):
    blk_words = tm * ROW_WORDS
    half = ROW_WORDS * LANES
    shift = tm.bit_length() - 1

    def n_blocks_of(e):
        return lax.shift_right_logical(cnt_ref[e] + (tm - 1), shift)

    def next_nonempty(e):
        return lax.while_loop(
            lambda c: jnp.logical_and(c < ne, n_blocks_of(jnp.minimum(c, ne - 1)) == 0), lambda c: c + 1, e)

    nu = lax.fori_loop(0, ne, lambda e, acc: acc + n_blocks_of(e), jnp.int32(0))

    def x_copy(i, slot):
        src = xs_ref.at[pl.ds(pl.multiple_of(i * blk_words, blk_words), blk_words), :]
        return pltpu.make_async_copy(src, xbuf.at[slot], sem_x.at[slot])

    def y_copy(i, slot):
        dst = ys_ref.at[pl.ds(pl.multiple_of(i * blk_words, blk_words), blk_words), :]
        return pltpu.make_async_copy(ybuf.at[slot], dst, sem_y.at[slot])

    def w_copies(e, ws):
        return [pltpu.make_async_copy(src.at[e], dst.at[ws], sem_w.at[ws])
                for src, dst in ((w1_ref, w1f), (w3_ref, w3f), (w2_ref, w2f))]

    for p in range(X_SLOTS - 1):
        @pl.when(p < nu)
        def _(p=p):
            x_copy(p, p).start()

    e_first = next_nonempty(jnp.int32(0))

    @pl.when(e_first < ne)
    def _():
        for cp in w_copies(e_first, 0):
            cp.start()

    def block(i):
        slot = i % X_SLOTS
        yslot = i % Y_SLOTS
        x_copy(i, slot).wait()

        @pl.when(i + (X_SLOTS - 1) < nu)
        def _():
            x_copy(i + (X_SLOTS - 1), (i + (X_SLOTS - 1)) % X_SLOTS).start()

        @pl.when(i >= Y_SLOTS)
        def _():
            y_copy(i - Y_SLOTS, yslot).wait()

        xin = xbuf.at[slot]
        for w in range(ROW_WORDS):
            lo, hi = _unpack_halves(_load_row_word(xin, w, tm))
            xlo[:, LANES * w:LANES * (w + 1)] = lo.astype(BF16)
            xhi[:, LANES * w:LANES * (w + 1)] = hi.astype(BF16)

        def up(wsc):
            return (jnp.dot(xlo[...], wsc[0:half, :], preferred_element_type=F32)
                    + jnp.dot(xhi[...], wsc[half:2 * half, :], preferred_element_type=F32))

        hid = (_silu(up(w1s)) * up(w3s)).astype(BF16)
        y = jnp.dot(hid, w2s[...], preferred_element_type=F32)
        _store_rows(ybuf.at[yslot], _pack_halves(y[:, 0:half], y[:, half:2 * half]))
        y_copy(i, yslot).start()

    def per_expert(e, carry):
        i0, ws = carry
        n = n_blocks_of(e)

        @pl.when(n > 0)
        def _():
            for cp in w_copies(e, ws):
                cp.wait()
            w1s[...] = w1f[ws].astype(BF16)
            w3s[...] = w3f[ws].astype(BF16)
            w2s[...] = w2f[ws].astype(BF16)
            e_next = next_nonempty(e + 1)

            @pl.when(e_next < ne)
            def _():
                for cp in w_copies(e_next, 1 - ws):
                    cp.start()

            def body(j, c):
                block(i0 + j)
                return c

            lax.fori_loop(0, n, body, 0)

        return i0 + n, jnp.where(n > 0, 1 - ws, ws)

    lax.fori_loop(0, ne, per_expert, (jnp.int32(0), jnp.int32(0)))

    for q in range(Y_SLOTS):
        @pl.when(nu > q)
        def _(q=q):
            y_copy(nu - 1 - q, (nu - 1 - q) % Y_SLOTS).wait()


def _experts(counts, xs, w1e, w3e, w2e):
    tm = EXPERT_ROWS
    ne, d, de = w1e.shape
    half = ROW_WORDS * LANES
    anyspec = pl.BlockSpec(memory_space=pl.ANY)
    blk_buf = lambda n: pltpu.VMEM((n, tm * ROW_WORDS, LANES), U32)
    return pl.pallas_call(
        functools.partial(_expert_body, tm=tm, ne=ne),
        grid_spec=pltpu.PrefetchScalarGridSpec(
            num_scalar_prefetch=1,
            grid=(1,),
            in_specs=[anyspec, anyspec, anyspec, anyspec],
            out_specs=anyspec,
            scratch_shapes=[pltpu.VMEM((2, d, de), F32), pltpu.VMEM((2, d, de), F32), pltpu.VMEM((2, de, d), F32),
                            pltpu.VMEM((d, de), BF16), pltpu.VMEM((d, de), BF16), pltpu.VMEM((de, d), BF16),
                            blk_buf(X_SLOTS), blk_buf(Y_SLOTS),
                            pltpu.VMEM((tm, half), BF16), pltpu.VMEM((tm, half), BF16),
                            pltpu.SemaphoreType.DMA((X_SLOTS,)), pltpu.SemaphoreType.DMA((Y_SLOTS,)),
                            pltpu.SemaphoreType.DMA((2,))]),
        out_shape=jax.ShapeDtypeStruct(xs.shape, U32),
        compiler_params=pltpu.CompilerParams(dimension_semantics=("arbitrary",), vmem_limit_bytes=VMEM_LIMIT),
        name="experts",
    )(counts, xs, w1e, w3e, w2e)


def _combine_body(z_ref, x2_ref, gtf_ref, wt_ref, gfin_ref, y_ref, wcol, *, td, topk):
    half = ROW_WORDS * LANES
    rc = COMBINE_ROWS
    wcol[...] = wt_ref[...].T
    per_row_gate = gtf_ref.shape[0] != 1

    def chunk(c, carry):
        r0 = pl.multiple_of(c * rc, rc)
        rows = pl.ds(r0, rc)
        w = wcol[rows, :]
        ws = [w[:, k:k + 1] for k in range(topk)]
        xs = {}
        sq = jnp.zeros((rc, 1), F32)
        for j in range(ROW_WORDS):
            acc_lo = jnp.zeros((rc, LANES), F32)
            acc_hi = jnp.zeros((rc, LANES), F32)
            for k in range(topk):
                p = z_ref[k, pl.ds(r0 * ROW_WORDS + j, rc, stride=ROW_WORDS), :]
                lo, hi = _unpack_halves(p)
                acc_lo = acc_lo + ws[k] * lo
                acc_hi = acc_hi + ws[k] * hi
            for base, acc in ((0, acc_lo), (half, acc_hi)):
                cols = slice(base + LANES * j, base + LANES * (j + 1))
                gate = gtf_ref[rows, cols] if per_row_gate else gtf_ref[:, cols]
                x = x2_ref[rows, cols] + gate * acc
                xs[cols.start] = x
                sq = sq + jnp.sum(x * x, axis=-1, keepdims=True)
        rs = lax.rsqrt(sq / (2 * half) + EPS)
        for start, x in xs.items():
            y_ref[rows, start:start + LANES] = x * rs * gfin_ref[:, start:start + LANES]
        return carry

    lax.fori_loop(0, td // rc, chunk, 0)


def _combine_body_into(z_ref, x2_ref, gtf_ref, wt_ref, gfin_ref, prev_ref, y_ref, wcol, *, td, topk):
    del prev_ref
    _combine_body(z_ref, x2_ref, gtf_ref, wt_ref, gfin_ref, y_ref, wcol, td=td, topk=topk)


def _combine(z, token0, wts, x2, mod, g_final, *, rows_per_mod, per_row_mod, b0=0, out_rows=None, into=None):
    t, d = x2.shape
    td = min(t, COMBINE_TILE)
    tile0 = token0 // td
    out_rows = t if out_rows is None else out_rows
    if per_row_mod:
        gtf_spec = pl.BlockSpec((td, d), lambda i: (i, 5))
        out0 = 0
    else:
        tiles_per_mod = rows_per_mod // td
        gtf_spec = pl.BlockSpec((None, None, 1, d), lambda i: (i // tiles_per_mod + b0, 5, 0, 0))
        out0 = b0 * tiles_per_mod
    in_specs = [pl.BlockSpec((TOP_K, td * ROW_WORDS, LANES), lambda i: (0, i + tile0, 0)),
                pl.BlockSpec((td, d), lambda i: (i, 0)),
                gtf_spec,
                pl.BlockSpec((SUBLANES, td), lambda i: (0, i + tile0)),
                pl.BlockSpec((1, d), lambda i: (0, 0))]
    args = [z, x2, mod, wts, g_final]
    body, aliases = _combine_body, {}
    if into is not None:
        in_specs.append(pl.BlockSpec(memory_space=pl.ANY))
        args.append(into)
        body, aliases = _combine_body_into, {len(args) - 1: 0}
    return pl.pallas_call(
        functools.partial(body, td=td, topk=TOP_K),
        grid=(t // td,),
        in_specs=in_specs,
        out_specs=pl.BlockSpec((td, d), lambda i: (i + out0, 0)),
        scratch_shapes=[pltpu.VMEM((td, SUBLANES), F32)],
        out_shape=jax.ShapeDtypeStruct((out_rows, d), F32),
        input_output_aliases=aliases,
        name="combine",
    )(*args)


def _retention_tables(length, nh, hd):
    c = math.gcd(length, RET_CHUNK)
    log_g = jnp.log(1.0 - 2.0 ** (-5.0 - jnp.arange(nh, dtype=F32)))
    idx = jnp.arange(c, dtype=F32)
    rel = idx[:, None] - idx[None, :]
    mask = jnp.where(rel >= 0, jnp.exp(log_g[:, None, None] * jnp.maximum(rel, 0.0)), 0.0)
    q_decay = jnp.exp(log_g[None, :] * (idx[:, None] + 1.0))
    k_decay = jnp.exp(log_g[None, :] * (c - 1.0 - idx[:, None]))
    chunk_decay = jnp.exp(log_g * c)
    qd = jnp.broadcast_to(q_decay.T[:, :, None], (nh, c, hd))
    kd = jnp.broadcast_to(k_decay.T[:, :, None], (nh, c, hd))
    cd = jnp.broadcast_to(chunk_decay[:, None, None], (nh, hd, hd))
    return mask, qd, kd, cd


def kernel(x_prompt, x_sample, c_prompt, c_sample, state_conv, state_ret, w_ada, b_ada, g_mix, g_ffn, w_in,
           conv_w, conv_b, conv_norm_g, conv_norm_b, ret_norm_g, ret_norm_b, w_out, w_router, router_bias,
           w1, w3, w2, ws1, ws3, ws2, g_final):
    depth = w_ada.shape[0]
    assert depth == 1, "single-layer trunk"
    bp, lp, d = x_prompt.shape
    bs, ls, _ = x_sample.shape
    assert ls == 1
    dc = conv_w.shape[2]
    dr = ret_norm_g.shape[1]
    nh = RET_HEADS
    hd = dr // nh
    assert hd == LANES and lp % 256 == 0 and bs % TOKEN_TILE == 0 and d // 2 == ROW_WORDS * LANES
    ne = w_router.shape[2]
    row = lambda a: a.reshape(1, -1)

    mod = _ada(jnp.concatenate([c_prompt, c_sample], axis=0), w_ada[0], row(b_ada[0]))
    mod_p = mod[:bp].reshape(bp, 6, 1, d)
    mod_s = mod[bp:]

    half = hd // 2
    inv = ROPE_BASE ** (-jnp.arange(half, dtype=F32) / half)
    inv2 = jnp.concatenate([inv, inv]).reshape(1, hd)
    cos_p, sin_p = _rope_tables(inv2, lp, 0)
    cos_s, sin_s = _rope_tables(inv2, SUBLANES, PAST_LEN)

    w_in_b = w_in[0].astype(BF16)
    wo_b = w_out[0].astype(BF16)
    wr_t = w_router[0].T
    wrh = wr_t.astype(BF16)
    wrl = (wr_t - wrh.astype(F32)).astype(BF16)
    ws1_b, ws3_b, ws2_b = ws1[0].astype(BF16), ws3[0].astype(BF16), ws2[0].astype(BF16)
    dims = dict(dc=dc, dr=dr, hd=hd)

    tables = _retention_tables(lp, nh, hd)
    log_g = jnp.log(1.0 - 2.0 ** (-5.0 - jnp.arange(nh, dtype=F32)))
    gam = jnp.broadcast_to(jnp.exp(log_g)[:, None, None], (nh, SUBLANES, hd))
    norm_rows = (row(conv_b[0]), row(conv_norm_g[0]), row(conv_norm_b[0]), row(ret_norm_g[0]), row(ret_norm_b[0]))
    post_w = (row(g_ffn[0]), wo_b, wrh, wrl, ws1_b, ws3_b, ws2_b)
    bias_col = router_bias[0].reshape(ne, 1)
    hw = d // 2
    tm = EXPERT_ROWS

    def pre_prompt(b0, nb):
        glu, q, k, v, sg = _proj(x_prompt, mod_p, row(g_mix[0]), w_in_b, cos_p, sin_p,
                                 per_row_mod=False, b0=b0, nb=nb, **dims)
        cat, ret = _mix(glu, q, k, v, sg, conv_w[0], *norm_rows, tables, nb=nb, length=lp, **dims)
        x2, hp, lg = _post(cat, x_prompt, mod_p, *post_w, per_row_mod=False, b0=b0, nb=nb)
        return glu, ret, x2, hp, lg

    def pre_sample():
        xs3 = x_sample.reshape(1, bs, d)
        glu, q, k, v, sg = _proj(xs3, mod_s, row(g_mix[0]), w_in_b, cos_s, sin_s, per_row_mod=True, **dims)
        cat, ret, new_conv = _mix1(glu, q, k, v, sg, state_conv[0], state_ret[0], conv_w[0], *norm_rows, gam,
                                   **dims)
        x2, hp, lg = _post(cat, xs3, mod_s, *post_w, per_row_mod=True)
        return new_conv, ret, x2, hp, lg

    def moe(hps, lgs):
        lg = lgs[0] if len(lgs) == 1 else jnp.concatenate(lgs, axis=1)
        tokens = lg.shape[1]
        eidx, wts, rank, cnt = _route(lg, bias_col)
        counts = cnt[:, 0].astype(I32)
        n_rows = -(-(tokens * TOP_K + ne * (tm - 1)) // tm) * tm
        dest3 = _dest_rows(counts, eidx, rank).reshape(tokens // TOKEN_TILE, SUBLANES, TOKEN_TILE)
        xs = _dispatch([h.reshape(-1, hw) for h in hps], dest3, n_rows)
        ys = _experts(counts, xs.reshape(n_rows * ROW_WORDS, LANES), w1[0], w3[0], w2[0])
        z = _undispatch(ys.reshape(n_rows, hw), dest3, tokens).reshape(TOP_K, tokens * ROW_WORDS, LANES)
        return z, wts

    nb0 = bp // 2
    nb1 = bp - nb0
    glu_0, ret_0, x2_0, hp_0, lg_0 = pre_prompt(0, nb0)
    z_0, wts_0 = moe([hp_0], [lg_0])
    glu_1, ret_1, x2_1, hp_1, lg_1 = pre_prompt(nb0, nb1)
    new_conv_s, ret_s, x2_s, hp_s, lg_s = pre_sample()
    z_1, wts_1 = moe([hp_1, hp_s], [lg_1, lg_s])
    y_p = _combine(z_0, 0, wts_0, x2_0, mod_p, row(g_final), rows_per_mod=lp, per_row_mod=False,
                   out_rows=bp * lp)
    y_p = _combine(z_1, 0, wts_1, x2_1, mod_p, row(g_final), rows_per_mod=lp, per_row_mod=False,
                   b0=nb0, out_rows=bp * lp, into=y_p)
    y_s = _combine(z_1, nb1 * lp, wts_1, x2_s, mod_s, row(g_final), rows_per_mod=bs, per_row_mod=True)
    ret_p = jnp.concatenate([ret_0, ret_1], axis=0)

    tail = lambda g, n: g.reshape(n, lp, dc)[:, lp - CONV_BUF:, :]
    new_conv_p = jnp.concatenate([tail(glu_0, nb0), tail(glu_1, nb1)], axis=0)
    return (y_p.reshape(bp, lp, d), y_s.reshape(bs, ls, d), new_conv_p[None], ret_p[None],
            new_conv_s[None], ret_s[None])
```

```python
import functools
import math

import jax
import jax.numpy as jnp
from jax import lax
from jax.experimental import pallas as pl
from jax.experimental.pallas import tpu as pltpu
from jax.experimental.pallas import tpu_sc as plsc

F32 = jnp.float32
BF16 = jnp.bfloat16
U32 = jnp.uint32
I32 = jnp.int32

EPS = 1e-6
PAST_LEN = 16384
RET_HEADS = 4
RET_CHUNK = 128
CONV_WIDTH = 31
CONV_BUF = CONV_WIDTH - 1
ROPE_BASE = 10000.0
N_EXPERTS = 64
TOP_K = 6
N_GROUPS = 8
TOPK_GROUPS = 4
ROUTED_SCALE = 2.5

LANES = 128
SUBLANES = 8
CONV_PAD = 32
EXPERT_ROWS = 256
ROUTE_TILE = 640
TOKEN_TILE = 128
COMBINE_TILE = 512
VMEM_LIMIT = 56 * 1024 * 1024
SC_CORES = 2
SC_SUBCORES = 16
SC_WORKERS = SC_CORES * SC_SUBCORES
SCATTER_ROWS = 64
GATHER_ROWS = 32
ROW_WORDS = 4
X_SLOTS = 4
Y_SLOTS = 3

HI_MASK = 0xFFFF0000


def _sigmoid(x):
    return jax.nn.sigmoid(x)


def _silu(x):
    return x * jax.nn.sigmoid(x)


def _pack_halves(lo, hi):
    lo_u = lax.bitcast_convert_type(lo.astype(BF16).astype(F32), U32) >> 16
    hi_u = lax.bitcast_convert_type(hi.astype(BF16).astype(F32), U32) & jnp.uint32(HI_MASK)
    return hi_u | lo_u


def _unpack_halves(p):
    lo = lax.bitcast_convert_type(p << 16, F32)
    hi = lax.bitcast_convert_type(p & jnp.uint32(HI_MASK), F32)
    return lo, hi


def _store_rows(ref, x):
    rows = x.shape[0]
    for j in range(ROW_WORDS):
        ref[pl.ds(j, rows, stride=ROW_WORDS), :] = x[:, LANES * j:LANES * (j + 1)]


def _load_row_word(ref, j, rows):
    return ref[pl.ds(j, rows, stride=ROW_WORDS), :]


def _ada_body(c_ref, w_ref, b_ref, o_ref):
    s = _silu(c_ref[...]).astype(BF16)
    o_ref[...] = jnp.dot(s, w_ref[...].astype(BF16), preferred_element_type=F32) + b_ref[...]


def _ada(c_all, w_ada, b_ada):
    rows, d = c_all.shape
    n = w_ada.shape[1]
    tn = 2048
    return pl.pallas_call(
        _ada_body,
        grid=(n // tn,),
        in_specs=[
            pl.BlockSpec((rows, d), lambda j: (0, 0)),
            pl.BlockSpec((d, tn), lambda j: (0, j)),
            pl.BlockSpec((1, tn), lambda j: (0, j)),
        ],
        out_specs=pl.BlockSpec((rows, tn), lambda j: (0, j)),
        out_shape=jax.ShapeDtypeStruct((rows, n), F32),
        compiler_params=pltpu.CompilerParams(vmem_limit_bytes=VMEM_LIMIT),
        name="ada",
    )(c_all, w_ada, b_ada)


def _rope_body(inv_ref, cos_ref, sin_ref, *, pos0, tl, half):
    row = lax.broadcasted_iota(I32, (tl, LANES), 0) + pl.program_id(0) * tl
    ang = (row.astype(F32) + pos0) * inv_ref[...]
    lane = lax.broadcasted_iota(I32, (tl, LANES), 1)
    s = jnp.sin(ang)
    cos_ref[...] = jnp.cos(ang)
    sin_ref[...] = jnp.where(lane < half, -s, s)


def _rope_tables(inv2, rows, pos0):
    tl = min(rows, 256)
    return pl.pallas_call(
        functools.partial(_rope_body, pos0=float(pos0), tl=tl, half=LANES // 2),
        grid=(rows // tl,),
        in_specs=[pl.BlockSpec((1, LANES), lambda i: (0, 0))],
        out_specs=[pl.BlockSpec((tl, LANES), lambda i: (i, 0))] * 2,
        out_shape=[jax.ShapeDtypeStruct((rows, LANES), F32)] * 2,
        name="rope",
    )(inv2)


def _modulated_rmsnorm(x, g, sc, sh):
    ms = jnp.mean(x * x, axis=-1, keepdims=True)
    h = x * lax.rsqrt(ms + EPS) * g
    return h * (1.0 + sc) + sh


def _proj_body(x_ref, sh_ref, sc_ref, g_ref, w_ref, cos_ref, sin_ref,
               glu_ref, q_ref, k_ref, v_ref, sg_ref, *, dc, dr, hd, rope_rows):
    hb = _modulated_rmsnorm(x_ref[...], g_ref[...], sc_ref[...], sh_ref[...]).astype(BF16)

    def proj(lo, n):
        return jnp.dot(hb, w_ref[:, lo:lo + n], preferred_element_type=F32)

    glu_ref[...] = proj(0, dc) * _sigmoid(proj(dc, dc))
    cos = cos_ref[...] if rope_rows else cos_ref[0:1, :]
    sin = sin_ref[...] if rope_rows else sin_ref[0:1, :]
    for ref, lo, scale in ((q_ref, 2 * dc, hd ** -0.5), (k_ref, 2 * dc + dr, None)):
        t = proj(lo, dr)
        for hh in range(dr // hd):
            th = t[:, hh * hd:(hh + 1) * hd]
            r = th * cos + pltpu.roll(th, hd // 2, 1) * sin
            if scale is not None:
                r = r * scale
            ref[:, hh * hd:(hh + 1) * hd] = r.astype(BF16)
    v_ref[...] = proj(2 * dc + 2 * dr, dr).astype(BF16)
    sg_ref[...] = _silu(proj(2 * dc + 3 * dr, dr)).astype(BF16)


def _proj(x, mod, g_mix, w_in_b, cos2, sin2, *, dc, dr, hd, per_row_mod, b0=0, nb=None):
    nb_all, length, d = x.shape
    nb = nb_all if nb is None else nb
    tl = min(length, 256)
    x2 = x.reshape(nb_all * length, d)
    nl = length // tl
    if per_row_mod:
        mod_spec = lambda j: pl.BlockSpec((tl, d), lambda b, l: (l, j))
        rope_spec = pl.BlockSpec((SUBLANES, LANES), lambda b, l: (0, 0))
    else:
        mod_spec = lambda j: pl.BlockSpec((None, None, 1, d), lambda b, l: (b + b0, j, 0, 0))
        rope_spec = pl.BlockSpec((tl, LANES), lambda b, l: (l, 0))
    row_spec = lambda w: pl.BlockSpec((tl, w), lambda b, l: (b * nl + l, 0))
    t = nb * length
    outs = pl.pallas_call(
        functools.partial(_proj_body, dc=dc, dr=dr, hd=hd, rope_rows=not per_row_mod),
        grid=(nb, nl),
        in_specs=[
            pl.BlockSpec((tl, d), lambda b, l: ((b + b0) * nl + l, 0)), mod_spec(0), mod_spec(1),
            pl.BlockSpec((1, d), lambda b, l: (0, 0)),
            pl.BlockSpec(w_in_b.shape, lambda b, l: (0, 0)),
            rope_spec, rope_spec,
        ],
        out_specs=[row_spec(dc), row_spec(dr), row_spec(dr), row_spec(dr), row_spec(dr)],
        out_shape=[
            jax.ShapeDtypeStruct((t, dc), F32),
            jax.ShapeDtypeStruct((t, dr), BF16),
            jax.ShapeDtypeStruct((t, dr), BF16),
            jax.ShapeDtypeStruct((t, dr), BF16),
            jax.ShapeDtypeStruct((t, dr), BF16),
        ],
        compiler_params=pltpu.CompilerParams(vmem_limit_bytes=VMEM_LIMIT),
        name="proj",
    )(x2, mod, mod, g_mix, w_in_b, cos2, sin2)
    return outs


def _layernorm_silu(c, g, b):
    mu = jnp.mean(c, axis=-1, keepdims=True)
    d = c - mu
    var = jnp.mean(d * d, axis=-1, keepdims=True)
    return _silu(d * lax.rsqrt(var + EPS) * g + b)


def _groupnorm(o, g, b):
    mu = jnp.mean(o, axis=-1, keepdims=True)
    d = o - mu
    var = jnp.mean(d * d, axis=-1, keepdims=True)
    return d * lax.rsqrt(var + EPS) * g + b


def _mix_body(glu_ref, q_ref, k_ref, v_ref, sg_ref, cw_ref, cb_ref, lng_ref, lnb_ref, rg_ref, rb_ref,
              mask_ref, qd_ref, kd_ref, cd_ref, cat_ref, st_ref, buf, cscr, *, tl, dc, hd, nh, chunk):
    nslab = dc // LANES

    @pl.when(pl.program_id(1) == 0)
    def _():
        buf[:, 0:CONV_PAD, :] = jnp.zeros((nslab, CONV_PAD, LANES), F32)
        st_ref[...] = jnp.zeros(st_ref.shape, F32)

    for j in range(nslab):
        buf[j, CONV_PAD:CONV_PAD + tl, :] = glu_ref[:, LANES * j:LANES * (j + 1)]
    first = CONV_PAD - CONV_BUF
    rows_per_iter = 8 * SUBLANES
    for j in range(nslab):
        cols = slice(LANES * j, LANES * (j + 1))
        wv = [jnp.broadcast_to(cw_ref[t:t + 1, cols], (SUBLANES, LANES)) for t in range(CONV_WIDTH)]
        bias = jnp.broadcast_to(cb_ref[0:1, cols], (SUBLANES, LANES))

        def body(r, carry, j=j, cols=cols, wv=wv, bias=bias):
            base = pl.multiple_of(r * rows_per_iter, rows_per_iter)
            for u in range(rows_per_iter // SUBLANES):
                acc = bias
                for t in range(CONV_WIDTH):
                    acc = acc + wv[t] * buf[j, pl.ds(base + (u * SUBLANES + first + t), SUBLANES), :]
                cscr[pl.ds(base + u * SUBLANES, SUBLANES), cols] = acc
            return carry

        lax.fori_loop(0, tl // rows_per_iter, body, 0)
    for j in range(nslab):
        buf[j, 0:CONV_PAD, :] = buf[j, tl:tl + CONV_PAD, :]
    cat_ref[:, 0:dc] = _layernorm_silu(cscr[...], lng_ref[...], lnb_ref[...]).astype(BF16)

    nt = (((1,), (1,)), ((), ()))
    tn = (((0,), (0,)), ((), ()))
    for c in range(tl // chunk):
        rows = slice(c * chunk, (c + 1) * chunk)
        for hh in range(nh):
            cols = slice(hh * hd, (hh + 1) * hd)
            qh = q_ref[rows, cols]
            kh = k_ref[rows, cols]
            vh = v_ref[rows, cols]
            s = st_ref[0, hh]
            scores = lax.dot_general(qh, kh, nt, preferred_element_type=F32) * mask_ref[hh]
            inner = jnp.dot(scores.astype(BF16), vh, preferred_element_type=F32)
            qd = (qh.astype(F32) * qd_ref[hh]).astype(BF16)
            cross = jnp.dot(qd, s.astype(BF16), preferred_element_type=F32)
            kd = (kh.astype(F32) * kd_ref[hh]).astype(BF16)
            st_ref[0, hh] = cd_ref[hh] * s + lax.dot_general(kd, vh, tn, preferred_element_type=F32)
            o = _groupnorm(inner + cross, rg_ref[0:1, cols], rb_ref[0:1, cols])
            cat_ref[rows, dc + hh * hd:dc + (hh + 1) * hd] = (o * sg_ref[rows, cols].astype(F32)).astype(BF16)


def _mix(glu, q, k, v, sg, conv_w, conv_b, ln_g, ln_b, rg, rb, tables, *, nb, length, dc, dr, hd):
    nh = dr // hd
    chunk = math.gcd(length, RET_CHUNK)
    tl = min(length, 256)
    nl = length // tl
    mask, qd, kd, cd = tables
    row_spec = lambda w: pl.BlockSpec((tl, w), lambda b, l: (b * nl + l, 0))
    full = lambda a: pl.BlockSpec(a.shape, lambda b, l: (0,) * a.ndim)
    cat, st = pl.pallas_call(
        functools.partial(_mix_body, tl=tl, dc=dc, hd=hd, nh=nh, chunk=chunk),
        grid=(nb, nl),
        in_specs=[row_spec(dc), row_spec(dr), row_spec(dr), row_spec(dr), row_spec(dr),
                  full(conv_w), full(conv_b), full(ln_g), full(ln_b), full(rg), full(rb),
                  full(mask), full(qd), full(kd), full(cd)],
        out_specs=[row_spec(dc + dr), pl.BlockSpec((1, nh, hd, hd), lambda b, l: (b, 0, 0, 0))],
        out_shape=[jax.ShapeDtypeStruct((nb * length, dc + dr), BF16),
                   jax.ShapeDtypeStruct((nb, nh, hd, hd), F32)],
        scratch_shapes=[pltpu.VMEM((dc // LANES, tl + CONV_PAD, LANES), F32),
                        pltpu.VMEM((tl, dc), F32)],
        compiler_params=pltpu.CompilerParams(dimension_semantics=("arbitrary", "arbitrary")),
        name="mix",
    )(glu, q, k, v, sg, conv_w, conv_b, ln_g, ln_b, rg, rb, mask, qd, kd, cd)
    return cat, st


def _mix1_body(glu_ref, q_ref, k_ref, v_ref, sg_ref, sc_ref, s0_ref, cw_ref, cb_ref, lng_ref, lnb_ref,
               rg_ref, rb_ref, gam_ref, cat_ref, st_ref, nc_ref, cscr, oscr, qf, kf, vf, *, tb, dc, hd, nh):
    w_hist = cw_ref[0:CONV_BUF, :]
    w_last = cw_ref[CONV_BUF:CONV_WIDTH, :]
    for bb in range(tb):
        hist = jnp.sum(sc_ref[bb] * w_hist, axis=0, keepdims=True)
        cscr[bb:bb + 1, :] = hist + glu_ref[bb:bb + 1, :] * w_last + cb_ref[...]
        nc_ref[bb, 0:CONV_BUF - 1, :] = sc_ref[bb, 1:CONV_BUF, :]
        nc_ref[bb, CONV_BUF - 1:CONV_BUF, :] = glu_ref[bb:bb + 1, :]
    cat_ref[:, 0:dc] = _layernorm_silu(cscr[...], lng_ref[...], lnb_ref[...]).astype(BF16)

    tn = (((0,), (0,)), ((), ()))
    rowid = lax.broadcasted_iota(I32, (tb, hd), 0)
    qf[...] = q_ref[...].astype(F32)
    kf[...] = k_ref[...].astype(F32)
    vf[...] = v_ref[...].astype(F32)
    for hh in range(nh):
        cols = slice(hh * hd, (hh + 1) * hd)
        qa = q_ref[:, cols]
        ka = k_ref[:, cols]
        gam = gam_ref[hh, 0:1, :]
        for bb in range(tb):
            onehot = (rowid == bb).astype(BF16)
            qcol = lax.dot_general(qa, onehot, tn, preferred_element_type=F32)
            kcol = lax.dot_general(ka, onehot, tn, preferred_element_type=F32)
            s0 = s0_ref[bb, hh]
            qrow = qf[bb:bb + 1, cols]
            krow = kf[bb:bb + 1, cols]
            vrow = vf[bb:bb + 1, cols]
            qk = jnp.sum(qrow * krow, axis=-1, keepdims=True)
            cross = gam * jnp.sum(qcol * s0, axis=0, keepdims=True)
            st_ref[bb, hh] = gam * s0 + kcol * vrow
            oscr[bb:bb + 1, cols] = qk * vrow + cross
    for hh in range(nh):
        cols = slice(hh * hd, (hh + 1) * hd)
        o = _groupnorm(oscr[:, cols], rg_ref[0:1, cols], rb_ref[0:1, cols])
        cat_ref[:, dc + hh * hd:dc + (hh + 1) * hd] = (o * sg_ref[:, cols].astype(F32)).astype(BF16)


def _mix1(glu, q, k, v, sg, state_conv, state_ret, conv_w, conv_b, ln_g, ln_b, rg, rb, gam, *, dc, dr, hd):
    nb = glu.shape[0]
    nh = dr // hd
    tb = 16
    row_spec = lambda w: pl.BlockSpec((tb, w), lambda i: (i, 0))
    full = lambda a: pl.BlockSpec(a.shape, lambda i: (0,) * a.ndim)
    st_spec = pl.BlockSpec((None, tb, nh, hd, hd), lambda i: (0, i, 0, 0, 0))
    conv_spec = pl.BlockSpec((None, tb, CONV_BUF, dc), lambda i: (0, i, 0, 0))
    cat, st, new_conv = pl.pallas_call(
        functools.partial(_mix1_body, tb=tb, dc=dc, hd=hd, nh=nh),
        grid=(nb // tb,),
        in_specs=[row_spec(dc), row_spec(dr), row_spec(dr), row_spec(dr), row_spec(dr), conv_spec, st_spec,
                  full(conv_w), full(conv_b), full(ln_g), full(ln_b), full(rg), full(rb), full(gam)],
        out_specs=[row_spec(dc + dr), st_spec, conv_spec],
        out_shape=[jax.ShapeDtypeStruct((nb, dc + dr), BF16),
                   jax.ShapeDtypeStruct((1, nb, nh, hd, hd), F32),
                   jax.ShapeDtypeStruct((1, nb, CONV_BUF, dc), F32)],
        scratch_shapes=[pltpu.VMEM((tb, dc), F32)] + [pltpu.VMEM((tb, dr), F32)] * 4,
        compiler_params=pltpu.CompilerParams(vmem_limit_bytes=VMEM_LIMIT),
        name="mix1",
    )(glu, q, k, v, sg, state_conv, state_ret, conv_w, conv_b, ln_g, ln_b, rg, rb, gam)
    return cat, st, new_conv


def _post_body(cat_ref, x_ref, gtm_ref, scf_ref, shf_ref, gtf_ref, g_ref, wo_ref, wrh_ref, wrl_ref,
               ws1_ref, ws3_ref, ws2_ref, x2_ref, hp_ref, lg_ref):
    d = x_ref.shape[1]
    y = jnp.dot(cat_ref[...], wo_ref[...], preferred_element_type=F32)
    x1 = x_ref[...] + gtm_ref[...] * y
    h = _modulated_rmsnorm(x1, g_ref[...], scf_ref[...], shf_ref[...])
    hb = h.astype(BF16)
    _store_rows(hp_ref, _pack_halves(h[:, 0:d // 2], h[:, d // 2:d]))
    hl = (h - hb.astype(F32)).astype(BF16)
    nt = (((1,), (1,)), ((), ()))
    lg_ref[...] = (lax.dot_general(wrh_ref[...], hb, nt, preferred_element_type=F32)
                   + lax.dot_general(wrh_ref[...], hl, nt, preferred_element_type=F32)
                   + lax.dot_general(wrl_ref[...], hb, nt, preferred_element_type=F32))
    s1 = jnp.dot(hb, ws1_ref[...], preferred_element_type=F32)
    s3 = jnp.dot(hb, ws3_ref[...], preferred_element_type=F32)
    shared = jnp.dot((_silu(s1) * s3).astype(BF16), ws2_ref[...], preferred_element_type=F32)
    x2_ref[...] = x1 + gtf_ref[...] * shared


def _post(cat, x, mod, g_ffn, wo_b, wrh, wrl, ws1_b, ws3_b, ws2_b, *, per_row_mod, b0=0, nb=None):
    nb_all, length, d = x.shape
    nb = nb_all if nb is None else nb
    tl = min(length, 512)
    nl = length // tl
    t = nb * length
    ne = wrh.shape[0]
    x2d = x.reshape(nb_all * length, d)
    if per_row_mod:
        mod_spec = lambda j: pl.BlockSpec((tl, d), lambda b, l: (l, j))
    else:
        mod_spec = lambda j: pl.BlockSpec((None, None, 1, d), lambda b, l: (b + b0, j, 0, 0))
    row_spec = lambda w: pl.BlockSpec((tl, w), lambda b, l: (b * nl + l, 0))
    full = lambda a: pl.BlockSpec(a.shape, lambda b, l: (0,) * a.ndim)
    return pl.pallas_call(
        _post_body,
        grid=(nb, nl),
        in_specs=[row_spec(d), pl.BlockSpec((tl, d), lambda b, l: ((b + b0) * nl + l, 0)),
                  mod_spec(2), mod_spec(4), mod_spec(3), mod_spec(5),
                  full(g_ffn), full(wo_b), full(wrh), full(wrl), full(ws1_b), full(ws3_b), full(ws2_b)],
        out_specs=[row_spec(d),
                   pl.BlockSpec((tl * ROW_WORDS, LANES), lambda b, l: (b * nl + l, 0)),
                   pl.BlockSpec((ne, tl), lambda b, l: (0, b * nl + l))],
        out_shape=[jax.ShapeDtypeStruct((t, d), F32),
                   jax.ShapeDtypeStruct((t * ROW_WORDS, LANES), U32),
                   jax.ShapeDtypeStruct((ne, t), F32)],
        compiler_params=pltpu.CompilerParams(vmem_limit_bytes=VMEM_LIMIT),
        name="post",
    )(cat, x2d, mod, mod, mod, mod, g_ffn, wo_b, wrh, wrl, ws1_b, ws3_b, ws2_b)


def _first_max(x, idx, sentinel):
    m = jnp.max(x, axis=0, keepdims=True)
    f = jnp.min(jnp.where(x == m, idx, sentinel), axis=0, keepdims=True)
    return m, f


def _route_body(lg_ref, bias_ref, e_ref, w_ref, r_ref, cnt_ref, cnt_scr, *, tr, ne, ng, topk, topg):
    @pl.when(pl.program_id(0) == 0)
    def _():
        cnt_scr[...] = jnp.zeros(cnt_scr.shape, F32)

    per = ne // ng
    neg = -jnp.inf
    scores = _sigmoid(lg_ref[...])
    sel = scores + bias_ref[...]
    sub = lax.broadcasted_iota(I32, (per, tr), 0)
    gs = []
    for g in range(ng):
        s_g = sel[g * per:(g + 1) * per, :]
        m1, f1 = _first_max(s_g, sub, per)
        m2 = jnp.max(jnp.where(sub == f1, neg, s_g), axis=0, keepdims=True)
        gs.append(m1 + m2)
    gsc = jnp.concatenate(gs, axis=0)
    gi = lax.broadcasted_iota(I32, (ng, tr), 0)
    keep = jnp.zeros((ng, tr), F32)
    for _ in range(topg):
        _, f = _first_max(gsc, gi, ng)
        pick = gi == f
        keep = jnp.where(pick, 1.0, keep)
        gsc = jnp.where(pick, neg, gsc)
    work = jnp.concatenate(
        [jnp.where(keep[g:g + 1, :] > 0.5, sel[g * per:(g + 1) * per, :], neg) for g in range(ng)], axis=0)
    ei = lax.broadcasted_iota(I32, (ne, tr), 0)
    picks, es, ws = [], [], []
    for _ in range(topk):
        _, f = _first_max(work, ei, ne)
        pick = ei == f
        picks.append(pick)
        es.append(f)
        ws.append(jnp.sum(jnp.where(pick, scores, 0.0), axis=0, keepdims=True))
        work = jnp.where(pick, neg, work)
    wsum = ws[0]
    for w in ws[1:]:
        wsum = wsum + w
    scale = ROUTED_SCALE / wsum
    chosen = picks[0]
    for p in picks[1:]:
        chosen = jnp.logical_or(chosen, p)
    chosen_f = chosen.astype(F32)
    t_row = lax.broadcasted_iota(I32, (tr, tr), 0)
    t_col = lax.broadcasted_iota(I32, (tr, tr), 1)
    before = (t_row < t_col).astype(BF16)
    prior = cnt_scr[:, 0:1] + jnp.dot(chosen_f.astype(BF16), before, preferred_element_type=F32)
    rs = [jnp.sum(jnp.where(p, prior, 0.0), axis=0, keepdims=True).astype(I32) for p in picks]
    pad_i = jnp.zeros((SUBLANES - topk, tr), I32)
    pad_f = jnp.zeros((SUBLANES - topk, tr), F32)
    e_ref[...] = jnp.concatenate(es + [pad_i], axis=0)
    w_ref[...] = jnp.concatenate([w * scale for w in ws] + [pad_f], axis=0)
    r_ref[...] = jnp.concatenate(rs + [pad_i], axis=0)
    total = cnt_scr[:, 0:1] + jnp.sum(chosen_f, axis=1, keepdims=True)
    cnt_scr[...] = jnp.broadcast_to(total, cnt_scr.shape)
    cnt_ref[...] = jnp.broadcast_to(total, cnt_ref.shape)


def _route_tile(t):
    return max(m for m in range(LANES, ROUTE_TILE + 1, LANES) if t % m == 0)


def _route(logits_t, bias_col):
    ne, t = logits_t.shape
    tr = _route_tile(t)
    tok = lambda dt: jax.ShapeDtypeStruct((SUBLANES, t), dt)
    tok_spec = pl.BlockSpec((SUBLANES, tr), lambda i: (0, i))
    return pl.pallas_call(
        functools.partial(_route_body, tr=tr, ne=ne, ng=N_GROUPS, topk=TOP_K, topg=TOPK_GROUPS),
        grid=(t // tr,),
        in_specs=[pl.BlockSpec((ne, tr), lambda i: (0, i)), pl.BlockSpec((ne, 1), lambda i: (0, 0))],
        out_specs=[tok_spec, tok_spec, tok_spec, pl.BlockSpec((ne, LANES), lambda i: (0, 0))],
        out_shape=[tok(I32), tok(F32), tok(I32), jax.ShapeDtypeStruct((ne, LANES), F32)],
        scratch_shapes=[pltpu.VMEM((ne, LANES), F32)],
        compiler_params=pltpu.CompilerParams(dimension_semantics=("arbitrary",)),
        name="route",
    )(logits_t, bias_col)


def _dest_body(cnt_ref, e_ref, r_ref, d_ref, ps, *, ne, tr):
    shift = EXPERT_ROWS.bit_length() - 1

    @pl.when(pl.program_id(0) == 0)
    def _():
        def step(j, start):
            ps[j] = start
            return start + lax.shift_left(lax.shift_right_logical(cnt_ref[j] + (EXPERT_ROWS - 1), shift), shift)

        lax.fori_loop(0, ne, step, jnp.int32(0))

    e = e_ref[...]
    base = jnp.zeros(e.shape, I32)
    for j in range(ne):
        base = jnp.where(e == j, ps[j], base)
    dest = base + r_ref[...]
    for m in range(tr // TOKEN_TILE):
        d_ref[SUBLANES * m:SUBLANES * (m + 1), :] = dest[:, TOKEN_TILE * m:TOKEN_TILE * (m + 1)]


def _dest_rows(counts, eidx, rank):
    rows, t = eidx.shape
    ne = counts.shape[0]
    tr = _route_tile(t)
    spec = pl.BlockSpec((rows, tr), lambda i, cnt: (0, i))
    return pl.pallas_call(
        functools.partial(_dest_body, ne=ne, tr=tr),
        grid_spec=pltpu.PrefetchScalarGridSpec(
            num_scalar_prefetch=1, grid=(t // tr,), in_specs=[spec, spec],
            out_specs=pl.BlockSpec((tr // TOKEN_TILE * rows, TOKEN_TILE), lambda i, cnt: (i, 0)),
            scratch_shapes=[pltpu.SMEM((ne,), I32)]),
        out_shape=jax.ShapeDtypeStruct((t // TOKEN_TILE * rows, TOKEN_TILE), I32),
        compiler_params=pltpu.CompilerParams(dimension_semantics=("arbitrary",)),
        name="dest",
    )(counts, eidx, rank)


def _sc_mesh():
    return plsc.VectorSubcoreMesh(core_axis_name="c", subcore_axis_name="s")


def _sc_worker_id():
    return lax.axis_index("s") * SC_CORES + lax.axis_index("c")


def _index_block(dest_ref, chunk, width):
    per_tile = TOKEN_TILE // width
    return dest_ref.at[chunk // per_tile, :, pl.ds((chunk % per_tile) * width, width)]


def _dispatch(hps, dest3, n_rows):
    w = SCATTER_ROWS
    width = hps[0].shape[1]
    bounds = [0]
    for h in hps:
        bounds.append(bounds[-1] + h.shape[0] // w)
    nch = bounds[-1]
    nsrc = len(hps)

    @functools.partial(
        pl.kernel, mesh=_sc_mesh(),
        out_type=jax.ShapeDtypeStruct((n_rows, width), U32),
        scratch_types=[pltpu.VMEM((SUBLANES, w), I32), pltpu.VMEM((w, width), U32), pltpu.SemaphoreType.DMA],
        compiler_params=pltpu.CompilerParams(use_tc_tiling_on_sc=False),
        name="dispatch",
    )
    def run(*refs):
        src_refs, dest_ref, xs_ref = refs[:nsrc], refs[nsrc], refs[nsrc + 1]
        idx_v, rows_v, sem = refs[nsrc + 2:]
        wid = _sc_worker_id()

        @pl.loop(0, pl.cdiv(nch, SC_WORKERS))
        def _(r):
            c = r * SC_WORKERS + wid

            @pl.when(c < nch)
            def _():
                pltpu.sync_copy(_index_block(dest_ref, c, w), idx_v)
                for i, src in enumerate(src_refs):
                    @pl.when(jnp.logical_and(c >= bounds[i], c < bounds[i + 1]))
                    def _(src=src, lo=bounds[i]):
                        pltpu.sync_copy(src.at[pl.ds((c - lo) * w, w)], rows_v)

                copies = [pltpu.async_copy(rows_v, xs_ref.at[idx_v.at[k]], sem) for k in range(TOP_K)]
                for cp in copies:
                    cp.wait()

    return run(*hps, dest3)


def _undispatch(ys, dest3, n_tokens):
    w = GATHER_ROWS
    width = ys.shape[1]
    nch = n_tokens // w

    @functools.partial(
        pl.kernel, mesh=_sc_mesh(),
        out_type=jax.ShapeDtypeStruct((TOP_K, n_tokens, width), U32),
        scratch_types=[pltpu.VMEM((SUBLANES, w), I32), pltpu.VMEM((TOP_K, w, width), U32),
                       pltpu.SemaphoreType.DMA],
        compiler_params=pltpu.CompilerParams(use_tc_tiling_on_sc=False),
        name="undispatch",
    )
    def run(ys_ref, dest_ref, z_ref, idx_v, bufs, sem):
        wid = _sc_worker_id()

        @pl.loop(0, pl.cdiv(nch, SC_WORKERS))
        def _(r):
            c = r * SC_WORKERS + wid

            @pl.when(c < nch)
            def _():
                pltpu.sync_copy(_index_block(dest_ref, c, w), idx_v)
                gathers = [pltpu.async_copy(ys_ref.at[idx_v.at[k]], bufs.at[k], sem) for k in range(TOP_K)]
                for cp in gathers:
                    cp.wait()
                stores = [pltpu.async_copy(bufs.at[k], z_ref.at[k, pl.ds(c * w, w)], sem) for k in range(TOP_K)]
                for cp in stores:
                    cp.wait()

    return run(ys, dest3)


PART_SHIFT = 24


def _expert_body(*refs, tm, ne, nparts):
    cnt_refs = refs[:nparts]
    xs_refs = refs[nparts:2 * nparts]
    w1_ref, w3_ref, w2_ref = refs[2 * nparts:2 * nparts + 3]
    ys_refs = refs[2 * nparts + 3:3 * nparts + 3]
    (w1f, w3f, w2f, w1s, w3s, w2s, xbuf, ybuf, xlo, xhi, sched, sem_x, sem_y, sem_w) = refs[3 * nparts + 3:]
    blk_words = tm * ROW_WORDS
    half = ROW_WORDS * LANES
    shift = tm.bit_length() - 1

    def n_blocks_of(p, e):
        return lax.shift_right_logical(cnt_refs[p][e] + (tm - 1), shift)

    def n_all(e):
        n = n_blocks_of(0, e)
        for p in range(1, nparts):
            n = n + n_blocks_of(p, e)
        return n

    def next_nonempty(e):
        return lax.while_loop(
            lambda c: jnp.logical_and(c < ne, n_all(jnp.minimum(c, ne - 1)) == 0), lambda c: c + 1, e)

    def plan(e, carry):
        i, starts = carry[0], list(carry[1:])
        for p in range(nparts):
            n = n_blocks_of(p, e)

            def put(j, c, p=p, i=i, start=starts[p]):
                sched[i + j] = (start + j) + (p << PART_SHIFT)
                return c

            lax.fori_loop(0, n, put, 0)
            i = i + n
            starts[p] = starts[p] + n
        return (i, *starts)

    nu = lax.fori_loop(0, ne, plan, (jnp.int32(0),) * (nparts + 1))[0]

    def rows_of(code):
        blk = code & ((1 << PART_SHIFT) - 1)
        return pl.ds(pl.multiple_of(blk * blk_words, blk_words), blk_words)

    def x_start(i, slot):
        code = sched[i]
        for p in range(nparts):
            @pl.when(lax.shift_right_logical(code, PART_SHIFT) == p)
            def _(p=p):
                pltpu.make_async_copy(xs_refs[p].at[rows_of(code), :], xbuf.at[slot], sem_x.at[slot]).start()

    def x_wait(slot):
        pltpu.make_async_copy(xs_refs[0].at[pl.ds(0, blk_words), :], xbuf.at[slot], sem_x.at[slot]).wait()

    def y_start(i, slot):
        code = sched[i]
        for p in range(nparts):
            @pl.when(lax.shift_right_logical(code, PART_SHIFT) == p)
            def _(p=p):
                pltpu.make_async_copy(ybuf.at[slot], ys_refs[p].at[rows_of(code), :], sem_y.at[slot]).start()

    def y_wait(slot):
        pltpu.make_async_copy(ybuf.at[slot], ys_refs[0].at[pl.ds(0, blk_words), :], sem_y.at[slot]).wait()

    def w_copies(e, ws):
        return [pltpu.make_async_copy(src.at[e], dst.at[ws], sem_w.at[ws])
                for src, dst in ((w1_ref, w1f), (w3_ref, w3f), (w2_ref, w2f))]

    for q in range(X_SLOTS - 1):
        @pl.when(q < nu)
        def _(q=q):
            x_start(q, q)

    e_first = next_nonempty(jnp.int32(0))

    @pl.when(e_first < ne)
    def _():
        for cp in w_copies(e_first, 0):
            cp.start()

    def block(i):
        slot = i % X_SLOTS
        yslot = i % Y_SLOTS
        x_wait(slot)

        @pl.when(i + (X_SLOTS - 1) < nu)
        def _():
            x_start(i + (X_SLOTS - 1), (i + (X_SLOTS - 1)) % X_SLOTS)

        @pl.when(i >= Y_SLOTS)
        def _():
            y_wait(yslot)

        xin = xbuf.at[slot]
        for w in range(ROW_WORDS):
            lo, hi = _unpack_halves(_load_row_word(xin, w, tm))
            xlo[:, LANES * w:LANES * (w + 1)] = lo.astype(BF16)
            xhi[:, LANES * w:LANES * (w + 1)] = hi.astype(BF16)

        def up(wsc):
            return (jnp.dot(xlo[...], wsc[0:half, :], preferred_element_type=F32)
                    + jnp.dot(xhi[...], wsc[half:2 * half, :], preferred_element_type=F32))

        hid = (_silu(up(w1s)) * up(w3s)).astype(BF16)
        y = jnp.dot(hid, w2s[...], preferred_element_type=F32)
        _store_rows(ybuf.at[yslot], _pack_halves(y[:, 0:half], y[:, half:2 * half]))
        y_start(i, yslot)

    def per_expert(e, carry):
        i0, ws = carry
        n = n_all(e)

        @pl.when(n > 0)
        def _():
            for cp in w_copies(e, ws):
                cp.wait()
            w1s[...] = w1f[ws].astype(BF16)
            w3s[...] = w3f[ws].astype(BF16)
            w2s[...] = w2f[ws].astype(BF16)
            e_next = next_nonempty(e + 1)

            @pl.when(e_next < ne)
            def _():
                for cp in w_copies(e_next, 1 - ws):
                    cp.start()

            def body(j, c):
                block(i0 + j)
                return c

            lax.fori_loop(0, n, body, 0)

        return i0 + n, jnp.where(n > 0, 1 - ws, ws)

    lax.fori_loop(0, ne, per_expert, (jnp.int32(0), jnp.int32(0)))

    for q in range(Y_SLOTS):
        @pl.when(nu > q)
        def _(q=q):
            y_wait((nu - 1 - q) % Y_SLOTS)


def _experts(counts, xss, w1e, w3e, w2e):
    tm = EXPERT_ROWS
    ne, d, de = w1e.shape
    half = ROW_WORDS * LANES
    nparts = len(xss)
    cap = sum(x.shape[0] // (tm * ROW_WORDS) for x in xss)
    assert cap < (1 << PART_SHIFT)
    anyspec = pl.BlockSpec(memory_space=pl.ANY)
    blk_buf = lambda n: pltpu.VMEM((n, tm * ROW_WORDS, LANES), U32)
    return pl.pallas_call(
        functools.partial(_expert_body, tm=tm, ne=ne, nparts=nparts),
        grid_spec=pltpu.PrefetchScalarGridSpec(
            num_scalar_prefetch=nparts,
            grid=(1,),
            in_specs=[anyspec] * (nparts + 3),
            out_specs=[anyspec] * nparts,
            scratch_shapes=[pltpu.VMEM((2, d, de), F32), pltpu.VMEM((2, d, de), F32), pltpu.VMEM((2, de, d), F32),
                            pltpu.VMEM((d, de), BF16), pltpu.VMEM((d, de), BF16), pltpu.VMEM((de, d), BF16),
                            blk_buf(X_SLOTS), blk_buf(Y_SLOTS),
                            pltpu.VMEM((tm, half), BF16), pltpu.VMEM((tm, half), BF16),
                            pltpu.SMEM((cap,), I32),
                            pltpu.SemaphoreType.DMA((X_SLOTS,)), pltpu.SemaphoreType.DMA((Y_SLOTS,)),
                            pltpu.SemaphoreType.DMA((2,))]),
        out_shape=[jax.ShapeDtypeStruct(x.shape, U32) for x in xss],
        compiler_params=pltpu.CompilerParams(dimension_semantics=("arbitrary",), vmem_limit_bytes=VMEM_LIMIT),
        name="experts",
    )(*counts, *xss, w1e, w3e, w2e)


def _combine_body(z_ref, x2_ref, gtf_ref, wt_ref, gfin_ref, y_ref, xo, *, td, topk):
    half = ROW_WORDS * LANES
    wt = wt_ref[...].T
    ws = [wt[:, k:k + 1] for k in range(topk)]
    sq = jnp.zeros((td, 1), F32)
    for j in range(ROW_WORDS):
        acc_lo = jnp.zeros((td, LANES), F32)
        acc_hi = jnp.zeros((td, LANES), F32)
        for k in range(topk):
            lo, hi = _unpack_halves(_load_row_word(z_ref.at[k], j, td))
            acc_lo = acc_lo + ws[k] * lo
            acc_hi = acc_hi + ws[k] * hi
        for base, acc in ((0, acc_lo), (half, acc_hi)):
            cols = slice(base + LANES * j, base + LANES * (j + 1))
            x = x2_ref[:, cols] + gtf_ref[:, cols] * acc
            xo[:, cols] = x
            sq = sq + jnp.sum(x * x, axis=-1, keepdims=True)
    rs = lax.rsqrt(sq / (2 * half) + EPS)
    y_ref[...] = xo[...] * rs * gfin_ref[...]


def _combine_body_into(z_ref, x2_ref, gtf_ref, wt_ref, gfin_ref, prev_ref, y_ref, xo, *, td, topk):
    del prev_ref
    _combine_body(z_ref, x2_ref, gtf_ref, wt_ref, gfin_ref, y_ref, xo, td=td, topk=topk)


def _combine(z, token0, wts, x2, mod, g_final, *, rows_per_mod, per_row_mod, b0=0, out_rows=None, into=None):
    t, d = x2.shape
    td = min(t, COMBINE_TILE)
    tile0 = token0 // td
    out_rows = t if out_rows is None else out_rows
    if per_row_mod:
        gtf_spec = pl.BlockSpec((td, d), lambda i: (i, 5))
        out0 = 0
    else:
        tiles_per_mod = rows_per_mod // td
        gtf_spec = pl.BlockSpec((None, None, 1, d), lambda i: (i // tiles_per_mod + b0, 5, 0, 0))
        out0 = b0 * tiles_per_mod
    in_specs = [pl.BlockSpec((TOP_K, td * ROW_WORDS, LANES), lambda i: (0, i + tile0, 0)),
                pl.BlockSpec((td, d), lambda i: (i, 0)),
                gtf_spec,
                pl.BlockSpec((SUBLANES, td), lambda i: (0, i + tile0)),
                pl.BlockSpec((1, d), lambda i: (0, 0))]
    args = [z, x2, mod, wts, g_final]
    body, aliases = _combine_body, {}
    if into is not None:
        in_specs.append(pl.BlockSpec(memory_space=pl.ANY))
        args.append(into)
        body, aliases = _combine_body_into, {len(args) - 1: 0}
    return pl.pallas_call(
        functools.partial(body, td=td, topk=TOP_K),
        grid=(t // td,),
        in_specs=in_specs,
        out_specs=pl.BlockSpec((td, d), lambda i: (i + out0, 0)),
        scratch_shapes=[pltpu.VMEM((td, d), F32)],
        out_shape=jax.ShapeDtypeStruct((out_rows, d), F32),
        input_output_aliases=aliases,
        compiler_params=pltpu.CompilerParams(vmem_limit_bytes=VMEM_LIMIT),
        name="combine",
    )(*args)


def _retention_tables(length, nh, hd):
    c = math.gcd(length, RET_CHUNK)
    log_g = jnp.log(1.0 - 2.0 ** (-5.0 - jnp.arange(nh, dtype=F32)))
    idx = jnp.arange(c, dtype=F32)
    rel = idx[:, None] - idx[None, :]
    mask = jnp.where(rel >= 0, jnp.exp(log_g[:, None, None] * jnp.maximum(rel, 0.0)), 0.0)
    q_decay = jnp.exp(log_g[None, :] * (idx[:, None] + 1.0))
    k_decay = jnp.exp(log_g[None, :] * (c - 1.0 - idx[:, None]))
    chunk_decay = jnp.exp(log_g * c)
    qd = jnp.broadcast_to(q_decay.T[:, :, None], (nh, c, hd))
    kd = jnp.broadcast_to(k_decay.T[:, :, None], (nh, c, hd))
    cd = jnp.broadcast_to(chunk_decay[:, None, None], (nh, hd, hd))
    return mask, qd, kd, cd


def kernel(x_prompt, x_sample, c_prompt, c_sample, state_conv, state_ret, w_ada, b_ada, g_mix, g_ffn, w_in,
           conv_w, conv_b, conv_norm_g, conv_norm_b, ret_norm_g, ret_norm_b, w_out, w_router, router_bias,
           w1, w3, w2, ws1, ws3, ws2, g_final):
    depth = w_ada.shape[0]
    assert depth == 1, "single-layer trunk"
    bp, lp, d = x_prompt.shape
    bs, ls, _ = x_sample.shape
    assert ls == 1
    dc = conv_w.shape[2]
    dr = ret_norm_g.shape[1]
    nh = RET_HEADS
    hd = dr // nh
    assert hd == LANES and lp % 256 == 0 and bs % TOKEN_TILE == 0 and d // 2 == ROW_WORDS * LANES
    ne = w_router.shape[2]
    row = lambda a: a.reshape(1, -1)

    mod = _ada(jnp.concatenate([c_prompt, c_sample], axis=0), w_ada[0], row(b_ada[0]))
    mod_p = mod[:bp].reshape(bp, 6, 1, d)
    mod_s = mod[bp:]

    half = hd // 2
    inv = ROPE_BASE ** (-jnp.arange(half, dtype=F32) / half)
    inv2 = jnp.concatenate([inv, inv]).reshape(1, hd)
    cos_p, sin_p = _rope_tables(inv2, lp, 0)
    cos_s, sin_s = _rope_tables(inv2, SUBLANES, PAST_LEN)

    w_in_b = w_in[0].astype(BF16)
    wo_b = w_out[0].astype(BF16)
    wr_t = w_router[0].T
    wrh = wr_t.astype(BF16)
    wrl = (wr_t - wrh.astype(F32)).astype(BF16)
    ws1_b, ws3_b, ws2_b = ws1[0].astype(BF16), ws3[0].astype(BF16), ws2[0].astype(BF16)
    dims = dict(dc=dc, dr=dr, hd=hd)

    tables = _retention_tables(lp, nh, hd)
    log_g = jnp.log(1.0 - 2.0 ** (-5.0 - jnp.arange(nh, dtype=F32)))
    gam = jnp.broadcast_to(jnp.exp(log_g)[:, None, None], (nh, SUBLANES, hd))
    norm_rows = (row(conv_b[0]), row(conv_norm_g[0]), row(conv_norm_b[0]), row(ret_norm_g[0]), row(ret_norm_b[0]))
    post_w = (row(g_ffn[0]), wo_b, wrh, wrl, ws1_b, ws3_b, ws2_b)
    bias_col = router_bias[0].reshape(ne, 1)
    hw = d // 2
    tm = EXPERT_ROWS

    def pre_prompt(b0, nb):
        glu, q, k, v, sg = _proj(x_prompt, mod_p, row(g_mix[0]), w_in_b, cos_p, sin_p,
                                 per_row_mod=False, b0=b0, nb=nb, **dims)
        cat, ret = _mix(glu, q, k, v, sg, conv_w[0], *norm_rows, tables, nb=nb, length=lp, **dims)
        x2, hp, lg = _post(cat, x_prompt, mod_p, *post_w, per_row_mod=False, b0=b0, nb=nb)
        return glu, ret, x2, hp, lg

    def pre_sample():
        xs3 = x_sample.reshape(1, bs, d)
        glu, q, k, v, sg = _proj(xs3, mod_s, row(g_mix[0]), w_in_b, cos_s, sin_s, per_row_mod=True, **dims)
        cat, ret, new_conv = _mix1(glu, q, k, v, sg, state_conv, state_ret, conv_w[0], *norm_rows, gam,
                                   **dims)
        x2, hp, lg = _post(cat, xs3, mod_s, *post_w, per_row_mod=True)
        return new_conv, ret, x2, hp, lg

    def route_and_dispatch(hps, lgs):
        lg = lgs[0] if len(lgs) == 1 else jnp.concatenate(lgs, axis=1)
        tokens = lg.shape[1]
        eidx, wts, rank, cnt = _route(lg, bias_col)
        counts = cnt[:, 0].astype(I32)
        n_rows = -(-(tokens * TOP_K + ne * (tm - 1)) // tm) * tm
        dest3 = _dest_rows(counts, eidx, rank).reshape(tokens // TOKEN_TILE, SUBLANES, TOKEN_TILE)
        xs = _dispatch([h.reshape(-1, hw) for h in hps], dest3, n_rows)
        return counts, dest3, wts, xs.reshape(n_rows * ROW_WORDS, LANES)

    def undispatch(ys, dest3):
        tokens = dest3.shape[0] * TOKEN_TILE
        return _undispatch(ys.reshape(-1, hw), dest3, tokens).reshape(TOP_K, tokens * ROW_WORDS, LANES)

    nb0 = bp // 2
    nb1 = bp - nb0
    glu_0, ret_0, x2_0, hp_0, lg_0 = pre_prompt(0, nb0)
    counts_0, dest_0, wts_0, xs_0 = route_and_dispatch([hp_0], [lg_0])
    glu_1, ret_1, x2_1, hp_1, lg_1 = pre_prompt(nb0, nb1)
    new_conv_s, ret_s, x2_s, hp_s, lg_s = pre_sample()
    counts_1, dest_1, wts_1, xs_1 = route_and_dispatch([hp_1, hp_s], [lg_1, lg_s])
    ys_0, ys_1 = _experts((counts_0, counts_1), (xs_0, xs_1), w1[0], w3[0], w2[0])
    z_0 = undispatch(ys_0, dest_0)
    z_1 = undispatch(ys_1, dest_1)
    y_p = _combine(z_0, 0, wts_0, x2_0, mod_p, row(g_final), rows_per_mod=lp, per_row_mod=False,
                   out_rows=bp * lp)
    y_p = _combine(z_1, 0, wts_1, x2_1, mod_p, row(g_final), rows_per_mod=lp, per_row_mod=False,
                   b0=nb0, out_rows=bp * lp, into=y_p)
    y_s = _combine(z_1, nb1 * lp, wts_1, x2_s, mod_s, row(g_final), rows_per_mod=bs, per_row_mod=True)
    ret_p = jnp.concatenate([ret_0, ret_1], axis=0)

    tail = lambda g, n: g.reshape(n, lp, dc)[:, lp - CONV_BUF:, :]
    new_conv_p = jnp.concatenate([tail(glu_0, nb0), tail(glu_1, nb1)], axis=0)
    return (y_p.reshape(bp, lp, d), y_s.reshape(bs, ls, d), new_conv_p[None], ret_p[None],
            new_conv_s, ret_s)
```

```python
import functools
import math

import jax
import jax.numpy as jnp
from jax import lax
from jax.experimental import pallas as pl
from jax.experimental.pallas import tpu as pltpu
from jax.experimental.pallas import tpu_sc as plsc

F32 = jnp.float32
BF16 = jnp.bfloat16
U32 = jnp.uint32
I32 = jnp.int32

EPS = 1e-6
PAST_LEN = 16384
RET_HEADS = 4
RET_CHUNK = 128
CONV_WIDTH = 31
CONV_BUF = CONV_WIDTH - 1
ROPE_BASE = 10000.0
N_EXPERTS = 64
TOP_K = 6
N_GROUPS = 8
TOPK_GROUPS = 4
ROUTED_SCALE = 2.5

LANES = 128
SUBLANES = 8
CONV_PAD = 32
EXPERT_ROWS = 256
ROUTE_TILE = 640
TOKEN_TILE = 128
COMBINE_TILE = 512
VMEM_LIMIT = 56 * 1024 * 1024
SC_CORES = 2
SC_SUBCORES = 16
SC_WORKERS = SC_CORES * SC_SUBCORES
SCATTER_ROWS = 64
GATHER_ROWS = 32
ROW_WORDS = 4
X_SLOTS = 4
Y_SLOTS = 3

HI_MASK = 0xFFFF0000


def _sigmoid(x):
    return jax.nn.sigmoid(x)


def _silu(x):
    return x * jax.nn.sigmoid(x)


def _pack_halves(lo, hi):
    lo_u = lax.bitcast_convert_type(lo.astype(BF16).astype(F32), U32) >> 16
    hi_u = lax.bitcast_convert_type(hi.astype(BF16).astype(F32), U32) & jnp.uint32(HI_MASK)
    return hi_u | lo_u


def _unpack_halves(p):
    lo = lax.bitcast_convert_type(p << 16, F32)
    hi = lax.bitcast_convert_type(p & jnp.uint32(HI_MASK), F32)
    return lo, hi


def _store_rows(ref, x):
    rows = x.shape[0]
    for j in range(ROW_WORDS):
        ref[pl.ds(j, rows, stride=ROW_WORDS), :] = x[:, LANES * j:LANES * (j + 1)]


def _load_row_word(ref, j, rows):
    return ref[pl.ds(j, rows, stride=ROW_WORDS), :]


def _ada_body(c_ref, w_ref, b_ref, o_ref):
    s = _silu(c_ref[...]).astype(BF16)
    o_ref[...] = jnp.dot(s, w_ref[...].astype(BF16), preferred_element_type=F32) + b_ref[...]


def _ada(c_all, w_ada, b_ada):
    rows, d = c_all.shape
    n = w_ada.shape[1]
    tn = 2048
    return pl.pallas_call(
        _ada_body,
        grid=(n // tn,),
        in_specs=[
            pl.BlockSpec((rows, d), lambda j: (0, 0)),
            pl.BlockSpec((d, tn), lambda j: (0, j)),
            pl.BlockSpec((1, tn), lambda j: (0, j)),
        ],
        out_specs=pl.BlockSpec((rows, tn), lambda j: (0, j)),
        out_shape=jax.ShapeDtypeStruct((rows, n), F32),
        compiler_params=pltpu.CompilerParams(vmem_limit_bytes=VMEM_LIMIT),
        name="ada",
    )(c_all, w_ada, b_ada)


def _rope_body(inv_ref, cos_ref, sin_ref, *, pos0, tl, half):
    row = lax.broadcasted_iota(I32, (tl, LANES), 0) + pl.program_id(0) * tl
    ang = (row.astype(F32) + pos0) * inv_ref[...]
    lane = lax.broadcasted_iota(I32, (tl, LANES), 1)
    s = jnp.sin(ang)
    cos_ref[...] = jnp.cos(ang)
    sin_ref[...] = jnp.where(lane < half, -s, s)


def _rope_tables(inv2, rows, pos0):
    tl = min(rows, 256)
    return pl.pallas_call(
        functools.partial(_rope_body, pos0=float(pos0), tl=tl, half=LANES // 2),
        grid=(rows // tl,),
        in_specs=[pl.BlockSpec((1, LANES), lambda i: (0, 0))],
        out_specs=[pl.BlockSpec((tl, LANES), lambda i: (i, 0))] * 2,
        out_shape=[jax.ShapeDtypeStruct((rows, LANES), F32)] * 2,
        name="rope",
    )(inv2)


def _modulated_rmsnorm(x, g, sc, sh):
    ms = jnp.mean(x * x, axis=-1, keepdims=True)
    h = x * lax.rsqrt(ms + EPS) * g
    return h * (1.0 + sc) + sh


def _proj_body(x_ref, sh_ref, sc_ref, g_ref, w_ref, cos_ref, sin_ref,
               glu_ref, q_ref, k_ref, v_ref, sg_ref, *, dc, dr, hd, rope_rows):
    hb = _modulated_rmsnorm(x_ref[...], g_ref[...], sc_ref[...], sh_ref[...]).astype(BF16)

    def proj(lo, n):
        return jnp.dot(hb, w_ref[:, lo:lo + n], preferred_element_type=F32)

    glu_ref[...] = proj(0, dc) * _sigmoid(proj(dc, dc))
    cos = cos_ref[...] if rope_rows else cos_ref[0:1, :]
    sin = sin_ref[...] if rope_rows else sin_ref[0:1, :]
    for ref, lo, scale in ((q_ref, 2 * dc, hd ** -0.5), (k_ref, 2 * dc + dr, None)):
        t = proj(lo, dr)
        for hh in range(dr // hd):
            th = t[:, hh * hd:(hh + 1) * hd]
            r = th * cos + pltpu.roll(th, hd // 2, 1) * sin
            if scale is not None:
                r = r * scale
            ref[:, hh * hd:(hh + 1) * hd] = r.astype(BF16)
    v_ref[...] = proj(2 * dc + 2 * dr, dr).astype(BF16)
    sg_ref[...] = _silu(proj(2 * dc + 3 * dr, dr)).astype(BF16)


def _proj(x, mod, g_mix, w_in_b, cos2, sin2, *, dc, dr, hd, per_row_mod, b0=0, nb=None):
    nb_all, length, d = x.shape
    nb = nb_all if nb is None else nb
    tl = min(length, 256)
    x2 = x.reshape(nb_all * length, d)
    nl = length // tl
    if per_row_mod:
        mod_spec = lambda j: pl.BlockSpec((tl, d), lambda b, l: (l, j))
        rope_spec = pl.BlockSpec((SUBLANES, LANES), lambda b, l: (0, 0))
    else:
        mod_spec = lambda j: pl.BlockSpec((None, None, 1, d), lambda b, l: (b + b0, j, 0, 0))
        rope_spec = pl.BlockSpec((tl, LANES), lambda b, l: (l, 0))
    row_spec = lambda w: pl.BlockSpec((tl, w), lambda b, l: (b * nl + l, 0))
    t = nb * length
    outs = pl.pallas_call(
        functools.partial(_proj_body, dc=dc, dr=dr, hd=hd, rope_rows=not per_row_mod),
        grid=(nb, nl),
        in_specs=[
            pl.BlockSpec((tl, d), lambda b, l: ((b + b0) * nl + l, 0)), mod_spec(0), mod_spec(1),
            pl.BlockSpec((1, d), lambda b, l: (0, 0)),
            pl.BlockSpec(w_in_b.shape, lambda b, l: (0, 0)),
            rope_spec, rope_spec,
        ],
        out_specs=[row_spec(dc), row_spec(dr), row_spec(dr), row_spec(dr), row_spec(dr)],
        out_shape=[
            jax.ShapeDtypeStruct((t, dc), F32),
            jax.ShapeDtypeStruct((t, dr), BF16),
            jax.ShapeDtypeStruct((t, dr), BF16),
            jax.ShapeDtypeStruct((t, dr), BF16),
            jax.ShapeDtypeStruct((t, dr), BF16),
        ],
        compiler_params=pltpu.CompilerParams(vmem_limit_bytes=VMEM_LIMIT),
        name="proj",
    )(x2, mod, mod, g_mix, w_in_b, cos2, sin2)
    return outs


def _layernorm_silu(c, g, b):
    mu = jnp.mean(c, axis=-1, keepdims=True)
    d = c - mu
    var = jnp.mean(d * d, axis=-1, keepdims=True)
    return _silu(d * lax.rsqrt(var + EPS) * g + b)


def _groupnorm(o, g, b):
    mu = jnp.mean(o, axis=-1, keepdims=True)
    d = o - mu
    var = jnp.mean(d * d, axis=-1, keepdims=True)
    return d * lax.rsqrt(var + EPS) * g + b


def _mix_body(glu_ref, q_ref, k_ref, v_ref, sg_ref, cw_ref, cb_ref, lng_ref, lnb_ref, rg_ref, rb_ref,
              mask_ref, qd_ref, kd_ref, cd_ref, cat_ref, st_ref, buf, cscr, *, tl, dc, hd, nh, chunk):
    nslab = dc // LANES

    @pl.when(pl.program_id(1) == 0)
    def _():
        buf[:, 0:CONV_PAD, :] = jnp.zeros((nslab, CONV_PAD, LANES), F32)
        st_ref[...] = jnp.zeros(st_ref.shape, F32)

    for j in range(nslab):
        buf[j, CONV_PAD:CONV_PAD + tl, :] = glu_ref[:, LANES * j:LANES * (j + 1)]
    first = CONV_PAD - CONV_BUF
    rows_per_iter = 8 * SUBLANES
    for j in range(nslab):
        cols = slice(LANES * j, LANES * (j + 1))
        wv = [jnp.broadcast_to(cw_ref[t:t + 1, cols], (SUBLANES, LANES)) for t in range(CONV_WIDTH)]
        bias = jnp.broadcast_to(cb_ref[0:1, cols], (SUBLANES, LANES))

        def body(r, carry, j=j, cols=cols, wv=wv, bias=bias):
            base = pl.multiple_of(r * rows_per_iter, rows_per_iter)
            for u in range(rows_per_iter // SUBLANES):
                acc = bias
                for t in range(CONV_WIDTH):
                    acc = acc + wv[t] * buf[j, pl.ds(base + (u * SUBLANES + first + t), SUBLANES), :]
                cscr[pl.ds(base + u * SUBLANES, SUBLANES), cols] = acc
            return carry

        lax.fori_loop(0, tl // rows_per_iter, body, 0)
    for j in range(nslab):
        buf[j, 0:CONV_PAD, :] = buf[j, tl:tl + CONV_PAD, :]
    cat_ref[:, 0:dc] = _layernorm_silu(cscr[...], lng_ref[...], lnb_ref[...]).astype(BF16)

    nt = (((1,), (1,)), ((), ()))
    tn = (((0,), (0,)), ((), ()))
    for c in range(tl // chunk):
        rows = slice(c * chunk, (c + 1) * chunk)
        for hh in range(nh):
            cols = slice(hh * hd, (hh + 1) * hd)
            qh = q_ref[rows, cols]
            kh = k_ref[rows, cols]
            vh = v_ref[rows, cols]
            s = st_ref[0, hh]
            scores = lax.dot_general(qh, kh, nt, preferred_element_type=F32) * mask_ref[hh]
            inner = jnp.dot(scores.astype(BF16), vh, preferred_element_type=F32)
            qd = (qh.astype(F32) * qd_ref[hh]).astype(BF16)
            cross = jnp.dot(qd, s.astype(BF16), preferred_element_type=F32)
            kd = (kh.astype(F32) * kd_ref[hh]).astype(BF16)
            st_ref[0, hh] = cd_ref[hh] * s + lax.dot_general(kd, vh, tn, preferred_element_type=F32)
            o = _groupnorm(inner + cross, rg_ref[0:1, cols], rb_ref[0:1, cols])
            cat_ref[rows, dc + hh * hd:dc + (hh + 1) * hd] = (o * sg_ref[rows, cols].astype(F32)).astype(BF16)


def _mix(glu, q, k, v, sg, conv_w, conv_b, ln_g, ln_b, rg, rb, tables, *, nb, length, dc, dr, hd):
    nh = dr // hd
    chunk = math.gcd(length, RET_CHUNK)
    tl = min(length, 256)
    nl = length // tl
    mask, qd, kd, cd = tables
    row_spec = lambda w: pl.BlockSpec((tl, w), lambda b, l: (b * nl + l, 0))
    full = lambda a: pl.BlockSpec(a.shape, lambda b, l: (0,) * a.ndim)
    cat, st = pl.pallas_call(
        functools.partial(_mix_body, tl=tl, dc=dc, hd=hd, nh=nh, chunk=chunk),
        grid=(nb, nl),
        in_specs=[row_spec(dc), row_spec(dr), row_spec(dr), row_spec(dr), row_spec(dr),
                  full(conv_w), full(conv_b), full(ln_g), full(ln_b), full(rg), full(rb),
                  full(mask), full(qd), full(kd), full(cd)],
        out_specs=[row_spec(dc + dr), pl.BlockSpec((1, nh, hd, hd), lambda b, l: (b, 0, 0, 0))],
        out_shape=[jax.ShapeDtypeStruct((nb * length, dc + dr), BF16),
                   jax.ShapeDtypeStruct((nb, nh, hd, hd), F32)],
        scratch_shapes=[pltpu.VMEM((dc // LANES, tl + CONV_PAD, LANES), F32),
                        pltpu.VMEM((tl, dc), F32)],
        compiler_params=pltpu.CompilerParams(dimension_semantics=("arbitrary", "arbitrary")),
        name="mix",
    )(glu, q, k, v, sg, conv_w, conv_b, ln_g, ln_b, rg, rb, mask, qd, kd, cd)
    return cat, st


def _mix1_body(glu_ref, q_ref, k_ref, v_ref, sg_ref, sc_ref, s0_ref, cw_ref, cb_ref, lng_ref, lnb_ref,
               rg_ref, rb_ref, gam_ref, cat_ref, st_ref, nc_ref, cscr, oscr, qf, kf, vf, *, tb, dc, hd, nh):
    w_hist = cw_ref[0:CONV_BUF, :]
    w_last = cw_ref[CONV_BUF:CONV_WIDTH, :]
    for bb in range(tb):
        hist = jnp.sum(sc_ref[bb] * w_hist, axis=0, keepdims=True)
        cscr[bb:bb + 1, :] = hist + glu_ref[bb:bb + 1, :] * w_last + cb_ref[...]
        nc_ref[bb, 0:CONV_BUF - 1, :] = sc_ref[bb, 1:CONV_BUF, :]
        nc_ref[bb, CONV_BUF - 1:CONV_BUF, :] = glu_ref[bb:bb + 1, :]
    cat_ref[:, 0:dc] = _layernorm_silu(cscr[...], lng_ref[...], lnb_ref[...]).astype(BF16)

    tn = (((0,), (0,)), ((), ()))
    rowid = lax.broadcasted_iota(I32, (tb, hd), 0)
    qf[...] = q_ref[...].astype(F32)
    kf[...] = k_ref[...].astype(F32)
    vf[...] = v_ref[...].astype(F32)
    for hh in range(nh):
        cols = slice(hh * hd, (hh + 1) * hd)
        qa = q_ref[:, cols]
        ka = k_ref[:, cols]
        gam = gam_ref[hh, 0:1, :]
        for bb in range(tb):
            onehot = (rowid == bb).astype(BF16)
            qcol = lax.dot_general(qa, onehot, tn, preferred_element_type=F32)
            kcol = lax.dot_general(ka, onehot, tn, preferred_element_type=F32)
            s0 = s0_ref[bb, hh]
            qrow = qf[bb:bb + 1, cols]
            krow = kf[bb:bb + 1, cols]
            vrow = vf[bb:bb + 1, cols]
            qk = jnp.sum(qrow * krow, axis=-1, keepdims=True)
            cross = gam * jnp.sum(qcol * s0, axis=0, keepdims=True)
            st_ref[bb, hh] = gam * s0 + kcol * vrow
            oscr[bb:bb + 1, cols] = qk * vrow + cross
    for hh in range(nh):
        cols = slice(hh * hd, (hh + 1) * hd)
        o = _groupnorm(oscr[:, cols], rg_ref[0:1, cols], rb_ref[0:1, cols])
        cat_ref[:, dc + hh * hd:dc + (hh + 1) * hd] = (o * sg_ref[:, cols].astype(F32)).astype(BF16)


def _mix1(glu, q, k, v, sg, state_conv, state_ret, conv_w, conv_b, ln_g, ln_b, rg, rb, gam, *, dc, dr, hd):
    nb = glu.shape[0]
    nh = dr // hd
    tb = 16
    row_spec = lambda w: pl.BlockSpec((tb, w), lambda i: (i, 0))
    full = lambda a: pl.BlockSpec(a.shape, lambda i: (0,) * a.ndim)
    st_spec = pl.BlockSpec((None, tb, nh, hd, hd), lambda i: (0, i, 0, 0, 0))
    conv_spec = pl.BlockSpec((None, tb, CONV_BUF, dc), lambda i: (0, i, 0, 0))
    cat, st, new_conv = pl.pallas_call(
        functools.partial(_mix1_body, tb=tb, dc=dc, hd=hd, nh=nh),
        grid=(nb // tb,),
        in_specs=[row_spec(dc), row_spec(dr), row_spec(dr), row_spec(dr), row_spec(dr), conv_spec, st_spec,
                  full(conv_w), full(conv_b), full(ln_g), full(ln_b), full(rg), full(rb), full(gam)],
        out_specs=[row_spec(dc + dr), st_spec, conv_spec],
        out_shape=[jax.ShapeDtypeStruct((nb, dc + dr), BF16),
                   jax.ShapeDtypeStruct((1, nb, nh, hd, hd), F32),
                   jax.ShapeDtypeStruct((1, nb, CONV_BUF, dc), F32)],
        scratch_shapes=[pltpu.VMEM((tb, dc), F32)] + [pltpu.VMEM((tb, dr), F32)] * 4,
        compiler_params=pltpu.CompilerParams(vmem_limit_bytes=VMEM_LIMIT),
        name="mix1",
    )(glu, q, k, v, sg, state_conv, state_ret, conv_w, conv_b, ln_g, ln_b, rg, rb, gam)
    return cat, st, new_conv


def _post_body(cat_ref, x_ref, gtm_ref, scf_ref, shf_ref, gtf_ref, g_ref, wo_ref, wrh_ref, wrl_ref,
               ws1_ref, ws3_ref, ws2_ref, x2_ref, hp_ref, lg_ref):
    d = x_ref.shape[1]
    y = jnp.dot(cat_ref[...], wo_ref[...], preferred_element_type=F32)
    x1 = x_ref[...] + gtm_ref[...] * y
    h = _modulated_rmsnorm(x1, g_ref[...], scf_ref[...], shf_ref[...])
    hb = h.astype(BF16)
    _store_rows(hp_ref, _pack_halves(h[:, 0:d // 2], h[:, d // 2:d]))
    hl = (h - hb.astype(F32)).astype(BF16)
    nt = (((1,), (1,)), ((), ()))
    lg_ref[...] = (lax.dot_general(wrh_ref[...], hb, nt, preferred_element_type=F32)
                   + lax.dot_general(wrh_ref[...], hl, nt, preferred_element_type=F32)
                   + lax.dot_general(wrl_ref[...], hb, nt, preferred_element_type=F32))
    s1 = jnp.dot(hb, ws1_ref[...], preferred_element_type=F32)
    s3 = jnp.dot(hb, ws3_ref[...], preferred_element_type=F32)
    shared = jnp.dot((_silu(s1) * s3).astype(BF16), ws2_ref[...], preferred_element_type=F32)
    x2_ref[...] = x1 + gtf_ref[...] * shared


def _post(cat, x, mod, g_ffn, wo_b, wrh, wrl, ws1_b, ws3_b, ws2_b, *, per_row_mod, b0=0, nb=None):
    nb_all, length, d = x.shape
    nb = nb_all if nb is None else nb
    tl = min(length, 512)
    nl = length // tl
    t = nb * length
    ne = wrh.shape[0]
    x2d = x.reshape(nb_all * length, d)
    if per_row_mod:
        mod_spec = lambda j: pl.BlockSpec((tl, d), lambda b, l: (l, j))
    else:
        mod_spec = lambda j: pl.BlockSpec((None, None, 1, d), lambda b, l: (b + b0, j, 0, 0))
    row_spec = lambda w: pl.BlockSpec((tl, w), lambda b, l: (b * nl + l, 0))
    full = lambda a: pl.BlockSpec(a.shape, lambda b, l: (0,) * a.ndim)
    return pl.pallas_call(
        _post_body,
        grid=(nb, nl),
        in_specs=[row_spec(d), pl.BlockSpec((tl, d), lambda b, l: ((b + b0) * nl + l, 0)),
                  mod_spec(2), mod_spec(4), mod_spec(3), mod_spec(5),
                  full(g_ffn), full(wo_b), full(wrh), full(wrl), full(ws1_b), full(ws3_b), full(ws2_b)],
        out_specs=[row_spec(d),
                   pl.BlockSpec((tl * ROW_WORDS, LANES), lambda b, l: (b * nl + l, 0)),
                   pl.BlockSpec((ne, tl), lambda b, l: (0, b * nl + l))],
        out_shape=[jax.ShapeDtypeStruct((t, d), F32),
                   jax.ShapeDtypeStruct((t * ROW_WORDS, LANES), U32),
                   jax.ShapeDtypeStruct((ne, t), F32)],
        compiler_params=pltpu.CompilerParams(vmem_limit_bytes=VMEM_LIMIT),
        name="post",
    )(cat, x2d, mod, mod, mod, mod, g_ffn, wo_b, wrh, wrl, ws1_b, ws3_b, ws2_b)


def _first_max(x, idx, sentinel):
    m = jnp.max(x, axis=0, keepdims=True)
    f = jnp.min(jnp.where(x == m, idx, sentinel), axis=0, keepdims=True)
    return m, f


def _route_body(lg_ref, bias_ref, e_ref, w_ref, r_ref, cnt_ref, cnt_scr, *, tr, ne, ng, topk, topg):
    @pl.when(pl.program_id(0) == 0)
    def _():
        cnt_scr[...] = jnp.zeros(cnt_scr.shape, F32)

    per = ne // ng
    neg = -jnp.inf
    scores = _sigmoid(lg_ref[...])
    sel = scores + bias_ref[...]
    sub = lax.broadcasted_iota(I32, (per, tr), 0)
    gs = []
    for g in range(ng):
        s_g = sel[g * per:(g + 1) * per, :]
        m1, f1 = _first_max(s_g, sub, per)
        m2 = jnp.max(jnp.where(sub == f1, neg, s_g), axis=0, keepdims=True)
        gs.append(m1 + m2)
    gsc = jnp.concatenate(gs, axis=0)
    gi = lax.broadcasted_iota(I32, (ng, tr), 0)
    keep = jnp.zeros((ng, tr), F32)
    for _ in range(topg):
        _, f = _first_max(gsc, gi, ng)
        pick = gi == f
        keep = jnp.where(pick, 1.0, keep)
        gsc = jnp.where(pick, neg, gsc)
    work = jnp.concatenate(
        [jnp.where(keep[g:g + 1, :] > 0.5, sel[g * per:(g + 1) * per, :], neg) for g in range(ng)], axis=0)
    ei = lax.broadcasted_iota(I32, (ne, tr), 0)
    picks, es, ws = [], [], []
    for _ in range(topk):
        _, f = _first_max(work, ei, ne)
        pick = ei == f
        picks.append(pick)
        es.append(f)
        ws.append(jnp.sum(jnp.where(pick, scores, 0.0), axis=0, keepdims=True))
        work = jnp.where(pick, neg, work)
    wsum = ws[0]
    for w in ws[1:]:
        wsum = wsum + w
    scale = ROUTED_SCALE / wsum
    chosen = picks[0]
    for p in picks[1:]:
        chosen = jnp.logical_or(chosen, p)
    chosen_f = chosen.astype(F32)
    t_row = lax.broadcasted_iota(I32, (tr, tr), 0)
    t_col = lax.broadcasted_iota(I32, (tr, tr), 1)
    before = (t_row < t_col).astype(BF16)
    prior = cnt_scr[:, 0:1] + jnp.dot(chosen_f.astype(BF16), before, preferred_element_type=F32)
    rs = [jnp.sum(jnp.where(p, prior, 0.0), axis=0, keepdims=True).astype(I32) for p in picks]
    pad_i = jnp.zeros((SUBLANES - topk, tr), I32)
    pad_f = jnp.zeros((SUBLANES - topk, tr), F32)
    e_ref[...] = jnp.concatenate(es + [pad_i], axis=0)
    w_ref[...] = jnp.concatenate([w * scale for w in ws] + [pad_f], axis=0)
    r_ref[...] = jnp.concatenate(rs + [pad_i], axis=0)
    total = cnt_scr[:, 0:1] + jnp.sum(chosen_f, axis=1, keepdims=True)
    cnt_scr[...] = jnp.broadcast_to(total, cnt_scr.shape)
    cnt_ref[...] = jnp.broadcast_to(total, cnt_ref.shape)


def _route_tile(t):
    return max(m for m in range(LANES, ROUTE_TILE + 1, LANES) if t % m == 0)


def _route(logits_t, bias_col):
    ne, t = logits_t.shape
    tr = _route_tile(t)
    tok = lambda dt: jax.ShapeDtypeStruct((SUBLANES, t), dt)
    tok_spec = pl.BlockSpec((SUBLANES, tr), lambda i: (0, i))
    return pl.pallas_call(
        functools.partial(_route_body, tr=tr, ne=ne, ng=N_GROUPS, topk=TOP_K, topg=TOPK_GROUPS),
        grid=(t // tr,),
        in_specs=[pl.BlockSpec((ne, tr), lambda i: (0, i)), pl.BlockSpec((ne, 1), lambda i: (0, 0))],
        out_specs=[tok_spec, tok_spec, tok_spec, pl.BlockSpec((ne, LANES), lambda i: (0, 0))],
        out_shape=[tok(I32), tok(F32), tok(I32), jax.ShapeDtypeStruct((ne, LANES), F32)],
        scratch_shapes=[pltpu.VMEM((ne, LANES), F32)],
        compiler_params=pltpu.CompilerParams(dimension_semantics=("arbitrary",)),
        name="route",
    )(logits_t, bias_col)


def _dest_body(cnt_ref, e_ref, r_ref, d_ref, ps, *, ne, tr):
    shift = EXPERT_ROWS.bit_length() - 1

    @pl.when(pl.program_id(0) == 0)
    def _():
        def step(j, start):
            ps[j] = start
            return start + lax.shift_left(lax.shift_right_logical(cnt_ref[j] + (EXPERT_ROWS - 1), shift), shift)

        lax.fori_loop(0, ne, step, jnp.int32(0))

    e = e_ref[...]
    base = jnp.zeros(e.shape, I32)
    for j in range(ne):
        base = jnp.where(e == j, ps[j], base)
    dest = base + r_ref[...]
    for m in range(tr // TOKEN_TILE):
        d_ref[SUBLANES * m:SUBLANES * (m + 1), :] = dest[:, TOKEN_TILE * m:TOKEN_TILE * (m + 1)]


def _dest_rows(counts, eidx, rank):
    rows, t = eidx.shape
    ne = counts.shape[0]
    tr = _route_tile(t)
    spec = pl.BlockSpec((rows, tr), lambda i, cnt: (0, i))
    return pl.pallas_call(
        functools.partial(_dest_body, ne=ne, tr=tr),
        grid_spec=pltpu.PrefetchScalarGridSpec(
            num_scalar_prefetch=1, grid=(t // tr,), in_specs=[spec, spec],
            out_specs=pl.BlockSpec((tr // TOKEN_TILE * rows, TOKEN_TILE), lambda i, cnt: (i, 0)),
            scratch_shapes=[pltpu.SMEM((ne,), I32)]),
        out_shape=jax.ShapeDtypeStruct((t // TOKEN_TILE * rows, TOKEN_TILE), I32),
        compiler_params=pltpu.CompilerParams(dimension_semantics=("arbitrary",)),
        name="dest",
    )(counts, eidx, rank)


def _sc_mesh():
    return plsc.VectorSubcoreMesh(core_axis_name="c", subcore_axis_name="s")


def _sc_worker_id():
    return lax.axis_index("s") * SC_CORES + lax.axis_index("c")


def _index_block(dest_ref, chunk, width):
    per_tile = TOKEN_TILE // width
    return dest_ref.at[chunk // per_tile, :, pl.ds((chunk % per_tile) * width, width)]


def _dispatch(hps, dest3, n_rows):
    w = SCATTER_ROWS
    width = hps[0].shape[1]
    bounds = [0]
    for h in hps:
        bounds.append(bounds[-1] + h.shape[0] // w)
    nch = bounds[-1]
    nsrc = len(hps)

    @functools.partial(
        pl.kernel, mesh=_sc_mesh(),
        out_type=jax.ShapeDtypeStruct((n_rows, width), U32),
        scratch_types=[pltpu.VMEM((SUBLANES, w), I32), pltpu.VMEM((w, width), U32), pltpu.SemaphoreType.DMA],
        compiler_params=pltpu.CompilerParams(use_tc_tiling_on_sc=False),
        name="dispatch",
    )
    def run(*refs):
        src_refs, dest_ref, xs_ref = refs[:nsrc], refs[nsrc], refs[nsrc + 1]
        idx_v, rows_v, sem = refs[nsrc + 2:]
        wid = _sc_worker_id()

        @pl.loop(0, pl.cdiv(nch, SC_WORKERS))
        def _(r):
            c = r * SC_WORKERS + wid

            @pl.when(c < nch)
            def _():
                pltpu.sync_copy(_index_block(dest_ref, c, w), idx_v)
                for i, src in enumerate(src_refs):
                    @pl.when(jnp.logical_and(c >= bounds[i], c < bounds[i + 1]))
                    def _(src=src, lo=bounds[i]):
                        pltpu.sync_copy(src.at[pl.ds((c - lo) * w, w)], rows_v)

                copies = [pltpu.async_copy(rows_v, xs_ref.at[idx_v.at[k]], sem) for k in range(TOP_K)]
                for cp in copies:
                    cp.wait()

    return run(*hps, dest3)


def _undispatch(ys, dest3, n_tokens, after=None):
    w = GATHER_ROWS
    width = ys.shape[1]
    nch = n_tokens // w

    @functools.partial(
        pl.kernel, mesh=_sc_mesh(),
        out_type=jax.ShapeDtypeStruct((TOP_K, n_tokens, width), U32),
        scratch_types=[pltpu.VMEM((SUBLANES, w), I32), pltpu.VMEM((TOP_K, w, width), U32),
                       pltpu.SemaphoreType.DMA],
        compiler_params=pltpu.CompilerParams(use_tc_tiling_on_sc=False),
        name="undispatch",
    )
    def run(*refs):
        ys_ref, dest_ref = refs[:2]
        z_ref, idx_v, bufs, sem = refs[-4:]
        wid = _sc_worker_id()

        @pl.loop(0, pl.cdiv(nch, SC_WORKERS))
        def _(r):
            c = r * SC_WORKERS + wid

            @pl.when(c < nch)
            def _():
                pltpu.sync_copy(_index_block(dest_ref, c, w), idx_v)
                gathers = [pltpu.async_copy(ys_ref.at[idx_v.at[k]], bufs.at[k], sem) for k in range(TOP_K)]
                for cp in gathers:
                    cp.wait()
                stores = [pltpu.async_copy(bufs.at[k], z_ref.at[k, pl.ds(c * w, w)], sem) for k in range(TOP_K)]
                for cp in stores:
                    cp.wait()

    return run(ys, dest3) if after is None else run(ys, dest3, after)


PART_SHIFT = 24


def _expert_body(*refs, tm, ne, nparts):
    cnt_refs = refs[:nparts]
    xs_refs = refs[nparts:2 * nparts]
    w1_ref, w3_ref, w2_ref = refs[2 * nparts:2 * nparts + 3]
    ys_refs = refs[2 * nparts + 3:3 * nparts + 3]
    (w1f, w3f, w2f, w1s, w3s, w2s, xbuf, ybuf, xlo, xhi, sched, sem_x, sem_y, sem_w) = refs[3 * nparts + 3:]
    blk_words = tm * ROW_WORDS
    half = ROW_WORDS * LANES
    shift = tm.bit_length() - 1

    def n_blocks_of(p, e):
        return lax.shift_right_logical(cnt_refs[p][e] + (tm - 1), shift)

    def n_all(e):
        n = n_blocks_of(0, e)
        for p in range(1, nparts):
            n = n + n_blocks_of(p, e)
        return n

    def next_nonempty(e):
        return lax.while_loop(
            lambda c: jnp.logical_and(c < ne, n_all(jnp.minimum(c, ne - 1)) == 0), lambda c: c + 1, e)

    def plan(e, carry):
        i, starts = carry[0], list(carry[1:])
        for p in range(nparts):
            n = n_blocks_of(p, e)

            def put(j, c, p=p, i=i, start=starts[p]):
                sched[i + j] = (start + j) + (p << PART_SHIFT)
                return c

            lax.fori_loop(0, n, put, 0)
            i = i + n
            starts[p] = starts[p] + n
        return (i, *starts)

    nu = lax.fori_loop(0, ne, plan, (jnp.int32(0),) * (nparts + 1))[0]

    def rows_of(code):
        blk = code & ((1 << PART_SHIFT) - 1)
        return pl.ds(pl.multiple_of(blk * blk_words, blk_words), blk_words)

    def x_start(i, slot):
        code = sched[i]
        for p in range(nparts):
            @pl.when(lax.shift_right_logical(code, PART_SHIFT) == p)
            def _(p=p):
                pltpu.make_async_copy(xs_refs[p].at[rows_of(code), :], xbuf.at[slot], sem_x.at[slot]).start()

    def x_wait(slot):
        pltpu.make_async_copy(xs_refs[0].at[pl.ds(0, blk_words), :], xbuf.at[slot], sem_x.at[slot]).wait()

    def y_start(i, slot):
        code = sched[i]
        for p in range(nparts):
            @pl.when(lax.shift_right_logical(code, PART_SHIFT) == p)
            def _(p=p):
                pltpu.make_async_copy(ybuf.at[slot], ys_refs[p].at[rows_of(code), :], sem_y.at[slot]).start()

    def y_wait(slot):
        pltpu.make_async_copy(ybuf.at[slot], ys_refs[0].at[pl.ds(0, blk_words), :], sem_y.at[slot]).wait()

    def w_copies(e, ws):
        return [pltpu.make_async_copy(src.at[e], dst.at[ws], sem_w.at[ws])
                for src, dst in ((w1_ref, w1f), (w3_ref, w3f), (w2_ref, w2f))]

    for q in range(X_SLOTS - 1):
        @pl.when(q < nu)
        def _(q=q):
            x_start(q, q)

    e_first = next_nonempty(jnp.int32(0))

    @pl.when(e_first < ne)
    def _():
        for cp in w_copies(e_first, 0):
            cp.start()

    def block(i):
        slot = i % X_SLOTS
        yslot = i % Y_SLOTS
        x_wait(slot)

        @pl.when(i + (X_SLOTS - 1) < nu)
        def _():
            x_start(i + (X_SLOTS - 1), (i + (X_SLOTS - 1)) % X_SLOTS)

        @pl.when(i >= Y_SLOTS)
        def _():
            y_wait(yslot)

        xin = xbuf.at[slot]
        for w in range(ROW_WORDS):
            lo, hi = _unpack_halves(_load_row_word(xin, w, tm))
            xlo[:, LANES * w:LANES * (w + 1)] = lo.astype(BF16)
            xhi[:, LANES * w:LANES * (w + 1)] = hi.astype(BF16)

        def up(wsc):
            return (jnp.dot(xlo[...], wsc[0:half, :], preferred_element_type=F32)
                    + jnp.dot(xhi[...], wsc[half:2 * half, :], preferred_element_type=F32))

        hid = (_silu(up(w1s)) * up(w3s)).astype(BF16)
        y = jnp.dot(hid, w2s[...], preferred_element_type=F32)
        _store_rows(ybuf.at[yslot], _pack_halves(y[:, 0:half], y[:, half:2 * half]))
        y_start(i, yslot)

    def per_expert(e, carry):
        i0, ws = carry
        n = n_all(e)

        @pl.when(n > 0)
        def _():
            for cp in w_copies(e, ws):
                cp.wait()
            w1s[...] = w1f[ws].astype(BF16)
            w3s[...] = w3f[ws].astype(BF16)
            w2s[...] = w2f[ws].astype(BF16)
            e_next = next_nonempty(e + 1)

            @pl.when(e_next < ne)
            def _():
                for cp in w_copies(e_next, 1 - ws):
                    cp.start()

            def body(j, c):
                block(i0 + j)
                return c

            lax.fori_loop(0, n, body, 0)

        return i0 + n, jnp.where(n > 0, 1 - ws, ws)

    lax.fori_loop(0, ne, per_expert, (jnp.int32(0), jnp.int32(0)))

    for q in range(Y_SLOTS):
        @pl.when(nu > q)
        def _(q=q):
            y_wait((nu - 1 - q) % Y_SLOTS)


def _experts(counts, xss, w1e, w3e, w2e):
    tm = EXPERT_ROWS
    ne, d, de = w1e.shape
    half = ROW_WORDS * LANES
    nparts = len(xss)
    cap = sum(x.shape[0] // (tm * ROW_WORDS) for x in xss)
    assert cap < (1 << PART_SHIFT)
    anyspec = pl.BlockSpec(memory_space=pl.ANY)
    blk_buf = lambda n: pltpu.VMEM((n, tm * ROW_WORDS, LANES), U32)
    return pl.pallas_call(
        functools.partial(_expert_body, tm=tm, ne=ne, nparts=nparts),
        grid_spec=pltpu.PrefetchScalarGridSpec(
            num_scalar_prefetch=nparts,
            grid=(1,),
            in_specs=[anyspec] * (nparts + 3),
            out_specs=[anyspec] * nparts,
            scratch_shapes=[pltpu.VMEM((2, d, de), F32), pltpu.VMEM((2, d, de), F32), pltpu.VMEM((2, de, d), F32),
                            pltpu.VMEM((d, de), BF16), pltpu.VMEM((d, de), BF16), pltpu.VMEM((de, d), BF16),
                            blk_buf(X_SLOTS), blk_buf(Y_SLOTS),
                            pltpu.VMEM((tm, half), BF16), pltpu.VMEM((tm, half), BF16),
                            pltpu.SMEM((cap,), I32),
                            pltpu.SemaphoreType.DMA((X_SLOTS,)), pltpu.SemaphoreType.DMA((Y_SLOTS,)),
                            pltpu.SemaphoreType.DMA((2,))]),
        out_shape=[jax.ShapeDtypeStruct(x.shape, U32) for x in xss],
        compiler_params=pltpu.CompilerParams(dimension_semantics=("arbitrary",), vmem_limit_bytes=VMEM_LIMIT),
        name="experts",
    )(*counts, *xss, w1e, w3e, w2e)


def _combine_body(z_ref, x2_ref, gtf_ref, wt_ref, gfin_ref, y_ref, xo, *, td, topk):
    half = ROW_WORDS * LANES
    wt = wt_ref[...].T
    ws = [wt[:, k:k + 1] for k in range(topk)]
    sq = jnp.zeros((td, 1), F32)
    for j in range(ROW_WORDS):
        acc_lo = jnp.zeros((td, LANES), F32)
        acc_hi = jnp.zeros((td, LANES), F32)
        for k in range(topk):
            lo, hi = _unpack_halves(_load_row_word(z_ref.at[k], j, td))
            acc_lo = acc_lo + ws[k] * lo
            acc_hi = acc_hi + ws[k] * hi
        for base, acc in ((0, acc_lo), (half, acc_hi)):
            cols = slice(base + LANES * j, base + LANES * (j + 1))
            x = x2_ref[:, cols] + gtf_ref[:, cols] * acc
            xo[:, cols] = x
            sq = sq + jnp.sum(x * x, axis=-1, keepdims=True)
    rs = lax.rsqrt(sq / (2 * half) + EPS)
    y_ref[...] = xo[...] * rs * gfin_ref[...]


def _combine_body_into(z_ref, x2_ref, gtf_ref, wt_ref, gfin_ref, prev_ref, y_ref, xo, *, td, topk):
    del prev_ref
    _combine_body(z_ref, x2_ref, gtf_ref, wt_ref, gfin_ref, y_ref, xo, td=td, topk=topk)


def _combine(z, token0, wts, x2, mod, g_final, *, rows_per_mod, per_row_mod, b0=0, out_rows=None, into=None):
    t, d = x2.shape
    td = min(t, COMBINE_TILE)
    tile0 = token0 // td
    out_rows = t if out_rows is None else out_rows
    if per_row_mod:
        gtf_spec = pl.BlockSpec((td, d), lambda i: (i, 5))
        out0 = 0
    else:
        tiles_per_mod = rows_per_mod // td
        gtf_spec = pl.BlockSpec((None, None, 1, d), lambda i: (i // tiles_per_mod + b0, 5, 0, 0))
        out0 = b0 * tiles_per_mod
    in_specs = [pl.BlockSpec((TOP_K, td * ROW_WORDS, LANES), lambda i: (0, i + tile0, 0)),
                pl.BlockSpec((td, d), lambda i: (i, 0)),
                gtf_spec,
                pl.BlockSpec((SUBLANES, td), lambda i: (0, i + tile0)),
                pl.BlockSpec((1, d), lambda i: (0, 0))]
    args = [z, x2, mod, wts, g_final]
    body, aliases = _combine_body, {}
    if into is not None:
        in_specs.append(pl.BlockSpec(memory_space=pl.ANY))
        args.append(into)
        body, aliases = _combine_body_into, {len(args) - 1: 0}
    return pl.pallas_call(
        functools.partial(body, td=td, topk=TOP_K),
        grid=(t // td,),
        in_specs=in_specs,
        out_specs=pl.BlockSpec((td, d), lambda i: (i + out0, 0)),
        scratch_shapes=[pltpu.VMEM((td, d), F32)],
        out_shape=jax.ShapeDtypeStruct((out_rows, d), F32),
        input_output_aliases=aliases,
        compiler_params=pltpu.CompilerParams(vmem_limit_bytes=VMEM_LIMIT),
        name="combine",
    )(*args)


def _retention_tables(length, nh, hd):
    c = math.gcd(length, RET_CHUNK)
    log_g = jnp.log(1.0 - 2.0 ** (-5.0 - jnp.arange(nh, dtype=F32)))
    idx = jnp.arange(c, dtype=F32)
    rel = idx[:, None] - idx[None, :]
    mask = jnp.where(rel >= 0, jnp.exp(log_g[:, None, None] * jnp.maximum(rel, 0.0)), 0.0)
    q_decay = jnp.exp(log_g[None, :] * (idx[:, None] + 1.0))
    k_decay = jnp.exp(log_g[None, :] * (c - 1.0 - idx[:, None]))
    chunk_decay = jnp.exp(log_g * c)
    qd = jnp.broadcast_to(q_decay.T[:, :, None], (nh, c, hd))
    kd = jnp.broadcast_to(k_decay.T[:, :, None], (nh, c, hd))
    cd = jnp.broadcast_to(chunk_decay[:, None, None], (nh, hd, hd))
    return mask, qd, kd, cd


def kernel(x_prompt, x_sample, c_prompt, c_sample, state_conv, state_ret, w_ada, b_ada, g_mix, g_ffn, w_in,
           conv_w, conv_b, conv_norm_g, conv_norm_b, ret_norm_g, ret_norm_b, w_out, w_router, router_bias,
           w1, w3, w2, ws1, ws3, ws2, g_final):
    depth = w_ada.shape[0]
    assert depth == 1, "single-layer trunk"
    bp, lp, d = x_prompt.shape
    bs, ls, _ = x_sample.shape
    assert ls == 1
    dc = conv_w.shape[2]
    dr = ret_norm_g.shape[1]
    nh = RET_HEADS
    hd = dr // nh
    assert hd == LANES and lp % 256 == 0 and bs % TOKEN_TILE == 0 and d // 2 == ROW_WORDS * LANES
    ne = w_router.shape[2]
    row = lambda a: a.reshape(1, -1)

    mod = _ada(jnp.concatenate([c_prompt, c_sample], axis=0), w_ada[0], row(b_ada[0]))
    mod_p = mod[:bp].reshape(bp, 6, 1, d)
    mod_s = mod[bp:]

    half = hd // 2
    inv = ROPE_BASE ** (-jnp.arange(half, dtype=F32) / half)
    inv2 = jnp.concatenate([inv, inv]).reshape(1, hd)
    cos_p, sin_p = _rope_tables(inv2, lp, 0)
    cos_s, sin_s = _rope_tables(inv2, SUBLANES, PAST_LEN)

    w_in_b = w_in[0].astype(BF16)
    wo_b = w_out[0].astype(BF16)
    wr_t = w_router[0].T
    wrh = wr_t.astype(BF16)
    wrl = (wr_t - wrh.astype(F32)).astype(BF16)
    ws1_b, ws3_b, ws2_b = ws1[0].astype(BF16), ws3[0].astype(BF16), ws2[0].astype(BF16)
    dims = dict(dc=dc, dr=dr, hd=hd)

    tables = _retention_tables(lp, nh, hd)
    log_g = jnp.log(1.0 - 2.0 ** (-5.0 - jnp.arange(nh, dtype=F32)))
    gam = jnp.broadcast_to(jnp.exp(log_g)[:, None, None], (nh, SUBLANES, hd))
    norm_rows = (row(conv_b[0]), row(conv_norm_g[0]), row(conv_norm_b[0]), row(ret_norm_g[0]), row(ret_norm_b[0]))
    post_w = (row(g_ffn[0]), wo_b, wrh, wrl, ws1_b, ws3_b, ws2_b)
    bias_col = router_bias[0].reshape(ne, 1)
    hw = d // 2
    tm = EXPERT_ROWS

    def pre_prompt(b0, nb):
        glu, q, k, v, sg = _proj(x_prompt, mod_p, row(g_mix[0]), w_in_b, cos_p, sin_p,
                                 per_row_mod=False, b0=b0, nb=nb, **dims)
        cat, ret = _mix(glu, q, k, v, sg, conv_w[0], *norm_rows, tables, nb=nb, length=lp, **dims)
        x2, hp, lg = _post(cat, x_prompt, mod_p, *post_w, per_row_mod=False, b0=b0, nb=nb)
        return glu, ret, x2, hp, lg

    def pre_sample():
        xs3 = x_sample.reshape(1, bs, d)
        glu, q, k, v, sg = _proj(xs3, mod_s, row(g_mix[0]), w_in_b, cos_s, sin_s, per_row_mod=True, **dims)
        cat, ret, new_conv = _mix1(glu, q, k, v, sg, state_conv, state_ret, conv_w[0], *norm_rows, gam,
                                   **dims)
        x2, hp, lg = _post(cat, xs3, mod_s, *post_w, per_row_mod=True)
        return new_conv, ret, x2, hp, lg

    def route_and_dispatch(hps, lgs):
        lg = lgs[0] if len(lgs) == 1 else jnp.concatenate(lgs, axis=1)
        tokens = lg.shape[1]
        eidx, wts, rank, cnt = _route(lg, bias_col)
        counts = cnt[:, 0].astype(I32)
        n_rows = -(-(tokens * TOP_K + ne * (tm - 1)) // tm) * tm
        dest3 = _dest_rows(counts, eidx, rank).reshape(tokens // TOKEN_TILE, SUBLANES, TOKEN_TILE)
        xs = _dispatch([h.reshape(-1, hw) for h in hps], dest3, n_rows)
        return counts, dest3, wts, xs.reshape(n_rows * ROW_WORDS, LANES)

    def undispatch(ys, dest3, after=None):
        tokens = dest3.shape[0] * TOKEN_TILE
        z = _undispatch(ys.reshape(-1, hw), dest3, tokens, after)
        return z.reshape(TOP_K, tokens * ROW_WORDS, LANES)

    nb0 = bp // 2
    nb1 = bp - nb0
    glu_0, ret_0, x2_0, hp_0, lg_0 = pre_prompt(0, nb0)
    counts_0, dest_0, wts_0, xs_0 = route_and_dispatch([hp_0], [lg_0])
    glu_1, ret_1, x2_1, hp_1, lg_1 = pre_prompt(nb0, nb1)
    new_conv_s, ret_s, x2_s, hp_s, lg_s = pre_sample()
    counts_1, dest_1, wts_1, xs_1 = route_and_dispatch([hp_1, hp_s], [lg_1, lg_s])
    ys_0, ys_1 = _experts((counts_0, counts_1), (xs_0, xs_1), w1[0], w3[0], w2[0])
    z_0 = undispatch(ys_0, dest_0)
    z_1 = undispatch(ys_1, dest_1, after=z_0[0, :SUBLANES])
    y_p = _combine(z_0, 0, wts_0, x2_0, mod_p, row(g_final), rows_per_mod=lp, per_row_mod=False,
                   out_rows=bp * lp)
    y_p = _combine(z_1, 0, wts_1, x2_1, mod_p, row(g_final), rows_per_mod=lp, per_row_mod=False,
                   b0=nb0, out_rows=bp * lp, into=y_p)
    y_s = _combine(z_1, nb1 * lp, wts_1, x2_s, mod_s, row(g_final), rows_per_mod=bs, per_row_mod=True)
    ret_p = jnp.concatenate([ret_0, ret_1], axis=0)

    tail = lambda g, n: g.reshape(n, lp, dc)[:, lp - CONV_BUF:, :]
    new_conv_p = jnp.concatenate([tail(glu_0, nb0), tail(glu_1, nb1)], axis=0)
    return (y_p.reshape(bp, lp, d), y_s.reshape(bs, ls, d), new_conv_p[None], ret_p[None],
            new_conv_s, ret_s)
```

```python
import functools
import math

import jax
import jax.numpy as jnp
from jax import lax
from jax.experimental import pallas as pl
from jax.experimental.pallas import tpu as pltpu
from jax.experimental.pallas import tpu_sc as plsc

F32 = jnp.float32
BF16 = jnp.bfloat16
U32 = jnp.uint32
I32 = jnp.int32

EPS = 1e-6
PAST_LEN = 16384
RET_HEADS = 4
RET_CHUNK = 128
CONV_WIDTH = 31
CONV_BUF = CONV_WIDTH - 1
ROPE_BASE = 10000.0
N_EXPERTS = 64
TOP_K = 6
N_GROUPS = 8
TOPK_GROUPS = 4
ROUTED_SCALE = 2.5

LANES = 128
SUBLANES = 8
CONV_PAD = 32
EXPERT_ROWS = 256
ROUTE_TILE = 640
TOKEN_TILE = 128
COMBINE_TILE = 512
VMEM_LIMIT = 56 * 1024 * 1024
SC_CORES = 2
SC_SUBCORES = 16
SC_WORKERS = SC_CORES * SC_SUBCORES
SCATTER_ROWS = 64
GATHER_ROWS = 32
ROW_WORDS = 4
X_SLOTS = 6
X_AHEAD = 4
Y_SLOTS = 4

HI_MASK = 0xFFFF0000


def _sigmoid(x):
    return jax.nn.sigmoid(x)


def _silu(x):
    return x * jax.nn.sigmoid(x)


def _pack_halves(lo, hi):
    lo_u = lax.bitcast_convert_type(lo.astype(BF16).astype(F32), U32) >> 16
    hi_u = lax.bitcast_convert_type(hi.astype(BF16).astype(F32), U32) & jnp.uint32(HI_MASK)
    return hi_u | lo_u


def _unpack_halves(p):
    lo = lax.bitcast_convert_type(p << 16, F32)
    hi = lax.bitcast_convert_type(p & jnp.uint32(HI_MASK), F32)
    return lo, hi


def _store_rows(ref, x):
    rows = x.shape[0]
    for j in range(ROW_WORDS):
        ref[pl.ds(j, rows, stride=ROW_WORDS), :] = x[:, LANES * j:LANES * (j + 1)]


def _load_row_word(ref, j, rows):
    return ref[pl.ds(j, rows, stride=ROW_WORDS), :]


def _ada_body(c_ref, w_ref, b_ref, o_ref):
    s = _silu(c_ref[...]).astype(BF16)
    o_ref[...] = jnp.dot(s, w_ref[...].astype(BF16), preferred_element_type=F32) + b_ref[...]


def _ada(c_all, w_ada, b_ada):
    rows, d = c_all.shape
    n = w_ada.shape[1]
    tn = 2048
    return pl.pallas_call(
        _ada_body,
        grid=(n // tn,),
        in_specs=[
            pl.BlockSpec((rows, d), lambda j: (0, 0)),
            pl.BlockSpec((d, tn), lambda j: (0, j)),
            pl.BlockSpec((1, tn), lambda j: (0, j)),
        ],
        out_specs=pl.BlockSpec((rows, tn), lambda j: (0, j)),
        out_shape=jax.ShapeDtypeStruct((rows, n), F32),
        compiler_params=pltpu.CompilerParams(vmem_limit_bytes=VMEM_LIMIT),
        name="ada",
    )(c_all, w_ada, b_ada)


def _rope_body(inv_ref, cos_ref, sin_ref, *, pos0, tl, half):
    row = lax.broadcasted_iota(I32, (tl, LANES), 0) + pl.program_id(0) * tl
    ang = (row.astype(F32) + pos0) * inv_ref[...]
    lane = lax.broadcasted_iota(I32, (tl, LANES), 1)
    s = jnp.sin(ang)
    cos_ref[...] = jnp.cos(ang)
    sin_ref[...] = jnp.where(lane < half, -s, s)


def _rope_tables(inv2, rows, pos0):
    tl = min(rows, 256)
    return pl.pallas_call(
        functools.partial(_rope_body, pos0=float(pos0), tl=tl, half=LANES // 2),
        grid=(rows // tl,),
        in_specs=[pl.BlockSpec((1, LANES), lambda i: (0, 0))],
        out_specs=[pl.BlockSpec((tl, LANES), lambda i: (i, 0))] * 2,
        out_shape=[jax.ShapeDtypeStruct((rows, LANES), F32)] * 2,
        name="rope",
    )(inv2)


def _modulated_rmsnorm(x, g, sc, sh):
    ms = jnp.mean(x * x, axis=-1, keepdims=True)
    h = x * lax.rsqrt(ms + EPS) * g
    return h * (1.0 + sc) + sh


def _proj_body(x_ref, sh_ref, sc_ref, g_ref, w_ref, cos_ref, sin_ref,
               glu_ref, q_ref, k_ref, v_ref, sg_ref, *, dc, dr, hd, rope_rows):
    hb = _modulated_rmsnorm(x_ref[...], g_ref[...], sc_ref[...], sh_ref[...]).astype(BF16)

    def proj(lo, n):
        return jnp.dot(hb, w_ref[:, lo:lo + n], preferred_element_type=F32)

    glu_ref[...] = proj(0, dc) * _sigmoid(proj(dc, dc))
    cos = cos_ref[...] if rope_rows else cos_ref[0:1, :]
    sin = sin_ref[...] if rope_rows else sin_ref[0:1, :]
    for ref, lo, scale in ((q_ref, 2 * dc, hd ** -0.5), (k_ref, 2 * dc + dr, None)):
        t = proj(lo, dr)
        for hh in range(dr // hd):
            th = t[:, hh * hd:(hh + 1) * hd]
            r = th * cos + pltpu.roll(th, hd // 2, 1) * sin
            if scale is not None:
                r = r * scale
            ref[:, hh * hd:(hh + 1) * hd] = r.astype(BF16)
    v_ref[...] = proj(2 * dc + 2 * dr, dr).astype(BF16)
    sg_ref[...] = _silu(proj(2 * dc + 3 * dr, dr)).astype(BF16)


def _proj(x, mod, g_mix, w_in_b, cos2, sin2, *, dc, dr, hd, per_row_mod, b0=0, nb=None):
    nb_all, length, d = x.shape
    nb = nb_all if nb is None else nb
    tl = min(length, 512)
    x2 = x.reshape(nb_all * length, d)
    nl = length // tl
    if per_row_mod:
        mod_spec = lambda j: pl.BlockSpec((tl, d), lambda b, l: (l, j))
        rope_spec = pl.BlockSpec((SUBLANES, LANES), lambda b, l: (0, 0))
    else:
        mod_spec = lambda j: pl.BlockSpec((None, None, 1, d), lambda b, l: (b + b0, j, 0, 0))
        rope_spec = pl.BlockSpec((tl, LANES), lambda b, l: (l, 0))
    row_spec = lambda w: pl.BlockSpec((tl, w), lambda b, l: (b * nl + l, 0))
    t = nb * length
    outs = pl.pallas_call(
        functools.partial(_proj_body, dc=dc, dr=dr, hd=hd, rope_rows=not per_row_mod),
        grid=(nb, nl),
        in_specs=[
            pl.BlockSpec((tl, d), lambda b, l: ((b + b0) * nl + l, 0)), mod_spec(0), mod_spec(1),
            pl.BlockSpec((1, d), lambda b, l: (0, 0)),
            pl.BlockSpec(w_in_b.shape, lambda b, l: (0, 0)),
            rope_spec, rope_spec,
        ],
        out_specs=[row_spec(dc), row_spec(dr), row_spec(dr), row_spec(dr), row_spec(dr)],
        out_shape=[
            jax.ShapeDtypeStruct((t, dc), F32),
            jax.ShapeDtypeStruct((t, dr), BF16),
            jax.ShapeDtypeStruct((t, dr), BF16),
            jax.ShapeDtypeStruct((t, dr), BF16),
            jax.ShapeDtypeStruct((t, dr), BF16),
        ],
        compiler_params=pltpu.CompilerParams(vmem_limit_bytes=VMEM_LIMIT),
        name="proj",
    )(x2, mod, mod, g_mix, w_in_b, cos2, sin2)
    return outs


def _layernorm_silu(c, g, b):
    mu = jnp.mean(c, axis=-1, keepdims=True)
    d = c - mu
    var = jnp.mean(d * d, axis=-1, keepdims=True)
    return _silu(d * lax.rsqrt(var + EPS) * g + b)


def _groupnorm(o, g, b):
    mu = jnp.mean(o, axis=-1, keepdims=True)
    d = o - mu
    var = jnp.mean(d * d, axis=-1, keepdims=True)
    return d * lax.rsqrt(var + EPS) * g + b


def _mix_body(glu_ref, q_ref, k_ref, v_ref, sg_ref, cw_ref, cb_ref, lng_ref, lnb_ref, rg_ref, rb_ref,
              mask_ref, qd_ref, kd_ref, cd_ref, cat_ref, st_ref, buf, cscr, *, tl, dc, hd, nh, chunk):
    nslab = dc // LANES

    @pl.when(pl.program_id(1) == 0)
    def _():
        buf[:, 0:CONV_PAD, :] = jnp.zeros((nslab, CONV_PAD, LANES), F32)
        st_ref[...] = jnp.zeros(st_ref.shape, F32)

    for j in range(nslab):
        buf[j, CONV_PAD:CONV_PAD + tl, :] = glu_ref[:, LANES * j:LANES * (j + 1)]
    first = CONV_PAD - CONV_BUF
    rows_per_iter = 8 * SUBLANES
    for j in range(nslab):
        cols = slice(LANES * j, LANES * (j + 1))
        wv = [jnp.broadcast_to(cw_ref[t:t + 1, cols], (SUBLANES, LANES)) for t in range(CONV_WIDTH)]
        bias = jnp.broadcast_to(cb_ref[0:1, cols], (SUBLANES, LANES))

        def body(r, carry, j=j, cols=cols, wv=wv, bias=bias):
            base = pl.multiple_of(r * rows_per_iter, rows_per_iter)
            for u in range(rows_per_iter // SUBLANES):
                acc = bias
                for t in range(CONV_WIDTH):
                    acc = acc + wv[t] * buf[j, pl.ds(base + (u * SUBLANES + first + t), SUBLANES), :]
                cscr[pl.ds(base + u * SUBLANES, SUBLANES), cols] = acc
            return carry

        lax.fori_loop(0, tl // rows_per_iter, body, 0)
    for j in range(nslab):
        buf[j, 0:CONV_PAD, :] = buf[j, tl:tl + CONV_PAD, :]
    cat_ref[:, 0:dc] = _layernorm_silu(cscr[...], lng_ref[...], lnb_ref[...]).astype(BF16)

    nt = (((1,), (1,)), ((), ()))
    tn = (((0,), (0,)), ((), ()))
    for c in range(tl // chunk):
        rows = slice(c * chunk, (c + 1) * chunk)
        for hh in range(nh):
            cols = slice(hh * hd, (hh + 1) * hd)
            qh = q_ref[rows, cols]
            kh = k_ref[rows, cols]
            vh = v_ref[rows, cols]
            s = st_ref[0, hh]
            scores = lax.dot_general(qh, kh, nt, preferred_element_type=F32) * mask_ref[hh]
            inner = jnp.dot(scores.astype(BF16), vh, preferred_element_type=F32)
            qd = (qh.astype(F32) * qd_ref[hh]).astype(BF16)
            cross = jnp.dot(qd, s.astype(BF16), preferred_element_type=F32)
            kd = (kh.astype(F32) * kd_ref[hh]).astype(BF16)
            st_ref[0, hh] = cd_ref[hh] * s + lax.dot_general(kd, vh, tn, preferred_element_type=F32)
            o = _groupnorm(inner + cross, rg_ref[0:1, cols], rb_ref[0:1, cols])
            cat_ref[rows, dc + hh * hd:dc + (hh + 1) * hd] = (o * sg_ref[rows, cols].astype(F32)).astype(BF16)


def _mix(glu, q, k, v, sg, conv_w, conv_b, ln_g, ln_b, rg, rb, tables, *, nb, length, dc, dr, hd):
    nh = dr // hd
    chunk = math.gcd(length, RET_CHUNK)
    tl = min(length, 256)
    nl = length // tl
    mask, qd, kd, cd = tables
    row_spec = lambda w: pl.BlockSpec((tl, w), lambda b, l: (b * nl + l, 0))
    full = lambda a: pl.BlockSpec(a.shape, lambda b, l: (0,) * a.ndim)
    cat, st = pl.pallas_call(
        functools.partial(_mix_body, tl=tl, dc=dc, hd=hd, nh=nh, chunk=chunk),
        grid=(nb, nl),
        in_specs=[row_spec(dc), row_spec(dr), row_spec(dr), row_spec(dr), row_spec(dr),
                  full(conv_w), full(conv_b), full(ln_g), full(ln_b), full(rg), full(rb),
                  full(mask), full(qd), full(kd), full(cd)],
        out_specs=[row_spec(dc + dr), pl.BlockSpec((1, nh, hd, hd), lambda b, l: (b, 0, 0, 0))],
        out_shape=[jax.ShapeDtypeStruct((nb * length, dc + dr), BF16),
                   jax.ShapeDtypeStruct((nb, nh, hd, hd), F32)],
        scratch_shapes=[pltpu.VMEM((dc // LANES, tl + CONV_PAD, LANES), F32),
                        pltpu.VMEM((tl, dc), F32)],
        compiler_params=pltpu.CompilerParams(dimension_semantics=("arbitrary", "arbitrary")),
        name="mix",
    )(glu, q, k, v, sg, conv_w, conv_b, ln_g, ln_b, rg, rb, mask, qd, kd, cd)
    return cat, st


def _mix1_body(glu_ref, q_ref, k_ref, v_ref, sg_ref, sc_ref, s0_ref, cw_ref, cb_ref, lng_ref, lnb_ref,
               rg_ref, rb_ref, gam_ref, cat_ref, st_ref, nc_ref, cscr, oscr, qf, kf, vf, *, tb, dc, hd, nh):
    w_hist = cw_ref[0:CONV_BUF, :]
    w_last = cw_ref[CONV_BUF:CONV_WIDTH, :]
    for bb in range(tb):
        hist = jnp.sum(sc_ref[bb] * w_hist, axis=0, keepdims=True)
        cscr[bb:bb + 1, :] = hist + glu_ref[bb:bb + 1, :] * w_last + cb_ref[...]
        nc_ref[bb, 0:CONV_BUF - 1, :] = sc_ref[bb, 1:CONV_BUF, :]
        nc_ref[bb, CONV_BUF - 1:CONV_BUF, :] = glu_ref[bb:bb + 1, :]
    cat_ref[:, 0:dc] = _layernorm_silu(cscr[...], lng_ref[...], lnb_ref[...]).astype(BF16)

    tn = (((0,), (0,)), ((), ()))
    rowid = lax.broadcasted_iota(I32, (tb, hd), 0)
    qf[...] = q_ref[...].astype(F32)
    kf[...] = k_ref[...].astype(F32)
    vf[...] = v_ref[...].astype(F32)
    for hh in range(nh):
        cols = slice(hh * hd, (hh + 1) * hd)
        qa = q_ref[:, cols]
        ka = k_ref[:, cols]
        gam = gam_ref[hh, 0:1, :]
        for bb in range(tb):
            onehot = (rowid == bb).astype(BF16)
            qcol = lax.dot_general(qa, onehot, tn, preferred_element_type=F32)
            kcol = lax.dot_general(ka, onehot, tn, preferred_element_type=F32)
            s0 = s0_ref[bb, hh]
            qrow = qf[bb:bb + 1, cols]
            krow = kf[bb:bb + 1, cols]
            vrow = vf[bb:bb + 1, cols]
            qk = jnp.sum(qrow * krow, axis=-1, keepdims=True)
            cross = gam * jnp.sum(qcol * s0, axis=0, keepdims=True)
            st_ref[bb, hh] = gam * s0 + kcol * vrow
            oscr[bb:bb + 1, cols] = qk * vrow + cross
    for hh in range(nh):
        cols = slice(hh * hd, (hh + 1) * hd)
        o = _groupnorm(oscr[:, cols], rg_ref[0:1, cols], rb_ref[0:1, cols])
        cat_ref[:, dc + hh * hd:dc + (hh + 1) * hd] = (o * sg_ref[:, cols].astype(F32)).astype(BF16)


def _mix1(glu, q, k, v, sg, state_conv, state_ret, conv_w, conv_b, ln_g, ln_b, rg, rb, gam, *, dc, dr, hd):
    nb = glu.shape[0]
    nh = dr // hd
    tb = 16
    row_spec = lambda w: pl.BlockSpec((tb, w), lambda i: (i, 0))
    full = lambda a: pl.BlockSpec(a.shape, lambda i: (0,) * a.ndim)
    st_spec = pl.BlockSpec((None, tb, nh, hd, hd), lambda i: (0, i, 0, 0, 0))
    conv_spec = pl.BlockSpec((None, tb, CONV_BUF, dc), lambda i: (0, i, 0, 0))
    cat, st, new_conv = pl.pallas_call(
        functools.partial(_mix1_body, tb=tb, dc=dc, hd=hd, nh=nh),
        grid=(nb // tb,),
        in_specs=[row_spec(dc), row_spec(dr), row_spec(dr), row_spec(dr), row_spec(dr), conv_spec, st_spec,
                  full(conv_w), full(conv_b), full(ln_g), full(ln_b), full(rg), full(rb), full(gam)],
        out_specs=[row_spec(dc + dr), st_spec, conv_spec],
        out_shape=[jax.ShapeDtypeStruct((nb, dc + dr), BF16),
                   jax.ShapeDtypeStruct((1, nb, nh, hd, hd), F32),
                   jax.ShapeDtypeStruct((1, nb, CONV_BUF, dc), F32)],
        scratch_shapes=[pltpu.VMEM((tb, dc), F32)] + [pltpu.VMEM((tb, dr), F32)] * 4,
        compiler_params=pltpu.CompilerParams(vmem_limit_bytes=VMEM_LIMIT),
        name="mix1",
    )(glu, q, k, v, sg, state_conv, state_ret, conv_w, conv_b, ln_g, ln_b, rg, rb, gam)
    return cat, st, new_conv


def _post_body(cat_ref, x_ref, gtm_ref, scf_ref, shf_ref, gtf_ref, g_ref, wo_ref, wrh_ref, wrl_ref,
               ws1_ref, ws3_ref, ws2_ref, x2_ref, hp_ref, lg_ref):
    d = x_ref.shape[1]
    y = jnp.dot(cat_ref[...], wo_ref[...], preferred_element_type=F32)
    x1 = x_ref[...] + gtm_ref[...] * y
    h = _modulated_rmsnorm(x1, g_ref[...], scf_ref[...], shf_ref[...])
    hb = h.astype(BF16)
    _store_rows(hp_ref, _pack_halves(h[:, 0:d // 2], h[:, d // 2:d]))
    hl = (h - hb.astype(F32)).astype(BF16)
    nt = (((1,), (1,)), ((), ()))
    lg_ref[...] = (lax.dot_general(wrh_ref[...], hb, nt, preferred_element_type=F32)
                   + lax.dot_general(wrh_ref[...], hl, nt, preferred_element_type=F32)
                   + lax.dot_general(wrl_ref[...], hb, nt, preferred_element_type=F32))
    s1 = jnp.dot(hb, ws1_ref[...], preferred_element_type=F32)
    s3 = jnp.dot(hb, ws3_ref[...], preferred_element_type=F32)
    shared = jnp.dot((_silu(s1) * s3).astype(BF16), ws2_ref[...], preferred_element_type=F32)
    x2_ref[...] = x1 + gtf_ref[...] * shared


def _post(cat, x, mod, g_ffn, wo_b, wrh, wrl, ws1_b, ws3_b, ws2_b, *, per_row_mod, b0=0, nb=None):
    nb_all, length, d = x.shape
    nb = nb_all if nb is None else nb
    tl = min(length, 1024)
    nl = length // tl
    t = nb * length
    ne = wrh.shape[0]
    x2d = x.reshape(nb_all * length, d)
    if per_row_mod:
        mod_spec = lambda j: pl.BlockSpec((tl, d), lambda b, l: (l, j))
    else:
        mod_spec = lambda j: pl.BlockSpec((None, None, 1, d), lambda b, l: (b + b0, j, 0, 0))
    row_spec = lambda w: pl.BlockSpec((tl, w), lambda b, l: (b * nl + l, 0))
    full = lambda a: pl.BlockSpec(a.shape, lambda b, l: (0,) * a.ndim)
    return pl.pallas_call(
        _post_body,
        grid=(nb, nl),
        in_specs=[row_spec(d), pl.BlockSpec((tl, d), lambda b, l: ((b + b0) * nl + l, 0)),
                  mod_spec(2), mod_spec(4), mod_spec(3), mod_spec(5),
                  full(g_ffn), full(wo_b), full(wrh), full(wrl), full(ws1_b), full(ws3_b), full(ws2_b)],
        out_specs=[row_spec(d),
                   pl.BlockSpec((tl * ROW_WORDS, LANES), lambda b, l: (b * nl + l, 0)),
                   pl.BlockSpec((ne, tl), lambda b, l: (0, b * nl + l))],
        out_shape=[jax.ShapeDtypeStruct((t, d), F32),
                   jax.ShapeDtypeStruct((t * ROW_WORDS, LANES), U32),
                   jax.ShapeDtypeStruct((ne, t), F32)],
        compiler_params=pltpu.CompilerParams(vmem_limit_bytes=VMEM_LIMIT),
        name="post",
    )(cat, x2d, mod, mod, mod, mod, g_ffn, wo_b, wrh, wrl, ws1_b, ws3_b, ws2_b)


def _first_max(x, idx, sentinel):
    m = jnp.max(x, axis=0, keepdims=True)
    f = jnp.min(jnp.where(x == m, idx, sentinel), axis=0, keepdims=True)
    return m, f


def _route_body(lg_ref, bias_ref, e_ref, w_ref, r_ref, cnt_ref, cnt_scr, *, tr, ne, ng, topk, topg):
    @pl.when(pl.program_id(0) == 0)
    def _():
        cnt_scr[...] = jnp.zeros(cnt_scr.shape, F32)

    per = ne // ng
    neg = -jnp.inf
    scores = _sigmoid(lg_ref[...])
    sel = scores + bias_ref[...]
    sub = lax.broadcasted_iota(I32, (per, tr), 0)
    gs = []
    for g in range(ng):
        s_g = sel[g * per:(g + 1) * per, :]
        m1, f1 = _first_max(s_g, sub, per)
        m2 = jnp.max(jnp.where(sub == f1, neg, s_g), axis=0, keepdims=True)
        gs.append(m1 + m2)
    gsc = jnp.concatenate(gs, axis=0)
    gi = lax.broadcasted_iota(I32, (ng, tr), 0)
    keep = jnp.zeros((ng, tr), F32)
    for _ in range(topg):
        _, f = _first_max(gsc, gi, ng)
        pick = gi == f
        keep = jnp.where(pick, 1.0, keep)
        gsc = jnp.where(pick, neg, gsc)
    work = jnp.concatenate(
        [jnp.where(keep[g:g + 1, :] > 0.5, sel[g * per:(g + 1) * per, :], neg) for g in range(ng)], axis=0)
    ei = lax.broadcasted_iota(I32, (ne, tr), 0)
    picks, es, ws = [], [], []
    for _ in range(topk):
        _, f = _first_max(work, ei, ne)
        pick = ei == f
        picks.append(pick)
        es.append(f)
        ws.append(jnp.sum(jnp.where(pick, scores, 0.0), axis=0, keepdims=True))
        work = jnp.where(pick, neg, work)
    wsum = ws[0]
    for w in ws[1:]:
        wsum = wsum + w
    scale = ROUTED_SCALE / wsum
    chosen = picks[0]
    for p in picks[1:]:
        chosen = jnp.logical_or(chosen, p)
    chosen_f = chosen.astype(F32)
    t_row = lax.broadcasted_iota(I32, (tr, tr), 0)
    t_col = lax.broadcasted_iota(I32, (tr, tr), 1)
    before = (t_row < t_col).astype(BF16)
    prior = cnt_scr[:, 0:1] + jnp.dot(chosen_f.astype(BF16), before, preferred_element_type=F32)
    rs = [jnp.sum(jnp.where(p, prior, 0.0), axis=0, keepdims=True).astype(I32) for p in picks]
    pad_i = jnp.zeros((SUBLANES - topk, tr), I32)
    pad_f = jnp.zeros((SUBLANES - topk, tr), F32)
    e_ref[...] = jnp.concatenate(es + [pad_i], axis=0)
    w_ref[...] = jnp.concatenate([w * scale for w in ws] + [pad_f], axis=0)
    r_ref[...] = jnp.concatenate(rs + [pad_i], axis=0)
    total = cnt_scr[:, 0:1] + jnp.sum(chosen_f, axis=1, keepdims=True)
    cnt_scr[...] = jnp.broadcast_to(total, cnt_scr.shape)
    cnt_ref[...] = jnp.broadcast_to(total, cnt_ref.shape)


def _route_tile(t):
    return max(m for m in range(LANES, ROUTE_TILE + 1, LANES) if t % m == 0)


def _route(logits_t, bias_col):
    ne, t = logits_t.shape
    tr = _route_tile(t)
    tok = lambda dt: jax.ShapeDtypeStruct((SUBLANES, t), dt)
    tok_spec = pl.BlockSpec((SUBLANES, tr), lambda i: (0, i))
    return pl.pallas_call(
        functools.partial(_route_body, tr=tr, ne=ne, ng=N_GROUPS, topk=TOP_K, topg=TOPK_GROUPS),
        grid=(t // tr,),
        in_specs=[pl.BlockSpec((ne, tr), lambda i: (0, i)), pl.BlockSpec((ne, 1), lambda i: (0, 0))],
        out_specs=[tok_spec, tok_spec, tok_spec, pl.BlockSpec((ne, LANES), lambda i: (0, 0))],
        out_shape=[tok(I32), tok(F32), tok(I32), jax.ShapeDtypeStruct((ne, LANES), F32)],
        scratch_shapes=[pltpu.VMEM((ne, LANES), F32)],
        compiler_params=pltpu.CompilerParams(dimension_semantics=("arbitrary",)),
        name="route",
    )(logits_t, bias_col)


def _dest_body(cnt_ref, e_ref, r_ref, d_ref, ps, *, ne, tr):
    shift = EXPERT_ROWS.bit_length() - 1

    @pl.when(pl.program_id(0) == 0)
    def _():
        def step(j, start):
            ps[j] = start
            return start + lax.shift_left(lax.shift_right_logical(cnt_ref[j] + (EXPERT_ROWS - 1), shift), shift)

        lax.fori_loop(0, ne, step, jnp.int32(0))

    e = e_ref[...]
    base = jnp.zeros(e.shape, I32)
    for j in range(ne):
        base = jnp.where(e == j, ps[j], base)
    dest = base + r_ref[...]
    for m in range(tr // TOKEN_TILE):
        d_ref[SUBLANES * m:SUBLANES * (m + 1), :] = dest[:, TOKEN_TILE * m:TOKEN_TILE * (m + 1)]


def _dest_rows(counts, eidx, rank):
    rows, t = eidx.shape
    ne = counts.shape[0]
    tr = _route_tile(t)
    spec = pl.BlockSpec((rows, tr), lambda i, cnt: (0, i))
    return pl.pallas_call(
        functools.partial(_dest_body, ne=ne, tr=tr),
        grid_spec=pltpu.PrefetchScalarGridSpec(
            num_scalar_prefetch=1, grid=(t // tr,), in_specs=[spec, spec],
            out_specs=pl.BlockSpec((tr // TOKEN_TILE * rows, TOKEN_TILE), lambda i, cnt: (i, 0)),
            scratch_shapes=[pltpu.SMEM((ne,), I32)]),
        out_shape=jax.ShapeDtypeStruct((t // TOKEN_TILE * rows, TOKEN_TILE), I32),
        compiler_params=pltpu.CompilerParams(dimension_semantics=("arbitrary",)),
        name="dest",
    )(counts, eidx, rank)


def _sc_mesh():
    return plsc.VectorSubcoreMesh(core_axis_name="c", subcore_axis_name="s")


def _sc_worker_id():
    return lax.axis_index("s") * SC_CORES + lax.axis_index("c")


def _index_block(dest_ref, chunk, width):
    per_tile = TOKEN_TILE // width
    return dest_ref.at[chunk // per_tile, :, pl.ds((chunk % per_tile) * width, width)]


def _dispatch(hps, dest3, n_rows):
    w = SCATTER_ROWS
    width = hps[0].shape[1]
    bounds = [0]
    for h in hps:
        bounds.append(bounds[-1] + h.shape[0] // w)
    nch = bounds[-1]
    nsrc = len(hps)

    @functools.partial(
        pl.kernel, mesh=_sc_mesh(),
        out_type=jax.ShapeDtypeStruct((n_rows, width), U32),
        scratch_types=[pltpu.VMEM((SUBLANES, w), I32), pltpu.VMEM((w, width), U32), pltpu.SemaphoreType.DMA],
        compiler_params=pltpu.CompilerParams(use_tc_tiling_on_sc=False),
        name="dispatch",
    )
    def run(*refs):
        src_refs, dest_ref, xs_ref = refs[:nsrc], refs[nsrc], refs[nsrc + 1]
        idx_v, rows_v, sem = refs[nsrc + 2:]
        wid = _sc_worker_id()

        @pl.loop(0, pl.cdiv(nch, SC_WORKERS))
        def _(r):
            c = r * SC_WORKERS + wid

            @pl.when(c < nch)
            def _():
                pltpu.sync_copy(_index_block(dest_ref, c, w), idx_v)
                for i, src in enumerate(src_refs):
                    @pl.when(jnp.logical_and(c >= bounds[i], c < bounds[i + 1]))
                    def _(src=src, lo=bounds[i]):
                        pltpu.sync_copy(src.at[pl.ds((c - lo) * w, w)], rows_v)

                copies = [pltpu.async_copy(rows_v, xs_ref.at[idx_v.at[k]], sem) for k in range(TOP_K)]
                for cp in copies:
                    cp.wait()

    return run(*hps, dest3)


def _undispatch(ys, dest3, n_tokens, after=None):
    w = GATHER_ROWS
    width = ys.shape[1]
    nch = n_tokens // w

    @functools.partial(
        pl.kernel, mesh=_sc_mesh(),
        out_type=jax.ShapeDtypeStruct((TOP_K, n_tokens, width), U32),
        scratch_types=[pltpu.VMEM((SUBLANES, w), I32), pltpu.VMEM((TOP_K, w, width), U32),
                       pltpu.SemaphoreType.DMA],
        compiler_params=pltpu.CompilerParams(use_tc_tiling_on_sc=False),
        name="undispatch",
    )
    def run(*refs):
        ys_ref, dest_ref = refs[:2]
        z_ref, idx_v, bufs, sem = refs[-4:]
        wid = _sc_worker_id()

        @pl.loop(0, pl.cdiv(nch, SC_WORKERS))
        def _(r):
            c = r * SC_WORKERS + wid

            @pl.when(c < nch)
            def _():
                pltpu.sync_copy(_index_block(dest_ref, c, w), idx_v)
                gathers = [pltpu.async_copy(ys_ref.at[idx_v.at[k]], bufs.at[k], sem) for k in range(TOP_K)]
                for cp in gathers:
                    cp.wait()
                stores = [pltpu.async_copy(bufs.at[k], z_ref.at[k, pl.ds(c * w, w)], sem) for k in range(TOP_K)]
                for cp in stores:
                    cp.wait()

    return run(ys, dest3) if after is None else run(ys, dest3, after)


PART_SHIFT = 24


def _expert_body(*refs, tm, ne, nparts):
    cnt_refs = refs[:nparts]
    xs_refs = refs[nparts:2 * nparts]
    w1_ref, w3_ref, w2_ref = refs[2 * nparts:2 * nparts + 3]
    ys_refs = refs[2 * nparts + 3:3 * nparts + 3]
    (w1f, w3f, w2f, w1s, w3s, w2s, xbuf, ybuf, xlo, xhi, sched, sem_x, sem_y, sem_w) = refs[3 * nparts + 3:]
    blk_words = tm * ROW_WORDS
    half = ROW_WORDS * LANES
    shift = tm.bit_length() - 1

    def n_blocks_of(p, e):
        return lax.shift_right_logical(cnt_refs[p][e] + (tm - 1), shift)

    def n_all(e):
        n = n_blocks_of(0, e)
        for p in range(1, nparts):
            n = n + n_blocks_of(p, e)
        return n

    def next_nonempty(e):
        return lax.while_loop(
            lambda c: jnp.logical_and(c < ne, n_all(jnp.minimum(c, ne - 1)) == 0), lambda c: c + 1, e)

    def plan(e, carry):
        i, starts = carry[0], list(carry[1:])
        for p in range(nparts):
            n = n_blocks_of(p, e)

            def put(j, c, p=p, i=i, start=starts[p]):
                sched[i + j] = (start + j) + (p << PART_SHIFT)
                return c

            lax.fori_loop(0, n, put, 0)
            i = i + n
            starts[p] = starts[p] + n
        return (i, *starts)

    nu = lax.fori_loop(0, ne, plan, (jnp.int32(0),) * (nparts + 1))[0]

    def rows_of(code):
        blk = code & ((1 << PART_SHIFT) - 1)
        return pl.ds(pl.multiple_of(blk * blk_words, blk_words), blk_words)

    def x_start(i, slot):
        code = sched[i]
        for p in range(nparts):
            @pl.when(lax.shift_right_logical(code, PART_SHIFT) == p)
            def _(p=p):
                pltpu.make_async_copy(xs_refs[p].at[rows_of(code), :], xbuf.at[slot], sem_x.at[slot]).start()

    def x_wait(slot):
        pltpu.make_async_copy(xs_refs[0].at[pl.ds(0, blk_words), :], xbuf.at[slot], sem_x.at[slot]).wait()

    def y_start(i, slot):
        code = sched[i]
        for p in range(nparts):
            @pl.when(lax.shift_right_logical(code, PART_SHIFT) == p)
            def _(p=p):
                pltpu.make_async_copy(ybuf.at[slot], ys_refs[p].at[rows_of(code), :], sem_y.at[slot]).start()

    def y_wait(slot):
        pltpu.make_async_copy(ybuf.at[slot], ys_refs[0].at[pl.ds(0, blk_words), :], sem_y.at[slot]).wait()

    def w_copies(e, ws):
        return [pltpu.make_async_copy(src.at[e], dst.at[ws], sem_w.at[ws])
                for src, dst in ((w1_ref, w1f), (w3_ref, w3f), (w2_ref, w2f))]

    for q in range(X_AHEAD):
        @pl.when(q < nu)
        def _(q=q):
            x_start(q, q)

    e_first = next_nonempty(jnp.int32(0))

    @pl.when(e_first < ne)
    def _():
        for cp in w_copies(e_first, 0):
            cp.start()

    def blocks(i, nblk):
        for b in range(nblk):
            x_wait((i + b) % X_SLOTS)
        for b in range(nblk):
            nxt = i + b + X_AHEAD

            @pl.when(nxt < nu)
            def _(nxt=nxt):
                x_start(nxt, nxt % X_SLOTS)

        for b in range(nblk):
            @pl.when(i + b >= Y_SLOTS)
            def _(b=b):
                y_wait((i + b) % Y_SLOTS)

        for b in range(nblk):
            xin = xbuf.at[(i + b) % X_SLOTS]
            for w in range(ROW_WORDS):
                lo, hi = _unpack_halves(_load_row_word(xin, w, tm))
                xlo[b, :, LANES * w:LANES * (w + 1)] = lo.astype(BF16)
                xhi[b, :, LANES * w:LANES * (w + 1)] = hi.astype(BF16)

        def up(b, wsc):
            return (jnp.dot(xlo[b], wsc[0:half, :], preferred_element_type=F32)
                    + jnp.dot(xhi[b], wsc[half:2 * half, :], preferred_element_type=F32))

        for b in range(nblk):
            hid = (_silu(up(b, w1s)) * up(b, w3s)).astype(BF16)
            y = jnp.dot(hid, w2s[...], preferred_element_type=F32)
            _store_rows(ybuf.at[(i + b) % Y_SLOTS], _pack_halves(y[:, 0:half], y[:, half:2 * half]))
        for b in range(nblk):
            y_start(i + b, (i + b) % Y_SLOTS)

    def per_expert(e, carry):
        i0, ws = carry
        n = n_all(e)

        @pl.when(n > 0)
        def _():
            for cp in w_copies(e, ws):
                cp.wait()
            w1s[...] = w1f[ws].astype(BF16)
            w3s[...] = w3f[ws].astype(BF16)
            w2s[...] = w2f[ws].astype(BF16)
            e_next = next_nonempty(e + 1)

            @pl.when(e_next < ne)
            def _():
                for cp in w_copies(e_next, 1 - ws):
                    cp.start()

            def pair(j, c):
                blocks(i0 + 2 * j, 2)
                return c

            lax.fori_loop(0, lax.shift_right_logical(n, 1), pair, 0)

            @pl.when(n % 2 == 1)
            def _():
                blocks(i0 + n - 1, 1)

        return i0 + n, jnp.where(n > 0, 1 - ws, ws)

    lax.fori_loop(0, ne, per_expert, (jnp.int32(0), jnp.int32(0)))

    for q in range(Y_SLOTS):
        @pl.when(nu > q)
        def _(q=q):
            y_wait((nu - 1 - q) % Y_SLOTS)


def _experts(counts, xss, w1e, w3e, w2e):
    tm = EXPERT_ROWS
    ne, d, de = w1e.shape
    half = ROW_WORDS * LANES
    nparts = len(xss)
    cap = sum(x.shape[0] // (tm * ROW_WORDS) for x in xss)
    assert cap < (1 << PART_SHIFT)
    anyspec = pl.BlockSpec(memory_space=pl.ANY)
    blk_buf = lambda n: pltpu.VMEM((n, tm * ROW_WORDS, LANES), U32)
    return pl.pallas_call(
        functools.partial(_expert_body, tm=tm, ne=ne, nparts=nparts),
        grid_spec=pltpu.PrefetchScalarGridSpec(
            num_scalar_prefetch=nparts,
            grid=(1,),
            in_specs=[anyspec] * (nparts + 3),
            out_specs=[anyspec] * nparts,
            scratch_shapes=[pltpu.VMEM((2, d, de), F32), pltpu.VMEM((2, d, de), F32), pltpu.VMEM((2, de, d), F32),
                            pltpu.VMEM((d, de), BF16), pltpu.VMEM((d, de), BF16), pltpu.VMEM((de, d), BF16),
                            blk_buf(X_SLOTS), blk_buf(Y_SLOTS),
                            pltpu.VMEM((2, tm, half), BF16), pltpu.VMEM((2, tm, half), BF16),
                            pltpu.SMEM((cap,), I32),
                            pltpu.SemaphoreType.DMA((X_SLOTS,)), pltpu.SemaphoreType.DMA((Y_SLOTS,)),
                            pltpu.SemaphoreType.DMA((2,))]),
        out_shape=[jax.ShapeDtypeStruct(x.shape, U32) for x in xss],
        compiler_params=pltpu.CompilerParams(dimension_semantics=("arbitrary",), vmem_limit_bytes=VMEM_LIMIT),
        name="experts",
    )(*counts, *xss, w1e, w3e, w2e)


def _combine_body(z_ref, x2_ref, gtf_ref, wt_ref, gfin_ref, y_ref, xo, *, td, topk):
    half = ROW_WORDS * LANES
    wt = wt_ref[...].T
    ws = [wt[:, k:k + 1] for k in range(topk)]
    sq = jnp.zeros((td, 1), F32)
    for j in range(ROW_WORDS):
        acc_lo = jnp.zeros((td, LANES), F32)
        acc_hi = jnp.zeros((td, LANES), F32)
        for k in range(topk):
            lo, hi = _unpack_halves(_load_row_word(z_ref.at[k], j, td))
            acc_lo = acc_lo + ws[k] * lo
            acc_hi = acc_hi + ws[k] * hi
        for base, acc in ((0, acc_lo), (half, acc_hi)):
            cols = slice(base + LANES * j, base + LANES * (j + 1))
            x = x2_ref[:, cols] + gtf_ref[:, cols] * acc
            xo[:, cols] = x
            sq = sq + jnp.sum(x * x, axis=-1, keepdims=True)
    rs = lax.rsqrt(sq / (2 * half) + EPS)
    y_ref[...] = xo[...] * rs * gfin_ref[...]


def _combine_body_into(z_ref, x2_ref, gtf_ref, wt_ref, gfin_ref, prev_ref, y_ref, xo, *, td, topk):
    del prev_ref
    _combine_body(z_ref, x2_ref, gtf_ref, wt_ref, gfin_ref, y_ref, xo, td=td, topk=topk)


def _combine(z, token0, wts, x2, mod, g_final, *, rows_per_mod, per_row_mod, b0=0, out_rows=None, into=None):
    t, d = x2.shape
    td = min(t, COMBINE_TILE)
    tile0 = token0 // td
    out_rows = t if out_rows is None else out_rows
    if per_row_mod:
        gtf_spec = pl.BlockSpec((td, d), lambda i: (i, 5))
        out0 = 0
    else:
        tiles_per_mod = rows_per_mod // td
        gtf_spec = pl.BlockSpec((None, None, 1, d), lambda i: (i // tiles_per_mod + b0, 5, 0, 0))
        out0 = b0 * tiles_per_mod
    in_specs = [pl.BlockSpec((TOP_K, td * ROW_WORDS, LANES), lambda i: (0, i + tile0, 0)),
                pl.BlockSpec((td, d), lambda i: (i, 0)),
                gtf_spec,
                pl.BlockSpec((SUBLANES, td), lambda i: (0, i + tile0)),
                pl.BlockSpec((1, d), lambda i: (0, 0))]
    args = [z, x2, mod, wts, g_final]
    body, aliases = _combine_body, {}
    if into is not None:
        in_specs.append(pl.BlockSpec(memory_space=pl.ANY))
        args.append(into)
        body, aliases = _combine_body_into, {len(args) - 1: 0}
    return pl.pallas_call(
        functools.partial(body, td=td, topk=TOP_K),
        grid=(t // td,),
        in_specs=in_specs,
        out_specs=pl.BlockSpec((td, d), lambda i: (i + out0, 0)),
        scratch_shapes=[pltpu.VMEM((td, d), F32)],
        out_shape=jax.ShapeDtypeStruct((out_rows, d), F32),
        input_output_aliases=aliases,
        compiler_params=pltpu.CompilerParams(vmem_limit_bytes=VMEM_LIMIT),
        name="combine",
    )(*args)


def _retention_tables(length, nh, hd):
    c = math.gcd(length, RET_CHUNK)
    log_g = jnp.log(1.0 - 2.0 ** (-5.0 - jnp.arange(nh, dtype=F32)))
    idx = jnp.arange(c, dtype=F32)
    rel = idx[:, None] - idx[None, :]
    mask = jnp.where(rel >= 0, jnp.exp(log_g[:, None, None] * jnp.maximum(rel, 0.0)), 0.0)
    q_decay = jnp.exp(log_g[None, :] * (idx[:, None] + 1.0))
    k_decay = jnp.exp(log_g[None, :] * (c - 1.0 - idx[:, None]))
    chunk_decay = jnp.exp(log_g * c)
    qd = jnp.broadcast_to(q_decay.T[:, :, None], (nh, c, hd))
    kd = jnp.broadcast_to(k_decay.T[:, :, None], (nh, c, hd))
    cd = jnp.broadcast_to(chunk_decay[:, None, None], (nh, hd, hd))
    return mask, qd, kd, cd


def kernel(x_prompt, x_sample, c_prompt, c_sample, state_conv, state_ret, w_ada, b_ada, g_mix, g_ffn, w_in,
           conv_w, conv_b, conv_norm_g, conv_norm_b, ret_norm_g, ret_norm_b, w_out, w_router, router_bias,
           w1, w3, w2, ws1, ws3, ws2, g_final):
    depth = w_ada.shape[0]
    assert depth == 1, "single-layer trunk"
    bp, lp, d = x_prompt.shape
    bs, ls, _ = x_sample.shape
    assert ls == 1
    dc = conv_w.shape[2]
    dr = ret_norm_g.shape[1]
    nh = RET_HEADS
    hd = dr // nh
    assert hd == LANES and lp % 256 == 0 and bs % TOKEN_TILE == 0 and d // 2 == ROW_WORDS * LANES
    ne = w_router.shape[2]
    row = lambda a: a.reshape(1, -1)

    mod = _ada(jnp.concatenate([c_prompt, c_sample], axis=0), w_ada[0], row(b_ada[0]))
    mod_p = mod[:bp].reshape(bp, 6, 1, d)
    mod_s = mod[bp:]

    half = hd // 2
    inv = ROPE_BASE ** (-jnp.arange(half, dtype=F32) / half)
    inv2 = jnp.concatenate([inv, inv]).reshape(1, hd)
    cos_p, sin_p = _rope_tables(inv2, lp, 0)
    cos_s, sin_s = _rope_tables(inv2, SUBLANES, PAST_LEN)

    w_in_b = w_in[0].astype(BF16)
    wo_b = w_out[0].astype(BF16)
    wr_t = w_router[0].T
    wrh = wr_t.astype(BF16)
    wrl = (wr_t - wrh.astype(F32)).astype(BF16)
    ws1_b, ws3_b, ws2_b = ws1[0].astype(BF16), ws3[0].astype(BF16), ws2[0].astype(BF16)
    dims = dict(dc=dc, dr=dr, hd=hd)

    tables = _retention_tables(lp, nh, hd)
    log_g = jnp.log(1.0 - 2.0 ** (-5.0 - jnp.arange(nh, dtype=F32)))
    gam = jnp.broadcast_to(jnp.exp(log_g)[:, None, None], (nh, SUBLANES, hd))
    norm_rows = (row(conv_b[0]), row(conv_norm_g[0]), row(conv_norm_b[0]), row(ret_norm_g[0]), row(ret_norm_b[0]))
    post_w = (row(g_ffn[0]), wo_b, wrh, wrl, ws1_b, ws3_b, ws2_b)
    bias_col = router_bias[0].reshape(ne, 1)
    hw = d // 2
    tm = EXPERT_ROWS

    def pre_prompt(b0, nb):
        glu, q, k, v, sg = _proj(x_prompt, mod_p, row(g_mix[0]), w_in_b, cos_p, sin_p,
                                 per_row_mod=False, b0=b0, nb=nb, **dims)
        cat, ret = _mix(glu, q, k, v, sg, conv_w[0], *norm_rows, tables, nb=nb, length=lp, **dims)
        x2, hp, lg = _post(cat, x_prompt, mod_p, *post_w, per_row_mod=False, b0=b0, nb=nb)
        return glu, ret, x2, hp, lg

    def pre_sample():
        xs3 = x_sample.reshape(1, bs, d)
        glu, q, k, v, sg = _proj(xs3, mod_s, row(g_mix[0]), w_in_b, cos_s, sin_s, per_row_mod=True, **dims)
        cat, ret, new_conv = _mix1(glu, q, k, v, sg, state_conv, state_ret, conv_w[0], *norm_rows, gam,
                                   **dims)
        x2, hp, lg = _post(cat, xs3, mod_s, *post_w, per_row_mod=True)
        return new_conv, ret, x2, hp, lg

    def route_and_dispatch(hps, lgs):
        lg = lgs[0] if len(lgs) == 1 else jnp.concatenate(lgs, axis=1)
        tokens = lg.shape[1]
        eidx, wts, rank, cnt = _route(lg, bias_col)
        counts = cnt[:, 0].astype(I32)
        n_rows = -(-(tokens * TOP_K + ne * (tm - 1)) // tm) * tm
        dest3 = _dest_rows(counts, eidx, rank).reshape(tokens // TOKEN_TILE, SUBLANES, TOKEN_TILE)
        xs = _dispatch([h.reshape(-1, hw) for h in hps], dest3, n_rows)
        return counts, dest3, wts, xs.reshape(n_rows * ROW_WORDS, LANES)

    def undispatch(ys, dest3, after=None):
        tokens = dest3.shape[0] * TOKEN_TILE
        z = _undispatch(ys.reshape(-1, hw), dest3, tokens, after)
        return z.reshape(TOP_K, tokens * ROW_WORDS, LANES)

    nb0 = bp // 2
    nb1 = bp - nb0
    glu_0, ret_0, x2_0, hp_0, lg_0 = pre_prompt(0, nb0)
    counts_0, dest_0, wts_0, xs_0 = route_and_dispatch([hp_0], [lg_0])
    glu_1, ret_1, x2_1, hp_1, lg_1 = pre_prompt(nb0, nb1)
    new_conv_s, ret_s, x2_s, hp_s, lg_s = pre_sample()
    counts_1, dest_1, wts_1, xs_1 = route_and_dispatch([hp_1, hp_s], [lg_1, lg_s])
    ys_0, ys_1 = _experts((counts_0, counts_1), (xs_0, xs_1), w1[0], w3[0], w2[0])
    z_0 = undispatch(ys_0, dest_0)
    z_1 = undispatch(ys_1, dest_1, after=z_0[0, :SUBLANES])
    y_p = _combine(z_0, 0, wts_0, x2_0, mod_p, row(g_final), rows_per_mod=lp, per_row_mod=False,
                   out_rows=bp * lp)
    y_p = _combine(z_1, 0, wts_1, x2_1, mod_p, row(g_final), rows_per_mod=lp, per_row_mod=False,
                   b0=nb0, out_rows=bp * lp, into=y_p)
    y_s = _combine(z_1, nb1 * lp, wts_1, x2_s, mod_s, row(g_final), rows_per_mod=bs, per_row_mod=True)
    ret_p = jnp.concatenate([ret_0, ret_1], axis=0)

    tail = lambda g, n: g.reshape(n, lp, dc)[:, lp - CONV_BUF:, :]
    new_conv_p = jnp.concatenate([tail(glu_0, nb0), tail(glu_1, nb1)], axis=0)
    return (y_p.reshape(bp, lp, d), y_s.reshape(bs, ls, d), new_conv_p[None], ret_p[None],
            new_conv_s, ret_s)
```

```python
import functools
import math

import jax
import jax.numpy as jnp
from jax import lax
from jax.experimental import pallas as pl
from jax.experimental.pallas import tpu as pltpu
from jax.experimental.pallas import tpu_sc as plsc

F32 = jnp.float32
BF16 = jnp.bfloat16
U32 = jnp.uint32
I32 = jnp.int32

EPS = 1e-6
PAST_LEN = 16384
RET_HEADS = 4
RET_CHUNK = 128
CONV_WIDTH = 31
CONV_BUF = CONV_WIDTH - 1
ROPE_BASE = 10000.0
N_EXPERTS = 64
TOP_K = 6
N_GROUPS = 8
TOPK_GROUPS = 4
ROUTED_SCALE = 2.5

LANES = 128
SUBLANES = 8
CONV_PAD = 32
EXPERT_ROWS = 256
ROUTE_TILE = 640
TOKEN_TILE = 128
COMBINE_TILE = 512
VMEM_LIMIT = 56 * 1024 * 1024
SC_CORES = 2
SC_SUBCORES = 16
SC_WORKERS = SC_CORES * SC_SUBCORES
SCATTER_ROWS = 64
GATHER_ROWS = 16
ROW_WORDS = 4
X_SLOTS = 6
X_AHEAD = 4
Y_SLOTS = 4

HI_MASK = 0xFFFF0000


def _sigmoid(x):
    return jax.nn.sigmoid(x)


def _silu(x):
    return x * jax.nn.sigmoid(x)


def _pack_halves(lo, hi):
    lo_u = lax.bitcast_convert_type(lo.astype(BF16).astype(F32), U32) >> 16
    hi_u = lax.bitcast_convert_type(hi.astype(BF16).astype(F32), U32) & jnp.uint32(HI_MASK)
    return hi_u | lo_u


def _unpack_halves(p):
    lo = lax.bitcast_convert_type(p << 16, F32)
    hi = lax.bitcast_convert_type(p & jnp.uint32(HI_MASK), F32)
    return lo, hi


def _store_rows(ref, x):
    rows = x.shape[0]
    for j in range(ROW_WORDS):
        ref[pl.ds(j, rows, stride=ROW_WORDS), :] = x[:, LANES * j:LANES * (j + 1)]


def _load_row_word(ref, j, rows):
    return ref[pl.ds(j, rows, stride=ROW_WORDS), :]


def _ada_body(c_ref, w_ref, b_ref, o_ref):
    s = _silu(c_ref[...]).astype(BF16)
    o_ref[...] = jnp.dot(s, w_ref[...].astype(BF16), preferred_element_type=F32) + b_ref[...]


def _ada(c_all, w_ada, b_ada):
    rows, d = c_all.shape
    n = w_ada.shape[1]
    tn = 2048
    return pl.pallas_call(
        _ada_body,
        grid=(n // tn,),
        in_specs=[
            pl.BlockSpec((rows, d), lambda j: (0, 0)),
            pl.BlockSpec((d, tn), lambda j: (0, j)),
            pl.BlockSpec((1, tn), lambda j: (0, j)),
        ],
        out_specs=pl.BlockSpec((rows, tn), lambda j: (0, j)),
        out_shape=jax.ShapeDtypeStruct((rows, n), F32),
        compiler_params=pltpu.CompilerParams(vmem_limit_bytes=VMEM_LIMIT),
        name="ada",
    )(c_all, w_ada, b_ada)


def _rope_body(inv_ref, cos_ref, sin_ref, *, pos0, tl, half):
    row = lax.broadcasted_iota(I32, (tl, LANES), 0) + pl.program_id(0) * tl
    ang = (row.astype(F32) + pos0) * inv_ref[...]
    lane = lax.broadcasted_iota(I32, (tl, LANES), 1)
    s = jnp.sin(ang)
    cos_ref[...] = jnp.cos(ang)
    sin_ref[...] = jnp.where(lane < half, -s, s)


def _rope_tables(inv2, rows, pos0):
    tl = min(rows, 256)
    return pl.pallas_call(
        functools.partial(_rope_body, pos0=float(pos0), tl=tl, half=LANES // 2),
        grid=(rows // tl,),
        in_specs=[pl.BlockSpec((1, LANES), lambda i: (0, 0))],
        out_specs=[pl.BlockSpec((tl, LANES), lambda i: (i, 0))] * 2,
        out_shape=[jax.ShapeDtypeStruct((rows, LANES), F32)] * 2,
        name="rope",
    )(inv2)


def _modulated_rmsnorm(x, g, sc, sh):
    ms = jnp.mean(x * x, axis=-1, keepdims=True)
    h = x * lax.rsqrt(ms + EPS) * g
    return h * (1.0 + sc) + sh


def _proj_body(x_ref, sh_ref, sc_ref, g_ref, w_ref, cos_ref, sin_ref,
               glu_ref, q_ref, k_ref, v_ref, sg_ref, *, dc, dr, hd, rope_rows):
    hb = _modulated_rmsnorm(x_ref[...], g_ref[...], sc_ref[...], sh_ref[...]).astype(BF16)

    def proj(lo, n):
        return jnp.dot(hb, w_ref[:, lo:lo + n], preferred_element_type=F32)

    glu_ref[...] = proj(0, dc) * _sigmoid(proj(dc, dc))
    cos = cos_ref[...] if rope_rows else cos_ref[0:1, :]
    sin = sin_ref[...] if rope_rows else sin_ref[0:1, :]
    for ref, lo, scale in ((q_ref, 2 * dc, hd ** -0.5), (k_ref, 2 * dc + dr, None)):
        t = proj(lo, dr)
        for hh in range(dr // hd):
            th = t[:, hh * hd:(hh + 1) * hd]
            r = th * cos + pltpu.roll(th, hd // 2, 1) * sin
            if scale is not None:
                r = r * scale
            ref[:, hh * hd:(hh + 1) * hd] = r.astype(BF16)
    v_ref[...] = proj(2 * dc + 2 * dr, dr).astype(BF16)
    sg_ref[...] = _silu(proj(2 * dc + 3 * dr, dr)).astype(BF16)


def _proj(x, mod, g_mix, w_in_b, cos2, sin2, *, dc, dr, hd, per_row_mod, b0=0, nb=None):
    nb_all, length, d = x.shape
    nb = nb_all if nb is None else nb
    tl = min(length, 512)
    x2 = x.reshape(nb_all * length, d)
    nl = length // tl
    if per_row_mod:
        mod_spec = lambda j: pl.BlockSpec((tl, d), lambda b, l: (l, j))
        rope_spec = pl.BlockSpec((SUBLANES, LANES), lambda b, l: (0, 0))
    else:
        mod_spec = lambda j: pl.BlockSpec((None, None, 1, d), lambda b, l: (b + b0, j, 0, 0))
        rope_spec = pl.BlockSpec((tl, LANES), lambda b, l: (l, 0))
    row_spec = lambda w: pl.BlockSpec((tl, w), lambda b, l: (b * nl + l, 0))
    t = nb * length
    outs = pl.pallas_call(
        functools.partial(_proj_body, dc=dc, dr=dr, hd=hd, rope_rows=not per_row_mod),
        grid=(nb, nl),
        in_specs=[
            pl.BlockSpec((tl, d), lambda b, l: ((b + b0) * nl + l, 0)), mod_spec(0), mod_spec(1),
            pl.BlockSpec((1, d), lambda b, l: (0, 0)),
            pl.BlockSpec(w_in_b.shape, lambda b, l: (0, 0)),
            rope_spec, rope_spec,
        ],
        out_specs=[row_spec(dc), row_spec(dr), row_spec(dr), row_spec(dr), row_spec(dr)],
        out_shape=[
            jax.ShapeDtypeStruct((t, dc), F32),
            jax.ShapeDtypeStruct((t, dr), BF16),
            jax.ShapeDtypeStruct((t, dr), BF16),
            jax.ShapeDtypeStruct((t, dr), BF16),
            jax.ShapeDtypeStruct((t, dr), BF16),
        ],
        compiler_params=pltpu.CompilerParams(vmem_limit_bytes=VMEM_LIMIT),
        name="proj",
    )(x2, mod, mod, g_mix, w_in_b, cos2, sin2)
    return outs


def _layernorm_silu(c, g, b):
    mu = jnp.mean(c, axis=-1, keepdims=True)
    d = c - mu
    var = jnp.mean(d * d, axis=-1, keepdims=True)
    return _silu(d * lax.rsqrt(var + EPS) * g + b)


def _groupnorm(o, g, b):
    mu = jnp.mean(o, axis=-1, keepdims=True)
    d = o - mu
    var = jnp.mean(d * d, axis=-1, keepdims=True)
    return d * lax.rsqrt(var + EPS) * g + b


def _mix_body(glu_ref, q_ref, k_ref, v_ref, sg_ref, cw_ref, cb_ref, lng_ref, lnb_ref, rg_ref, rb_ref,
              mask_ref, qd_ref, kd_ref, cd_ref, cat_ref, st_ref, buf, cscr, *, tl, dc, hd, nh, chunk):
    nslab = dc // LANES

    @pl.when(pl.program_id(1) == 0)
    def _():
        buf[:, 0:CONV_PAD, :] = jnp.zeros((nslab, CONV_PAD, LANES), F32)
        st_ref[...] = jnp.zeros(st_ref.shape, F32)

    for j in range(nslab):
        buf[j, CONV_PAD:CONV_PAD + tl, :] = glu_ref[:, LANES * j:LANES * (j + 1)]
    first = CONV_PAD - CONV_BUF
    rows_per_iter = 8 * SUBLANES
    for j in range(nslab):
        cols = slice(LANES * j, LANES * (j + 1))
        wv = [jnp.broadcast_to(cw_ref[t:t + 1, cols], (SUBLANES, LANES)) for t in range(CONV_WIDTH)]
        bias = jnp.broadcast_to(cb_ref[0:1, cols], (SUBLANES, LANES))

        def body(r, carry, j=j, cols=cols, wv=wv, bias=bias):
            base = pl.multiple_of(r * rows_per_iter, rows_per_iter)
            for u in range(rows_per_iter // SUBLANES):
                acc = bias
                for t in range(CONV_WIDTH):
                    acc = acc + wv[t] * buf[j, pl.ds(base + (u * SUBLANES + first + t), SUBLANES), :]
                cscr[pl.ds(base + u * SUBLANES, SUBLANES), cols] = acc
            return carry

        lax.fori_loop(0, tl // rows_per_iter, body, 0)
    for j in range(nslab):
        buf[j, 0:CONV_PAD, :] = buf[j, tl:tl + CONV_PAD, :]
    cat_ref[:, 0:dc] = _layernorm_silu(cscr[...], lng_ref[...], lnb_ref[...]).astype(BF16)

    nt = (((1,), (1,)), ((), ()))
    tn = (((0,), (0,)), ((), ()))
    for c in range(tl // chunk):
        rows = slice(c * chunk, (c + 1) * chunk)
        for hh in range(nh):
            cols = slice(hh * hd, (hh + 1) * hd)
            qh = q_ref[rows, cols]
            kh = k_ref[rows, cols]
            vh = v_ref[rows, cols]
            s = st_ref[0, hh]
            scores = lax.dot_general(qh, kh, nt, preferred_element_type=F32) * mask_ref[hh]
            inner = jnp.dot(scores.astype(BF16), vh, preferred_element_type=F32)
            qd = (qh.astype(F32) * qd_ref[hh]).astype(BF16)
            cross = jnp.dot(qd, s.astype(BF16), preferred_element_type=F32)
            kd = (kh.astype(F32) * kd_ref[hh]).astype(BF16)
            st_ref[0, hh] = cd_ref[hh] * s + lax.dot_general(kd, vh, tn, preferred_element_type=F32)
            o = _groupnorm(inner + cross, rg_ref[0:1, cols], rb_ref[0:1, cols])
            cat_ref[rows, dc + hh * hd:dc + (hh + 1) * hd] = (o * sg_ref[rows, cols].astype(F32)).astype(BF16)


def _mix(glu, q, k, v, sg, conv_w, conv_b, ln_g, ln_b, rg, rb, tables, *, nb, length, dc, dr, hd):
    nh = dr // hd
    chunk = math.gcd(length, RET_CHUNK)
    tl = min(length, 256)
    nl = length // tl
    mask, qd, kd, cd = tables
    row_spec = lambda w: pl.BlockSpec((tl, w), lambda b, l: (b * nl + l, 0))
    full = lambda a: pl.BlockSpec(a.shape, lambda b, l: (0,) * a.ndim)
    cat, st = pl.pallas_call(
        functools.partial(_mix_body, tl=tl, dc=dc, hd=hd, nh=nh, chunk=chunk),
        grid=(nb, nl),
        in_specs=[row_spec(dc), row_spec(dr), row_spec(dr), row_spec(dr), row_spec(dr),
                  full(conv_w), full(conv_b), full(ln_g), full(ln_b), full(rg), full(rb),
                  full(mask), full(qd), full(kd), full(cd)],
        out_specs=[row_spec(dc + dr), pl.BlockSpec((1, nh, hd, hd), lambda b, l: (b, 0, 0, 0))],
        out_shape=[jax.ShapeDtypeStruct((nb * length, dc + dr), BF16),
                   jax.ShapeDtypeStruct((nb, nh, hd, hd), F32)],
        scratch_shapes=[pltpu.VMEM((dc // LANES, tl + CONV_PAD, LANES), F32),
                        pltpu.VMEM((tl, dc), F32)],
        compiler_params=pltpu.CompilerParams(dimension_semantics=("arbitrary", "arbitrary")),
        name="mix",
    )(glu, q, k, v, sg, conv_w, conv_b, ln_g, ln_b, rg, rb, mask, qd, kd, cd)
    return cat, st


def _mix1_body(glu_ref, q_ref, k_ref, v_ref, sg_ref, sc_ref, s0_ref, cw_ref, cb_ref, lng_ref, lnb_ref,
               rg_ref, rb_ref, gam_ref, cat_ref, st_ref, nc_ref, cscr, oscr, qf, kf, vf, *, tb, dc, hd, nh):
    w_hist = cw_ref[0:CONV_BUF, :]
    w_last = cw_ref[CONV_BUF:CONV_WIDTH, :]
    for bb in range(tb):
        hist = jnp.sum(sc_ref[bb] * w_hist, axis=0, keepdims=True)
        cscr[bb:bb + 1, :] = hist + glu_ref[bb:bb + 1, :] * w_last + cb_ref[...]
        nc_ref[bb, 0:CONV_BUF - 1, :] = sc_ref[bb, 1:CONV_BUF, :]
        nc_ref[bb, CONV_BUF - 1:CONV_BUF, :] = glu_ref[bb:bb + 1, :]
    cat_ref[:, 0:dc] = _layernorm_silu(cscr[...], lng_ref[...], lnb_ref[...]).astype(BF16)

    tn = (((0,), (0,)), ((), ()))
    rowid = lax.broadcasted_iota(I32, (tb, hd), 0)
    qf[...] = q_ref[...].astype(F32)
    kf[...] = k_ref[...].astype(F32)
    vf[...] = v_ref[...].astype(F32)
    for hh in range(nh):
        cols = slice(hh * hd, (hh + 1) * hd)
        qa = q_ref[:, cols]
        ka = k_ref[:, cols]
        gam = gam_ref[hh, 0:1, :]
        for bb in range(tb):
            onehot = (rowid == bb).astype(BF16)
            qcol = lax.dot_general(qa, onehot, tn, preferred_element_type=F32)
            kcol = lax.dot_general(ka, onehot, tn, preferred_element_type=F32)
            s0 = s0_ref[bb, hh]
            qrow = qf[bb:bb + 1, cols]
            krow = kf[bb:bb + 1, cols]
            vrow = vf[bb:bb + 1, cols]
            qk = jnp.sum(qrow * krow, axis=-1, keepdims=True)
            cross = gam * jnp.sum(qcol * s0, axis=0, keepdims=True)
            st_ref[bb, hh] = gam * s0 + kcol * vrow
            oscr[bb:bb + 1, cols] = qk * vrow + cross
    for hh in range(nh):
        cols = slice(hh * hd, (hh + 1) * hd)
        o = _groupnorm(oscr[:, cols], rg_ref[0:1, cols], rb_ref[0:1, cols])
        cat_ref[:, dc + hh * hd:dc + (hh + 1) * hd] = (o * sg_ref[:, cols].astype(F32)).astype(BF16)


def _mix1(glu, q, k, v, sg, state_conv, state_ret, conv_w, conv_b, ln_g, ln_b, rg, rb, gam, *, dc, dr, hd):
    nb = glu.shape[0]
    nh = dr // hd
    tb = 16
    row_spec = lambda w: pl.BlockSpec((tb, w), lambda i: (i, 0))
    full = lambda a: pl.BlockSpec(a.shape, lambda i: (0,) * a.ndim)
    st_spec = pl.BlockSpec((None, tb, nh, hd, hd), lambda i: (0, i, 0, 0, 0))
    conv_spec = pl.BlockSpec((None, tb, CONV_BUF, dc), lambda i: (0, i, 0, 0))
    cat, st, new_conv = pl.pallas_call(
        functools.partial(_mix1_body, tb=tb, dc=dc, hd=hd, nh=nh),
        grid=(nb // tb,),
        in_specs=[row_spec(dc), row_spec(dr), row_spec(dr), row_spec(dr), row_spec(dr), conv_spec, st_spec,
                  full(conv_w), full(conv_b), full(ln_g), full(ln_b), full(rg), full(rb), full(gam)],
        out_specs=[row_spec(dc + dr), st_spec, conv_spec],
        out_shape=[jax.ShapeDtypeStruct((nb, dc + dr), BF16),
                   jax.ShapeDtypeStruct((1, nb, nh, hd, hd), F32),
                   jax.ShapeDtypeStruct((1, nb, CONV_BUF, dc), F32)],
        scratch_shapes=[pltpu.VMEM((tb, dc), F32)] + [pltpu.VMEM((tb, dr), F32)] * 4,
        compiler_params=pltpu.CompilerParams(vmem_limit_bytes=VMEM_LIMIT),
        name="mix1",
    )(glu, q, k, v, sg, state_conv, state_ret, conv_w, conv_b, ln_g, ln_b, rg, rb, gam)
    return cat, st, new_conv


def _post_body(cat_ref, x_ref, gtm_ref, scf_ref, shf_ref, gtf_ref, g_ref, wo_ref, wrh_ref, wrl_ref,
               ws1_ref, ws3_ref, ws2_ref, x2_ref, hp_ref, lg_ref):
    d = x_ref.shape[1]
    y = jnp.dot(cat_ref[...], wo_ref[...], preferred_element_type=F32)
    x1 = x_ref[...] + gtm_ref[...] * y
    h = _modulated_rmsnorm(x1, g_ref[...], scf_ref[...], shf_ref[...])
    hb = h.astype(BF16)
    _store_rows(hp_ref, _pack_halves(h[:, 0:d // 2], h[:, d // 2:d]))
    hl = (h - hb.astype(F32)).astype(BF16)
    nt = (((1,), (1,)), ((), ()))
    lg_ref[...] = (lax.dot_general(wrh_ref[...], hb, nt, preferred_element_type=F32)
                   + lax.dot_general(wrh_ref[...], hl, nt, preferred_element_type=F32)
                   + lax.dot_general(wrl_ref[...], hb, nt, preferred_element_type=F32))
    s1 = jnp.dot(hb, ws1_ref[...], preferred_element_type=F32)
    s3 = jnp.dot(hb, ws3_ref[...], preferred_element_type=F32)
    shared = jnp.dot((_silu(s1) * s3).astype(BF16), ws2_ref[...], preferred_element_type=F32)
    x2_ref[...] = x1 + gtf_ref[...] * shared


def _post(cat, x, mod, g_ffn, wo_b, wrh, wrl, ws1_b, ws3_b, ws2_b, *, per_row_mod, b0=0, nb=None):
    nb_all, length, d = x.shape
    nb = nb_all if nb is None else nb
    tl = min(length, 1024)
    nl = length // tl
    t = nb * length
    ne = wrh.shape[0]
    x2d = x.reshape(nb_all * length, d)
    if per_row_mod:
        mod_spec = lambda j: pl.BlockSpec((tl, d), lambda b, l: (l, j))
    else:
        mod_spec = lambda j: pl.BlockSpec((None, None, 1, d), lambda b, l: (b + b0, j, 0, 0))
    row_spec = lambda w: pl.BlockSpec((tl, w), lambda b, l: (b * nl + l, 0))
    full = lambda a: pl.BlockSpec(a.shape, lambda b, l: (0,) * a.ndim)
    return pl.pallas_call(
        _post_body,
        grid=(nb, nl),
        in_specs=[row_spec(d), pl.BlockSpec((tl, d), lambda b, l: ((b + b0) * nl + l, 0)),
                  mod_spec(2), mod_spec(4), mod_spec(3), mod_spec(5),
                  full(g_ffn), full(wo_b), full(wrh), full(wrl), full(ws1_b), full(ws3_b), full(ws2_b)],
        out_specs=[row_spec(d),
                   pl.BlockSpec((tl * ROW_WORDS, LANES), lambda b, l: (b * nl + l, 0)),
                   pl.BlockSpec((ne, tl), lambda b, l: (0, b * nl + l))],
        out_shape=[jax.ShapeDtypeStruct((t, d), F32),
                   jax.ShapeDtypeStruct((t * ROW_WORDS, LANES), U32),
                   jax.ShapeDtypeStruct((ne, t), F32)],
        compiler_params=pltpu.CompilerParams(vmem_limit_bytes=VMEM_LIMIT),
        name="post",
    )(cat, x2d, mod, mod, mod, mod, g_ffn, wo_b, wrh, wrl, ws1_b, ws3_b, ws2_b)


def _first_max(x, idx, sentinel):
    m = jnp.max(x, axis=0, keepdims=True)
    f = jnp.min(jnp.where(x == m, idx, sentinel), axis=0, keepdims=True)
    return m, f


def _route_body(lg_ref, bias_ref, e_ref, w_ref, r_ref, cnt_ref, cnt_scr, *, tr, ne, ng, topk, topg):
    @pl.when(pl.program_id(0) == 0)
    def _():
        cnt_scr[...] = jnp.zeros(cnt_scr.shape, F32)

    per = ne // ng
    neg = -jnp.inf
    scores = _sigmoid(lg_ref[...])
    sel = scores + bias_ref[...]
    sub = lax.broadcasted_iota(I32, (per, tr), 0)
    gs = []
    for g in range(ng):
        s_g = sel[g * per:(g + 1) * per, :]
        m1, f1 = _first_max(s_g, sub, per)
        m2 = jnp.max(jnp.where(sub == f1, neg, s_g), axis=0, keepdims=True)
        gs.append(m1 + m2)
    gsc = jnp.concatenate(gs, axis=0)
    gi = lax.broadcasted_iota(I32, (ng, tr), 0)
    keep = jnp.zeros((ng, tr), F32)
    for _ in range(topg):
        _, f = _first_max(gsc, gi, ng)
        pick = gi == f
        keep = jnp.where(pick, 1.0, keep)
        gsc = jnp.where(pick, neg, gsc)
    work = jnp.concatenate(
        [jnp.where(keep[g:g + 1, :] > 0.5, sel[g * per:(g + 1) * per, :], neg) for g in range(ng)], axis=0)
    ei = lax.broadcasted_iota(I32, (ne, tr), 0)
    picks, es, ws = [], [], []
    for _ in range(topk):
        _, f = _first_max(work, ei, ne)
        pick = ei == f
        picks.append(pick)
        es.append(f)
        ws.append(jnp.sum(jnp.where(pick, scores, 0.0), axis=0, keepdims=True))
        work = jnp.where(pick, neg, work)
    wsum = ws[0]
    for w in ws[1:]:
        wsum = wsum + w
    scale = ROUTED_SCALE / wsum
    chosen = picks[0]
    for p in picks[1:]:
        chosen = jnp.logical_or(chosen, p)
    chosen_f = chosen.astype(F32)
    t_row = lax.broadcasted_iota(I32, (tr, tr), 0)
    t_col = lax.broadcasted_iota(I32, (tr, tr), 1)
    before = (t_row < t_col).astype(BF16)
    prior = cnt_scr[:, 0:1] + jnp.dot(chosen_f.astype(BF16), before, preferred_element_type=F32)
    rs = [jnp.sum(jnp.where(p, prior, 0.0), axis=0, keepdims=True).astype(I32) for p in picks]
    pad_i = jnp.zeros((SUBLANES - topk, tr), I32)
    pad_f = jnp.zeros((SUBLANES - topk, tr), F32)
    e_ref[...] = jnp.concatenate(es + [pad_i], axis=0)
    w_ref[...] = jnp.concatenate([w * scale for w in ws] + [pad_f], axis=0)
    r_ref[...] = jnp.concatenate(rs + [pad_i], axis=0)
    total = cnt_scr[:, 0:1] + jnp.sum(chosen_f, axis=1, keepdims=True)
    cnt_scr[...] = jnp.broadcast_to(total, cnt_scr.shape)
    cnt_ref[...] = jnp.broadcast_to(total, cnt_ref.shape)


def _route_tile(t):
    return max(m for m in range(LANES, ROUTE_TILE + 1, LANES) if t % m == 0)


def _route(logits_t, bias_col):
    ne, t = logits_t.shape
    tr = _route_tile(t)
    tok = lambda dt: jax.ShapeDtypeStruct((SUBLANES, t), dt)
    tok_spec = pl.BlockSpec((SUBLANES, tr), lambda i: (0, i))
    return pl.pallas_call(
        functools.partial(_route_body, tr=tr, ne=ne, ng=N_GROUPS, topk=TOP_K, topg=TOPK_GROUPS),
        grid=(t // tr,),
        in_specs=[pl.BlockSpec((ne, tr), lambda i: (0, i)), pl.BlockSpec((ne, 1), lambda i: (0, 0))],
        out_specs=[tok_spec, tok_spec, tok_spec, pl.BlockSpec((ne, LANES), lambda i: (0, 0))],
        out_shape=[tok(I32), tok(F32), tok(I32), jax.ShapeDtypeStruct((ne, LANES), F32)],
        scratch_shapes=[pltpu.VMEM((ne, LANES), F32)],
        compiler_params=pltpu.CompilerParams(dimension_semantics=("arbitrary",)),
        name="route",
    )(logits_t, bias_col)


def _dest_body(cnt_ref, e_ref, r_ref, d_ref, ps, *, ne, tr):
    shift = EXPERT_ROWS.bit_length() - 1

    @pl.when(pl.program_id(0) == 0)
    def _():
        def step(j, start):
            ps[j] = start
            return start + lax.shift_left(lax.shift_right_logical(cnt_ref[j] + (EXPERT_ROWS - 1), shift), shift)

        lax.fori_loop(0, ne, step, jnp.int32(0))

    e = e_ref[...]
    base = jnp.zeros(e.shape, I32)
    for j in range(ne):
        base = jnp.where(e == j, ps[j], base)
    dest = base + r_ref[...]
    for m in range(tr // TOKEN_TILE):
        d_ref[SUBLANES * m:SUBLANES * (m + 1), :] = dest[:, TOKEN_TILE * m:TOKEN_TILE * (m + 1)]


def _dest_rows(counts, eidx, rank):
    rows, t = eidx.shape
    ne = counts.shape[0]
    tr = _route_tile(t)
    spec = pl.BlockSpec((rows, tr), lambda i, cnt: (0, i))
    return pl.pallas_call(
        functools.partial(_dest_body, ne=ne, tr=tr),
        grid_spec=pltpu.PrefetchScalarGridSpec(
            num_scalar_prefetch=1, grid=(t // tr,), in_specs=[spec, spec],
            out_specs=pl.BlockSpec((tr // TOKEN_TILE * rows, TOKEN_TILE), lambda i, cnt: (i, 0)),
            scratch_shapes=[pltpu.SMEM((ne,), I32)]),
        out_shape=jax.ShapeDtypeStruct((t // TOKEN_TILE * rows, TOKEN_TILE), I32),
        compiler_params=pltpu.CompilerParams(dimension_semantics=("arbitrary",)),
        name="dest",
    )(counts, eidx, rank)


def _sc_mesh():
    return plsc.VectorSubcoreMesh(core_axis_name="c", subcore_axis_name="s")


def _sc_worker_id():
    return lax.axis_index("s") * SC_CORES + lax.axis_index("c")


def _index_block(dest_ref, chunk, width):
    per_tile = TOKEN_TILE // width
    return dest_ref.at[chunk // per_tile, :, pl.ds((chunk % per_tile) * width, width)]


def _dispatch(hps, dest3, n_rows):
    w = SCATTER_ROWS
    width = hps[0].shape[1]
    bounds = [0]
    for h in hps:
        bounds.append(bounds[-1] + h.shape[0] // w)
    nch = bounds[-1]
    nsrc = len(hps)

    @functools.partial(
        pl.kernel, mesh=_sc_mesh(),
        out_type=jax.ShapeDtypeStruct((n_rows, width), U32),
        scratch_types=[pltpu.VMEM((2, SUBLANES, w), I32), pltpu.VMEM((2, w, width), U32),
                       pltpu.SemaphoreType.DMA, pltpu.SemaphoreType.DMA, pltpu.SemaphoreType.DMA],
        compiler_params=pltpu.CompilerParams(use_tc_tiling_on_sc=False),
        name="dispatch",
    )
    def run(*refs):
        src_refs, dest_ref, xs_ref = refs[:nsrc], refs[nsrc], refs[nsrc + 1]
        idx_v, rows_v = refs[nsrc + 2:nsrc + 4]
        sem_load = refs[nsrc + 4:nsrc + 6]
        sem_scatter = refs[nsrc + 6]
        wid = _sc_worker_id()

        def start_loads(c, slot):
            pltpu.async_copy(_index_block(dest_ref, c, w), idx_v.at[slot], sem_load[slot])
            for i, src in enumerate(src_refs):
                @pl.when(jnp.logical_and(c >= bounds[i], c < bounds[i + 1]))
                def _(src=src, lo=bounds[i]):
                    pltpu.async_copy(src.at[pl.ds((c - lo) * w, w)], rows_v.at[slot], sem_load[slot])

        def wait_loads(slot):
            pltpu.make_async_copy(_index_block(dest_ref, 0, w), idx_v.at[slot], sem_load[slot]).wait()
            pltpu.make_async_copy(src_refs[0].at[pl.ds(0, w)], rows_v.at[slot], sem_load[slot]).wait()

        @pl.when(wid < nch)
        def _():
            start_loads(wid, 0)

        @pl.loop(0, pl.cdiv(pl.cdiv(nch, SC_WORKERS), 2))
        def _(rr):
            for slot in range(2):
                c = (rr * 2 + slot) * SC_WORKERS + wid

                @pl.when(c < nch)
                def _(c=c, slot=slot):
                    wait_loads(slot)

                    @pl.when(c + SC_WORKERS < nch)
                    def _():
                        start_loads(c + SC_WORKERS, 1 - slot)

                    copies = [pltpu.async_copy(rows_v.at[slot], xs_ref.at[idx_v.at[slot, k]], sem_scatter)
                              for k in range(TOP_K)]
                    for cp in copies:
                        cp.wait()

    return run(*hps, dest3)


def _undispatch(ys, dest3, n_tokens, after=None):
    w = GATHER_ROWS
    width = ys.shape[1]
    nch = n_tokens // w

    @functools.partial(
        pl.kernel, mesh=_sc_mesh(),
        out_type=jax.ShapeDtypeStruct((TOP_K, n_tokens, width), U32),
        scratch_types=[pltpu.VMEM((2, SUBLANES, w), I32), pltpu.VMEM((2, TOP_K, w, width), U32),
                       pltpu.SemaphoreType.DMA, pltpu.SemaphoreType.DMA, pltpu.SemaphoreType.DMA],
        compiler_params=pltpu.CompilerParams(use_tc_tiling_on_sc=False),
        name="undispatch",
    )
    def run(*refs):
        ys_ref, dest_ref = refs[:2]
        z_ref, idx_v, bufs = refs[-6:-3]
        sem_gather = refs[-3:-1]
        sem_store = refs[-1]
        wid = _sc_worker_id()

        def start_gathers(c, slot):
            pltpu.sync_copy(_index_block(dest_ref, c, w), idx_v.at[slot])
            for k in range(TOP_K):
                pltpu.async_copy(ys_ref.at[idx_v.at[slot, k]], bufs.at[slot, k], sem_gather[slot])

        def wait_gathers(slot):
            for k in range(TOP_K):
                pltpu.make_async_copy(ys_ref.at[idx_v.at[slot, k]], bufs.at[slot, k], sem_gather[slot]).wait()

        @pl.when(wid < nch)
        def _():
            start_gathers(wid, 0)

        @pl.loop(0, pl.cdiv(pl.cdiv(nch, SC_WORKERS), 2))
        def _(rr):
            for slot in range(2):
                c = (rr * 2 + slot) * SC_WORKERS + wid

                @pl.when(c < nch)
                def _(c=c, slot=slot):
                    @pl.when(c + SC_WORKERS < nch)
                    def _():
                        start_gathers(c + SC_WORKERS, 1 - slot)

                    wait_gathers(slot)
                    stores = [pltpu.async_copy(bufs.at[slot, k], z_ref.at[k, pl.ds(c * w, w)], sem_store)
                              for k in range(TOP_K)]
                    for cp in stores:
                        cp.wait()

    return run(ys, dest3) if after is None else run(ys, dest3, after)


PART_SHIFT = 24


def _expert_body(*refs, tm, ne, nparts):
    cnt_refs = refs[:nparts]
    xs_refs = refs[nparts:2 * nparts]
    w1_ref, w3_ref, w2_ref = refs[2 * nparts:2 * nparts + 3]
    ys_refs = refs[2 * nparts + 3:3 * nparts + 3]
    (w1f, w3f, w2f, w1s, w3s, w2s, xbuf, ybuf, xlo, xhi, sched, sem_x, sem_y, sem_w) = refs[3 * nparts + 3:]
    blk_words = tm * ROW_WORDS
    half = ROW_WORDS * LANES
    shift = tm.bit_length() - 1

    def n_blocks_of(p, e):
        return lax.shift_right_logical(cnt_refs[p][e] + (tm - 1), shift)

    def n_all(e):
        n = n_blocks_of(0, e)
        for p in range(1, nparts):
            n = n + n_blocks_of(p, e)
        return n

    def next_nonempty(e):
        return lax.while_loop(
            lambda c: jnp.logical_and(c < ne, n_all(jnp.minimum(c, ne - 1)) == 0), lambda c: c + 1, e)

    def plan(e, carry):
        i, starts = carry[0], list(carry[1:])
        for p in range(nparts):
            n = n_blocks_of(p, e)

            def put(j, c, p=p, i=i, start=starts[p]):
                sched[i + j] = (start + j) + (p << PART_SHIFT)
                return c

            lax.fori_loop(0, n, put, 0)
            i = i + n
            starts[p] = starts[p] + n
        return (i, *starts)

    nu = lax.fori_loop(0, ne, plan, (jnp.int32(0),) * (nparts + 1))[0]

    def rows_of(code):
        blk = code & ((1 << PART_SHIFT) - 1)
        return pl.ds(pl.multiple_of(blk * blk_words, blk_words), blk_words)

    def x_start(i, slot):
        code = sched[i]
        for p in range(nparts):
            @pl.when(lax.shift_right_logical(code, PART_SHIFT) == p)
            def _(p=p):
                pltpu.make_async_copy(xs_refs[p].at[rows_of(code), :], xbuf.at[slot], sem_x.at[slot]).start()

    def x_wait(slot):
        pltpu.make_async_copy(xs_refs[0].at[pl.ds(0, blk_words), :], xbuf.at[slot], sem_x.at[slot]).wait()

    def y_start(i, slot):
        code = sched[i]
        for p in range(nparts):
            @pl.when(lax.shift_right_logical(code, PART_SHIFT) == p)
            def _(p=p):
                pltpu.make_async_copy(ybuf.at[slot], ys_refs[p].at[rows_of(code), :], sem_y.at[slot]).start()

    def y_wait(slot):
        pltpu.make_async_copy(ybuf.at[slot], ys_refs[0].at[pl.ds(0, blk_words), :], sem_y.at[slot]).wait()

    def w_copies(e, ws):
        return [pltpu.make_async_copy(src.at[e], dst.at[ws], sem_w.at[ws])
                for src, dst in ((w1_ref, w1f), (w3_ref, w3f), (w2_ref, w2f))]

    for q in range(X_AHEAD):
        @pl.when(q < nu)
        def _(q=q):
            x_start(q, q)

    e_first = next_nonempty(jnp.int32(0))

    @pl.when(e_first < ne)
    def _():
        for cp in w_copies(e_first, 0):
            cp.start()

    def blocks(i, nblk):
        for b in range(nblk):
            x_wait((i + b) % X_SLOTS)
        for b in range(nblk):
            nxt = i + b + X_AHEAD

            @pl.when(nxt < nu)
            def _(nxt=nxt):
                x_start(nxt, nxt % X_SLOTS)

        for b in range(nblk):
            @pl.when(i + b >= Y_SLOTS)
            def _(b=b):
                y_wait((i + b) % Y_SLOTS)

        for b in range(nblk):
            xin = xbuf.at[(i + b) % X_SLOTS]
            for w in range(ROW_WORDS):
                lo, hi = _unpack_halves(_load_row_word(xin, w, tm))
                xlo[b, :, LANES * w:LANES * (w + 1)] = lo.astype(BF16)
                xhi[b, :, LANES * w:LANES * (w + 1)] = hi.astype(BF16)

        def up(b, wsc):
            return (jnp.dot(xlo[b], wsc[0:half, :], preferred_element_type=F32)
                    + jnp.dot(xhi[b], wsc[half:2 * half, :], preferred_element_type=F32))

        for b in range(nblk):
            hid = (_silu(up(b, w1s)) * up(b, w3s)).astype(BF16)
            y = jnp.dot(hid, w2s[...], preferred_element_type=F32)
            _store_rows(ybuf.at[(i + b) % Y_SLOTS], _pack_halves(y[:, 0:half], y[:, half:2 * half]))
        for b in range(nblk):
            y_start(i + b, (i + b) % Y_SLOTS)

    def per_expert(e, carry):
        i0, ws = carry
        n = n_all(e)

        @pl.when(n > 0)
        def _():
            for cp in w_copies(e, ws):
                cp.wait()
            w1s[...] = w1f[ws].astype(BF16)
            w3s[...] = w3f[ws].astype(BF16)
            w2s[...] = w2f[ws].astype(BF16)
            e_next = next_nonempty(e + 1)

            @pl.when(e_next < ne)
            def _():
                for cp in w_copies(e_next, 1 - ws):
                    cp.start()

            def pair(j, c):
                blocks(i0 + 2 * j, 2)
                return c

            lax.fori_loop(0, lax.shift_right_logical(n, 1), pair, 0)

            @pl.when(n % 2 == 1)
            def _():
                blocks(i0 + n - 1, 1)

        return i0 + n, jnp.where(n > 0, 1 - ws, ws)

    lax.fori_loop(0, ne, per_expert, (jnp.int32(0), jnp.int32(0)))

    for q in range(Y_SLOTS):
        @pl.when(nu > q)
        def _(q=q):
            y_wait((nu - 1 - q) % Y_SLOTS)


def _experts(counts, xss, w1e, w3e, w2e):
    tm = EXPERT_ROWS
    ne, d, de = w1e.shape
    half = ROW_WORDS * LANES
    nparts = len(xss)
    cap = sum(x.shape[0] // (tm * ROW_WORDS) for x in xss)
    assert cap < (1 << PART_SHIFT)
    anyspec = pl.BlockSpec(memory_space=pl.ANY)
    blk_buf = lambda n: pltpu.VMEM((n, tm * ROW_WORDS, LANES), U32)
    return pl.pallas_call(
        functools.partial(_expert_body, tm=tm, ne=ne, nparts=nparts),
        grid_spec=pltpu.PrefetchScalarGridSpec(
            num_scalar_prefetch=nparts,
            grid=(1,),
            in_specs=[anyspec] * (nparts + 3),
            out_specs=[anyspec] * nparts,
            scratch_shapes=[pltpu.VMEM((2, d, de), F32), pltpu.VMEM((2, d, de), F32), pltpu.VMEM((2, de, d), F32),
                            pltpu.VMEM((d, de), BF16), pltpu.VMEM((d, de), BF16), pltpu.VMEM((de, d), BF16),
                            blk_buf(X_SLOTS), blk_buf(Y_SLOTS),
                            pltpu.VMEM((2, tm, half), BF16), pltpu.VMEM((2, tm, half), BF16),
                            pltpu.SMEM((cap,), I32),
                            pltpu.SemaphoreType.DMA((X_SLOTS,)), pltpu.SemaphoreType.DMA((Y_SLOTS,)),
                            pltpu.SemaphoreType.DMA((2,))]),
        out_shape=[jax.ShapeDtypeStruct(x.shape, U32) for x in xss],
        compiler_params=pltpu.CompilerParams(dimension_semantics=("arbitrary",), vmem_limit_bytes=VMEM_LIMIT),
        name="experts",
    )(*counts, *xss, w1e, w3e, w2e)


def _combine_body(z_ref, x2_ref, gtf_ref, wt_ref, gfin_ref, y_ref, xo, *, td, topk):
    half = ROW_WORDS * LANES
    wt = wt_ref[...].T
    ws = [wt[:, k:k + 1] for k in range(topk)]
    sq = jnp.zeros((td, 1), F32)
    for j in range(ROW_WORDS):
        acc_lo = jnp.zeros((td, LANES), F32)
        acc_hi = jnp.zeros((td, LANES), F32)
        for k in range(topk):
            lo, hi = _unpack_halves(_load_row_word(z_ref.at[k], j, td))
            acc_lo = acc_lo + ws[k] * lo
            acc_hi = acc_hi + ws[k] * hi
        for base, acc in ((0, acc_lo), (half, acc_hi)):
            cols = slice(base + LANES * j, base + LANES * (j + 1))
            x = x2_ref[:, cols] + gtf_ref[:, cols] * acc
            xo[:, cols] = x
            sq = sq + jnp.sum(x * x, axis=-1, keepdims=True)
    rs = lax.rsqrt(sq / (2 * half) + EPS)
    y_ref[...] = xo[...] * rs * gfin_ref[...]


def _combine_body_into(z_ref, x2_ref, gtf_ref, wt_ref, gfin_ref, prev_ref, y_ref, xo, *, td, topk):
    del prev_ref
    _combine_body(z_ref, x2_ref, gtf_ref, wt_ref, gfin_ref, y_ref, xo, td=td, topk=topk)


def _combine(z, token0, wts, x2, mod, g_final, *, rows_per_mod, per_row_mod, b0=0, out_rows=None, into=None):
    t, d = x2.shape
    td = min(t, COMBINE_TILE)
    tile0 = token0 // td
    out_rows = t if out_rows is None else out_rows
    if per_row_mod:
        gtf_spec = pl.BlockSpec((td, d), lambda i: (i, 5))
        out0 = 0
    else:
        tiles_per_mod = rows_per_mod // td
        gtf_spec = pl.BlockSpec((None, None, 1, d), lambda i: (i // tiles_per_mod + b0, 5, 0, 0))
        out0 = b0 * tiles_per_mod
    in_specs = [pl.BlockSpec((TOP_K, td * ROW_WORDS, LANES), lambda i: (0, i + tile0, 0)),
                pl.BlockSpec((td, d), lambda i: (i, 0)),
                gtf_spec,
                pl.BlockSpec((SUBLANES, td), lambda i: (0, i + tile0)),
                pl.BlockSpec((1, d), lambda i: (0, 0))]
    args = [z, x2, mod, wts, g_final]
    body, aliases = _combine_body, {}
    if into is not None:
        in_specs.append(pl.BlockSpec(memory_space=pl.ANY))
        args.append(into)
        body, aliases = _combine_body_into, {len(args) - 1: 0}
    return pl.pallas_call(
        functools.partial(body, td=td, topk=TOP_K),
        grid=(t // td,),
        in_specs=in_specs,
        out_specs=pl.BlockSpec((td, d), lambda i: (i + out0, 0)),
        scratch_shapes=[pltpu.VMEM((td, d), F32)],
        out_shape=jax.ShapeDtypeStruct((out_rows, d), F32),
        input_output_aliases=aliases,
        compiler_params=pltpu.CompilerParams(vmem_limit_bytes=VMEM_LIMIT),
        name="combine",
    )(*args)


def _retention_tables(length, nh, hd):
    c = math.gcd(length, RET_CHUNK)
    log_g = jnp.log(1.0 - 2.0 ** (-5.0 - jnp.arange(nh, dtype=F32)))
    idx = jnp.arange(c, dtype=F32)
    rel = idx[:, None] - idx[None, :]
    mask = jnp.where(rel >= 0, jnp.exp(log_g[:, None, None] * jnp.maximum(rel, 0.0)), 0.0)
    q_decay = jnp.exp(log_g[None, :] * (idx[:, None] + 1.0))
    k_decay = jnp.exp(log_g[None, :] * (c - 1.0 - idx[:, None]))
    chunk_decay = jnp.exp(log_g * c)
    qd = jnp.broadcast_to(q_decay.T[:, :, None], (nh, c, hd))
    kd = jnp.broadcast_to(k_decay.T[:, :, None], (nh, c, hd))
    cd = jnp.broadcast_to(chunk_decay[:, None, None], (nh, hd, hd))
    return mask, qd, kd, cd


def kernel(x_prompt, x_sample, c_prompt, c_sample, state_conv, state_ret, w_ada, b_ada, g_mix, g_ffn, w_in,
           conv_w, conv_b, conv_norm_g, conv_norm_b, ret_norm_g, ret_norm_b, w_out, w_router, router_bias,
           w1, w3, w2, ws1, ws3, ws2, g_final):
    depth = w_ada.shape[0]
    assert depth == 1, "single-layer trunk"
    bp, lp, d = x_prompt.shape
    bs, ls, _ = x_sample.shape
    assert ls == 1
    dc = conv_w.shape[2]
    dr = ret_norm_g.shape[1]
    nh = RET_HEADS
    hd = dr // nh
    assert hd == LANES and lp % 256 == 0 and bs % TOKEN_TILE == 0 and d // 2 == ROW_WORDS * LANES
    ne = w_router.shape[2]
    row = lambda a: a.reshape(1, -1)

    mod = _ada(jnp.concatenate([c_prompt, c_sample], axis=0), w_ada[0], row(b_ada[0]))
    mod_p = mod[:bp].reshape(bp, 6, 1, d)
    mod_s = mod[bp:]

    half = hd // 2
    inv = ROPE_BASE ** (-jnp.arange(half, dtype=F32) / half)
    inv2 = jnp.concatenate([inv, inv]).reshape(1, hd)
    cos_p, sin_p = _rope_tables(inv2, lp, 0)
    cos_s, sin_s = _rope_tables(inv2, SUBLANES, PAST_LEN)

    w_in_b = w_in[0].astype(BF16)
    wo_b = w_out[0].astype(BF16)
    wr_t = w_router[0].T
    wrh = wr_t.astype(BF16)
    wrl = (wr_t - wrh.astype(F32)).astype(BF16)
    ws1_b, ws3_b, ws2_b = ws1[0].astype(BF16), ws3[0].astype(BF16), ws2[0].astype(BF16)
    dims = dict(dc=dc, dr=dr, hd=hd)

    tables = _retention_tables(lp, nh, hd)
    log_g = jnp.log(1.0 - 2.0 ** (-5.0 - jnp.arange(nh, dtype=F32)))
    gam = jnp.broadcast_to(jnp.exp(log_g)[:, None, None], (nh, SUBLANES, hd))
    norm_rows = (row(conv_b[0]), row(conv_norm_g[0]), row(conv_norm_b[0]), row(ret_norm_g[0]), row(ret_norm_b[0]))
    post_w = (row(g_ffn[0]), wo_b, wrh, wrl, ws1_b, ws3_b, ws2_b)
    bias_col = router_bias[0].reshape(ne, 1)
    hw = d // 2
    tm = EXPERT_ROWS

    def pre_prompt(b0, nb):
        glu, q, k, v, sg = _proj(x_prompt, mod_p, row(g_mix[0]), w_in_b, cos_p, sin_p,
                                 per_row_mod=False, b0=b0, nb=nb, **dims)
        cat, ret = _mix(glu, q, k, v, sg, conv_w[0], *norm_rows, tables, nb=nb, length=lp, **dims)
        x2, hp, lg = _post(cat, x_prompt, mod_p, *post_w, per_row_mod=False, b0=b0, nb=nb)
        return glu, ret, x2, hp, lg

    def pre_sample():
        xs3 = x_sample.reshape(1, bs, d)
        glu, q, k, v, sg = _proj(xs3, mod_s, row(g_mix[0]), w_in_b, cos_s, sin_s, per_row_mod=True, **dims)
        cat, ret, new_conv = _mix1(glu, q, k, v, sg, state_conv, state_ret, conv_w[0], *norm_rows, gam,
                                   **dims)
        x2, hp, lg = _post(cat, xs3, mod_s, *post_w, per_row_mod=True)
        return new_conv, ret, x2, hp, lg

    def route_and_dispatch(hps, lgs):
        lg = lgs[0] if len(lgs) == 1 else jnp.concatenate(lgs, axis=1)
        tokens = lg.shape[1]
        eidx, wts, rank, cnt = _route(lg, bias_col)
        counts = cnt[:, 0].astype(I32)
        n_rows = -(-(tokens * TOP_K + ne * (tm - 1)) // tm) * tm
        dest3 = _dest_rows(counts, eidx, rank).reshape(tokens // TOKEN_TILE, SUBLANES, TOKEN_TILE)
        xs = _dispatch([h.reshape(-1, hw) for h in hps], dest3, n_rows)
        return counts, dest3, wts, xs.reshape(n_rows * ROW_WORDS, LANES)

    def undispatch(ys, dest3, after=None):
        tokens = dest3.shape[0] * TOKEN_TILE
        z = _undispatch(ys.reshape(-1, hw), dest3, tokens, after)
        return z.reshape(TOP_K, tokens * ROW_WORDS, LANES)

    nb0 = bp // 2
    nb1 = bp - nb0
    glu_0, ret_0, x2_0, hp_0, lg_0 = pre_prompt(0, nb0)
    counts_0, dest_0, wts_0, xs_0 = route_and_dispatch([hp_0], [lg_0])
    glu_1, ret_1, x2_1, hp_1, lg_1 = pre_prompt(nb0, nb1)
    new_conv_s, ret_s, x2_s, hp_s, lg_s = pre_sample()
    counts_1, dest_1, wts_1, xs_1 = route_and_dispatch([hp_1, hp_s], [lg_1, lg_s])
    ys_0, ys_1 = _experts((counts_0, counts_1), (xs_0, xs_1), w1[0], w3[0], w2[0])
    z_0 = undispatch(ys_0, dest_0)
    z_1 = undispatch(ys_1, dest_1, after=z_0[0, :SUBLANES])
    y_p = _combine(z_0, 0, wts_0, x2_0, mod_p, row(g_final), rows_per_mod=lp, per_row_mod=False,
                   out_rows=bp * lp)
    y_p = _combine(z_1, 0, wts_1, x2_1, mod_p, row(g_final), rows_per_mod=lp, per_row_mod=False,
                   b0=nb0, out_rows=bp * lp, into=y_p)
    y_s = _combine(z_1, nb1 * lp, wts_1, x2_s, mod_s, row(g_final), rows_per_mod=bs, per_row_mod=True)
    ret_p = jnp.concatenate([ret_0, ret_1], axis=0)

    tail = lambda g, n: g.reshape(n, lp, dc)[:, lp - CONV_BUF:, :]
    new_conv_p = jnp.concatenate([tail(glu_0, nb0), tail(glu_1, nb1)], axis=0)
    return (y_p.reshape(bp, lp, d), y_s.reshape(bs, ls, d), new_conv_p[None], ret_p[None],
            new_conv_s, ret_s)
```

```python
import functools
import math

import jax
import jax.numpy as jnp
import numpy as np
from jax import lax
from jax.experimental import pallas as pl
from jax.experimental.pallas import tpu as pltpu
from jax.experimental.pallas import tpu_sc as plsc

F32 = jnp.float32
BF16 = jnp.bfloat16
U32 = jnp.uint32
I32 = jnp.int32

EPS = 1e-6
PAST_LEN = 16384
RET_HEADS = 4
RET_CHUNK = 128
CONV_WIDTH = 31
CONV_BUF = CONV_WIDTH - 1
ROPE_BASE = 10000.0
N_EXPERTS = 64
TOP_K = 6
N_GROUPS = 8
TOPK_GROUPS = 4
ROUTED_SCALE = 2.5

LANES = 128
SUBLANES = 8
CONV_PAD = 32
EXPERT_ROWS = 256
ROUTE_TILE = 640
TOKEN_TILE = 128
COMBINE_TILE = 512
VMEM_LIMIT = 56 * 1024 * 1024
SC_CORES = 2
SC_SUBCORES = 16
SC_WORKERS = SC_CORES * SC_SUBCORES
SCATTER_ROWS = 64
GATHER_ROWS = 16
ROW_WORDS = 4
X_SLOTS = 6
X_AHEAD = 4
Y_SLOTS = 4

HI_MASK = 0xFFFF0000


def _sigmoid(x):
    return jax.nn.sigmoid(x)


def _silu(x):
    return x * jax.nn.sigmoid(x)


def _pack_halves(lo, hi):
    lo_u = lax.bitcast_convert_type(lo.astype(BF16).astype(F32), U32) >> 16
    hi_u = lax.bitcast_convert_type(hi.astype(BF16).astype(F32), U32) & jnp.uint32(HI_MASK)
    return hi_u | lo_u


def _unpack_halves(p):
    lo = lax.bitcast_convert_type(p << 16, F32)
    hi = lax.bitcast_convert_type(p & jnp.uint32(HI_MASK), F32)
    return lo, hi


def _store_rows(ref, x):
    rows = x.shape[0]
    for j in range(ROW_WORDS):
        ref[pl.ds(j, rows, stride=ROW_WORDS), :] = x[:, LANES * j:LANES * (j + 1)]


def _load_row_word(ref, j, rows):
    return ref[pl.ds(j, rows, stride=ROW_WORDS), :]


def _ada_body(c_ref, w_ref, b_ref, o_ref):
    s = _silu(c_ref[...]).astype(BF16)
    o_ref[...] = jnp.dot(s, w_ref[...].astype(BF16), preferred_element_type=F32) + b_ref[...]


def _ada(c_all, w_ada, b_ada):
    rows, d = c_all.shape
    n = w_ada.shape[1]
    tn = 2048
    return pl.pallas_call(
        _ada_body,
        grid=(n // tn,),
        in_specs=[
            pl.BlockSpec((rows, d), lambda j: (0, 0)),
            pl.BlockSpec((d, tn), lambda j: (0, j)),
            pl.BlockSpec((1, tn), lambda j: (0, j)),
        ],
        out_specs=pl.BlockSpec((rows, tn), lambda j: (0, j)),
        out_shape=jax.ShapeDtypeStruct((rows, n), F32),
        compiler_params=pltpu.CompilerParams(vmem_limit_bytes=VMEM_LIMIT),
        name="ada",
    )(c_all, w_ada, b_ada)


def _rope_body(inv_ref, cos_ref, sin_ref, *, pos0, tl, half):
    row = lax.broadcasted_iota(I32, (tl, LANES), 0) + pl.program_id(0) * tl
    ang = (row.astype(F32) + pos0) * inv_ref[...]
    lane = lax.broadcasted_iota(I32, (tl, LANES), 1)
    s = jnp.sin(ang)
    cos_ref[...] = jnp.cos(ang)
    sin_ref[...] = jnp.where(lane < half, -s, s)


def _rope_tables(inv2, rows, pos0):
    tl = min(rows, 256)
    return pl.pallas_call(
        functools.partial(_rope_body, pos0=float(pos0), tl=tl, half=LANES // 2),
        grid=(rows // tl,),
        in_specs=[pl.BlockSpec((1, LANES), lambda i: (0, 0))],
        out_specs=[pl.BlockSpec((tl, LANES), lambda i: (i, 0))] * 2,
        out_shape=[jax.ShapeDtypeStruct((rows, LANES), F32)] * 2,
        name="rope",
    )(inv2)


def _modulated_rmsnorm(x, g, sc, sh):
    ms = jnp.mean(x * x, axis=-1, keepdims=True)
    h = x * lax.rsqrt(ms + EPS) * g
    return h * (1.0 + sc) + sh


def _proj_body(x_ref, sh_ref, sc_ref, g_ref, w_ref, cos_ref, sin_ref,
               glu_ref, q_ref, k_ref, v_ref, sg_ref, *, dc, dr, hd, rope_rows):
    hb = _modulated_rmsnorm(x_ref[...], g_ref[...], sc_ref[...], sh_ref[...]).astype(BF16)

    def proj(lo, n):
        return jnp.dot(hb, w_ref[:, lo:lo + n], preferred_element_type=F32)

    glu_ref[...] = proj(0, dc) * _sigmoid(proj(dc, dc))
    cos = cos_ref[...] if rope_rows else cos_ref[0:1, :]
    sin = sin_ref[...] if rope_rows else sin_ref[0:1, :]
    for ref, lo, scale in ((q_ref, 2 * dc, hd ** -0.5), (k_ref, 2 * dc + dr, None)):
        t = proj(lo, dr)
        for hh in range(dr // hd):
            th = t[:, hh * hd:(hh + 1) * hd]
            r = th * cos + pltpu.roll(th, hd // 2, 1) * sin
            if scale is not None:
                r = r * scale
            ref[:, hh * hd:(hh + 1) * hd] = r.astype(BF16)
    v_ref[...] = proj(2 * dc + 2 * dr, dr).astype(BF16)
    sg_ref[...] = _silu(proj(2 * dc + 3 * dr, dr)).astype(BF16)


def _proj(x, mod, g_mix, w_in_b, cos2, sin2, *, dc, dr, hd, per_row_mod, b0=0, nb=None):
    nb_all, length, d = x.shape
    nb = nb_all if nb is None else nb
    tl = min(length, 512)
    x2 = x.reshape(nb_all * length, d)
    nl = length // tl
    if per_row_mod:
        mod_spec = lambda j: pl.BlockSpec((tl, d), lambda b, l: (l, j))
        rope_spec = pl.BlockSpec((SUBLANES, LANES), lambda b, l: (0, 0))
    else:
        mod_spec = lambda j: pl.BlockSpec((None, None, 1, d), lambda b, l: (b + b0, j, 0, 0))
        rope_spec = pl.BlockSpec((tl, LANES), lambda b, l: (l, 0))
    row_spec = lambda w: pl.BlockSpec((tl, w), lambda b, l: (b * nl + l, 0))
    t = nb * length
    outs = pl.pallas_call(
        functools.partial(_proj_body, dc=dc, dr=dr, hd=hd, rope_rows=not per_row_mod),
        grid=(nb, nl),
        in_specs=[
            pl.BlockSpec((tl, d), lambda b, l: ((b + b0) * nl + l, 0)), mod_spec(0), mod_spec(1),
            pl.BlockSpec((1, d), lambda b, l: (0, 0)),
            pl.BlockSpec(w_in_b.shape, lambda b, l: (0, 0)),
            rope_spec, rope_spec,
        ],
        out_specs=[row_spec(dc), row_spec(dr), row_spec(dr), row_spec(dr), row_spec(dr)],
        out_shape=[
            jax.ShapeDtypeStruct((t, dc), F32),
            jax.ShapeDtypeStruct((t, dr), BF16),
            jax.ShapeDtypeStruct((t, dr), BF16),
            jax.ShapeDtypeStruct((t, dr), BF16),
            jax.ShapeDtypeStruct((t, dr), BF16),
        ],
        compiler_params=pltpu.CompilerParams(vmem_limit_bytes=VMEM_LIMIT),
        name="proj",
    )(x2, mod, mod, g_mix, w_in_b, cos2, sin2)
    return outs


def _layernorm_silu(c, g, b):
    mu = jnp.mean(c, axis=-1, keepdims=True)
    d = c - mu
    var = jnp.mean(d * d, axis=-1, keepdims=True)
    return _silu(d * lax.rsqrt(var + EPS) * g + b)


def _groupnorm(o, g, b):
    mu = jnp.mean(o, axis=-1, keepdims=True)
    d = o - mu
    var = jnp.mean(d * d, axis=-1, keepdims=True)
    return d * lax.rsqrt(var + EPS) * g + b


def _mix_body(glu_ref, q_ref, k_ref, v_ref, sg_ref, cw_ref, cb_ref, lng_ref, lnb_ref, rg_ref, rb_ref,
              mask_ref, qd_ref, kd_ref, cd_ref, cat_ref, st_ref, buf, cscr, *, tl, dc, hd, nh, chunk):
    nslab = dc // LANES

    @pl.when(pl.program_id(1) == 0)
    def _():
        buf[:, 0:CONV_PAD, :] = jnp.zeros((nslab, CONV_PAD, LANES), F32)
        st_ref[...] = jnp.zeros(st_ref.shape, F32)

    for j in range(nslab):
        buf[j, CONV_PAD:CONV_PAD + tl, :] = glu_ref[:, LANES * j:LANES * (j + 1)]
    first = CONV_PAD - CONV_BUF
    rows_per_iter = 8 * SUBLANES
    for j in range(nslab):
        cols = slice(LANES * j, LANES * (j + 1))
        wv = [jnp.broadcast_to(cw_ref[t:t + 1, cols], (SUBLANES, LANES)) for t in range(CONV_WIDTH)]
        bias = jnp.broadcast_to(cb_ref[0:1, cols], (SUBLANES, LANES))

        def body(r, carry, j=j, cols=cols, wv=wv, bias=bias):
            base = pl.multiple_of(r * rows_per_iter, rows_per_iter)
            for u in range(rows_per_iter // SUBLANES):
                acc = bias
                for t in range(CONV_WIDTH):
                    acc = acc + wv[t] * buf[j, pl.ds(base + (u * SUBLANES + first + t), SUBLANES), :]
                cscr[pl.ds(base + u * SUBLANES, SUBLANES), cols] = acc
            return carry

        lax.fori_loop(0, tl // rows_per_iter, body, 0)
    for j in range(nslab):
        buf[j, 0:CONV_PAD, :] = buf[j, tl:tl + CONV_PAD, :]
    cat_ref[:, 0:dc] = _layernorm_silu(cscr[...], lng_ref[...], lnb_ref[...]).astype(BF16)

    nt = (((1,), (1,)), ((), ()))
    tn = (((0,), (0,)), ((), ()))
    for c in range(tl // chunk):
        rows = slice(c * chunk, (c + 1) * chunk)
        for hh in range(nh):
            cols = slice(hh * hd, (hh + 1) * hd)
            qh = q_ref[rows, cols]
            kh = k_ref[rows, cols]
            vh = v_ref[rows, cols]
            s = st_ref[0, hh]
            scores = lax.dot_general(qh, kh, nt, preferred_element_type=F32) * mask_ref[hh]
            inner = jnp.dot(scores.astype(BF16), vh, preferred_element_type=F32)
            qd = (qh.astype(F32) * qd_ref[hh]).astype(BF16)
            cross = jnp.dot(qd, s.astype(BF16), preferred_element_type=F32)
            kd = (kh.astype(F32) * kd_ref[hh]).astype(BF16)
            st_ref[0, hh] = cd_ref[hh] * s + lax.dot_general(kd, vh, tn, preferred_element_type=F32)
            o = _groupnorm(inner + cross, rg_ref[0:1, cols], rb_ref[0:1, cols])
            cat_ref[rows, dc + hh * hd:dc + (hh + 1) * hd] = (o * sg_ref[rows, cols].astype(F32)).astype(BF16)


def _mix(glu, q, k, v, sg, conv_w, conv_b, ln_g, ln_b, rg, rb, tables, *, nb, length, dc, dr, hd):
    nh = dr // hd
    chunk = math.gcd(length, RET_CHUNK)
    tl = min(length, 256)
    nl = length // tl
    mask, qd, kd, cd = tables
    row_spec = lambda w: pl.BlockSpec((tl, w), lambda b, l: (b * nl + l, 0))
    full = lambda a: pl.BlockSpec(a.shape, lambda b, l: (0,) * a.ndim)
    cat, st = pl.pallas_call(
        functools.partial(_mix_body, tl=tl, dc=dc, hd=hd, nh=nh, chunk=chunk),
        grid=(nb, nl),
        in_specs=[row_spec(dc), row_spec(dr), row_spec(dr), row_spec(dr), row_spec(dr),
                  full(conv_w), full(conv_b), full(ln_g), full(ln_b), full(rg), full(rb),
                  full(mask), full(qd), full(kd), full(cd)],
        out_specs=[row_spec(dc + dr), pl.BlockSpec((1, nh, hd, hd), lambda b, l: (b, 0, 0, 0))],
        out_shape=[jax.ShapeDtypeStruct((nb * length, dc + dr), BF16),
                   jax.ShapeDtypeStruct((nb, nh, hd, hd), F32)],
        scratch_shapes=[pltpu.VMEM((dc // LANES, tl + CONV_PAD, LANES), F32),
                        pltpu.VMEM((tl, dc), F32)],
        compiler_params=pltpu.CompilerParams(dimension_semantics=("arbitrary", "arbitrary")),
        name="mix",
    )(glu, q, k, v, sg, conv_w, conv_b, ln_g, ln_b, rg, rb, mask, qd, kd, cd)
    return cat, st


def _mix1_body(glu_ref, q_ref, k_ref, v_ref, sg_ref, sc_ref, s0_ref, cw_ref, cb_ref, lng_ref, lnb_ref,
               rg_ref, rb_ref, gam_ref, cat_ref, st_ref, nc_ref, cscr, oscr, qf, kf, vf, *, tb, dc, hd, nh):
    w_hist = cw_ref[0:CONV_BUF, :]
    w_last = cw_ref[CONV_BUF:CONV_WIDTH, :]
    for bb in range(tb):
        hist = jnp.sum(sc_ref[bb] * w_hist, axis=0, keepdims=True)
        cscr[bb:bb + 1, :] = hist + glu_ref[bb:bb + 1, :] * w_last + cb_ref[...]
        nc_ref[bb, 0:CONV_BUF - 1, :] = sc_ref[bb, 1:CONV_BUF, :]
        nc_ref[bb, CONV_BUF - 1:CONV_BUF, :] = glu_ref[bb:bb + 1, :]
    cat_ref[:, 0:dc] = _layernorm_silu(cscr[...], lng_ref[...], lnb_ref[...]).astype(BF16)

    tn = (((0,), (0,)), ((), ()))
    rowid = lax.broadcasted_iota(I32, (tb, hd), 0)
    qf[...] = q_ref[...].astype(F32)
    kf[...] = k_ref[...].astype(F32)
    vf[...] = v_ref[...].astype(F32)
    for hh in range(nh):
        cols = slice(hh * hd, (hh + 1) * hd)
        qa = q_ref[:, cols]
        ka = k_ref[:, cols]
        gam = gam_ref[hh, 0:1, :]
        for bb in range(tb):
            onehot = (rowid == bb).astype(BF16)
            qcol = lax.dot_general(qa, onehot, tn, preferred_element_type=F32)
            kcol = lax.dot_general(ka, onehot, tn, preferred_element_type=F32)
            s0 = s0_ref[bb, hh]
            qrow = qf[bb:bb + 1, cols]
            krow = kf[bb:bb + 1, cols]
            vrow = vf[bb:bb + 1, cols]
            qk = jnp.sum(qrow * krow, axis=-1, keepdims=True)
            cross = gam * jnp.sum(qcol * s0, axis=0, keepdims=True)
            st_ref[bb, hh] = gam * s0 + kcol * vrow
            oscr[bb:bb + 1, cols] = qk * vrow + cross
    for hh in range(nh):
        cols = slice(hh * hd, (hh + 1) * hd)
        o = _groupnorm(oscr[:, cols], rg_ref[0:1, cols], rb_ref[0:1, cols])
        cat_ref[:, dc + hh * hd:dc + (hh + 1) * hd] = (o * sg_ref[:, cols].astype(F32)).astype(BF16)


def _mix1(glu, q, k, v, sg, state_conv, state_ret, conv_w, conv_b, ln_g, ln_b, rg, rb, gam, *, dc, dr, hd):
    nb = glu.shape[0]
    nh = dr // hd
    tb = 16
    row_spec = lambda w: pl.BlockSpec((tb, w), lambda i: (i, 0))
    full = lambda a: pl.BlockSpec(a.shape, lambda i: (0,) * a.ndim)
    st_spec = pl.BlockSpec((None, tb, nh, hd, hd), lambda i: (0, i, 0, 0, 0))
    conv_spec = pl.BlockSpec((None, tb, CONV_BUF, dc), lambda i: (0, i, 0, 0))
    cat, st, new_conv = pl.pallas_call(
        functools.partial(_mix1_body, tb=tb, dc=dc, hd=hd, nh=nh),
        grid=(nb // tb,),
        in_specs=[row_spec(dc), row_spec(dr), row_spec(dr), row_spec(dr), row_spec(dr), conv_spec, st_spec,
                  full(conv_w), full(conv_b), full(ln_g), full(ln_b), full(rg), full(rb), full(gam)],
        out_specs=[row_spec(dc + dr), st_spec, conv_spec],
        out_shape=[jax.ShapeDtypeStruct((nb, dc + dr), BF16),
                   jax.ShapeDtypeStruct((1, nb, nh, hd, hd), F32),
                   jax.ShapeDtypeStruct((1, nb, CONV_BUF, dc), F32)],
        scratch_shapes=[pltpu.VMEM((tb, dc), F32)] + [pltpu.VMEM((tb, dr), F32)] * 4,
        compiler_params=pltpu.CompilerParams(vmem_limit_bytes=VMEM_LIMIT),
        name="mix1",
    )(glu, q, k, v, sg, state_conv, state_ret, conv_w, conv_b, ln_g, ln_b, rg, rb, gam)
    return cat, st, new_conv


def _post_body(cat_ref, x_ref, gtm_ref, scf_ref, shf_ref, gtf_ref, g_ref, wo_ref, wrh_ref, wrl_ref,
               ws1_ref, ws3_ref, ws2_ref, x2_ref, hp_ref, lg_ref):
    d = x_ref.shape[1]
    y = jnp.dot(cat_ref[...], wo_ref[...], preferred_element_type=F32)
    x1 = x_ref[...] + gtm_ref[...] * y
    h = _modulated_rmsnorm(x1, g_ref[...], scf_ref[...], shf_ref[...])
    hb = h.astype(BF16)
    _store_rows(hp_ref, _pack_halves(h[:, 0:d // 2], h[:, d // 2:d]))
    hl = (h - hb.astype(F32)).astype(BF16)
    nt = (((1,), (1,)), ((), ()))
    lg_ref[...] = (lax.dot_general(wrh_ref[...], hb, nt, preferred_element_type=F32)
                   + lax.dot_general(wrh_ref[...], hl, nt, preferred_element_type=F32)
                   + lax.dot_general(wrl_ref[...], hb, nt, preferred_element_type=F32))
    s1 = jnp.dot(hb, ws1_ref[...], preferred_element_type=F32)
    s3 = jnp.dot(hb, ws3_ref[...], preferred_element_type=F32)
    shared = jnp.dot((_silu(s1) * s3).astype(BF16), ws2_ref[...], preferred_element_type=F32)
    x2_ref[...] = x1 + gtf_ref[...] * shared


def _post(cat, x, mod, g_ffn, wo_b, wrh, wrl, ws1_b, ws3_b, ws2_b, *, per_row_mod, b0=0, nb=None):
    nb_all, length, d = x.shape
    nb = nb_all if nb is None else nb
    tl = min(length, 1024)
    nl = length // tl
    t = nb * length
    ne = wrh.shape[0]
    x2d = x.reshape(nb_all * length, d)
    if per_row_mod:
        mod_spec = lambda j: pl.BlockSpec((tl, d), lambda b, l: (l, j))
    else:
        mod_spec = lambda j: pl.BlockSpec((None, None, 1, d), lambda b, l: (b + b0, j, 0, 0))
    row_spec = lambda w: pl.BlockSpec((tl, w), lambda b, l: (b * nl + l, 0))
    full = lambda a: pl.BlockSpec(a.shape, lambda b, l: (0,) * a.ndim)
    return pl.pallas_call(
        _post_body,
        grid=(nb, nl),
        in_specs=[row_spec(d), pl.BlockSpec((tl, d), lambda b, l: ((b + b0) * nl + l, 0)),
                  mod_spec(2), mod_spec(4), mod_spec(3), mod_spec(5),
                  full(g_ffn), full(wo_b), full(wrh), full(wrl), full(ws1_b), full(ws3_b), full(ws2_b)],
        out_specs=[row_spec(d),
                   pl.BlockSpec((tl * ROW_WORDS, LANES), lambda b, l: (b * nl + l, 0)),
                   pl.BlockSpec((ne, tl), lambda b, l: (0, b * nl + l))],
        out_shape=[jax.ShapeDtypeStruct((t, d), F32),
                   jax.ShapeDtypeStruct((t * ROW_WORDS, LANES), U32),
                   jax.ShapeDtypeStruct((ne, t), F32)],
        compiler_params=pltpu.CompilerParams(vmem_limit_bytes=VMEM_LIMIT),
        name="post",
    )(cat, x2d, mod, mod, mod, mod, g_ffn, wo_b, wrh, wrl, ws1_b, ws3_b, ws2_b)


def _first_max(x, idx, sentinel):
    m = jnp.max(x, axis=0, keepdims=True)
    f = jnp.min(jnp.where(x == m, idx, sentinel), axis=0, keepdims=True)
    return m, f


def _route_body(lg_ref, bias_ref, e_ref, w_ref, r_ref, cnt_ref, cnt_scr, *, tr, ne, ng, topk, topg):
    @pl.when(pl.program_id(0) == 0)
    def _():
        cnt_scr[...] = jnp.zeros(cnt_scr.shape, F32)

    per = ne // ng
    neg = -jnp.inf
    scores = _sigmoid(lg_ref[...])
    sel = scores + bias_ref[...]
    sub = lax.broadcasted_iota(I32, (per, tr), 0)
    gs = []
    for g in range(ng):
        s_g = sel[g * per:(g + 1) * per, :]
        m1, f1 = _first_max(s_g, sub, per)
        m2 = jnp.max(jnp.where(sub == f1, neg, s_g), axis=0, keepdims=True)
        gs.append(m1 + m2)
    gsc = jnp.concatenate(gs, axis=0)
    gi = lax.broadcasted_iota(I32, (ng, tr), 0)
    keep = jnp.zeros((ng, tr), F32)
    for _ in range(topg):
        _, f = _first_max(gsc, gi, ng)
        pick = gi == f
        keep = jnp.where(pick, 1.0, keep)
        gsc = jnp.where(pick, neg, gsc)
    work = jnp.concatenate(
        [jnp.where(keep[g:g + 1, :] > 0.5, sel[g * per:(g + 1) * per, :], neg) for g in range(ng)], axis=0)
    ei = lax.broadcasted_iota(I32, (ne, tr), 0)
    picks, es, ws = [], [], []
    for _ in range(topk):
        _, f = _first_max(work, ei, ne)
        pick = ei == f
        picks.append(pick)
        es.append(f)
        ws.append(jnp.sum(jnp.where(pick, scores, 0.0), axis=0, keepdims=True))
        work = jnp.where(pick, neg, work)
    wsum = ws[0]
    for w in ws[1:]:
        wsum = wsum + w
    scale = ROUTED_SCALE / wsum
    chosen = picks[0]
    for p in picks[1:]:
        chosen = jnp.logical_or(chosen, p)
    chosen_f = chosen.astype(F32)
    t_row = lax.broadcasted_iota(I32, (tr, tr), 0)
    t_col = lax.broadcasted_iota(I32, (tr, tr), 1)
    before = (t_row < t_col).astype(BF16)
    prior = cnt_scr[:, 0:1] + jnp.dot(chosen_f.astype(BF16), before, preferred_element_type=F32)
    rs = [jnp.sum(jnp.where(p, prior, 0.0), axis=0, keepdims=True).astype(I32) for p in picks]
    pad_i = jnp.zeros((SUBLANES - topk, tr), I32)
    pad_f = jnp.zeros((SUBLANES - topk, tr), F32)
    e_ref[...] = jnp.concatenate(es + [pad_i], axis=0)
    w_ref[...] = jnp.concatenate([w * scale for w in ws] + [pad_f], axis=0)
    r_ref[...] = jnp.concatenate(rs + [pad_i], axis=0)
    total = cnt_scr[:, 0:1] + jnp.sum(chosen_f, axis=1, keepdims=True)
    cnt_scr[...] = jnp.broadcast_to(total, cnt_scr.shape)
    cnt_ref[...] = jnp.broadcast_to(total, cnt_ref.shape)


def _route_tile(t):
    return max(m for m in range(LANES, ROUTE_TILE + 1, LANES) if t % m == 0)


def _route(logits_t, bias_col):
    ne, t = logits_t.shape
    tr = _route_tile(t)
    tok = lambda dt: jax.ShapeDtypeStruct((SUBLANES, t), dt)
    tok_spec = pl.BlockSpec((SUBLANES, tr), lambda i: (0, i))
    return pl.pallas_call(
        functools.partial(_route_body, tr=tr, ne=ne, ng=N_GROUPS, topk=TOP_K, topg=TOPK_GROUPS),
        grid=(t // tr,),
        in_specs=[pl.BlockSpec((ne, tr), lambda i: (0, i)), pl.BlockSpec((ne, 1), lambda i: (0, 0))],
        out_specs=[tok_spec, tok_spec, tok_spec, pl.BlockSpec((ne, LANES), lambda i: (0, 0))],
        out_shape=[tok(I32), tok(F32), tok(I32), jax.ShapeDtypeStruct((ne, LANES), F32)],
        scratch_shapes=[pltpu.VMEM((ne, LANES), F32)],
        compiler_params=pltpu.CompilerParams(dimension_semantics=("arbitrary",)),
        name="route",
    )(logits_t, bias_col)


def _dest_body(cnt_ref, e_ref, r_ref, d_ref, ps, *, ne, tr):
    shift = EXPERT_ROWS.bit_length() - 1

    @pl.when(pl.program_id(0) == 0)
    def _():
        def step(j, start):
            ps[j] = start
            return start + lax.shift_left(lax.shift_right_logical(cnt_ref[j] + (EXPERT_ROWS - 1), shift), shift)

        lax.fori_loop(0, ne, step, jnp.int32(0))

    e = e_ref[...]
    base = jnp.zeros(e.shape, I32)
    for j in range(ne):
        base = jnp.where(e == j, ps[j], base)
    dest = base + r_ref[...]
    for m in range(tr // TOKEN_TILE):
        d_ref[SUBLANES * m:SUBLANES * (m + 1), :] = dest[:, TOKEN_TILE * m:TOKEN_TILE * (m + 1)]


def _dest_rows(counts, eidx, rank):
    rows, t = eidx.shape
    ne = counts.shape[0]
    tr = _route_tile(t)
    spec = pl.BlockSpec((rows, tr), lambda i, cnt: (0, i))
    return pl.pallas_call(
        functools.partial(_dest_body, ne=ne, tr=tr),
        grid_spec=pltpu.PrefetchScalarGridSpec(
            num_scalar_prefetch=1, grid=(t // tr,), in_specs=[spec, spec],
            out_specs=pl.BlockSpec((tr // TOKEN_TILE * rows, TOKEN_TILE), lambda i, cnt: (i, 0)),
            scratch_shapes=[pltpu.SMEM((ne,), I32)]),
        out_shape=jax.ShapeDtypeStruct((t // TOKEN_TILE * rows, TOKEN_TILE), I32),
        compiler_params=pltpu.CompilerParams(dimension_semantics=("arbitrary",)),
        name="dest",
    )(counts, eidx, rank)


def _sc_mesh():
    return plsc.VectorSubcoreMesh(core_axis_name="c", subcore_axis_name="s")


def _sc_worker_id():
    return lax.axis_index("s") * SC_CORES + lax.axis_index("c")


def _index_block(dest_ref, chunk, width):
    per_tile = TOKEN_TILE // width
    return dest_ref.at[chunk // per_tile, :, pl.ds((chunk % per_tile) * width, width)]


def _dispatch(hps, dest3, n_rows):
    w = SCATTER_ROWS
    width = hps[0].shape[1]
    bounds = [0]
    for h in hps:
        bounds.append(bounds[-1] + h.shape[0] // w)
    nch = bounds[-1]
    nsrc = len(hps)

    @functools.partial(
        pl.kernel, mesh=_sc_mesh(),
        out_type=jax.ShapeDtypeStruct((n_rows, width), U32),
        scratch_types=[pltpu.VMEM((2, SUBLANES, w), I32), pltpu.VMEM((2, w, width), U32),
                       pltpu.SemaphoreType.DMA, pltpu.SemaphoreType.DMA, pltpu.SemaphoreType.DMA],
        compiler_params=pltpu.CompilerParams(use_tc_tiling_on_sc=False),
        name="dispatch",
    )
    def run(*refs):
        src_refs, dest_ref, xs_ref = refs[:nsrc], refs[nsrc], refs[nsrc + 1]
        idx_v, rows_v = refs[nsrc + 2:nsrc + 4]
        sem_load = refs[nsrc + 4:nsrc + 6]
        sem_scatter = refs[nsrc + 6]
        wid = _sc_worker_id()

        def start_loads(c, slot):
            pltpu.async_copy(_index_block(dest_ref, c, w), idx_v.at[slot], sem_load[slot])
            for i, src in enumerate(src_refs):
                @pl.when(jnp.logical_and(c >= bounds[i], c < bounds[i + 1]))
                def _(src=src, lo=bounds[i]):
                    pltpu.async_copy(src.at[pl.ds((c - lo) * w, w)], rows_v.at[slot], sem_load[slot])

        def wait_loads(slot):
            pltpu.make_async_copy(_index_block(dest_ref, 0, w), idx_v.at[slot], sem_load[slot]).wait()
            pltpu.make_async_copy(src_refs[0].at[pl.ds(0, w)], rows_v.at[slot], sem_load[slot]).wait()

        @pl.when(wid < nch)
        def _():
            start_loads(wid, 0)

        @pl.loop(0, pl.cdiv(pl.cdiv(nch, SC_WORKERS), 2))
        def _(rr):
            for slot in range(2):
                c = (rr * 2 + slot) * SC_WORKERS + wid

                @pl.when(c < nch)
                def _(c=c, slot=slot):
                    wait_loads(slot)

                    @pl.when(c + SC_WORKERS < nch)
                    def _():
                        start_loads(c + SC_WORKERS, 1 - slot)

                    copies = [pltpu.async_copy(rows_v.at[slot], xs_ref.at[idx_v.at[slot, k]], sem_scatter)
                              for k in range(TOP_K)]
                    for cp in copies:
                        cp.wait()

    return run(*hps, dest3)


def _undispatch(ys, dest3, n_tokens, after=None):
    w = GATHER_ROWS
    width = ys.shape[1]
    nch = n_tokens // w

    @functools.partial(
        pl.kernel, mesh=_sc_mesh(),
        out_type=jax.ShapeDtypeStruct((TOP_K, n_tokens, width), U32),
        scratch_types=[pltpu.VMEM((2, SUBLANES, w), I32), pltpu.VMEM((2, TOP_K, w, width), U32),
                       pltpu.SemaphoreType.DMA, pltpu.SemaphoreType.DMA, pltpu.SemaphoreType.DMA],
        compiler_params=pltpu.CompilerParams(use_tc_tiling_on_sc=False),
        name="undispatch",
    )
    def run(*refs):
        ys_ref, dest_ref = refs[:2]
        z_ref, idx_v, bufs = refs[-6:-3]
        sem_gather = refs[-3:-1]
        sem_store = refs[-1]
        wid = _sc_worker_id()

        def start_gathers(c, slot):
            pltpu.sync_copy(_index_block(dest_ref, c, w), idx_v.at[slot])
            for k in range(TOP_K):
                pltpu.async_copy(ys_ref.at[idx_v.at[slot, k]], bufs.at[slot, k], sem_gather[slot])

        def wait_gathers(slot):
            for k in range(TOP_K):
                pltpu.make_async_copy(ys_ref.at[idx_v.at[slot, k]], bufs.at[slot, k], sem_gather[slot]).wait()

        @pl.when(wid < nch)
        def _():
            start_gathers(wid, 0)

        @pl.loop(0, pl.cdiv(pl.cdiv(nch, SC_WORKERS), 2))
        def _(rr):
            for slot in range(2):
                c = (rr * 2 + slot) * SC_WORKERS + wid

                @pl.when(c < nch)
                def _(c=c, slot=slot):
                    @pl.when(c + SC_WORKERS < nch)
                    def _():
                        start_gathers(c + SC_WORKERS, 1 - slot)

                    wait_gathers(slot)
                    stores = [pltpu.async_copy(bufs.at[slot, k], z_ref.at[k, pl.ds(c * w, w)], sem_store)
                              for k in range(TOP_K)]
                    for cp in stores:
                        cp.wait()

    return run(ys, dest3) if after is None else run(ys, dest3, after)


PART_SHIFT = 24


def _expert_body(*refs, tm, ne, nparts):
    cnt_refs = refs[:nparts]
    xs_refs = refs[nparts:2 * nparts]
    w1_ref, w3_ref, w2_ref = refs[2 * nparts:2 * nparts + 3]
    ys_refs = refs[2 * nparts + 3:3 * nparts + 3]
    (w1f, w3f, w2f, w1s, w3s, w2s, xbuf, ybuf, xlo, xhi, sched, sem_x, sem_y, sem_w) = refs[3 * nparts + 3:]
    blk_words = tm * ROW_WORDS
    half = ROW_WORDS * LANES
    shift = tm.bit_length() - 1

    def n_blocks_of(p, e):
        return lax.shift_right_logical(cnt_refs[p][e] + (tm - 1), shift)

    def n_all(e):
        n = n_blocks_of(0, e)
        for p in range(1, nparts):
            n = n + n_blocks_of(p, e)
        return n

    def next_nonempty(e):
        return lax.while_loop(
            lambda c: jnp.logical_and(c < ne, n_all(jnp.minimum(c, ne - 1)) == 0), lambda c: c + 1, e)

    def plan(e, carry):
        i, starts = carry[0], list(carry[1:])
        for p in range(nparts):
            n = n_blocks_of(p, e)

            def put(j, c, p=p, i=i, start=starts[p]):
                sched[i + j] = (start + j) + (p << PART_SHIFT)
                return c

            lax.fori_loop(0, n, put, 0)
            i = i + n
            starts[p] = starts[p] + n
        return (i, *starts)

    nu = lax.fori_loop(0, ne, plan, (jnp.int32(0),) * (nparts + 1))[0]

    def rows_of(code):
        blk = code & ((1 << PART_SHIFT) - 1)
        return pl.ds(pl.multiple_of(blk * blk_words, blk_words), blk_words)

    def x_start(i, slot):
        code = sched[i]
        for p in range(nparts):
            @pl.when(lax.shift_right_logical(code, PART_SHIFT) == p)
            def _(p=p):
                pltpu.make_async_copy(xs_refs[p].at[rows_of(code), :], xbuf.at[slot], sem_x.at[slot]).start()

    def x_wait(slot):
        pltpu.make_async_copy(xs_refs[0].at[pl.ds(0, blk_words), :], xbuf.at[slot], sem_x.at[slot]).wait()

    def y_start(i, slot):
        code = sched[i]
        for p in range(nparts):
            @pl.when(lax.shift_right_logical(code, PART_SHIFT) == p)
            def _(p=p):
                pltpu.make_async_copy(ybuf.at[slot], ys_refs[p].at[rows_of(code), :], sem_y.at[slot]).start()

    def y_wait(slot):
        pltpu.make_async_copy(ybuf.at[slot], ys_refs[0].at[pl.ds(0, blk_words), :], sem_y.at[slot]).wait()

    def w_copies(e, ws):
        return [pltpu.make_async_copy(src.at[e], dst.at[ws], sem_w.at[ws])
                for src, dst in ((w1_ref, w1f), (w3_ref, w3f), (w2_ref, w2f))]

    for q in range(X_AHEAD):
        @pl.when(q < nu)
        def _(q=q):
            x_start(q, q)

    e_first = next_nonempty(jnp.int32(0))

    @pl.when(e_first < ne)
    def _():
        for cp in w_copies(e_first, 0):
            cp.start()

    def blocks(i, nblk):
        for b in range(nblk):
            x_wait((i + b) % X_SLOTS)
        for b in range(nblk):
            nxt = i + b + X_AHEAD

            @pl.when(nxt < nu)
            def _(nxt=nxt):
                x_start(nxt, nxt % X_SLOTS)

        for b in range(nblk):
            @pl.when(i + b >= Y_SLOTS)
            def _(b=b):
                y_wait((i + b) % Y_SLOTS)

        for b in range(nblk):
            xin = xbuf.at[(i + b) % X_SLOTS]
            for w in range(ROW_WORDS):
                lo, hi = _unpack_halves(_load_row_word(xin, w, tm))
                xlo[b, :, LANES * w:LANES * (w + 1)] = lo.astype(BF16)
                xhi[b, :, LANES * w:LANES * (w + 1)] = hi.astype(BF16)

        def up(b, wsc):
            return (jnp.dot(xlo[b], wsc[0:half, :], preferred_element_type=F32)
                    + jnp.dot(xhi[b], wsc[half:2 * half, :], preferred_element_type=F32))

        for b in range(nblk):
            hid = (_silu(up(b, w1s)) * up(b, w3s)).astype(BF16)
            y = jnp.dot(hid, w2s[...], preferred_element_type=F32)
            _store_rows(ybuf.at[(i + b) % Y_SLOTS], _pack_halves(y[:, 0:half], y[:, half:2 * half]))
        for b in range(nblk):
            y_start(i + b, (i + b) % Y_SLOTS)

    def per_expert(e, carry):
        i0, ws = carry
        n = n_all(e)

        @pl.when(n > 0)
        def _():
            for cp in w_copies(e, ws):
                cp.wait()
            w1s[...] = w1f[ws].astype(BF16)
            w3s[...] = w3f[ws].astype(BF16)
            w2s[...] = w2f[ws].astype(BF16)
            e_next = next_nonempty(e + 1)

            @pl.when(e_next < ne)
            def _():
                for cp in w_copies(e_next, 1 - ws):
                    cp.start()

            def pair(j, c):
                blocks(i0 + 2 * j, 2)
                return c

            lax.fori_loop(0, lax.shift_right_logical(n, 1), pair, 0)

            @pl.when(n % 2 == 1)
            def _():
                blocks(i0 + n - 1, 1)

        return i0 + n, jnp.where(n > 0, 1 - ws, ws)

    lax.fori_loop(0, ne, per_expert, (jnp.int32(0), jnp.int32(0)))

    for q in range(Y_SLOTS):
        @pl.when(nu > q)
        def _(q=q):
            y_wait((nu - 1 - q) % Y_SLOTS)


def _experts(counts, xss, w1e, w3e, w2e):
    tm = EXPERT_ROWS
    ne, d, de = w1e.shape
    half = ROW_WORDS * LANES
    nparts = len(xss)
    cap = sum(x.shape[0] // (tm * ROW_WORDS) for x in xss)
    assert cap < (1 << PART_SHIFT)
    anyspec = pl.BlockSpec(memory_space=pl.ANY)
    blk_buf = lambda n: pltpu.VMEM((n, tm * ROW_WORDS, LANES), U32)
    return pl.pallas_call(
        functools.partial(_expert_body, tm=tm, ne=ne, nparts=nparts),
        grid_spec=pltpu.PrefetchScalarGridSpec(
            num_scalar_prefetch=nparts,
            grid=(1,),
            in_specs=[anyspec] * (nparts + 3),
            out_specs=[anyspec] * nparts,
            scratch_shapes=[pltpu.VMEM((2, d, de), F32), pltpu.VMEM((2, d, de), F32), pltpu.VMEM((2, de, d), F32),
                            pltpu.VMEM((d, de), BF16), pltpu.VMEM((d, de), BF16), pltpu.VMEM((de, d), BF16),
                            blk_buf(X_SLOTS), blk_buf(Y_SLOTS),
                            pltpu.VMEM((2, tm, half), BF16), pltpu.VMEM((2, tm, half), BF16),
                            pltpu.SMEM((cap,), I32),
                            pltpu.SemaphoreType.DMA((X_SLOTS,)), pltpu.SemaphoreType.DMA((Y_SLOTS,)),
                            pltpu.SemaphoreType.DMA((2,))]),
        out_shape=[jax.ShapeDtypeStruct(x.shape, U32) for x in xss],
        compiler_params=pltpu.CompilerParams(dimension_semantics=("arbitrary",), vmem_limit_bytes=VMEM_LIMIT),
        name="experts",
    )(*counts, *xss, w1e, w3e, w2e)


def _combine_body(z_ref, x2_ref, gtf_ref, wt_ref, gfin_ref, y_ref, xo, *, td, topk):
    half = ROW_WORDS * LANES
    wt = wt_ref[...].T
    ws = [wt[:, k:k + 1] for k in range(topk)]
    sq = jnp.zeros((td, 1), F32)
    for j in range(ROW_WORDS):
        acc_lo = jnp.zeros((td, LANES), F32)
        acc_hi = jnp.zeros((td, LANES), F32)
        for k in range(topk):
            lo, hi = _unpack_halves(_load_row_word(z_ref.at[k], j, td))
            acc_lo = acc_lo + ws[k] * lo
            acc_hi = acc_hi + ws[k] * hi
        for base, acc in ((0, acc_lo), (half, acc_hi)):
            cols = slice(base + LANES * j, base + LANES * (j + 1))
            x = x2_ref[:, cols] + gtf_ref[:, cols] * acc
            xo[:, cols] = x
            sq = sq + jnp.sum(x * x, axis=-1, keepdims=True)
    rs = lax.rsqrt(sq / (2 * half) + EPS)
    y_ref[...] = xo[...] * rs * gfin_ref[...]


def _combine_body_into(z_ref, x2_ref, gtf_ref, wt_ref, gfin_ref, prev_ref, y_ref, xo, *, td, topk):
    del prev_ref
    _combine_body(z_ref, x2_ref, gtf_ref, wt_ref, gfin_ref, y_ref, xo, td=td, topk=topk)


def _combine(z, token0, wts, x2, mod, g_final, *, rows_per_mod, per_row_mod, b0=0, out_rows=None, into=None):
    t, d = x2.shape
    td = min(t, COMBINE_TILE)
    tile0 = token0 // td
    out_rows = t if out_rows is None else out_rows
    if per_row_mod:
        gtf_spec = pl.BlockSpec((td, d), lambda i: (i, 5))
        out0 = 0
    else:
        tiles_per_mod = rows_per_mod // td
        gtf_spec = pl.BlockSpec((None, None, 1, d), lambda i: (i // tiles_per_mod + b0, 5, 0, 0))
        out0 = b0 * tiles_per_mod
    in_specs = [pl.BlockSpec((TOP_K, td * ROW_WORDS, LANES), lambda i: (0, i + tile0, 0)),
                pl.BlockSpec((td, d), lambda i: (i, 0)),
                gtf_spec,
                pl.BlockSpec((SUBLANES, td), lambda i: (0, i + tile0)),
                pl.BlockSpec((1, d), lambda i: (0, 0))]
    args = [z, x2, mod, wts, g_final]
    body, aliases = _combine_body, {}
    if into is not None:
        in_specs.append(pl.BlockSpec(memory_space=pl.ANY))
        args.append(into)
        body, aliases = _combine_body_into, {len(args) - 1: 0}
    return pl.pallas_call(
        functools.partial(body, td=td, topk=TOP_K),
        grid=(t // td,),
        in_specs=in_specs,
        out_specs=pl.BlockSpec((td, d), lambda i: (i + out0, 0)),
        scratch_shapes=[pltpu.VMEM((td, d), F32)],
        out_shape=jax.ShapeDtypeStruct((out_rows, d), F32),
        input_output_aliases=aliases,
        compiler_params=pltpu.CompilerParams(vmem_limit_bytes=VMEM_LIMIT),
        name="combine",
    )(*args)


def _log_gamma(nh):
    return np.log(1.0 - 2.0 ** (-5.0 - np.arange(nh, dtype=np.float32))).astype(np.float32)


def _retention_tables(length, nh, hd):
    c = math.gcd(length, RET_CHUNK)
    log_g = _log_gamma(nh)
    idx = np.arange(c, dtype=np.float32)
    rel = idx[:, None] - idx[None, :]
    mask = np.where(rel >= 0, np.exp(log_g[:, None, None] * np.maximum(rel, 0.0)), 0.0).astype(np.float32)
    q_decay = np.exp(log_g[None, :] * (idx[:, None] + 1.0)).astype(np.float32)
    k_decay = np.exp(log_g[None, :] * (c - 1.0 - idx[:, None])).astype(np.float32)
    chunk_decay = np.exp(log_g * np.float32(c)).astype(np.float32)
    qd = np.broadcast_to(q_decay.T[:, :, None], (nh, c, hd))
    kd = np.broadcast_to(k_decay.T[:, :, None], (nh, c, hd))
    cd = np.broadcast_to(chunk_decay[:, None, None], (nh, hd, hd))
    return tuple(jnp.asarray(t) for t in (mask, qd, kd, cd))


def kernel(x_prompt, x_sample, c_prompt, c_sample, state_conv, state_ret, w_ada, b_ada, g_mix, g_ffn, w_in,
           conv_w, conv_b, conv_norm_g, conv_norm_b, ret_norm_g, ret_norm_b, w_out, w_router, router_bias,
           w1, w3, w2, ws1, ws3, ws2, g_final):
    depth = w_ada.shape[0]
    assert depth == 1, "single-layer trunk"
    bp, lp, d = x_prompt.shape
    bs, ls, _ = x_sample.shape
    assert ls == 1
    dc = conv_w.shape[2]
    dr = ret_norm_g.shape[1]
    nh = RET_HEADS
    hd = dr // nh
    assert hd == LANES and lp % 256 == 0 and bs % TOKEN_TILE == 0 and d // 2 == ROW_WORDS * LANES
    ne = w_router.shape[2]
    row = lambda a: a.reshape(1, -1)

    mod = _ada(jnp.concatenate([c_prompt, c_sample], axis=0), w_ada[0], row(b_ada[0]))
    mod_p = mod[:bp].reshape(bp, 6, 1, d)
    mod_s = mod[bp:]

    half = hd // 2
    inv = (np.float32(ROPE_BASE) ** (-np.arange(half, dtype=np.float32) / np.float32(half))).astype(np.float32)
    inv2 = jnp.asarray(np.concatenate([inv, inv]).reshape(1, hd))
    cos_p, sin_p = _rope_tables(inv2, lp, 0)
    cos_s, sin_s = _rope_tables(inv2, SUBLANES, PAST_LEN)

    w_in_b = w_in[0].astype(BF16)
    wo_b = w_out[0].astype(BF16)
    wr_t = w_router[0].T
    wrh = wr_t.astype(BF16)
    wrl = (wr_t - wrh.astype(F32)).astype(BF16)
    ws1_b, ws3_b, ws2_b = ws1[0].astype(BF16), ws3[0].astype(BF16), ws2[0].astype(BF16)
    dims = dict(dc=dc, dr=dr, hd=hd)

    tables = _retention_tables(lp, nh, hd)
    gam = jnp.asarray(np.broadcast_to(np.exp(_log_gamma(nh))[:, None, None], (nh, SUBLANES, hd)))
    norm_rows = (row(conv_b[0]), row(conv_norm_g[0]), row(conv_norm_b[0]), row(ret_norm_g[0]), row(ret_norm_b[0]))
    post_w = (row(g_ffn[0]), wo_b, wrh, wrl, ws1_b, ws3_b, ws2_b)
    bias_col = router_bias[0].reshape(ne, 1)
    hw = d // 2
    tm = EXPERT_ROWS

    def pre_prompt(b0, nb):
        glu, q, k, v, sg = _proj(x_prompt, mod_p, row(g_mix[0]), w_in_b, cos_p, sin_p,
                                 per_row_mod=False, b0=b0, nb=nb, **dims)
        cat, ret = _mix(glu, q, k, v, sg, conv_w[0], *norm_rows, tables, nb=nb, length=lp, **dims)
        x2, hp, lg = _post(cat, x_prompt, mod_p, *post_w, per_row_mod=False, b0=b0, nb=nb)
        return glu, ret, x2, hp, lg

    def pre_sample():
        xs3 = x_sample.reshape(1, bs, d)
        glu, q, k, v, sg = _proj(xs3, mod_s, row(g_mix[0]), w_in_b, cos_s, sin_s, per_row_mod=True, **dims)
        cat, ret, new_conv = _mix1(glu, q, k, v, sg, state_conv, state_ret, conv_w[0], *norm_rows, gam,
                                   **dims)
        x2, hp, lg = _post(cat, xs3, mod_s, *post_w, per_row_mod=True)
        return new_conv, ret, x2, hp, lg

    def route_and_dispatch(hps, lgs):
        lg = lgs[0] if len(lgs) == 1 else jnp.concatenate(lgs, axis=1)
        tokens = lg.shape[1]
        eidx, wts, rank, cnt = _route(lg, bias_col)
        counts = cnt[:, 0].astype(I32)
        n_rows = -(-(tokens * TOP_K + ne * (tm - 1)) // tm) * tm
        dest3 = _dest_rows(counts, eidx, rank).reshape(tokens // TOKEN_TILE, SUBLANES, TOKEN_TILE)
        xs = _dispatch([h.reshape(-1, hw) for h in hps], dest3, n_rows)
        return counts, dest3, wts, xs.reshape(n_rows * ROW_WORDS, LANES)

    def undispatch(ys, dest3, after=None):
        tokens = dest3.shape[0] * TOKEN_TILE
        z = _undispatch(ys.reshape(-1, hw), dest3, tokens, after)
        return z.reshape(TOP_K, tokens * ROW_WORDS, LANES)

    nb0 = bp // 2
    nb1 = bp - nb0
    glu_0, ret_0, x2_0, hp_0, lg_0 = pre_prompt(0, nb0)
    counts_0, dest_0, wts_0, xs_0 = route_and_dispatch([hp_0], [lg_0])
    glu_1, ret_1, x2_1, hp_1, lg_1 = pre_prompt(nb0, nb1)
    new_conv_s, ret_s, x2_s, hp_s, lg_s = pre_sample()
    counts_1, dest_1, wts_1, xs_1 = route_and_dispatch([hp_1, hp_s], [lg_1, lg_s])
    ys_0, ys_1 = _experts((counts_0, counts_1), (xs_0, xs_1), w1[0], w3[0], w2[0])
    z_0 = undispatch(ys_0, dest_0)
    z_1 = undispatch(ys_1, dest_1, after=z_0[0, :SUBLANES])
    y_p = _combine(z_0, 0, wts_0, x2_0, mod_p, row(g_final), rows_per_mod=lp, per_row_mod=False,
                   out_rows=bp * lp)
    y_p = _combine(z_1, 0, wts_1, x2_1, mod_p, row(g_final), rows_per_mod=lp, per_row_mod=False,
                   b0=nb0, out_rows=bp * lp, into=y_p)
    y_s = _combine(z_1, nb1 * lp, wts_1, x2_s, mod_s, row(g_final), rows_per_mod=bs, per_row_mod=True)
    ret_p = jnp.concatenate([ret_0, ret_1], axis=0)

    tail = lambda g, n: g.reshape(n, lp, dc)[:, lp - CONV_BUF:, :]
    new_conv_p = jnp.concatenate([tail(glu_0, nb0), tail(glu_1, nb1)], axis=0)
    return (y_p.reshape(bp, lp, d), y_s.reshape(bs, ls, d), new_conv_p[None], ret_p[None],
            new_conv_s, ret_s)
```

```python
import functools
import math

import jax
import jax.numpy as jnp
import numpy as np
from jax import lax
from jax.experimental import pallas as pl
from jax.experimental.pallas import tpu as pltpu
from jax.experimental.pallas import tpu_sc as plsc

F32 = jnp.float32
BF16 = jnp.bfloat16
U32 = jnp.uint32
I32 = jnp.int32

EPS = 1e-6
PAST_LEN = 16384
RET_HEADS = 4
RET_CHUNK = 128
CONV_WIDTH = 31
CONV_BUF = CONV_WIDTH - 1
ROPE_BASE = 10000.0
N_EXPERTS = 64
TOP_K = 6
N_GROUPS = 8
TOPK_GROUPS = 4
ROUTED_SCALE = 2.5

LANES = 128
SUBLANES = 8
CONV_PAD = 32
EXPERT_ROWS = 256
ROUTE_TILE = 640
TOKEN_TILE = 128
COMBINE_TILE = 512
VMEM_LIMIT = 56 * 1024 * 1024
SC_CORES = 2
SC_SUBCORES = 16
SC_WORKERS = SC_CORES * SC_SUBCORES
SCATTER_ROWS = 64
GATHER_ROWS = 16
ROW_WORDS = 4
X_SLOTS = 6
X_AHEAD = 4
Y_SLOTS = 4

HI_MASK = 0xFFFF0000


def _sigmoid(x):
    return jax.nn.sigmoid(x)


def _silu(x):
    return x * jax.nn.sigmoid(x)


def _pack_halves(lo, hi):
    lo_u = lax.bitcast_convert_type(lo.astype(BF16).astype(F32), U32) >> 16
    hi_u = lax.bitcast_convert_type(hi.astype(BF16).astype(F32), U32) & jnp.uint32(HI_MASK)
    return hi_u | lo_u


def _unpack_halves(p):
    lo = lax.bitcast_convert_type(p << 16, F32)
    hi = lax.bitcast_convert_type(p & jnp.uint32(HI_MASK), F32)
    return lo, hi


def _store_rows(ref, x):
    rows = x.shape[0]
    for j in range(ROW_WORDS):
        ref[pl.ds(j, rows, stride=ROW_WORDS), :] = x[:, LANES * j:LANES * (j + 1)]


def _load_row_word(ref, j, rows):
    return ref[pl.ds(j, rows, stride=ROW_WORDS), :]


def _ada_body(c_ref, w_ref, b_ref, o_ref):
    s = _silu(c_ref[...]).astype(BF16)
    o_ref[...] = jnp.dot(s, w_ref[...].astype(BF16), preferred_element_type=F32) + b_ref[...]


def _ada(c_all, w_ada, b_ada):
    rows, d = c_all.shape
    n = w_ada.shape[1]
    tn = 2048
    return pl.pallas_call(
        _ada_body,
        grid=(n // tn,),
        in_specs=[
            pl.BlockSpec((rows, d), lambda j: (0, 0)),
            pl.BlockSpec((d, tn), lambda j: (0, j)),
            pl.BlockSpec((1, tn), lambda j: (0, j)),
        ],
        out_specs=pl.BlockSpec((rows, tn), lambda j: (0, j)),
        out_shape=jax.ShapeDtypeStruct((rows, n), F32),
        compiler_params=pltpu.CompilerParams(vmem_limit_bytes=VMEM_LIMIT),
        name="ada",
    )(c_all, w_ada, b_ada)


def _rope_body(inv_ref, cos_ref, sin_ref, *, pos0, tl, half):
    row = lax.broadcasted_iota(I32, (tl, LANES), 0) + pl.program_id(0) * tl
    ang = (row.astype(F32) + pos0) * inv_ref[...]
    lane = lax.broadcasted_iota(I32, (tl, LANES), 1)
    s = jnp.sin(ang)
    cos_ref[...] = jnp.cos(ang)
    sin_ref[...] = jnp.where(lane < half, -s, s)


def _rope_tables(inv2, rows, pos0):
    tl = min(rows, 256)
    return pl.pallas_call(
        functools.partial(_rope_body, pos0=float(pos0), tl=tl, half=LANES // 2),
        grid=(rows // tl,),
        in_specs=[pl.BlockSpec((1, LANES), lambda i: (0, 0))],
        out_specs=[pl.BlockSpec((tl, LANES), lambda i: (i, 0))] * 2,
        out_shape=[jax.ShapeDtypeStruct((rows, LANES), F32)] * 2,
        name="rope",
    )(inv2)


def _modulated_rmsnorm(x, g, sc, sh):
    ms = jnp.mean(x * x, axis=-1, keepdims=True)
    h = x * lax.rsqrt(ms + EPS) * g
    return h * (1.0 + sc) + sh


def _proj_body(x_ref, sh_ref, sc_ref, g_ref, w_ref, cos_ref, sin_ref,
               glu_ref, q_ref, k_ref, v_ref, sg_ref, *, dc, dr, hd, rope_rows):
    hb = _modulated_rmsnorm(x_ref[...], g_ref[...], sc_ref[...], sh_ref[...]).astype(BF16)

    def proj(lo, n):
        return jnp.dot(hb, w_ref[:, lo:lo + n], preferred_element_type=F32)

    glu_ref[...] = proj(0, dc) * _sigmoid(proj(dc, dc))
    cos = cos_ref[...] if rope_rows else cos_ref[0:1, :]
    sin = sin_ref[...] if rope_rows else sin_ref[0:1, :]
    for ref, lo, scale in ((q_ref, 2 * dc, hd ** -0.5), (k_ref, 2 * dc + dr, None)):
        t = proj(lo, dr)
        for hh in range(dr // hd):
            th = t[:, hh * hd:(hh + 1) * hd]
            r = th * cos + pltpu.roll(th, hd // 2, 1) * sin
            if scale is not None:
                r = r * scale
            ref[:, hh * hd:(hh + 1) * hd] = r.astype(BF16)
    v_ref[...] = proj(2 * dc + 2 * dr, dr).astype(BF16)
    sg_ref[...] = _silu(proj(2 * dc + 3 * dr, dr)).astype(BF16)


def _proj(x, mod, g_mix, w_in_b, cos2, sin2, *, dc, dr, hd, per_row_mod, b0=0, nb=None):
    nb_all, length, d = x.shape
    nb = nb_all if nb is None else nb
    tl = min(length, 512)
    x2 = x.reshape(nb_all * length, d)
    nl = length // tl
    if per_row_mod:
        mod_spec = lambda j: pl.BlockSpec((tl, d), lambda b, l: (l, j))
        rope_spec = pl.BlockSpec((SUBLANES, LANES), lambda b, l: (0, 0))
    else:
        mod_spec = lambda j: pl.BlockSpec((None, None, 1, d), lambda b, l: (b + b0, j, 0, 0))
        rope_spec = pl.BlockSpec((tl, LANES), lambda b, l: (l, 0))
    row_spec = lambda w: pl.BlockSpec((tl, w), lambda b, l: (b * nl + l, 0))
    t = nb * length
    outs = pl.pallas_call(
        functools.partial(_proj_body, dc=dc, dr=dr, hd=hd, rope_rows=not per_row_mod),
        grid=(nb, nl),
        in_specs=[
            pl.BlockSpec((tl, d), lambda b, l: ((b + b0) * nl + l, 0)), mod_spec(0), mod_spec(1),
            pl.BlockSpec((1, d), lambda b, l: (0, 0)),
            pl.BlockSpec(w_in_b.shape, lambda b, l: (0, 0)),
            rope_spec, rope_spec,
        ],
        out_specs=[row_spec(dc), row_spec(dr), row_spec(dr), row_spec(dr), row_spec(dr)],
        out_shape=[
            jax.ShapeDtypeStruct((t, dc), F32),
            jax.ShapeDtypeStruct((t, dr), BF16),
            jax.ShapeDtypeStruct((t, dr), BF16),
            jax.ShapeDtypeStruct((t, dr), BF16),
            jax.ShapeDtypeStruct((t, dr), BF16),
        ],
        compiler_params=pltpu.CompilerParams(vmem_limit_bytes=VMEM_LIMIT),
        name="proj",
    )(x2, mod, mod, g_mix, w_in_b, cos2, sin2)
    return outs


def _layernorm_silu(c, g, b):
    mu = jnp.mean(c, axis=-1, keepdims=True)
    d = c - mu
    var = jnp.mean(d * d, axis=-1, keepdims=True)
    return _silu(d * lax.rsqrt(var + EPS) * g + b)


def _groupnorm(o, g, b):
    mu = jnp.mean(o, axis=-1, keepdims=True)
    d = o - mu
    var = jnp.mean(d * d, axis=-1, keepdims=True)
    return d * lax.rsqrt(var + EPS) * g + b


def _mix_body(glu_ref, q_ref, k_ref, v_ref, sg_ref, cw_ref, cb_ref, lng_ref, lnb_ref, rg_ref, rb_ref,
              mask_ref, qd_ref, kd_ref, cd_ref, cat_ref, st_ref, buf, cscr, *, tl, dc, hd, nh, chunk):
    nslab = dc // LANES

    @pl.when(pl.program_id(1) == 0)
    def _():
        buf[:, 0:CONV_PAD, :] = jnp.zeros((nslab, CONV_PAD, LANES), F32)
        st_ref[...] = jnp.zeros(st_ref.shape, F32)

    for j in range(nslab):
        buf[j, CONV_PAD:CONV_PAD + tl, :] = glu_ref[:, LANES * j:LANES * (j + 1)]
    first = CONV_PAD - CONV_BUF
    rows_per_iter = 8 * SUBLANES
    for j in range(nslab):
        cols = slice(LANES * j, LANES * (j + 1))
        wv = [jnp.broadcast_to(cw_ref[t:t + 1, cols], (SUBLANES, LANES)) for t in range(CONV_WIDTH)]
        bias = jnp.broadcast_to(cb_ref[0:1, cols], (SUBLANES, LANES))

        def body(r, carry, j=j, cols=cols, wv=wv, bias=bias):
            base = pl.multiple_of(r * rows_per_iter, rows_per_iter)
            for u in range(rows_per_iter // SUBLANES):
                acc = bias
                for t in range(CONV_WIDTH):
                    acc = acc + wv[t] * buf[j, pl.ds(base + (u * SUBLANES + first + t), SUBLANES), :]
                cscr[pl.ds(base + u * SUBLANES, SUBLANES), cols] = acc
            return carry

        lax.fori_loop(0, tl // rows_per_iter, body, 0)
    for j in range(nslab):
        buf[j, 0:CONV_PAD, :] = buf[j, tl:tl + CONV_PAD, :]
    cat_ref[:, 0:dc] = _layernorm_silu(cscr[...], lng_ref[...], lnb_ref[...]).astype(BF16)

    nt = (((1,), (1,)), ((), ()))
    tn = (((0,), (0,)), ((), ()))
    for c in range(tl // chunk):
        rows = slice(c * chunk, (c + 1) * chunk)
        for hh in range(nh):
            cols = slice(hh * hd, (hh + 1) * hd)
            qh = q_ref[rows, cols]
            kh = k_ref[rows, cols]
            vh = v_ref[rows, cols]
            s = st_ref[0, hh]
            scores = lax.dot_general(qh, kh, nt, preferred_element_type=F32) * mask_ref[hh]
            inner = jnp.dot(scores.astype(BF16), vh, preferred_element_type=F32)
            qd = (qh.astype(F32) * qd_ref[hh]).astype(BF16)
            cross = jnp.dot(qd, s.astype(BF16), preferred_element_type=F32)
            kd = (kh.astype(F32) * kd_ref[hh]).astype(BF16)
            st_ref[0, hh] = cd_ref[hh] * s + lax.dot_general(kd, vh, tn, preferred_element_type=F32)
            o = _groupnorm(inner + cross, rg_ref[0:1, cols], rb_ref[0:1, cols])
            cat_ref[rows, dc + hh * hd:dc + (hh + 1) * hd] = (o * sg_ref[rows, cols].astype(F32)).astype(BF16)


def _mix(glu, q, k, v, sg, conv_w, conv_b, ln_g, ln_b, rg, rb, tables, *, nb, length, dc, dr, hd):
    nh = dr // hd
    chunk = math.gcd(length, RET_CHUNK)
    tl = min(length, 256)
    nl = length // tl
    mask, qd, kd, cd = tables
    row_spec = lambda w: pl.BlockSpec((tl, w), lambda b, l: (b * nl + l, 0))
    full = lambda a: pl.BlockSpec(a.shape, lambda b, l: (0,) * a.ndim)
    cat, st = pl.pallas_call(
        functools.partial(_mix_body, tl=tl, dc=dc, hd=hd, nh=nh, chunk=chunk),
        grid=(nb, nl),
        in_specs=[row_spec(dc), row_spec(dr), row_spec(dr), row_spec(dr), row_spec(dr),
                  full(conv_w), full(conv_b), full(ln_g), full(ln_b), full(rg), full(rb),
                  full(mask), full(qd), full(kd), full(cd)],
        out_specs=[row_spec(dc + dr), pl.BlockSpec((1, nh, hd, hd), lambda b, l: (b, 0, 0, 0))],
        out_shape=[jax.ShapeDtypeStruct((nb * length, dc + dr), BF16),
                   jax.ShapeDtypeStruct((nb, nh, hd, hd), F32)],
        scratch_shapes=[pltpu.VMEM((dc // LANES, tl + CONV_PAD, LANES), F32),
                        pltpu.VMEM((tl, dc), F32)],
        compiler_params=pltpu.CompilerParams(dimension_semantics=("arbitrary", "arbitrary")),
        name="mix",
    )(glu, q, k, v, sg, conv_w, conv_b, ln_g, ln_b, rg, rb, mask, qd, kd, cd)
    return cat, st


def _mix1_body(glu_ref, q_ref, k_ref, v_ref, sg_ref, sc_ref, s0_ref, cw_ref, cb_ref, lng_ref, lnb_ref,
               rg_ref, rb_ref, gam_ref, cat_ref, st_ref, nc_ref, cscr, oscr, qf, kf, vf, *, tb, dc, hd, nh):
    w_hist = cw_ref[0:CONV_BUF, :]
    w_last = cw_ref[CONV_BUF:CONV_WIDTH, :]
    for bb in range(tb):
        hist = jnp.sum(sc_ref[bb] * w_hist, axis=0, keepdims=True)
        cscr[bb:bb + 1, :] = hist + glu_ref[bb:bb + 1, :] * w_last + cb_ref[...]
        nc_ref[bb, 0:CONV_BUF - 1, :] = sc_ref[bb, 1:CONV_BUF, :]
        nc_ref[bb, CONV_BUF - 1:CONV_BUF, :] = glu_ref[bb:bb + 1, :]
    cat_ref[:, 0:dc] = _layernorm_silu(cscr[...], lng_ref[...], lnb_ref[...]).astype(BF16)

    tn = (((0,), (0,)), ((), ()))
    rowid = lax.broadcasted_iota(I32, (tb, hd), 0)
    qf[...] = q_ref[...].astype(F32)
    kf[...] = k_ref[...].astype(F32)
    vf[...] = v_ref[...].astype(F32)
    for hh in range(nh):
        cols = slice(hh * hd, (hh + 1) * hd)
        qa = q_ref[:, cols]
        ka = k_ref[:, cols]
        gam = gam_ref[hh, 0:1, :]
        for bb in range(tb):
            onehot = (rowid == bb).astype(BF16)
            qcol = lax.dot_general(qa, onehot, tn, preferred_element_type=F32)
            kcol = lax.dot_general(ka, onehot, tn, preferred_element_type=F32)
            s0 = s0_ref[bb, hh]
            qrow = qf[bb:bb + 1, cols]
            krow = kf[bb:bb + 1, cols]
            vrow = vf[bb:bb + 1, cols]
            qk = jnp.sum(qrow * krow, axis=-1, keepdims=True)
            cross = gam * jnp.sum(qcol * s0, axis=0, keepdims=True)
            st_ref[bb, hh] = gam * s0 + kcol * vrow
            oscr[bb:bb + 1, cols] = qk * vrow + cross
    for hh in range(nh):
        cols = slice(hh * hd, (hh + 1) * hd)
        o = _groupnorm(oscr[:, cols], rg_ref[0:1, cols], rb_ref[0:1, cols])
        cat_ref[:, dc + hh * hd:dc + (hh + 1) * hd] = (o * sg_ref[:, cols].astype(F32)).astype(BF16)


def _mix1(glu, q, k, v, sg, state_conv, state_ret, conv_w, conv_b, ln_g, ln_b, rg, rb, gam, *, dc, dr, hd):
    nb = glu.shape[0]
    nh = dr // hd
    tb = 16
    row_spec = lambda w: pl.BlockSpec((tb, w), lambda i: (i, 0))
    full = lambda a: pl.BlockSpec(a.shape, lambda i: (0,) * a.ndim)
    st_spec = pl.BlockSpec((None, tb, nh, hd, hd), lambda i: (0, i, 0, 0, 0))
    conv_spec = pl.BlockSpec((None, tb, CONV_BUF, dc), lambda i: (0, i, 0, 0))
    cat, st, new_conv = pl.pallas_call(
        functools.partial(_mix1_body, tb=tb, dc=dc, hd=hd, nh=nh),
        grid=(nb // tb,),
        in_specs=[row_spec(dc), row_spec(dr), row_spec(dr), row_spec(dr), row_spec(dr), conv_spec, st_spec,
                  full(conv_w), full(conv_b), full(ln_g), full(ln_b), full(rg), full(rb), full(gam)],
        out_specs=[row_spec(dc + dr), st_spec, conv_spec],
        out_shape=[jax.ShapeDtypeStruct((nb, dc + dr), BF16),
                   jax.ShapeDtypeStruct((1, nb, nh, hd, hd), F32),
                   jax.ShapeDtypeStruct((1, nb, CONV_BUF, dc), F32)],
        scratch_shapes=[pltpu.VMEM((tb, dc), F32)] + [pltpu.VMEM((tb, dr), F32)] * 4,
        compiler_params=pltpu.CompilerParams(vmem_limit_bytes=VMEM_LIMIT),
        name="mix1",
    )(glu, q, k, v, sg, state_conv, state_ret, conv_w, conv_b, ln_g, ln_b, rg, rb, gam)
    return cat, st, new_conv


def _post_body(cat_ref, x_ref, gtm_ref, scf_ref, shf_ref, g_ref, wo_ref, wrh_ref, wrl_ref, x1_ref, hp_ref, lg_ref):
    d = x_ref.shape[1]
    y = jnp.dot(cat_ref[...], wo_ref[...], preferred_element_type=F32)
    x1 = x_ref[...] + gtm_ref[...] * y
    x1_ref[...] = x1
    h = _modulated_rmsnorm(x1, g_ref[...], scf_ref[...], shf_ref[...])
    hb = h.astype(BF16)
    _store_rows(hp_ref, _pack_halves(h[:, 0:d // 2], h[:, d // 2:d]))
    hl = (h - hb.astype(F32)).astype(BF16)
    nt = (((1,), (1,)), ((), ()))
    lg_ref[...] = (lax.dot_general(wrh_ref[...], hb, nt, preferred_element_type=F32)
                   + lax.dot_general(wrh_ref[...], hl, nt, preferred_element_type=F32)
                   + lax.dot_general(wrl_ref[...], hb, nt, preferred_element_type=F32))


def _post(cat, x, mod, g_ffn, wo_b, wrh, wrl, *, per_row_mod, b0=0, nb=None):
    nb_all, length, d = x.shape
    nb = nb_all if nb is None else nb
    tl = min(length, 1024)
    nl = length // tl
    t = nb * length
    ne = wrh.shape[0]
    x2d = x.reshape(nb_all * length, d)
    if per_row_mod:
        mod_spec = lambda j: pl.BlockSpec((tl, d), lambda b, l: (l, j))
    else:
        mod_spec = lambda j: pl.BlockSpec((None, None, 1, d), lambda b, l: (b + b0, j, 0, 0))
    row_spec = lambda w: pl.BlockSpec((tl, w), lambda b, l: (b * nl + l, 0))
    full = lambda a: pl.BlockSpec(a.shape, lambda b, l: (0,) * a.ndim)
    return pl.pallas_call(
        _post_body,
        grid=(nb, nl),
        in_specs=[row_spec(d), pl.BlockSpec((tl, d), lambda b, l: ((b + b0) * nl + l, 0)),
                  mod_spec(2), mod_spec(4), mod_spec(3),
                  full(g_ffn), full(wo_b), full(wrh), full(wrl)],
        out_specs=[row_spec(d),
                   pl.BlockSpec((tl * ROW_WORDS, LANES), lambda b, l: (b * nl + l, 0)),
                   pl.BlockSpec((ne, tl), lambda b, l: (0, b * nl + l))],
        out_shape=[jax.ShapeDtypeStruct((t, d), F32),
                   jax.ShapeDtypeStruct((t * ROW_WORDS, LANES), U32),
                   jax.ShapeDtypeStruct((ne, t), F32)],
        compiler_params=pltpu.CompilerParams(vmem_limit_bytes=VMEM_LIMIT),
        name="post",
    )(cat, x2d, mod, mod, mod, g_ffn, wo_b, wrh, wrl)


def _first_max(x, idx, sentinel):
    m = jnp.max(x, axis=0, keepdims=True)
    f = jnp.min(jnp.where(x == m, idx, sentinel), axis=0, keepdims=True)
    return m, f


def _route_body(lg_ref, bias_ref, e_ref, w_ref, r_ref, cnt_ref, cnt_scr, *, tr, ne, ng, topk, topg):
    @pl.when(pl.program_id(0) == 0)
    def _():
        cnt_scr[...] = jnp.zeros(cnt_scr.shape, F32)

    per = ne // ng
    neg = -jnp.inf
    scores = _sigmoid(lg_ref[...])
    sel = scores + bias_ref[...]
    sub = lax.broadcasted_iota(I32, (per, tr), 0)
    gs = []
    for g in range(ng):
        s_g = sel[g * per:(g + 1) * per, :]
        m1, f1 = _first_max(s_g, sub, per)
        m2 = jnp.max(jnp.where(sub == f1, neg, s_g), axis=0, keepdims=True)
        gs.append(m1 + m2)
    gsc = jnp.concatenate(gs, axis=0)
    gi = lax.broadcasted_iota(I32, (ng, tr), 0)
    keep = jnp.zeros((ng, tr), F32)
    for _ in range(topg):
        _, f = _first_max(gsc, gi, ng)
        pick = gi == f
        keep = jnp.where(pick, 1.0, keep)
        gsc = jnp.where(pick, neg, gsc)
    work = jnp.concatenate(
        [jnp.where(keep[g:g + 1, :] > 0.5, sel[g * per:(g + 1) * per, :], neg) for g in range(ng)], axis=0)
    ei = lax.broadcasted_iota(I32, (ne, tr), 0)
    picks, es, ws = [], [], []
    for _ in range(topk):
        _, f = _first_max(work, ei, ne)
        pick = ei == f
        picks.append(pick)
        es.append(f)
        ws.append(jnp.sum(jnp.where(pick, scores, 0.0), axis=0, keepdims=True))
        work = jnp.where(pick, neg, work)
    wsum = ws[0]
    for w in ws[1:]:
        wsum = wsum + w
    scale = ROUTED_SCALE / wsum
    chosen = picks[0]
    for p in picks[1:]:
        chosen = jnp.logical_or(chosen, p)
    chosen_f = chosen.astype(F32)
    t_row = lax.broadcasted_iota(I32, (tr, tr), 0)
    t_col = lax.broadcasted_iota(I32, (tr, tr), 1)
    before = (t_row < t_col).astype(BF16)
    prior = cnt_scr[:, 0:1] + jnp.dot(chosen_f.astype(BF16), before, preferred_element_type=F32)
    rs = [jnp.sum(jnp.where(p, prior, 0.0), axis=0, keepdims=True).astype(I32) for p in picks]
    pad_i = jnp.zeros((SUBLANES - topk, tr), I32)
    pad_f = jnp.zeros((SUBLANES - topk, tr), F32)
    e_ref[...] = jnp.concatenate(es + [pad_i], axis=0)
    w_ref[...] = jnp.concatenate([w * scale for w in ws] + [pad_f], axis=0)
    r_ref[...] = jnp.concatenate(rs + [pad_i], axis=0)
    total = cnt_scr[:, 0:1] + jnp.sum(chosen_f, axis=1, keepdims=True)
    cnt_scr[...] = jnp.broadcast_to(total, cnt_scr.shape)
    cnt_ref[...] = jnp.broadcast_to(total, cnt_ref.shape)


def _route_tile(t):
    return max(m for m in range(LANES, ROUTE_TILE + 1, LANES) if t % m == 0)


def _route(logits_t, bias_col):
    ne, t = logits_t.shape
    tr = _route_tile(t)
    tok = lambda dt: jax.ShapeDtypeStruct((SUBLANES, t), dt)
    tok_spec = pl.BlockSpec((SUBLANES, tr), lambda i: (0, i))
    return pl.pallas_call(
        functools.partial(_route_body, tr=tr, ne=ne, ng=N_GROUPS, topk=TOP_K, topg=TOPK_GROUPS),
        grid=(t // tr,),
        in_specs=[pl.BlockSpec((ne, tr), lambda i: (0, i)), pl.BlockSpec((ne, 1), lambda i: (0, 0))],
        out_specs=[tok_spec, tok_spec, tok_spec, pl.BlockSpec((ne, LANES), lambda i: (0, 0))],
        out_shape=[tok(I32), tok(F32), tok(I32), jax.ShapeDtypeStruct((ne, LANES), F32)],
        scratch_shapes=[pltpu.VMEM((ne, LANES), F32)],
        compiler_params=pltpu.CompilerParams(dimension_semantics=("arbitrary",)),
        name="route",
    )(logits_t, bias_col)


def _dest_body(cnt_ref, e_ref, r_ref, d_ref, ps, *, ne, tr):
    shift = EXPERT_ROWS.bit_length() - 1

    @pl.when(pl.program_id(0) == 0)
    def _():
        def step(j, start):
            ps[j] = start
            return start + lax.shift_left(lax.shift_right_logical(cnt_ref[j] + (EXPERT_ROWS - 1), shift), shift)

        lax.fori_loop(0, ne, step, jnp.int32(0))

    e = e_ref[...]
    base = jnp.zeros(e.shape, I32)
    for j in range(ne):
        base = jnp.where(e == j, ps[j], base)
    dest = base + r_ref[...]
    for m in range(tr // TOKEN_TILE):
        d_ref[SUBLANES * m:SUBLANES * (m + 1), :] = dest[:, TOKEN_TILE * m:TOKEN_TILE * (m + 1)]


def _dest_rows(counts, eidx, rank):
    rows, t = eidx.shape
    ne = counts.shape[0]
    tr = _route_tile(t)
    spec = pl.BlockSpec((rows, tr), lambda i, cnt: (0, i))
    return pl.pallas_call(
        functools.partial(_dest_body, ne=ne, tr=tr),
        grid_spec=pltpu.PrefetchScalarGridSpec(
            num_scalar_prefetch=1, grid=(t // tr,), in_specs=[spec, spec],
            out_specs=pl.BlockSpec((tr // TOKEN_TILE * rows, TOKEN_TILE), lambda i, cnt: (i, 0)),
            scratch_shapes=[pltpu.SMEM((ne,), I32)]),
        out_shape=jax.ShapeDtypeStruct((t // TOKEN_TILE * rows, TOKEN_TILE), I32),
        compiler_params=pltpu.CompilerParams(dimension_semantics=("arbitrary",)),
        name="dest",
    )(counts, eidx, rank)


def _sc_mesh():
    return plsc.VectorSubcoreMesh(core_axis_name="c", subcore_axis_name="s")


def _sc_worker_id():
    return lax.axis_index("s") * SC_CORES + lax.axis_index("c")


def _index_block(dest_ref, chunk, width):
    per_tile = TOKEN_TILE // width
    return dest_ref.at[chunk // per_tile, :, pl.ds((chunk % per_tile) * width, width)]


def _dispatch(hps, dest3, n_rows):
    w = SCATTER_ROWS
    width = hps[0].shape[1]
    bounds = [0]
    for h in hps:
        bounds.append(bounds[-1] + h.shape[0] // w)
    nch = bounds[-1]
    nsrc = len(hps)

    @functools.partial(
        pl.kernel, mesh=_sc_mesh(),
        out_type=jax.ShapeDtypeStruct((n_rows, width), U32),
        scratch_types=[pltpu.VMEM((2, SUBLANES, w), I32), pltpu.VMEM((2, w, width), U32),
                       pltpu.SemaphoreType.DMA, pltpu.SemaphoreType.DMA, pltpu.SemaphoreType.DMA],
        compiler_params=pltpu.CompilerParams(use_tc_tiling_on_sc=False),
        name="dispatch",
    )
    def run(*refs):
        src_refs, dest_ref, xs_ref = refs[:nsrc], refs[nsrc], refs[nsrc + 1]
        idx_v, rows_v = refs[nsrc + 2:nsrc + 4]
        sem_load = refs[nsrc + 4:nsrc + 6]
        sem_scatter = refs[nsrc + 6]
        wid = _sc_worker_id()

        def start_loads(c, slot):
            pltpu.async_copy(_index_block(dest_ref, c, w), idx_v.at[slot], sem_load[slot])
            for i, src in enumerate(src_refs):
                @pl.when(jnp.logical_and(c >= bounds[i], c < bounds[i + 1]))
                def _(src=src, lo=bounds[i]):
                    pltpu.async_copy(src.at[pl.ds((c - lo) * w, w)], rows_v.at[slot], sem_load[slot])

        def wait_loads(slot):
            pltpu.make_async_copy(_index_block(dest_ref, 0, w), idx_v.at[slot], sem_load[slot]).wait()
            pltpu.make_async_copy(src_refs[0].at[pl.ds(0, w)], rows_v.at[slot], sem_load[slot]).wait()

        @pl.when(wid < nch)
        def _():
            start_loads(wid, 0)

        @pl.loop(0, pl.cdiv(pl.cdiv(nch, SC_WORKERS), 2))
        def _(rr):
            for slot in range(2):
                c = (rr * 2 + slot) * SC_WORKERS + wid

                @pl.when(c < nch)
                def _(c=c, slot=slot):
                    wait_loads(slot)

                    @pl.when(c + SC_WORKERS < nch)
                    def _():
                        start_loads(c + SC_WORKERS, 1 - slot)

                    copies = [pltpu.async_copy(rows_v.at[slot], xs_ref.at[idx_v.at[slot, k]], sem_scatter)
                              for k in range(TOP_K)]
                    for cp in copies:
                        cp.wait()

    return run(*hps, dest3)


def _undispatch(ys, dest3, n_tokens, after=None):
    w = GATHER_ROWS
    width = ys.shape[1]
    nch = n_tokens // w

    @functools.partial(
        pl.kernel, mesh=_sc_mesh(),
        out_type=jax.ShapeDtypeStruct((TOP_K, n_tokens, width), U32),
        scratch_types=[pltpu.VMEM((2, SUBLANES, w), I32), pltpu.VMEM((2, TOP_K, w, width), U32),
                       pltpu.SemaphoreType.DMA, pltpu.SemaphoreType.DMA, pltpu.SemaphoreType.DMA],
        compiler_params=pltpu.CompilerParams(use_tc_tiling_on_sc=False),
        name="undispatch",
    )
    def run(*refs):
        ys_ref, dest_ref = refs[:2]
        z_ref, idx_v, bufs = refs[-6:-3]
        sem_gather = refs[-3:-1]
        sem_store = refs[-1]
        wid = _sc_worker_id()

        def start_gathers(c, slot):
            pltpu.sync_copy(_index_block(dest_ref, c, w), idx_v.at[slot])
            for k in range(TOP_K):
                pltpu.async_copy(ys_ref.at[idx_v.at[slot, k]], bufs.at[slot, k], sem_gather[slot])

        def wait_gathers(slot):
            for k in range(TOP_K):
                pltpu.make_async_copy(ys_ref.at[idx_v.at[slot, k]], bufs.at[slot, k], sem_gather[slot]).wait()

        @pl.when(wid < nch)
        def _():
            start_gathers(wid, 0)

        @pl.loop(0, pl.cdiv(pl.cdiv(nch, SC_WORKERS), 2))
        def _(rr):
            for slot in range(2):
                c = (rr * 2 + slot) * SC_WORKERS + wid

                @pl.when(c < nch)
                def _(c=c, slot=slot):
                    @pl.when(c + SC_WORKERS < nch)
                    def _():
                        start_gathers(c + SC_WORKERS, 1 - slot)

                    wait_gathers(slot)
                    stores = [pltpu.async_copy(bufs.at[slot, k], z_ref.at[k, pl.ds(c * w, w)], sem_store)
                              for k in range(TOP_K)]
                    for cp in stores:
                        cp.wait()

    return run(ys, dest3) if after is None else run(ys, dest3, after)


PART_SHIFT = 24


def _expert_body(*refs, tm, ne, nparts):
    cnt_refs = refs[:nparts]
    xs_refs = refs[nparts:2 * nparts]
    w1_ref, w3_ref, w2_ref = refs[2 * nparts:2 * nparts + 3]
    ys_refs = refs[2 * nparts + 3:3 * nparts + 3]
    (w1f, w3f, w2f, w1s, w3s, w2s, xbuf, ybuf, xlo, xhi, sched, sem_x, sem_y, sem_w) = refs[3 * nparts + 3:]
    blk_words = tm * ROW_WORDS
    half = ROW_WORDS * LANES
    shift = tm.bit_length() - 1

    def n_blocks_of(p, e):
        return lax.shift_right_logical(cnt_refs[p][e] + (tm - 1), shift)

    def n_all(e):
        n = n_blocks_of(0, e)
        for p in range(1, nparts):
            n = n + n_blocks_of(p, e)
        return n

    def next_nonempty(e):
        return lax.while_loop(
            lambda c: jnp.logical_and(c < ne, n_all(jnp.minimum(c, ne - 1)) == 0), lambda c: c + 1, e)

    def plan(e, carry):
        i, starts = carry[0], list(carry[1:])
        for p in range(nparts):
            n = n_blocks_of(p, e)

            def put(j, c, p=p, i=i, start=starts[p]):
                sched[i + j] = (start + j) + (p << PART_SHIFT)
                return c

            lax.fori_loop(0, n, put, 0)
            i = i + n
            starts[p] = starts[p] + n
        return (i, *starts)

    nu = lax.fori_loop(0, ne, plan, (jnp.int32(0),) * (nparts + 1))[0]

    def rows_of(code):
        blk = code & ((1 << PART_SHIFT) - 1)
        return pl.ds(pl.multiple_of(blk * blk_words, blk_words), blk_words)

    def x_start(i, slot):
        code = sched[i]
        for p in range(nparts):
            @pl.when(lax.shift_right_logical(code, PART_SHIFT) == p)
            def _(p=p):
                pltpu.make_async_copy(xs_refs[p].at[rows_of(code), :], xbuf.at[slot], sem_x.at[slot]).start()

    def x_wait(slot):
        pltpu.make_async_copy(xs_refs[0].at[pl.ds(0, blk_words), :], xbuf.at[slot], sem_x.at[slot]).wait()

    def y_start(i, slot):
        code = sched[i]
        for p in range(nparts):
            @pl.when(lax.shift_right_logical(code, PART_SHIFT) == p)
            def _(p=p):
                pltpu.make_async_copy(ybuf.at[slot], ys_refs[p].at[rows_of(code), :], sem_y.at[slot]).start()

    def y_wait(slot):
        pltpu.make_async_copy(ybuf.at[slot], ys_refs[0].at[pl.ds(0, blk_words), :], sem_y.at[slot]).wait()

    def w_copies(e, ws):
        return [pltpu.make_async_copy(src.at[e], dst.at[ws], sem_w.at[ws])
                for src, dst in ((w1_ref, w1f), (w3_ref, w3f), (w2_ref, w2f))]

    for q in range(X_AHEAD):
        @pl.when(q < nu)
        def _(q=q):
            x_start(q, q)

    e_first = next_nonempty(jnp.int32(0))

    @pl.when(e_first < ne)
    def _():
        for cp in w_copies(e_first, 0):
            cp.start()

    def blocks(i, nblk):
        for b in range(nblk):
            x_wait((i + b) % X_SLOTS)
        for b in range(nblk):
            nxt = i + b + X_AHEAD

            @pl.when(nxt < nu)
            def _(nxt=nxt):
                x_start(nxt, nxt % X_SLOTS)

        for b in range(nblk):
            @pl.when(i + b >= Y_SLOTS)
            def _(b=b):
                y_wait((i + b) % Y_SLOTS)

        for b in range(nblk):
            xin = xbuf.at[(i + b) % X_SLOTS]
            for w in range(ROW_WORDS):
                lo, hi = _unpack_halves(_load_row_word(xin, w, tm))
                xlo[b, :, LANES * w:LANES * (w + 1)] = lo.astype(BF16)
                xhi[b, :, LANES * w:LANES * (w + 1)] = hi.astype(BF16)

        def up(b, wsc):
            return (jnp.dot(xlo[b], wsc[0:half, :], preferred_element_type=F32)
                    + jnp.dot(xhi[b], wsc[half:2 * half, :], preferred_element_type=F32))

        for b in range(nblk):
            hid = (_silu(up(b, w1s)) * up(b, w3s)).astype(BF16)
            y = jnp.dot(hid, w2s[...], preferred_element_type=F32)
            _store_rows(ybuf.at[(i + b) % Y_SLOTS], _pack_halves(y[:, 0:half], y[:, half:2 * half]))
        for b in range(nblk):
            y_start(i + b, (i + b) % Y_SLOTS)

    def per_expert(e, carry):
        i0, ws = carry
        n = n_all(e)

        @pl.when(n > 0)
        def _():
            for cp in w_copies(e, ws):
                cp.wait()
            w1s[...] = w1f[ws].astype(BF16)
            w3s[...] = w3f[ws].astype(BF16)
            w2s[...] = w2f[ws].astype(BF16)
            e_next = next_nonempty(e + 1)

            @pl.when(e_next < ne)
            def _():
                for cp in w_copies(e_next, 1 - ws):
                    cp.start()

            def pair(j, c):
                blocks(i0 + 2 * j, 2)
                return c

            lax.fori_loop(0, lax.shift_right_logical(n, 1), pair, 0)

            @pl.when(n % 2 == 1)
            def _():
                blocks(i0 + n - 1, 1)

        return i0 + n, jnp.where(n > 0, 1 - ws, ws)

    lax.fori_loop(0, ne, per_expert, (jnp.int32(0), jnp.int32(0)))

    for q in range(Y_SLOTS):
        @pl.when(nu > q)
        def _(q=q):
            y_wait((nu - 1 - q) % Y_SLOTS)


def _experts(counts, xss, w1e, w3e, w2e):
    tm = EXPERT_ROWS
    ne, d, de = w1e.shape
    half = ROW_WORDS * LANES
    nparts = len(xss)
    cap = sum(x.shape[0] // (tm * ROW_WORDS) for x in xss)
    assert cap < (1 << PART_SHIFT)
    anyspec = pl.BlockSpec(memory_space=pl.ANY)
    blk_buf = lambda n: pltpu.VMEM((n, tm * ROW_WORDS, LANES), U32)
    return pl.pallas_call(
        functools.partial(_expert_body, tm=tm, ne=ne, nparts=nparts),
        grid_spec=pltpu.PrefetchScalarGridSpec(
            num_scalar_prefetch=nparts,
            grid=(1,),
            in_specs=[anyspec] * (nparts + 3),
            out_specs=[anyspec] * nparts,
            scratch_shapes=[pltpu.VMEM((2, d, de), F32), pltpu.VMEM((2, d, de), F32), pltpu.VMEM((2, de, d), F32),
                            pltpu.VMEM((d, de), BF16), pltpu.VMEM((d, de), BF16), pltpu.VMEM((de, d), BF16),
                            blk_buf(X_SLOTS), blk_buf(Y_SLOTS),
                            pltpu.VMEM((2, tm, half), BF16), pltpu.VMEM((2, tm, half), BF16),
                            pltpu.SMEM((cap,), I32),
                            pltpu.SemaphoreType.DMA((X_SLOTS,)), pltpu.SemaphoreType.DMA((Y_SLOTS,)),
                            pltpu.SemaphoreType.DMA((2,))]),
        out_shape=[jax.ShapeDtypeStruct(x.shape, U32) for x in xss],
        compiler_params=pltpu.CompilerParams(dimension_semantics=("arbitrary",), vmem_limit_bytes=VMEM_LIMIT),
        name="experts",
    )(*counts, *xss, w1e, w3e, w2e)


def _combine_body(z_ref, x1_ref, hp_ref, gtf_ref, wt_ref, gfin_ref, ws1_ref, ws3_ref, ws2_ref, y_ref, xo, hlo, hhi,
                  *, td, topk):
    half = ROW_WORDS * LANES
    for w in range(ROW_WORDS):
        lo, hi = _unpack_halves(_load_row_word(hp_ref, w, td))
        hlo[:, LANES * w:LANES * (w + 1)] = lo.astype(BF16)
        hhi[:, LANES * w:LANES * (w + 1)] = hi.astype(BF16)

    def up(w_ref):
        return (jnp.dot(hlo[...], w_ref[0:half, :], preferred_element_type=F32)
                + jnp.dot(hhi[...], w_ref[half:2 * half, :], preferred_element_type=F32))

    xo[...] = jnp.dot((_silu(up(ws1_ref)) * up(ws3_ref)).astype(BF16), ws2_ref[...], preferred_element_type=F32)
    wt = wt_ref[...].T
    ws = [wt[:, k:k + 1] for k in range(topk)]
    sq = jnp.zeros((td, 1), F32)
    for j in range(ROW_WORDS):
        acc_lo = jnp.zeros((td, LANES), F32)
        acc_hi = jnp.zeros((td, LANES), F32)
        for k in range(topk):
            lo, hi = _unpack_halves(_load_row_word(z_ref.at[k], j, td))
            acc_lo = acc_lo + ws[k] * lo
            acc_hi = acc_hi + ws[k] * hi
        for base, acc in ((0, acc_lo), (half, acc_hi)):
            cols = slice(base + LANES * j, base + LANES * (j + 1))
            x = x1_ref[:, cols] + gtf_ref[:, cols] * (acc + xo[:, cols])
            xo[:, cols] = x
            sq = sq + jnp.sum(x * x, axis=-1, keepdims=True)
    rs = lax.rsqrt(sq / (2 * half) + EPS)
    y_ref[...] = xo[...] * rs * gfin_ref[...]


def _combine_body_into(*refs, td, topk):
    _combine_body(*refs[:9], *refs[10:], td=td, topk=topk)


def _combine(z, token0, wts, x1, hp, mod, g_final, shared_w, *, rows_per_mod, per_row_mod, b0=0, out_rows=None,
             into=None):
    t, d = x1.shape
    td = min(t, COMBINE_TILE)
    tile0 = token0 // td
    out_rows = t if out_rows is None else out_rows
    if per_row_mod:
        gtf_spec = pl.BlockSpec((td, d), lambda i: (i, 5))
        out0 = 0
    else:
        tiles_per_mod = rows_per_mod // td
        gtf_spec = pl.BlockSpec((None, None, 1, d), lambda i: (i // tiles_per_mod + b0, 5, 0, 0))
        out0 = b0 * tiles_per_mod
    full = lambda a: pl.BlockSpec(a.shape, lambda i: (0,) * a.ndim)
    in_specs = [pl.BlockSpec((TOP_K, td * ROW_WORDS, LANES), lambda i: (0, i + tile0, 0)),
                pl.BlockSpec((td, d), lambda i: (i, 0)),
                pl.BlockSpec((td * ROW_WORDS, LANES), lambda i: (i, 0)),
                gtf_spec,
                pl.BlockSpec((SUBLANES, td), lambda i: (0, i + tile0)),
                pl.BlockSpec((1, d), lambda i: (0, 0))] + [full(w) for w in shared_w]
    args = [z, x1, hp, mod, wts, g_final, *shared_w]
    body, aliases = _combine_body, {}
    if into is not None:
        in_specs.append(pl.BlockSpec(memory_space=pl.ANY))
        args.append(into)
        body, aliases = _combine_body_into, {len(args) - 1: 0}
    return pl.pallas_call(
        functools.partial(body, td=td, topk=TOP_K),
        grid=(t // td,),
        in_specs=in_specs,
        out_specs=pl.BlockSpec((td, d), lambda i: (i + out0, 0)),
        scratch_shapes=[pltpu.VMEM((td, d), F32), pltpu.VMEM((td, d // 2), BF16), pltpu.VMEM((td, d // 2), BF16)],
        out_shape=jax.ShapeDtypeStruct((out_rows, d), F32),
        input_output_aliases=aliases,
        compiler_params=pltpu.CompilerParams(vmem_limit_bytes=VMEM_LIMIT),
        name="combine",
    )(*args)


def _log_gamma(nh):
    return np.log(1.0 - 2.0 ** (-5.0 - np.arange(nh, dtype=np.float32))).astype(np.float32)


def _retention_tables(length, nh, hd):
    c = math.gcd(length, RET_CHUNK)
    log_g = _log_gamma(nh)
    idx = np.arange(c, dtype=np.float32)
    rel = idx[:, None] - idx[None, :]
    mask = np.where(rel >= 0, np.exp(log_g[:, None, None] * np.maximum(rel, 0.0)), 0.0).astype(np.float32)
    q_decay = np.exp(log_g[None, :] * (idx[:, None] + 1.0)).astype(np.float32)
    k_decay = np.exp(log_g[None, :] * (c - 1.0 - idx[:, None])).astype(np.float32)
    chunk_decay = np.exp(log_g * np.float32(c)).astype(np.float32)
    qd = np.broadcast_to(q_decay.T[:, :, None], (nh, c, hd))
    kd = np.broadcast_to(k_decay.T[:, :, None], (nh, c, hd))
    cd = np.broadcast_to(chunk_decay[:, None, None], (nh, hd, hd))
    return tuple(jnp.asarray(t) for t in (mask, qd, kd, cd))


def kernel(x_prompt, x_sample, c_prompt, c_sample, state_conv, state_ret, w_ada, b_ada, g_mix, g_ffn, w_in,
           conv_w, conv_b, conv_norm_g, conv_norm_b, ret_norm_g, ret_norm_b, w_out, w_router, router_bias,
           w1, w3, w2, ws1, ws3, ws2, g_final):
    depth = w_ada.shape[0]
    assert depth == 1, "single-layer trunk"
    bp, lp, d = x_prompt.shape
    bs, ls, _ = x_sample.shape
    assert ls == 1
    dc = conv_w.shape[2]
    dr = ret_norm_g.shape[1]
    nh = RET_HEADS
    hd = dr // nh
    assert hd == LANES and lp % 256 == 0 and bs % TOKEN_TILE == 0 and d // 2 == ROW_WORDS * LANES
    ne = w_router.shape[2]
    row = lambda a: a.reshape(1, -1)

    mod = _ada(jnp.concatenate([c_prompt, c_sample], axis=0), w_ada[0], row(b_ada[0]))
    mod_p = mod[:bp].reshape(bp, 6, 1, d)
    mod_s = mod[bp:]

    half = hd // 2
    inv = (np.float32(ROPE_BASE) ** (-np.arange(half, dtype=np.float32) / np.float32(half))).astype(np.float32)
    inv2 = jnp.asarray(np.concatenate([inv, inv]).reshape(1, hd))
    cos_p, sin_p = _rope_tables(inv2, lp, 0)
    cos_s, sin_s = _rope_tables(inv2, SUBLANES, PAST_LEN)

    w_in_b = w_in[0].astype(BF16)
    wo_b = w_out[0].astype(BF16)
    wr_t = w_router[0].T
    wrh = wr_t.astype(BF16)
    wrl = (wr_t - wrh.astype(F32)).astype(BF16)
    ws1_b, ws3_b, ws2_b = ws1[0].astype(BF16), ws3[0].astype(BF16), ws2[0].astype(BF16)
    dims = dict(dc=dc, dr=dr, hd=hd)

    tables = _retention_tables(lp, nh, hd)
    gam = jnp.asarray(np.broadcast_to(np.exp(_log_gamma(nh))[:, None, None], (nh, SUBLANES, hd)))
    norm_rows = (row(conv_b[0]), row(conv_norm_g[0]), row(conv_norm_b[0]), row(ret_norm_g[0]), row(ret_norm_b[0]))
    post_w = (row(g_ffn[0]), wo_b, wrh, wrl)
    shared_w = (ws1_b, ws3_b, ws2_b)
    bias_col = router_bias[0].reshape(ne, 1)
    hw = d // 2
    tm = EXPERT_ROWS

    def pre_prompt(b0, nb):
        glu, q, k, v, sg = _proj(x_prompt, mod_p, row(g_mix[0]), w_in_b, cos_p, sin_p,
                                 per_row_mod=False, b0=b0, nb=nb, **dims)
        cat, ret = _mix(glu, q, k, v, sg, conv_w[0], *norm_rows, tables, nb=nb, length=lp, **dims)
        x2, hp, lg = _post(cat, x_prompt, mod_p, *post_w, per_row_mod=False, b0=b0, nb=nb)
        return glu, ret, x2, hp, lg

    def pre_sample():
        xs3 = x_sample.reshape(1, bs, d)
        glu, q, k, v, sg = _proj(xs3, mod_s, row(g_mix[0]), w_in_b, cos_s, sin_s, per_row_mod=True, **dims)
        cat, ret, new_conv = _mix1(glu, q, k, v, sg, state_conv, state_ret, conv_w[0], *norm_rows, gam,
                                   **dims)
        x2, hp, lg = _post(cat, xs3, mod_s, *post_w, per_row_mod=True)
        return new_conv, ret, x2, hp, lg

    def route_and_dispatch(hps, lgs):
        lg = lgs[0] if len(lgs) == 1 else jnp.concatenate(lgs, axis=1)
        tokens = lg.shape[1]
        eidx, wts, rank, cnt = _route(lg, bias_col)
        counts = cnt[:, 0].astype(I32)
        n_rows = -(-(tokens * TOP_K + ne * (tm - 1)) // tm) * tm
        dest3 = _dest_rows(counts, eidx, rank).reshape(tokens // TOKEN_TILE, SUBLANES, TOKEN_TILE)
        xs = _dispatch([h.reshape(-1, hw) for h in hps], dest3, n_rows)
        return counts, dest3, wts, xs.reshape(n_rows * ROW_WORDS, LANES)

    def undispatch(ys, dest3, after=None):
        tokens = dest3.shape[0] * TOKEN_TILE
        z = _undispatch(ys.reshape(-1, hw), dest3, tokens, after)
        return z.reshape(TOP_K, tokens * ROW_WORDS, LANES)

    nb0 = bp // 2
    nb1 = bp - nb0
    glu_0, ret_0, x2_0, hp_0, lg_0 = pre_prompt(0, nb0)
    counts_0, dest_0, wts_0, xs_0 = route_and_dispatch([hp_0], [lg_0])
    glu_1, ret_1, x2_1, hp_1, lg_1 = pre_prompt(nb0, nb1)
    new_conv_s, ret_s, x2_s, hp_s, lg_s = pre_sample()
    counts_1, dest_1, wts_1, xs_1 = route_and_dispatch([hp_1, hp_s], [lg_1, lg_s])
    ys_0, ys_1 = _experts((counts_0, counts_1), (xs_0, xs_1), w1[0], w3[0], w2[0])
    z_0 = undispatch(ys_0, dest_0)
    z_1 = undispatch(ys_1, dest_1, after=z_0[0, :SUBLANES])
    y_p = _combine(z_0, 0, wts_0, x2_0, hp_0, mod_p, row(g_final), shared_w, rows_per_mod=lp, per_row_mod=False,
                   out_rows=bp * lp)
    y_p = _combine(z_1, 0, wts_1, x2_1, hp_1, mod_p, row(g_final), shared_w, rows_per_mod=lp, per_row_mod=False,
                   b0=nb0, out_rows=bp * lp, into=y_p)
    y_s = _combine(z_1, nb1 * lp, wts_1, x2_s, hp_s, mod_s, row(g_final), shared_w, rows_per_mod=bs,
                   per_row_mod=True)
    ret_p = jnp.concatenate([ret_0, ret_1], axis=0)

    tail = lambda g, n: g.reshape(n, lp, dc)[:, lp - CONV_BUF:, :]
    new_conv_p = jnp.concatenate([tail(glu_0, nb0), tail(glu_1, nb1)], axis=0)
    return (y_p.reshape(bp, lp, d), y_s.reshape(bs, ls, d), new_conv_p[None], ret_p[None],
            new_conv_s, ret_s)
```

```python
import functools
import math

import jax
import jax.numpy as jnp
import numpy as np
from jax import lax
from jax.experimental import pallas as pl
from jax.experimental.pallas import tpu as pltpu
from jax.experimental.pallas import tpu_sc as plsc

F32 = jnp.float32
BF16 = jnp.bfloat16
U32 = jnp.uint32
I32 = jnp.int32

EPS = 1e-6
PAST_LEN = 16384
RET_HEADS = 4
RET_CHUNK = 128
CONV_WIDTH = 31
CONV_BUF = CONV_WIDTH - 1
ROPE_BASE = 10000.0
N_EXPERTS = 64
TOP_K = 6
N_GROUPS = 8
TOPK_GROUPS = 4
ROUTED_SCALE = 2.5

LANES = 128
SUBLANES = 8
CONV_PAD = 32
EXPERT_ROWS = 256
ROUTE_TILE = 640
TOKEN_TILE = 128
COMBINE_TILE = 512
VMEM_LIMIT = 56 * 1024 * 1024
SC_CORES = 2
SC_SUBCORES = 16
SC_WORKERS = SC_CORES * SC_SUBCORES
SCATTER_ROWS = 64
GATHER_ROWS = 16
ROW_WORDS = 4
X_SLOTS = 6
X_AHEAD = 4
Y_SLOTS = 4

HI_MASK = 0xFFFF0000


def _sigmoid(x):
    return jax.nn.sigmoid(x)


def _silu(x):
    return x * jax.nn.sigmoid(x)


def _pack_halves(lo, hi):
    lo_u = lax.bitcast_convert_type(lo.astype(BF16).astype(F32), U32) >> 16
    hi_u = lax.bitcast_convert_type(hi.astype(BF16).astype(F32), U32) & jnp.uint32(HI_MASK)
    return hi_u | lo_u


def _unpack_halves(p):
    lo = lax.bitcast_convert_type(p << 16, F32)
    hi = lax.bitcast_convert_type(p & jnp.uint32(HI_MASK), F32)
    return lo, hi


def _store_rows(ref, x):
    rows = x.shape[0]
    for j in range(ROW_WORDS):
        ref[pl.ds(j, rows, stride=ROW_WORDS), :] = x[:, LANES * j:LANES * (j + 1)]


def _load_row_word(ref, j, rows):
    return ref[pl.ds(j, rows, stride=ROW_WORDS), :]


def _ada_body(c_ref, w_ref, b_ref, o_ref):
    s = _silu(c_ref[...]).astype(BF16)
    o_ref[...] = jnp.dot(s, w_ref[...].astype(BF16), preferred_element_type=F32) + b_ref[...]


def _ada(c_all, w_ada, b_ada):
    rows, d = c_all.shape
    n = w_ada.shape[1]
    tn = 2048
    return pl.pallas_call(
        _ada_body,
        grid=(n // tn,),
        in_specs=[
            pl.BlockSpec((rows, d), lambda j: (0, 0)),
            pl.BlockSpec((d, tn), lambda j: (0, j)),
            pl.BlockSpec((1, tn), lambda j: (0, j)),
        ],
        out_specs=pl.BlockSpec((rows, tn), lambda j: (0, j)),
        out_shape=jax.ShapeDtypeStruct((rows, n), F32),
        compiler_params=pltpu.CompilerParams(vmem_limit_bytes=VMEM_LIMIT),
        name="ada",
    )(c_all, w_ada, b_ada)


def _rope_body(inv_ref, cos_ref, sin_ref, *, pos0, tl, half):
    row = lax.broadcasted_iota(I32, (tl, LANES), 0) + pl.program_id(0) * tl
    ang = (row.astype(F32) + pos0) * inv_ref[...]
    lane = lax.broadcasted_iota(I32, (tl, LANES), 1)
    s = jnp.sin(ang)
    cos_ref[...] = jnp.cos(ang)
    sin_ref[...] = jnp.where(lane < half, -s, s)


def _rope_tables(inv2, rows, pos0):
    tl = min(rows, 256)
    return pl.pallas_call(
        functools.partial(_rope_body, pos0=float(pos0), tl=tl, half=LANES // 2),
        grid=(rows // tl,),
        in_specs=[pl.BlockSpec((1, LANES), lambda i: (0, 0))],
        out_specs=[pl.BlockSpec((tl, LANES), lambda i: (i, 0))] * 2,
        out_shape=[jax.ShapeDtypeStruct((rows, LANES), F32)] * 2,
        name="rope",
    )(inv2)


def _modulated_rmsnorm(x, g, sc, sh):
    ms = jnp.mean(x * x, axis=-1, keepdims=True)
    h = x * lax.rsqrt(ms + EPS) * g
    return h * (1.0 + sc) + sh


def _proj_body(x_ref, sh_ref, sc_ref, g_ref, w_ref, cos_ref, sin_ref, *rest, dc, dr, hd, rope_rows):
    glu_ref, q_ref, k_ref, v_ref, sg_ref = rest[-5:]
    hb = _modulated_rmsnorm(x_ref[...], g_ref[...], sc_ref[...], sh_ref[...]).astype(BF16)

    def proj(lo, n):
        return jnp.dot(hb, w_ref[:, lo:lo + n], preferred_element_type=F32)

    glu_ref[...] = proj(0, dc) * _sigmoid(proj(dc, dc))
    cos = cos_ref[...] if rope_rows else cos_ref[0:1, :]
    sin = sin_ref[...] if rope_rows else sin_ref[0:1, :]
    for ref, lo, scale in ((q_ref, 2 * dc, hd ** -0.5), (k_ref, 2 * dc + dr, None)):
        t = proj(lo, dr)
        for hh in range(dr // hd):
            th = t[:, hh * hd:(hh + 1) * hd]
            r = th * cos + pltpu.roll(th, hd // 2, 1) * sin
            if scale is not None:
                r = r * scale
            ref[:, hh * hd:(hh + 1) * hd] = r.astype(BF16)
    v_ref[...] = proj(2 * dc + 2 * dr, dr).astype(BF16)
    sg_ref[...] = _silu(proj(2 * dc + 3 * dr, dr)).astype(BF16)


def _proj(x, mod, g_mix, w_in_b, cos2, sin2, *, dc, dr, hd, per_row_mod, b0=0, nb=None, after=None):
    nb_all, length, d = x.shape
    nb = nb_all if nb is None else nb
    tl = min(length, 512)
    x2 = x.reshape(nb_all * length, d)
    nl = length // tl
    if per_row_mod:
        mod_spec = lambda j: pl.BlockSpec((tl, d), lambda b, l: (l, j))
        rope_spec = pl.BlockSpec((SUBLANES, LANES), lambda b, l: (0, 0))
    else:
        mod_spec = lambda j: pl.BlockSpec((None, None, 1, d), lambda b, l: (b + b0, j, 0, 0))
        rope_spec = pl.BlockSpec((tl, LANES), lambda b, l: (l, 0))
    row_spec = lambda w: pl.BlockSpec((tl, w), lambda b, l: (b * nl + l, 0))
    t = nb * length
    outs = pl.pallas_call(
        functools.partial(_proj_body, dc=dc, dr=dr, hd=hd, rope_rows=not per_row_mod),
        grid=(nb, nl),
        in_specs=[
            pl.BlockSpec((tl, d), lambda b, l: ((b + b0) * nl + l, 0)), mod_spec(0), mod_spec(1),
            pl.BlockSpec((1, d), lambda b, l: (0, 0)),
            pl.BlockSpec(w_in_b.shape, lambda b, l: (0, 0)),
            rope_spec, rope_spec,
        ] + ([] if after is None else [pl.BlockSpec(memory_space=pl.ANY)]),
        out_specs=[row_spec(dc), row_spec(dr), row_spec(dr), row_spec(dr), row_spec(dr)],
        out_shape=[
            jax.ShapeDtypeStruct((t, dc), F32),
            jax.ShapeDtypeStruct((t, dr), BF16),
            jax.ShapeDtypeStruct((t, dr), BF16),
            jax.ShapeDtypeStruct((t, dr), BF16),
            jax.ShapeDtypeStruct((t, dr), BF16),
        ],
        compiler_params=pltpu.CompilerParams(vmem_limit_bytes=VMEM_LIMIT),
        name="proj",
    )(x2, mod, mod, g_mix, w_in_b, cos2, sin2, *(() if after is None else (after,)))
    return outs


def _layernorm_silu(c, g, b):
    mu = jnp.mean(c, axis=-1, keepdims=True)
    d = c - mu
    var = jnp.mean(d * d, axis=-1, keepdims=True)
    return _silu(d * lax.rsqrt(var + EPS) * g + b)


def _groupnorm(o, g, b):
    mu = jnp.mean(o, axis=-1, keepdims=True)
    d = o - mu
    var = jnp.mean(d * d, axis=-1, keepdims=True)
    return d * lax.rsqrt(var + EPS) * g + b


def _mix_body(glu_ref, q_ref, k_ref, v_ref, sg_ref, cw_ref, cb_ref, lng_ref, lnb_ref, rg_ref, rb_ref,
              mask_ref, qd_ref, kd_ref, cd_ref, cat_ref, st_ref, buf, cscr, *, tl, dc, hd, nh, chunk):
    nslab = dc // LANES

    @pl.when(pl.program_id(1) == 0)
    def _():
        buf[:, 0:CONV_PAD, :] = jnp.zeros((nslab, CONV_PAD, LANES), F32)
        st_ref[...] = jnp.zeros(st_ref.shape, F32)

    for j in range(nslab):
        buf[j, CONV_PAD:CONV_PAD + tl, :] = glu_ref[:, LANES * j:LANES * (j + 1)]
    first = CONV_PAD - CONV_BUF
    rows_per_iter = 8 * SUBLANES
    for j in range(nslab):
        cols = slice(LANES * j, LANES * (j + 1))
        wv = [jnp.broadcast_to(cw_ref[t:t + 1, cols], (SUBLANES, LANES)) for t in range(CONV_WIDTH)]
        bias = jnp.broadcast_to(cb_ref[0:1, cols], (SUBLANES, LANES))

        def body(r, carry, j=j, cols=cols, wv=wv, bias=bias):
            base = pl.multiple_of(r * rows_per_iter, rows_per_iter)
            for u in range(rows_per_iter // SUBLANES):
                acc = bias
                for t in range(CONV_WIDTH):
                    acc = acc + wv[t] * buf[j, pl.ds(base + (u * SUBLANES + first + t), SUBLANES), :]
                cscr[pl.ds(base + u * SUBLANES, SUBLANES), cols] = acc
            return carry

        lax.fori_loop(0, tl // rows_per_iter, body, 0)
    for j in range(nslab):
        buf[j, 0:CONV_PAD, :] = buf[j, tl:tl + CONV_PAD, :]
    cat_ref[:, 0:dc] = _layernorm_silu(cscr[...], lng_ref[...], lnb_ref[...]).astype(BF16)

    nt = (((1,), (1,)), ((), ()))
    tn = (((0,), (0,)), ((), ()))
    for c in range(tl // chunk):
        rows = slice(c * chunk, (c + 1) * chunk)
        for hh in range(nh):
            cols = slice(hh * hd, (hh + 1) * hd)
            qh = q_ref[rows, cols]
            kh = k_ref[rows, cols]
            vh = v_ref[rows, cols]
            s = st_ref[0, hh]
            scores = lax.dot_general(qh, kh, nt, preferred_element_type=F32) * mask_ref[hh]
            inner = jnp.dot(scores.astype(BF16), vh, preferred_element_type=F32)
            qd = (qh.astype(F32) * qd_ref[hh]).astype(BF16)
            cross = jnp.dot(qd, s.astype(BF16), preferred_element_type=F32)
            kd = (kh.astype(F32) * kd_ref[hh]).astype(BF16)
            st_ref[0, hh] = cd_ref[hh] * s + lax.dot_general(kd, vh, tn, preferred_element_type=F32)
            o = _groupnorm(inner + cross, rg_ref[0:1, cols], rb_ref[0:1, cols])
            cat_ref[rows, dc + hh * hd:dc + (hh + 1) * hd] = (o * sg_ref[rows, cols].astype(F32)).astype(BF16)


def _mix(glu, q, k, v, sg, conv_w, conv_b, ln_g, ln_b, rg, rb, tables, *, nb, length, dc, dr, hd):
    nh = dr // hd
    chunk = math.gcd(length, RET_CHUNK)
    tl = min(length, 256)
    nl = length // tl
    mask, qd, kd, cd = tables
    row_spec = lambda w: pl.BlockSpec((tl, w), lambda b, l: (b * nl + l, 0))
    full = lambda a: pl.BlockSpec(a.shape, lambda b, l: (0,) * a.ndim)
    cat, st = pl.pallas_call(
        functools.partial(_mix_body, tl=tl, dc=dc, hd=hd, nh=nh, chunk=chunk),
        grid=(nb, nl),
        in_specs=[row_spec(dc), row_spec(dr), row_spec(dr), row_spec(dr), row_spec(dr),
                  full(conv_w), full(conv_b), full(ln_g), full(ln_b), full(rg), full(rb),
                  full(mask), full(qd), full(kd), full(cd)],
        out_specs=[row_spec(dc + dr), pl.BlockSpec((1, nh, hd, hd), lambda b, l: (b, 0, 0, 0))],
        out_shape=[jax.ShapeDtypeStruct((nb * length, dc + dr), BF16),
                   jax.ShapeDtypeStruct((nb, nh, hd, hd), F32)],
        scratch_shapes=[pltpu.VMEM((dc // LANES, tl + CONV_PAD, LANES), F32),
                        pltpu.VMEM((tl, dc), F32)],
        compiler_params=pltpu.CompilerParams(dimension_semantics=("arbitrary", "arbitrary")),
        name="mix",
    )(glu, q, k, v, sg, conv_w, conv_b, ln_g, ln_b, rg, rb, mask, qd, kd, cd)
    return cat, st


def _mix1_body(glu_ref, q_ref, k_ref, v_ref, sg_ref, sc_ref, s0_ref, cw_ref, cb_ref, lng_ref, lnb_ref,
               rg_ref, rb_ref, gam_ref, cat_ref, st_ref, nc_ref, cscr, oscr, qf, kf, vf, *, tb, dc, hd, nh):
    w_hist = cw_ref[0:CONV_BUF, :]
    w_last = cw_ref[CONV_BUF:CONV_WIDTH, :]
    for bb in range(tb):
        hist = jnp.sum(sc_ref[bb] * w_hist, axis=0, keepdims=True)
        cscr[bb:bb + 1, :] = hist + glu_ref[bb:bb + 1, :] * w_last + cb_ref[...]
        nc_ref[bb, 0:CONV_BUF - 1, :] = sc_ref[bb, 1:CONV_BUF, :]
        nc_ref[bb, CONV_BUF - 1:CONV_BUF, :] = glu_ref[bb:bb + 1, :]
    cat_ref[:, 0:dc] = _layernorm_silu(cscr[...], lng_ref[...], lnb_ref[...]).astype(BF16)

    tn = (((0,), (0,)), ((), ()))
    rowid = lax.broadcasted_iota(I32, (tb, hd), 0)
    qf[...] = q_ref[...].astype(F32)
    kf[...] = k_ref[...].astype(F32)
    vf[...] = v_ref[...].astype(F32)
    for hh in range(nh):
        cols = slice(hh * hd, (hh + 1) * hd)
        qa = q_ref[:, cols]
        ka = k_ref[:, cols]
        gam = gam_ref[hh, 0:1, :]
        for bb in range(tb):
            onehot = (rowid == bb).astype(BF16)
            qcol = lax.dot_general(qa, onehot, tn, preferred_element_type=F32)
            kcol = lax.dot_general(ka, onehot, tn, preferred_element_type=F32)
            s0 = s0_ref[bb, hh]
            qrow = qf[bb:bb + 1, cols]
            krow = kf[bb:bb + 1, cols]
            vrow = vf[bb:bb + 1, cols]
            qk = jnp.sum(qrow * krow, axis=-1, keepdims=True)
            cross = gam * jnp.sum(qcol * s0, axis=0, keepdims=True)
            st_ref[bb, hh] = gam * s0 + kcol * vrow
            oscr[bb:bb + 1, cols] = qk * vrow + cross
    for hh in range(nh):
        cols = slice(hh * hd, (hh + 1) * hd)
        o = _groupnorm(oscr[:, cols], rg_ref[0:1, cols], rb_ref[0:1, cols])
        cat_ref[:, dc + hh * hd:dc + (hh + 1) * hd] = (o * sg_ref[:, cols].astype(F32)).astype(BF16)


def _mix1(glu, q, k, v, sg, state_conv, state_ret, conv_w, conv_b, ln_g, ln_b, rg, rb, gam, *, dc, dr, hd):
    nb = glu.shape[0]
    nh = dr // hd
    tb = 16
    row_spec = lambda w: pl.BlockSpec((tb, w), lambda i: (i, 0))
    full = lambda a: pl.BlockSpec(a.shape, lambda i: (0,) * a.ndim)
    st_spec = pl.BlockSpec((None, tb, nh, hd, hd), lambda i: (0, i, 0, 0, 0))
    conv_spec = pl.BlockSpec((None, tb, CONV_BUF, dc), lambda i: (0, i, 0, 0))
    cat, st, new_conv = pl.pallas_call(
        functools.partial(_mix1_body, tb=tb, dc=dc, hd=hd, nh=nh),
        grid=(nb // tb,),
        in_specs=[row_spec(dc), row_spec(dr), row_spec(dr), row_spec(dr), row_spec(dr), conv_spec, st_spec,
                  full(conv_w), full(conv_b), full(ln_g), full(ln_b), full(rg), full(rb), full(gam)],
        out_specs=[row_spec(dc + dr), st_spec, conv_spec],
        out_shape=[jax.ShapeDtypeStruct((nb, dc + dr), BF16),
                   jax.ShapeDtypeStruct((1, nb, nh, hd, hd), F32),
                   jax.ShapeDtypeStruct((1, nb, CONV_BUF, dc), F32)],
        scratch_shapes=[pltpu.VMEM((tb, dc), F32)] + [pltpu.VMEM((tb, dr), F32)] * 4,
        compiler_params=pltpu.CompilerParams(vmem_limit_bytes=VMEM_LIMIT),
        name="mix1",
    )(glu, q, k, v, sg, state_conv, state_ret, conv_w, conv_b, ln_g, ln_b, rg, rb, gam)
    return cat, st, new_conv


def _post_body(cat_ref, x_ref, gtm_ref, scf_ref, shf_ref, g_ref, wo_ref, wrh_ref, wrl_ref, x1_ref, hp_ref, lg_ref):
    d = x_ref.shape[1]
    y = jnp.dot(cat_ref[...], wo_ref[...], preferred_element_type=F32)
    x1 = x_ref[...] + gtm_ref[...] * y
    x1_ref[...] = x1
    h = _modulated_rmsnorm(x1, g_ref[...], scf_ref[...], shf_ref[...])
    hb = h.astype(BF16)
    _store_rows(hp_ref, _pack_halves(h[:, 0:d // 2], h[:, d // 2:d]))
    hl = (h - hb.astype(F32)).astype(BF16)
    nt = (((1,), (1,)), ((), ()))
    lg_ref[...] = (lax.dot_general(wrh_ref[...], hb, nt, preferred_element_type=F32)
                   + lax.dot_general(wrh_ref[...], hl, nt, preferred_element_type=F32)
                   + lax.dot_general(wrl_ref[...], hb, nt, preferred_element_type=F32))


def _post(cat, x, mod, g_ffn, wo_b, wrh, wrl, *, per_row_mod, b0=0, nb=None):
    nb_all, length, d = x.shape
    nb = nb_all if nb is None else nb
    tl = min(length, 1024)
    nl = length // tl
    t = nb * length
    ne = wrh.shape[0]
    x2d = x.reshape(nb_all * length, d)
    if per_row_mod:
        mod_spec = lambda j: pl.BlockSpec((tl, d), lambda b, l: (l, j))
    else:
        mod_spec = lambda j: pl.BlockSpec((None, None, 1, d), lambda b, l: (b + b0, j, 0, 0))
    row_spec = lambda w: pl.BlockSpec((tl, w), lambda b, l: (b * nl + l, 0))
    full = lambda a: pl.BlockSpec(a.shape, lambda b, l: (0,) * a.ndim)
    return pl.pallas_call(
        _post_body,
        grid=(nb, nl),
        in_specs=[row_spec(d), pl.BlockSpec((tl, d), lambda b, l: ((b + b0) * nl + l, 0)),
                  mod_spec(2), mod_spec(4), mod_spec(3),
                  full(g_ffn), full(wo_b), full(wrh), full(wrl)],
        out_specs=[row_spec(d),
                   pl.BlockSpec((tl * ROW_WORDS, LANES), lambda b, l: (b * nl + l, 0)),
                   pl.BlockSpec((ne, tl), lambda b, l: (0, b * nl + l))],
        out_shape=[jax.ShapeDtypeStruct((t, d), F32),
                   jax.ShapeDtypeStruct((t * ROW_WORDS, LANES), U32),
                   jax.ShapeDtypeStruct((ne, t), F32)],
        compiler_params=pltpu.CompilerParams(vmem_limit_bytes=VMEM_LIMIT),
        name="post",
    )(cat, x2d, mod, mod, mod, g_ffn, wo_b, wrh, wrl)


def _first_max(x, idx, sentinel):
    m = jnp.max(x, axis=0, keepdims=True)
    f = jnp.min(jnp.where(x == m, idx, sentinel), axis=0, keepdims=True)
    return m, f


def _route_body(lg_ref, bias_ref, e_ref, w_ref, r_ref, cnt_ref, cnt_scr, *, tr, ne, ng, topk, topg):
    @pl.when(pl.program_id(0) == 0)
    def _():
        cnt_scr[...] = jnp.zeros(cnt_scr.shape, F32)

    per = ne // ng
    neg = -jnp.inf
    scores = _sigmoid(lg_ref[...])
    sel = scores + bias_ref[...]
    sub = lax.broadcasted_iota(I32, (per, tr), 0)
    gs = []
    for g in range(ng):
        s_g = sel[g * per:(g + 1) * per, :]
        m1, f1 = _first_max(s_g, sub, per)
        m2 = jnp.max(jnp.where(sub == f1, neg, s_g), axis=0, keepdims=True)
        gs.append(m1 + m2)
    gsc = jnp.concatenate(gs, axis=0)
    gi = lax.broadcasted_iota(I32, (ng, tr), 0)
    keep = jnp.zeros((ng, tr), F32)
    for _ in range(topg):
        _, f = _first_max(gsc, gi, ng)
        pick = gi == f
        keep = jnp.where(pick, 1.0, keep)
        gsc = jnp.where(pick, neg, gsc)
    work = jnp.concatenate(
        [jnp.where(keep[g:g + 1, :] > 0.5, sel[g * per:(g + 1) * per, :], neg) for g in range(ng)], axis=0)
    ei = lax.broadcasted_iota(I32, (ne, tr), 0)
    picks, es, ws = [], [], []
    for _ in range(topk):
        _, f = _first_max(work, ei, ne)
        pick = ei == f
        picks.append(pick)
        es.append(f)
        ws.append(jnp.sum(jnp.where(pick, scores, 0.0), axis=0, keepdims=True))
        work = jnp.where(pick, neg, work)
    wsum = ws[0]
    for w in ws[1:]:
        wsum = wsum + w
    scale = ROUTED_SCALE / wsum
    chosen = picks[0]
    for p in picks[1:]:
        chosen = jnp.logical_or(chosen, p)
    chosen_f = chosen.astype(F32)
    t_row = lax.broadcasted_iota(I32, (tr, tr), 0)
    t_col = lax.broadcasted_iota(I32, (tr, tr), 1)
    before = (t_row < t_col).astype(BF16)
    prior = cnt_scr[:, 0:1] + jnp.dot(chosen_f.astype(BF16), before, preferred_element_type=F32)
    rs = [jnp.sum(jnp.where(p, prior, 0.0), axis=0, keepdims=True).astype(I32) for p in picks]
    pad_i = jnp.zeros((SUBLANES - topk, tr), I32)
    pad_f = jnp.zeros((SUBLANES - topk, tr), F32)
    e_ref[...] = jnp.concatenate(es + [pad_i], axis=0)
    w_ref[...] = jnp.concatenate([w * scale for w in ws] + [pad_f], axis=0)
    r_ref[...] = jnp.concatenate(rs + [pad_i], axis=0)
    total = cnt_scr[:, 0:1] + jnp.sum(chosen_f, axis=1, keepdims=True)
    cnt_scr[...] = jnp.broadcast_to(total, cnt_scr.shape)
    cnt_ref[...] = jnp.broadcast_to(total, cnt_ref.shape)


def _route_tile(t):
    return max(m for m in range(LANES, ROUTE_TILE + 1, LANES) if t % m == 0)


def _route(logits_t, bias_col):
    ne, t = logits_t.shape
    tr = _route_tile(t)
    tok = lambda dt: jax.ShapeDtypeStruct((SUBLANES, t), dt)
    tok_spec = pl.BlockSpec((SUBLANES, tr), lambda i: (0, i))
    return pl.pallas_call(
        functools.partial(_route_body, tr=tr, ne=ne, ng=N_GROUPS, topk=TOP_K, topg=TOPK_GROUPS),
        grid=(t // tr,),
        in_specs=[pl.BlockSpec((ne, tr), lambda i: (0, i)), pl.BlockSpec((ne, 1), lambda i: (0, 0))],
        out_specs=[tok_spec, tok_spec, tok_spec, pl.BlockSpec((ne, LANES), lambda i: (0, 0))],
        out_shape=[tok(I32), tok(F32), tok(I32), jax.ShapeDtypeStruct((ne, LANES), F32)],
        scratch_shapes=[pltpu.VMEM((ne, LANES), F32)],
        compiler_params=pltpu.CompilerParams(dimension_semantics=("arbitrary",)),
        name="route",
    )(logits_t, bias_col)


def _dest_body(cnt_ref, e_ref, r_ref, d_ref, ps, *, ne, tr):
    shift = EXPERT_ROWS.bit_length() - 1

    @pl.when(pl.program_id(0) == 0)
    def _():
        def step(j, start):
            ps[j] = start
            return start + lax.shift_left(lax.shift_right_logical(cnt_ref[j] + (EXPERT_ROWS - 1), shift), shift)

        lax.fori_loop(0, ne, step, jnp.int32(0))

    e = e_ref[...]
    base = jnp.zeros(e.shape, I32)
    for j in range(ne):
        base = jnp.where(e == j, ps[j], base)
    dest = base + r_ref[...]
    for m in range(tr // TOKEN_TILE):
        d_ref[SUBLANES * m:SUBLANES * (m + 1), :] = dest[:, TOKEN_TILE * m:TOKEN_TILE * (m + 1)]


def _dest_rows(counts, eidx, rank):
    rows, t = eidx.shape
    ne = counts.shape[0]
    tr = _route_tile(t)
    spec = pl.BlockSpec((rows, tr), lambda i, cnt: (0, i))
    return pl.pallas_call(
        functools.partial(_dest_body, ne=ne, tr=tr),
        grid_spec=pltpu.PrefetchScalarGridSpec(
            num_scalar_prefetch=1, grid=(t // tr,), in_specs=[spec, spec],
            out_specs=pl.BlockSpec((tr // TOKEN_TILE * rows, TOKEN_TILE), lambda i, cnt: (i, 0)),
            scratch_shapes=[pltpu.SMEM((ne,), I32)]),
        out_shape=jax.ShapeDtypeStruct((t // TOKEN_TILE * rows, TOKEN_TILE), I32),
        compiler_params=pltpu.CompilerParams(dimension_semantics=("arbitrary",)),
        name="dest",
    )(counts, eidx, rank)


def _sc_mesh():
    return plsc.VectorSubcoreMesh(core_axis_name="c", subcore_axis_name="s")


def _sc_worker_id():
    return lax.axis_index("s") * SC_CORES + lax.axis_index("c")


def _index_block(dest_ref, chunk, width):
    per_tile = TOKEN_TILE // width
    return dest_ref.at[chunk // per_tile, :, pl.ds((chunk % per_tile) * width, width)]


def _dispatch(hps, dest3, n_rows):
    w = SCATTER_ROWS
    width = hps[0].shape[1]
    bounds = [0]
    for h in hps:
        bounds.append(bounds[-1] + h.shape[0] // w)
    nch = bounds[-1]
    nsrc = len(hps)

    @functools.partial(
        pl.kernel, mesh=_sc_mesh(),
        out_type=jax.ShapeDtypeStruct((n_rows, width), U32),
        scratch_types=[pltpu.VMEM((2, SUBLANES, w), I32), pltpu.VMEM((2, w, width), U32),
                       pltpu.SemaphoreType.DMA, pltpu.SemaphoreType.DMA, pltpu.SemaphoreType.DMA],
        compiler_params=pltpu.CompilerParams(use_tc_tiling_on_sc=False),
        name="dispatch",
    )
    def run(*refs):
        src_refs, dest_ref, xs_ref = refs[:nsrc], refs[nsrc], refs[nsrc + 1]
        idx_v, rows_v = refs[nsrc + 2:nsrc + 4]
        sem_load = refs[nsrc + 4:nsrc + 6]
        sem_scatter = refs[nsrc + 6]
        wid = _sc_worker_id()

        def start_loads(c, slot):
            pltpu.async_copy(_index_block(dest_ref, c, w), idx_v.at[slot], sem_load[slot])
            for i, src in enumerate(src_refs):
                @pl.when(jnp.logical_and(c >= bounds[i], c < bounds[i + 1]))
                def _(src=src, lo=bounds[i]):
                    pltpu.async_copy(src.at[pl.ds((c - lo) * w, w)], rows_v.at[slot], sem_load[slot])

        def wait_loads(slot):
            pltpu.make_async_copy(_index_block(dest_ref, 0, w), idx_v.at[slot], sem_load[slot]).wait()
            pltpu.make_async_copy(src_refs[0].at[pl.ds(0, w)], rows_v.at[slot], sem_load[slot]).wait()

        @pl.when(wid < nch)
        def _():
            start_loads(wid, 0)

        @pl.loop(0, pl.cdiv(pl.cdiv(nch, SC_WORKERS), 2))
        def _(rr):
            for slot in range(2):
                c = (rr * 2 + slot) * SC_WORKERS + wid

                @pl.when(c < nch)
                def _(c=c, slot=slot):
                    wait_loads(slot)

                    @pl.when(c + SC_WORKERS < nch)
                    def _():
                        start_loads(c + SC_WORKERS, 1 - slot)

                    copies = [pltpu.async_copy(rows_v.at[slot], xs_ref.at[idx_v.at[slot, k]], sem_scatter)
                              for k in range(TOP_K)]
                    for cp in copies:
                        cp.wait()

    return run(*hps, dest3)


def _undispatch(ys, dest3, n_tokens, after=None):
    w = GATHER_ROWS
    width = ys.shape[1]
    nch = n_tokens // w

    @functools.partial(
        pl.kernel, mesh=_sc_mesh(),
        out_type=jax.ShapeDtypeStruct((TOP_K, n_tokens, width), U32),
        scratch_types=[pltpu.VMEM((2, SUBLANES, w), I32), pltpu.VMEM((2, TOP_K, w, width), U32),
                       pltpu.SemaphoreType.DMA, pltpu.SemaphoreType.DMA, pltpu.SemaphoreType.DMA],
        compiler_params=pltpu.CompilerParams(use_tc_tiling_on_sc=False),
        name="undispatch",
    )
    def run(*refs):
        ys_ref, dest_ref = refs[:2]
        z_ref, idx_v, bufs = refs[-6:-3]
        sem_gather = refs[-3:-1]
        sem_store = refs[-1]
        wid = _sc_worker_id()

        def start_gathers(c, slot):
            pltpu.sync_copy(_index_block(dest_ref, c, w), idx_v.at[slot])
            for k in range(TOP_K):
                pltpu.async_copy(ys_ref.at[idx_v.at[slot, k]], bufs.at[slot, k], sem_gather[slot])

        def wait_gathers(slot):
            for k in range(TOP_K):
                pltpu.make_async_copy(ys_ref.at[idx_v.at[slot, k]], bufs.at[slot, k], sem_gather[slot]).wait()

        @pl.when(wid < nch)
        def _():
            start_gathers(wid, 0)

        @pl.loop(0, pl.cdiv(pl.cdiv(nch, SC_WORKERS), 2))
        def _(rr):
            for slot in range(2):
                c = (rr * 2 + slot) * SC_WORKERS + wid

                @pl.when(c < nch)
                def _(c=c, slot=slot):
                    @pl.when(c + SC_WORKERS < nch)
                    def _():
                        start_gathers(c + SC_WORKERS, 1 - slot)

                    wait_gathers(slot)
                    stores = [pltpu.async_copy(bufs.at[slot, k], z_ref.at[k, pl.ds(c * w, w)], sem_store)
                              for k in range(TOP_K)]
                    for cp in stores:
                        cp.wait()

    return run(ys, dest3) if after is None else run(ys, dest3, after)


PART_SHIFT = 24


def _expert_body(*refs, tm, ne, nparts):
    cnt_refs = refs[:nparts]
    xs_refs = refs[nparts:2 * nparts]
    w1_ref, w3_ref, w2_ref = refs[2 * nparts:2 * nparts + 3]
    ys_refs = refs[2 * nparts + 3:3 * nparts + 3]
    (w1f, w3f, w2f, w1s, w3s, w2s, xbuf, ybuf, xlo, xhi, sched, sem_x, sem_y, sem_w) = refs[3 * nparts + 3:]
    blk_words = tm * ROW_WORDS
    half = ROW_WORDS * LANES
    shift = tm.bit_length() - 1

    def n_blocks_of(p, e):
        return lax.shift_right_logical(cnt_refs[p][e] + (tm - 1), shift)

    def n_all(e):
        n = n_blocks_of(0, e)
        for p in range(1, nparts):
            n = n + n_blocks_of(p, e)
        return n

    def next_nonempty(e):
        return lax.while_loop(
            lambda c: jnp.logical_and(c < ne, n_all(jnp.minimum(c, ne - 1)) == 0), lambda c: c + 1, e)

    def plan(e, carry):
        i, starts = carry[0], list(carry[1:])
        for p in range(nparts):
            n = n_blocks_of(p, e)

            def put(j, c, p=p, i=i, start=starts[p]):
                sched[i + j] = (start + j) + (p << PART_SHIFT)
                return c

            lax.fori_loop(0, n, put, 0)
            i = i + n
            starts[p] = starts[p] + n
        return (i, *starts)

    nu = lax.fori_loop(0, ne, plan, (jnp.int32(0),) * (nparts + 1))[0]

    def rows_of(code):
        blk = code & ((1 << PART_SHIFT) - 1)
        return pl.ds(pl.multiple_of(blk * blk_words, blk_words), blk_words)

    def x_start(i, slot):
        code = sched[i]
        for p in range(nparts):
            @pl.when(lax.shift_right_logical(code, PART_SHIFT) == p)
            def _(p=p):
                pltpu.make_async_copy(xs_refs[p].at[rows_of(code), :], xbuf.at[slot], sem_x.at[slot]).start()

    def x_wait(slot):
        pltpu.make_async_copy(xs_refs[0].at[pl.ds(0, blk_words), :], xbuf.at[slot], sem_x.at[slot]).wait()

    def y_start(i, slot):
        code = sched[i]
        for p in range(nparts):
            @pl.when(lax.shift_right_logical(code, PART_SHIFT) == p)
            def _(p=p):
                pltpu.make_async_copy(ybuf.at[slot], ys_refs[p].at[rows_of(code), :], sem_y.at[slot]).start()

    def y_wait(slot):
        pltpu.make_async_copy(ybuf.at[slot], ys_refs[0].at[pl.ds(0, blk_words), :], sem_y.at[slot]).wait()

    def w_copies(e, ws):
        return [pltpu.make_async_copy(src.at[e], dst.at[ws], sem_w.at[ws])
                for src, dst in ((w1_ref, w1f), (w3_ref, w3f), (w2_ref, w2f))]

    for q in range(X_AHEAD):
        @pl.when(q < nu)
        def _(q=q):
            x_start(q, q)

    e_first = next_nonempty(jnp.int32(0))

    @pl.when(e_first < ne)
    def _():
        for cp in w_copies(e_first, 0):
            cp.start()

    def blocks(i, nblk):
        for b in range(nblk):
            x_wait((i + b) % X_SLOTS)
        for b in range(nblk):
            nxt = i + b + X_AHEAD

            @pl.when(nxt < nu)
            def _(nxt=nxt):
                x_start(nxt, nxt % X_SLOTS)

        for b in range(nblk):
            @pl.when(i + b >= Y_SLOTS)
            def _(b=b):
                y_wait((i + b) % Y_SLOTS)

        for b in range(nblk):
            xin = xbuf.at[(i + b) % X_SLOTS]
            for w in range(ROW_WORDS):
                lo, hi = _unpack_halves(_load_row_word(xin, w, tm))
                xlo[b, :, LANES * w:LANES * (w + 1)] = lo.astype(BF16)
                xhi[b, :, LANES * w:LANES * (w + 1)] = hi.astype(BF16)

        def up(b, wsc):
            return (jnp.dot(xlo[b], wsc[0:half, :], preferred_element_type=F32)
                    + jnp.dot(xhi[b], wsc[half:2 * half, :], preferred_element_type=F32))

        for b in range(nblk):
            hid = (_silu(up(b, w1s)) * up(b, w3s)).astype(BF16)
            y = jnp.dot(hid, w2s[...], preferred_element_type=F32)
            _store_rows(ybuf.at[(i + b) % Y_SLOTS], _pack_halves(y[:, 0:half], y[:, half:2 * half]))
        for b in range(nblk):
            y_start(i + b, (i + b) % Y_SLOTS)

    def per_expert(e, carry):
        i0, ws = carry
        n = n_all(e)

        @pl.when(n > 0)
        def _():
            for cp in w_copies(e, ws):
                cp.wait()
            w1s[...] = w1f[ws].astype(BF16)
            w3s[...] = w3f[ws].astype(BF16)
            w2s[...] = w2f[ws].astype(BF16)
            e_next = next_nonempty(e + 1)

            @pl.when(e_next < ne)
            def _():
                for cp in w_copies(e_next, 1 - ws):
                    cp.start()

            def pair(j, c):
                blocks(i0 + 2 * j, 2)
                return c

            lax.fori_loop(0, lax.shift_right_logical(n, 1), pair, 0)

            @pl.when(n % 2 == 1)
            def _():
                blocks(i0 + n - 1, 1)

        return i0 + n, jnp.where(n > 0, 1 - ws, ws)

    lax.fori_loop(0, ne, per_expert, (jnp.int32(0), jnp.int32(0)))

    for q in range(Y_SLOTS):
        @pl.when(nu > q)
        def _(q=q):
            y_wait((nu - 1 - q) % Y_SLOTS)


def _experts(counts, xss, w1e, w3e, w2e):
    tm = EXPERT_ROWS
    ne, d, de = w1e.shape
    half = ROW_WORDS * LANES
    nparts = len(xss)
    cap = sum(x.shape[0] // (tm * ROW_WORDS) for x in xss)
    assert cap < (1 << PART_SHIFT)
    anyspec = pl.BlockSpec(memory_space=pl.ANY)
    blk_buf = lambda n: pltpu.VMEM((n, tm * ROW_WORDS, LANES), U32)
    return pl.pallas_call(
        functools.partial(_expert_body, tm=tm, ne=ne, nparts=nparts),
        grid_spec=pltpu.PrefetchScalarGridSpec(
            num_scalar_prefetch=nparts,
            grid=(1,),
            in_specs=[anyspec] * (nparts + 3),
            out_specs=[anyspec] * nparts,
            scratch_shapes=[pltpu.VMEM((2, d, de), F32), pltpu.VMEM((2, d, de), F32), pltpu.VMEM((2, de, d), F32),
                            pltpu.VMEM((d, de), BF16), pltpu.VMEM((d, de), BF16), pltpu.VMEM((de, d), BF16),
                            blk_buf(X_SLOTS), blk_buf(Y_SLOTS),
                            pltpu.VMEM((2, tm, half), BF16), pltpu.VMEM((2, tm, half), BF16),
                            pltpu.SMEM((cap,), I32),
                            pltpu.SemaphoreType.DMA((X_SLOTS,)), pltpu.SemaphoreType.DMA((Y_SLOTS,)),
                            pltpu.SemaphoreType.DMA((2,))]),
        out_shape=[jax.ShapeDtypeStruct(x.shape, U32) for x in xss],
        compiler_params=pltpu.CompilerParams(dimension_semantics=("arbitrary",), vmem_limit_bytes=VMEM_LIMIT),
        name="experts",
    )(*counts, *xss, w1e, w3e, w2e)


def _combine_body(z_ref, x1_ref, hp_ref, gtf_ref, wt_ref, gfin_ref, ws1_ref, ws3_ref, ws2_ref, y_ref, xo, hlo, hhi,
                  *, td, topk):
    half = ROW_WORDS * LANES
    for w in range(ROW_WORDS):
        lo, hi = _unpack_halves(_load_row_word(hp_ref, w, td))
        hlo[:, LANES * w:LANES * (w + 1)] = lo.astype(BF16)
        hhi[:, LANES * w:LANES * (w + 1)] = hi.astype(BF16)

    def up(w_ref):
        return (jnp.dot(hlo[...], w_ref[0:half, :], preferred_element_type=F32)
                + jnp.dot(hhi[...], w_ref[half:2 * half, :], preferred_element_type=F32))

    xo[...] = jnp.dot((_silu(up(ws1_ref)) * up(ws3_ref)).astype(BF16), ws2_ref[...], preferred_element_type=F32)
    wt = wt_ref[...].T
    ws = [wt[:, k:k + 1] for k in range(topk)]
    sq = jnp.zeros((td, 1), F32)
    for j in range(ROW_WORDS):
        acc_lo = jnp.zeros((td, LANES), F32)
        acc_hi = jnp.zeros((td, LANES), F32)
        for k in range(topk):
            lo, hi = _unpack_halves(_load_row_word(z_ref.at[k], j, td))
            acc_lo = acc_lo + ws[k] * lo
            acc_hi = acc_hi + ws[k] * hi
        for base, acc in ((0, acc_lo), (half, acc_hi)):
            cols = slice(base + LANES * j, base + LANES * (j + 1))
            x = x1_ref[:, cols] + gtf_ref[:, cols] * (acc + xo[:, cols])
            xo[:, cols] = x
            sq = sq + jnp.sum(x * x, axis=-1, keepdims=True)
    rs = lax.rsqrt(sq / (2 * half) + EPS)
    y_ref[...] = xo[...] * rs * gfin_ref[...]


def _combine_body_into(*refs, td, topk):
    _combine_body(*refs[:9], *refs[10:], td=td, topk=topk)


def _combine(z, token0, wts, x1, hp, mod, g_final, shared_w, *, rows_per_mod, per_row_mod, b0=0, out_rows=None,
             into=None):
    t, d = x1.shape
    td = min(t, COMBINE_TILE)
    tile0 = token0 // td
    out_rows = t if out_rows is None else out_rows
    if per_row_mod:
        gtf_spec = pl.BlockSpec((td, d), lambda i: (i, 5))
        out0 = 0
    else:
        tiles_per_mod = rows_per_mod // td
        gtf_spec = pl.BlockSpec((None, None, 1, d), lambda i: (i // tiles_per_mod + b0, 5, 0, 0))
        out0 = b0 * tiles_per_mod
    full = lambda a: pl.BlockSpec(a.shape, lambda i: (0,) * a.ndim)
    in_specs = [pl.BlockSpec((TOP_K, td * ROW_WORDS, LANES), lambda i: (0, i + tile0, 0)),
                pl.BlockSpec((td, d), lambda i: (i, 0)),
                pl.BlockSpec((td * ROW_WORDS, LANES), lambda i: (i, 0)),
                gtf_spec,
                pl.BlockSpec((SUBLANES, td), lambda i: (0, i + tile0)),
                pl.BlockSpec((1, d), lambda i: (0, 0))] + [full(w) for w in shared_w]
    args = [z, x1, hp, mod, wts, g_final, *shared_w]
    body, aliases = _combine_body, {}
    if into is not None:
        in_specs.append(pl.BlockSpec(memory_space=pl.ANY))
        args.append(into)
        body, aliases = _combine_body_into, {len(args) - 1: 0}
    return pl.pallas_call(
        functools.partial(body, td=td, topk=TOP_K),
        grid=(t // td,),
        in_specs=in_specs,
        out_specs=pl.BlockSpec((td, d), lambda i: (i + out0, 0)),
        scratch_shapes=[pltpu.VMEM((td, d), F32), pltpu.VMEM((td, d // 2), BF16), pltpu.VMEM((td, d // 2), BF16)],
        out_shape=jax.ShapeDtypeStruct((out_rows, d), F32),
        input_output_aliases=aliases,
        compiler_params=pltpu.CompilerParams(vmem_limit_bytes=VMEM_LIMIT),
        name="combine",
    )(*args)


def _log_gamma(nh):
    return np.log(1.0 - 2.0 ** (-5.0 - np.arange(nh, dtype=np.float32))).astype(np.float32)


def _retention_tables(length, nh, hd):
    c = math.gcd(length, RET_CHUNK)
    log_g = _log_gamma(nh)
    idx = np.arange(c, dtype=np.float32)
    rel = idx[:, None] - idx[None, :]
    mask = np.where(rel >= 0, np.exp(log_g[:, None, None] * np.maximum(rel, 0.0)), 0.0).astype(np.float32)
    q_decay = np.exp(log_g[None, :] * (idx[:, None] + 1.0)).astype(np.float32)
    k_decay = np.exp(log_g[None, :] * (c - 1.0 - idx[:, None])).astype(np.float32)
    chunk_decay = np.exp(log_g * np.float32(c)).astype(np.float32)
    qd = np.broadcast_to(q_decay.T[:, :, None], (nh, c, hd))
    kd = np.broadcast_to(k_decay.T[:, :, None], (nh, c, hd))
    cd = np.broadcast_to(chunk_decay[:, None, None], (nh, hd, hd))
    return tuple(jnp.asarray(t) for t in (mask, qd, kd, cd))


def kernel(x_prompt, x_sample, c_prompt, c_sample, state_conv, state_ret, w_ada, b_ada, g_mix, g_ffn, w_in,
           conv_w, conv_b, conv_norm_g, conv_norm_b, ret_norm_g, ret_norm_b, w_out, w_router, router_bias,
           w1, w3, w2, ws1, ws3, ws2, g_final):
    depth = w_ada.shape[0]
    assert depth == 1, "single-layer trunk"
    bp, lp, d = x_prompt.shape
    bs, ls, _ = x_sample.shape
    assert ls == 1
    dc = conv_w.shape[2]
    dr = ret_norm_g.shape[1]
    nh = RET_HEADS
    hd = dr // nh
    assert hd == LANES and lp % 256 == 0 and bs % TOKEN_TILE == 0 and d // 2 == ROW_WORDS * LANES
    ne = w_router.shape[2]
    row = lambda a: a.reshape(1, -1)

    mod = _ada(jnp.concatenate([c_prompt, c_sample], axis=0), w_ada[0], row(b_ada[0]))
    mod_p = mod[:bp].reshape(bp, 6, 1, d)
    mod_s = mod[bp:]

    half = hd // 2
    inv = (np.float32(ROPE_BASE) ** (-np.arange(half, dtype=np.float32) / np.float32(half))).astype(np.float32)
    inv2 = jnp.asarray(np.concatenate([inv, inv]).reshape(1, hd))
    cos_p, sin_p = _rope_tables(inv2, lp, 0)
    cos_s, sin_s = _rope_tables(inv2, SUBLANES, PAST_LEN)

    w_in_b = w_in[0].astype(BF16)
    wo_b = w_out[0].astype(BF16)
    wr_t = w_router[0].T
    wrh = wr_t.astype(BF16)
    wrl = (wr_t - wrh.astype(F32)).astype(BF16)
    ws1_b, ws3_b, ws2_b = ws1[0].astype(BF16), ws3[0].astype(BF16), ws2[0].astype(BF16)
    dims = dict(dc=dc, dr=dr, hd=hd)

    tables = _retention_tables(lp, nh, hd)
    gam = jnp.asarray(np.broadcast_to(np.exp(_log_gamma(nh))[:, None, None], (nh, SUBLANES, hd)))
    norm_rows = (row(conv_b[0]), row(conv_norm_g[0]), row(conv_norm_b[0]), row(ret_norm_g[0]), row(ret_norm_b[0]))
    post_w = (row(g_ffn[0]), wo_b, wrh, wrl)
    shared_w = (ws1_b, ws3_b, ws2_b)
    bias_col = router_bias[0].reshape(ne, 1)
    hw = d // 2
    tm = EXPERT_ROWS

    def pre_prompt(b0, nb, after=None):
        glu, q, k, v, sg = _proj(x_prompt, mod_p, row(g_mix[0]), w_in_b, cos_p, sin_p,
                                 per_row_mod=False, b0=b0, nb=nb, after=after, **dims)
        cat, ret = _mix(glu, q, k, v, sg, conv_w[0], *norm_rows, tables, nb=nb, length=lp, **dims)
        x2, hp, lg = _post(cat, x_prompt, mod_p, *post_w, per_row_mod=False, b0=b0, nb=nb)
        return glu, ret, x2, hp, lg

    def pre_sample():
        xs3 = x_sample.reshape(1, bs, d)
        glu, q, k, v, sg = _proj(xs3, mod_s, row(g_mix[0]), w_in_b, cos_s, sin_s, per_row_mod=True, **dims)
        cat, ret, new_conv = _mix1(glu, q, k, v, sg, state_conv, state_ret, conv_w[0], *norm_rows, gam,
                                   **dims)
        x2, hp, lg = _post(cat, xs3, mod_s, *post_w, per_row_mod=True)
        return new_conv, ret, x2, hp, lg

    def route_and_dispatch(hps, lgs):
        lg = lgs[0] if len(lgs) == 1 else jnp.concatenate(lgs, axis=1)
        tokens = lg.shape[1]
        eidx, wts, rank, cnt = _route(lg, bias_col)
        counts = cnt[:, 0].astype(I32)
        n_rows = -(-(tokens * TOP_K + ne * (tm - 1)) // tm) * tm
        dest3 = _dest_rows(counts, eidx, rank).reshape(tokens // TOKEN_TILE, SUBLANES, TOKEN_TILE)
        xs = _dispatch([h.reshape(-1, hw) for h in hps], dest3, n_rows)
        return counts, dest3, wts, xs.reshape(n_rows * ROW_WORDS, LANES)

    def undispatch(ys, dest3, after=None):
        tokens = dest3.shape[0] * TOKEN_TILE
        z = _undispatch(ys.reshape(-1, hw), dest3, tokens, after)
        return z.reshape(TOP_K, tokens * ROW_WORDS, LANES)

    nb0 = bp // 2
    nb1 = bp - nb0
    glu_0, ret_0, x2_0, hp_0, lg_0 = pre_prompt(0, nb0)
    counts_0, dest_0, wts_0, xs_0 = route_and_dispatch([hp_0], [lg_0])
    glu_1, ret_1, x2_1, hp_1, lg_1 = pre_prompt(nb0, nb1, after=dest_0)
    new_conv_s, ret_s, x2_s, hp_s, lg_s = pre_sample()
    counts_1, dest_1, wts_1, xs_1 = route_and_dispatch([hp_1, hp_s], [lg_1, lg_s])
    ys_0, ys_1 = _experts((counts_0, counts_1), (xs_0, xs_1), w1[0], w3[0], w2[0])
    z_0 = undispatch(ys_0, dest_0)
    z_1 = undispatch(ys_1, dest_1, after=z_0[0, :SUBLANES])
    y_p = _combine(z_0, 0, wts_0, x2_0, hp_0, mod_p, row(g_final), shared_w, rows_per_mod=lp, per_row_mod=False,
                   out_rows=bp * lp)
    y_p = _combine(z_1, 0, wts_1, x2_1, hp_1, mod_p, row(g_final), shared_w, rows_per_mod=lp, per_row_mod=False,
                   b0=nb0, out_rows=bp * lp, into=y_p)
    y_s = _combine(z_1, nb1 * lp, wts_1, x2_s, hp_s, mod_s, row(g_final), shared_w, rows_per_mod=bs,
                   per_row_mod=True)
    ret_p = jnp.concatenate([ret_0, ret_1], axis=0)

    tail = lambda g, n: g.reshape(n, lp, dc)[:, lp - CONV_BUF:, :]
    new_conv_p = jnp.concatenate([tail(glu_0, nb0), tail(glu_1, nb1)], axis=0)
    return (y_p.reshape(bp, lp, d), y_s.reshape(bs, ls, d), new_conv_p[None], ret_p[None],
            new_conv_s, ret_s)
```

```python
import functools
import math

import jax
import jax.numpy as jnp
import numpy as np
from jax import lax
from jax.experimental import pallas as pl
from jax.experimental.pallas import tpu as pltpu
from jax.experimental.pallas import tpu_sc as plsc

F32 = jnp.float32
BF16 = jnp.bfloat16
U32 = jnp.uint32
I32 = jnp.int32

EPS = 1e-6
PAST_LEN = 16384
RET_HEADS = 4
RET_CHUNK = 128
CONV_WIDTH = 31
CONV_BUF = CONV_WIDTH - 1
ROPE_BASE = 10000.0
N_EXPERTS = 64
TOP_K = 6
N_GROUPS = 8
TOPK_GROUPS = 4
ROUTED_SCALE = 2.5

LANES = 128
SUBLANES = 8
CONV_PAD = 32
EXPERT_ROWS = 256
ROUTE_TILE = 640
TOKEN_TILE = 128
COMBINE_TILE = 512
VMEM_LIMIT = 56 * 1024 * 1024
SC_CORES = 2
SC_SUBCORES = 16
SC_WORKERS = SC_CORES * SC_SUBCORES
SCATTER_ROWS = 64
GATHER_ROWS = 16
ROW_WORDS = 4
X_SLOTS = 6
X_AHEAD = 4
Y_SLOTS = 4

HI_MASK = 0xFFFF0000


def _sigmoid(x):
    return jax.nn.sigmoid(x)


def _silu(x):
    return x * jax.nn.sigmoid(x)


def _pack_halves(lo, hi):
    lo_u = lax.bitcast_convert_type(lo.astype(BF16).astype(F32), U32) >> 16
    hi_u = lax.bitcast_convert_type(hi.astype(BF16).astype(F32), U32) & jnp.uint32(HI_MASK)
    return hi_u | lo_u


def _unpack_halves(p):
    lo = lax.bitcast_convert_type(p << 16, F32)
    hi = lax.bitcast_convert_type(p & jnp.uint32(HI_MASK), F32)
    return lo, hi


def _store_rows(ref, x):
    rows = x.shape[0]
    for j in range(ROW_WORDS):
        ref[pl.ds(j, rows, stride=ROW_WORDS), :] = x[:, LANES * j:LANES * (j + 1)]


def _load_row_word(ref, j, rows):
    return ref[pl.ds(j, rows, stride=ROW_WORDS), :]


def _ada_body(c_ref, w_ref, b_ref, o_ref):
    s = _silu(c_ref[...]).astype(BF16)
    o_ref[...] = jnp.dot(s, w_ref[...].astype(BF16), preferred_element_type=F32) + b_ref[...]


def _ada(c_all, w_ada, b_ada):
    rows, d = c_all.shape
    n = w_ada.shape[1]
    tn = 2048
    return pl.pallas_call(
        _ada_body,
        grid=(n // tn,),
        in_specs=[
            pl.BlockSpec((rows, d), lambda j: (0, 0)),
            pl.BlockSpec((d, tn), lambda j: (0, j)),
            pl.BlockSpec((1, tn), lambda j: (0, j)),
        ],
        out_specs=pl.BlockSpec((rows, tn), lambda j: (0, j)),
        out_shape=jax.ShapeDtypeStruct((rows, n), F32),
        compiler_params=pltpu.CompilerParams(vmem_limit_bytes=VMEM_LIMIT),
        name="ada",
    )(c_all, w_ada, b_ada)


def _rope_body(inv_ref, cos_ref, sin_ref, *, pos0, tl, half):
    row = lax.broadcasted_iota(I32, (tl, LANES), 0) + pl.program_id(0) * tl
    ang = (row.astype(F32) + pos0) * inv_ref[...]
    lane = lax.broadcasted_iota(I32, (tl, LANES), 1)
    s = jnp.sin(ang)
    cos_ref[...] = jnp.cos(ang)
    sin_ref[...] = jnp.where(lane < half, -s, s)


def _rope_tables(inv2, rows, pos0):
    tl = min(rows, 256)
    return pl.pallas_call(
        functools.partial(_rope_body, pos0=float(pos0), tl=tl, half=LANES // 2),
        grid=(rows // tl,),
        in_specs=[pl.BlockSpec((1, LANES), lambda i: (0, 0))],
        out_specs=[pl.BlockSpec((tl, LANES), lambda i: (i, 0))] * 2,
        out_shape=[jax.ShapeDtypeStruct((rows, LANES), F32)] * 2,
        name="rope",
    )(inv2)


def _modulated_rmsnorm(x, g, sc, sh):
    ms = jnp.mean(x * x, axis=-1, keepdims=True)
    h = x * lax.rsqrt(ms + EPS) * g
    return h * (1.0 + sc) + sh


def _proj_body(x_ref, sh_ref, sc_ref, g_ref, w_ref, cos_ref, sin_ref, *rest, dc, dr, hd, rope_rows):
    glu_ref, q_ref, k_ref, v_ref, sg_ref = rest[-5:]
    hb = _modulated_rmsnorm(x_ref[...], g_ref[...], sc_ref[...], sh_ref[...]).astype(BF16)

    def proj(lo, n):
        return jnp.dot(hb, w_ref[:, lo:lo + n], preferred_element_type=F32)

    glu_ref[...] = proj(0, dc) * _sigmoid(proj(dc, dc))
    cos = cos_ref[...] if rope_rows else cos_ref[0:1, :]
    sin = sin_ref[...] if rope_rows else sin_ref[0:1, :]
    for ref, lo, scale in ((q_ref, 2 * dc, hd ** -0.5), (k_ref, 2 * dc + dr, None)):
        t = proj(lo, dr)
        for hh in range(dr // hd):
            th = t[:, hh * hd:(hh + 1) * hd]
            r = th * cos + pltpu.roll(th, hd // 2, 1) * sin
            if scale is not None:
                r = r * scale
            ref[:, hh * hd:(hh + 1) * hd] = r.astype(BF16)
    v_ref[...] = proj(2 * dc + 2 * dr, dr).astype(BF16)
    sg_ref[...] = _silu(proj(2 * dc + 3 * dr, dr)).astype(BF16)


def _proj(x, mod, g_mix, w_in_b, cos2, sin2, *, dc, dr, hd, per_row_mod, b0=0, nb=None, after=None):
    nb_all, length, d = x.shape
    nb = nb_all if nb is None else nb
    tl = min(length, 512)
    x2 = x.reshape(nb_all * length, d)
    nl = length // tl
    if per_row_mod:
        mod_spec = lambda j: pl.BlockSpec((tl, d), lambda b, l: (l, j))
        rope_spec = pl.BlockSpec((SUBLANES, LANES), lambda b, l: (0, 0))
    else:
        mod_spec = lambda j: pl.BlockSpec((None, None, 1, d), lambda b, l: (b + b0, j, 0, 0))
        rope_spec = pl.BlockSpec((tl, LANES), lambda b, l: (l, 0))
    row_spec = lambda w: pl.BlockSpec((tl, w), lambda b, l: (b * nl + l, 0))
    t = nb * length
    outs = pl.pallas_call(
        functools.partial(_proj_body, dc=dc, dr=dr, hd=hd, rope_rows=not per_row_mod),
        grid=(nb, nl),
        in_specs=[
            pl.BlockSpec((tl, d), lambda b, l: ((b + b0) * nl + l, 0)), mod_spec(0), mod_spec(1),
            pl.BlockSpec((1, d), lambda b, l: (0, 0)),
            pl.BlockSpec(w_in_b.shape, lambda b, l: (0, 0)),
            rope_spec, rope_spec,
        ] + ([] if after is None else [pl.BlockSpec(memory_space=pl.ANY)]),
        out_specs=[row_spec(dc), row_spec(dr), row_spec(dr), row_spec(dr), row_spec(dr)],
        out_shape=[
            jax.ShapeDtypeStruct((t, dc), F32),
            jax.ShapeDtypeStruct((t, dr), BF16),
            jax.ShapeDtypeStruct((t, dr), BF16),
            jax.ShapeDtypeStruct((t, dr), BF16),
            jax.ShapeDtypeStruct((t, dr), BF16),
        ],
        compiler_params=pltpu.CompilerParams(vmem_limit_bytes=VMEM_LIMIT),
        name="proj",
    )(x2, mod, mod, g_mix, w_in_b, cos2, sin2, *(() if after is None else (after,)))
    return outs


def _layernorm_silu(c, g, b):
    mu = jnp.mean(c, axis=-1, keepdims=True)
    d = c - mu
    var = jnp.mean(d * d, axis=-1, keepdims=True)
    return _silu(d * lax.rsqrt(var + EPS) * g + b)


def _groupnorm(o, g, b):
    mu = jnp.mean(o, axis=-1, keepdims=True)
    d = o - mu
    var = jnp.mean(d * d, axis=-1, keepdims=True)
    return d * lax.rsqrt(var + EPS) * g + b


def _mix_body(glu_ref, q_ref, k_ref, v_ref, sg_ref, cw_ref, cb_ref, lng_ref, lnb_ref, rg_ref, rb_ref,
              mask_ref, qd_ref, kd_ref, cd_ref, cat_ref, st_ref, buf, cscr, *, tl, dc, hd, nh, chunk):
    nslab = dc // LANES

    @pl.when(pl.program_id(1) == 0)
    def _():
        buf[:, 0:CONV_PAD, :] = jnp.zeros((nslab, CONV_PAD, LANES), F32)
        st_ref[...] = jnp.zeros(st_ref.shape, F32)

    for j in range(nslab):
        buf[j, CONV_PAD:CONV_PAD + tl, :] = glu_ref[:, LANES * j:LANES * (j + 1)]
    first = CONV_PAD - CONV_BUF
    rows_per_iter = 8 * SUBLANES
    for j in range(nslab):
        cols = slice(LANES * j, LANES * (j + 1))
        wv = [jnp.broadcast_to(cw_ref[t:t + 1, cols], (SUBLANES, LANES)) for t in range(CONV_WIDTH)]
        bias = jnp.broadcast_to(cb_ref[0:1, cols], (SUBLANES, LANES))

        def body(r, carry, j=j, cols=cols, wv=wv, bias=bias):
            base = pl.multiple_of(r * rows_per_iter, rows_per_iter)
            for u in range(rows_per_iter // SUBLANES):
                acc = bias
                for t in range(CONV_WIDTH):
                    acc = acc + wv[t] * buf[j, pl.ds(base + (u * SUBLANES + first + t), SUBLANES), :]
                cscr[pl.ds(base + u * SUBLANES, SUBLANES), cols] = acc
            return carry

        lax.fori_loop(0, tl // rows_per_iter, body, 0)
    for j in range(nslab):
        buf[j, 0:CONV_PAD, :] = buf[j, tl:tl + CONV_PAD, :]
    cat_ref[:, 0:dc] = _layernorm_silu(cscr[...], lng_ref[...], lnb_ref[...]).astype(BF16)

    nt = (((1,), (1,)), ((), ()))
    tn = (((0,), (0,)), ((), ()))
    for c in range(tl // chunk):
        rows = slice(c * chunk, (c + 1) * chunk)
        for hh in range(nh):
            cols = slice(hh * hd, (hh + 1) * hd)
            qh = q_ref[rows, cols]
            kh = k_ref[rows, cols]
            vh = v_ref[rows, cols]
            s = st_ref[0, hh]
            scores = lax.dot_general(qh, kh, nt, preferred_element_type=F32) * mask_ref[hh]
            inner = jnp.dot(scores.astype(BF16), vh, preferred_element_type=F32)
            qd = (qh.astype(F32) * qd_ref[hh]).astype(BF16)
            cross = jnp.dot(qd, s.astype(BF16), preferred_element_type=F32)
            kd = (kh.astype(F32) * kd_ref[hh]).astype(BF16)
            st_ref[0, hh] = cd_ref[hh] * s + lax.dot_general(kd, vh, tn, preferred_element_type=F32)
            o = _groupnorm(inner + cross, rg_ref[0:1, cols], rb_ref[0:1, cols])
            cat_ref[rows, dc + hh * hd:dc + (hh + 1) * hd] = (o * sg_ref[rows, cols].astype(F32)).astype(BF16)


def _mix(glu, q, k, v, sg, conv_w, conv_b, ln_g, ln_b, rg, rb, tables, *, nb, length, dc, dr, hd):
    nh = dr // hd
    chunk = math.gcd(length, RET_CHUNK)
    tl = min(length, 256)
    nl = length // tl
    mask, qd, kd, cd = tables
    row_spec = lambda w: pl.BlockSpec((tl, w), lambda b, l: (b * nl + l, 0))
    full = lambda a: pl.BlockSpec(a.shape, lambda b, l: (0,) * a.ndim)
    cat, st = pl.pallas_call(
        functools.partial(_mix_body, tl=tl, dc=dc, hd=hd, nh=nh, chunk=chunk),
        grid=(nb, nl),
        in_specs=[row_spec(dc), row_spec(dr), row_spec(dr), row_spec(dr), row_spec(dr),
                  full(conv_w), full(conv_b), full(ln_g), full(ln_b), full(rg), full(rb),
                  full(mask), full(qd), full(kd), full(cd)],
        out_specs=[row_spec(dc + dr), pl.BlockSpec((1, nh, hd, hd), lambda b, l: (b, 0, 0, 0))],
        out_shape=[jax.ShapeDtypeStruct((nb * length, dc + dr), BF16),
                   jax.ShapeDtypeStruct((nb, nh, hd, hd), F32)],
        scratch_shapes=[pltpu.VMEM((dc // LANES, tl + CONV_PAD, LANES), F32),
                        pltpu.VMEM((tl, dc), F32)],
        compiler_params=pltpu.CompilerParams(dimension_semantics=("arbitrary", "arbitrary")),
        name="mix",
    )(glu, q, k, v, sg, conv_w, conv_b, ln_g, ln_b, rg, rb, mask, qd, kd, cd)
    return cat, st


def _mix1_body(glu_ref, q_ref, k_ref, v_ref, sg_ref, sc_ref, s0_ref, cw_ref, cb_ref, lng_ref, lnb_ref,
               rg_ref, rb_ref, gam_ref, cat_ref, st_ref, nc_ref, cscr, oscr, qf, kf, vf, *, tb, dc, hd, nh):
    w_hist = cw_ref[0:CONV_BUF, :]
    w_last = cw_ref[CONV_BUF:CONV_WIDTH, :]
    for bb in range(tb):
        hist = jnp.sum(sc_ref[bb] * w_hist, axis=0, keepdims=True)
        cscr[bb:bb + 1, :] = hist + glu_ref[bb:bb + 1, :] * w_last + cb_ref[...]
        nc_ref[bb, 0:CONV_BUF - 1, :] = sc_ref[bb, 1:CONV_BUF, :]
        nc_ref[bb, CONV_BUF - 1:CONV_BUF, :] = glu_ref[bb:bb + 1, :]
    cat_ref[:, 0:dc] = _layernorm_silu(cscr[...], lng_ref[...], lnb_ref[...]).astype(BF16)

    tn = (((0,), (0,)), ((), ()))
    rowid = lax.broadcasted_iota(I32, (tb, hd), 0)
    qf[...] = q_ref[...].astype(F32)
    kf[...] = k_ref[...].astype(F32)
    vf[...] = v_ref[...].astype(F32)
    for hh in range(nh):
        cols = slice(hh * hd, (hh + 1) * hd)
        qa = q_ref[:, cols]
        ka = k_ref[:, cols]
        gam = gam_ref[hh, 0:1, :]
        for bb in range(tb):
            onehot = (rowid == bb).astype(BF16)
            qcol = lax.dot_general(qa, onehot, tn, preferred_element_type=F32)
            kcol = lax.dot_general(ka, onehot, tn, preferred_element_type=F32)
            s0 = s0_ref[bb, hh]
            qrow = qf[bb:bb + 1, cols]
            krow = kf[bb:bb + 1, cols]
            vrow = vf[bb:bb + 1, cols]
            qk = jnp.sum(qrow * krow, axis=-1, keepdims=True)
            cross = gam * jnp.sum(qcol * s0, axis=0, keepdims=True)
            st_ref[bb, hh] = gam * s0 + kcol * vrow
            oscr[bb:bb + 1, cols] = qk * vrow + cross
    for hh in range(nh):
        cols = slice(hh * hd, (hh + 1) * hd)
        o = _groupnorm(oscr[:, cols], rg_ref[0:1, cols], rb_ref[0:1, cols])
        cat_ref[:, dc + hh * hd:dc + (hh + 1) * hd] = (o * sg_ref[:, cols].astype(F32)).astype(BF16)


def _mix1(glu, q, k, v, sg, state_conv, state_ret, conv_w, conv_b, ln_g, ln_b, rg, rb, gam, *, dc, dr, hd):
    nb = glu.shape[0]
    nh = dr // hd
    tb = 16
    row_spec = lambda w: pl.BlockSpec((tb, w), lambda i: (i, 0))
    full = lambda a: pl.BlockSpec(a.shape, lambda i: (0,) * a.ndim)
    st_spec = pl.BlockSpec((None, tb, nh, hd, hd), lambda i: (0, i, 0, 0, 0))
    conv_spec = pl.BlockSpec((None, tb, CONV_BUF, dc), lambda i: (0, i, 0, 0))
    cat, st, new_conv = pl.pallas_call(
        functools.partial(_mix1_body, tb=tb, dc=dc, hd=hd, nh=nh),
        grid=(nb // tb,),
        in_specs=[row_spec(dc), row_spec(dr), row_spec(dr), row_spec(dr), row_spec(dr), conv_spec, st_spec,
                  full(conv_w), full(conv_b), full(ln_g), full(ln_b), full(rg), full(rb), full(gam)],
        out_specs=[row_spec(dc + dr), st_spec, conv_spec],
        out_shape=[jax.ShapeDtypeStruct((nb, dc + dr), BF16),
                   jax.ShapeDtypeStruct((1, nb, nh, hd, hd), F32),
                   jax.ShapeDtypeStruct((1, nb, CONV_BUF, dc), F32)],
        scratch_shapes=[pltpu.VMEM((tb, dc), F32)] + [pltpu.VMEM((tb, dr), F32)] * 4,
        compiler_params=pltpu.CompilerParams(vmem_limit_bytes=VMEM_LIMIT),
        name="mix1",
    )(glu, q, k, v, sg, state_conv, state_ret, conv_w, conv_b, ln_g, ln_b, rg, rb, gam)
    return cat, st, new_conv


def _post_body(cat_ref, x_ref, gtm_ref, scf_ref, shf_ref, g_ref, wo_ref, wrh_ref, wrl_ref, x1_ref, hp_ref, lg_ref):
    d = x_ref.shape[1]
    y = jnp.dot(cat_ref[...], wo_ref[...], preferred_element_type=F32)
    x1 = x_ref[...] + gtm_ref[...] * y
    x1_ref[...] = x1
    h = _modulated_rmsnorm(x1, g_ref[...], scf_ref[...], shf_ref[...])
    hb = h.astype(BF16)
    _store_rows(hp_ref, _pack_halves(h[:, 0:d // 2], h[:, d // 2:d]))
    hl = (h - hb.astype(F32)).astype(BF16)
    nt = (((1,), (1,)), ((), ()))
    lg_ref[...] = (lax.dot_general(wrh_ref[...], hb, nt, preferred_element_type=F32)
                   + lax.dot_general(wrh_ref[...], hl, nt, preferred_element_type=F32)
                   + lax.dot_general(wrl_ref[...], hb, nt, preferred_element_type=F32))


def _post(cat, x, mod, g_ffn, wo_b, wrh, wrl, *, per_row_mod, b0=0, nb=None):
    nb_all, length, d = x.shape
    nb = nb_all if nb is None else nb
    tl = min(length, 1024)
    nl = length // tl
    t = nb * length
    ne = wrh.shape[0]
    x2d = x.reshape(nb_all * length, d)
    if per_row_mod:
        mod_spec = lambda j: pl.BlockSpec((tl, d), lambda b, l: (l, j))
    else:
        mod_spec = lambda j: pl.BlockSpec((None, None, 1, d), lambda b, l: (b + b0, j, 0, 0))
    row_spec = lambda w: pl.BlockSpec((tl, w), lambda b, l: (b * nl + l, 0))
    full = lambda a: pl.BlockSpec(a.shape, lambda b, l: (0,) * a.ndim)
    return pl.pallas_call(
        _post_body,
        grid=(nb, nl),
        in_specs=[row_spec(d), pl.BlockSpec((tl, d), lambda b, l: ((b + b0) * nl + l, 0)),
                  mod_spec(2), mod_spec(4), mod_spec(3),
                  full(g_ffn), full(wo_b), full(wrh), full(wrl)],
        out_specs=[row_spec(d),
                   pl.BlockSpec((tl * ROW_WORDS, LANES), lambda b, l: (b * nl + l, 0)),
                   pl.BlockSpec((ne, tl), lambda b, l: (0, b * nl + l))],
        out_shape=[jax.ShapeDtypeStruct((t, d), F32),
                   jax.ShapeDtypeStruct((t * ROW_WORDS, LANES), U32),
                   jax.ShapeDtypeStruct((ne, t), F32)],
        compiler_params=pltpu.CompilerParams(vmem_limit_bytes=VMEM_LIMIT),
        name="post",
    )(cat, x2d, mod, mod, mod, g_ffn, wo_b, wrh, wrl)


def _first_max(x, idx, sentinel):
    m = jnp.max(x, axis=0, keepdims=True)
    f = jnp.min(jnp.where(x == m, idx, sentinel), axis=0, keepdims=True)
    return m, f


def _route_body(lg_ref, bias_ref, e_ref, w_ref, r_ref, cnt_ref, cnt_scr, *, tr, ne, ng, topk, topg):
    @pl.when(pl.program_id(0) == 0)
    def _():
        cnt_scr[...] = jnp.zeros(cnt_scr.shape, F32)

    per = ne // ng
    neg = -jnp.inf
    scores = _sigmoid(lg_ref[...])
    sel = scores + bias_ref[...]
    sub = lax.broadcasted_iota(I32, (per, tr), 0)
    gs = []
    for g in range(ng):
        s_g = sel[g * per:(g + 1) * per, :]
        m1, f1 = _first_max(s_g, sub, per)
        m2 = jnp.max(jnp.where(sub == f1, neg, s_g), axis=0, keepdims=True)
        gs.append(m1 + m2)
    gsc = jnp.concatenate(gs, axis=0)
    gi = lax.broadcasted_iota(I32, (ng, tr), 0)
    keep = jnp.zeros((ng, tr), F32)
    for _ in range(topg):
        _, f = _first_max(gsc, gi, ng)
        pick = gi == f
        keep = jnp.where(pick, 1.0, keep)
        gsc = jnp.where(pick, neg, gsc)
    work = jnp.concatenate(
        [jnp.where(keep[g:g + 1, :] > 0.5, sel[g * per:(g + 1) * per, :], neg) for g in range(ng)], axis=0)
    ei = lax.broadcasted_iota(I32, (ne, tr), 0)
    picks, es, ws = [], [], []
    for _ in range(topk):
        _, f = _first_max(work, ei, ne)
        pick = ei == f
        picks.append(pick)
        es.append(f)
        ws.append(jnp.sum(jnp.where(pick, scores, 0.0), axis=0, keepdims=True))
        work = jnp.where(pick, neg, work)
    wsum = ws[0]
    for w in ws[1:]:
        wsum = wsum + w
    scale = ROUTED_SCALE / wsum
    chosen = picks[0]
    for p in picks[1:]:
        chosen = jnp.logical_or(chosen, p)
    chosen_f = chosen.astype(F32)
    t_row = lax.broadcasted_iota(I32, (tr, tr), 0)
    t_col = lax.broadcasted_iota(I32, (tr, tr), 1)
    before = (t_row < t_col).astype(BF16)
    prior = cnt_scr[:, 0:1] + jnp.dot(chosen_f.astype(BF16), before, preferred_element_type=F32)
    rs = [jnp.sum(jnp.where(p, prior, 0.0), axis=0, keepdims=True).astype(I32) for p in picks]
    pad_i = jnp.zeros((SUBLANES - topk, tr), I32)
    pad_f = jnp.zeros((SUBLANES - topk, tr), F32)
    e_ref[...] = jnp.concatenate(es + [pad_i], axis=0)
    w_ref[...] = jnp.concatenate([w * scale for w in ws] + [pad_f], axis=0)
    r_ref[...] = jnp.concatenate(rs + [pad_i], axis=0)
    total = cnt_scr[:, 0:1] + jnp.sum(chosen_f, axis=1, keepdims=True)
    cnt_scr[...] = jnp.broadcast_to(total, cnt_scr.shape)
    cnt_ref[...] = jnp.broadcast_to(total, cnt_ref.shape)


def _route_tile(t):
    return max(m for m in range(LANES, ROUTE_TILE + 1, LANES) if t % m == 0)


def _route(logits_t, bias_col):
    ne, t = logits_t.shape
    tr = _route_tile(t)
    tok = lambda dt: jax.ShapeDtypeStruct((SUBLANES, t), dt)
    tok_spec = pl.BlockSpec((SUBLANES, tr), lambda i: (0, i))
    return pl.pallas_call(
        functools.partial(_route_body, tr=tr, ne=ne, ng=N_GROUPS, topk=TOP_K, topg=TOPK_GROUPS),
        grid=(t // tr,),
        in_specs=[pl.BlockSpec((ne, tr), lambda i: (0, i)), pl.BlockSpec((ne, 1), lambda i: (0, 0))],
        out_specs=[tok_spec, tok_spec, tok_spec, pl.BlockSpec((ne, LANES), lambda i: (0, 0))],
        out_shape=[tok(I32), tok(F32), tok(I32), jax.ShapeDtypeStruct((ne, LANES), F32)],
        scratch_shapes=[pltpu.VMEM((ne, LANES), F32)],
        compiler_params=pltpu.CompilerParams(dimension_semantics=("arbitrary",)),
        name="route",
    )(logits_t, bias_col)


def _dest_body(cnt_ref, e_ref, r_ref, d_ref, ps, *, ne, tr):
    shift = EXPERT_ROWS.bit_length() - 1

    @pl.when(pl.program_id(0) == 0)
    def _():
        def step(j, start):
            ps[j] = start
            return start + lax.shift_left(lax.shift_right_logical(cnt_ref[j] + (EXPERT_ROWS - 1), shift), shift)

        lax.fori_loop(0, ne, step, jnp.int32(0))

    e = e_ref[...]
    base = jnp.zeros(e.shape, I32)
    for j in range(ne):
        base = jnp.where(e == j, ps[j], base)
    dest = base + r_ref[...]
    for m in range(tr // TOKEN_TILE):
        d_ref[SUBLANES * m:SUBLANES * (m + 1), :] = dest[:, TOKEN_TILE * m:TOKEN_TILE * (m + 1)]


def _dest_rows(counts, eidx, rank):
    rows, t = eidx.shape
    ne = counts.shape[0]
    tr = _route_tile(t)
    spec = pl.BlockSpec((rows, tr), lambda i, cnt: (0, i))
    return pl.pallas_call(
        functools.partial(_dest_body, ne=ne, tr=tr),
        grid_spec=pltpu.PrefetchScalarGridSpec(
            num_scalar_prefetch=1, grid=(t // tr,), in_specs=[spec, spec],
            out_specs=pl.BlockSpec((tr // TOKEN_TILE * rows, TOKEN_TILE), lambda i, cnt: (i, 0)),
            scratch_shapes=[pltpu.SMEM((ne,), I32)]),
        out_shape=jax.ShapeDtypeStruct((t // TOKEN_TILE * rows, TOKEN_TILE), I32),
        compiler_params=pltpu.CompilerParams(dimension_semantics=("arbitrary",)),
        name="dest",
    )(counts, eidx, rank)


def _sc_mesh():
    return plsc.VectorSubcoreMesh(core_axis_name="c", subcore_axis_name="s")


def _sc_worker_id():
    return lax.axis_index("s") * SC_CORES + lax.axis_index("c")


def _index_block(dest_ref, chunk, width):
    per_tile = TOKEN_TILE // width
    return dest_ref.at[chunk // per_tile, :, pl.ds((chunk % per_tile) * width, width)]


def _dispatch(hps, dest3, n_rows):
    w = SCATTER_ROWS
    width = hps[0].shape[1]
    bounds = [0]
    for h in hps:
        bounds.append(bounds[-1] + h.shape[0] // w)
    nch = bounds[-1]
    nsrc = len(hps)

    @functools.partial(
        pl.kernel, mesh=_sc_mesh(),
        out_type=jax.ShapeDtypeStruct((n_rows, width), U32),
        scratch_types=[pltpu.VMEM((2, SUBLANES, w), I32), pltpu.VMEM((2, w, width), U32),
                       pltpu.SemaphoreType.DMA, pltpu.SemaphoreType.DMA, pltpu.SemaphoreType.DMA],
        compiler_params=pltpu.CompilerParams(use_tc_tiling_on_sc=False),
        name="dispatch",
    )
    def run(*refs):
        src_refs, dest_ref, xs_ref = refs[:nsrc], refs[nsrc], refs[nsrc + 1]
        idx_v, rows_v = refs[nsrc + 2:nsrc + 4]
        sem_load = refs[nsrc + 4:nsrc + 6]
        sem_scatter = refs[nsrc + 6]
        wid = _sc_worker_id()

        def start_loads(c, slot):
            pltpu.async_copy(_index_block(dest_ref, c, w), idx_v.at[slot], sem_load[slot])
            for i, src in enumerate(src_refs):
                @pl.when(jnp.logical_and(c >= bounds[i], c < bounds[i + 1]))
                def _(src=src, lo=bounds[i]):
                    pltpu.async_copy(src.at[pl.ds((c - lo) * w, w)], rows_v.at[slot], sem_load[slot])

        def wait_loads(slot):
            pltpu.make_async_copy(_index_block(dest_ref, 0, w), idx_v.at[slot], sem_load[slot]).wait()
            pltpu.make_async_copy(src_refs[0].at[pl.ds(0, w)], rows_v.at[slot], sem_load[slot]).wait()

        @pl.when(wid < nch)
        def _():
            start_loads(wid, 0)

        @pl.loop(0, pl.cdiv(pl.cdiv(nch, SC_WORKERS), 2))
        def _(rr):
            for slot in range(2):
                c = (rr * 2 + slot) * SC_WORKERS + wid

                @pl.when(c < nch)
                def _(c=c, slot=slot):
                    wait_loads(slot)

                    @pl.when(c + SC_WORKERS < nch)
                    def _():
                        start_loads(c + SC_WORKERS, 1 - slot)

                    copies = [pltpu.async_copy(rows_v.at[slot], xs_ref.at[idx_v.at[slot, k]], sem_scatter)
                              for k in range(TOP_K)]
                    for cp in copies:
                        cp.wait()

    return run(*hps, dest3)


def _undispatch(ys, dest3, n_tokens, after=None):
    w = GATHER_ROWS
    width = ys.shape[1]
    nch = n_tokens // w

    @functools.partial(
        pl.kernel, mesh=_sc_mesh(),
        out_type=jax.ShapeDtypeStruct((TOP_K, n_tokens, width), U32),
        scratch_types=[pltpu.VMEM((2, SUBLANES, w), I32), pltpu.VMEM((2, TOP_K, w, width), U32),
                       pltpu.SemaphoreType.DMA, pltpu.SemaphoreType.DMA, pltpu.SemaphoreType.DMA],
        compiler_params=pltpu.CompilerParams(use_tc_tiling_on_sc=False),
        name="undispatch",
    )
    def run(*refs):
        ys_ref, dest_ref = refs[:2]
        z_ref, idx_v, bufs = refs[-6:-3]
        sem_gather = refs[-3:-1]
        sem_store = refs[-1]
        wid = _sc_worker_id()

        def start_gathers(c, slot):
            pltpu.sync_copy(_index_block(dest_ref, c, w), idx_v.at[slot])
            for k in range(TOP_K):
                pltpu.async_copy(ys_ref.at[idx_v.at[slot, k]], bufs.at[slot, k], sem_gather[slot])

        def wait_gathers(slot):
            for k in range(TOP_K):
                pltpu.make_async_copy(ys_ref.at[idx_v.at[slot, k]], bufs.at[slot, k], sem_gather[slot]).wait()

        @pl.when(wid < nch)
        def _():
            start_gathers(wid, 0)

        @pl.loop(0, pl.cdiv(pl.cdiv(nch, SC_WORKERS), 2))
        def _(rr):
            for slot in range(2):
                c = (rr * 2 + slot) * SC_WORKERS + wid

                @pl.when(c < nch)
                def _(c=c, slot=slot):
                    @pl.when(c + SC_WORKERS < nch)
                    def _():
                        start_gathers(c + SC_WORKERS, 1 - slot)

                    wait_gathers(slot)
                    stores = [pltpu.async_copy(bufs.at[slot, k], z_ref.at[k, pl.ds(c * w, w)], sem_store)
                              for k in range(TOP_K)]
                    for cp in stores:
                        cp.wait()

    return run(ys, dest3) if after is None else run(ys, dest3, after)


PART_SHIFT = 24


def _expert_body(*refs, tm, ne, nparts):
    cnt_refs = refs[:nparts]
    xs_refs = refs[nparts:2 * nparts]
    w1_ref, w3_ref, w2_ref = refs[2 * nparts:2 * nparts + 3]
    ys_refs = refs[2 * nparts + 3:3 * nparts + 3]
    (w1f, w3f, w2f, w1s, w3s, w2s, xbuf, ybuf, xlo, xhi, sched, sem_x, sem_y, sem_w) = refs[3 * nparts + 3:]
    blk_words = tm * ROW_WORDS
    half = ROW_WORDS * LANES
    shift = tm.bit_length() - 1

    def n_blocks_of(p, e):
        return lax.shift_right_logical(cnt_refs[p][e] + (tm - 1), shift)

    def n_all(e):
        n = n_blocks_of(0, e)
        for p in range(1, nparts):
            n = n + n_blocks_of(p, e)
        return n

    def next_nonempty(e):
        return lax.while_loop(
            lambda c: jnp.logical_and(c < ne, n_all(jnp.minimum(c, ne - 1)) == 0), lambda c: c + 1, e)

    def plan(e, carry):
        i, starts = carry[0], list(carry[1:])
        for p in range(nparts):
            n = n_blocks_of(p, e)

            def put(j, c, p=p, i=i, start=starts[p]):
                sched[i + j] = (start + j) + (p << PART_SHIFT)
                return c

            lax.fori_loop(0, n, put, 0)
            i = i + n
            starts[p] = starts[p] + n
        return (i, *starts)

    nu = lax.fori_loop(0, ne, plan, (jnp.int32(0),) * (nparts + 1))[0]

    def rows_of(code):
        blk = code & ((1 << PART_SHIFT) - 1)
        return pl.ds(pl.multiple_of(blk * blk_words, blk_words), blk_words)

    def x_start(i, slot):
        code = sched[i]
        for p in range(nparts):
            @pl.when(lax.shift_right_logical(code, PART_SHIFT) == p)
            def _(p=p):
                pltpu.make_async_copy(xs_refs[p].at[rows_of(code), :], xbuf.at[slot], sem_x.at[slot]).start()

    def x_wait(slot):
        pltpu.make_async_copy(xs_refs[0].at[pl.ds(0, blk_words), :], xbuf.at[slot], sem_x.at[slot]).wait()

    def y_start(i, slot):
        code = sched[i]
        for p in range(nparts):
            @pl.when(lax.shift_right_logical(code, PART_SHIFT) == p)
            def _(p=p):
                pltpu.make_async_copy(ybuf.at[slot], ys_refs[p].at[rows_of(code), :], sem_y.at[slot]).start()

    def y_wait(slot):
        pltpu.make_async_copy(ybuf.at[slot], ys_refs[0].at[pl.ds(0, blk_words), :], sem_y.at[slot]).wait()

    def w_copies(e, ws):
        return [pltpu.make_async_copy(src.at[e], dst.at[ws], sem_w.at[ws])
                for src, dst in ((w1_ref, w1f), (w3_ref, w3f), (w2_ref, w2f))]

    for q in range(X_AHEAD):
        @pl.when(q < nu)
        def _(q=q):
            x_start(q, q)

    e_first = next_nonempty(jnp.int32(0))

    @pl.when(e_first < ne)
    def _():
        for cp in w_copies(e_first, 0):
            cp.start()

    def blocks(i, nblk):
        for b in range(nblk):
            x_wait((i + b) % X_SLOTS)
        for b in range(nblk):
            nxt = i + b + X_AHEAD

            @pl.when(nxt < nu)
            def _(nxt=nxt):
                x_start(nxt, nxt % X_SLOTS)

        for b in range(nblk):
            @pl.when(i + b >= Y_SLOTS)
            def _(b=b):
                y_wait((i + b) % Y_SLOTS)

        for b in range(nblk):
            xin = xbuf.at[(i + b) % X_SLOTS]
            for w in range(ROW_WORDS):
                lo, hi = _unpack_halves(_load_row_word(xin, w, tm))
                xlo[b, :, LANES * w:LANES * (w + 1)] = lo.astype(BF16)
                xhi[b, :, LANES * w:LANES * (w + 1)] = hi.astype(BF16)

        def up(b, wsc):
            return (jnp.dot(xlo[b], wsc[0:half, :], preferred_element_type=F32)
                    + jnp.dot(xhi[b], wsc[half:2 * half, :], preferred_element_type=F32))

        for b in range(nblk):
            hid = (_silu(up(b, w1s)) * up(b, w3s)).astype(BF16)
            y = jnp.dot(hid, w2s[...], preferred_element_type=F32)
            _store_rows(ybuf.at[(i + b) % Y_SLOTS], _pack_halves(y[:, 0:half], y[:, half:2 * half]))
        for b in range(nblk):
            y_start(i + b, (i + b) % Y_SLOTS)

    def per_expert(e, carry):
        i0, ws = carry
        n = n_all(e)

        @pl.when(n > 0)
        def _():
            for cp in w_copies(e, ws):
                cp.wait()
            w1s[...] = w1f[ws].astype(BF16)
            w3s[...] = w3f[ws].astype(BF16)
            w2s[...] = w2f[ws].astype(BF16)
            e_next = next_nonempty(e + 1)

            @pl.when(e_next < ne)
            def _():
                for cp in w_copies(e_next, 1 - ws):
                    cp.start()

            def pair(j, c):
                blocks(i0 + 2 * j, 2)
                return c

            lax.fori_loop(0, lax.shift_right_logical(n, 1), pair, 0)

            @pl.when(n % 2 == 1)
            def _():
                blocks(i0 + n - 1, 1)

        return i0 + n, jnp.where(n > 0, 1 - ws, ws)

    lax.fori_loop(0, ne, per_expert, (jnp.int32(0), jnp.int32(0)))

    for q in range(Y_SLOTS):
        @pl.when(nu > q)
        def _(q=q):
            y_wait((nu - 1 - q) % Y_SLOTS)


def _experts(counts, xss, w1e, w3e, w2e):
    tm = EXPERT_ROWS
    ne, d, de = w1e.shape
    half = ROW_WORDS * LANES
    nparts = len(xss)
    cap = sum(x.shape[0] // (tm * ROW_WORDS) for x in xss)
    assert cap < (1 << PART_SHIFT)
    anyspec = pl.BlockSpec(memory_space=pl.ANY)
    blk_buf = lambda n: pltpu.VMEM((n, tm * ROW_WORDS, LANES), U32)
    return pl.pallas_call(
        functools.partial(_expert_body, tm=tm, ne=ne, nparts=nparts),
        grid_spec=pltpu.PrefetchScalarGridSpec(
            num_scalar_prefetch=nparts,
            grid=(1,),
            in_specs=[anyspec] * (nparts + 3),
            out_specs=[anyspec] * nparts,
            scratch_shapes=[pltpu.VMEM((2, d, de), F32), pltpu.VMEM((2, d, de), F32), pltpu.VMEM((2, de, d), F32),
                            pltpu.VMEM((d, de), BF16), pltpu.VMEM((d, de), BF16), pltpu.VMEM((de, d), BF16),
                            blk_buf(X_SLOTS), blk_buf(Y_SLOTS),
                            pltpu.VMEM((2, tm, half), BF16), pltpu.VMEM((2, tm, half), BF16),
                            pltpu.SMEM((cap,), I32),
                            pltpu.SemaphoreType.DMA((X_SLOTS,)), pltpu.SemaphoreType.DMA((Y_SLOTS,)),
                            pltpu.SemaphoreType.DMA((2,))]),
        out_shape=[jax.ShapeDtypeStruct(x.shape, U32) for x in xss],
        compiler_params=pltpu.CompilerParams(dimension_semantics=("arbitrary",), vmem_limit_bytes=VMEM_LIMIT),
        name="experts",
    )(*counts, *xss, w1e, w3e, w2e)


def _combine_body(z_ref, x1_ref, hp_ref, gtf_ref, wt_ref, gfin_ref, ws1_ref, ws3_ref, ws2_ref, y_ref, xo, hlo, hhi,
                  *, td, topk):
    half = ROW_WORDS * LANES
    for w in range(ROW_WORDS):
        lo, hi = _unpack_halves(_load_row_word(hp_ref, w, td))
        hlo[:, LANES * w:LANES * (w + 1)] = lo.astype(BF16)
        hhi[:, LANES * w:LANES * (w + 1)] = hi.astype(BF16)

    def up(w_ref):
        return (jnp.dot(hlo[...], w_ref[0:half, :], preferred_element_type=F32)
                + jnp.dot(hhi[...], w_ref[half:2 * half, :], preferred_element_type=F32))

    xo[...] = jnp.dot((_silu(up(ws1_ref)) * up(ws3_ref)).astype(BF16), ws2_ref[...], preferred_element_type=F32)
    wt = wt_ref[...].T
    ws = [wt[:, k:k + 1] for k in range(topk)]
    sq = jnp.zeros((td, 1), F32)
    for j in range(ROW_WORDS):
        acc_lo = jnp.zeros((td, LANES), F32)
        acc_hi = jnp.zeros((td, LANES), F32)
        for k in range(topk):
            lo, hi = _unpack_halves(_load_row_word(z_ref.at[k], j, td))
            acc_lo = acc_lo + ws[k] * lo
            acc_hi = acc_hi + ws[k] * hi
        for base, acc in ((0, acc_lo), (half, acc_hi)):
            cols = slice(base + LANES * j, base + LANES * (j + 1))
            x = x1_ref[:, cols] + gtf_ref[:, cols] * (acc + xo[:, cols])
            xo[:, cols] = x
            sq = sq + jnp.sum(x * x, axis=-1, keepdims=True)
    rs = lax.rsqrt(sq / (2 * half) + EPS)
    y_ref[...] = xo[...] * rs * gfin_ref[...]


def _combine_body_into(*refs, td, topk):
    _combine_body(*refs[:9], *refs[10:], td=td, topk=topk)


def _combine(z, token0, wts, x1, hp, mod, g_final, shared_w, *, rows_per_mod, per_row_mod, b0=0, out_rows=None,
             into=None):
    t, d = x1.shape
    td = min(t, COMBINE_TILE)
    tile0 = token0 // td
    out_rows = t if out_rows is None else out_rows
    if per_row_mod:
        gtf_spec = pl.BlockSpec((td, d), lambda i: (i, 5))
        out0 = 0
    else:
        tiles_per_mod = rows_per_mod // td
        gtf_spec = pl.BlockSpec((None, None, 1, d), lambda i: (i // tiles_per_mod + b0, 5, 0, 0))
        out0 = b0 * tiles_per_mod
    full = lambda a: pl.BlockSpec(a.shape, lambda i: (0,) * a.ndim)
    in_specs = [pl.BlockSpec((TOP_K, td * ROW_WORDS, LANES), lambda i: (0, i + tile0, 0)),
                pl.BlockSpec((td, d), lambda i: (i, 0)),
                pl.BlockSpec((td * ROW_WORDS, LANES), lambda i: (i, 0)),
                gtf_spec,
                pl.BlockSpec((SUBLANES, td), lambda i: (0, i + tile0)),
                pl.BlockSpec((1, d), lambda i: (0, 0))] + [full(w) for w in shared_w]
    args = [z, x1, hp, mod, wts, g_final, *shared_w]
    body, aliases = _combine_body, {}
    if into is not None:
        in_specs.append(pl.BlockSpec(memory_space=pl.ANY))
        args.append(into)
        body, aliases = _combine_body_into, {len(args) - 1: 0}
    return pl.pallas_call(
        functools.partial(body, td=td, topk=TOP_K),
        grid=(t // td,),
        in_specs=in_specs,
        out_specs=pl.BlockSpec((td, d), lambda i: (i + out0, 0)),
        scratch_shapes=[pltpu.VMEM((td, d), F32), pltpu.VMEM((td, d // 2), BF16), pltpu.VMEM((td, d // 2), BF16)],
        out_shape=jax.ShapeDtypeStruct((out_rows, d), F32),
        input_output_aliases=aliases,
        compiler_params=pltpu.CompilerParams(vmem_limit_bytes=VMEM_LIMIT),
        name="combine",
    )(*args)


def _log_gamma(nh):
    return np.log(1.0 - 2.0 ** (-5.0 - np.arange(nh, dtype=np.float32))).astype(np.float32)


def _retention_tables(length, nh, hd):
    c = math.gcd(length, RET_CHUNK)
    log_g = _log_gamma(nh)
    idx = np.arange(c, dtype=np.float32)
    rel = idx[:, None] - idx[None, :]
    mask = np.where(rel >= 0, np.exp(log_g[:, None, None] * np.maximum(rel, 0.0)), 0.0).astype(np.float32)
    q_decay = np.exp(log_g[None, :] * (idx[:, None] + 1.0)).astype(np.float32)
    k_decay = np.exp(log_g[None, :] * (c - 1.0 - idx[:, None])).astype(np.float32)
    chunk_decay = np.exp(log_g * np.float32(c)).astype(np.float32)
    qd = np.broadcast_to(q_decay.T[:, :, None], (nh, c, hd))
    kd = np.broadcast_to(k_decay.T[:, :, None], (nh, c, hd))
    cd = np.broadcast_to(chunk_decay[:, None, None], (nh, hd, hd))
    return tuple(jnp.asarray(t) for t in (mask, qd, kd, cd))


def kernel(x_prompt, x_sample, c_prompt, c_sample, state_conv, state_ret, w_ada, b_ada, g_mix, g_ffn, w_in,
           conv_w, conv_b, conv_norm_g, conv_norm_b, ret_norm_g, ret_norm_b, w_out, w_router, router_bias,
           w1, w3, w2, ws1, ws3, ws2, g_final):
    depth = w_ada.shape[0]
    assert depth == 1, "single-layer trunk"
    bp, lp, d = x_prompt.shape
    bs, ls, _ = x_sample.shape
    assert ls == 1
    dc = conv_w.shape[2]
    dr = ret_norm_g.shape[1]
    nh = RET_HEADS
    hd = dr // nh
    assert hd == LANES and lp % 256 == 0 and bs % TOKEN_TILE == 0 and d // 2 == ROW_WORDS * LANES
    ne = w_router.shape[2]
    row = lambda a: a.reshape(1, -1)

    mod = _ada(jnp.concatenate([c_prompt, c_sample], axis=0), w_ada[0], row(b_ada[0]))
    mod_p = mod[:bp].reshape(bp, 6, 1, d)
    mod_s = mod[bp:]

    half = hd // 2
    inv = (np.float32(ROPE_BASE) ** (-np.arange(half, dtype=np.float32) / np.float32(half))).astype(np.float32)
    inv2 = jnp.asarray(np.concatenate([inv, inv]).reshape(1, hd))
    cos_p, sin_p = _rope_tables(inv2, lp, 0)
    cos_s, sin_s = _rope_tables(inv2, SUBLANES, PAST_LEN)

    w_in_b = w_in[0].astype(BF16)
    wo_b = w_out[0].astype(BF16)
    wr_t = w_router[0].T
    wrh = wr_t.astype(BF16)
    wrl = (wr_t - wrh.astype(F32)).astype(BF16)
    ws1_b, ws3_b, ws2_b = ws1[0].astype(BF16), ws3[0].astype(BF16), ws2[0].astype(BF16)
    dims = dict(dc=dc, dr=dr, hd=hd)

    tables = _retention_tables(lp, nh, hd)
    gam = jnp.asarray(np.broadcast_to(np.exp(_log_gamma(nh))[:, None, None], (nh, SUBLANES, hd)))
    norm_rows = (row(conv_b[0]), row(conv_norm_g[0]), row(conv_norm_b[0]), row(ret_norm_g[0]), row(ret_norm_b[0]))
    post_w = (row(g_ffn[0]), wo_b, wrh, wrl)
    shared_w = (ws1_b, ws3_b, ws2_b)
    bias_col = router_bias[0].reshape(ne, 1)
    hw = d // 2
    tm = EXPERT_ROWS

    def pre_prompt(b0, nb, after=None):
        glu, q, k, v, sg = _proj(x_prompt, mod_p, row(g_mix[0]), w_in_b, cos_p, sin_p,
                                 per_row_mod=False, b0=b0, nb=nb, after=after, **dims)
        cat, ret = _mix(glu, q, k, v, sg, conv_w[0], *norm_rows, tables, nb=nb, length=lp, **dims)
        x2, hp, lg = _post(cat, x_prompt, mod_p, *post_w, per_row_mod=False, b0=b0, nb=nb)
        return glu, ret, x2, hp, lg

    def pre_sample():
        xs3 = x_sample.reshape(1, bs, d)
        glu, q, k, v, sg = _proj(xs3, mod_s, row(g_mix[0]), w_in_b, cos_s, sin_s, per_row_mod=True, **dims)
        cat, ret, new_conv = _mix1(glu, q, k, v, sg, state_conv, state_ret, conv_w[0], *norm_rows, gam,
                                   **dims)
        x2, hp, lg = _post(cat, xs3, mod_s, *post_w, per_row_mod=True)
        return new_conv, ret, x2, hp, lg

    def route_and_dispatch(hps, lgs):
        lg = lgs[0] if len(lgs) == 1 else jnp.concatenate(lgs, axis=1)
        tokens = lg.shape[1]
        eidx, wts, rank, cnt = _route(lg, bias_col)
        counts = cnt[:, 0].astype(I32)
        n_rows = -(-(tokens * TOP_K + ne * (tm - 1)) // tm) * tm
        dest3 = _dest_rows(counts, eidx, rank).reshape(tokens // TOKEN_TILE, SUBLANES, TOKEN_TILE)
        xs = _dispatch([h.reshape(-1, hw) for h in hps], dest3, n_rows)
        return counts, dest3, wts, xs.reshape(n_rows * ROW_WORDS, LANES)

    def undispatch(ys, dest3, after=None):
        tokens = dest3.shape[0] * TOKEN_TILE
        z = _undispatch(ys.reshape(-1, hw), dest3, tokens, after)
        return z.reshape(TOP_K, tokens * ROW_WORDS, LANES)

    nb0 = bp // 2
    nb1 = bp - nb0
    new_conv_s, ret_s, x2_s, hp_s, lg_s = pre_sample()
    glu_0, ret_0, x2_0, hp_0, lg_0 = pre_prompt(0, nb0, after=lg_s)
    counts_0, dest_0, wts_0, xs_0 = route_and_dispatch([hp_0], [lg_0])
    glu_1, ret_1, x2_1, hp_1, lg_1 = pre_prompt(nb0, nb1, after=dest_0)
    counts_1, dest_1, wts_1, xs_1 = route_and_dispatch([hp_1, hp_s], [lg_1, lg_s])
    ys_0, ys_1 = _experts((counts_0, counts_1), (xs_0, xs_1), w1[0], w3[0], w2[0])
    z_0 = undispatch(ys_0, dest_0)
    z_1 = undispatch(ys_1, dest_1, after=z_0[0, :SUBLANES])
    y_p = _combine(z_0, 0, wts_0, x2_0, hp_0, mod_p, row(g_final), shared_w, rows_per_mod=lp, per_row_mod=False,
                   out_rows=bp * lp)
    y_p = _combine(z_1, 0, wts_1, x2_1, hp_1, mod_p, row(g_final), shared_w, rows_per_mod=lp, per_row_mod=False,
                   b0=nb0, out_rows=bp * lp, into=y_p)
    y_s = _combine(z_1, nb1 * lp, wts_1, x2_s, hp_s, mod_s, row(g_final), shared_w, rows_per_mod=bs,
                   per_row_mod=True)
    ret_p = jnp.concatenate([ret_0, ret_1], axis=0)

    tail = lambda g, n: g.reshape(n, lp, dc)[:, lp - CONV_BUF:, :]
    new_conv_p = jnp.concatenate([tail(glu_0, nb0), tail(glu_1, nb1)], axis=0)
    return (y_p.reshape(bp, lp, d), y_s.reshape(bs, ls, d), new_conv_p[None], ret_p[None],
            new_conv_s, ret_s)
```

```python
import functools
import math

import jax
import jax.numpy as jnp
import numpy as np
from jax import lax
from jax.experimental import pallas as pl
from jax.experimental.pallas import tpu as pltpu
from jax.experimental.pallas import tpu_sc as plsc

F32 = jnp.float32
BF16 = jnp.bfloat16
U32 = jnp.uint32
I32 = jnp.int32

EPS = 1e-6
PAST_LEN = 16384
RET_HEADS = 4
RET_CHUNK = 128
CONV_WIDTH = 31
CONV_BUF = CONV_WIDTH - 1
ROPE_BASE = 10000.0
N_EXPERTS = 64
TOP_K = 6
N_GROUPS = 8
TOPK_GROUPS = 4
ROUTED_SCALE = 2.5

LANES = 128
SUBLANES = 8
CONV_PAD = 32
EXPERT_ROWS = 256
ROUTE_TILE = 640
TOKEN_TILE = 128
COMBINE_TILE = 512
VMEM_LIMIT = 56 * 1024 * 1024
SC_CORES = 2
SC_SUBCORES = 16
SC_WORKERS = SC_CORES * SC_SUBCORES
SCATTER_ROWS = 64
GATHER_ROWS = 16
ROW_WORDS = 4
X_SLOTS = 6
X_AHEAD = 4
Y_SLOTS = 4

HI_MASK = 0xFFFF0000


def _sigmoid(x):
    return jax.nn.sigmoid(x)


def _silu(x):
    return x * jax.nn.sigmoid(x)


def _pack_halves(lo, hi):
    lo_u = lax.bitcast_convert_type(lo.astype(BF16).astype(F32), U32) >> 16
    hi_u = lax.bitcast_convert_type(hi.astype(BF16).astype(F32), U32) & jnp.uint32(HI_MASK)
    return hi_u | lo_u


def _unpack_halves(p):
    lo = lax.bitcast_convert_type(p << 16, F32)
    hi = lax.bitcast_convert_type(p & jnp.uint32(HI_MASK), F32)
    return lo, hi


def _store_rows(ref, x):
    rows = x.shape[0]
    for j in range(ROW_WORDS):
        ref[pl.ds(j, rows, stride=ROW_WORDS), :] = x[:, LANES * j:LANES * (j + 1)]


def _load_row_word(ref, j, rows):
    return ref[pl.ds(j, rows, stride=ROW_WORDS), :]


def _ada_body(c_ref, w_ref, b_ref, o_ref):
    s = _silu(c_ref[...]).astype(BF16)
    o_ref[...] = jnp.dot(s, w_ref[...].astype(BF16), preferred_element_type=F32) + b_ref[...]


def _ada(c_all, w_ada, b_ada):
    rows, d = c_all.shape
    n = w_ada.shape[1]
    tn = 2048
    return pl.pallas_call(
        _ada_body,
        grid=(n // tn,),
        in_specs=[
            pl.BlockSpec((rows, d), lambda j: (0, 0)),
            pl.BlockSpec((d, tn), lambda j: (0, j)),
            pl.BlockSpec((1, tn), lambda j: (0, j)),
        ],
        out_specs=pl.BlockSpec((rows, tn), lambda j: (0, j)),
        out_shape=jax.ShapeDtypeStruct((rows, n), F32),
        compiler_params=pltpu.CompilerParams(vmem_limit_bytes=VMEM_LIMIT),
        name="ada",
    )(c_all, w_ada, b_ada)


def _rope_body(inv_ref, cos_ref, sin_ref, *, pos0, tl, half):
    row = lax.broadcasted_iota(I32, (tl, LANES), 0) + pl.program_id(0) * tl
    ang = (row.astype(F32) + pos0) * inv_ref[...]
    lane = lax.broadcasted_iota(I32, (tl, LANES), 1)
    s = jnp.sin(ang)
    cos_ref[...] = jnp.cos(ang)
    sin_ref[...] = jnp.where(lane < half, -s, s)


def _rope_tables(inv2, rows, pos0):
    tl = min(rows, 256)
    return pl.pallas_call(
        functools.partial(_rope_body, pos0=float(pos0), tl=tl, half=LANES // 2),
        grid=(rows // tl,),
        in_specs=[pl.BlockSpec((1, LANES), lambda i: (0, 0))],
        out_specs=[pl.BlockSpec((tl, LANES), lambda i: (i, 0))] * 2,
        out_shape=[jax.ShapeDtypeStruct((rows, LANES), F32)] * 2,
        name="rope",
    )(inv2)


def _modulated_rmsnorm(x, g, sc, sh):
    ms = jnp.mean(x * x, axis=-1, keepdims=True)
    h = x * lax.rsqrt(ms + EPS) * g
    return h * (1.0 + sc) + sh


def _proj_body(x_ref, sh_ref, sc_ref, g_ref, w_ref, cos_ref, sin_ref, *rest, dc, dr, hd, rope_rows):
    glu_ref, q_ref, k_ref, v_ref, sg_ref = rest[-5:]
    hb = _modulated_rmsnorm(x_ref[...], g_ref[...], sc_ref[...], sh_ref[...]).astype(BF16)

    def proj(lo, n):
        return jnp.dot(hb, w_ref[:, lo:lo + n], preferred_element_type=F32)

    glu_ref[...] = proj(0, dc) * _sigmoid(proj(dc, dc))
    cos = cos_ref[...] if rope_rows else cos_ref[0:1, :]
    sin = sin_ref[...] if rope_rows else sin_ref[0:1, :]
    for ref, lo, scale in ((q_ref, 2 * dc, hd ** -0.5), (k_ref, 2 * dc + dr, None)):
        t = proj(lo, dr)
        for hh in range(dr // hd):
            th = t[:, hh * hd:(hh + 1) * hd]
            r = th * cos + pltpu.roll(th, hd // 2, 1) * sin
            if scale is not None:
                r = r * scale
            ref[:, hh * hd:(hh + 1) * hd] = r.astype(BF16)
    v_ref[...] = proj(2 * dc + 2 * dr, dr).astype(BF16)
    sg_ref[...] = _silu(proj(2 * dc + 3 * dr, dr)).astype(BF16)


def _proj(x, mod, g_mix, w_in_b, cos2, sin2, *, dc, dr, hd, per_row_mod, b0=0, nb=None, after=None):
    nb_all, length, d = x.shape
    nb = nb_all if nb is None else nb
    tl = min(length, 512)
    x2 = x.reshape(nb_all * length, d)
    nl = length // tl
    if per_row_mod:
        mod_spec = lambda j: pl.BlockSpec((tl, d), lambda b, l: (l, j))
        rope_spec = pl.BlockSpec((SUBLANES, LANES), lambda b, l: (0, 0))
    else:
        mod_spec = lambda j: pl.BlockSpec((None, None, 1, d), lambda b, l: (b + b0, j, 0, 0))
        rope_spec = pl.BlockSpec((tl, LANES), lambda b, l: (l, 0))
    row_spec = lambda w: pl.BlockSpec((tl, w), lambda b, l: (b * nl + l, 0))
    t = nb * length
    outs = pl.pallas_call(
        functools.partial(_proj_body, dc=dc, dr=dr, hd=hd, rope_rows=not per_row_mod),
        grid=(nb, nl),
        in_specs=[
            pl.BlockSpec((tl, d), lambda b, l: ((b + b0) * nl + l, 0)), mod_spec(0), mod_spec(1),
            pl.BlockSpec((1, d), lambda b, l: (0, 0)),
            pl.BlockSpec(w_in_b.shape, lambda b, l: (0, 0)),
            rope_spec, rope_spec,
        ] + ([] if after is None else [pl.BlockSpec(memory_space=pl.ANY)]),
        out_specs=[row_spec(dc), row_spec(dr), row_spec(dr), row_spec(dr), row_spec(dr)],
        out_shape=[
            jax.ShapeDtypeStruct((t, dc), F32),
            jax.ShapeDtypeStruct((t, dr), BF16),
            jax.ShapeDtypeStruct((t, dr), BF16),
            jax.ShapeDtypeStruct((t, dr), BF16),
            jax.ShapeDtypeStruct((t, dr), BF16),
        ],
        compiler_params=pltpu.CompilerParams(vmem_limit_bytes=VMEM_LIMIT),
        name="proj",
    )(x2, mod, mod, g_mix, w_in_b, cos2, sin2, *(() if after is None else (after,)))
    return outs


def _layernorm_silu(c, g, b):
    mu = jnp.mean(c, axis=-1, keepdims=True)
    d = c - mu
    var = jnp.mean(d * d, axis=-1, keepdims=True)
    return _silu(d * lax.rsqrt(var + EPS) * g + b)


def _groupnorm(o, g, b):
    mu = jnp.mean(o, axis=-1, keepdims=True)
    d = o - mu
    var = jnp.mean(d * d, axis=-1, keepdims=True)
    return d * lax.rsqrt(var + EPS) * g + b


def _mix_body(glu_ref, q_ref, k_ref, v_ref, sg_ref, cw_ref, cb_ref, lng_ref, lnb_ref, rg_ref, rb_ref,
              mask_ref, qd_ref, kd_ref, cd_ref, cat_ref, st_ref, buf, cscr, *, tl, dc, hd, nh, chunk):
    nslab = dc // LANES

    @pl.when(pl.program_id(1) == 0)
    def _():
        buf[:, 0:CONV_PAD, :] = jnp.zeros((nslab, CONV_PAD, LANES), F32)
        st_ref[...] = jnp.zeros(st_ref.shape, F32)

    for j in range(nslab):
        buf[j, CONV_PAD:CONV_PAD + tl, :] = glu_ref[:, LANES * j:LANES * (j + 1)]
    first = CONV_PAD - CONV_BUF
    rows_per_iter = 8 * SUBLANES
    for j in range(nslab):
        cols = slice(LANES * j, LANES * (j + 1))
        wv = [jnp.broadcast_to(cw_ref[t:t + 1, cols], (SUBLANES, LANES)) for t in range(CONV_WIDTH)]
        bias = jnp.broadcast_to(cb_ref[0:1, cols], (SUBLANES, LANES))

        def body(r, carry, j=j, cols=cols, wv=wv, bias=bias):
            base = pl.multiple_of(r * rows_per_iter, rows_per_iter)
            for u in range(rows_per_iter // SUBLANES):
                acc = bias
                for t in range(CONV_WIDTH):
                    acc = acc + wv[t] * buf[j, pl.ds(base + (u * SUBLANES + first + t), SUBLANES), :]
                cscr[pl.ds(base + u * SUBLANES, SUBLANES), cols] = acc
            return carry

        lax.fori_loop(0, tl // rows_per_iter, body, 0)
    for j in range(nslab):
        buf[j, 0:CONV_PAD, :] = buf[j, tl:tl + CONV_PAD, :]
    cat_ref[:, 0:dc] = _layernorm_silu(cscr[...], lng_ref[...], lnb_ref[...]).astype(BF16)

    nt = (((1,), (1,)), ((), ()))
    tn = (((0,), (0,)), ((), ()))
    for c in range(tl // chunk):
        rows = slice(c * chunk, (c + 1) * chunk)
        for hh in range(nh):
            cols = slice(hh * hd, (hh + 1) * hd)
            qh = q_ref[rows, cols]
            kh = k_ref[rows, cols]
            vh = v_ref[rows, cols]
            s = st_ref[0, hh]
            scores = lax.dot_general(qh, kh, nt, preferred_element_type=F32) * mask_ref[hh]
            inner = jnp.dot(scores.astype(BF16), vh, preferred_element_type=F32)
            qd = (qh.astype(F32) * qd_ref[hh]).astype(BF16)
            cross = jnp.dot(qd, s.astype(BF16), preferred_element_type=F32)
            kd = (kh.astype(F32) * kd_ref[hh]).astype(BF16)
            st_ref[0, hh] = cd_ref[hh] * s + lax.dot_general(kd, vh, tn, preferred_element_type=F32)
            o = _groupnorm(inner + cross, rg_ref[0:1, cols], rb_ref[0:1, cols])
            cat_ref[rows, dc + hh * hd:dc + (hh + 1) * hd] = (o * sg_ref[rows, cols].astype(F32)).astype(BF16)


def _mix(glu, q, k, v, sg, conv_w, conv_b, ln_g, ln_b, rg, rb, tables, *, nb, length, dc, dr, hd):
    nh = dr // hd
    chunk = math.gcd(length, RET_CHUNK)
    tl = min(length, 256)
    nl = length // tl
    mask, qd, kd, cd = tables
    row_spec = lambda w: pl.BlockSpec((tl, w), lambda b, l: (b * nl + l, 0))
    full = lambda a: pl.BlockSpec(a.shape, lambda b, l: (0,) * a.ndim)
    cat, st = pl.pallas_call(
        functools.partial(_mix_body, tl=tl, dc=dc, hd=hd, nh=nh, chunk=chunk),
        grid=(nb, nl),
        in_specs=[row_spec(dc), row_spec(dr), row_spec(dr), row_spec(dr), row_spec(dr),
                  full(conv_w), full(conv_b), full(ln_g), full(ln_b), full(rg), full(rb),
                  full(mask), full(qd), full(kd), full(cd)],
        out_specs=[row_spec(dc + dr), pl.BlockSpec((1, nh, hd, hd), lambda b, l: (b, 0, 0, 0))],
        out_shape=[jax.ShapeDtypeStruct((nb * length, dc + dr), BF16),
                   jax.ShapeDtypeStruct((nb, nh, hd, hd), F32)],
        scratch_shapes=[pltpu.VMEM((dc // LANES, tl + CONV_PAD, LANES), F32),
                        pltpu.VMEM((tl, dc), F32)],
        compiler_params=pltpu.CompilerParams(dimension_semantics=("arbitrary", "arbitrary")),
        name="mix",
    )(glu, q, k, v, sg, conv_w, conv_b, ln_g, ln_b, rg, rb, mask, qd, kd, cd)
    return cat, st


def _mix1_body(glu_ref, q_ref, k_ref, v_ref, sg_ref, sc_ref, s0_ref, cw_ref, cb_ref, lng_ref, lnb_ref,
               rg_ref, rb_ref, gam_ref, cat_ref, st_ref, nc_ref, cscr, oscr, qf, kf, vf, *, tb, dc, hd, nh):
    w_hist = cw_ref[0:CONV_BUF, :]
    w_last = cw_ref[CONV_BUF:CONV_WIDTH, :]
    for bb in range(tb):
        hist = jnp.sum(sc_ref[bb] * w_hist, axis=0, keepdims=True)
        cscr[bb:bb + 1, :] = hist + glu_ref[bb:bb + 1, :] * w_last + cb_ref[...]
        nc_ref[bb, 0:CONV_BUF - 1, :] = sc_ref[bb, 1:CONV_BUF, :]
        nc_ref[bb, CONV_BUF - 1:CONV_BUF, :] = glu_ref[bb:bb + 1, :]
    cat_ref[:, 0:dc] = _layernorm_silu(cscr[...], lng_ref[...], lnb_ref[...]).astype(BF16)

    tn = (((0,), (0,)), ((), ()))
    rowid = lax.broadcasted_iota(I32, (tb, hd), 0)
    qf[...] = q_ref[...].astype(F32)
    kf[...] = k_ref[...].astype(F32)
    vf[...] = v_ref[...].astype(F32)
    for hh in range(nh):
        cols = slice(hh * hd, (hh + 1) * hd)
        qa = q_ref[:, cols]
        ka = k_ref[:, cols]
        gam = gam_ref[hh, 0:1, :]
        for bb in range(tb):
            onehot = (rowid == bb).astype(BF16)
            qcol = lax.dot_general(qa, onehot, tn, preferred_element_type=F32)
            kcol = lax.dot_general(ka, onehot, tn, preferred_element_type=F32)
            s0 = s0_ref[bb, hh]
            qrow = qf[bb:bb + 1, cols]
            krow = kf[bb:bb + 1, cols]
            vrow = vf[bb:bb + 1, cols]
            qk = jnp.sum(qrow * krow, axis=-1, keepdims=True)
            cross = gam * jnp.sum(qcol * s0, axis=0, keepdims=True)
            st_ref[bb, hh] = gam * s0 + kcol * vrow
            oscr[bb:bb + 1, cols] = qk * vrow + cross
    for hh in range(nh):
        cols = slice(hh * hd, (hh + 1) * hd)
        o = _groupnorm(oscr[:, cols], rg_ref[0:1, cols], rb_ref[0:1, cols])
        cat_ref[:, dc + hh * hd:dc + (hh + 1) * hd] = (o * sg_ref[:, cols].astype(F32)).astype(BF16)


def _mix1(glu, q, k, v, sg, state_conv, state_ret, conv_w, conv_b, ln_g, ln_b, rg, rb, gam, *, dc, dr, hd):
    nb = glu.shape[0]
    nh = dr // hd
    tb = 16
    row_spec = lambda w: pl.BlockSpec((tb, w), lambda i: (i, 0))
    full = lambda a: pl.BlockSpec(a.shape, lambda i: (0,) * a.ndim)
    st_spec = pl.BlockSpec((None, tb, nh, hd, hd), lambda i: (0, i, 0, 0, 0))
    conv_spec = pl.BlockSpec((None, tb, CONV_BUF, dc), lambda i: (0, i, 0, 0))
    cat, st, new_conv = pl.pallas_call(
        functools.partial(_mix1_body, tb=tb, dc=dc, hd=hd, nh=nh),
        grid=(nb // tb,),
        in_specs=[row_spec(dc), row_spec(dr), row_spec(dr), row_spec(dr), row_spec(dr), conv_spec, st_spec,
                  full(conv_w), full(conv_b), full(ln_g), full(ln_b), full(rg), full(rb), full(gam)],
        out_specs=[row_spec(dc + dr), st_spec, conv_spec],
        out_shape=[jax.ShapeDtypeStruct((nb, dc + dr), BF16),
                   jax.ShapeDtypeStruct((1, nb, nh, hd, hd), F32),
                   jax.ShapeDtypeStruct((1, nb, CONV_BUF, dc), F32)],
        scratch_shapes=[pltpu.VMEM((tb, dc), F32)] + [pltpu.VMEM((tb, dr), F32)] * 4,
        compiler_params=pltpu.CompilerParams(vmem_limit_bytes=VMEM_LIMIT),
        name="mix1",
    )(glu, q, k, v, sg, state_conv, state_ret, conv_w, conv_b, ln_g, ln_b, rg, rb, gam)
    return cat, st, new_conv


def _post_body(cat_ref, x_ref, gtm_ref, scf_ref, shf_ref, g_ref, wo_ref, wrh_ref, wrl_ref, x1_ref, hp_ref, lg_ref):
    d = x_ref.shape[1]
    y = jnp.dot(cat_ref[...], wo_ref[...], preferred_element_type=F32)
    x1 = x_ref[...] + gtm_ref[...] * y
    x1_ref[...] = x1
    h = _modulated_rmsnorm(x1, g_ref[...], scf_ref[...], shf_ref[...])
    hb = h.astype(BF16)
    _store_rows(hp_ref, _pack_halves(h[:, 0:d // 2], h[:, d // 2:d]))
    hl = (h - hb.astype(F32)).astype(BF16)
    nt = (((1,), (1,)), ((), ()))
    lg_ref[...] = (lax.dot_general(wrh_ref[...], hb, nt, preferred_element_type=F32)
                   + lax.dot_general(wrh_ref[...], hl, nt, preferred_element_type=F32)
                   + lax.dot_general(wrl_ref[...], hb, nt, preferred_element_type=F32))


def _post(cat, x, mod, g_ffn, wo_b, wrh, wrl, *, per_row_mod, b0=0, nb=None):
    nb_all, length, d = x.shape
    nb = nb_all if nb is None else nb
    tl = min(length, 1024)
    nl = length // tl
    t = nb * length
    ne = wrh.shape[0]
    x2d = x.reshape(nb_all * length, d)
    if per_row_mod:
        mod_spec = lambda j: pl.BlockSpec((tl, d), lambda b, l: (l, j))
    else:
        mod_spec = lambda j: pl.BlockSpec((None, None, 1, d), lambda b, l: (b + b0, j, 0, 0))
    row_spec = lambda w: pl.BlockSpec((tl, w), lambda b, l: (b * nl + l, 0))
    full = lambda a: pl.BlockSpec(a.shape, lambda b, l: (0,) * a.ndim)
    return pl.pallas_call(
        _post_body,
        grid=(nb, nl),
        in_specs=[row_spec(d), pl.BlockSpec((tl, d), lambda b, l: ((b + b0) * nl + l, 0)),
                  mod_spec(2), mod_spec(4), mod_spec(3),
                  full(g_ffn), full(wo_b), full(wrh), full(wrl)],
        out_specs=[row_spec(d),
                   pl.BlockSpec((tl * ROW_WORDS, LANES), lambda b, l: (b * nl + l, 0)),
                   pl.BlockSpec((ne, tl), lambda b, l: (0, b * nl + l))],
        out_shape=[jax.ShapeDtypeStruct((t, d), F32),
                   jax.ShapeDtypeStruct((t * ROW_WORDS, LANES), U32),
                   jax.ShapeDtypeStruct((ne, t), F32)],
        compiler_params=pltpu.CompilerParams(vmem_limit_bytes=VMEM_LIMIT),
        name="post",
    )(cat, x2d, mod, mod, mod, g_ffn, wo_b, wrh, wrl)


def _first_max(x, idx, sentinel):
    m = jnp.max(x, axis=0, keepdims=True)
    f = jnp.min(jnp.where(x == m, idx, sentinel), axis=0, keepdims=True)
    return m, f


def _route_body(lg_ref, bias_ref, e_ref, w_ref, r_ref, cnt_ref, cnt_scr, *, tr, ne, ng, topk, topg):
    @pl.when(pl.program_id(0) == 0)
    def _():
        cnt_scr[...] = jnp.zeros(cnt_scr.shape, F32)

    per = ne // ng
    neg = -jnp.inf
    scores = _sigmoid(lg_ref[...])
    sel = scores + bias_ref[...]
    sub = lax.broadcasted_iota(I32, (per, tr), 0)
    gs = []
    for g in range(ng):
        s_g = sel[g * per:(g + 1) * per, :]
        m1, f1 = _first_max(s_g, sub, per)
        m2 = jnp.max(jnp.where(sub == f1, neg, s_g), axis=0, keepdims=True)
        gs.append(m1 + m2)
    gsc = jnp.concatenate(gs, axis=0)
    gi = lax.broadcasted_iota(I32, (ng, tr), 0)
    keep = jnp.zeros((ng, tr), F32)
    for _ in range(topg):
        _, f = _first_max(gsc, gi, ng)
        pick = gi == f
        keep = jnp.where(pick, 1.0, keep)
        gsc = jnp.where(pick, neg, gsc)
    work = jnp.concatenate(
        [jnp.where(keep[g:g + 1, :] > 0.5, sel[g * per:(g + 1) * per, :], neg) for g in range(ng)], axis=0)
    ei = lax.broadcasted_iota(I32, (ne, tr), 0)
    picks, es, ws = [], [], []
    for _ in range(topk):
        _, f = _first_max(work, ei, ne)
        pick = ei == f
        picks.append(pick)
        es.append(f)
        ws.append(jnp.sum(jnp.where(pick, scores, 0.0), axis=0, keepdims=True))
        work = jnp.where(pick, neg, work)
    wsum = ws[0]
    for w in ws[1:]:
        wsum = wsum + w
    scale = ROUTED_SCALE / wsum
    chosen = picks[0]
    for p in picks[1:]:
        chosen = jnp.logical_or(chosen, p)
    chosen_f = chosen.astype(F32)
    t_row = lax.broadcasted_iota(I32, (tr, tr), 0)
    t_col = lax.broadcasted_iota(I32, (tr, tr), 1)
    before = (t_row < t_col).astype(BF16)
    prior = cnt_scr[:, 0:1] + jnp.dot(chosen_f.astype(BF16), before, preferred_element_type=F32)
    rs = [jnp.sum(jnp.where(p, prior, 0.0), axis=0, keepdims=True).astype(I32) for p in picks]
    pad_i = jnp.zeros((SUBLANES - topk, tr), I32)
    pad_f = jnp.zeros((SUBLANES - topk, tr), F32)
    e_ref[...] = jnp.concatenate(es + [pad_i], axis=0)
    w_ref[...] = jnp.concatenate([w * scale for w in ws] + [pad_f], axis=0)
    r_ref[...] = jnp.concatenate(rs + [pad_i], axis=0)
    total = cnt_scr[:, 0:1] + jnp.sum(chosen_f, axis=1, keepdims=True)
    cnt_scr[...] = jnp.broadcast_to(total, cnt_scr.shape)
    cnt_ref[...] = jnp.broadcast_to(total, cnt_ref.shape)


def _route_tile(t):
    return max(m for m in range(LANES, ROUTE_TILE + 1, LANES) if t % m == 0)


def _route(logits_t, bias_col):
    ne, t = logits_t.shape
    tr = _route_tile(t)
    tok = lambda dt: jax.ShapeDtypeStruct((SUBLANES, t), dt)
    tok_spec = pl.BlockSpec((SUBLANES, tr), lambda i: (0, i))
    return pl.pallas_call(
        functools.partial(_route_body, tr=tr, ne=ne, ng=N_GROUPS, topk=TOP_K, topg=TOPK_GROUPS),
        grid=(t // tr,),
        in_specs=[pl.BlockSpec((ne, tr), lambda i: (0, i)), pl.BlockSpec((ne, 1), lambda i: (0, 0))],
        out_specs=[tok_spec, tok_spec, tok_spec, pl.BlockSpec((ne, LANES), lambda i: (0, 0))],
        out_shape=[tok(I32), tok(F32), tok(I32), jax.ShapeDtypeStruct((ne, LANES), F32)],
        scratch_shapes=[pltpu.VMEM((ne, LANES), F32)],
        compiler_params=pltpu.CompilerParams(dimension_semantics=("arbitrary",)),
        name="route",
    )(logits_t, bias_col)


def _dest_body(cnt_ref, e_ref, r_ref, d_ref, ps, *, ne, tr):
    shift = EXPERT_ROWS.bit_length() - 1

    @pl.when(pl.program_id(0) == 0)
    def _():
        def step(j, start):
            ps[j] = start
            return start + lax.shift_left(lax.shift_right_logical(cnt_ref[j] + (EXPERT_ROWS - 1), shift), shift)

        lax.fori_loop(0, ne, step, jnp.int32(0))

    e = e_ref[...]
    base = jnp.zeros(e.shape, I32)
    for j in range(ne):
        base = jnp.where(e == j, ps[j], base)
    dest = base + r_ref[...]
    for m in range(tr // TOKEN_TILE):
        d_ref[SUBLANES * m:SUBLANES * (m + 1), :] = dest[:, TOKEN_TILE * m:TOKEN_TILE * (m + 1)]


def _dest_rows(counts, eidx, rank):
    rows, t = eidx.shape
    ne = counts.shape[0]
    tr = _route_tile(t)
    spec = pl.BlockSpec((rows, tr), lambda i, cnt: (0, i))
    return pl.pallas_call(
        functools.partial(_dest_body, ne=ne, tr=tr),
        grid_spec=pltpu.PrefetchScalarGridSpec(
            num_scalar_prefetch=1, grid=(t // tr,), in_specs=[spec, spec],
            out_specs=pl.BlockSpec((tr // TOKEN_TILE * rows, TOKEN_TILE), lambda i, cnt: (i, 0)),
            scratch_shapes=[pltpu.SMEM((ne,), I32)]),
        out_shape=jax.ShapeDtypeStruct((t // TOKEN_TILE * rows, TOKEN_TILE), I32),
        compiler_params=pltpu.CompilerParams(dimension_semantics=("arbitrary",)),
        name="dest",
    )(counts, eidx, rank)


def _sc_mesh():
    return plsc.VectorSubcoreMesh(core_axis_name="c", subcore_axis_name="s")


def _sc_worker_id():
    return lax.axis_index("s") * SC_CORES + lax.axis_index("c")


def _index_block(dest_ref, chunk, width):
    per_tile = TOKEN_TILE // width
    return dest_ref.at[chunk // per_tile, :, pl.ds((chunk % per_tile) * width, width)]


def _dispatch(hps, dest3, n_rows):
    w = SCATTER_ROWS
    width = hps[0].shape[1]
    bounds = [0]
    for h in hps:
        bounds.append(bounds[-1] + h.shape[0] // w)
    nch = bounds[-1]
    nsrc = len(hps)

    @functools.partial(
        pl.kernel, mesh=_sc_mesh(),
        out_type=jax.ShapeDtypeStruct((n_rows, width), U32),
        scratch_types=[pltpu.VMEM((2, SUBLANES, w), I32), pltpu.VMEM((2, w, width), U32),
                       pltpu.SemaphoreType.DMA, pltpu.SemaphoreType.DMA, pltpu.SemaphoreType.DMA],
        compiler_params=pltpu.CompilerParams(use_tc_tiling_on_sc=False),
        name="dispatch",
    )
    def run(*refs):
        src_refs, dest_ref, xs_ref = refs[:nsrc], refs[nsrc], refs[nsrc + 1]
        idx_v, rows_v = refs[nsrc + 2:nsrc + 4]
        sem_load = refs[nsrc + 4:nsrc + 6]
        sem_scatter = refs[nsrc + 6]
        wid = _sc_worker_id()

        def start_loads(c, slot):
            pltpu.async_copy(_index_block(dest_ref, c, w), idx_v.at[slot], sem_load[slot])
            for i, src in enumerate(src_refs):
                @pl.when(jnp.logical_and(c >= bounds[i], c < bounds[i + 1]))
                def _(src=src, lo=bounds[i]):
                    pltpu.async_copy(src.at[pl.ds((c - lo) * w, w)], rows_v.at[slot], sem_load[slot])

        def wait_loads(slot):
            pltpu.make_async_copy(_index_block(dest_ref, 0, w), idx_v.at[slot], sem_load[slot]).wait()
            pltpu.make_async_copy(src_refs[0].at[pl.ds(0, w)], rows_v.at[slot], sem_load[slot]).wait()

        @pl.when(wid < nch)
        def _():
            start_loads(wid, 0)

        @pl.loop(0, pl.cdiv(pl.cdiv(nch, SC_WORKERS), 2))
        def _(rr):
            for slot in range(2):
                c = (rr * 2 + slot) * SC_WORKERS + wid

                @pl.when(c < nch)
                def _(c=c, slot=slot):
                    wait_loads(slot)

                    @pl.when(c + SC_WORKERS < nch)
                    def _():
                        start_loads(c + SC_WORKERS, 1 - slot)

                    copies = [pltpu.async_copy(rows_v.at[slot], xs_ref.at[idx_v.at[slot, k]], sem_scatter)
                              for k in range(TOP_K)]
                    for cp in copies:
                        cp.wait()

    return run(*hps, dest3)


def _undispatch(ys, dest3, n_tokens, after=None):
    w = GATHER_ROWS
    width = ys.shape[1]
    nch = n_tokens // w

    @functools.partial(
        pl.kernel, mesh=_sc_mesh(),
        out_type=jax.ShapeDtypeStruct((TOP_K, n_tokens, width), U32),
        scratch_types=[pltpu.VMEM((2, SUBLANES, w), I32), pltpu.VMEM((2, TOP_K, w, width), U32),
                       pltpu.SemaphoreType.DMA, pltpu.SemaphoreType.DMA, pltpu.SemaphoreType.DMA],
        compiler_params=pltpu.CompilerParams(use_tc_tiling_on_sc=False),
        name="undispatch",
    )
    def run(*refs):
        ys_ref, dest_ref = refs[:2]
        z_ref, idx_v, bufs = refs[-6:-3]
        sem_gather = refs[-3:-1]
        sem_store = refs[-1]
        wid = _sc_worker_id()

        def start_gathers(c, slot):
            pltpu.sync_copy(_index_block(dest_ref, c, w), idx_v.at[slot])
            for k in range(TOP_K):
                pltpu.async_copy(ys_ref.at[idx_v.at[slot, k]], bufs.at[slot, k], sem_gather[slot])

        def wait_gathers(slot):
            for k in range(TOP_K):
                pltpu.make_async_copy(ys_ref.at[idx_v.at[slot, k]], bufs.at[slot, k], sem_gather[slot]).wait()

        @pl.when(wid < nch)
        def _():
            start_gathers(wid, 0)

        @pl.loop(0, pl.cdiv(pl.cdiv(nch, SC_WORKERS), 2))
        def _(rr):
            for slot in range(2):
                c = (rr * 2 + slot) * SC_WORKERS + wid

                @pl.when(c < nch)
                def _(c=c, slot=slot):
                    @pl.when(c + SC_WORKERS < nch)
                    def _():
                        start_gathers(c + SC_WORKERS, 1 - slot)

                    wait_gathers(slot)
                    stores = [pltpu.async_copy(bufs.at[slot, k], z_ref.at[k, pl.ds(c * w, w)], sem_store)
                              for k in range(TOP_K)]
                    for cp in stores:
                        cp.wait()

    return run(ys, dest3) if after is None else run(ys, dest3, after)


PART_SHIFT = 24


def _expert_body(*refs, tm, ne, nparts):
    cnt_refs = refs[:nparts]
    xs_refs = refs[nparts:2 * nparts]
    w1_ref, w3_ref, w2_ref = refs[2 * nparts:2 * nparts + 3]
    ys_refs = refs[2 * nparts + 3:3 * nparts + 3]
    (w1f, w3f, w2f, w1s, w3s, w2s, xbuf, ybuf, xlo, xhi, sched, sem_x, sem_y, sem_w) = refs[3 * nparts + 3:]
    blk_words = tm * ROW_WORDS
    half = ROW_WORDS * LANES
    shift = tm.bit_length() - 1

    def n_blocks_of(p, e):
        return lax.shift_right_logical(cnt_refs[p][e] + (tm - 1), shift)

    def n_all(e):
        n = n_blocks_of(0, e)
        for p in range(1, nparts):
            n = n + n_blocks_of(p, e)
        return n

    def next_nonempty(e):
        return lax.while_loop(
            lambda c: jnp.logical_and(c < ne, n_all(jnp.minimum(c, ne - 1)) == 0), lambda c: c + 1, e)

    def plan(e, carry):
        i, starts = carry[0], list(carry[1:])
        for p in range(nparts):
            n = n_blocks_of(p, e)

            def put(j, c, p=p, i=i, start=starts[p]):
                sched[i + j] = (start + j) + (p << PART_SHIFT)
                return c

            lax.fori_loop(0, n, put, 0)
            i = i + n
            starts[p] = starts[p] + n
        return (i, *starts)

    nu = lax.fori_loop(0, ne, plan, (jnp.int32(0),) * (nparts + 1))[0]

    def rows_of(code):
        blk = code & ((1 << PART_SHIFT) - 1)
        return pl.ds(pl.multiple_of(blk * blk_words, blk_words), blk_words)

    def x_start(i, slot):
        code = sched[i]
        for p in range(nparts):
            @pl.when(lax.shift_right_logical(code, PART_SHIFT) == p)
            def _(p=p):
                pltpu.make_async_copy(xs_refs[p].at[rows_of(code), :], xbuf.at[slot], sem_x.at[slot]).start()

    def x_wait(slot):
        pltpu.make_async_copy(xs_refs[0].at[pl.ds(0, blk_words), :], xbuf.at[slot], sem_x.at[slot]).wait()

    def y_start(i, slot):
        code = sched[i]
        for p in range(nparts):
            @pl.when(lax.shift_right_logical(code, PART_SHIFT) == p)
            def _(p=p):
                pltpu.make_async_copy(ybuf.at[slot], ys_refs[p].at[rows_of(code), :], sem_y.at[slot]).start()

    def y_wait(slot):
        pltpu.make_async_copy(ybuf.at[slot], ys_refs[0].at[pl.ds(0, blk_words), :], sem_y.at[slot]).wait()

    def w_copies(e, ws):
        return [pltpu.make_async_copy(src.at[e], dst.at[ws], sem_w.at[ws])
                for src, dst in ((w1_ref, w1f), (w3_ref, w3f), (w2_ref, w2f))]

    for q in range(X_AHEAD):
        @pl.when(q < nu)
        def _(q=q):
            x_start(q, q)

    e_first = next_nonempty(jnp.int32(0))

    @pl.when(e_first < ne)
    def _():
        for cp in w_copies(e_first, 0):
            cp.start()

    def blocks(i, nblk):
        for b in range(nblk):
            x_wait((i + b) % X_SLOTS)
        for b in range(nblk):
            nxt = i + b + X_AHEAD

            @pl.when(nxt < nu)
            def _(nxt=nxt):
                x_start(nxt, nxt % X_SLOTS)

        for b in range(nblk):
            @pl.when(i + b >= Y_SLOTS)
            def _(b=b):
                y_wait((i + b) % Y_SLOTS)

        for b in range(nblk):
            xin = xbuf.at[(i + b) % X_SLOTS]
            for w in range(ROW_WORDS):
                lo, hi = _unpack_halves(_load_row_word(xin, w, tm))
                xlo[b, :, LANES * w:LANES * (w + 1)] = lo.astype(BF16)
                xhi[b, :, LANES * w:LANES * (w + 1)] = hi.astype(BF16)

        def up(b, wsc):
            return (jnp.dot(xlo[b], wsc[0:half, :], preferred_element_type=F32)
                    + jnp.dot(xhi[b], wsc[half:2 * half, :], preferred_element_type=F32))

        for b in range(nblk):
            hid = (_silu(up(b, w1s)) * up(b, w3s)).astype(BF16)
            y = jnp.dot(hid, w2s[...], preferred_element_type=F32)
            _store_rows(ybuf.at[(i + b) % Y_SLOTS], _pack_halves(y[:, 0:half], y[:, half:2 * half]))
        for b in range(nblk):
            y_start(i + b, (i + b) % Y_SLOTS)

    def per_expert(e, carry):
        i0, ws = carry
        n = n_all(e)

        @pl.when(n > 0)
        def _():
            for cp in w_copies(e, ws):
                cp.wait()
            w1s[...] = w1f[ws].astype(BF16)
            w3s[...] = w3f[ws].astype(BF16)
            w2s[...] = w2f[ws].astype(BF16)
            e_next = next_nonempty(e + 1)

            @pl.when(e_next < ne)
            def _():
                for cp in w_copies(e_next, 1 - ws):
                    cp.start()

            def pair(j, c):
                blocks(i0 + 2 * j, 2)
                return c

            lax.fori_loop(0, lax.shift_right_logical(n, 1), pair, 0)

            @pl.when(n % 2 == 1)
            def _():
                blocks(i0 + n - 1, 1)

        return i0 + n, jnp.where(n > 0, 1 - ws, ws)

    lax.fori_loop(0, ne, per_expert, (jnp.int32(0), jnp.int32(0)))

    for q in range(Y_SLOTS):
        @pl.when(nu > q)
        def _(q=q):
            y_wait((nu - 1 - q) % Y_SLOTS)


def _experts(counts, xss, w1e, w3e, w2e):
    tm = EXPERT_ROWS
    ne, d, de = w1e.shape
    half = ROW_WORDS * LANES
    nparts = len(xss)
    cap = sum(x.shape[0] // (tm * ROW_WORDS) for x in xss)
    assert cap < (1 << PART_SHIFT)
    anyspec = pl.BlockSpec(memory_space=pl.ANY)
    blk_buf = lambda n: pltpu.VMEM((n, tm * ROW_WORDS, LANES), U32)
    return pl.pallas_call(
        functools.partial(_expert_body, tm=tm, ne=ne, nparts=nparts),
        grid_spec=pltpu.PrefetchScalarGridSpec(
            num_scalar_prefetch=nparts,
            grid=(1,),
            in_specs=[anyspec] * (nparts + 3),
            out_specs=[anyspec] * nparts,
            scratch_shapes=[pltpu.VMEM((2, d, de), F32), pltpu.VMEM((2, d, de), F32), pltpu.VMEM((2, de, d), F32),
                            pltpu.VMEM((d, de), BF16), pltpu.VMEM((d, de), BF16), pltpu.VMEM((de, d), BF16),
                            blk_buf(X_SLOTS), blk_buf(Y_SLOTS),
                            pltpu.VMEM((2, tm, half), BF16), pltpu.VMEM((2, tm, half), BF16),
                            pltpu.SMEM((cap,), I32),
                            pltpu.SemaphoreType.DMA((X_SLOTS,)), pltpu.SemaphoreType.DMA((Y_SLOTS,)),
                            pltpu.SemaphoreType.DMA((2,))]),
        out_shape=[jax.ShapeDtypeStruct(x.shape, U32) for x in xss],
        compiler_params=pltpu.CompilerParams(dimension_semantics=("arbitrary",), vmem_limit_bytes=VMEM_LIMIT),
        name="experts",
    )(*counts, *xss, w1e, w3e, w2e)


def _combine_body(z_ref, x1_ref, hp_ref, gtf_ref, wt_ref, gfin_ref, ws1_ref, ws3_ref, ws2_ref, y_ref, xo, hlo, hhi,
                  *, td, topk):
    half = ROW_WORDS * LANES
    for w in range(ROW_WORDS):
        lo, hi = _unpack_halves(_load_row_word(hp_ref, w, td))
        hlo[:, LANES * w:LANES * (w + 1)] = lo.astype(BF16)
        hhi[:, LANES * w:LANES * (w + 1)] = hi.astype(BF16)

    def up(w_ref):
        return (jnp.dot(hlo[...], w_ref[0:half, :], preferred_element_type=F32)
                + jnp.dot(hhi[...], w_ref[half:2 * half, :], preferred_element_type=F32))

    xo[...] = jnp.dot((_silu(up(ws1_ref)) * up(ws3_ref)).astype(BF16), ws2_ref[...], preferred_element_type=F32)
    wt = wt_ref[...].T
    ws = [wt[:, k:k + 1] for k in range(topk)]
    sq = jnp.zeros((td, 1), F32)
    for j in range(ROW_WORDS):
        acc_lo = jnp.zeros((td, LANES), F32)
        acc_hi = jnp.zeros((td, LANES), F32)
        for k in range(topk):
            lo, hi = _unpack_halves(_load_row_word(z_ref.at[k], j, td))
            acc_lo = acc_lo + ws[k] * lo
            acc_hi = acc_hi + ws[k] * hi
        for base, acc in ((0, acc_lo), (half, acc_hi)):
            cols = slice(base + LANES * j, base + LANES * (j + 1))
            x = x1_ref[:, cols] + gtf_ref[:, cols] * (acc + xo[:, cols])
            xo[:, cols] = x
            sq = sq + jnp.sum(x * x, axis=-1, keepdims=True)
    rs = lax.rsqrt(sq / (2 * half) + EPS)
    y_ref[...] = xo[...] * rs * gfin_ref[...]


def _combine_body_into(*refs, td, topk):
    _combine_body(*refs[:9], *refs[10:], td=td, topk=topk)


def _combine(z, token0, wts, x1, hp, mod, g_final, shared_w, *, rows_per_mod, per_row_mod, b0=0, out_rows=None,
             into=None):
    t, d = x1.shape
    td = min(t, COMBINE_TILE)
    tile0 = token0 // td
    out_rows = t if out_rows is None else out_rows
    if per_row_mod:
        gtf_spec = pl.BlockSpec((td, d), lambda i: (i, 5))
        out0 = 0
    else:
        tiles_per_mod = rows_per_mod // td
        gtf_spec = pl.BlockSpec((None, None, 1, d), lambda i: (i // tiles_per_mod + b0, 5, 0, 0))
        out0 = b0 * tiles_per_mod
    full = lambda a: pl.BlockSpec(a.shape, lambda i: (0,) * a.ndim)
    in_specs = [pl.BlockSpec((TOP_K, td * ROW_WORDS, LANES), lambda i: (0, i + tile0, 0)),
                pl.BlockSpec((td, d), lambda i: (i, 0)),
                pl.BlockSpec((td * ROW_WORDS, LANES), lambda i: (i, 0)),
                gtf_spec,
                pl.BlockSpec((SUBLANES, td), lambda i: (0, i + tile0)),
                pl.BlockSpec((1, d), lambda i: (0, 0))] + [full(w) for w in shared_w]
    args = [z, x1, hp, mod, wts, g_final, *shared_w]
    body, aliases = _combine_body, {}
    if into is not None:
        in_specs.append(pl.BlockSpec(memory_space=pl.ANY))
        args.append(into)
        body, aliases = _combine_body_into, {len(args) - 1: 0}
    return pl.pallas_call(
        functools.partial(body, td=td, topk=TOP_K),
        grid=(t // td,),
        in_specs=in_specs,
        out_specs=pl.BlockSpec((td, d), lambda i: (i + out0, 0)),
        scratch_shapes=[pltpu.VMEM((td, d), F32), pltpu.VMEM((td, d // 2), BF16), pltpu.VMEM((td, d // 2), BF16)],
        out_shape=jax.ShapeDtypeStruct((out_rows, d), F32),
        input_output_aliases=aliases,
        compiler_params=pltpu.CompilerParams(vmem_limit_bytes=VMEM_LIMIT),
        name="combine",
    )(*args)


def _log_gamma(nh):
    return np.log(1.0 - 2.0 ** (-5.0 - np.arange(nh, dtype=np.float32))).astype(np.float32)


def _retention_tables(length, nh, hd):
    c = math.gcd(length, RET_CHUNK)
    log_g = _log_gamma(nh)
    idx = np.arange(c, dtype=np.float32)
    rel = idx[:, None] - idx[None, :]
    mask = np.where(rel >= 0, np.exp(log_g[:, None, None] * np.maximum(rel, 0.0)), 0.0).astype(np.float32)
    q_decay = np.exp(log_g[None, :] * (idx[:, None] + 1.0)).astype(np.float32)
    k_decay = np.exp(log_g[None, :] * (c - 1.0 - idx[:, None])).astype(np.float32)
    chunk_decay = np.exp(log_g * np.float32(c)).astype(np.float32)
    qd = np.broadcast_to(q_decay.T[:, :, None], (nh, c, hd))
    kd = np.broadcast_to(k_decay.T[:, :, None], (nh, c, hd))
    cd = np.broadcast_to(chunk_decay[:, None, None], (nh, hd, hd))
    return tuple(jnp.asarray(t) for t in (mask, qd, kd, cd))


def kernel(x_prompt, x_sample, c_prompt, c_sample, state_conv, state_ret, w_ada, b_ada, g_mix, g_ffn, w_in,
           conv_w, conv_b, conv_norm_g, conv_norm_b, ret_norm_g, ret_norm_b, w_out, w_router, router_bias,
           w1, w3, w2, ws1, ws3, ws2, g_final):
    depth = w_ada.shape[0]
    assert depth == 1, "single-layer trunk"
    bp, lp, d = x_prompt.shape
    bs, ls, _ = x_sample.shape
    assert ls == 1
    dc = conv_w.shape[2]
    dr = ret_norm_g.shape[1]
    nh = RET_HEADS
    hd = dr // nh
    assert hd == LANES and lp % 256 == 0 and bs % TOKEN_TILE == 0 and d // 2 == ROW_WORDS * LANES and bp >= 2
    ne = w_router.shape[2]
    row = lambda a: a.reshape(1, -1)

    mod = _ada(jnp.concatenate([c_prompt, c_sample], axis=0), w_ada[0], row(b_ada[0]))
    mod_p = mod[:bp].reshape(bp, 6, 1, d)
    mod_s = mod[bp:]

    half = hd // 2
    inv = (np.float32(ROPE_BASE) ** (-np.arange(half, dtype=np.float32) / np.float32(half))).astype(np.float32)
    inv2 = jnp.asarray(np.concatenate([inv, inv]).reshape(1, hd))
    cos_p, sin_p = _rope_tables(inv2, lp, 0)
    cos_s, sin_s = _rope_tables(inv2, SUBLANES, PAST_LEN)

    w_in_b = w_in[0].astype(BF16)
    wo_b = w_out[0].astype(BF16)
    wr_t = w_router[0].T
    wrh = wr_t.astype(BF16)
    wrl = (wr_t - wrh.astype(F32)).astype(BF16)
    ws1_b, ws3_b, ws2_b = ws1[0].astype(BF16), ws3[0].astype(BF16), ws2[0].astype(BF16)
    dims = dict(dc=dc, dr=dr, hd=hd)

    tables = _retention_tables(lp, nh, hd)
    gam = jnp.asarray(np.broadcast_to(np.exp(_log_gamma(nh))[:, None, None], (nh, SUBLANES, hd)))
    norm_rows = (row(conv_b[0]), row(conv_norm_g[0]), row(conv_norm_b[0]), row(ret_norm_g[0]), row(ret_norm_b[0]))
    post_w = (row(g_ffn[0]), wo_b, wrh, wrl)
    shared_w = (ws1_b, ws3_b, ws2_b)
    bias_col = router_bias[0].reshape(ne, 1)
    hw = d // 2
    tm = EXPERT_ROWS

    def pre_prompt(b0, nb, after=None):
        glu, q, k, v, sg = _proj(x_prompt, mod_p, row(g_mix[0]), w_in_b, cos_p, sin_p,
                                 per_row_mod=False, b0=b0, nb=nb, after=after, **dims)
        cat, ret = _mix(glu, q, k, v, sg, conv_w[0], *norm_rows, tables, nb=nb, length=lp, **dims)
        x2, hp, lg = _post(cat, x_prompt, mod_p, *post_w, per_row_mod=False, b0=b0, nb=nb)
        return glu, ret, x2, hp, lg

    def pre_sample():
        xs3 = x_sample.reshape(1, bs, d)
        glu, q, k, v, sg = _proj(xs3, mod_s, row(g_mix[0]), w_in_b, cos_s, sin_s, per_row_mod=True, **dims)
        cat, ret, new_conv = _mix1(glu, q, k, v, sg, state_conv, state_ret, conv_w[0], *norm_rows, gam,
                                   **dims)
        x2, hp, lg = _post(cat, xs3, mod_s, *post_w, per_row_mod=True)
        return new_conv, ret, x2, hp, lg

    def route_and_dispatch(hps, lgs):
        lg = lgs[0] if len(lgs) == 1 else jnp.concatenate(lgs, axis=1)
        tokens = lg.shape[1]
        eidx, wts, rank, cnt = _route(lg, bias_col)
        counts = cnt[:, 0].astype(I32)
        n_rows = -(-(tokens * TOP_K + ne * (tm - 1)) // tm) * tm
        dest3 = _dest_rows(counts, eidx, rank).reshape(tokens // TOKEN_TILE, SUBLANES, TOKEN_TILE)
        xs = _dispatch([h.reshape(-1, hw) for h in hps], dest3, n_rows)
        return counts, dest3, wts, xs.reshape(n_rows * ROW_WORDS, LANES)

    def undispatch(ys, dest3, after=None):
        tokens = dest3.shape[0] * TOKEN_TILE
        z = _undispatch(ys.reshape(-1, hw), dest3, tokens, after)
        return z.reshape(TOP_K, tokens * ROW_WORDS, LANES)

    nb1 = max(1, bp // 4)
    nb0 = bp - nb1
    new_conv_s, ret_s, x2_s, hp_s, lg_s = pre_sample()
    glu_0, ret_0, x2_0, hp_0, lg_0 = pre_prompt(0, nb0, after=lg_s)
    counts_0, dest_0, wts_0, xs_0 = route_and_dispatch([hp_0], [lg_0])
    glu_1, ret_1, x2_1, hp_1, lg_1 = pre_prompt(nb0, nb1, after=dest_0)
    counts_1, dest_1, wts_1, xs_1 = route_and_dispatch([hp_1, hp_s], [lg_1, lg_s])
    ys_0, ys_1 = _experts((counts_0, counts_1), (xs_0, xs_1), w1[0], w3[0], w2[0])
    z_0 = undispatch(ys_0, dest_0)
    z_1 = undispatch(ys_1, dest_1, after=z_0[0, :SUBLANES])
    y_p = _combine(z_0, 0, wts_0, x2_0, hp_0, mod_p, row(g_final), shared_w, rows_per_mod=lp, per_row_mod=False,
                   out_rows=bp * lp)
    y_p = _combine(z_1, 0, wts_1, x2_1, hp_1, mod_p, row(g_final), shared_w, rows_per_mod=lp, per_row_mod=False,
                   b0=nb0, out_rows=bp * lp, into=y_p)
    y_s = _combine(z_1, nb1 * lp, wts_1, x2_s, hp_s, mod_s, row(g_final), shared_w, rows_per_mod=bs,
                   per_row_mod=True)
    ret_p = jnp.concatenate([ret_0, ret_1], axis=0)

    tail = lambda g, n: g.reshape(n, lp, dc)[:, lp - CONV_BUF:, :]
    new_conv_p = jnp.concatenate([tail(glu_0, nb0), tail(glu_1, nb1)], axis=0)
    return (y_p.reshape(bp, lp, d), y_s.reshape(bs, ls, d), new_conv_p[None], ret_p[None],
            new_conv_s, ret_s)
```

```python
import functools
import math

import jax
import jax.numpy as jnp
import numpy as np
from jax import lax
from jax.experimental import pallas as pl
from jax.experimental.pallas import tpu as pltpu
from jax.experimental.pallas import tpu_sc as plsc

F32 = jnp.float32
BF16 = jnp.bfloat16
U32 = jnp.uint32
I32 = jnp.int32

EPS = 1e-6
PAST_LEN = 16384
RET_HEADS = 4
RET_CHUNK = 128
CONV_WIDTH = 31
CONV_BUF = CONV_WIDTH - 1
ROPE_BASE = 10000.0
N_EXPERTS = 64
TOP_K = 6
N_GROUPS = 8
TOPK_GROUPS = 4
ROUTED_SCALE = 2.5

LANES = 128
SUBLANES = 8
CONV_PAD = 32
EXPERT_ROWS = 256
ROUTE_TILE = 1024
DEST_TILE = 4096
TOKEN_TILE = 128
COMBINE_TILE = 512
VMEM_LIMIT = 56 * 1024 * 1024
SC_CORES = 2
SC_SUBCORES = 16
SC_WORKERS = SC_CORES * SC_SUBCORES
SCATTER_ROWS = 64
GATHER_ROWS = 16
ROW_WORDS = 4
X_SLOTS = 6
X_AHEAD = 4
Y_SLOTS = 4

HI_MASK = 0xFFFF0000


def _sigmoid(x):
    return jax.nn.sigmoid(x)


def _silu(x):
    return x * jax.nn.sigmoid(x)


def _pack_halves(lo, hi):
    lo_u = lax.bitcast_convert_type(lo.astype(BF16).astype(F32), U32) >> 16
    hi_u = lax.bitcast_convert_type(hi.astype(BF16).astype(F32), U32) & jnp.uint32(HI_MASK)
    return hi_u | lo_u


def _unpack_halves(p):
    lo = lax.bitcast_convert_type(p << 16, F32)
    hi = lax.bitcast_convert_type(p & jnp.uint32(HI_MASK), F32)
    return lo, hi


def _store_rows(ref, x):
    rows = x.shape[0]
    for j in range(ROW_WORDS):
        ref[pl.ds(j, rows, stride=ROW_WORDS), :] = x[:, LANES * j:LANES * (j + 1)]


def _load_row_word(ref, j, rows):
    return ref[pl.ds(j, rows, stride=ROW_WORDS), :]


def _ada_body(c_ref, w_ref, b_ref, o_ref):
    s = _silu(c_ref[...]).astype(BF16)
    o_ref[...] = jnp.dot(s, w_ref[...].astype(BF16), preferred_element_type=F32) + b_ref[...]


def _ada(c_all, w_ada, b_ada):
    rows, d = c_all.shape
    n = w_ada.shape[1]
    tn = 2048
    return pl.pallas_call(
        _ada_body,
        grid=(n // tn,),
        in_specs=[
            pl.BlockSpec((rows, d), lambda j: (0, 0)),
            pl.BlockSpec((d, tn), lambda j: (0, j)),
            pl.BlockSpec((1, tn), lambda j: (0, j)),
        ],
        out_specs=pl.BlockSpec((rows, tn), lambda j: (0, j)),
        out_shape=jax.ShapeDtypeStruct((rows, n), F32),
        compiler_params=pltpu.CompilerParams(vmem_limit_bytes=VMEM_LIMIT),
        name="ada",
    )(c_all, w_ada, b_ada)


def _rope_body(inv_ref, cos_ref, sin_ref, *, pos0, tl, half):
    row = lax.broadcasted_iota(I32, (tl, LANES), 0) + pl.program_id(0) * tl
    ang = (row.astype(F32) + pos0) * inv_ref[...]
    lane = lax.broadcasted_iota(I32, (tl, LANES), 1)
    s = jnp.sin(ang)
    cos_ref[...] = jnp.cos(ang)
    sin_ref[...] = jnp.where(lane < half, -s, s)


def _rope_tables(inv2, rows, pos0):
    tl = min(rows, 256)
    return pl.pallas_call(
        functools.partial(_rope_body, pos0=float(pos0), tl=tl, half=LANES // 2),
        grid=(rows // tl,),
        in_specs=[pl.BlockSpec((1, LANES), lambda i: (0, 0))],
        out_specs=[pl.BlockSpec((tl, LANES), lambda i: (i, 0))] * 2,
        out_shape=[jax.ShapeDtypeStruct((rows, LANES), F32)] * 2,
        name="rope",
    )(inv2)


def _modulated_rmsnorm(x, g, sc, sh):
    ms = jnp.mean(x * x, axis=-1, keepdims=True)
    h = x * lax.rsqrt(ms + EPS) * g
    return h * (1.0 + sc) + sh


def _proj_body(x_ref, sh_ref, sc_ref, g_ref, w_ref, cos_ref, sin_ref, *rest, dc, dr, hd, rope_rows):
    glu_ref, q_ref, k_ref, v_ref, sg_ref = rest[-5:]
    hb = _modulated_rmsnorm(x_ref[...], g_ref[...], sc_ref[...], sh_ref[...]).astype(BF16)

    def proj(lo, n):
        return jnp.dot(hb, w_ref[:, lo:lo + n], preferred_element_type=F32)

    glu_ref[...] = proj(0, dc) * _sigmoid(proj(dc, dc))
    cos = cos_ref[...] if rope_rows else cos_ref[0:1, :]
    sin = sin_ref[...] if rope_rows else sin_ref[0:1, :]
    for ref, lo, scale in ((q_ref, 2 * dc, hd ** -0.5), (k_ref, 2 * dc + dr, None)):
        t = proj(lo, dr)
        for hh in range(dr // hd):
            th = t[:, hh * hd:(hh + 1) * hd]
            r = th * cos + pltpu.roll(th, hd // 2, 1) * sin
            if scale is not None:
                r = r * scale
            ref[:, hh * hd:(hh + 1) * hd] = r.astype(BF16)
    v_ref[...] = proj(2 * dc + 2 * dr, dr).astype(BF16)
    sg_ref[...] = _silu(proj(2 * dc + 3 * dr, dr)).astype(BF16)


def _proj(x, mod, g_mix, w_in_b, cos2, sin2, *, dc, dr, hd, per_row_mod, b0=0, nb=None, after=None):
    nb_all, length, d = x.shape
    nb = nb_all if nb is None else nb
    tl = min(length, 512)
    x2 = x.reshape(nb_all * length, d)
    nl = length // tl
    if per_row_mod:
        mod_spec = lambda j: pl.BlockSpec((tl, d), lambda b, l: (l, j))
        rope_spec = pl.BlockSpec((SUBLANES, LANES), lambda b, l: (0, 0))
    else:
        mod_spec = lambda j: pl.BlockSpec((None, None, 1, d), lambda b, l: (b + b0, j, 0, 0))
        rope_spec = pl.BlockSpec((tl, LANES), lambda b, l: (l, 0))
    row_spec = lambda w: pl.BlockSpec((tl, w), lambda b, l: (b * nl + l, 0))
    t = nb * length
    outs = pl.pallas_call(
        functools.partial(_proj_body, dc=dc, dr=dr, hd=hd, rope_rows=not per_row_mod),
        grid=(nb, nl),
        in_specs=[
            pl.BlockSpec((tl, d), lambda b, l: ((b + b0) * nl + l, 0)), mod_spec(0), mod_spec(1),
            pl.BlockSpec((1, d), lambda b, l: (0, 0)),
            pl.BlockSpec(w_in_b.shape, lambda b, l: (0, 0)),
            rope_spec, rope_spec,
        ] + ([] if after is None else [pl.BlockSpec(memory_space=pl.ANY)]),
        out_specs=[row_spec(dc), row_spec(dr), row_spec(dr), row_spec(dr), row_spec(dr)],
        out_shape=[
            jax.ShapeDtypeStruct((t, dc), F32),
            jax.ShapeDtypeStruct((t, dr), BF16),
            jax.ShapeDtypeStruct((t, dr), BF16),
            jax.ShapeDtypeStruct((t, dr), BF16),
            jax.ShapeDtypeStruct((t, dr), BF16),
        ],
        compiler_params=pltpu.CompilerParams(vmem_limit_bytes=VMEM_LIMIT),
        name="proj",
    )(x2, mod, mod, g_mix, w_in_b, cos2, sin2, *(() if after is None else (after,)))
    return outs


def _layernorm_silu(c, g, b):
    mu = jnp.mean(c, axis=-1, keepdims=True)
    d = c - mu
    var = jnp.mean(d * d, axis=-1, keepdims=True)
    return _silu(d * lax.rsqrt(var + EPS) * g + b)


def _groupnorm(o, g, b):
    mu = jnp.mean(o, axis=-1, keepdims=True)
    d = o - mu
    var = jnp.mean(d * d, axis=-1, keepdims=True)
    return d * lax.rsqrt(var + EPS) * g + b


def _mix_body(glu_ref, q_ref, k_ref, v_ref, sg_ref, cw_ref, cb_ref, lng_ref, lnb_ref, rg_ref, rb_ref,
              mask_ref, qd_ref, kd_ref, cd_ref, cat_ref, st_ref, buf, cscr, *, tl, dc, hd, nh, chunk):
    nslab = dc // LANES

    @pl.when(pl.program_id(1) == 0)
    def _():
        buf[:, 0:CONV_PAD, :] = jnp.zeros((nslab, CONV_PAD, LANES), F32)
        st_ref[...] = jnp.zeros(st_ref.shape, F32)

    for j in range(nslab):
        buf[j, CONV_PAD:CONV_PAD + tl, :] = glu_ref[:, LANES * j:LANES * (j + 1)]
    first = CONV_PAD - CONV_BUF
    rows_per_iter = 8 * SUBLANES
    for j in range(nslab):
        cols = slice(LANES * j, LANES * (j + 1))
        wv = [jnp.broadcast_to(cw_ref[t:t + 1, cols], (SUBLANES, LANES)) for t in range(CONV_WIDTH)]
        bias = jnp.broadcast_to(cb_ref[0:1, cols], (SUBLANES, LANES))

        def body(r, carry, j=j, cols=cols, wv=wv, bias=bias):
            base = pl.multiple_of(r * rows_per_iter, rows_per_iter)
            for u in range(rows_per_iter // SUBLANES):
                acc = bias
                for t in range(CONV_WIDTH):
                    acc = acc + wv[t] * buf[j, pl.ds(base + (u * SUBLANES + first + t), SUBLANES), :]
                cscr[pl.ds(base + u * SUBLANES, SUBLANES), cols] = acc
            return carry

        lax.fori_loop(0, tl // rows_per_iter, body, 0)
    for j in range(nslab):
        buf[j, 0:CONV_PAD, :] = buf[j, tl:tl + CONV_PAD, :]
    cat_ref[:, 0:dc] = _layernorm_silu(cscr[...], lng_ref[...], lnb_ref[...]).astype(BF16)

    nt = (((1,), (1,)), ((), ()))
    tn = (((0,), (0,)), ((), ()))
    for c in range(tl // chunk):
        rows = slice(c * chunk, (c + 1) * chunk)
        for hh in range(nh):
            cols = slice(hh * hd, (hh + 1) * hd)
            qh = q_ref[rows, cols]
            kh = k_ref[rows, cols]
            vh = v_ref[rows, cols]
            s = st_ref[0, hh]
            scores = lax.dot_general(qh, kh, nt, preferred_element_type=F32) * mask_ref[hh]
            inner = jnp.dot(scores.astype(BF16), vh, preferred_element_type=F32)
            qd = (qh.astype(F32) * qd_ref[hh]).astype(BF16)
            cross = jnp.dot(qd, s.astype(BF16), preferred_element_type=F32)
            kd = (kh.astype(F32) * kd_ref[hh]).astype(BF16)
            st_ref[0, hh] = cd_ref[hh] * s + lax.dot_general(kd, vh, tn, preferred_element_type=F32)
            o = _groupnorm(inner + cross, rg_ref[0:1, cols], rb_ref[0:1, cols])
            cat_ref[rows, dc + hh * hd:dc + (hh + 1) * hd] = (o * sg_ref[rows, cols].astype(F32)).astype(BF16)


def _mix(glu, q, k, v, sg, conv_w, conv_b, ln_g, ln_b, rg, rb, tables, *, nb, length, dc, dr, hd):
    nh = dr // hd
    chunk = math.gcd(length, RET_CHUNK)
    tl = min(length, 256)
    nl = length // tl
    mask, qd, kd, cd = tables
    row_spec = lambda w: pl.BlockSpec((tl, w), lambda b, l: (b * nl + l, 0))
    full = lambda a: pl.BlockSpec(a.shape, lambda b, l: (0,) * a.ndim)
    cat, st = pl.pallas_call(
        functools.partial(_mix_body, tl=tl, dc=dc, hd=hd, nh=nh, chunk=chunk),
        grid=(nb, nl),
        in_specs=[row_spec(dc), row_spec(dr), row_spec(dr), row_spec(dr), row_spec(dr),
                  full(conv_w), full(conv_b), full(ln_g), full(ln_b), full(rg), full(rb),
                  full(mask), full(qd), full(kd), full(cd)],
        out_specs=[row_spec(dc + dr), pl.BlockSpec((1, nh, hd, hd), lambda b, l: (b, 0, 0, 0))],
        out_shape=[jax.ShapeDtypeStruct((nb * length, dc + dr), BF16),
                   jax.ShapeDtypeStruct((nb, nh, hd, hd), F32)],
        scratch_shapes=[pltpu.VMEM((dc // LANES, tl + CONV_PAD, LANES), F32),
                        pltpu.VMEM((tl, dc), F32)],
        compiler_params=pltpu.CompilerParams(dimension_semantics=("arbitrary", "arbitrary")),
        name="mix",
    )(glu, q, k, v, sg, conv_w, conv_b, ln_g, ln_b, rg, rb, mask, qd, kd, cd)
    return cat, st


def _mix1_body(glu_ref, q_ref, k_ref, v_ref, sg_ref, sc_ref, s0_ref, cw_ref, cb_ref, lng_ref, lnb_ref,
               rg_ref, rb_ref, gam_ref, cat_ref, st_ref, nc_ref, cscr, oscr, qf, kf, vf, *, tb, dc, hd, nh):
    w_hist = cw_ref[0:CONV_BUF, :]
    w_last = cw_ref[CONV_BUF:CONV_WIDTH, :]
    for bb in range(tb):
        hist = jnp.sum(sc_ref[bb] * w_hist, axis=0, keepdims=True)
        cscr[bb:bb + 1, :] = hist + glu_ref[bb:bb + 1, :] * w_last + cb_ref[...]
        nc_ref[bb, 0:CONV_BUF - 1, :] = sc_ref[bb, 1:CONV_BUF, :]
        nc_ref[bb, CONV_BUF - 1:CONV_BUF, :] = glu_ref[bb:bb + 1, :]
    cat_ref[:, 0:dc] = _layernorm_silu(cscr[...], lng_ref[...], lnb_ref[...]).astype(BF16)

    tn = (((0,), (0,)), ((), ()))
    rowid = lax.broadcasted_iota(I32, (tb, hd), 0)
    qf[...] = q_ref[...].astype(F32)
    kf[...] = k_ref[...].astype(F32)
    vf[...] = v_ref[...].astype(F32)
    for hh in range(nh):
        cols = slice(hh * hd, (hh + 1) * hd)
        qa = q_ref[:, cols]
        ka = k_ref[:, cols]
        gam = gam_ref[hh, 0:1, :]
        for bb in range(tb):
            onehot = (rowid == bb).astype(BF16)
            qcol = lax.dot_general(qa, onehot, tn, preferred_element_type=F32)
            kcol = lax.dot_general(ka, onehot, tn, preferred_element_type=F32)
            s0 = s0_ref[bb, hh]
            qrow = qf[bb:bb + 1, cols]
            krow = kf[bb:bb + 1, cols]
            vrow = vf[bb:bb + 1, cols]
            qk = jnp.sum(qrow * krow, axis=-1, keepdims=True)
            cross = gam * jnp.sum(qcol * s0, axis=0, keepdims=True)
            st_ref[bb, hh] = gam * s0 + kcol * vrow
            oscr[bb:bb + 1, cols] = qk * vrow + cross
    for hh in range(nh):
        cols = slice(hh * hd, (hh + 1) * hd)
        o = _groupnorm(oscr[:, cols], rg_ref[0:1, cols], rb_ref[0:1, cols])
        cat_ref[:, dc + hh * hd:dc + (hh + 1) * hd] = (o * sg_ref[:, cols].astype(F32)).astype(BF16)


def _mix1(glu, q, k, v, sg, state_conv, state_ret, conv_w, conv_b, ln_g, ln_b, rg, rb, gam, *, dc, dr, hd):
    nb = glu.shape[0]
    nh = dr // hd
    tb = 16
    row_spec = lambda w: pl.BlockSpec((tb, w), lambda i: (i, 0))
    full = lambda a: pl.BlockSpec(a.shape, lambda i: (0,) * a.ndim)
    st_spec = pl.BlockSpec((None, tb, nh, hd, hd), lambda i: (0, i, 0, 0, 0))
    conv_spec = pl.BlockSpec((None, tb, CONV_BUF, dc), lambda i: (0, i, 0, 0))
    cat, st, new_conv = pl.pallas_call(
        functools.partial(_mix1_body, tb=tb, dc=dc, hd=hd, nh=nh),
        grid=(nb // tb,),
        in_specs=[row_spec(dc), row_spec(dr), row_spec(dr), row_spec(dr), row_spec(dr), conv_spec, st_spec,
                  full(conv_w), full(conv_b), full(ln_g), full(ln_b), full(rg), full(rb), full(gam)],
        out_specs=[row_spec(dc + dr), st_spec, conv_spec],
        out_shape=[jax.ShapeDtypeStruct((nb, dc + dr), BF16),
                   jax.ShapeDtypeStruct((1, nb, nh, hd, hd), F32),
                   jax.ShapeDtypeStruct((1, nb, CONV_BUF, dc), F32)],
        scratch_shapes=[pltpu.VMEM((tb, dc), F32)] + [pltpu.VMEM((tb, dr), F32)] * 4,
        compiler_params=pltpu.CompilerParams(vmem_limit_bytes=VMEM_LIMIT),
        name="mix1",
    )(glu, q, k, v, sg, state_conv, state_ret, conv_w, conv_b, ln_g, ln_b, rg, rb, gam)
    return cat, st, new_conv


def _post_body(cat_ref, x_ref, gtm_ref, scf_ref, shf_ref, g_ref, wo_ref, wrh_ref, wrl_ref, x1_ref, hp_ref, lg_ref):
    d = x_ref.shape[1]
    y = jnp.dot(cat_ref[...], wo_ref[...], preferred_element_type=F32)
    x1 = x_ref[...] + gtm_ref[...] * y
    x1_ref[...] = x1
    h = _modulated_rmsnorm(x1, g_ref[...], scf_ref[...], shf_ref[...])
    hb = h.astype(BF16)
    _store_rows(hp_ref, _pack_halves(h[:, 0:d // 2], h[:, d // 2:d]))
    hl = (h - hb.astype(F32)).astype(BF16)
    nt = (((1,), (1,)), ((), ()))
    lg_ref[...] = (lax.dot_general(wrh_ref[...], hb, nt, preferred_element_type=F32)
                   + lax.dot_general(wrh_ref[...], hl, nt, preferred_element_type=F32)
                   + lax.dot_general(wrl_ref[...], hb, nt, preferred_element_type=F32))


def _post(cat, x, mod, g_ffn, wo_b, wrh, wrl, *, per_row_mod, b0=0, nb=None):
    nb_all, length, d = x.shape
    nb = nb_all if nb is None else nb
    tl = min(length, 1024)
    nl = length // tl
    t = nb * length
    ne = wrh.shape[0]
    x2d = x.reshape(nb_all * length, d)
    if per_row_mod:
        mod_spec = lambda j: pl.BlockSpec((tl, d), lambda b, l: (l, j))
    else:
        mod_spec = lambda j: pl.BlockSpec((None, None, 1, d), lambda b, l: (b + b0, j, 0, 0))
    row_spec = lambda w: pl.BlockSpec((tl, w), lambda b, l: (b * nl + l, 0))
    full = lambda a: pl.BlockSpec(a.shape, lambda b, l: (0,) * a.ndim)
    return pl.pallas_call(
        _post_body,
        grid=(nb, nl),
        in_specs=[row_spec(d), pl.BlockSpec((tl, d), lambda b, l: ((b + b0) * nl + l, 0)),
                  mod_spec(2), mod_spec(4), mod_spec(3),
                  full(g_ffn), full(wo_b), full(wrh), full(wrl)],
        out_specs=[row_spec(d),
                   pl.BlockSpec((tl * ROW_WORDS, LANES), lambda b, l: (b * nl + l, 0)),
                   pl.BlockSpec((ne, tl), lambda b, l: (0, b * nl + l))],
        out_shape=[jax.ShapeDtypeStruct((t, d), F32),
                   jax.ShapeDtypeStruct((t * ROW_WORDS, LANES), U32),
                   jax.ShapeDtypeStruct((ne, t), F32)],
        compiler_params=pltpu.CompilerParams(vmem_limit_bytes=VMEM_LIMIT),
        name="post",
    )(cat, x2d, mod, mod, mod, g_ffn, wo_b, wrh, wrl)


def _first_max(x, idx, sentinel):
    m = jnp.max(x, axis=0, keepdims=True)
    f = jnp.min(jnp.where(x == m, idx, sentinel), axis=0, keepdims=True)
    return m, f


def _route_body(lg_ref, bias_ref, e_ref, w_ref, r_ref, cnt_ref, cnt_scr, *, tr, ne, ng, topk, topg):
    @pl.when(pl.program_id(0) == 0)
    def _():
        cnt_scr[...] = jnp.zeros(cnt_scr.shape, F32)

    per = ne // ng
    neg = -jnp.inf
    scores = _sigmoid(lg_ref[...])
    sel = scores + bias_ref[...]
    sub = lax.broadcasted_iota(I32, (per, tr), 0)
    gs = []
    for g in range(ng):
        s_g = sel[g * per:(g + 1) * per, :]
        m1, f1 = _first_max(s_g, sub, per)
        m2 = jnp.max(jnp.where(sub == f1, neg, s_g), axis=0, keepdims=True)
        gs.append(m1 + m2)
    gsc = jnp.concatenate(gs, axis=0)
    gi = lax.broadcasted_iota(I32, (ng, tr), 0)
    keep = jnp.zeros((ng, tr), F32)
    for _ in range(topg):
        _, f = _first_max(gsc, gi, ng)
        pick = gi == f
        keep = jnp.where(pick, 1.0, keep)
        gsc = jnp.where(pick, neg, gsc)
    work = jnp.concatenate(
        [jnp.where(keep[g:g + 1, :] > 0.5, sel[g * per:(g + 1) * per, :], neg) for g in range(ng)], axis=0)
    ei = lax.broadcasted_iota(I32, (ne, tr), 0)
    picks, es, ws = [], [], []
    for _ in range(topk):
        _, f = _first_max(work, ei, ne)
        pick = ei == f
        picks.append(pick)
        es.append(f)
        ws.append(jnp.sum(jnp.where(pick, scores, 0.0), axis=0, keepdims=True))
        work = jnp.where(pick, neg, work)
    wsum = ws[0]
    for w in ws[1:]:
        wsum = wsum + w
    scale = ROUTED_SCALE / wsum
    chosen = picks[0]
    for p in picks[1:]:
        chosen = jnp.logical_or(chosen, p)
    chosen_f = chosen.astype(F32)
    t_row = lax.broadcasted_iota(I32, (tr, tr), 0)
    t_col = lax.broadcasted_iota(I32, (tr, tr), 1)
    before = (t_row < t_col).astype(BF16)
    prior = cnt_scr[:, 0:1] + jnp.dot(chosen_f.astype(BF16), before, preferred_element_type=F32)
    rs = [jnp.sum(jnp.where(p, prior, 0.0), axis=0, keepdims=True).astype(I32) for p in picks]
    pad_i = jnp.zeros((SUBLANES - topk, tr), I32)
    pad_f = jnp.zeros((SUBLANES - topk, tr), F32)
    e_ref[...] = jnp.concatenate(es + [pad_i], axis=0)
    w_ref[...] = jnp.concatenate([w * scale for w in ws] + [pad_f], axis=0)
    r_ref[...] = jnp.concatenate(rs + [pad_i], axis=0)
    total = cnt_scr[:, 0:1] + jnp.sum(chosen_f, axis=1, keepdims=True)
    cnt_scr[...] = jnp.broadcast_to(total, cnt_scr.shape)
    cnt_ref[...] = jnp.broadcast_to(total, cnt_ref.shape)


def _token_tile(t, limit):
    return max(m for m in range(LANES, limit + 1, LANES) if t % m == 0)


def _route(logits_t, bias_col):
    ne, t = logits_t.shape
    tr = _token_tile(t, ROUTE_TILE)
    tok = lambda dt: jax.ShapeDtypeStruct((SUBLANES, t), dt)
    tok_spec = pl.BlockSpec((SUBLANES, tr), lambda i: (0, i))
    return pl.pallas_call(
        functools.partial(_route_body, tr=tr, ne=ne, ng=N_GROUPS, topk=TOP_K, topg=TOPK_GROUPS),
        grid=(t // tr,),
        in_specs=[pl.BlockSpec((ne, tr), lambda i: (0, i)), pl.BlockSpec((ne, 1), lambda i: (0, 0))],
        out_specs=[tok_spec, tok_spec, tok_spec, pl.BlockSpec((ne, LANES), lambda i: (0, 0))],
        out_shape=[tok(I32), tok(F32), tok(I32), jax.ShapeDtypeStruct((ne, LANES), F32)],
        scratch_shapes=[pltpu.VMEM((ne, LANES), F32)],
        compiler_params=pltpu.CompilerParams(dimension_semantics=("arbitrary",)),
        name="route",
    )(logits_t, bias_col)


def _dest_body(cnt_ref, e_ref, r_ref, d_ref, ps, *, ne, tr):
    shift = EXPERT_ROWS.bit_length() - 1

    @pl.when(pl.program_id(0) == 0)
    def _():
        def step(j, start):
            ps[j] = start
            return start + lax.shift_left(lax.shift_right_logical(cnt_ref[j] + (EXPERT_ROWS - 1), shift), shift)

        lax.fori_loop(0, ne, step, jnp.int32(0))

    e = e_ref[...]
    base = jnp.zeros(e.shape, I32)
    for j in range(ne):
        base = jnp.where(e == j, ps[j], base)
    dest = base + r_ref[...]
    for m in range(tr // TOKEN_TILE):
        d_ref[SUBLANES * m:SUBLANES * (m + 1), :] = dest[:, TOKEN_TILE * m:TOKEN_TILE * (m + 1)]


def _dest_rows(counts, eidx, rank):
    rows, t = eidx.shape
    ne = counts.shape[0]
    tr = _token_tile(t, DEST_TILE)
    spec = pl.BlockSpec((rows, tr), lambda i, cnt: (0, i))
    return pl.pallas_call(
        functools.partial(_dest_body, ne=ne, tr=tr),
        grid_spec=pltpu.PrefetchScalarGridSpec(
            num_scalar_prefetch=1, grid=(t // tr,), in_specs=[spec, spec],
            out_specs=pl.BlockSpec((tr // TOKEN_TILE * rows, TOKEN_TILE), lambda i, cnt: (i, 0)),
            scratch_shapes=[pltpu.SMEM((ne,), I32)]),
        out_shape=jax.ShapeDtypeStruct((t // TOKEN_TILE * rows, TOKEN_TILE), I32),
        compiler_params=pltpu.CompilerParams(dimension_semantics=("arbitrary",)),
        name="dest",
    )(counts, eidx, rank)


def _sc_mesh():
    return plsc.VectorSubcoreMesh(core_axis_name="c", subcore_axis_name="s")


def _sc_worker_id():
    return lax.axis_index("s") * SC_CORES + lax.axis_index("c")


def _index_block(dest_ref, chunk, width):
    per_tile = TOKEN_TILE // width
    return dest_ref.at[chunk // per_tile, :, pl.ds((chunk % per_tile) * width, width)]


def _dispatch(hps, dest3, n_rows):
    w = SCATTER_ROWS
    width = hps[0].shape[1]
    bounds = [0]
    for h in hps:
        bounds.append(bounds[-1] + h.shape[0] // w)
    nch = bounds[-1]
    nsrc = len(hps)

    @functools.partial(
        pl.kernel, mesh=_sc_mesh(),
        out_type=jax.ShapeDtypeStruct((n_rows, width), U32),
        scratch_types=[pltpu.VMEM((2, SUBLANES, w), I32), pltpu.VMEM((2, w, width), U32),
                       pltpu.SemaphoreType.DMA, pltpu.SemaphoreType.DMA, pltpu.SemaphoreType.DMA],
        compiler_params=pltpu.CompilerParams(use_tc_tiling_on_sc=False),
        name="dispatch",
    )
    def run(*refs):
        src_refs, dest_ref, xs_ref = refs[:nsrc], refs[nsrc], refs[nsrc + 1]
        idx_v, rows_v = refs[nsrc + 2:nsrc + 4]
        sem_load = refs[nsrc + 4:nsrc + 6]
        sem_scatter = refs[nsrc + 6]
        wid = _sc_worker_id()

        def start_loads(c, slot):
            pltpu.async_copy(_index_block(dest_ref, c, w), idx_v.at[slot], sem_load[slot])
            for i, src in enumerate(src_refs):
                @pl.when(jnp.logical_and(c >= bounds[i], c < bounds[i + 1]))
                def _(src=src, lo=bounds[i]):
                    pltpu.async_copy(src.at[pl.ds((c - lo) * w, w)], rows_v.at[slot], sem_load[slot])

        def wait_loads(slot):
            pltpu.make_async_copy(_index_block(dest_ref, 0, w), idx_v.at[slot], sem_load[slot]).wait()
            pltpu.make_async_copy(src_refs[0].at[pl.ds(0, w)], rows_v.at[slot], sem_load[slot]).wait()

        @pl.when(wid < nch)
        def _():
            start_loads(wid, 0)

        @pl.loop(0, pl.cdiv(pl.cdiv(nch, SC_WORKERS), 2))
        def _(rr):
            for slot in range(2):
                c = (rr * 2 + slot) * SC_WORKERS + wid

                @pl.when(c < nch)
                def _(c=c, slot=slot):
                    wait_loads(slot)

                    @pl.when(c + SC_WORKERS < nch)
                    def _():
                        start_loads(c + SC_WORKERS, 1 - slot)

                    copies = [pltpu.async_copy(rows_v.at[slot], xs_ref.at[idx_v.at[slot, k]], sem_scatter)
                              for k in range(TOP_K)]
                    for cp in copies:
                        cp.wait()

    return run(*hps, dest3)


def _undispatch(ys, dest3, n_tokens, after=None):
    w = GATHER_ROWS
    width = ys.shape[1]
    nch = n_tokens // w

    @functools.partial(
        pl.kernel, mesh=_sc_mesh(),
        out_type=jax.ShapeDtypeStruct((TOP_K, n_tokens, width), U32),
        scratch_types=[pltpu.VMEM((2, SUBLANES, w), I32), pltpu.VMEM((2, TOP_K, w, width), U32),
                       pltpu.SemaphoreType.DMA, pltpu.SemaphoreType.DMA, pltpu.SemaphoreType.DMA],
        compiler_params=pltpu.CompilerParams(use_tc_tiling_on_sc=False),
        name="undispatch",
    )
    def run(*refs):
        ys_ref, dest_ref = refs[:2]
        z_ref, idx_v, bufs = refs[-6:-3]
        sem_gather = refs[-3:-1]
        sem_store = refs[-1]
        wid = _sc_worker_id()

        def start_gathers(c, slot):
            pltpu.sync_copy(_index_block(dest_ref, c, w), idx_v.at[slot])
            for k in range(TOP_K):
                pltpu.async_copy(ys_ref.at[idx_v.at[slot, k]], bufs.at[slot, k], sem_gather[slot])

        def wait_gathers(slot):
            for k in range(TOP_K):
                pltpu.make_async_copy(ys_ref.at[idx_v.at[slot, k]], bufs.at[slot, k], sem_gather[slot]).wait()

        @pl.when(wid < nch)
        def _():
            start_gathers(wid, 0)

        @pl.loop(0, pl.cdiv(pl.cdiv(nch, SC_WORKERS), 2))
        def _(rr):
            for slot in range(2):
                c = (rr * 2 + slot) * SC_WORKERS + wid

                @pl.when(c < nch)
                def _(c=c, slot=slot):
                    @pl.when(c + SC_WORKERS < nch)
                    def _():
                        start_gathers(c + SC_WORKERS, 1 - slot)

                    wait_gathers(slot)
                    stores = [pltpu.async_copy(bufs.at[slot, k], z_ref.at[k, pl.ds(c * w, w)], sem_store)
                              for k in range(TOP_K)]
                    for cp in stores:
                        cp.wait()

    return run(ys, dest3) if after is None else run(ys, dest3, after)


PART_SHIFT = 24


def _expert_body(*refs, tm, ne, nparts):
    cnt_refs = refs[:nparts]
    xs_refs = refs[nparts:2 * nparts]
    w1_ref, w3_ref, w2_ref = refs[2 * nparts:2 * nparts + 3]
    ys_refs = refs[2 * nparts + 3:3 * nparts + 3]
    (w1f, w3f, w2f, w1s, w3s, w2s, xbuf, ybuf, xlo, xhi, sched, sem_x, sem_y, sem_w) = refs[3 * nparts + 3:]
    blk_words = tm * ROW_WORDS
    half = ROW_WORDS * LANES
    shift = tm.bit_length() - 1

    def n_blocks_of(p, e):
        return lax.shift_right_logical(cnt_refs[p][e] + (tm - 1), shift)

    def n_all(e):
        n = n_blocks_of(0, e)
        for p in range(1, nparts):
            n = n + n_blocks_of(p, e)
        return n

    def next_nonempty(e):
        return lax.while_loop(
            lambda c: jnp.logical_and(c < ne, n_all(jnp.minimum(c, ne - 1)) == 0), lambda c: c + 1, e)

    def plan(e, carry):
        i, starts = carry[0], list(carry[1:])
        for p in range(nparts):
            n = n_blocks_of(p, e)

            def put(j, c, p=p, i=i, start=starts[p]):
                sched[i + j] = (start + j) + (p << PART_SHIFT)
                return c

            lax.fori_loop(0, n, put, 0)
            i = i + n
            starts[p] = starts[p] + n
        return (i, *starts)

    nu = lax.fori_loop(0, ne, plan, (jnp.int32(0),) * (nparts + 1))[0]

    def rows_of(code):
        blk = code & ((1 << PART_SHIFT) - 1)
        return pl.ds(pl.multiple_of(blk * blk_words, blk_words), blk_words)

    def x_start(i, slot):
        code = sched[i]
        for p in range(nparts):
            @pl.when(lax.shift_right_logical(code, PART_SHIFT) == p)
            def _(p=p):
                pltpu.make_async_copy(xs_refs[p].at[rows_of(code), :], xbuf.at[slot], sem_x.at[slot]).start()

    def x_wait(slot):
        pltpu.make_async_copy(xs_refs[0].at[pl.ds(0, blk_words), :], xbuf.at[slot], sem_x.at[slot]).wait()

    def y_start(i, slot):
        code = sched[i]
        for p in range(nparts):
            @pl.when(lax.shift_right_logical(code, PART_SHIFT) == p)
            def _(p=p):
                pltpu.make_async_copy(ybuf.at[slot], ys_refs[p].at[rows_of(code), :], sem_y.at[slot]).start()

    def y_wait(slot):
        pltpu.make_async_copy(ybuf.at[slot], ys_refs[0].at[pl.ds(0, blk_words), :], sem_y.at[slot]).wait()

    def w_copies(e, ws):
        return [pltpu.make_async_copy(src.at[e], dst.at[ws], sem_w.at[ws])
                for src, dst in ((w1_ref, w1f), (w3_ref, w3f), (w2_ref, w2f))]

    for q in range(X_AHEAD):
        @pl.when(q < nu)
        def _(q=q):
            x_start(q, q)

    e_first = next_nonempty(jnp.int32(0))

    @pl.when(e_first < ne)
    def _():
        for cp in w_copies(e_first, 0):
            cp.start()

    def blocks(i, nblk):
        for b in range(nblk):
            x_wait((i + b) % X_SLOTS)
        for b in range(nblk):
            nxt = i + b + X_AHEAD

            @pl.when(nxt < nu)
            def _(nxt=nxt):
                x_start(nxt, nxt % X_SLOTS)

        for b in range(nblk):
            @pl.when(i + b >= Y_SLOTS)
            def _(b=b):
                y_wait((i + b) % Y_SLOTS)

        for b in range(nblk):
            xin = xbuf.at[(i + b) % X_SLOTS]
            for w in range(ROW_WORDS):
                lo, hi = _unpack_halves(_load_row_word(xin, w, tm))
                xlo[b, :, LANES * w:LANES * (w + 1)] = lo.astype(BF16)
                xhi[b, :, LANES * w:LANES * (w + 1)] = hi.astype(BF16)

        def up(b, wsc):
            return (jnp.dot(xlo[b], wsc[0:half, :], preferred_element_type=F32)
                    + jnp.dot(xhi[b], wsc[half:2 * half, :], preferred_element_type=F32))

        for b in range(nblk):
            hid = (_silu(up(b, w1s)) * up(b, w3s)).astype(BF16)
            y = jnp.dot(hid, w2s[...], preferred_element_type=F32)
            _store_rows(ybuf.at[(i + b) % Y_SLOTS], _pack_halves(y[:, 0:half], y[:, half:2 * half]))
        for b in range(nblk):
            y_start(i + b, (i + b) % Y_SLOTS)

    def per_expert(e, carry):
        i0, ws = carry
        n = n_all(e)

        @pl.when(n > 0)
        def _():
            for cp in w_copies(e, ws):
                cp.wait()
            w1s[...] = w1f[ws].astype(BF16)
            w3s[...] = w3f[ws].astype(BF16)
            w2s[...] = w2f[ws].astype(BF16)
            e_next = next_nonempty(e + 1)

            @pl.when(e_next < ne)
            def _():
                for cp in w_copies(e_next, 1 - ws):
                    cp.start()

            def pair(j, c):
                blocks(i0 + 2 * j, 2)
                return c

            lax.fori_loop(0, lax.shift_right_logical(n, 1), pair, 0)

            @pl.when(n % 2 == 1)
            def _():
                blocks(i0 + n - 1, 1)

        return i0 + n, jnp.where(n > 0, 1 - ws, ws)

    lax.fori_loop(0, ne, per_expert, (jnp.int32(0), jnp.int32(0)))

    for q in range(Y_SLOTS):
        @pl.when(nu > q)
        def _(q=q):
            y_wait((nu - 1 - q) % Y_SLOTS)


def _experts(counts, xss, w1e, w3e, w2e):
    tm = EXPERT_ROWS
    ne, d, de = w1e.shape
    half = ROW_WORDS * LANES
    nparts = len(xss)
    cap = sum(x.shape[0] // (tm * ROW_WORDS) for x in xss)
    assert cap < (1 << PART_SHIFT)
    anyspec = pl.BlockSpec(memory_space=pl.ANY)
    blk_buf = lambda n: pltpu.VMEM((n, tm * ROW_WORDS, LANES), U32)
    return pl.pallas_call(
        functools.partial(_expert_body, tm=tm, ne=ne, nparts=nparts),
        grid_spec=pltpu.PrefetchScalarGridSpec(
            num_scalar_prefetch=nparts,
            grid=(1,),
            in_specs=[anyspec] * (nparts + 3),
            out_specs=[anyspec] * nparts,
            scratch_shapes=[pltpu.VMEM((2, d, de), F32), pltpu.VMEM((2, d, de), F32), pltpu.VMEM((2, de, d), F32),
                            pltpu.VMEM((d, de), BF16), pltpu.VMEM((d, de), BF16), pltpu.VMEM((de, d), BF16),
                            blk_buf(X_SLOTS), blk_buf(Y_SLOTS),
                            pltpu.VMEM((2, tm, half), BF16), pltpu.VMEM((2, tm, half), BF16),
                            pltpu.SMEM((cap,), I32),
                            pltpu.SemaphoreType.DMA((X_SLOTS,)), pltpu.SemaphoreType.DMA((Y_SLOTS,)),
                            pltpu.SemaphoreType.DMA((2,))]),
        out_shape=[jax.ShapeDtypeStruct(x.shape, U32) for x in xss],
        compiler_params=pltpu.CompilerParams(dimension_semantics=("arbitrary",), vmem_limit_bytes=VMEM_LIMIT),
        name="experts",
    )(*counts, *xss, w1e, w3e, w2e)


def _combine_body(z_ref, x1_ref, hp_ref, gtf_ref, wt_ref, gfin_ref, ws1_ref, ws3_ref, ws2_ref, y_ref, xo, hlo, hhi,
                  *, td, topk):
    half = ROW_WORDS * LANES
    for w in range(ROW_WORDS):
        lo, hi = _unpack_halves(_load_row_word(hp_ref, w, td))
        hlo[:, LANES * w:LANES * (w + 1)] = lo.astype(BF16)
        hhi[:, LANES * w:LANES * (w + 1)] = hi.astype(BF16)

    def up(w_ref):
        return (jnp.dot(hlo[...], w_ref[0:half, :], preferred_element_type=F32)
                + jnp.dot(hhi[...], w_ref[half:2 * half, :], preferred_element_type=F32))

    xo[...] = jnp.dot((_silu(up(ws1_ref)) * up(ws3_ref)).astype(BF16), ws2_ref[...], preferred_element_type=F32)
    wt = wt_ref[...].T
    ws = [wt[:, k:k + 1] for k in range(topk)]
    sq = jnp.zeros((td, 1), F32)
    for j in range(ROW_WORDS):
        acc_lo = jnp.zeros((td, LANES), F32)
        acc_hi = jnp.zeros((td, LANES), F32)
        for k in range(topk):
            lo, hi = _unpack_halves(_load_row_word(z_ref.at[k], j, td))
            acc_lo = acc_lo + ws[k] * lo
            acc_hi = acc_hi + ws[k] * hi
        for base, acc in ((0, acc_lo), (half, acc_hi)):
            cols = slice(base + LANES * j, base + LANES * (j + 1))
            x = x1_ref[:, cols] + gtf_ref[:, cols] * (acc + xo[:, cols])
            xo[:, cols] = x
            sq = sq + jnp.sum(x * x, axis=-1, keepdims=True)
    rs = lax.rsqrt(sq / (2 * half) + EPS)
    y_ref[...] = xo[...] * rs * gfin_ref[...]


def _combine_body_into(*refs, td, topk):
    _combine_body(*refs[:9], *refs[10:], td=td, topk=topk)


def _combine(z, token0, wts, x1, hp, mod, g_final, shared_w, *, rows_per_mod, per_row_mod, b0=0, out_rows=None,
             into=None):
    t, d = x1.shape
    td = min(t, COMBINE_TILE)
    tile0 = token0 // td
    out_rows = t if out_rows is None else out_rows
    if per_row_mod:
        gtf_spec = pl.BlockSpec((td, d), lambda i: (i, 5))
        out0 = 0
    else:
        tiles_per_mod = rows_per_mod // td
        gtf_spec = pl.BlockSpec((None, None, 1, d), lambda i: (i // tiles_per_mod + b0, 5, 0, 0))
        out0 = b0 * tiles_per_mod
    full = lambda a: pl.BlockSpec(a.shape, lambda i: (0,) * a.ndim)
    in_specs = [pl.BlockSpec((TOP_K, td * ROW_WORDS, LANES), lambda i: (0, i + tile0, 0)),
                pl.BlockSpec((td, d), lambda i: (i, 0)),
                pl.BlockSpec((td * ROW_WORDS, LANES), lambda i: (i, 0)),
                gtf_spec,
                pl.BlockSpec((SUBLANES, td), lambda i: (0, i + tile0)),
                pl.BlockSpec((1, d), lambda i: (0, 0))] + [full(w) for w in shared_w]
    args = [z, x1, hp, mod, wts, g_final, *shared_w]
    body, aliases = _combine_body, {}
    if into is not None:
        in_specs.append(pl.BlockSpec(memory_space=pl.ANY))
        args.append(into)
        body, aliases = _combine_body_into, {len(args) - 1: 0}
    return pl.pallas_call(
        functools.partial(body, td=td, topk=TOP_K),
        grid=(t // td,),
        in_specs=in_specs,
        out_specs=pl.BlockSpec((td, d), lambda i: (i + out0, 0)),
        scratch_shapes=[pltpu.VMEM((td, d), F32), pltpu.VMEM((td, d // 2), BF16), pltpu.VMEM((td, d // 2), BF16)],
        out_shape=jax.ShapeDtypeStruct((out_rows, d), F32),
        input_output_aliases=aliases,
        compiler_params=pltpu.CompilerParams(vmem_limit_bytes=VMEM_LIMIT),
        name="combine",
    )(*args)


def _log_gamma(nh):
    return np.log(1.0 - 2.0 ** (-5.0 - np.arange(nh, dtype=np.float32))).astype(np.float32)


def _retention_tables(length, nh, hd):
    c = math.gcd(length, RET_CHUNK)
    log_g = _log_gamma(nh)
    idx = np.arange(c, dtype=np.float32)
    rel = idx[:, None] - idx[None, :]
    mask = np.where(rel >= 0, np.exp(log_g[:, None, None] * np.maximum(rel, 0.0)), 0.0).astype(np.float32)
    q_decay = np.exp(log_g[None, :] * (idx[:, None] + 1.0)).astype(np.float32)
    k_decay = np.exp(log_g[None, :] * (c - 1.0 - idx[:, None])).astype(np.float32)
    chunk_decay = np.exp(log_g * np.float32(c)).astype(np.float32)
    qd = np.broadcast_to(q_decay.T[:, :, None], (nh, c, hd))
    kd = np.broadcast_to(k_decay.T[:, :, None], (nh, c, hd))
    cd = np.broadcast_to(chunk_decay[:, None, None], (nh, hd, hd))
    return tuple(jnp.asarray(t) for t in (mask, qd, kd, cd))


def kernel(x_prompt, x_sample, c_prompt, c_sample, state_conv, state_ret, w_ada, b_ada, g_mix, g_ffn, w_in,
           conv_w, conv_b, conv_norm_g, conv_norm_b, ret_norm_g, ret_norm_b, w_out, w_router, router_bias,
           w1, w3, w2, ws1, ws3, ws2, g_final):
    depth = w_ada.shape[0]
    assert depth == 1, "single-layer trunk"
    bp, lp, d = x_prompt.shape
    bs, ls, _ = x_sample.shape
    assert ls == 1
    dc = conv_w.shape[2]
    dr = ret_norm_g.shape[1]
    nh = RET_HEADS
    hd = dr // nh
    assert hd == LANES and lp % 256 == 0 and bs % TOKEN_TILE == 0 and d // 2 == ROW_WORDS * LANES
    ne = w_router.shape[2]
    row = lambda a: a.reshape(1, -1)

    mod = _ada(jnp.concatenate([c_prompt, c_sample], axis=0), w_ada[0], row(b_ada[0]))
    mod_p = mod[:bp].reshape(bp, 6, 1, d)
    mod_s = mod[bp:]

    half = hd // 2
    inv = (np.float32(ROPE_BASE) ** (-np.arange(half, dtype=np.float32) / np.float32(half))).astype(np.float32)
    inv2 = jnp.asarray(np.concatenate([inv, inv]).reshape(1, hd))
    cos_p, sin_p = _rope_tables(inv2, lp, 0)
    cos_s, sin_s = _rope_tables(inv2, SUBLANES, PAST_LEN)

    w_in_b = w_in[0].astype(BF16)
    wo_b = w_out[0].astype(BF16)
    wr_t = w_router[0].T
    wrh = wr_t.astype(BF16)
    wrl = (wr_t - wrh.astype(F32)).astype(BF16)
    ws1_b, ws3_b, ws2_b = ws1[0].astype(BF16), ws3[0].astype(BF16), ws2[0].astype(BF16)
    dims = dict(dc=dc, dr=dr, hd=hd)

    tables = _retention_tables(lp, nh, hd)
    gam = jnp.asarray(np.broadcast_to(np.exp(_log_gamma(nh))[:, None, None], (nh, SUBLANES, hd)))
    norm_rows = (row(conv_b[0]), row(conv_norm_g[0]), row(conv_norm_b[0]), row(ret_norm_g[0]), row(ret_norm_b[0]))
    post_w = (row(g_ffn[0]), wo_b, wrh, wrl)
    shared_w = (ws1_b, ws3_b, ws2_b)
    bias_col = router_bias[0].reshape(ne, 1)
    hw = d // 2
    tm = EXPERT_ROWS

    def pre_prompt(b0, nb, after=None):
        glu, q, k, v, sg = _proj(x_prompt, mod_p, row(g_mix[0]), w_in_b, cos_p, sin_p,
                                 per_row_mod=False, b0=b0, nb=nb, after=after, **dims)
        cat, ret = _mix(glu, q, k, v, sg, conv_w[0], *norm_rows, tables, nb=nb, length=lp, **dims)
        x2, hp, lg = _post(cat, x_prompt, mod_p, *post_w, per_row_mod=False, b0=b0, nb=nb)
        return glu, ret, x2, hp, lg

    def pre_sample():
        xs3 = x_sample.reshape(1, bs, d)
        glu, q, k, v, sg = _proj(xs3, mod_s, row(g_mix[0]), w_in_b, cos_s, sin_s, per_row_mod=True, **dims)
        cat, ret, new_conv = _mix1(glu, q, k, v, sg, state_conv, state_ret, conv_w[0], *norm_rows, gam,
                                   **dims)
        x2, hp, lg = _post(cat, xs3, mod_s, *post_w, per_row_mod=True)
        return new_conv, ret, x2, hp, lg

    def route_and_dispatch(hps, lgs):
        lg = lgs[0] if len(lgs) == 1 else jnp.concatenate(lgs, axis=1)
        tokens = lg.shape[1]
        eidx, wts, rank, cnt = _route(lg, bias_col)
        counts = cnt[:, 0].astype(I32)
        n_rows = -(-(tokens * TOP_K + ne * (tm - 1)) // tm) * tm
        dest3 = _dest_rows(counts, eidx, rank).reshape(tokens // TOKEN_TILE, SUBLANES, TOKEN_TILE)
        xs = _dispatch([h.reshape(-1, hw) for h in hps], dest3, n_rows)
        return counts, dest3, wts, xs.reshape(n_rows * ROW_WORDS, LANES)

    def undispatch(ys, dest3, after=None):
        tokens = dest3.shape[0] * TOKEN_TILE
        z = _undispatch(ys.reshape(-1, hw), dest3, tokens, after)
        return z.reshape(TOP_K, tokens * ROW_WORDS, LANES)

    nb0 = bp // 2
    nb1 = bp - nb0
    new_conv_s, ret_s, x2_s, hp_s, lg_s = pre_sample()
    glu_0, ret_0, x2_0, hp_0, lg_0 = pre_prompt(0, nb0, after=lg_s)
    counts_0, dest_0, wts_0, xs_0 = route_and_dispatch([hp_0], [lg_0])
    glu_1, ret_1, x2_1, hp_1, lg_1 = pre_prompt(nb0, nb1, after=dest_0)
    counts_1, dest_1, wts_1, xs_1 = route_and_dispatch([hp_1, hp_s], [lg_1, lg_s])
    ys_0, ys_1 = _experts((counts_0, counts_1), (xs_0, xs_1), w1[0], w3[0], w2[0])
    z_0 = undispatch(ys_0, dest_0)
    z_1 = undispatch(ys_1, dest_1, after=z_0[0, :SUBLANES])
    y_p = _combine(z_0, 0, wts_0, x2_0, hp_0, mod_p, row(g_final), shared_w, rows_per_mod=lp, per_row_mod=False,
                   out_rows=bp * lp)
    y_p = _combine(z_1, 0, wts_1, x2_1, hp_1, mod_p, row(g_final), shared_w, rows_per_mod=lp, per_row_mod=False,
                   b0=nb0, out_rows=bp * lp, into=y_p)
    y_s = _combine(z_1, nb1 * lp, wts_1, x2_s, hp_s, mod_s, row(g_final), shared_w, rows_per_mod=bs,
                   per_row_mod=True)
    ret_p = jnp.concatenate([ret_0, ret_1], axis=0)

    tail = lambda g, n: g.reshape(n, lp, dc)[:, lp - CONV_BUF:, :]
    new_conv_p = jnp.concatenate([tail(glu_0, nb0), tail(glu_1, nb1)], axis=0)
    return (y_p.reshape(bp, lp, d), y_s.reshape(bs, ls, d), new_conv_p[None], ret_p[None],
            new_conv_s, ret_s)
```

```python
import functools
import math

import jax
import jax.numpy as jnp
import numpy as np
from jax import lax
from jax.experimental import pallas as pl
from jax.experimental.pallas import tpu as pltpu
from jax.experimental.pallas import tpu_sc as plsc

F32 = jnp.float32
BF16 = jnp.bfloat16
U32 = jnp.uint32
I32 = jnp.int32

EPS = 1e-6
PAST_LEN = 16384
RET_HEADS = 4
RET_CHUNK = 128
CONV_WIDTH = 31
CONV_BUF = CONV_WIDTH - 1
ROPE_BASE = 10000.0
TOP_K = 6
N_GROUPS = 8
TOPK_GROUPS = 4
ROUTED_SCALE = 2.5

LANES = 128
SUBLANES = 8
CONV_PAD = 32
EXPERT_ROWS = 256
ROUTE_TILE = 1024
DEST_TILE = 4096
TOKEN_TILE = 128
COMBINE_TILE = 512
VMEM_LIMIT = 56 * 1024 * 1024
SC_CORES = 2
SC_SUBCORES = 16
SC_WORKERS = SC_CORES * SC_SUBCORES
SCATTER_ROWS = 64
GATHER_ROWS = 16
ROW_WORDS = 4
X_SLOTS = 6
X_AHEAD = 4
Y_SLOTS = 4

HI_MASK = 0xFFFF0000


def _sigmoid(x):
    return jax.nn.sigmoid(x)


def _silu(x):
    return x * jax.nn.sigmoid(x)


def _pack_halves(lo, hi):
    lo_u = lax.bitcast_convert_type(lo.astype(BF16).astype(F32), U32) >> 16
    hi_u = lax.bitcast_convert_type(hi.astype(BF16).astype(F32), U32) & jnp.uint32(HI_MASK)
    return hi_u | lo_u


def _unpack_halves(p):
    lo = lax.bitcast_convert_type(p << 16, F32)
    hi = lax.bitcast_convert_type(p & jnp.uint32(HI_MASK), F32)
    return lo, hi


def _store_rows(ref, x):
    rows = x.shape[0]
    for j in range(ROW_WORDS):
        ref[pl.ds(j, rows, stride=ROW_WORDS), :] = x[:, LANES * j:LANES * (j + 1)]


def _load_row_word(ref, j, rows):
    return ref[pl.ds(j, rows, stride=ROW_WORDS), :]


def _ada_body(c_ref, w_ref, b_ref, o_ref):
    s = _silu(c_ref[...]).astype(BF16)
    o_ref[...] = jnp.dot(s, w_ref[...].astype(BF16), preferred_element_type=F32) + b_ref[...]


def _ada(c_all, w_ada, b_ada):
    rows, d = c_all.shape
    n = w_ada.shape[1]
    tn = 2048
    return pl.pallas_call(
        _ada_body,
        grid=(n // tn,),
        in_specs=[
            pl.BlockSpec((rows, d), lambda j: (0, 0)),
            pl.BlockSpec((d, tn), lambda j: (0, j)),
            pl.BlockSpec((1, tn), lambda j: (0, j)),
        ],
        out_specs=pl.BlockSpec((rows, tn), lambda j: (0, j)),
        out_shape=jax.ShapeDtypeStruct((rows, n), F32),
        compiler_params=pltpu.CompilerParams(vmem_limit_bytes=VMEM_LIMIT),
        name="ada",
    )(c_all, w_ada, b_ada)


def _rope_body(inv_ref, cos_ref, sin_ref, *, pos0, tl, half):
    row = lax.broadcasted_iota(I32, (tl, LANES), 0) + pl.program_id(0) * tl
    ang = (row.astype(F32) + pos0) * inv_ref[...]
    lane = lax.broadcasted_iota(I32, (tl, LANES), 1)
    s = jnp.sin(ang)
    cos_ref[...] = jnp.cos(ang)
    sin_ref[...] = jnp.where(lane < half, -s, s)


def _rope_tables(inv2, rows, pos0):
    tl = min(rows, 256)
    return pl.pallas_call(
        functools.partial(_rope_body, pos0=float(pos0), tl=tl, half=LANES // 2),
        grid=(rows // tl,),
        in_specs=[pl.BlockSpec((1, LANES), lambda i: (0, 0))],
        out_specs=[pl.BlockSpec((tl, LANES), lambda i: (i, 0))] * 2,
        out_shape=[jax.ShapeDtypeStruct((rows, LANES), F32)] * 2,
        name="rope",
    )(inv2)


def _modulated_rmsnorm(x, g, sc, sh):
    ms = jnp.mean(x * x, axis=-1, keepdims=True)
    h = x * lax.rsqrt(ms + EPS) * g
    return h * (1.0 + sc) + sh


def _proj_body(x_ref, sh_ref, sc_ref, g_ref, w_ref, cos_ref, sin_ref, *rest, dc, dr, hd, rope_rows):
    glu_ref, q_ref, k_ref, v_ref, sg_ref = rest[-5:]
    hb = _modulated_rmsnorm(x_ref[...], g_ref[...], sc_ref[...], sh_ref[...]).astype(BF16)

    def proj(lo, n):
        return jnp.dot(hb, w_ref[:, lo:lo + n], preferred_element_type=F32)

    glu_ref[...] = proj(0, dc) * _sigmoid(proj(dc, dc))
    cos = cos_ref[...] if rope_rows else cos_ref[0:1, :]
    sin = sin_ref[...] if rope_rows else sin_ref[0:1, :]
    for ref, lo, scale in ((q_ref, 2 * dc, hd ** -0.5), (k_ref, 2 * dc + dr, None)):
        t = proj(lo, dr)
        for hh in range(dr // hd):
            th = t[:, hh * hd:(hh + 1) * hd]
            r = th * cos + pltpu.roll(th, hd // 2, 1) * sin
            if scale is not None:
                r = r * scale
            ref[:, hh * hd:(hh + 1) * hd] = r.astype(BF16)
    v_ref[...] = proj(2 * dc + 2 * dr, dr).astype(BF16)
    sg_ref[...] = _silu(proj(2 * dc + 3 * dr, dr)).astype(BF16)


def _proj(x, mod, g_mix, w_in_b, cos2, sin2, *, dc, dr, hd, per_row_mod, b0=0, nb=None, after=None):
    nb_all, length, d = x.shape
    nb = nb_all if nb is None else nb
    tl = min(length, 512)
    x2 = x.reshape(nb_all * length, d)
    nl = length // tl
    if per_row_mod:
        mod_spec = lambda j: pl.BlockSpec((tl, d), lambda b, l: (l, j))
        rope_spec = pl.BlockSpec((SUBLANES, LANES), lambda b, l: (0, 0))
    else:
        mod_spec = lambda j: pl.BlockSpec((None, None, 1, d), lambda b, l: (b + b0, j, 0, 0))
        rope_spec = pl.BlockSpec((tl, LANES), lambda b, l: (l, 0))
    row_spec = lambda w: pl.BlockSpec((tl, w), lambda b, l: (b * nl + l, 0))
    t = nb * length
    outs = pl.pallas_call(
        functools.partial(_proj_body, dc=dc, dr=dr, hd=hd, rope_rows=not per_row_mod),
        grid=(nb, nl),
        in_specs=[
            pl.BlockSpec((tl, d), lambda b, l: ((b + b0) * nl + l, 0)), mod_spec(0), mod_spec(1),
            pl.BlockSpec((1, d), lambda b, l: (0, 0)),
            pl.BlockSpec(w_in_b.shape, lambda b, l: (0, 0)),
            rope_spec, rope_spec,
        ] + ([] if after is None else [pl.BlockSpec(memory_space=pl.ANY)]),
        out_specs=[row_spec(dc), row_spec(dr), row_spec(dr), row_spec(dr), row_spec(dr)],
        out_shape=[
            jax.ShapeDtypeStruct((t, dc), F32),
            jax.ShapeDtypeStruct((t, dr), BF16),
            jax.ShapeDtypeStruct((t, dr), BF16),
            jax.ShapeDtypeStruct((t, dr), BF16),
            jax.ShapeDtypeStruct((t, dr), BF16),
        ],
        compiler_params=pltpu.CompilerParams(vmem_limit_bytes=VMEM_LIMIT),
        name="proj",
    )(x2, mod, mod, g_mix, w_in_b, cos2, sin2, *(() if after is None else (after,)))
    return outs


def _layernorm_silu(c, g, b):
    mu = jnp.mean(c, axis=-1, keepdims=True)
    d = c - mu
    var = jnp.mean(d * d, axis=-1, keepdims=True)
    return _silu(d * lax.rsqrt(var + EPS) * g + b)


def _groupnorm(o, g, b):
    mu = jnp.mean(o, axis=-1, keepdims=True)
    d = o - mu
    var = jnp.mean(d * d, axis=-1, keepdims=True)
    return d * lax.rsqrt(var + EPS) * g + b


def _mix_body(glu_ref, q_ref, k_ref, v_ref, sg_ref, cw_ref, cb_ref, lng_ref, lnb_ref, rg_ref, rb_ref,
              mask_ref, qd_ref, kd_ref, cd_ref, cat_ref, st_ref, buf, cscr, *, tl, dc, hd, nh, chunk):
    nslab = dc // LANES

    @pl.when(pl.program_id(1) == 0)
    def _():
        buf[:, 0:CONV_PAD, :] = jnp.zeros((nslab, CONV_PAD, LANES), F32)
        st_ref[...] = jnp.zeros(st_ref.shape, F32)

    for j in range(nslab):
        buf[j, CONV_PAD:CONV_PAD + tl, :] = glu_ref[:, LANES * j:LANES * (j + 1)]
    first = CONV_PAD - CONV_BUF
    rows_per_iter = 8 * SUBLANES
    for j in range(nslab):
        cols = slice(LANES * j, LANES * (j + 1))
        wv = [jnp.broadcast_to(cw_ref[t:t + 1, cols], (SUBLANES, LANES)) for t in range(CONV_WIDTH)]
        bias = jnp.broadcast_to(cb_ref[0:1, cols], (SUBLANES, LANES))

        def body(r, carry, j=j, cols=cols, wv=wv, bias=bias):
            base = pl.multiple_of(r * rows_per_iter, rows_per_iter)
            for u in range(rows_per_iter // SUBLANES):
                acc = bias
                for t in range(CONV_WIDTH):
                    acc = acc + wv[t] * buf[j, pl.ds(base + (u * SUBLANES + first + t), SUBLANES), :]
                cscr[pl.ds(base + u * SUBLANES, SUBLANES), cols] = acc
            return carry

        lax.fori_loop(0, tl // rows_per_iter, body, 0)
    for j in range(nslab):
        buf[j, 0:CONV_PAD, :] = buf[j, tl:tl + CONV_PAD, :]
    cat_ref[:, 0:dc] = _layernorm_silu(cscr[...], lng_ref[...], lnb_ref[...]).astype(BF16)

    nt = (((1,), (1,)), ((), ()))
    tn = (((0,), (0,)), ((), ()))
    for c in range(tl // chunk):
        rows = slice(c * chunk, (c + 1) * chunk)
        for hh in range(nh):
            cols = slice(hh * hd, (hh + 1) * hd)
            qh = q_ref[rows, cols]
            kh = k_ref[rows, cols]
            vh = v_ref[rows, cols]
            s = st_ref[0, hh]
            scores = lax.dot_general(qh, kh, nt, preferred_element_type=F32) * mask_ref[hh]
            inner = jnp.dot(scores.astype(BF16), vh, preferred_element_type=F32)
            qd = (qh.astype(F32) * qd_ref[hh]).astype(BF16)
            cross = jnp.dot(qd, s.astype(BF16), preferred_element_type=F32)
            kd = (kh.astype(F32) * kd_ref[hh]).astype(BF16)
            st_ref[0, hh] = cd_ref[hh] * s + lax.dot_general(kd, vh, tn, preferred_element_type=F32)
            o = _groupnorm(inner + cross, rg_ref[0:1, cols], rb_ref[0:1, cols])
            cat_ref[rows, dc + hh * hd:dc + (hh + 1) * hd] = (o * sg_ref[rows, cols].astype(F32)).astype(BF16)


def _mix(glu, q, k, v, sg, conv_w, conv_b, ln_g, ln_b, rg, rb, tables, *, nb, length, dc, dr, hd):
    nh = dr // hd
    chunk = math.gcd(length, RET_CHUNK)
    tl = min(length, 256)
    nl = length // tl
    mask, qd, kd, cd = tables
    row_spec = lambda w: pl.BlockSpec((tl, w), lambda b, l: (b * nl + l, 0))
    full = lambda a: pl.BlockSpec(a.shape, lambda b, l: (0,) * a.ndim)
    cat, st = pl.pallas_call(
        functools.partial(_mix_body, tl=tl, dc=dc, hd=hd, nh=nh, chunk=chunk),
        grid=(nb, nl),
        in_specs=[row_spec(dc), row_spec(dr), row_spec(dr), row_spec(dr), row_spec(dr),
                  full(conv_w), full(conv_b), full(ln_g), full(ln_b), full(rg), full(rb),
                  full(mask), full(qd), full(kd), full(cd)],
        out_specs=[row_spec(dc + dr), pl.BlockSpec((1, nh, hd, hd), lambda b, l: (b, 0, 0, 0))],
        out_shape=[jax.ShapeDtypeStruct((nb * length, dc + dr), BF16),
                   jax.ShapeDtypeStruct((nb, nh, hd, hd), F32)],
        scratch_shapes=[pltpu.VMEM((dc // LANES, tl + CONV_PAD, LANES), F32),
                        pltpu.VMEM((tl, dc), F32)],
        compiler_params=pltpu.CompilerParams(dimension_semantics=("arbitrary", "arbitrary")),
        name="mix",
    )(glu, q, k, v, sg, conv_w, conv_b, ln_g, ln_b, rg, rb, mask, qd, kd, cd)
    return cat, st


def _mix1_body(glu_ref, q_ref, k_ref, v_ref, sg_ref, sc_ref, s0_ref, cw_ref, cb_ref, lng_ref, lnb_ref,
               rg_ref, rb_ref, gam_ref, cat_ref, st_ref, nc_ref, cscr, oscr, qf, kf, vf, *, tb, dc, hd, nh):
    w_hist = cw_ref[0:CONV_BUF, :]
    w_last = cw_ref[CONV_BUF:CONV_WIDTH, :]
    for bb in range(tb):
        hist = jnp.sum(sc_ref[bb] * w_hist, axis=0, keepdims=True)
        cscr[bb:bb + 1, :] = hist + glu_ref[bb:bb + 1, :] * w_last + cb_ref[...]
        nc_ref[bb, 0:CONV_BUF - 1, :] = sc_ref[bb, 1:CONV_BUF, :]
        nc_ref[bb, CONV_BUF - 1:CONV_BUF, :] = glu_ref[bb:bb + 1, :]
    cat_ref[:, 0:dc] = _layernorm_silu(cscr[...], lng_ref[...], lnb_ref[...]).astype(BF16)

    tn = (((0,), (0,)), ((), ()))
    rowid = lax.broadcasted_iota(I32, (tb, hd), 0)
    qf[...] = q_ref[...].astype(F32)
    kf[...] = k_ref[...].astype(F32)
    vf[...] = v_ref[...].astype(F32)
    for hh in range(nh):
        cols = slice(hh * hd, (hh + 1) * hd)
        qa = q_ref[:, cols]
        ka = k_ref[:, cols]
        gam = gam_ref[hh, 0:1, :]
        for bb in range(tb):
            onehot = (rowid == bb).astype(BF16)
            qcol = lax.dot_general(qa, onehot, tn, preferred_element_type=F32)
            kcol = lax.dot_general(ka, onehot, tn, preferred_element_type=F32)
            s0 = s0_ref[bb, hh]
            qrow = qf[bb:bb + 1, cols]
            krow = kf[bb:bb + 1, cols]
            vrow = vf[bb:bb + 1, cols]
            qk = jnp.sum(qrow * krow, axis=-1, keepdims=True)
            cross = gam * jnp.sum(qcol * s0, axis=0, keepdims=True)
            st_ref[bb, hh] = gam * s0 + kcol * vrow
            oscr[bb:bb + 1, cols] = qk * vrow + cross
    for hh in range(nh):
        cols = slice(hh * hd, (hh + 1) * hd)
        o = _groupnorm(oscr[:, cols], rg_ref[0:1, cols], rb_ref[0:1, cols])
        cat_ref[:, dc + hh * hd:dc + (hh + 1) * hd] = (o * sg_ref[:, cols].astype(F32)).astype(BF16)


def _mix1(glu, q, k, v, sg, state_conv, state_ret, conv_w, conv_b, ln_g, ln_b, rg, rb, gam, *, dc, dr, hd):
    nb = glu.shape[0]
    nh = dr // hd
    tb = 16
    row_spec = lambda w: pl.BlockSpec((tb, w), lambda i: (i, 0))
    full = lambda a: pl.BlockSpec(a.shape, lambda i: (0,) * a.ndim)
    st_spec = pl.BlockSpec((None, tb, nh, hd, hd), lambda i: (0, i, 0, 0, 0))
    conv_spec = pl.BlockSpec((None, tb, CONV_BUF, dc), lambda i: (0, i, 0, 0))
    cat, st, new_conv = pl.pallas_call(
        functools.partial(_mix1_body, tb=tb, dc=dc, hd=hd, nh=nh),
        grid=(nb // tb,),
        in_specs=[row_spec(dc), row_spec(dr), row_spec(dr), row_spec(dr), row_spec(dr), conv_spec, st_spec,
                  full(conv_w), full(conv_b), full(ln_g), full(ln_b), full(rg), full(rb), full(gam)],
        out_specs=[row_spec(dc + dr), st_spec, conv_spec],
        out_shape=[jax.ShapeDtypeStruct((nb, dc + dr), BF16),
                   jax.ShapeDtypeStruct((1, nb, nh, hd, hd), F32),
                   jax.ShapeDtypeStruct((1, nb, CONV_BUF, dc), F32)],
        scratch_shapes=[pltpu.VMEM((tb, dc), F32)] + [pltpu.VMEM((tb, dr), F32)] * 4,
        compiler_params=pltpu.CompilerParams(vmem_limit_bytes=VMEM_LIMIT),
        name="mix1",
    )(glu, q, k, v, sg, state_conv, state_ret, conv_w, conv_b, ln_g, ln_b, rg, rb, gam)
    return cat, st, new_conv


def _post_body(cat_ref, x_ref, gtm_ref, scf_ref, shf_ref, g_ref, wo_ref, wrh_ref, wrl_ref, x1_ref, hp_ref, lg_ref):
    d = x_ref.shape[1]
    y = jnp.dot(cat_ref[...], wo_ref[...], preferred_element_type=F32)
    x1 = x_ref[...] + gtm_ref[...] * y
    x1_ref[...] = x1
    h = _modulated_rmsnorm(x1, g_ref[...], scf_ref[...], shf_ref[...])
    hb = h.astype(BF16)
    _store_rows(hp_ref, _pack_halves(h[:, 0:d // 2], h[:, d // 2:d]))
    hl = (h - hb.astype(F32)).astype(BF16)
    nt = (((1,), (1,)), ((), ()))
    lg_ref[...] = (lax.dot_general(wrh_ref[...], hb, nt, preferred_element_type=F32)
                   + lax.dot_general(wrh_ref[...], hl, nt, preferred_element_type=F32)
                   + lax.dot_general(wrl_ref[...], hb, nt, preferred_element_type=F32))


def _post(cat, x, mod, g_ffn, wo_b, wrh, wrl, *, per_row_mod, b0=0, nb=None):
    nb_all, length, d = x.shape
    nb = nb_all if nb is None else nb
    tl = min(length, 1024)
    nl = length // tl
    t = nb * length
    ne = wrh.shape[0]
    x2d = x.reshape(nb_all * length, d)
    if per_row_mod:
        mod_spec = lambda j: pl.BlockSpec((tl, d), lambda b, l: (l, j))
    else:
        mod_spec = lambda j: pl.BlockSpec((None, None, 1, d), lambda b, l: (b + b0, j, 0, 0))
    row_spec = lambda w: pl.BlockSpec((tl, w), lambda b, l: (b * nl + l, 0))
    full = lambda a: pl.BlockSpec(a.shape, lambda b, l: (0,) * a.ndim)
    return pl.pallas_call(
        _post_body,
        grid=(nb, nl),
        in_specs=[row_spec(d), pl.BlockSpec((tl, d), lambda b, l: ((b + b0) * nl + l, 0)),
                  mod_spec(2), mod_spec(4), mod_spec(3),
                  full(g_ffn), full(wo_b), full(wrh), full(wrl)],
        out_specs=[row_spec(d),
                   pl.BlockSpec((tl * ROW_WORDS, LANES), lambda b, l: (b * nl + l, 0)),
                   pl.BlockSpec((ne, tl), lambda b, l: (0, b * nl + l))],
        out_shape=[jax.ShapeDtypeStruct((t, d), F32),
                   jax.ShapeDtypeStruct((t * ROW_WORDS, LANES), U32),
                   jax.ShapeDtypeStruct((ne, t), F32)],
        compiler_params=pltpu.CompilerParams(vmem_limit_bytes=VMEM_LIMIT),
        name="post",
    )(cat, x2d, mod, mod, mod, g_ffn, wo_b, wrh, wrl)


def _first_max(x, idx, sentinel):
    m = jnp.max(x, axis=0, keepdims=True)
    f = jnp.min(jnp.where(x == m, idx, sentinel), axis=0, keepdims=True)
    return m, f


def _route_body(lg_ref, bias_ref, e_ref, w_ref, r_ref, cnt_ref, cnt_scr, before, *, tr, ne, ng, topk, topg):
    @pl.when(pl.program_id(0) == 0)
    def _():
        cnt_scr[...] = jnp.zeros(cnt_scr.shape, F32)
        t_row = lax.broadcasted_iota(I32, (tr, tr), 0)
        t_col = lax.broadcasted_iota(I32, (tr, tr), 1)
        before[...] = (t_row < t_col).astype(BF16)

    per = ne // ng
    neg = -jnp.inf
    scores = _sigmoid(lg_ref[...])
    sel = scores + bias_ref[...]
    sub = lax.broadcasted_iota(I32, (per, tr), 0)
    gs = []
    for g in range(ng):
        s_g = sel[g * per:(g + 1) * per, :]
        m1, f1 = _first_max(s_g, sub, per)
        m2 = jnp.max(jnp.where(sub == f1, neg, s_g), axis=0, keepdims=True)
        gs.append(m1 + m2)
    gsc = jnp.concatenate(gs, axis=0)
    gi = lax.broadcasted_iota(I32, (ng, tr), 0)
    keep = jnp.zeros((ng, tr), F32)
    for _ in range(topg):
        _, f = _first_max(gsc, gi, ng)
        pick = gi == f
        keep = jnp.where(pick, 1.0, keep)
        gsc = jnp.where(pick, neg, gsc)
    work = jnp.concatenate(
        [jnp.where(keep[g:g + 1, :] > 0.5, sel[g * per:(g + 1) * per, :], neg) for g in range(ng)], axis=0)
    ei = lax.broadcasted_iota(I32, (ne, tr), 0)
    picks, es, ws = [], [], []
    for _ in range(topk):
        _, f = _first_max(work, ei, ne)
        pick = ei == f
        picks.append(pick)
        es.append(f)
        ws.append(jnp.sum(jnp.where(pick, scores, 0.0), axis=0, keepdims=True))
        work = jnp.where(pick, neg, work)
    wsum = ws[0]
    for w in ws[1:]:
        wsum = wsum + w
    scale = ROUTED_SCALE / wsum
    chosen = picks[0]
    for p in picks[1:]:
        chosen = jnp.logical_or(chosen, p)
    chosen_f = chosen.astype(F32)
    prior = cnt_scr[:, 0:1] + jnp.dot(chosen_f.astype(BF16), before[...], preferred_element_type=F32)
    rs = [jnp.sum(jnp.where(p, prior, 0.0), axis=0, keepdims=True).astype(I32) for p in picks]
    pad_i = jnp.zeros((SUBLANES - topk, tr), I32)
    pad_f = jnp.zeros((SUBLANES - topk, tr), F32)
    e_ref[...] = jnp.concatenate(es + [pad_i], axis=0)
    w_ref[...] = jnp.concatenate([w * scale for w in ws] + [pad_f], axis=0)
    r_ref[...] = jnp.concatenate(rs + [pad_i], axis=0)
    total = cnt_scr[:, 0:1] + jnp.sum(chosen_f, axis=1, keepdims=True)
    cnt_scr[...] = jnp.broadcast_to(total, cnt_scr.shape)
    cnt_ref[...] = jnp.broadcast_to(total, cnt_ref.shape)


def _token_tile(t, limit):
    return max(m for m in range(LANES, limit + 1, LANES) if t % m == 0)


def _route(logits_t, bias_col):
    ne, t = logits_t.shape
    tr = _token_tile(t, ROUTE_TILE)
    tok = lambda dt: jax.ShapeDtypeStruct((SUBLANES, t), dt)
    tok_spec = pl.BlockSpec((SUBLANES, tr), lambda i: (0, i))
    return pl.pallas_call(
        functools.partial(_route_body, tr=tr, ne=ne, ng=N_GROUPS, topk=TOP_K, topg=TOPK_GROUPS),
        grid=(t // tr,),
        in_specs=[pl.BlockSpec((ne, tr), lambda i: (0, i)), pl.BlockSpec((ne, 1), lambda i: (0, 0))],
        out_specs=[tok_spec, tok_spec, tok_spec, pl.BlockSpec((ne, LANES), lambda i: (0, 0))],
        out_shape=[tok(I32), tok(F32), tok(I32), jax.ShapeDtypeStruct((ne, LANES), F32)],
        scratch_shapes=[pltpu.VMEM((ne, LANES), F32), pltpu.VMEM((tr, tr), BF16)],
        compiler_params=pltpu.CompilerParams(dimension_semantics=("arbitrary",)),
        name="route",
    )(logits_t, bias_col)


def _dest_body(cnt_ref, e_ref, r_ref, d_ref, ps, *, ne, tr):
    shift = EXPERT_ROWS.bit_length() - 1

    @pl.when(pl.program_id(0) == 0)
    def _():
        def step(j, start):
            ps[j] = start
            return start + lax.shift_left(lax.shift_right_logical(cnt_ref[j] + (EXPERT_ROWS - 1), shift), shift)

        lax.fori_loop(0, ne, step, jnp.int32(0))

    e = e_ref[...]
    base = jnp.zeros(e.shape, I32)
    for j in range(ne):
        base = jnp.where(e == j, ps[j], base)
    dest = base + r_ref[...]
    for m in range(tr // TOKEN_TILE):
        d_ref[SUBLANES * m:SUBLANES * (m + 1), :] = dest[:, TOKEN_TILE * m:TOKEN_TILE * (m + 1)]


def _dest_rows(counts, eidx, rank):
    rows, t = eidx.shape
    ne = counts.shape[0]
    tr = _token_tile(t, DEST_TILE)
    spec = pl.BlockSpec((rows, tr), lambda i, cnt: (0, i))
    return pl.pallas_call(
        functools.partial(_dest_body, ne=ne, tr=tr),
        grid_spec=pltpu.PrefetchScalarGridSpec(
            num_scalar_prefetch=1, grid=(t // tr,), in_specs=[spec, spec],
            out_specs=pl.BlockSpec((tr // TOKEN_TILE * rows, TOKEN_TILE), lambda i, cnt: (i, 0)),
            scratch_shapes=[pltpu.SMEM((ne,), I32)]),
        out_shape=jax.ShapeDtypeStruct((t // TOKEN_TILE * rows, TOKEN_TILE), I32),
        compiler_params=pltpu.CompilerParams(dimension_semantics=("arbitrary",)),
        name="dest",
    )(counts, eidx, rank)


def _sc_mesh():
    return plsc.VectorSubcoreMesh(core_axis_name="c", subcore_axis_name="s")


def _sc_worker_id():
    return lax.axis_index("s") * SC_CORES + lax.axis_index("c")


def _index_block(dest_ref, chunk, width):
    per_tile = TOKEN_TILE // width
    return dest_ref.at[chunk // per_tile, :, pl.ds((chunk % per_tile) * width, width)]


def _dispatch(hps, dest3, n_rows):
    w = SCATTER_ROWS
    width = hps[0].shape[1]
    bounds = [0]
    for h in hps:
        bounds.append(bounds[-1] + h.shape[0] // w)
    nch = bounds[-1]
    nsrc = len(hps)

    @functools.partial(
        pl.kernel, mesh=_sc_mesh(),
        out_type=jax.ShapeDtypeStruct((n_rows, width), U32),
        scratch_types=[pltpu.VMEM((2, SUBLANES, w), I32), pltpu.VMEM((2, w, width), U32),
                       pltpu.SemaphoreType.DMA, pltpu.SemaphoreType.DMA, pltpu.SemaphoreType.DMA],
        compiler_params=pltpu.CompilerParams(use_tc_tiling_on_sc=False),
        name="dispatch",
    )
    def run(*refs):
        src_refs, dest_ref, xs_ref = refs[:nsrc], refs[nsrc], refs[nsrc + 1]
        idx_v, rows_v = refs[nsrc + 2:nsrc + 4]
        sem_load = refs[nsrc + 4:nsrc + 6]
        sem_scatter = refs[nsrc + 6]
        wid = _sc_worker_id()

        def start_loads(c, slot):
            pltpu.async_copy(_index_block(dest_ref, c, w), idx_v.at[slot], sem_load[slot])
            for i, src in enumerate(src_refs):
                @pl.when(jnp.logical_and(c >= bounds[i], c < bounds[i + 1]))
                def _(src=src, lo=bounds[i]):
                    pltpu.async_copy(src.at[pl.ds((c - lo) * w, w)], rows_v.at[slot], sem_load[slot])

        def wait_loads(slot):
            pltpu.make_async_copy(_index_block(dest_ref, 0, w), idx_v.at[slot], sem_load[slot]).wait()
            pltpu.make_async_copy(src_refs[0].at[pl.ds(0, w)], rows_v.at[slot], sem_load[slot]).wait()

        @pl.when(wid < nch)
        def _():
            start_loads(wid, 0)

        @pl.loop(0, pl.cdiv(pl.cdiv(nch, SC_WORKERS), 2))
        def _(rr):
            for slot in range(2):
                c = (rr * 2 + slot) * SC_WORKERS + wid

                @pl.when(c < nch)
                def _(c=c, slot=slot):
                    wait_loads(slot)

                    @pl.when(c + SC_WORKERS < nch)
                    def _():
                        start_loads(c + SC_WORKERS, 1 - slot)

                    copies = [pltpu.async_copy(rows_v.at[slot], xs_ref.at[idx_v.at[slot, k]], sem_scatter)
                              for k in range(TOP_K)]
                    for cp in copies:
                        cp.wait()

    return run(*hps, dest3)


def _undispatch(ys, dest3, n_tokens, after=None):
    w = GATHER_ROWS
    width = ys.shape[1]
    nch = n_tokens // w

    @functools.partial(
        pl.kernel, mesh=_sc_mesh(),
        out_type=jax.ShapeDtypeStruct((TOP_K, n_tokens, width), U32),
        scratch_types=[pltpu.VMEM((2, SUBLANES, w), I32), pltpu.VMEM((2, TOP_K, w, width), U32),
                       pltpu.SemaphoreType.DMA, pltpu.SemaphoreType.DMA, pltpu.SemaphoreType.DMA],
        compiler_params=pltpu.CompilerParams(use_tc_tiling_on_sc=False),
        name="undispatch",
    )
    def run(*refs):
        ys_ref, dest_ref = refs[:2]
        z_ref, idx_v, bufs = refs[-6:-3]
        sem_gather = refs[-3:-1]
        sem_store = refs[-1]
        wid = _sc_worker_id()

        def start_gathers(c, slot):
            pltpu.sync_copy(_index_block(dest_ref, c, w), idx_v.at[slot])
            for k in range(TOP_K):
                pltpu.async_copy(ys_ref.at[idx_v.at[slot, k]], bufs.at[slot, k], sem_gather[slot])

        def wait_gathers(slot):
            for k in range(TOP_K):
                pltpu.make_async_copy(ys_ref.at[idx_v.at[slot, k]], bufs.at[slot, k], sem_gather[slot]).wait()

        @pl.when(wid < nch)
        def _():
            start_gathers(wid, 0)

        @pl.loop(0, pl.cdiv(pl.cdiv(nch, SC_WORKERS), 2))
        def _(rr):
            for slot in range(2):
                c = (rr * 2 + slot) * SC_WORKERS + wid

                @pl.when(c < nch)
                def _(c=c, slot=slot):
                    @pl.when(c + SC_WORKERS < nch)
                    def _():
                        start_gathers(c + SC_WORKERS, 1 - slot)

                    wait_gathers(slot)
                    stores = [pltpu.async_copy(bufs.at[slot, k], z_ref.at[k, pl.ds(c * w, w)], sem_store)
                              for k in range(TOP_K)]
                    for cp in stores:
                        cp.wait()

    return run(ys, dest3) if after is None else run(ys, dest3, after)


PART_SHIFT = 24


def _expert_body(*refs, tm, ne, nparts):
    cnt_refs = refs[:nparts]
    xs_refs = refs[nparts:2 * nparts]
    w1_ref, w3_ref, w2_ref = refs[2 * nparts:2 * nparts + 3]
    ys_refs = refs[2 * nparts + 3:3 * nparts + 3]
    (w1f, w3f, w2f, w1s, w3s, w2s, xbuf, ybuf, xlo, xhi, sched, sem_x, sem_y, sem_w) = refs[3 * nparts + 3:]
    blk_words = tm * ROW_WORDS
    half = ROW_WORDS * LANES
    shift = tm.bit_length() - 1

    def n_blocks_of(p, e):
        return lax.shift_right_logical(cnt_refs[p][e] + (tm - 1), shift)

    def n_all(e):
        n = n_blocks_of(0, e)
        for p in range(1, nparts):
            n = n + n_blocks_of(p, e)
        return n

    def next_nonempty(e):
        return lax.while_loop(
            lambda c: jnp.logical_and(c < ne, n_all(jnp.minimum(c, ne - 1)) == 0), lambda c: c + 1, e)

    def plan(e, carry):
        i, starts = carry[0], list(carry[1:])
        for p in range(nparts):
            n = n_blocks_of(p, e)

            def put(j, c, p=p, i=i, start=starts[p]):
                sched[i + j] = (start + j) + (p << PART_SHIFT)
                return c

            lax.fori_loop(0, n, put, 0)
            i = i + n
            starts[p] = starts[p] + n
        return (i, *starts)

    nu = lax.fori_loop(0, ne, plan, (jnp.int32(0),) * (nparts + 1))[0]

    def rows_of(code):
        blk = code & ((1 << PART_SHIFT) - 1)
        return pl.ds(pl.multiple_of(blk * blk_words, blk_words), blk_words)

    def x_start(i, slot):
        code = sched[i]
        for p in range(nparts):
            @pl.when(lax.shift_right_logical(code, PART_SHIFT) == p)
            def _(p=p):
                pltpu.make_async_copy(xs_refs[p].at[rows_of(code), :], xbuf.at[slot], sem_x.at[slot]).start()

    def x_wait(slot):
        pltpu.make_async_copy(xs_refs[0].at[pl.ds(0, blk_words), :], xbuf.at[slot], sem_x.at[slot]).wait()

    def y_start(i, slot):
        code = sched[i]
        for p in range(nparts):
            @pl.when(lax.shift_right_logical(code, PART_SHIFT) == p)
            def _(p=p):
                pltpu.make_async_copy(ybuf.at[slot], ys_refs[p].at[rows_of(code), :], sem_y.at[slot]).start()

    def y_wait(slot):
        pltpu.make_async_copy(ybuf.at[slot], ys_refs[0].at[pl.ds(0, blk_words), :], sem_y.at[slot]).wait()

    def w_copies(e, ws):
        return [pltpu.make_async_copy(src.at[e], dst.at[ws], sem_w.at[ws])
                for src, dst in ((w1_ref, w1f), (w3_ref, w3f), (w2_ref, w2f))]

    for q in range(X_AHEAD):
        @pl.when(q < nu)
        def _(q=q):
            x_start(q, q)

    e_first = next_nonempty(jnp.int32(0))

    @pl.when(e_first < ne)
    def _():
        for cp in w_copies(e_first, 0):
            cp.start()

    def blocks(i, nblk):
        for b in range(nblk):
            x_wait((i + b) % X_SLOTS)
        for b in range(nblk):
            nxt = i + b + X_AHEAD

            @pl.when(nxt < nu)
            def _(nxt=nxt):
                x_start(nxt, nxt % X_SLOTS)

        for b in range(nblk):
            @pl.when(i + b >= Y_SLOTS)
            def _(b=b):
                y_wait((i + b) % Y_SLOTS)

        for b in range(nblk):
            xin = xbuf.at[(i + b) % X_SLOTS]
            for w in range(ROW_WORDS):
                lo, hi = _unpack_halves(_load_row_word(xin, w, tm))
                xlo[b, :, LANES * w:LANES * (w + 1)] = lo.astype(BF16)
                xhi[b, :, LANES * w:LANES * (w + 1)] = hi.astype(BF16)

        def up(b, wsc):
            return (jnp.dot(xlo[b], wsc[0:half, :], preferred_element_type=F32)
                    + jnp.dot(xhi[b], wsc[half:2 * half, :], preferred_element_type=F32))

        for b in range(nblk):
            hid = (_silu(up(b, w1s)) * up(b, w3s)).astype(BF16)
            y = jnp.dot(hid, w2s[...], preferred_element_type=F32)
            _store_rows(ybuf.at[(i + b) % Y_SLOTS], _pack_halves(y[:, 0:half], y[:, half:2 * half]))
        for b in range(nblk):
            y_start(i + b, (i + b) % Y_SLOTS)

    def per_expert(e, carry):
        i0, ws = carry
        n = n_all(e)

        @pl.when(n > 0)
        def _():
            for cp in w_copies(e, ws):
                cp.wait()
            w1s[...] = w1f[ws].astype(BF16)
            w3s[...] = w3f[ws].astype(BF16)
            w2s[...] = w2f[ws].astype(BF16)
            e_next = next_nonempty(e + 1)

            @pl.when(e_next < ne)
            def _():
                for cp in w_copies(e_next, 1 - ws):
                    cp.start()

            def pair(j, c):
                blocks(i0 + 2 * j, 2)
                return c

            lax.fori_loop(0, lax.shift_right_logical(n, 1), pair, 0)

            @pl.when(n % 2 == 1)
            def _():
                blocks(i0 + n - 1, 1)

        return i0 + n, jnp.where(n > 0, 1 - ws, ws)

    lax.fori_loop(0, ne, per_expert, (jnp.int32(0), jnp.int32(0)))

    for q in range(Y_SLOTS):
        @pl.when(nu > q)
        def _(q=q):
            y_wait((nu - 1 - q) % Y_SLOTS)


def _experts(counts, xss, w1e, w3e, w2e):
    tm = EXPERT_ROWS
    ne, d, de = w1e.shape
    half = ROW_WORDS * LANES
    nparts = len(xss)
    cap = sum(x.shape[0] // (tm * ROW_WORDS) for x in xss)
    assert cap < (1 << PART_SHIFT)
    anyspec = pl.BlockSpec(memory_space=pl.ANY)
    blk_buf = lambda n: pltpu.VMEM((n, tm * ROW_WORDS, LANES), U32)
    return pl.pallas_call(
        functools.partial(_expert_body, tm=tm, ne=ne, nparts=nparts),
        grid_spec=pltpu.PrefetchScalarGridSpec(
            num_scalar_prefetch=nparts,
            grid=(1,),
            in_specs=[anyspec] * (nparts + 3),
            out_specs=[anyspec] * nparts,
            scratch_shapes=[pltpu.VMEM((2, d, de), F32), pltpu.VMEM((2, d, de), F32), pltpu.VMEM((2, de, d), F32),
                            pltpu.VMEM((d, de), BF16), pltpu.VMEM((d, de), BF16), pltpu.VMEM((de, d), BF16),
                            blk_buf(X_SLOTS), blk_buf(Y_SLOTS),
                            pltpu.VMEM((2, tm, half), BF16), pltpu.VMEM((2, tm, half), BF16),
                            pltpu.SMEM((cap,), I32),
                            pltpu.SemaphoreType.DMA((X_SLOTS,)), pltpu.SemaphoreType.DMA((Y_SLOTS,)),
                            pltpu.SemaphoreType.DMA((2,))]),
        out_shape=[jax.ShapeDtypeStruct(x.shape, U32) for x in xss],
        compiler_params=pltpu.CompilerParams(dimension_semantics=("arbitrary",), vmem_limit_bytes=VMEM_LIMIT),
        name="experts",
    )(*counts, *xss, w1e, w3e, w2e)


def _combine_body(z_ref, x1_ref, hp_ref, gtf_ref, wt_ref, gfin_ref, ws1_ref, ws3_ref, ws2_ref, y_ref, xo, hlo, hhi,
                  *, td, topk):
    half = ROW_WORDS * LANES
    for w in range(ROW_WORDS):
        lo, hi = _unpack_halves(_load_row_word(hp_ref, w, td))
        hlo[:, LANES * w:LANES * (w + 1)] = lo.astype(BF16)
        hhi[:, LANES * w:LANES * (w + 1)] = hi.astype(BF16)

    def up(w_ref):
        return (jnp.dot(hlo[...], w_ref[0:half, :], preferred_element_type=F32)
                + jnp.dot(hhi[...], w_ref[half:2 * half, :], preferred_element_type=F32))

    xo[...] = jnp.dot((_silu(up(ws1_ref)) * up(ws3_ref)).astype(BF16), ws2_ref[...], preferred_element_type=F32)
    wt = wt_ref[...].T
    ws = [wt[:, k:k + 1] for k in range(topk)]
    sq = jnp.zeros((td, 1), F32)
    for j in range(ROW_WORDS):
        acc_lo = jnp.zeros((td, LANES), F32)
        acc_hi = jnp.zeros((td, LANES), F32)
        for k in range(topk):
            lo, hi = _unpack_halves(_load_row_word(z_ref.at[k], j, td))
            acc_lo = acc_lo + ws[k] * lo
            acc_hi = acc_hi + ws[k] * hi
        for base, acc in ((0, acc_lo), (half, acc_hi)):
            cols = slice(base + LANES * j, base + LANES * (j + 1))
            x = x1_ref[:, cols] + gtf_ref[:, cols] * (acc + xo[:, cols])
            xo[:, cols] = x
            sq = sq + jnp.sum(x * x, axis=-1, keepdims=True)
    rs = lax.rsqrt(sq / (2 * half) + EPS)
    y_ref[...] = xo[...] * rs * gfin_ref[...]


def _combine_body_into(*refs, td, topk):
    _combine_body(*refs[:9], *refs[10:], td=td, topk=topk)


def _combine(z, token0, wts, x1, hp, mod, g_final, shared_w, *, rows_per_mod, per_row_mod, b0=0, out_rows=None,
             into=None):
    t, d = x1.shape
    td = min(t, COMBINE_TILE)
    tile0 = token0 // td
    out_rows = t if out_rows is None else out_rows
    if per_row_mod:
        gtf_spec = pl.BlockSpec((td, d), lambda i: (i, 5))
        out0 = 0
    else:
        tiles_per_mod = rows_per_mod // td
        gtf_spec = pl.BlockSpec((None, None, 1, d), lambda i: (i // tiles_per_mod + b0, 5, 0, 0))
        out0 = b0 * tiles_per_mod
    full = lambda a: pl.BlockSpec(a.shape, lambda i: (0,) * a.ndim)
    in_specs = [pl.BlockSpec((TOP_K, td * ROW_WORDS, LANES), lambda i: (0, i + tile0, 0)),
                pl.BlockSpec((td, d), lambda i: (i, 0)),
                pl.BlockSpec((td * ROW_WORDS, LANES), lambda i: (i, 0)),
                gtf_spec,
                pl.BlockSpec((SUBLANES, td), lambda i: (0, i + tile0)),
                pl.BlockSpec((1, d), lambda i: (0, 0))] + [full(w) for w in shared_w]
    args = [z, x1, hp, mod, wts, g_final, *shared_w]
    body, aliases = _combine_body, {}
    if into is not None:
        in_specs.append(pl.BlockSpec(memory_space=pl.ANY))
        args.append(into)
        body, aliases = _combine_body_into, {len(args) - 1: 0}
    return pl.pallas_call(
        functools.partial(body, td=td, topk=TOP_K),
        grid=(t // td,),
        in_specs=in_specs,
        out_specs=pl.BlockSpec((td, d), lambda i: (i + out0, 0)),
        scratch_shapes=[pltpu.VMEM((td, d), F32), pltpu.VMEM((td, d // 2), BF16), pltpu.VMEM((td, d // 2), BF16)],
        out_shape=jax.ShapeDtypeStruct((out_rows, d), F32),
        input_output_aliases=aliases,
        compiler_params=pltpu.CompilerParams(vmem_limit_bytes=VMEM_LIMIT),
        name="combine",
    )(*args)


def _log_gamma(nh):
    return np.log(1.0 - 2.0 ** (-5.0 - np.arange(nh, dtype=np.float32))).astype(np.float32)


def _retention_tables(length, nh, hd):
    c = math.gcd(length, RET_CHUNK)
    log_g = _log_gamma(nh)
    idx = np.arange(c, dtype=np.float32)
    rel = idx[:, None] - idx[None, :]
    mask = np.where(rel >= 0, np.exp(log_g[:, None, None] * np.maximum(rel, 0.0)), 0.0).astype(np.float32)
    q_decay = np.exp(log_g[None, :] * (idx[:, None] + 1.0)).astype(np.float32)
    k_decay = np.exp(log_g[None, :] * (c - 1.0 - idx[:, None])).astype(np.float32)
    chunk_decay = np.exp(log_g * np.float32(c)).astype(np.float32)
    qd = np.broadcast_to(q_decay.T[:, :, None], (nh, c, hd))
    kd = np.broadcast_to(k_decay.T[:, :, None], (nh, c, hd))
    cd = np.broadcast_to(chunk_decay[:, None, None], (nh, hd, hd))
    return tuple(jnp.asarray(t) for t in (mask, qd, kd, cd))


def kernel(x_prompt, x_sample, c_prompt, c_sample, state_conv, state_ret, w_ada, b_ada, g_mix, g_ffn, w_in,
           conv_w, conv_b, conv_norm_g, conv_norm_b, ret_norm_g, ret_norm_b, w_out, w_router, router_bias,
           w1, w3, w2, ws1, ws3, ws2, g_final):
    depth = w_ada.shape[0]
    assert depth == 1, "single-layer trunk"
    bp, lp, d = x_prompt.shape
    bs, ls, _ = x_sample.shape
    assert ls == 1
    dc = conv_w.shape[2]
    dr = ret_norm_g.shape[1]
    nh = RET_HEADS
    hd = dr // nh
    assert hd == LANES and lp % 256 == 0 and bs % TOKEN_TILE == 0 and d // 2 == ROW_WORDS * LANES
    ne = w_router.shape[2]
    row = lambda a: a.reshape(1, -1)

    mod = _ada(jnp.concatenate([c_prompt, c_sample], axis=0), w_ada[0], row(b_ada[0]))
    mod_p = mod[:bp].reshape(bp, 6, 1, d)
    mod_s = mod[bp:]

    half = hd // 2
    inv = (np.float32(ROPE_BASE) ** (-np.arange(half, dtype=np.float32) / np.float32(half))).astype(np.float32)
    inv2 = jnp.asarray(np.concatenate([inv, inv]).reshape(1, hd))
    cos_p, sin_p = _rope_tables(inv2, lp, 0)
    cos_s, sin_s = _rope_tables(inv2, SUBLANES, PAST_LEN)

    w_in_b = w_in[0].astype(BF16)
    wo_b = w_out[0].astype(BF16)
    wr_t = w_router[0].T
    wrh = wr_t.astype(BF16)
    wrl = (wr_t - wrh.astype(F32)).astype(BF16)
    ws1_b, ws3_b, ws2_b = ws1[0].astype(BF16), ws3[0].astype(BF16), ws2[0].astype(BF16)
    dims = dict(dc=dc, dr=dr, hd=hd)

    tables = _retention_tables(lp, nh, hd)
    gam = jnp.asarray(np.broadcast_to(np.exp(_log_gamma(nh))[:, None, None], (nh, SUBLANES, hd)))
    norm_rows = (row(conv_b[0]), row(conv_norm_g[0]), row(conv_norm_b[0]), row(ret_norm_g[0]), row(ret_norm_b[0]))
    post_w = (row(g_ffn[0]), wo_b, wrh, wrl)
    shared_w = (ws1_b, ws3_b, ws2_b)
    bias_col = router_bias[0].reshape(ne, 1)
    hw = d // 2
    tm = EXPERT_ROWS

    def pre_prompt(b0, nb, after=None):
        glu, q, k, v, sg = _proj(x_prompt, mod_p, row(g_mix[0]), w_in_b, cos_p, sin_p,
                                 per_row_mod=False, b0=b0, nb=nb, after=after, **dims)
        cat, ret = _mix(glu, q, k, v, sg, conv_w[0], *norm_rows, tables, nb=nb, length=lp, **dims)
        x2, hp, lg = _post(cat, x_prompt, mod_p, *post_w, per_row_mod=False, b0=b0, nb=nb)
        return glu, ret, x2, hp, lg

    def pre_sample():
        xs3 = x_sample.reshape(1, bs, d)
        glu, q, k, v, sg = _proj(xs3, mod_s, row(g_mix[0]), w_in_b, cos_s, sin_s, per_row_mod=True, **dims)
        cat, ret, new_conv = _mix1(glu, q, k, v, sg, state_conv, state_ret, conv_w[0], *norm_rows, gam,
                                   **dims)
        x2, hp, lg = _post(cat, xs3, mod_s, *post_w, per_row_mod=True)
        return new_conv, ret, x2, hp, lg

    def route_and_dispatch(hps, lgs):
        lg = lgs[0] if len(lgs) == 1 else jnp.concatenate(lgs, axis=1)
        tokens = lg.shape[1]
        eidx, wts, rank, cnt = _route(lg, bias_col)
        counts = cnt[:, 0].astype(I32)
        n_rows = -(-(tokens * TOP_K + ne * (tm - 1)) // tm) * tm
        dest3 = _dest_rows(counts, eidx, rank).reshape(tokens // TOKEN_TILE, SUBLANES, TOKEN_TILE)
        xs = _dispatch([h.reshape(-1, hw) for h in hps], dest3, n_rows)
        return counts, dest3, wts, xs.reshape(n_rows * ROW_WORDS, LANES)

    def undispatch(ys, dest3, after=None):
        tokens = dest3.shape[0] * TOKEN_TILE
        z = _undispatch(ys.reshape(-1, hw), dest3, tokens, after)
        return z.reshape(TOP_K, tokens * ROW_WORDS, LANES)

    nb0 = bp // 2
    nb1 = bp - nb0
    new_conv_s, ret_s, x2_s, hp_s, lg_s = pre_sample()
    glu_0, ret_0, x2_0, hp_0, lg_0 = pre_prompt(0, nb0, after=lg_s)
    counts_0, dest_0, wts_0, xs_0 = route_and_dispatch([hp_0], [lg_0])
    glu_1, ret_1, x2_1, hp_1, lg_1 = pre_prompt(nb0, nb1, after=dest_0)
    counts_1, dest_1, wts_1, xs_1 = route_and_dispatch([hp_1, hp_s], [lg_1, lg_s])
    ys_0, ys_1 = _experts((counts_0, counts_1), (xs_0, xs_1), w1[0], w3[0], w2[0])
    z_0 = undispatch(ys_0, dest_0)
    z_1 = undispatch(ys_1, dest_1, after=z_0[0, :SUBLANES])
    y_p = _combine(z_0, 0, wts_0, x2_0, hp_0, mod_p, row(g_final), shared_w, rows_per_mod=lp, per_row_mod=False,
                   out_rows=bp * lp)
    y_p = _combine(z_1, 0, wts_1, x2_1, hp_1, mod_p, row(g_final), shared_w, rows_per_mod=lp, per_row_mod=False,
                   b0=nb0, out_rows=bp * lp, into=y_p)
    y_s = _combine(z_1, nb1 * lp, wts_1, x2_s, hp_s, mod_s, row(g_final), shared_w, rows_per_mod=bs,
                   per_row_mod=True)
    ret_p = jnp.concatenate([ret_0, ret_1], axis=0)

    tail = lambda g, n: g.reshape(n, lp, dc)[:, lp - CONV_BUF:, :]
    new_conv_p = jnp.concatenate([tail(glu_0, nb0), tail(glu_1, nb1)], axis=0)
    return (y_p.reshape(bp, lp, d), y_s.reshape(bs, ls, d), new_conv_p[None], ret_p[None],
            new_conv_s, ret_s)
```

```python
import functools
import math

import jax
import jax.numpy as jnp
import numpy as np
from jax import lax
from jax.experimental import pallas as pl
from jax.experimental.pallas import tpu as pltpu
from jax.experimental.pallas import tpu_sc as plsc

F32 = jnp.float32
BF16 = jnp.bfloat16
U32 = jnp.uint32
I32 = jnp.int32

EPS = 1e-6
PAST_LEN = 16384
RET_HEADS = 4
RET_CHUNK = 128
CONV_WIDTH = 31
CONV_BUF = CONV_WIDTH - 1
ROPE_BASE = 10000.0
TOP_K = 6
N_GROUPS = 8
TOPK_GROUPS = 4
ROUTED_SCALE = 2.5

LANES = 128
SUBLANES = 8
CONV_PAD = 32
EXPERT_ROWS = 256
ROUTE_TILE = 1024
DEST_TILE = 4096
TOKEN_TILE = 128
COMBINE_TILE = 512
VMEM_LIMIT = 56 * 1024 * 1024
SC_CORES = 2
SC_SUBCORES = 16
SC_WORKERS = SC_CORES * SC_SUBCORES
SCATTER_ROWS = 64
GATHER_ROWS = 16
ROW_WORDS = 4
X_SLOTS = 6
X_AHEAD = 4
Y_SLOTS = 4

HI_MASK = 0xFFFF0000


def _sigmoid(x):
    return jax.nn.sigmoid(x)


def _silu(x):
    return x * jax.nn.sigmoid(x)


def _pack_halves(lo, hi):
    lo_u = lax.bitcast_convert_type(lo.astype(BF16).astype(F32), U32) >> 16
    hi_u = lax.bitcast_convert_type(hi.astype(BF16).astype(F32), U32) & jnp.uint32(HI_MASK)
    return hi_u | lo_u


def _unpack_halves(p):
    lo = lax.bitcast_convert_type(p << 16, F32)
    hi = lax.bitcast_convert_type(p & jnp.uint32(HI_MASK), F32)
    return lo, hi


def _store_rows(ref, x):
    rows = x.shape[0]
    for j in range(ROW_WORDS):
        ref[pl.ds(j, rows, stride=ROW_WORDS), :] = x[:, LANES * j:LANES * (j + 1)]


def _load_row_word(ref, j, rows):
    return ref[pl.ds(j, rows, stride=ROW_WORDS), :]


def _ada_body(w_ref, b_ref, *refs):
    w = w_ref[...].astype(BF16)
    groups = len(refs) // 2
    for c_ref, o_ref in zip(refs[:groups], refs[groups:]):
        o_ref[...] = jnp.dot(_silu(c_ref[...]).astype(BF16), w, preferred_element_type=F32) + b_ref[...]


def _ada(cs, w_ada, b_ada):
    d, n = w_ada.shape
    tn = 2048
    return pl.pallas_call(
        _ada_body,
        grid=(n // tn,),
        in_specs=[pl.BlockSpec((d, tn), lambda j: (0, j)), pl.BlockSpec((1, tn), lambda j: (0, j))]
        + [pl.BlockSpec(c.shape, lambda j: (0, 0)) for c in cs],
        out_specs=[pl.BlockSpec((c.shape[0], tn), lambda j: (0, j)) for c in cs],
        out_shape=[jax.ShapeDtypeStruct((c.shape[0], n), F32) for c in cs],
        compiler_params=pltpu.CompilerParams(vmem_limit_bytes=VMEM_LIMIT),
        name="ada",
    )(w_ada, b_ada, *cs)


def _rope_body(inv_ref, cos_ref, sin_ref, *, pos0, tl, half):
    row = lax.broadcasted_iota(I32, (tl, LANES), 0) + pl.program_id(0) * tl
    ang = (row.astype(F32) + pos0) * inv_ref[...]
    lane = lax.broadcasted_iota(I32, (tl, LANES), 1)
    s = jnp.sin(ang)
    cos_ref[...] = jnp.cos(ang)
    sin_ref[...] = jnp.where(lane < half, -s, s)


def _rope_tables(inv2, rows, pos0):
    tl = min(rows, 256)
    return pl.pallas_call(
        functools.partial(_rope_body, pos0=float(pos0), tl=tl, half=LANES // 2),
        grid=(rows // tl,),
        in_specs=[pl.BlockSpec((1, LANES), lambda i: (0, 0))],
        out_specs=[pl.BlockSpec((tl, LANES), lambda i: (i, 0))] * 2,
        out_shape=[jax.ShapeDtypeStruct((rows, LANES), F32)] * 2,
        name="rope",
    )(inv2)


def _modulated_rmsnorm(x, g, sc, sh):
    ms = jnp.mean(x * x, axis=-1, keepdims=True)
    h = x * lax.rsqrt(ms + EPS) * g
    return h * (1.0 + sc) + sh


def _proj_body(x_ref, sh_ref, sc_ref, g_ref, w_ref, cos_ref, sin_ref, *rest, dc, dr, hd, rope_rows):
    glu_ref, q_ref, k_ref, v_ref, sg_ref = rest[-5:]
    hb = _modulated_rmsnorm(x_ref[...], g_ref[...], sc_ref[...], sh_ref[...]).astype(BF16)

    def proj(lo, n):
        return jnp.dot(hb, w_ref[:, lo:lo + n], preferred_element_type=F32)

    glu_ref[...] = proj(0, dc) * _sigmoid(proj(dc, dc))
    cos = cos_ref[...] if rope_rows else cos_ref[0:1, :]
    sin = sin_ref[...] if rope_rows else sin_ref[0:1, :]
    for ref, lo, scale in ((q_ref, 2 * dc, hd ** -0.5), (k_ref, 2 * dc + dr, None)):
        t = proj(lo, dr)
        for hh in range(dr // hd):
            th = t[:, hh * hd:(hh + 1) * hd]
            r = th * cos + pltpu.roll(th, hd // 2, 1) * sin
            if scale is not None:
                r = r * scale
            ref[:, hh * hd:(hh + 1) * hd] = r.astype(BF16)
    v_ref[...] = proj(2 * dc + 2 * dr, dr).astype(BF16)
    sg_ref[...] = _silu(proj(2 * dc + 3 * dr, dr)).astype(BF16)


def _proj(x, mod, g_mix, w_in_b, cos2, sin2, *, dc, dr, hd, per_row_mod, b0=0, nb=None, after=None):
    nb_all, length, d = x.shape
    nb = nb_all if nb is None else nb
    tl = min(length, 512)
    x2 = x.reshape(nb_all * length, d)
    nl = length // tl
    if per_row_mod:
        mod_spec = lambda j: pl.BlockSpec((tl, d), lambda b, l: (l, j))
        rope_spec = pl.BlockSpec((SUBLANES, LANES), lambda b, l: (0, 0))
    else:
        mod_spec = lambda j: pl.BlockSpec((None, None, 1, d), lambda b, l: (b + b0, j, 0, 0))
        rope_spec = pl.BlockSpec((tl, LANES), lambda b, l: (l, 0))
    row_spec = lambda w: pl.BlockSpec((tl, w), lambda b, l: (b * nl + l, 0))
    t = nb * length
    outs = pl.pallas_call(
        functools.partial(_proj_body, dc=dc, dr=dr, hd=hd, rope_rows=not per_row_mod),
        grid=(nb, nl),
        in_specs=[
            pl.BlockSpec((tl, d), lambda b, l: ((b + b0) * nl + l, 0)), mod_spec(0), mod_spec(1),
            pl.BlockSpec((1, d), lambda b, l: (0, 0)),
            pl.BlockSpec(w_in_b.shape, lambda b, l: (0, 0)),
            rope_spec, rope_spec,
        ] + ([] if after is None else [pl.BlockSpec(memory_space=pl.ANY)]),
        out_specs=[row_spec(dc), row_spec(dr), row_spec(dr), row_spec(dr), row_spec(dr)],
        out_shape=[
            jax.ShapeDtypeStruct((t, dc), F32),
            jax.ShapeDtypeStruct((t, dr), BF16),
            jax.ShapeDtypeStruct((t, dr), BF16),
            jax.ShapeDtypeStruct((t, dr), BF16),
            jax.ShapeDtypeStruct((t, dr), BF16),
        ],
        compiler_params=pltpu.CompilerParams(vmem_limit_bytes=VMEM_LIMIT),
        name="proj",
    )(x2, mod, mod, g_mix, w_in_b, cos2, sin2, *(() if after is None else (after,)))
    return outs


def _layernorm_silu(c, g, b):
    mu = jnp.mean(c, axis=-1, keepdims=True)
    d = c - mu
    var = jnp.mean(d * d, axis=-1, keepdims=True)
    return _silu(d * lax.rsqrt(var + EPS) * g + b)


def _groupnorm(o, g, b):
    mu = jnp.mean(o, axis=-1, keepdims=True)
    d = o - mu
    var = jnp.mean(d * d, axis=-1, keepdims=True)
    return d * lax.rsqrt(var + EPS) * g + b


def _mix_body(glu_ref, q_ref, k_ref, v_ref, sg_ref, cw_ref, cb_ref, lng_ref, lnb_ref, rg_ref, rb_ref,
              mask_ref, qd_ref, kd_ref, cd_ref, cat_ref, st_ref, buf, cscr, *, tl, dc, hd, nh, chunk):
    nslab = dc // LANES

    @pl.when(pl.program_id(1) == 0)
    def _():
        buf[:, 0:CONV_PAD, :] = jnp.zeros((nslab, CONV_PAD, LANES), F32)
        st_ref[...] = jnp.zeros(st_ref.shape, F32)

    for j in range(nslab):
        buf[j, CONV_PAD:CONV_PAD + tl, :] = glu_ref[:, LANES * j:LANES * (j + 1)]
    first = CONV_PAD - CONV_BUF
    rows_per_iter = 8 * SUBLANES
    for j in range(nslab):
        cols = slice(LANES * j, LANES * (j + 1))
        wv = [jnp.broadcast_to(cw_ref[t:t + 1, cols], (SUBLANES, LANES)) for t in range(CONV_WIDTH)]
        bias = jnp.broadcast_to(cb_ref[0:1, cols], (SUBLANES, LANES))

        def body(r, carry, j=j, cols=cols, wv=wv, bias=bias):
            base = pl.multiple_of(r * rows_per_iter, rows_per_iter)
            for u in range(rows_per_iter // SUBLANES):
                acc = bias
                for t in range(CONV_WIDTH):
                    acc = acc + wv[t] * buf[j, pl.ds(base + (u * SUBLANES + first + t), SUBLANES), :]
                cscr[pl.ds(base + u * SUBLANES, SUBLANES), cols] = acc
            return carry

        lax.fori_loop(0, tl // rows_per_iter, body, 0)
    for j in range(nslab):
        buf[j, 0:CONV_PAD, :] = buf[j, tl:tl + CONV_PAD, :]
    cat_ref[:, 0:dc] = _layernorm_silu(cscr[...], lng_ref[...], lnb_ref[...]).astype(BF16)

    nt = (((1,), (1,)), ((), ()))
    tn = (((0,), (0,)), ((), ()))
    for c in range(tl // chunk):
        rows = slice(c * chunk, (c + 1) * chunk)
        for hh in range(nh):
            cols = slice(hh * hd, (hh + 1) * hd)
            qh = q_ref[rows, cols]
            kh = k_ref[rows, cols]
            vh = v_ref[rows, cols]
            s = st_ref[0, hh]
            scores = lax.dot_general(qh, kh, nt, preferred_element_type=F32) * mask_ref[hh]
            inner = jnp.dot(scores.astype(BF16), vh, preferred_element_type=F32)
            qd = (qh.astype(F32) * qd_ref[hh]).astype(BF16)
            cross = jnp.dot(qd, s.astype(BF16), preferred_element_type=F32)
            kd = (kh.astype(F32) * kd_ref[hh]).astype(BF16)
            st_ref[0, hh] = cd_ref[hh] * s + lax.dot_general(kd, vh, tn, preferred_element_type=F32)
            o = _groupnorm(inner + cross, rg_ref[0:1, cols], rb_ref[0:1, cols])
            cat_ref[rows, dc + hh * hd:dc + (hh + 1) * hd] = (o * sg_ref[rows, cols].astype(F32)).astype(BF16)


def _mix(glu, q, k, v, sg, conv_w, conv_b, ln_g, ln_b, rg, rb, tables, *, nb, length, dc, dr, hd):
    nh = dr // hd
    chunk = math.gcd(length, RET_CHUNK)
    tl = min(length, 256)
    nl = length // tl
    mask, qd, kd, cd = tables
    row_spec = lambda w: pl.BlockSpec((tl, w), lambda b, l: (b * nl + l, 0))
    full = lambda a: pl.BlockSpec(a.shape, lambda b, l: (0,) * a.ndim)
    cat, st = pl.pallas_call(
        functools.partial(_mix_body, tl=tl, dc=dc, hd=hd, nh=nh, chunk=chunk),
        grid=(nb, nl),
        in_specs=[row_spec(dc), row_spec(dr), row_spec(dr), row_spec(dr), row_spec(dr),
                  full(conv_w), full(conv_b), full(ln_g), full(ln_b), full(rg), full(rb),
                  full(mask), full(qd), full(kd), full(cd)],
        out_specs=[row_spec(dc + dr), pl.BlockSpec((1, nh, hd, hd), lambda b, l: (b, 0, 0, 0))],
        out_shape=[jax.ShapeDtypeStruct((nb * length, dc + dr), BF16),
                   jax.ShapeDtypeStruct((nb, nh, hd, hd), F32)],
        scratch_shapes=[pltpu.VMEM((dc // LANES, tl + CONV_PAD, LANES), F32),
                        pltpu.VMEM((tl, dc), F32)],
        compiler_params=pltpu.CompilerParams(dimension_semantics=("arbitrary", "arbitrary")),
        name="mix",
    )(glu, q, k, v, sg, conv_w, conv_b, ln_g, ln_b, rg, rb, mask, qd, kd, cd)
    return cat, st


def _mix1_body(glu_ref, q_ref, k_ref, v_ref, sg_ref, sc_ref, s0_ref, cw_ref, cb_ref, lng_ref, lnb_ref,
               rg_ref, rb_ref, gam_ref, cat_ref, st_ref, nc_ref, oscr, qf, kf, vf, *, tb, dc, hd, nh):
    glu = glu_ref[...]
    conv = glu * cw_ref[CONV_BUF:CONV_WIDTH, :] + cb_ref[...]
    for r in range(CONV_BUF):
        past = sc_ref[r]
        conv = conv + past * cw_ref[r:r + 1, :]
        if r > 0:
            nc_ref[r - 1] = past
    nc_ref[CONV_BUF - 1] = glu
    cat_ref[:, 0:dc] = _layernorm_silu(conv, lng_ref[...], lnb_ref[...]).astype(BF16)

    tn = (((0,), (0,)), ((), ()))
    rowid = lax.broadcasted_iota(I32, (tb, hd), 0)
    qf[...] = q_ref[...].astype(F32)
    kf[...] = k_ref[...].astype(F32)
    vf[...] = v_ref[...].astype(F32)
    for hh in range(nh):
        cols = slice(hh * hd, (hh + 1) * hd)
        qa = q_ref[:, cols]
        ka = k_ref[:, cols]
        gam = gam_ref[hh, 0:1, :]
        for bb in range(tb):
            onehot = (rowid == bb).astype(BF16)
            qcol = lax.dot_general(qa, onehot, tn, preferred_element_type=F32)
            kcol = lax.dot_general(ka, onehot, tn, preferred_element_type=F32)
            s0 = s0_ref[bb, hh]
            qrow = qf[bb:bb + 1, cols]
            krow = kf[bb:bb + 1, cols]
            vrow = vf[bb:bb + 1, cols]
            qk = jnp.sum(qrow * krow, axis=-1, keepdims=True)
            cross = gam * jnp.sum(qcol * s0, axis=0, keepdims=True)
            st_ref[bb, hh] = gam * s0 + kcol * vrow
            oscr[bb:bb + 1, cols] = qk * vrow + cross
    for hh in range(nh):
        cols = slice(hh * hd, (hh + 1) * hd)
        o = _groupnorm(oscr[:, cols], rg_ref[0:1, cols], rb_ref[0:1, cols])
        cat_ref[:, dc + hh * hd:dc + (hh + 1) * hd] = (o * sg_ref[:, cols].astype(F32)).astype(BF16)


def _mix1(glu, q, k, v, sg, state_conv, state_ret, conv_w, conv_b, ln_g, ln_b, rg, rb, gam, *, dc, dr, hd):
    nb = glu.shape[0]
    nh = dr // hd
    tb = 16
    row_spec = lambda w: pl.BlockSpec((tb, w), lambda i: (i, 0))
    full = lambda a: pl.BlockSpec(a.shape, lambda i: (0,) * a.ndim)
    st_spec = pl.BlockSpec((None, tb, nh, hd, hd), lambda i: (0, i, 0, 0, 0))
    conv_spec = pl.BlockSpec((None, CONV_BUF, tb, dc), lambda i: (0, 0, i, 0))
    cat, st, new_conv = pl.pallas_call(
        functools.partial(_mix1_body, tb=tb, dc=dc, hd=hd, nh=nh),
        grid=(nb // tb,),
        in_specs=[row_spec(dc), row_spec(dr), row_spec(dr), row_spec(dr), row_spec(dr), conv_spec, st_spec,
                  full(conv_w), full(conv_b), full(ln_g), full(ln_b), full(rg), full(rb), full(gam)],
        out_specs=[row_spec(dc + dr), st_spec, conv_spec],
        out_shape=[jax.ShapeDtypeStruct((nb, dc + dr), BF16),
                   jax.ShapeDtypeStruct((1, nb, nh, hd, hd), F32),
                   jax.ShapeDtypeStruct((1, CONV_BUF, nb, dc), F32)],
        scratch_shapes=[pltpu.VMEM((tb, dr), F32)] * 4,
        compiler_params=pltpu.CompilerParams(vmem_limit_bytes=VMEM_LIMIT),
        name="mix1",
    )(glu, q, k, v, sg, state_conv, state_ret, conv_w, conv_b, ln_g, ln_b, rg, rb, gam)
    return cat, st, new_conv


def _post_body(cat_ref, x_ref, gtm_ref, scf_ref, shf_ref, g_ref, wo_ref, wrh_ref, wrl_ref, x1_ref, hp_ref, lg_ref):
    d = x_ref.shape[1]
    y = jnp.dot(cat_ref[...], wo_ref[...], preferred_element_type=F32)
    x1 = x_ref[...] + gtm_ref[...] * y
    x1_ref[...] = x1
    h = _modulated_rmsnorm(x1, g_ref[...], scf_ref[...], shf_ref[...])
    hb = h.astype(BF16)
    _store_rows(hp_ref, _pack_halves(h[:, 0:d // 2], h[:, d // 2:d]))
    hl = (h - hb.astype(F32)).astype(BF16)
    nt = (((1,), (1,)), ((), ()))
    lg_ref[...] = (lax.dot_general(wrh_ref[...], hb, nt, preferred_element_type=F32)
                   + lax.dot_general(wrh_ref[...], hl, nt, preferred_element_type=F32)
                   + lax.dot_general(wrl_ref[...], hb, nt, preferred_element_type=F32))


def _post(cat, x, mod, g_ffn, wo_b, wrh, wrl, *, per_row_mod, b0=0, nb=None):
    nb_all, length, d = x.shape
    nb = nb_all if nb is None else nb
    tl = min(length, 1024)
    nl = length // tl
    t = nb * length
    ne = wrh.shape[0]
    x2d = x.reshape(nb_all * length, d)
    if per_row_mod:
        mod_spec = lambda j: pl.BlockSpec((tl, d), lambda b, l: (l, j))
    else:
        mod_spec = lambda j: pl.BlockSpec((None, None, 1, d), lambda b, l: (b + b0, j, 0, 0))
    row_spec = lambda w: pl.BlockSpec((tl, w), lambda b, l: (b * nl + l, 0))
    full = lambda a: pl.BlockSpec(a.shape, lambda b, l: (0,) * a.ndim)
    return pl.pallas_call(
        _post_body,
        grid=(nb, nl),
        in_specs=[row_spec(d), pl.BlockSpec((tl, d), lambda b, l: ((b + b0) * nl + l, 0)),
                  mod_spec(2), mod_spec(4), mod_spec(3),
                  full(g_ffn), full(wo_b), full(wrh), full(wrl)],
        out_specs=[row_spec(d),
                   pl.BlockSpec((tl * ROW_WORDS, LANES), lambda b, l: (b * nl + l, 0)),
                   pl.BlockSpec((ne, tl), lambda b, l: (0, b * nl + l))],
        out_shape=[jax.ShapeDtypeStruct((t, d), F32),
                   jax.ShapeDtypeStruct((t * ROW_WORDS, LANES), U32),
                   jax.ShapeDtypeStruct((ne, t), F32)],
        compiler_params=pltpu.CompilerParams(vmem_limit_bytes=VMEM_LIMIT),
        name="post",
    )(cat, x2d, mod, mod, mod, g_ffn, wo_b, wrh, wrl)


def _first_max(x, idx, sentinel):
    m = jnp.max(x, axis=0, keepdims=True)
    f = jnp.min(jnp.where(x == m, idx, sentinel), axis=0, keepdims=True)
    return m, f


def _route_body(lg_ref, bias_ref, e_ref, w_ref, r_ref, cnt_ref, cnt_scr, before, *, tr, ne, ng, topk, topg):
    @pl.when(pl.program_id(0) == 0)
    def _():
        cnt_scr[...] = jnp.zeros(cnt_scr.shape, F32)
        t_row = lax.broadcasted_iota(I32, (tr, tr), 0)
        t_col = lax.broadcasted_iota(I32, (tr, tr), 1)
        before[...] = (t_row < t_col).astype(BF16)

    per = ne // ng
    neg = -jnp.inf
    scores = _sigmoid(lg_ref[...])
    sel = scores + bias_ref[...]
    sub = lax.broadcasted_iota(I32, (per, tr), 0)
    gs = []
    for g in range(ng):
        s_g = sel[g * per:(g + 1) * per, :]
        m1, f1 = _first_max(s_g, sub, per)
        m2 = jnp.max(jnp.where(sub == f1, neg, s_g), axis=0, keepdims=True)
        gs.append(m1 + m2)
    gsc = jnp.concatenate(gs, axis=0)
    gi = lax.broadcasted_iota(I32, (ng, tr), 0)
    keep = jnp.zeros((ng, tr), F32)
    for _ in range(topg):
        _, f = _first_max(gsc, gi, ng)
        pick = gi == f
        keep = jnp.where(pick, 1.0, keep)
        gsc = jnp.where(pick, neg, gsc)
    work = jnp.concatenate(
        [jnp.where(keep[g:g + 1, :] > 0.5, sel[g * per:(g + 1) * per, :], neg) for g in range(ng)], axis=0)
    ei = lax.broadcasted_iota(I32, (ne, tr), 0)
    picks, es, ws = [], [], []
    for _ in range(topk):
        _, f = _first_max(work, ei, ne)
        pick = ei == f
        picks.append(pick)
        es.append(f)
        ws.append(jnp.sum(jnp.where(pick, scores, 0.0), axis=0, keepdims=True))
        work = jnp.where(pick, neg, work)
    wsum = ws[0]
    for w in ws[1:]:
        wsum = wsum + w
    scale = ROUTED_SCALE / wsum
    chosen = picks[0]
    for p in picks[1:]:
        chosen = jnp.logical_or(chosen, p)
    chosen_f = chosen.astype(F32)
    prior = cnt_scr[:, 0:1] + jnp.dot(chosen_f.astype(BF16), before[...], preferred_element_type=F32)
    rs = [jnp.sum(jnp.where(p, prior, 0.0), axis=0, keepdims=True).astype(I32) for p in picks]
    pad_i = jnp.zeros((SUBLANES - topk, tr), I32)
    pad_f = jnp.zeros((SUBLANES - topk, tr), F32)
    e_ref[...] = jnp.concatenate(es + [pad_i], axis=0)
    w_ref[...] = jnp.concatenate([w * scale for w in ws] + [pad_f], axis=0)
    r_ref[...] = jnp.concatenate(rs + [pad_i], axis=0)
    total = cnt_scr[:, 0:1] + jnp.sum(chosen_f, axis=1, keepdims=True)
    cnt_scr[...] = jnp.broadcast_to(total, cnt_scr.shape)
    cnt_ref[...] = jnp.broadcast_to(total, cnt_ref.shape)


def _token_tile(t, limit):
    return max(m for m in range(LANES, limit + 1, LANES) if t % m == 0)


def _route(logits_t, bias_col):
    ne, t = logits_t.shape
    tr = _token_tile(t, ROUTE_TILE)
    tok = lambda dt: jax.ShapeDtypeStruct((SUBLANES, t), dt)
    tok_spec = pl.BlockSpec((SUBLANES, tr), lambda i: (0, i))
    return pl.pallas_call(
        functools.partial(_route_body, tr=tr, ne=ne, ng=N_GROUPS, topk=TOP_K, topg=TOPK_GROUPS),
        grid=(t // tr,),
        in_specs=[pl.BlockSpec((ne, tr), lambda i: (0, i)), pl.BlockSpec((ne, 1), lambda i: (0, 0))],
        out_specs=[tok_spec, tok_spec, tok_spec, pl.BlockSpec((ne, LANES), lambda i: (0, 0))],
        out_shape=[tok(I32), tok(F32), tok(I32), jax.ShapeDtypeStruct((ne, LANES), F32)],
        scratch_shapes=[pltpu.VMEM((ne, LANES), F32), pltpu.VMEM((tr, tr), BF16)],
        compiler_params=pltpu.CompilerParams(dimension_semantics=("arbitrary",)),
        name="route",
    )(logits_t, bias_col)


def _dest_body(cnt_ref, e_ref, r_ref, d_ref, ps, *, ne, tr):
    shift = EXPERT_ROWS.bit_length() - 1

    @pl.when(pl.program_id(0) == 0)
    def _():
        def step(j, start):
            ps[j] = start
            return start + lax.shift_left(lax.shift_right_logical(cnt_ref[j] + (EXPERT_ROWS - 1), shift), shift)

        lax.fori_loop(0, ne, step, jnp.int32(0))

    e = e_ref[...]
    base = jnp.zeros(e.shape, I32)
    for j in range(ne):
        base = jnp.where(e == j, ps[j], base)
    dest = base + r_ref[...]
    for m in range(tr // TOKEN_TILE):
        d_ref[SUBLANES * m:SUBLANES * (m + 1), :] = dest[:, TOKEN_TILE * m:TOKEN_TILE * (m + 1)]


def _dest_rows(counts, eidx, rank):
    rows, t = eidx.shape
    ne = counts.shape[0]
    tr = _token_tile(t, DEST_TILE)
    spec = pl.BlockSpec((rows, tr), lambda i, cnt: (0, i))
    return pl.pallas_call(
        functools.partial(_dest_body, ne=ne, tr=tr),
        grid_spec=pltpu.PrefetchScalarGridSpec(
            num_scalar_prefetch=1, grid=(t // tr,), in_specs=[spec, spec],
            out_specs=pl.BlockSpec((tr // TOKEN_TILE * rows, TOKEN_TILE), lambda i, cnt: (i, 0)),
            scratch_shapes=[pltpu.SMEM((ne,), I32)]),
        out_shape=jax.ShapeDtypeStruct((t // TOKEN_TILE * rows, TOKEN_TILE), I32),
        compiler_params=pltpu.CompilerParams(dimension_semantics=("arbitrary",)),
        name="dest",
    )(counts, eidx, rank)


def _sc_mesh():
    return plsc.VectorSubcoreMesh(core_axis_name="c", subcore_axis_name="s")


def _sc_worker_id():
    return lax.axis_index("s") * SC_CORES + lax.axis_index("c")


def _index_block(dest_ref, chunk, width):
    per_tile = TOKEN_TILE // width
    return dest_ref.at[chunk // per_tile, :, pl.ds((chunk % per_tile) * width, width)]


def _dispatch(hps, dest3, n_rows):
    w = SCATTER_ROWS
    width = hps[0].shape[1]
    bounds = [0]
    for h in hps:
        bounds.append(bounds[-1] + h.shape[0] // w)
    nch = bounds[-1]
    nsrc = len(hps)

    @functools.partial(
        pl.kernel, mesh=_sc_mesh(),
        out_type=jax.ShapeDtypeStruct((n_rows, width), U32),
        scratch_types=[pltpu.VMEM((2, SUBLANES, w), I32), pltpu.VMEM((2, w, width), U32),
                       pltpu.SemaphoreType.DMA, pltpu.SemaphoreType.DMA, pltpu.SemaphoreType.DMA],
        compiler_params=pltpu.CompilerParams(use_tc_tiling_on_sc=False),
        name="dispatch",
    )
    def run(*refs):
        src_refs, dest_ref, xs_ref = refs[:nsrc], refs[nsrc], refs[nsrc + 1]
        idx_v, rows_v = refs[nsrc + 2:nsrc + 4]
        sem_load = refs[nsrc + 4:nsrc + 6]
        sem_scatter = refs[nsrc + 6]
        wid = _sc_worker_id()

        def start_loads(c, slot):
            pltpu.async_copy(_index_block(dest_ref, c, w), idx_v.at[slot], sem_load[slot])
            for i, src in enumerate(src_refs):
                @pl.when(jnp.logical_and(c >= bounds[i], c < bounds[i + 1]))
                def _(src=src, lo=bounds[i]):
                    pltpu.async_copy(src.at[pl.ds((c - lo) * w, w)], rows_v.at[slot], sem_load[slot])

        def wait_loads(slot):
            pltpu.make_async_copy(_index_block(dest_ref, 0, w), idx_v.at[slot], sem_load[slot]).wait()
            pltpu.make_async_copy(src_refs[0].at[pl.ds(0, w)], rows_v.at[slot], sem_load[slot]).wait()

        @pl.when(wid < nch)
        def _():
            start_loads(wid, 0)

        @pl.loop(0, pl.cdiv(pl.cdiv(nch, SC_WORKERS), 2))
        def _(rr):
            for slot in range(2):
                c = (rr * 2 + slot) * SC_WORKERS + wid

                @pl.when(c < nch)
                def _(c=c, slot=slot):
                    wait_loads(slot)

                    @pl.when(c + SC_WORKERS < nch)
                    def _():
                        start_loads(c + SC_WORKERS, 1 - slot)

                    copies = [pltpu.async_copy(rows_v.at[slot], xs_ref.at[idx_v.at[slot, k]], sem_scatter)
                              for k in range(TOP_K)]
                    for cp in copies:
                        cp.wait()

    return run(*hps, dest3)


def _undispatch(ys, dest3, n_tokens, after=None):
    w = GATHER_ROWS
    width = ys.shape[1]
    nch = n_tokens // w

    @functools.partial(
        pl.kernel, mesh=_sc_mesh(),
        out_type=jax.ShapeDtypeStruct((TOP_K, n_tokens, width), U32),
        scratch_types=[pltpu.VMEM((2, SUBLANES, w), I32), pltpu.VMEM((2, TOP_K, w, width), U32),
                       pltpu.SemaphoreType.DMA, pltpu.SemaphoreType.DMA, pltpu.SemaphoreType.DMA],
        compiler_params=pltpu.CompilerParams(use_tc_tiling_on_sc=False),
        name="undispatch",
    )
    def run(*refs):
        ys_ref, dest_ref = refs[:2]
        z_ref, idx_v, bufs = refs[-6:-3]
        sem_gather = refs[-3:-1]
        sem_store = refs[-1]
        wid = _sc_worker_id()

        def start_gathers(c, slot):
            pltpu.sync_copy(_index_block(dest_ref, c, w), idx_v.at[slot])
            for k in range(TOP_K):
                pltpu.async_copy(ys_ref.at[idx_v.at[slot, k]], bufs.at[slot, k], sem_gather[slot])

        def wait_gathers(slot):
            for k in range(TOP_K):
                pltpu.make_async_copy(ys_ref.at[idx_v.at[slot, k]], bufs.at[slot, k], sem_gather[slot]).wait()

        @pl.when(wid < nch)
        def _():
            start_gathers(wid, 0)

        @pl.loop(0, pl.cdiv(pl.cdiv(nch, SC_WORKERS), 2))
        def _(rr):
            for slot in range(2):
                c = (rr * 2 + slot) * SC_WORKERS + wid

                @pl.when(c < nch)
                def _(c=c, slot=slot):
                    @pl.when(c + SC_WORKERS < nch)
                    def _():
                        start_gathers(c + SC_WORKERS, 1 - slot)

                    wait_gathers(slot)
                    stores = [pltpu.async_copy(bufs.at[slot, k], z_ref.at[k, pl.ds(c * w, w)], sem_store)
                              for k in range(TOP_K)]
                    for cp in stores:
                        cp.wait()

    return run(ys, dest3) if after is None else run(ys, dest3, after)


PART_SHIFT = 24


def _expert_body(*refs, tm, ne, nparts):
    cnt_refs = refs[:nparts]
    xs_refs = refs[nparts:2 * nparts]
    w1_ref, w3_ref, w2_ref = refs[2 * nparts:2 * nparts + 3]
    ys_refs = refs[2 * nparts + 3:3 * nparts + 3]
    (w1f, w3f, w2f, w1s, w3s, w2s, xbuf, ybuf, xlo, xhi, sched, sem_x, sem_y, sem_w) = refs[3 * nparts + 3:]
    blk_words = tm * ROW_WORDS
    half = ROW_WORDS * LANES
    shift = tm.bit_length() - 1

    def n_blocks_of(p, e):
        return lax.shift_right_logical(cnt_refs[p][e] + (tm - 1), shift)

    def n_all(e):
        n = n_blocks_of(0, e)
        for p in range(1, nparts):
            n = n + n_blocks_of(p, e)
        return n

    def next_nonempty(e):
        return lax.while_loop(
            lambda c: jnp.logical_and(c < ne, n_all(jnp.minimum(c, ne - 1)) == 0), lambda c: c + 1, e)

    def plan(e, carry):
        i, starts = carry[0], list(carry[1:])
        for p in range(nparts):
            n = n_blocks_of(p, e)

            def put(j, c, p=p, i=i, start=starts[p]):
                sched[i + j] = (start + j) + (p << PART_SHIFT)
                return c

            lax.fori_loop(0, n, put, 0)
            i = i + n
            starts[p] = starts[p] + n
        return (i, *starts)

    nu = lax.fori_loop(0, ne, plan, (jnp.int32(0),) * (nparts + 1))[0]

    def rows_of(code):
        blk = code & ((1 << PART_SHIFT) - 1)
        return pl.ds(pl.multiple_of(blk * blk_words, blk_words), blk_words)

    def x_start(i, slot):
        code = sched[i]
        for p in range(nparts):
            @pl.when(lax.shift_right_logical(code, PART_SHIFT) == p)
            def _(p=p):
                pltpu.make_async_copy(xs_refs[p].at[rows_of(code), :], xbuf.at[slot], sem_x.at[slot]).start()

    def x_wait(slot):
        pltpu.make_async_copy(xs_refs[0].at[pl.ds(0, blk_words), :], xbuf.at[slot], sem_x.at[slot]).wait()

    def y_start(i, slot):
        code = sched[i]
        for p in range(nparts):
            @pl.when(lax.shift_right_logical(code, PART_SHIFT) == p)
            def _(p=p):
                pltpu.make_async_copy(ybuf.at[slot], ys_refs[p].at[rows_of(code), :], sem_y.at[slot]).start()

    def y_wait(slot):
        pltpu.make_async_copy(ybuf.at[slot], ys_refs[0].at[pl.ds(0, blk_words), :], sem_y.at[slot]).wait()

    def w_copies(e, ws):
        return [pltpu.make_async_copy(src.at[e], dst.at[ws], sem_w.at[ws])
                for src, dst in ((w1_ref, w1f), (w3_ref, w3f), (w2_ref, w2f))]

    for q in range(X_AHEAD):
        @pl.when(q < nu)
        def _(q=q):
            x_start(q, q)

    e_first = next_nonempty(jnp.int32(0))

    @pl.when(e_first < ne)
    def _():
        for cp in w_copies(e_first, 0):
            cp.start()

    def blocks(i, nblk):
        for b in range(nblk):
            x_wait((i + b) % X_SLOTS)
        for b in range(nblk):
            nxt = i + b + X_AHEAD

            @pl.when(nxt < nu)
            def _(nxt=nxt):
                x_start(nxt, nxt % X_SLOTS)

        for b in range(nblk):
            @pl.when(i + b >= Y_SLOTS)
            def _(b=b):
                y_wait((i + b) % Y_SLOTS)

        for b in range(nblk):
            xin = xbuf.at[(i + b) % X_SLOTS]
            for w in range(ROW_WORDS):
                lo, hi = _unpack_halves(_load_row_word(xin, w, tm))
                xlo[b, :, LANES * w:LANES * (w + 1)] = lo.astype(BF16)
                xhi[b, :, LANES * w:LANES * (w + 1)] = hi.astype(BF16)

        def up(b, wsc):
            return (jnp.dot(xlo[b], wsc[0:half, :], preferred_element_type=F32)
                    + jnp.dot(xhi[b], wsc[half:2 * half, :], preferred_element_type=F32))

        for b in range(nblk):
            hid = (_silu(up(b, w1s)) * up(b, w3s)).astype(BF16)
            y = jnp.dot(hid, w2s[...], preferred_element_type=F32)
            _store_rows(ybuf.at[(i + b) % Y_SLOTS], _pack_halves(y[:, 0:half], y[:, half:2 * half]))
        for b in range(nblk):
            y_start(i + b, (i + b) % Y_SLOTS)

    def per_expert(e, carry):
        i0, ws = carry
        n = n_all(e)

        @pl.when(n > 0)
        def _():
            for cp in w_copies(e, ws):
                cp.wait()
            w1s[...] = w1f[ws].astype(BF16)
            w3s[...] = w3f[ws].astype(BF16)
            w2s[...] = w2f[ws].astype(BF16)
            e_next = next_nonempty(e + 1)

            @pl.when(e_next < ne)
            def _():
                for cp in w_copies(e_next, 1 - ws):
                    cp.start()

            def pair(j, c):
                blocks(i0 + 2 * j, 2)
                return c

            lax.fori_loop(0, lax.shift_right_logical(n, 1), pair, 0)

            @pl.when(n % 2 == 1)
            def _():
                blocks(i0 + n - 1, 1)

        return i0 + n, jnp.where(n > 0, 1 - ws, ws)

    lax.fori_loop(0, ne, per_expert, (jnp.int32(0), jnp.int32(0)))

    for q in range(Y_SLOTS):
        @pl.when(nu > q)
        def _(q=q):
            y_wait((nu - 1 - q) % Y_SLOTS)


def _experts(counts, xss, w1e, w3e, w2e):
    tm = EXPERT_ROWS
    ne, d, de = w1e.shape
    half = ROW_WORDS * LANES
    nparts = len(xss)
    cap = sum(x.shape[0] // (tm * ROW_WORDS) for x in xss)
    assert cap < (1 << PART_SHIFT)
    anyspec = pl.BlockSpec(memory_space=pl.ANY)
    blk_buf = lambda n: pltpu.VMEM((n, tm * ROW_WORDS, LANES), U32)
    return pl.pallas_call(
        functools.partial(_expert_body, tm=tm, ne=ne, nparts=nparts),
        grid_spec=pltpu.PrefetchScalarGridSpec(
            num_scalar_prefetch=nparts,
            grid=(1,),
            in_specs=[anyspec] * (nparts + 3),
            out_specs=[anyspec] * nparts,
            scratch_shapes=[pltpu.VMEM((2, d, de), F32), pltpu.VMEM((2, d, de), F32), pltpu.VMEM((2, de, d), F32),
                            pltpu.VMEM((d, de), BF16), pltpu.VMEM((d, de), BF16), pltpu.VMEM((de, d), BF16),
                            blk_buf(X_SLOTS), blk_buf(Y_SLOTS),
                            pltpu.VMEM((2, tm, half), BF16), pltpu.VMEM((2, tm, half), BF16),
                            pltpu.SMEM((cap,), I32),
                            pltpu.SemaphoreType.DMA((X_SLOTS,)), pltpu.SemaphoreType.DMA((Y_SLOTS,)),
                            pltpu.SemaphoreType.DMA((2,))]),
        out_shape=[jax.ShapeDtypeStruct(x.shape, U32) for x in xss],
        compiler_params=pltpu.CompilerParams(dimension_semantics=("arbitrary",), vmem_limit_bytes=VMEM_LIMIT),
        name="experts",
    )(*counts, *xss, w1e, w3e, w2e)


def _combine_body(z_ref, x1_ref, hp_ref, gtf_ref, wt_ref, gfin_ref, ws1_ref, ws3_ref, ws2_ref, y_ref, xo, hlo, hhi,
                  *, td, topk):
    half = ROW_WORDS * LANES
    for w in range(ROW_WORDS):
        lo, hi = _unpack_halves(_load_row_word(hp_ref, w, td))
        hlo[:, LANES * w:LANES * (w + 1)] = lo.astype(BF16)
        hhi[:, LANES * w:LANES * (w + 1)] = hi.astype(BF16)

    def up(w_ref):
        return (jnp.dot(hlo[...], w_ref[0:half, :], preferred_element_type=F32)
                + jnp.dot(hhi[...], w_ref[half:2 * half, :], preferred_element_type=F32))

    xo[...] = jnp.dot((_silu(up(ws1_ref)) * up(ws3_ref)).astype(BF16), ws2_ref[...], preferred_element_type=F32)
    wt = wt_ref[...].T
    ws = [wt[:, k:k + 1] for k in range(topk)]
    sq = jnp.zeros((td, 1), F32)
    for j in range(ROW_WORDS):
        acc_lo = jnp.zeros((td, LANES), F32)
        acc_hi = jnp.zeros((td, LANES), F32)
        for k in range(topk):
            lo, hi = _unpack_halves(_load_row_word(z_ref.at[k], j, td))
            acc_lo = acc_lo + ws[k] * lo
            acc_hi = acc_hi + ws[k] * hi
        for base, acc in ((0, acc_lo), (half, acc_hi)):
            cols = slice(base + LANES * j, base + LANES * (j + 1))
            x = x1_ref[:, cols] + gtf_ref[:, cols] * (acc + xo[:, cols])
            xo[:, cols] = x
            sq = sq + jnp.sum(x * x, axis=-1, keepdims=True)
    rs = lax.rsqrt(sq / (2 * half) + EPS)
    y_ref[...] = xo[...] * rs * gfin_ref[...]


def _combine_body_into(*refs, td, topk):
    _combine_body(*refs[:9], *refs[10:], td=td, topk=topk)


def _combine(z, token0, wts, x1, hp, mod, g_final, shared_w, *, rows_per_mod, per_row_mod, b0=0, out_rows=None,
             into=None):
    t, d = x1.shape
    td = min(t, COMBINE_TILE)
    tile0 = token0 // td
    out_rows = t if out_rows is None else out_rows
    if per_row_mod:
        gtf_spec = pl.BlockSpec((td, d), lambda i: (i, 5))
        out0 = 0
    else:
        tiles_per_mod = rows_per_mod // td
        gtf_spec = pl.BlockSpec((None, None, 1, d), lambda i: (i // tiles_per_mod + b0, 5, 0, 0))
        out0 = b0 * tiles_per_mod
    full = lambda a: pl.BlockSpec(a.shape, lambda i: (0,) * a.ndim)
    in_specs = [pl.BlockSpec((TOP_K, td * ROW_WORDS, LANES), lambda i: (0, i + tile0, 0)),
                pl.BlockSpec((td, d), lambda i: (i, 0)),
                pl.BlockSpec((td * ROW_WORDS, LANES), lambda i: (i, 0)),
                gtf_spec,
                pl.BlockSpec((SUBLANES, td), lambda i: (0, i + tile0)),
                pl.BlockSpec((1, d), lambda i: (0, 0))] + [full(w) for w in shared_w]
    args = [z, x1, hp, mod, wts, g_final, *shared_w]
    body, aliases = _combine_body, {}
    if into is not None:
        in_specs.append(pl.BlockSpec(memory_space=pl.ANY))
        args.append(into)
        body, aliases = _combine_body_into, {len(args) - 1: 0}
    return pl.pallas_call(
        functools.partial(body, td=td, topk=TOP_K),
        grid=(t // td,),
        in_specs=in_specs,
        out_specs=pl.BlockSpec((td, d), lambda i: (i + out0, 0)),
        scratch_shapes=[pltpu.VMEM((td, d), F32), pltpu.VMEM((td, d // 2), BF16), pltpu.VMEM((td, d // 2), BF16)],
        out_shape=jax.ShapeDtypeStruct((out_rows, d), F32),
        input_output_aliases=aliases,
        compiler_params=pltpu.CompilerParams(vmem_limit_bytes=VMEM_LIMIT),
        name="combine",
    )(*args)


def _log_gamma(nh):
    return np.log(1.0 - 2.0 ** (-5.0 - np.arange(nh, dtype=np.float32))).astype(np.float32)


def _retention_tables(length, nh, hd):
    c = math.gcd(length, RET_CHUNK)
    log_g = _log_gamma(nh)
    idx = np.arange(c, dtype=np.float32)
    rel = idx[:, None] - idx[None, :]
    mask = np.where(rel >= 0, np.exp(log_g[:, None, None] * np.maximum(rel, 0.0)), 0.0).astype(np.float32)
    q_decay = np.exp(log_g[None, :] * (idx[:, None] + 1.0)).astype(np.float32)
    k_decay = np.exp(log_g[None, :] * (c - 1.0 - idx[:, None])).astype(np.float32)
    chunk_decay = np.exp(log_g * np.float32(c)).astype(np.float32)
    qd = np.broadcast_to(q_decay.T[:, :, None], (nh, c, hd))
    kd = np.broadcast_to(k_decay.T[:, :, None], (nh, c, hd))
    cd = np.broadcast_to(chunk_decay[:, None, None], (nh, hd, hd))
    return tuple(jnp.asarray(t) for t in (mask, qd, kd, cd))


def kernel(x_prompt, x_sample, c_prompt, c_sample, state_conv, state_ret, w_ada, b_ada, g_mix, g_ffn, w_in,
           conv_w, conv_b, conv_norm_g, conv_norm_b, ret_norm_g, ret_norm_b, w_out, w_router, router_bias,
           w1, w3, w2, ws1, ws3, ws2, g_final):
    depth = w_ada.shape[0]
    assert depth == 1, "single-layer trunk"
    bp, lp, d = x_prompt.shape
    bs, ls, _ = x_sample.shape
    assert ls == 1
    dc = conv_w.shape[2]
    dr = ret_norm_g.shape[1]
    nh = RET_HEADS
    hd = dr // nh
    assert hd == LANES and lp % 256 == 0 and bs % TOKEN_TILE == 0 and d // 2 == ROW_WORDS * LANES
    ne = w_router.shape[2]
    row = lambda a: a.reshape(1, -1)

    mod_p, mod_s = _ada([c_prompt, c_sample], w_ada[0], row(b_ada[0]))
    mod_p = mod_p.reshape(bp, 6, 1, d)

    half = hd // 2
    inv = (np.float32(ROPE_BASE) ** (-np.arange(half, dtype=np.float32) / np.float32(half))).astype(np.float32)
    inv2 = jnp.asarray(np.concatenate([inv, inv]).reshape(1, hd))
    cos_p, sin_p = _rope_tables(inv2, lp, 0)
    cos_s, sin_s = _rope_tables(inv2, SUBLANES, PAST_LEN)

    w_in_b = w_in[0].astype(BF16)
    wo_b = w_out[0].astype(BF16)
    wr_t = w_router[0].T
    wrh = wr_t.astype(BF16)
    wrl = (wr_t - wrh.astype(F32)).astype(BF16)
    ws1_b, ws3_b, ws2_b = ws1[0].astype(BF16), ws3[0].astype(BF16), ws2[0].astype(BF16)
    dims = dict(dc=dc, dr=dr, hd=hd)

    tables = _retention_tables(lp, nh, hd)
    gam = jnp.asarray(np.broadcast_to(np.exp(_log_gamma(nh))[:, None, None], (nh, SUBLANES, hd)))
    norm_rows = (row(conv_b[0]), row(conv_norm_g[0]), row(conv_norm_b[0]), row(ret_norm_g[0]), row(ret_norm_b[0]))
    post_w = (row(g_ffn[0]), wo_b, wrh, wrl)
    shared_w = (ws1_b, ws3_b, ws2_b)
    bias_col = router_bias[0].reshape(ne, 1)
    hw = d // 2
    tm = EXPERT_ROWS

    def pre_prompt(b0, nb, after=None):
        glu, q, k, v, sg = _proj(x_prompt, mod_p, row(g_mix[0]), w_in_b, cos_p, sin_p,
                                 per_row_mod=False, b0=b0, nb=nb, after=after, **dims)
        cat, ret = _mix(glu, q, k, v, sg, conv_w[0], *norm_rows, tables, nb=nb, length=lp, **dims)
        x2, hp, lg = _post(cat, x_prompt, mod_p, *post_w, per_row_mod=False, b0=b0, nb=nb)
        return glu, ret, x2, hp, lg

    def pre_sample():
        xs3 = x_sample.reshape(1, bs, d)
        glu, q, k, v, sg = _proj(xs3, mod_s, row(g_mix[0]), w_in_b, cos_s, sin_s, per_row_mod=True, **dims)
        cat, ret, new_conv = _mix1(glu, q, k, v, sg, state_conv.transpose(0, 2, 1, 3), state_ret, conv_w[0],
                                   *norm_rows, gam, **dims)
        x2, hp, lg = _post(cat, xs3, mod_s, *post_w, per_row_mod=True)
        return new_conv.transpose(0, 2, 1, 3), ret, x2, hp, lg

    def route_and_dispatch(hps, lgs):
        lg = lgs[0] if len(lgs) == 1 else jnp.concatenate(lgs, axis=1)
        tokens = lg.shape[1]
        eidx, wts, rank, cnt = _route(lg, bias_col)
        counts = cnt[:, 0].astype(I32)
        n_rows = -(-(tokens * TOP_K + ne * (tm - 1)) // tm) * tm
        dest3 = _dest_rows(counts, eidx, rank).reshape(tokens // TOKEN_TILE, SUBLANES, TOKEN_TILE)
        xs = _dispatch([h.reshape(-1, hw) for h in hps], dest3, n_rows)
        return counts, dest3, wts, xs.reshape(n_rows * ROW_WORDS, LANES)

    def undispatch(ys, dest3, after=None):
        tokens = dest3.shape[0] * TOKEN_TILE
        z = _undispatch(ys.reshape(-1, hw), dest3, tokens, after)
        return z.reshape(TOP_K, tokens * ROW_WORDS, LANES)

    nb0 = bp // 2
    nb1 = bp - nb0
    new_conv_s, ret_s, x2_s, hp_s, lg_s = pre_sample()
    glu_0, ret_0, x2_0, hp_0, lg_0 = pre_prompt(0, nb0, after=lg_s)
    counts_0, dest_0, wts_0, xs_0 = route_and_dispatch([hp_0], [lg_0])
    glu_1, ret_1, x2_1, hp_1, lg_1 = pre_prompt(nb0, nb1, after=dest_0)
    counts_1, dest_1, wts_1, xs_1 = route_and_dispatch([hp_1, hp_s], [lg_1, lg_s])
    ys_0, ys_1 = _experts((counts_0, counts_1), (xs_0, xs_1), w1[0], w3[0], w2[0])
    z_0 = undispatch(ys_0, dest_0)
    z_1 = undispatch(ys_1, dest_1, after=z_0[0, :SUBLANES])
    y_p = _combine(z_0, 0, wts_0, x2_0, hp_0, mod_p, row(g_final), shared_w, rows_per_mod=lp, per_row_mod=False,
                   out_rows=bp * lp)
    y_p = _combine(z_1, 0, wts_1, x2_1, hp_1, mod_p, row(g_final), shared_w, rows_per_mod=lp, per_row_mod=False,
                   b0=nb0, out_rows=bp * lp, into=y_p)
    y_s = _combine(z_1, nb1 * lp, wts_1, x2_s, hp_s, mod_s, row(g_final), shared_w, rows_per_mod=bs,
                   per_row_mod=True)
    ret_p = jnp.concatenate([ret_0, ret_1], axis=0)

    tail = lambda g, n: g.reshape(n, lp, dc)[:, lp - CONV_BUF:, :]
    new_conv_p = jnp.concatenate([tail(glu_0, nb0), tail(glu_1, nb1)], axis=0)
    return (y_p.reshape(bp, lp, d), y_s.reshape(bs, ls, d), new_conv_p[None], ret_p[None],
            new_conv_s, ret_s)
```

```python
import functools
import math

import jax
import jax.numpy as jnp
import numpy as np
from jax import lax
from jax.experimental import pallas as pl
from jax.experimental.pallas import tpu as pltpu
from jax.experimental.pallas import tpu_sc as plsc

F32 = jnp.float32
BF16 = jnp.bfloat16
U32 = jnp.uint32
I32 = jnp.int32

EPS = 1e-6
PAST_LEN = 16384
RET_HEADS = 4
RET_CHUNK = 128
CONV_WIDTH = 31
CONV_BUF = CONV_WIDTH - 1
ROPE_BASE = 10000.0
TOP_K = 6
N_GROUPS = 8
TOPK_GROUPS = 4
ROUTED_SCALE = 2.5

LANES = 128
SUBLANES = 8
CONV_PAD = 32
EXPERT_ROWS = 256
ROUTE_TILE = 1024
DEST_TILE = 4096
TOKEN_TILE = 128
COMBINE_TILE = 512
VMEM_LIMIT = 56 * 1024 * 1024
SC_CORES = 2
SC_SUBCORES = 16
SC_WORKERS = SC_CORES * SC_SUBCORES
SCATTER_ROWS = 64
GATHER_ROWS = 16
ROW_WORDS = 4
X_SLOTS = 6
X_AHEAD = 4
Y_SLOTS = 4

HI_MASK = 0xFFFF0000


def _sigmoid(x):
    return jax.nn.sigmoid(x)


def _silu(x):
    return x * jax.nn.sigmoid(x)


def _pack_halves(lo, hi):
    lo_u = lax.bitcast_convert_type(lo.astype(BF16).astype(F32), U32) >> 16
    hi_u = lax.bitcast_convert_type(hi.astype(BF16).astype(F32), U32) & jnp.uint32(HI_MASK)
    return hi_u | lo_u


def _unpack_halves(p):
    lo = lax.bitcast_convert_type(p << 16, F32)
    hi = lax.bitcast_convert_type(p & jnp.uint32(HI_MASK), F32)
    return lo, hi


def _store_rows(ref, x):
    rows = x.shape[0]
    for j in range(ROW_WORDS):
        ref[pl.ds(j, rows, stride=ROW_WORDS), :] = x[:, LANES * j:LANES * (j + 1)]


def _load_row_word(ref, j, rows):
    return ref[pl.ds(j, rows, stride=ROW_WORDS), :]


def _ada_body(w_ref, b_ref, *refs):
    w = w_ref[...].astype(BF16)
    groups = len(refs) // 2
    for c_ref, o_ref in zip(refs[:groups], refs[groups:]):
        o_ref[...] = jnp.dot(_silu(c_ref[...]).astype(BF16), w, preferred_element_type=F32) + b_ref[...]


def _ada(cs, w_ada, b_ada):
    d, n = w_ada.shape
    tn = 2048
    return pl.pallas_call(
        _ada_body,
        grid=(n // tn,),
        in_specs=[pl.BlockSpec((d, tn), lambda j: (0, j)), pl.BlockSpec((1, tn), lambda j: (0, j))]
        + [pl.BlockSpec(c.shape, lambda j: (0, 0)) for c in cs],
        out_specs=[pl.BlockSpec((c.shape[0], tn), lambda j: (0, j)) for c in cs],
        out_shape=[jax.ShapeDtypeStruct((c.shape[0], n), F32) for c in cs],
        compiler_params=pltpu.CompilerParams(vmem_limit_bytes=VMEM_LIMIT),
        name="ada",
    )(w_ada, b_ada, *cs)


def _rope_tables(inv2, rows, pos0):
    pos = np.arange(rows, dtype=np.float32) + np.float32(pos0)
    ang = (pos[:, None] * inv2).astype(np.float64)
    sign = np.where(np.arange(inv2.shape[1]) < inv2.shape[1] // 2, -1.0, 1.0)
    return jnp.asarray(np.cos(ang).astype(np.float32)), jnp.asarray((np.sin(ang) * sign).astype(np.float32))


def _modulated_rmsnorm(x, g, sc, sh):
    ms = jnp.mean(x * x, axis=-1, keepdims=True)
    h = x * lax.rsqrt(ms + EPS) * g
    return h * (1.0 + sc) + sh


def _proj_body(x_ref, sh_ref, sc_ref, g_ref, w_ref, cos_ref, sin_ref, *rest, dc, dr, hd, rope_rows):
    glu_ref, q_ref, k_ref, v_ref, sg_ref = rest[-5:]
    hb = _modulated_rmsnorm(x_ref[...], g_ref[...], sc_ref[...], sh_ref[...]).astype(BF16)

    def proj(lo, n):
        return jnp.dot(hb, w_ref[:, lo:lo + n], preferred_element_type=F32)

    glu_ref[...] = proj(0, dc) * _sigmoid(proj(dc, dc))
    cos = cos_ref[...] if rope_rows else cos_ref[0:1, :]
    sin = sin_ref[...] if rope_rows else sin_ref[0:1, :]
    for ref, lo, scale in ((q_ref, 2 * dc, hd ** -0.5), (k_ref, 2 * dc + dr, None)):
        t = proj(lo, dr)
        for hh in range(dr // hd):
            th = t[:, hh * hd:(hh + 1) * hd]
            r = th * cos + pltpu.roll(th, hd // 2, 1) * sin
            if scale is not None:
                r = r * scale
            ref[:, hh * hd:(hh + 1) * hd] = r.astype(BF16)
    v_ref[...] = proj(2 * dc + 2 * dr, dr).astype(BF16)
    sg_ref[...] = _silu(proj(2 * dc + 3 * dr, dr)).astype(BF16)


def _proj(x, mod, g_mix, w_in_b, cos2, sin2, *, dc, dr, hd, per_row_mod, b0=0, nb=None, after=None):
    nb_all, length, d = x.shape
    nb = nb_all if nb is None else nb
    tl = min(length, 512)
    x2 = x.reshape(nb_all * length, d)
    nl = length // tl
    if per_row_mod:
        mod_spec = lambda j: pl.BlockSpec((tl, d), lambda b, l: (l, j))
        rope_spec = pl.BlockSpec((SUBLANES, LANES), lambda b, l: (0, 0))
    else:
        mod_spec = lambda j: pl.BlockSpec((None, None, 1, d), lambda b, l: (b + b0, j, 0, 0))
        rope_spec = pl.BlockSpec((tl, LANES), lambda b, l: (l, 0))
    row_spec = lambda w: pl.BlockSpec((tl, w), lambda b, l: (b * nl + l, 0))
    t = nb * length
    outs = pl.pallas_call(
        functools.partial(_proj_body, dc=dc, dr=dr, hd=hd, rope_rows=not per_row_mod),
        grid=(nb, nl),
        in_specs=[
            pl.BlockSpec((tl, d), lambda b, l: ((b + b0) * nl + l, 0)), mod_spec(0), mod_spec(1),
            pl.BlockSpec((1, d), lambda b, l: (0, 0)),
            pl.BlockSpec(w_in_b.shape, lambda b, l: (0, 0)),
            rope_spec, rope_spec,
        ] + ([] if after is None else [pl.BlockSpec(memory_space=pl.ANY)]),
        out_specs=[row_spec(dc), row_spec(dr), row_spec(dr), row_spec(dr), row_spec(dr)],
        out_shape=[
            jax.ShapeDtypeStruct((t, dc), F32),
            jax.ShapeDtypeStruct((t, dr), BF16),
            jax.ShapeDtypeStruct((t, dr), BF16),
            jax.ShapeDtypeStruct((t, dr), BF16),
            jax.ShapeDtypeStruct((t, dr), BF16),
        ],
        compiler_params=pltpu.CompilerParams(vmem_limit_bytes=VMEM_LIMIT),
        name="proj",
    )(x2, mod, mod, g_mix, w_in_b, cos2, sin2, *(() if after is None else (after,)))
    return outs


def _layernorm_silu(c, g, b):
    mu = jnp.mean(c, axis=-1, keepdims=True)
    d = c - mu
    var = jnp.mean(d * d, axis=-1, keepdims=True)
    return _silu(d * lax.rsqrt(var + EPS) * g + b)


def _groupnorm(o, g, b):
    mu = jnp.mean(o, axis=-1, keepdims=True)
    d = o - mu
    var = jnp.mean(d * d, axis=-1, keepdims=True)
    return d * lax.rsqrt(var + EPS) * g + b


def _mix_body(glu_ref, q_ref, k_ref, v_ref, sg_ref, cw_ref, cb_ref, lng_ref, lnb_ref, rg_ref, rb_ref,
              mask_ref, qd_ref, kd_ref, cd_ref, cat_ref, st_ref, buf, cscr, *, tl, dc, hd, nh, chunk):
    nslab = dc // LANES

    @pl.when(pl.program_id(1) == 0)
    def _():
        buf[:, 0:CONV_PAD, :] = jnp.zeros((nslab, CONV_PAD, LANES), F32)
        st_ref[...] = jnp.zeros(st_ref.shape, F32)

    for j in range(nslab):
        buf[j, CONV_PAD:CONV_PAD + tl, :] = glu_ref[:, LANES * j:LANES * (j + 1)]
    first = CONV_PAD - CONV_BUF
    rows_per_iter = 8 * SUBLANES
    for j in range(nslab):
        cols = slice(LANES * j, LANES * (j + 1))
        wv = [jnp.broadcast_to(cw_ref[t:t + 1, cols], (SUBLANES, LANES)) for t in range(CONV_WIDTH)]
        bias = jnp.broadcast_to(cb_ref[0:1, cols], (SUBLANES, LANES))

        def body(r, carry, j=j, cols=cols, wv=wv, bias=bias):
            base = pl.multiple_of(r * rows_per_iter, rows_per_iter)
            for u in range(rows_per_iter // SUBLANES):
                acc = bias
                for t in range(CONV_WIDTH):
                    acc = acc + wv[t] * buf[j, pl.ds(base + (u * SUBLANES + first + t), SUBLANES), :]
                cscr[pl.ds(base + u * SUBLANES, SUBLANES), cols] = acc
            return carry

        lax.fori_loop(0, tl // rows_per_iter, body, 0)
    for j in range(nslab):
        buf[j, 0:CONV_PAD, :] = buf[j, tl:tl + CONV_PAD, :]
    cat_ref[:, 0:dc] = _layernorm_silu(cscr[...], lng_ref[...], lnb_ref[...]).astype(BF16)

    nt = (((1,), (1,)), ((), ()))
    tn = (((0,), (0,)), ((), ()))
    for c in range(tl // chunk):
        rows = slice(c * chunk, (c + 1) * chunk)
        for hh in range(nh):
            cols = slice(hh * hd, (hh + 1) * hd)
            qh = q_ref[rows, cols]
            kh = k_ref[rows, cols]
            vh = v_ref[rows, cols]
            s = st_ref[0, hh]
            scores = lax.dot_general(qh, kh, nt, preferred_element_type=F32) * mask_ref[hh]
            inner = jnp.dot(scores.astype(BF16), vh, preferred_element_type=F32)
            qd = (qh.astype(F32) * qd_ref[hh]).astype(BF16)
            cross = jnp.dot(qd, s.astype(BF16), preferred_element_type=F32)
            kd = (kh.astype(F32) * kd_ref[hh]).astype(BF16)
            st_ref[0, hh] = cd_ref[hh] * s + lax.dot_general(kd, vh, tn, preferred_element_type=F32)
            o = _groupnorm(inner + cross, rg_ref[0:1, cols], rb_ref[0:1, cols])
            cat_ref[rows, dc + hh * hd:dc + (hh + 1) * hd] = (o * sg_ref[rows, cols].astype(F32)).astype(BF16)


def _mix(glu, q, k, v, sg, conv_w, conv_b, ln_g, ln_b, rg, rb, tables, *, nb, length, dc, dr, hd):
    nh = dr // hd
    chunk = math.gcd(length, RET_CHUNK)
    tl = min(length, 256)
    nl = length // tl
    mask, qd, kd, cd = tables
    row_spec = lambda w: pl.BlockSpec((tl, w), lambda b, l: (b * nl + l, 0))
    full = lambda a: pl.BlockSpec(a.shape, lambda b, l: (0,) * a.ndim)
    cat, st = pl.pallas_call(
        functools.partial(_mix_body, tl=tl, dc=dc, hd=hd, nh=nh, chunk=chunk),
        grid=(nb, nl),
        in_specs=[row_spec(dc), row_spec(dr), row_spec(dr), row_spec(dr), row_spec(dr),
                  full(conv_w), full(conv_b), full(ln_g), full(ln_b), full(rg), full(rb),
                  full(mask), full(qd), full(kd), full(cd)],
        out_specs=[row_spec(dc + dr), pl.BlockSpec((1, nh, hd, hd), lambda b, l: (b, 0, 0, 0))],
        out_shape=[jax.ShapeDtypeStruct((nb * length, dc + dr), BF16),
                   jax.ShapeDtypeStruct((nb, nh, hd, hd), F32)],
        scratch_shapes=[pltpu.VMEM((dc // LANES, tl + CONV_PAD, LANES), F32),
                        pltpu.VMEM((tl, dc), F32)],
        compiler_params=pltpu.CompilerParams(dimension_semantics=("arbitrary", "arbitrary")),
        name="mix",
    )(glu, q, k, v, sg, conv_w, conv_b, ln_g, ln_b, rg, rb, mask, qd, kd, cd)
    return cat, st


def _mix1_body(glu_ref, q_ref, k_ref, v_ref, sg_ref, sc_ref, s0_ref, cw_ref, cb_ref, lng_ref, lnb_ref,
               rg_ref, rb_ref, gam_ref, cat_ref, st_ref, nc_ref, oscr, qf, kf, vf, *, tb, dc, hd, nh):
    glu = glu_ref[...]
    conv = glu * cw_ref[CONV_BUF:CONV_WIDTH, :] + cb_ref[...]
    for r in range(CONV_BUF):
        past = sc_ref[r]
        conv = conv + past * cw_ref[r:r + 1, :]
        if r > 0:
            nc_ref[r - 1] = past
    nc_ref[CONV_BUF - 1] = glu
    cat_ref[:, 0:dc] = _layernorm_silu(conv, lng_ref[...], lnb_ref[...]).astype(BF16)

    tn = (((0,), (0,)), ((), ()))
    rowid = lax.broadcasted_iota(I32, (tb, hd), 0)
    qf[...] = q_ref[...].astype(F32)
    kf[...] = k_ref[...].astype(F32)
    vf[...] = v_ref[...].astype(F32)
    for hh in range(nh):
        cols = slice(hh * hd, (hh + 1) * hd)
        qa = q_ref[:, cols]
        ka = k_ref[:, cols]
        gam = gam_ref[hh, 0:1, :]
        for bb in range(tb):
            onehot = (rowid == bb).astype(BF16)
            qcol = lax.dot_general(qa, onehot, tn, preferred_element_type=F32)
            kcol = lax.dot_general(ka, onehot, tn, preferred_element_type=F32)
            s0 = s0_ref[bb, hh]
            qrow = qf[bb:bb + 1, cols]
            krow = kf[bb:bb + 1, cols]
            vrow = vf[bb:bb + 1, cols]
            qk = jnp.sum(qrow * krow, axis=-1, keepdims=True)
            cross = gam * jnp.sum(qcol * s0, axis=0, keepdims=True)
            st_ref[bb, hh] = gam * s0 + kcol * vrow
            oscr[bb:bb + 1, cols] = qk * vrow + cross
    for hh in range(nh):
        cols = slice(hh * hd, (hh + 1) * hd)
        o = _groupnorm(oscr[:, cols], rg_ref[0:1, cols], rb_ref[0:1, cols])
        cat_ref[:, dc + hh * hd:dc + (hh + 1) * hd] = (o * sg_ref[:, cols].astype(F32)).astype(BF16)


def _mix1(glu, q, k, v, sg, state_conv, state_ret, conv_w, conv_b, ln_g, ln_b, rg, rb, gam, *, dc, dr, hd):
    nb = glu.shape[0]
    nh = dr // hd
    tb = 16
    row_spec = lambda w: pl.BlockSpec((tb, w), lambda i: (i, 0))
    full = lambda a: pl.BlockSpec(a.shape, lambda i: (0,) * a.ndim)
    st_spec = pl.BlockSpec((None, tb, nh, hd, hd), lambda i: (0, i, 0, 0, 0))
    conv_spec = pl.BlockSpec((None, CONV_BUF, tb, dc), lambda i: (0, 0, i, 0))
    cat, st, new_conv = pl.pallas_call(
        functools.partial(_mix1_body, tb=tb, dc=dc, hd=hd, nh=nh),
        grid=(nb // tb,),
        in_specs=[row_spec(dc), row_spec(dr), row_spec(dr), row_spec(dr), row_spec(dr), conv_spec, st_spec,
                  full(conv_w), full(conv_b), full(ln_g), full(ln_b), full(rg), full(rb), full(gam)],
        out_specs=[row_spec(dc + dr), st_spec, conv_spec],
        out_shape=[jax.ShapeDtypeStruct((nb, dc + dr), BF16),
                   jax.ShapeDtypeStruct((1, nb, nh, hd, hd), F32),
                   jax.ShapeDtypeStruct((1, CONV_BUF, nb, dc), F32)],
        scratch_shapes=[pltpu.VMEM((tb, dr), F32)] * 4,
        compiler_params=pltpu.CompilerParams(vmem_limit_bytes=VMEM_LIMIT),
        name="mix1",
    )(glu, q, k, v, sg, state_conv, state_ret, conv_w, conv_b, ln_g, ln_b, rg, rb, gam)
    return cat, st, new_conv


def _post_body(cat_ref, x_ref, gtm_ref, scf_ref, shf_ref, g_ref, wo_ref, wrh_ref, wrl_ref, x1_ref, hp_ref, lg_ref):
    d = x_ref.shape[1]
    y = jnp.dot(cat_ref[...], wo_ref[...], preferred_element_type=F32)
    x1 = x_ref[...] + gtm_ref[...] * y
    x1_ref[...] = x1
    h = _modulated_rmsnorm(x1, g_ref[...], scf_ref[...], shf_ref[...])
    hb = h.astype(BF16)
    _store_rows(hp_ref, _pack_halves(h[:, 0:d // 2], h[:, d // 2:d]))
    hl = (h - hb.astype(F32)).astype(BF16)
    nt = (((1,), (1,)), ((), ()))
    lg_ref[...] = (lax.dot_general(wrh_ref[...], hb, nt, preferred_element_type=F32)
                   + lax.dot_general(wrh_ref[...], hl, nt, preferred_element_type=F32)
                   + lax.dot_general(wrl_ref[...], hb, nt, preferred_element_type=F32))


def _post(cat, x, mod, g_ffn, wo_b, wrh, wrl, *, per_row_mod, b0=0, nb=None):
    nb_all, length, d = x.shape
    nb = nb_all if nb is None else nb
    tl = min(length, 1024)
    nl = length // tl
    t = nb * length
    ne = wrh.shape[0]
    x2d = x.reshape(nb_all * length, d)
    if per_row_mod:
        mod_spec = lambda j: pl.BlockSpec((tl, d), lambda b, l: (l, j))
    else:
        mod_spec = lambda j: pl.BlockSpec((None, None, 1, d), lambda b, l: (b + b0, j, 0, 0))
    row_spec = lambda w: pl.BlockSpec((tl, w), lambda b, l: (b * nl + l, 0))
    full = lambda a: pl.BlockSpec(a.shape, lambda b, l: (0,) * a.ndim)
    return pl.pallas_call(
        _post_body,
        grid=(nb, nl),
        in_specs=[row_spec(d), pl.BlockSpec((tl, d), lambda b, l: ((b + b0) * nl + l, 0)),
                  mod_spec(2), mod_spec(4), mod_spec(3),
                  full(g_ffn), full(wo_b), full(wrh), full(wrl)],
        out_specs=[row_spec(d),
                   pl.BlockSpec((tl * ROW_WORDS, LANES), lambda b, l: (b * nl + l, 0)),
                   pl.BlockSpec((ne, tl), lambda b, l: (0, b * nl + l))],
        out_shape=[jax.ShapeDtypeStruct((t, d), F32),
                   jax.ShapeDtypeStruct((t * ROW_WORDS, LANES), U32),
                   jax.ShapeDtypeStruct((ne, t), F32)],
        compiler_params=pltpu.CompilerParams(vmem_limit_bytes=VMEM_LIMIT),
        name="post",
    )(cat, x2d, mod, mod, mod, g_ffn, wo_b, wrh, wrl)


def _first_max(x, idx, sentinel):
    m = jnp.max(x, axis=0, keepdims=True)
    f = jnp.min(jnp.where(x == m, idx, sentinel), axis=0, keepdims=True)
    return m, f


def _route_body(lg_ref, bias_ref, e_ref, w_ref, r_ref, cnt_ref, cnt_scr, before, *, tr, ne, ng, topk, topg):
    @pl.when(pl.program_id(0) == 0)
    def _():
        cnt_scr[...] = jnp.zeros(cnt_scr.shape, F32)
        t_row = lax.broadcasted_iota(I32, (tr, tr), 0)
        t_col = lax.broadcasted_iota(I32, (tr, tr), 1)
        before[...] = (t_row < t_col).astype(BF16)

    per = ne // ng
    neg = -jnp.inf
    scores = _sigmoid(lg_ref[...])
    sel = scores + bias_ref[...]
    sub = lax.broadcasted_iota(I32, (per, tr), 0)
    gs = []
    for g in range(ng):
        s_g = sel[g * per:(g + 1) * per, :]
        m1, f1 = _first_max(s_g, sub, per)
        m2 = jnp.max(jnp.where(sub == f1, neg, s_g), axis=0, keepdims=True)
        gs.append(m1 + m2)
    gsc = jnp.concatenate(gs, axis=0)
    gi = lax.broadcasted_iota(I32, (ng, tr), 0)
    keep = jnp.zeros((ng, tr), F32)
    for _ in range(topg):
        _, f = _first_max(gsc, gi, ng)
        pick = gi == f
        keep = jnp.where(pick, 1.0, keep)
        gsc = jnp.where(pick, neg, gsc)
    work = jnp.concatenate(
        [jnp.where(keep[g:g + 1, :] > 0.5, sel[g * per:(g + 1) * per, :], neg) for g in range(ng)], axis=0)
    ei = lax.broadcasted_iota(I32, (ne, tr), 0)
    picks, es, ws = [], [], []
    for _ in range(topk):
        _, f = _first_max(work, ei, ne)
        pick = ei == f
        picks.append(pick)
        es.append(f)
        ws.append(jnp.sum(jnp.where(pick, scores, 0.0), axis=0, keepdims=True))
        work = jnp.where(pick, neg, work)
    wsum = ws[0]
    for w in ws[1:]:
        wsum = wsum + w
    scale = ROUTED_SCALE / wsum
    chosen = picks[0]
    for p in picks[1:]:
        chosen = jnp.logical_or(chosen, p)
    chosen_f = chosen.astype(F32)
    prior = cnt_scr[:, 0:1] + jnp.dot(chosen_f.astype(BF16), before[...], preferred_element_type=F32)
    rs = [jnp.sum(jnp.where(p, prior, 0.0), axis=0, keepdims=True).astype(I32) for p in picks]
    pad_i = jnp.zeros((SUBLANES - topk, tr), I32)
    pad_f = jnp.zeros((SUBLANES - topk, tr), F32)
    e_ref[...] = jnp.concatenate(es + [pad_i], axis=0)
    w_ref[...] = jnp.concatenate([w * scale for w in ws] + [pad_f], axis=0)
    r_ref[...] = jnp.concatenate(rs + [pad_i], axis=0)
    total = cnt_scr[:, 0:1] + jnp.sum(chosen_f, axis=1, keepdims=True)
    cnt_scr[...] = jnp.broadcast_to(total, cnt_scr.shape)
    cnt_ref[...] = jnp.broadcast_to(total, cnt_ref.shape)


def _token_tile(t, limit):
    return max(m for m in range(LANES, limit + 1, LANES) if t % m == 0)


def _route(logits_t, bias_col):
    ne, t = logits_t.shape
    tr = _token_tile(t, ROUTE_TILE)
    tok = lambda dt: jax.ShapeDtypeStruct((SUBLANES, t), dt)
    tok_spec = pl.BlockSpec((SUBLANES, tr), lambda i: (0, i))
    return pl.pallas_call(
        functools.partial(_route_body, tr=tr, ne=ne, ng=N_GROUPS, topk=TOP_K, topg=TOPK_GROUPS),
        grid=(t // tr,),
        in_specs=[pl.BlockSpec((ne, tr), lambda i: (0, i)), pl.BlockSpec((ne, 1), lambda i: (0, 0))],
        out_specs=[tok_spec, tok_spec, tok_spec, pl.BlockSpec((ne, LANES), lambda i: (0, 0))],
        out_shape=[tok(I32), tok(F32), tok(I32), jax.ShapeDtypeStruct((ne, LANES), F32)],
        scratch_shapes=[pltpu.VMEM((ne, LANES), F32), pltpu.VMEM((tr, tr), BF16)],
        compiler_params=pltpu.CompilerParams(dimension_semantics=("arbitrary",)),
        name="route",
    )(logits_t, bias_col)


def _dest_body(cnt_ref, e_ref, r_ref, d_ref, ps, *, ne, tr):
    shift = EXPERT_ROWS.bit_length() - 1

    @pl.when(pl.program_id(0) == 0)
    def _():
        def step(j, start):
            ps[j] = start
            return start + lax.shift_left(lax.shift_right_logical(cnt_ref[j] + (EXPERT_ROWS - 1), shift), shift)

        lax.fori_loop(0, ne, step, jnp.int32(0))

    e = e_ref[...]
    base = jnp.zeros(e.shape, I32)
    for j in range(ne):
        base = jnp.where(e == j, ps[j], base)
    dest = base + r_ref[...]
    for m in range(tr // TOKEN_TILE):
        d_ref[SUBLANES * m:SUBLANES * (m + 1), :] = dest[:, TOKEN_TILE * m:TOKEN_TILE * (m + 1)]


def _dest_rows(counts, eidx, rank):
    rows, t = eidx.shape
    ne = counts.shape[0]
    tr = _token_tile(t, DEST_TILE)
    spec = pl.BlockSpec((rows, tr), lambda i, cnt: (0, i))
    return pl.pallas_call(
        functools.partial(_dest_body, ne=ne, tr=tr),
        grid_spec=pltpu.PrefetchScalarGridSpec(
            num_scalar_prefetch=1, grid=(t // tr,), in_specs=[spec, spec],
            out_specs=pl.BlockSpec((tr // TOKEN_TILE * rows, TOKEN_TILE), lambda i, cnt: (i, 0)),
            scratch_shapes=[pltpu.SMEM((ne,), I32)]),
        out_shape=jax.ShapeDtypeStruct((t // TOKEN_TILE * rows, TOKEN_TILE), I32),
        compiler_params=pltpu.CompilerParams(dimension_semantics=("arbitrary",)),
        name="dest",
    )(counts, eidx, rank)


def _sc_mesh():
    return plsc.VectorSubcoreMesh(core_axis_name="c", subcore_axis_name="s")


def _sc_worker_id():
    return lax.axis_index("s") * SC_CORES + lax.axis_index("c")


def _index_block(dest_ref, chunk, width):
    per_tile = TOKEN_TILE // width
    return dest_ref.at[chunk // per_tile, :, pl.ds((chunk % per_tile) * width, width)]


def _dispatch(hps, dest3, n_rows):
    w = SCATTER_ROWS
    width = hps[0].shape[1]
    bounds = [0]
    for h in hps:
        bounds.append(bounds[-1] + h.shape[0] // w)
    nch = bounds[-1]
    nsrc = len(hps)

    @functools.partial(
        pl.kernel, mesh=_sc_mesh(),
        out_type=jax.ShapeDtypeStruct((n_rows, width), U32),
        scratch_types=[pltpu.VMEM((2, SUBLANES, w), I32), pltpu.VMEM((2, w, width), U32),
                       pltpu.SemaphoreType.DMA, pltpu.SemaphoreType.DMA, pltpu.SemaphoreType.DMA],
        compiler_params=pltpu.CompilerParams(use_tc_tiling_on_sc=False),
        name="dispatch",
    )
    def run(*refs):
        src_refs, dest_ref, xs_ref = refs[:nsrc], refs[nsrc], refs[nsrc + 1]
        idx_v, rows_v = refs[nsrc + 2:nsrc + 4]
        sem_load = refs[nsrc + 4:nsrc + 6]
        sem_scatter = refs[nsrc + 6]
        wid = _sc_worker_id()

        def start_loads(c, slot):
            pltpu.async_copy(_index_block(dest_ref, c, w), idx_v.at[slot], sem_load[slot])
            for i, src in enumerate(src_refs):
                @pl.when(jnp.logical_and(c >= bounds[i], c < bounds[i + 1]))
                def _(src=src, lo=bounds[i]):
                    pltpu.async_copy(src.at[pl.ds((c - lo) * w, w)], rows_v.at[slot], sem_load[slot])

        def wait_loads(slot):
            pltpu.make_async_copy(_index_block(dest_ref, 0, w), idx_v.at[slot], sem_load[slot]).wait()
            pltpu.make_async_copy(src_refs[0].at[pl.ds(0, w)], rows_v.at[slot], sem_load[slot]).wait()

        @pl.when(wid < nch)
        def _():
            start_loads(wid, 0)

        @pl.loop(0, pl.cdiv(pl.cdiv(nch, SC_WORKERS), 2))
        def _(rr):
            for slot in range(2):
                c = (rr * 2 + slot) * SC_WORKERS + wid

                @pl.when(c < nch)
                def _(c=c, slot=slot):
                    wait_loads(slot)

                    @pl.when(c + SC_WORKERS < nch)
                    def _():
                        start_loads(c + SC_WORKERS, 1 - slot)

                    copies = [pltpu.async_copy(rows_v.at[slot], xs_ref.at[idx_v.at[slot, k]], sem_scatter)
                              for k in range(TOP_K)]
                    for cp in copies:
                        cp.wait()

    return run(*hps, dest3)


def _undispatch(ys, dest3, n_tokens, after=None):
    w = GATHER_ROWS
    width = ys.shape[1]
    nch = n_tokens // w

    @functools.partial(
        pl.kernel, mesh=_sc_mesh(),
        out_type=jax.ShapeDtypeStruct((TOP_K, n_tokens, width), U32),
        scratch_types=[pltpu.VMEM((2, SUBLANES, w), I32), pltpu.VMEM((2, TOP_K, w, width), U32),
                       pltpu.SemaphoreType.DMA, pltpu.SemaphoreType.DMA, pltpu.SemaphoreType.DMA],
        compiler_params=pltpu.CompilerParams(use_tc_tiling_on_sc=False),
        name="undispatch",
    )
    def run(*refs):
        ys_ref, dest_ref = refs[:2]
        z_ref, idx_v, bufs = refs[-6:-3]
        sem_gather = refs[-3:-1]
        sem_store = refs[-1]
        wid = _sc_worker_id()

        def start_gathers(c, slot):
            pltpu.sync_copy(_index_block(dest_ref, c, w), idx_v.at[slot])
            for k in range(TOP_K):
                pltpu.async_copy(ys_ref.at[idx_v.at[slot, k]], bufs.at[slot, k], sem_gather[slot])

        def wait_gathers(slot):
            for k in range(TOP_K):
                pltpu.make_async_copy(ys_ref.at[idx_v.at[slot, k]], bufs.at[slot, k], sem_gather[slot]).wait()

        @pl.when(wid < nch)
        def _():
            start_gathers(wid, 0)

        @pl.loop(0, pl.cdiv(pl.cdiv(nch, SC_WORKERS), 2))
        def _(rr):
            for slot in range(2):
                c = (rr * 2 + slot) * SC_WORKERS + wid

                @pl.when(c < nch)
                def _(c=c, slot=slot):
                    @pl.when(c + SC_WORKERS < nch)
                    def _():
                        start_gathers(c + SC_WORKERS, 1 - slot)

                    wait_gathers(slot)
                    stores = [pltpu.async_copy(bufs.at[slot, k], z_ref.at[k, pl.ds(c * w, w)], sem_store)
                              for k in range(TOP_K)]
                    for cp in stores:
                        cp.wait()

    return run(ys, dest3) if after is None else run(ys, dest3, after)


PART_SHIFT = 24


def _expert_body(*refs, tm, ne, nparts):
    cnt_refs = refs[:nparts]
    xs_refs = refs[nparts:2 * nparts]
    w1_ref, w3_ref, w2_ref = refs[2 * nparts:2 * nparts + 3]
    ys_refs = refs[2 * nparts + 3:3 * nparts + 3]
    (w1f, w3f, w2f, w1s, w3s, w2s, xbuf, ybuf, xlo, xhi, sched, sem_x, sem_y, sem_w) = refs[3 * nparts + 3:]
    blk_words = tm * ROW_WORDS
    half = ROW_WORDS * LANES
    shift = tm.bit_length() - 1

    def n_blocks_of(p, e):
        return lax.shift_right_logical(cnt_refs[p][e] + (tm - 1), shift)

    def n_all(e):
        n = n_blocks_of(0, e)
        for p in range(1, nparts):
            n = n + n_blocks_of(p, e)
        return n

    def next_nonempty(e):
        return lax.while_loop(
            lambda c: jnp.logical_and(c < ne, n_all(jnp.minimum(c, ne - 1)) == 0), lambda c: c + 1, e)

    def plan(e, carry):
        i, starts = carry[0], list(carry[1:])
        for p in range(nparts):
            n = n_blocks_of(p, e)

            def put(j, c, p=p, i=i, start=starts[p]):
                sched[i + j] = (start + j) + (p << PART_SHIFT)
                return c

            lax.fori_loop(0, n, put, 0)
            i = i + n
            starts[p] = starts[p] + n
        return (i, *starts)

    nu = lax.fori_loop(0, ne, plan, (jnp.int32(0),) * (nparts + 1))[0]

    def rows_of(code):
        blk = code & ((1 << PART_SHIFT) - 1)
        return pl.ds(pl.multiple_of(blk * blk_words, blk_words), blk_words)

    def x_start(i, slot):
        code = sched[i]
        for p in range(nparts):
            @pl.when(lax.shift_right_logical(code, PART_SHIFT) == p)
            def _(p=p):
                pltpu.make_async_copy(xs_refs[p].at[rows_of(code), :], xbuf.at[slot], sem_x.at[slot]).start()

    def x_wait(slot):
        pltpu.make_async_copy(xs_refs[0].at[pl.ds(0, blk_words), :], xbuf.at[slot], sem_x.at[slot]).wait()

    def y_start(i, slot):
        code = sched[i]
        for p in range(nparts):
            @pl.when(lax.shift_right_logical(code, PART_SHIFT) == p)
            def _(p=p):
                pltpu.make_async_copy(ybuf.at[slot], ys_refs[p].at[rows_of(code), :], sem_y.at[slot]).start()

    def y_wait(slot):
        pltpu.make_async_copy(ybuf.at[slot], ys_refs[0].at[pl.ds(0, blk_words), :], sem_y.at[slot]).wait()

    def w_copies(e, ws):
        return [pltpu.make_async_copy(src.at[e], dst.at[ws], sem_w.at[ws])
                for src, dst in ((w1_ref, w1f), (w3_ref, w3f), (w2_ref, w2f))]

    for q in range(X_AHEAD):
        @pl.when(q < nu)
        def _(q=q):
            x_start(q, q)

    e_first = next_nonempty(jnp.int32(0))

    @pl.when(e_first < ne)
    def _():
        for cp in w_copies(e_first, 0):
            cp.start()

    def blocks(i, nblk):
        for b in range(nblk):
            x_wait((i + b) % X_SLOTS)
        for b in range(nblk):
            nxt = i + b + X_AHEAD

            @pl.when(nxt < nu)
            def _(nxt=nxt):
                x_start(nxt, nxt % X_SLOTS)

        for b in range(nblk):
            @pl.when(i + b >= Y_SLOTS)
            def _(b=b):
                y_wait((i + b) % Y_SLOTS)

        for b in range(nblk):
            xin = xbuf.at[(i + b) % X_SLOTS]
            for w in range(ROW_WORDS):
                lo, hi = _unpack_halves(_load_row_word(xin, w, tm))
                xlo[b, :, LANES * w:LANES * (w + 1)] = lo.astype(BF16)
                xhi[b, :, LANES * w:LANES * (w + 1)] = hi.astype(BF16)

        def up(b, wsc):
            return (jnp.dot(xlo[b], wsc[0:half, :], preferred_element_type=F32)
                    + jnp.dot(xhi[b], wsc[half:2 * half, :], preferred_element_type=F32))

        for b in range(nblk):
            hid = (_silu(up(b, w1s)) * up(b, w3s)).astype(BF16)
            y = jnp.dot(hid, w2s[...], preferred_element_type=F32)
            _store_rows(ybuf.at[(i + b) % Y_SLOTS], _pack_halves(y[:, 0:half], y[:, half:2 * half]))
        for b in range(nblk):
            y_start(i + b, (i + b) % Y_SLOTS)

    def per_expert(e, carry):
        i0, ws = carry
        n = n_all(e)

        @pl.when(n > 0)
        def _():
            for cp in w_copies(e, ws):
                cp.wait()
            w1s[...] = w1f[ws].astype(BF16)
            w3s[...] = w3f[ws].astype(BF16)
            w2s[...] = w2f[ws].astype(BF16)
            e_next = next_nonempty(e + 1)

            @pl.when(e_next < ne)
            def _():
                for cp in w_copies(e_next, 1 - ws):
                    cp.start()

            def pair(j, c):
                blocks(i0 + 2 * j, 2)
                return c

            lax.fori_loop(0, lax.shift_right_logical(n, 1), pair, 0)

            @pl.when(n % 2 == 1)
            def _():
                blocks(i0 + n - 1, 1)

        return i0 + n, jnp.where(n > 0, 1 - ws, ws)

    lax.fori_loop(0, ne, per_expert, (jnp.int32(0), jnp.int32(0)))

    for q in range(Y_SLOTS):
        @pl.when(nu > q)
        def _(q=q):
            y_wait((nu - 1 - q) % Y_SLOTS)


def _experts(counts, xss, w1e, w3e, w2e):
    tm = EXPERT_ROWS
    ne, d, de = w1e.shape
    half = ROW_WORDS * LANES
    nparts = len(xss)
    cap = sum(x.shape[0] // (tm * ROW_WORDS) for x in xss)
    assert cap < (1 << PART_SHIFT)
    anyspec = pl.BlockSpec(memory_space=pl.ANY)
    blk_buf = lambda n: pltpu.VMEM((n, tm * ROW_WORDS, LANES), U32)
    return pl.pallas_call(
        functools.partial(_expert_body, tm=tm, ne=ne, nparts=nparts),
        grid_spec=pltpu.PrefetchScalarGridSpec(
            num_scalar_prefetch=nparts,
            grid=(1,),
            in_specs=[anyspec] * (nparts + 3),
            out_specs=[anyspec] * nparts,
            scratch_shapes=[pltpu.VMEM((2, d, de), F32), pltpu.VMEM((2, d, de), F32), pltpu.VMEM((2, de, d), F32),
                            pltpu.VMEM((d, de), BF16), pltpu.VMEM((d, de), BF16), pltpu.VMEM((de, d), BF16),
                            blk_buf(X_SLOTS), blk_buf(Y_SLOTS),
                            pltpu.VMEM((2, tm, half), BF16), pltpu.VMEM((2, tm, half), BF16),
                            pltpu.SMEM((cap,), I32),
                            pltpu.SemaphoreType.DMA((X_SLOTS,)), pltpu.SemaphoreType.DMA((Y_SLOTS,)),
                            pltpu.SemaphoreType.DMA((2,))]),
        out_shape=[jax.ShapeDtypeStruct(x.shape, U32) for x in xss],
        compiler_params=pltpu.CompilerParams(dimension_semantics=("arbitrary",), vmem_limit_bytes=VMEM_LIMIT),
        name="experts",
    )(*counts, *xss, w1e, w3e, w2e)


def _combine_body(z_ref, x1_ref, hp_ref, gtf_ref, wt_ref, gfin_ref, ws1_ref, ws3_ref, ws2_ref, y_ref, xo, hlo, hhi,
                  *, td, topk):
    half = ROW_WORDS * LANES
    for w in range(ROW_WORDS):
        lo, hi = _unpack_halves(_load_row_word(hp_ref, w, td))
        hlo[:, LANES * w:LANES * (w + 1)] = lo.astype(BF16)
        hhi[:, LANES * w:LANES * (w + 1)] = hi.astype(BF16)

    def up(w_ref):
        return (jnp.dot(hlo[...], w_ref[0:half, :], preferred_element_type=F32)
                + jnp.dot(hhi[...], w_ref[half:2 * half, :], preferred_element_type=F32))

    xo[...] = jnp.dot((_silu(up(ws1_ref)) * up(ws3_ref)).astype(BF16), ws2_ref[...], preferred_element_type=F32)
    wt = wt_ref[...].T
    ws = [wt[:, k:k + 1] for k in range(topk)]
    sq = jnp.zeros((td, 1), F32)
    for j in range(ROW_WORDS):
        acc_lo = jnp.zeros((td, LANES), F32)
        acc_hi = jnp.zeros((td, LANES), F32)
        for k in range(topk):
            lo, hi = _unpack_halves(_load_row_word(z_ref.at[k], j, td))
            acc_lo = acc_lo + ws[k] * lo
            acc_hi = acc_hi + ws[k] * hi
        for base, acc in ((0, acc_lo), (half, acc_hi)):
            cols = slice(base + LANES * j, base + LANES * (j + 1))
            x = x1_ref[:, cols] + gtf_ref[:, cols] * (acc + xo[:, cols])
            xo[:, cols] = x
            sq = sq + jnp.sum(x * x, axis=-1, keepdims=True)
    rs = lax.rsqrt(sq / (2 * half) + EPS)
    y_ref[...] = xo[...] * rs * gfin_ref[...]


def _combine_body_into(*refs, td, topk):
    _combine_body(*refs[:9], *refs[10:], td=td, topk=topk)


def _combine(z, token0, wts, x1, hp, mod, g_final, shared_w, *, rows_per_mod, per_row_mod, b0=0, out_rows=None,
             into=None):
    t, d = x1.shape
    td = min(t, COMBINE_TILE)
    tile0 = token0 // td
    out_rows = t if out_rows is None else out_rows
    if per_row_mod:
        gtf_spec = pl.BlockSpec((td, d), lambda i: (i, 5))
        out0 = 0
    else:
        tiles_per_mod = rows_per_mod // td
        gtf_spec = pl.BlockSpec((None, None, 1, d), lambda i: (i // tiles_per_mod + b0, 5, 0, 0))
        out0 = b0 * tiles_per_mod
    full = lambda a: pl.BlockSpec(a.shape, lambda i: (0,) * a.ndim)
    in_specs = [pl.BlockSpec((TOP_K, td * ROW_WORDS, LANES), lambda i: (0, i + tile0, 0)),
                pl.BlockSpec((td, d), lambda i: (i, 0)),
                pl.BlockSpec((td * ROW_WORDS, LANES), lambda i: (i, 0)),
                gtf_spec,
                pl.BlockSpec((SUBLANES, td), lambda i: (0, i + tile0)),
                pl.BlockSpec((1, d), lambda i: (0, 0))] + [full(w) for w in shared_w]
    args = [z, x1, hp, mod, wts, g_final, *shared_w]
    body, aliases = _combine_body, {}
    if into is not None:
        in_specs.append(pl.BlockSpec(memory_space=pl.ANY))
        args.append(into)
        body, aliases = _combine_body_into, {len(args) - 1: 0}
    return pl.pallas_call(
        functools.partial(body, td=td, topk=TOP_K),
        grid=(t // td,),
        in_specs=in_specs,
        out_specs=pl.BlockSpec((td, d), lambda i: (i + out0, 0)),
        scratch_shapes=[pltpu.VMEM((td, d), F32), pltpu.VMEM((td, d // 2), BF16), pltpu.VMEM((td, d // 2), BF16)],
        out_shape=jax.ShapeDtypeStruct((out_rows, d), F32),
        input_output_aliases=aliases,
        compiler_params=pltpu.CompilerParams(vmem_limit_bytes=VMEM_LIMIT),
        name="combine",
    )(*args)


def _log_gamma(nh):
    return np.log(1.0 - 2.0 ** (-5.0 - np.arange(nh, dtype=np.float32))).astype(np.float32)


def _retention_tables(length, nh, hd):
    c = math.gcd(length, RET_CHUNK)
    log_g = _log_gamma(nh)
    idx = np.arange(c, dtype=np.float32)
    rel = idx[:, None] - idx[None, :]
    mask = np.where(rel >= 0, np.exp(log_g[:, None, None] * np.maximum(rel, 0.0)), 0.0).astype(np.float32)
    q_decay = np.exp(log_g[None, :] * (idx[:, None] + 1.0)).astype(np.float32)
    k_decay = np.exp(log_g[None, :] * (c - 1.0 - idx[:, None])).astype(np.float32)
    chunk_decay = np.exp(log_g * np.float32(c)).astype(np.float32)
    qd = np.broadcast_to(q_decay.T[:, :, None], (nh, c, hd))
    kd = np.broadcast_to(k_decay.T[:, :, None], (nh, c, hd))
    cd = np.broadcast_to(chunk_decay[:, None, None], (nh, hd, hd))
    return tuple(jnp.asarray(t) for t in (mask, qd, kd, cd))


def kernel(x_prompt, x_sample, c_prompt, c_sample, state_conv, state_ret, w_ada, b_ada, g_mix, g_ffn, w_in,
           conv_w, conv_b, conv_norm_g, conv_norm_b, ret_norm_g, ret_norm_b, w_out, w_router, router_bias,
           w1, w3, w2, ws1, ws3, ws2, g_final):
    depth = w_ada.shape[0]
    assert depth == 1, "single-layer trunk"
    bp, lp, d = x_prompt.shape
    bs, ls, _ = x_sample.shape
    assert ls == 1
    dc = conv_w.shape[2]
    dr = ret_norm_g.shape[1]
    nh = RET_HEADS
    hd = dr // nh
    assert hd == LANES and lp % 256 == 0 and bs % TOKEN_TILE == 0 and d // 2 == ROW_WORDS * LANES
    ne = w_router.shape[2]
    row = lambda a: a.reshape(1, -1)

    mod_p, mod_s = _ada([c_prompt, c_sample], w_ada[0], row(b_ada[0]))
    mod_p = mod_p.reshape(bp, 6, 1, d)

    half = hd // 2
    inv = (np.float32(ROPE_BASE) ** (-np.arange(half, dtype=np.float32) / np.float32(half))).astype(np.float32)
    inv2 = np.concatenate([inv, inv]).reshape(1, hd)
    cos_p, sin_p = _rope_tables(inv2, lp, 0)
    cos_s, sin_s = _rope_tables(inv2, SUBLANES, PAST_LEN)

    w_in_b = w_in[0].astype(BF16)
    wo_b = w_out[0].astype(BF16)
    wr_t = w_router[0].T
    wrh = wr_t.astype(BF16)
    wrl = (wr_t - wrh.astype(F32)).astype(BF16)
    ws1_b, ws3_b, ws2_b = ws1[0].astype(BF16), ws3[0].astype(BF16), ws2[0].astype(BF16)
    dims = dict(dc=dc, dr=dr, hd=hd)

    tables = _retention_tables(lp, nh, hd)
    gam = jnp.asarray(np.broadcast_to(np.exp(_log_gamma(nh))[:, None, None], (nh, SUBLANES, hd)))
    norm_rows = (row(conv_b[0]), row(conv_norm_g[0]), row(conv_norm_b[0]), row(ret_norm_g[0]), row(ret_norm_b[0]))
    post_w = (row(g_ffn[0]), wo_b, wrh, wrl)
    shared_w = (ws1_b, ws3_b, ws2_b)
    bias_col = router_bias[0].reshape(ne, 1)
    hw = d // 2
    tm = EXPERT_ROWS

    def pre_prompt(b0, nb, after=None):
        glu, q, k, v, sg = _proj(x_prompt, mod_p, row(g_mix[0]), w_in_b, cos_p, sin_p,
                                 per_row_mod=False, b0=b0, nb=nb, after=after, **dims)
        cat, ret = _mix(glu, q, k, v, sg, conv_w[0], *norm_rows, tables, nb=nb, length=lp, **dims)
        x2, hp, lg = _post(cat, x_prompt, mod_p, *post_w, per_row_mod=False, b0=b0, nb=nb)
        return glu, ret, x2, hp, lg

    def pre_sample():
        xs3 = x_sample.reshape(1, bs, d)
        glu, q, k, v, sg = _proj(xs3, mod_s, row(g_mix[0]), w_in_b, cos_s, sin_s, per_row_mod=True, **dims)
        cat, ret, new_conv = _mix1(glu, q, k, v, sg, state_conv.transpose(0, 2, 1, 3), state_ret, conv_w[0],
                                   *norm_rows, gam, **dims)
        x2, hp, lg = _post(cat, xs3, mod_s, *post_w, per_row_mod=True)
        return new_conv.transpose(0, 2, 1, 3), ret, x2, hp, lg

    def route_and_dispatch(hps, lgs):
        lg = lgs[0] if len(lgs) == 1 else jnp.concatenate(lgs, axis=1)
        tokens = lg.shape[1]
        eidx, wts, rank, cnt = _route(lg, bias_col)
        counts = cnt[:, 0].astype(I32)
        n_rows = -(-(tokens * TOP_K + ne * (tm - 1)) // tm) * tm
        dest3 = _dest_rows(counts, eidx, rank).reshape(tokens // TOKEN_TILE, SUBLANES, TOKEN_TILE)
        xs = _dispatch([h.reshape(-1, hw) for h in hps], dest3, n_rows)
        return counts, dest3, wts, xs.reshape(n_rows * ROW_WORDS, LANES)

    def undispatch(ys, dest3, after=None):
        tokens = dest3.shape[0] * TOKEN_TILE
        z = _undispatch(ys.reshape(-1, hw), dest3, tokens, after)
        return z.reshape(TOP_K, tokens * ROW_WORDS, LANES)

    nb0 = bp // 2
    nb1 = bp - nb0
    new_conv_s, ret_s, x2_s, hp_s, lg_s = pre_sample()
    glu_0, ret_0, x2_0, hp_0, lg_0 = pre_prompt(0, nb0, after=lg_s)
    counts_0, dest_0, wts_0, xs_0 = route_and_dispatch([hp_0], [lg_0])
    glu_1, ret_1, x2_1, hp_1, lg_1 = pre_prompt(nb0, nb1, after=dest_0)
    counts_1, dest_1, wts_1, xs_1 = route_and_dispatch([hp_1, hp_s], [lg_1, lg_s])
    ys_0, ys_1 = _experts((counts_0, counts_1), (xs_0, xs_1), w1[0], w3[0], w2[0])
    z_0 = undispatch(ys_0, dest_0)
    z_1 = undispatch(ys_1, dest_1, after=z_0[0, :SUBLANES])
    y_p = _combine(z_0, 0, wts_0, x2_0, hp_0, mod_p, row(g_final), shared_w, rows_per_mod=lp, per_row_mod=False,
                   out_rows=bp * lp)
    y_p = _combine(z_1, 0, wts_1, x2_1, hp_1, mod_p, row(g_final), shared_w, rows_per_mod=lp, per_row_mod=False,
                   b0=nb0, out_rows=bp * lp, into=y_p)
    y_s = _combine(z_1, nb1 * lp, wts_1, x2_s, hp_s, mod_s, row(g_final), shared_w, rows_per_mod=bs,
                   per_row_mod=True)
    ret_p = jnp.concatenate([ret_0, ret_1], axis=0)

    tail = lambda g, n: g.reshape(n, lp, dc)[:, lp - CONV_BUF:, :]
    new_conv_p = jnp.concatenate([tail(glu_0, nb0), tail(glu_1, nb1)], axis=0)
    return (y_p.reshape(bp, lp, d), y_s.reshape(bs, ls, d), new_conv_p[None], ret_p[None],
            new_conv_s, ret_s)
```

```python
import functools
import math

import jax
import jax.numpy as jnp
import numpy as np
from jax import lax
from jax.experimental import pallas as pl
from jax.experimental.pallas import tpu as pltpu
from jax.experimental.pallas import tpu_sc as plsc

F32 = jnp.float32
BF16 = jnp.bfloat16
U32 = jnp.uint32
I32 = jnp.int32

EPS = 1e-6
PAST_LEN = 16384
RET_HEADS = 4
RET_CHUNK = 128
CONV_WIDTH = 31
CONV_BUF = CONV_WIDTH - 1
ROPE_BASE = 10000.0
TOP_K = 6
N_GROUPS = 8
TOPK_GROUPS = 4
ROUTED_SCALE = 2.5

LANES = 128
SUBLANES = 8
CONV_PAD = 32
EXPERT_ROWS = 256
ROUTE_TILE = 1024
DEST_TILE = 4096
TOKEN_TILE = 128
COMBINE_TILE = 512
VMEM_LIMIT = 56 * 1024 * 1024
SC_CORES = 2
SC_SUBCORES = 16
SC_WORKERS = SC_CORES * SC_SUBCORES
SCATTER_ROWS = 64
GATHER_ROWS = 16
ROW_WORDS = 4
X_SLOTS = 6
X_AHEAD = 4
Y_SLOTS = 4

HI_MASK = 0xFFFF0000


def _sigmoid(x):
    return jax.nn.sigmoid(x)


def _silu(x):
    return x * jax.nn.sigmoid(x)


def _pack_halves(lo, hi):
    lo_u = lax.bitcast_convert_type(lo.astype(BF16).astype(F32), U32) >> 16
    hi_u = lax.bitcast_convert_type(hi.astype(BF16).astype(F32), U32) & jnp.uint32(HI_MASK)
    return hi_u | lo_u


def _unpack_halves(p):
    lo = lax.bitcast_convert_type(p << 16, F32)
    hi = lax.bitcast_convert_type(p & jnp.uint32(HI_MASK), F32)
    return lo, hi


def _store_rows(ref, x):
    rows = x.shape[0]
    for j in range(ROW_WORDS):
        ref[pl.ds(j, rows, stride=ROW_WORDS), :] = x[:, LANES * j:LANES * (j + 1)]


def _load_row_word(ref, j, rows):
    return ref[pl.ds(j, rows, stride=ROW_WORDS), :]


def _ada_body(w_ref, b_ref, *refs):
    w = w_ref[...].astype(BF16)
    groups = len(refs) // 2
    for c_ref, o_ref in zip(refs[:groups], refs[groups:]):
        o_ref[...] = jnp.dot(_silu(c_ref[...]).astype(BF16), w, preferred_element_type=F32) + b_ref[...]


def _ada(cs, w_ada, b_ada):
    d, n = w_ada.shape
    tn = 2048
    return pl.pallas_call(
        _ada_body,
        grid=(n // tn,),
        in_specs=[pl.BlockSpec((d, tn), lambda j: (0, j)), pl.BlockSpec((1, tn), lambda j: (0, j))]
        + [pl.BlockSpec(c.shape, lambda j: (0, 0)) for c in cs],
        out_specs=[pl.BlockSpec((c.shape[0], tn), lambda j: (0, j)) for c in cs],
        out_shape=[jax.ShapeDtypeStruct((c.shape[0], n), F32) for c in cs],
        compiler_params=pltpu.CompilerParams(vmem_limit_bytes=VMEM_LIMIT),
        name="ada",
    )(w_ada, b_ada, *cs)


def _rope_tables(inv2, rows, pos0):
    pos = np.arange(rows, dtype=np.float32) + np.float32(pos0)
    ang = (pos[:, None] * inv2).astype(np.float64)
    sign = np.where(np.arange(inv2.shape[1]) < inv2.shape[1] // 2, -1.0, 1.0)
    return jnp.asarray(np.cos(ang).astype(np.float32)), jnp.asarray((np.sin(ang) * sign).astype(np.float32))


def _modulated_rmsnorm(x, g, sc, sh):
    ms = jnp.mean(x * x, axis=-1, keepdims=True)
    h = x * lax.rsqrt(ms + EPS) * g
    return h * (1.0 + sc) + sh


def _proj_body(x_ref, sh_ref, sc_ref, g_ref, w_ref, cos_ref, sin_ref, *rest, dc, dr, hd, rope_rows):
    glu_ref, q_ref, k_ref, v_ref, sg_ref = rest[-5:]
    hb = _modulated_rmsnorm(x_ref[...], g_ref[...], sc_ref[...], sh_ref[...]).astype(BF16)

    def proj(lo, n):
        return jnp.dot(hb, w_ref[:, lo:lo + n], preferred_element_type=F32)

    glu_ref[...] = proj(0, dc) * _sigmoid(proj(dc, dc))
    cos = cos_ref[...] if rope_rows else cos_ref[0:1, :]
    sin = sin_ref[...] if rope_rows else sin_ref[0:1, :]
    for ref, lo, scale in ((q_ref, 2 * dc, hd ** -0.5), (k_ref, 2 * dc + dr, None)):
        t = proj(lo, dr)
        for hh in range(dr // hd):
            th = t[:, hh * hd:(hh + 1) * hd]
            r = th * cos + pltpu.roll(th, hd // 2, 1) * sin
            if scale is not None:
                r = r * scale
            ref[:, hh * hd:(hh + 1) * hd] = r.astype(BF16)
    v_ref[...] = proj(2 * dc + 2 * dr, dr).astype(BF16)
    sg_ref[...] = _silu(proj(2 * dc + 3 * dr, dr)).astype(BF16)


def _proj(x, mod, g_mix, w_in_b, cos2, sin2, *, dc, dr, hd, per_row_mod, b0=0, nb=None, after=None):
    nb_all, length, d = x.shape
    nb = nb_all if nb is None else nb
    tl = min(length, 512)
    x2 = x.reshape(nb_all * length, d)
    nl = length // tl
    if per_row_mod:
        mod_spec = lambda j: pl.BlockSpec((tl, d), lambda b, l: (l, j))
        rope_spec = pl.BlockSpec((SUBLANES, LANES), lambda b, l: (0, 0))
    else:
        mod_spec = lambda j: pl.BlockSpec((None, None, 1, d), lambda b, l: (b + b0, j, 0, 0))
        rope_spec = pl.BlockSpec((tl, LANES), lambda b, l: (l, 0))
    row_spec = lambda w: pl.BlockSpec((tl, w), lambda b, l: (b * nl + l, 0))
    t = nb * length
    outs = pl.pallas_call(
        functools.partial(_proj_body, dc=dc, dr=dr, hd=hd, rope_rows=not per_row_mod),
        grid=(nb, nl),
        in_specs=[
            pl.BlockSpec((tl, d), lambda b, l: ((b + b0) * nl + l, 0)), mod_spec(0), mod_spec(1),
            pl.BlockSpec((1, d), lambda b, l: (0, 0)),
            pl.BlockSpec(w_in_b.shape, lambda b, l: (0, 0)),
            rope_spec, rope_spec,
        ] + ([] if after is None else [pl.BlockSpec(memory_space=pl.ANY)]),
        out_specs=[row_spec(dc), row_spec(dr), row_spec(dr), row_spec(dr), row_spec(dr)],
        out_shape=[
            jax.ShapeDtypeStruct((t, dc), F32),
            jax.ShapeDtypeStruct((t, dr), BF16),
            jax.ShapeDtypeStruct((t, dr), BF16),
            jax.ShapeDtypeStruct((t, dr), BF16),
            jax.ShapeDtypeStruct((t, dr), BF16),
        ],
        compiler_params=pltpu.CompilerParams(vmem_limit_bytes=VMEM_LIMIT),
        name="proj",
    )(x2, mod, mod, g_mix, w_in_b, cos2, sin2, *(() if after is None else (after,)))
    return outs


def _layernorm_silu(c, g, b):
    mu = jnp.mean(c, axis=-1, keepdims=True)
    d = c - mu
    var = jnp.mean(d * d, axis=-1, keepdims=True)
    return _silu(d * lax.rsqrt(var + EPS) * g + b)


def _groupnorm(o, g, b):
    mu = jnp.mean(o, axis=-1, keepdims=True)
    d = o - mu
    var = jnp.mean(d * d, axis=-1, keepdims=True)
    return d * lax.rsqrt(var + EPS) * g + b


def _mix_body(glu_ref, q_ref, k_ref, v_ref, sg_ref, cw_ref, cb_ref, lng_ref, lnb_ref, rg_ref, rb_ref,
              mask_ref, qd_ref, kd_ref, cd_ref, x_ref, gtm_ref, scf_ref, shf_ref, g_ref, wo_ref, wrh_ref, wrl_ref,
              x1_ref, hp_ref, lg_ref, st_ref, buf, cscr, cat_ref, *, tl, sub, dc, hd, nh, chunk):
    nslab = dc // LANES
    part = pl.program_id(1) % sub
    off = pl.multiple_of(part * tl, tl)

    @pl.when(pl.program_id(1) == 0)
    def _():
        buf[:, 0:CONV_PAD, :] = jnp.zeros((nslab, CONV_PAD, LANES), F32)
        st_ref[...] = jnp.zeros(st_ref.shape, F32)

    for j in range(nslab):
        buf[j, CONV_PAD:CONV_PAD + tl, :] = glu_ref[:, LANES * j:LANES * (j + 1)]
    first = CONV_PAD - CONV_BUF
    rows_per_iter = 8 * SUBLANES
    for j in range(nslab):
        cols = slice(LANES * j, LANES * (j + 1))
        wv = [jnp.broadcast_to(cw_ref[t:t + 1, cols], (SUBLANES, LANES)) for t in range(CONV_WIDTH)]
        bias = jnp.broadcast_to(cb_ref[0:1, cols], (SUBLANES, LANES))

        def body(r, carry, j=j, cols=cols, wv=wv, bias=bias):
            base = pl.multiple_of(r * rows_per_iter, rows_per_iter)
            for u in range(rows_per_iter // SUBLANES):
                acc = bias
                for t in range(CONV_WIDTH):
                    acc = acc + wv[t] * buf[j, pl.ds(base + (u * SUBLANES + first + t), SUBLANES), :]
                cscr[pl.ds(base + u * SUBLANES, SUBLANES), cols] = acc
            return carry

        lax.fori_loop(0, tl // rows_per_iter, body, 0)
    for j in range(nslab):
        buf[j, 0:CONV_PAD, :] = buf[j, tl:tl + CONV_PAD, :]
    cat_ref[pl.ds(off, tl), 0:dc] = _layernorm_silu(cscr[...], lng_ref[...], lnb_ref[...]).astype(BF16)

    nt = (((1,), (1,)), ((), ()))
    tn = (((0,), (0,)), ((), ()))
    for c in range(tl // chunk):
        rows = slice(c * chunk, (c + 1) * chunk)
        for hh in range(nh):
            cols = slice(hh * hd, (hh + 1) * hd)
            qh = q_ref[rows, cols]
            kh = k_ref[rows, cols]
            vh = v_ref[rows, cols]
            s = st_ref[0, hh]
            scores = lax.dot_general(qh, kh, nt, preferred_element_type=F32) * mask_ref[hh]
            inner = jnp.dot(scores.astype(BF16), vh, preferred_element_type=F32)
            qd = (qh.astype(F32) * qd_ref[hh]).astype(BF16)
            cross = jnp.dot(qd, s.astype(BF16), preferred_element_type=F32)
            kd = (kh.astype(F32) * kd_ref[hh]).astype(BF16)
            st_ref[0, hh] = cd_ref[hh] * s + lax.dot_general(kd, vh, tn, preferred_element_type=F32)
            o = _groupnorm(inner + cross, rg_ref[0:1, cols], rb_ref[0:1, cols])
            cat_ref[pl.ds(off + c * chunk, chunk), dc + hh * hd:dc + (hh + 1) * hd] = (
                o * sg_ref[rows, cols].astype(F32)).astype(BF16)

    @pl.when(part == sub - 1)
    def _():
        _post_math(cat_ref[...], x_ref, gtm_ref, scf_ref, shf_ref, g_ref, wo_ref, wrh_ref, wrl_ref,
                   x1_ref, hp_ref, lg_ref)


def _mix(glu, q, k, v, sg, conv_w, conv_b, ln_g, ln_b, rg, rb, tables, x, mod, g_ffn, wo_b, wrh, wrl, *,
         b0, nb, dc, dr, hd):
    nb_all, length, d = x.shape
    nh = dr // hd
    ne = wrh.shape[0]
    chunk = math.gcd(length, RET_CHUNK)
    tl = min(length, 256)
    nl = length // tl
    sub = math.gcd(nl, 4)
    tp = sub * tl
    t = nb * length
    mask, qd, kd, cd = tables
    row_spec = lambda w: pl.BlockSpec((tl, w), lambda b, l: (b * nl + l, 0))
    post_spec = lambda rows, w: pl.BlockSpec((rows, w), lambda b, l: ((b * nl + l) // sub, 0))
    mod_spec = lambda j: pl.BlockSpec((None, None, 1, d), lambda b, l: (b + b0, j, 0, 0))
    full = lambda a: pl.BlockSpec(a.shape, lambda b, l: (0,) * a.ndim)
    return pl.pallas_call(
        functools.partial(_mix_body, tl=tl, sub=sub, dc=dc, hd=hd, nh=nh, chunk=chunk),
        grid=(nb, nl),
        in_specs=[row_spec(dc), row_spec(dr), row_spec(dr), row_spec(dr), row_spec(dr),
                  full(conv_w), full(conv_b), full(ln_g), full(ln_b), full(rg), full(rb),
                  full(mask), full(qd), full(kd), full(cd),
                  pl.BlockSpec((tp, d), lambda b, l: (((b + b0) * nl + l) // sub, 0)),
                  mod_spec(2), mod_spec(4), mod_spec(3), full(g_ffn), full(wo_b), full(wrh), full(wrl)],
        out_specs=[post_spec(tp, d), post_spec(tp * ROW_WORDS, LANES),
                   pl.BlockSpec((ne, tp), lambda b, l: (0, (b * nl + l) // sub)),
                   pl.BlockSpec((1, nh, hd, hd), lambda b, l: (b, 0, 0, 0))],
        out_shape=[jax.ShapeDtypeStruct((t, d), F32),
                   jax.ShapeDtypeStruct((t * ROW_WORDS, LANES), U32),
                   jax.ShapeDtypeStruct((ne, t), F32),
                   jax.ShapeDtypeStruct((nb, nh, hd, hd), F32)],
        scratch_shapes=[pltpu.VMEM((dc // LANES, tl + CONV_PAD, LANES), F32),
                        pltpu.VMEM((tl, dc), F32),
                        pltpu.VMEM((tp, dc + dr), BF16)],
        compiler_params=pltpu.CompilerParams(dimension_semantics=("arbitrary", "arbitrary"),
                                             vmem_limit_bytes=VMEM_LIMIT),
        name="mix",
    )(glu, q, k, v, sg, conv_w, conv_b, ln_g, ln_b, rg, rb, mask, qd, kd, cd,
      x.reshape(nb_all * length, d), mod, mod, mod, g_ffn, wo_b, wrh, wrl)


def _mix1_body(glu_ref, q_ref, k_ref, v_ref, sg_ref, sc_ref, s0_ref, cw_ref, cb_ref, lng_ref, lnb_ref,
               rg_ref, rb_ref, gam_ref, cat_ref, st_ref, nc_ref, oscr, qf, kf, vf, *, tb, dc, hd, nh):
    glu = glu_ref[...]
    conv = glu * cw_ref[CONV_BUF:CONV_WIDTH, :] + cb_ref[...]
    for r in range(CONV_BUF):
        past = sc_ref[r]
        conv = conv + past * cw_ref[r:r + 1, :]
        if r > 0:
            nc_ref[r - 1] = past
    nc_ref[CONV_BUF - 1] = glu
    cat_ref[:, 0:dc] = _layernorm_silu(conv, lng_ref[...], lnb_ref[...]).astype(BF16)

    tn = (((0,), (0,)), ((), ()))
    rowid = lax.broadcasted_iota(I32, (tb, hd), 0)
    qf[...] = q_ref[...].astype(F32)
    kf[...] = k_ref[...].astype(F32)
    vf[...] = v_ref[...].astype(F32)
    for hh in range(nh):
        cols = slice(hh * hd, (hh + 1) * hd)
        qa = q_ref[:, cols]
        ka = k_ref[:, cols]
        gam = gam_ref[hh, 0:1, :]
        for bb in range(tb):
            onehot = (rowid == bb).astype(BF16)
            qcol = lax.dot_general(qa, onehot, tn, preferred_element_type=F32)
            kcol = lax.dot_general(ka, onehot, tn, preferred_element_type=F32)
            s0 = s0_ref[bb, hh]
            qrow = qf[bb:bb + 1, cols]
            krow = kf[bb:bb + 1, cols]
            vrow = vf[bb:bb + 1, cols]
            qk = jnp.sum(qrow * krow, axis=-1, keepdims=True)
            cross = gam * jnp.sum(qcol * s0, axis=0, keepdims=True)
            st_ref[bb, hh] = gam * s0 + kcol * vrow
            oscr[bb:bb + 1, cols] = qk * vrow + cross
    for hh in range(nh):
        cols = slice(hh * hd, (hh + 1) * hd)
        o = _groupnorm(oscr[:, cols], rg_ref[0:1, cols], rb_ref[0:1, cols])
        cat_ref[:, dc + hh * hd:dc + (hh + 1) * hd] = (o * sg_ref[:, cols].astype(F32)).astype(BF16)


def _mix1(glu, q, k, v, sg, state_conv, state_ret, conv_w, conv_b, ln_g, ln_b, rg, rb, gam, *, dc, dr, hd):
    nb = glu.shape[0]
    nh = dr // hd
    tb = 16
    row_spec = lambda w: pl.BlockSpec((tb, w), lambda i: (i, 0))
    full = lambda a: pl.BlockSpec(a.shape, lambda i: (0,) * a.ndim)
    st_spec = pl.BlockSpec((None, tb, nh, hd, hd), lambda i: (0, i, 0, 0, 0))
    conv_spec = pl.BlockSpec((None, CONV_BUF, tb, dc), lambda i: (0, 0, i, 0))
    cat, st, new_conv = pl.pallas_call(
        functools.partial(_mix1_body, tb=tb, dc=dc, hd=hd, nh=nh),
        grid=(nb // tb,),
        in_specs=[row_spec(dc), row_spec(dr), row_spec(dr), row_spec(dr), row_spec(dr), conv_spec, st_spec,
                  full(conv_w), full(conv_b), full(ln_g), full(ln_b), full(rg), full(rb), full(gam)],
        out_specs=[row_spec(dc + dr), st_spec, conv_spec],
        out_shape=[jax.ShapeDtypeStruct((nb, dc + dr), BF16),
                   jax.ShapeDtypeStruct((1, nb, nh, hd, hd), F32),
                   jax.ShapeDtypeStruct((1, CONV_BUF, nb, dc), F32)],
        scratch_shapes=[pltpu.VMEM((tb, dr), F32)] * 4,
        compiler_params=pltpu.CompilerParams(vmem_limit_bytes=VMEM_LIMIT),
        name="mix1",
    )(glu, q, k, v, sg, state_conv, state_ret, conv_w, conv_b, ln_g, ln_b, rg, rb, gam)
    return cat, st, new_conv


def _post_math(cat, x_ref, gtm_ref, scf_ref, shf_ref, g_ref, wo_ref, wrh_ref, wrl_ref, x1_ref, hp_ref, lg_ref):
    d = x_ref.shape[1]
    y = jnp.dot(cat, wo_ref[...], preferred_element_type=F32)
    x1 = x_ref[...] + gtm_ref[...] * y
    x1_ref[...] = x1
    h = _modulated_rmsnorm(x1, g_ref[...], scf_ref[...], shf_ref[...])
    hb = h.astype(BF16)
    _store_rows(hp_ref, _pack_halves(h[:, 0:d // 2], h[:, d // 2:d]))
    hl = (h - hb.astype(F32)).astype(BF16)
    nt = (((1,), (1,)), ((), ()))
    lg_ref[...] = (lax.dot_general(wrh_ref[...], hb, nt, preferred_element_type=F32)
                   + lax.dot_general(wrh_ref[...], hl, nt, preferred_element_type=F32)
                   + lax.dot_general(wrl_ref[...], hb, nt, preferred_element_type=F32))


def _post_body(cat_ref, *refs):
    _post_math(cat_ref[...], *refs)


def _post(cat, x, mod, g_ffn, wo_b, wrh, wrl):
    t, d = x.shape
    tl = min(t, 1024)
    ne = wrh.shape[0]
    mod_spec = lambda j: pl.BlockSpec((tl, d), lambda l: (l, j))
    row_spec = lambda w: pl.BlockSpec((tl, w), lambda l: (l, 0))
    full = lambda a: pl.BlockSpec(a.shape, lambda l: (0,) * a.ndim)
    return pl.pallas_call(
        _post_body,
        grid=(t // tl,),
        in_specs=[row_spec(d), row_spec(d), mod_spec(2), mod_spec(4), mod_spec(3),
                  full(g_ffn), full(wo_b), full(wrh), full(wrl)],
        out_specs=[row_spec(d),
                   pl.BlockSpec((tl * ROW_WORDS, LANES), lambda l: (l, 0)),
                   pl.BlockSpec((ne, tl), lambda l: (0, l))],
        out_shape=[jax.ShapeDtypeStruct((t, d), F32),
                   jax.ShapeDtypeStruct((t * ROW_WORDS, LANES), U32),
                   jax.ShapeDtypeStruct((ne, t), F32)],
        compiler_params=pltpu.CompilerParams(vmem_limit_bytes=VMEM_LIMIT),
        name="post",
    )(cat, x, mod, mod, mod, g_ffn, wo_b, wrh, wrl)


def _first_max(x, idx, sentinel):
    m = jnp.max(x, axis=0, keepdims=True)
    f = jnp.min(jnp.where(x == m, idx, sentinel), axis=0, keepdims=True)
    return m, f


def _route_body(lg_ref, bias_ref, e_ref, w_ref, r_ref, cnt_ref, cnt_scr, before, *, tr, ne, ng, topk, topg):
    @pl.when(pl.program_id(0) == 0)
    def _():
        cnt_scr[...] = jnp.zeros(cnt_scr.shape, F32)
        t_row = lax.broadcasted_iota(I32, (tr, tr), 0)
        t_col = lax.broadcasted_iota(I32, (tr, tr), 1)
        before[...] = (t_row < t_col).astype(BF16)

    per = ne // ng
    neg = -jnp.inf
    scores = _sigmoid(lg_ref[...])
    sel = scores + bias_ref[...]
    sub = lax.broadcasted_iota(I32, (per, tr), 0)
    gs = []
    for g in range(ng):
        s_g = sel[g * per:(g + 1) * per, :]
        m1, f1 = _first_max(s_g, sub, per)
        m2 = jnp.max(jnp.where(sub == f1, neg, s_g), axis=0, keepdims=True)
        gs.append(m1 + m2)
    gsc = jnp.concatenate(gs, axis=0)
    gi = lax.broadcasted_iota(I32, (ng, tr), 0)
    keep = jnp.zeros((ng, tr), F32)
    for _ in range(topg):
        _, f = _first_max(gsc, gi, ng)
        pick = gi == f
        keep = jnp.where(pick, 1.0, keep)
        gsc = jnp.where(pick, neg, gsc)
    work = jnp.concatenate(
        [jnp.where(keep[g:g + 1, :] > 0.5, sel[g * per:(g + 1) * per, :], neg) for g in range(ng)], axis=0)
    ei = lax.broadcasted_iota(I32, (ne, tr), 0)
    picks, es, ws = [], [], []
    for _ in range(topk):
        _, f = _first_max(work, ei, ne)
        pick = ei == f
        picks.append(pick)
        es.append(f)
        ws.append(jnp.sum(jnp.where(pick, scores, 0.0), axis=0, keepdims=True))
        work = jnp.where(pick, neg, work)
    wsum = ws[0]
    for w in ws[1:]:
        wsum = wsum + w
    scale = ROUTED_SCALE / wsum
    chosen = picks[0]
    for p in picks[1:]:
        chosen = jnp.logical_or(chosen, p)
    chosen_f = chosen.astype(F32)
    prior = cnt_scr[:, 0:1] + jnp.dot(chosen_f.astype(BF16), before[...], preferred_element_type=F32)
    rs = [jnp.sum(jnp.where(p, prior, 0.0), axis=0, keepdims=True).astype(I32) for p in picks]
    pad_i = jnp.zeros((SUBLANES - topk, tr), I32)
    pad_f = jnp.zeros((SUBLANES - topk, tr), F32)
    e_ref[...] = jnp.concatenate(es + [pad_i], axis=0)
    w_ref[...] = jnp.concatenate([w * scale for w in ws] + [pad_f], axis=0)
    r_ref[...] = jnp.concatenate(rs + [pad_i], axis=0)
    total = cnt_scr[:, 0:1] + jnp.sum(chosen_f, axis=1, keepdims=True)
    cnt_scr[...] = jnp.broadcast_to(total, cnt_scr.shape)
    cnt_ref[...] = jnp.broadcast_to(total, cnt_ref.shape)


def _token_tile(t, limit):
    return max(m for m in range(LANES, limit + 1, LANES) if t % m == 0)


def _route(logits_t, bias_col):
    ne, t = logits_t.shape
    tr = _token_tile(t, ROUTE_TILE)
    tok = lambda dt: jax.ShapeDtypeStruct((SUBLANES, t), dt)
    tok_spec = pl.BlockSpec((SUBLANES, tr), lambda i: (0, i))
    return pl.pallas_call(
        functools.partial(_route_body, tr=tr, ne=ne, ng=N_GROUPS, topk=TOP_K, topg=TOPK_GROUPS),
        grid=(t // tr,),
        in_specs=[pl.BlockSpec((ne, tr), lambda i: (0, i)), pl.BlockSpec((ne, 1), lambda i: (0, 0))],
        out_specs=[tok_spec, tok_spec, tok_spec, pl.BlockSpec((ne, LANES), lambda i: (0, 0))],
        out_shape=[tok(I32), tok(F32), tok(I32), jax.ShapeDtypeStruct((ne, LANES), F32)],
        scratch_shapes=[pltpu.VMEM((ne, LANES), F32), pltpu.VMEM((tr, tr), BF16)],
        compiler_params=pltpu.CompilerParams(dimension_semantics=("arbitrary",)),
        name="route",
    )(logits_t, bias_col)


def _dest_body(cnt_ref, e_ref, r_ref, d_ref, ps, *, ne, tr):
    shift = EXPERT_ROWS.bit_length() - 1

    @pl.when(pl.program_id(0) == 0)
    def _():
        def step(j, start):
            ps[j] = start
            return start + lax.shift_left(lax.shift_right_logical(cnt_ref[j] + (EXPERT_ROWS - 1), shift), shift)

        lax.fori_loop(0, ne, step, jnp.int32(0))

    e = e_ref[...]
    base = jnp.zeros(e.shape, I32)
    for j in range(ne):
        base = jnp.where(e == j, ps[j], base)
    dest = base + r_ref[...]
    for m in range(tr // TOKEN_TILE):
        d_ref[SUBLANES * m:SUBLANES * (m + 1), :] = dest[:, TOKEN_TILE * m:TOKEN_TILE * (m + 1)]


def _dest_rows(counts, eidx, rank):
    rows, t = eidx.shape
    ne = counts.shape[0]
    tr = _token_tile(t, DEST_TILE)
    spec = pl.BlockSpec((rows, tr), lambda i, cnt: (0, i))
    return pl.pallas_call(
        functools.partial(_dest_body, ne=ne, tr=tr),
        grid_spec=pltpu.PrefetchScalarGridSpec(
            num_scalar_prefetch=1, grid=(t // tr,), in_specs=[spec, spec],
            out_specs=pl.BlockSpec((tr // TOKEN_TILE * rows, TOKEN_TILE), lambda i, cnt: (i, 0)),
            scratch_shapes=[pltpu.SMEM((ne,), I32)]),
        out_shape=jax.ShapeDtypeStruct((t // TOKEN_TILE * rows, TOKEN_TILE), I32),
        compiler_params=pltpu.CompilerParams(dimension_semantics=("arbitrary",)),
        name="dest",
    )(counts, eidx, rank)


def _sc_mesh():
    return plsc.VectorSubcoreMesh(core_axis_name="c", subcore_axis_name="s")


def _sc_worker_id():
    return lax.axis_index("s") * SC_CORES + lax.axis_index("c")


def _index_block(dest_ref, chunk, width):
    per_tile = TOKEN_TILE // width
    return dest_ref.at[chunk // per_tile, :, pl.ds((chunk % per_tile) * width, width)]


def _dispatch(hps, dest3, n_rows):
    w = SCATTER_ROWS
    width = hps[0].shape[1]
    bounds = [0]
    for h in hps:
        bounds.append(bounds[-1] + h.shape[0] // w)
    nch = bounds[-1]
    nsrc = len(hps)

    @functools.partial(
        pl.kernel, mesh=_sc_mesh(),
        out_type=jax.ShapeDtypeStruct((n_rows, width), U32),
        scratch_types=[pltpu.VMEM((2, SUBLANES, w), I32), pltpu.VMEM((2, w, width), U32),
                       pltpu.SemaphoreType.DMA, pltpu.SemaphoreType.DMA, pltpu.SemaphoreType.DMA],
        compiler_params=pltpu.CompilerParams(use_tc_tiling_on_sc=False),
        name="dispatch",
    )
    def run(*refs):
        src_refs, dest_ref, xs_ref = refs[:nsrc], refs[nsrc], refs[nsrc + 1]
        idx_v, rows_v = refs[nsrc + 2:nsrc + 4]
        sem_load = refs[nsrc + 4:nsrc + 6]
        sem_scatter = refs[nsrc + 6]
        wid = _sc_worker_id()

        def start_loads(c, slot):
            pltpu.async_copy(_index_block(dest_ref, c, w), idx_v.at[slot], sem_load[slot])
            for i, src in enumerate(src_refs):
                @pl.when(jnp.logical_and(c >= bounds[i], c < bounds[i + 1]))
                def _(src=src, lo=bounds[i]):
                    pltpu.async_copy(src.at[pl.ds((c - lo) * w, w)], rows_v.at[slot], sem_load[slot])

        def wait_loads(slot):
            pltpu.make_async_copy(_index_block(dest_ref, 0, w), idx_v.at[slot], sem_load[slot]).wait()
            pltpu.make_async_copy(src_refs[0].at[pl.ds(0, w)], rows_v.at[slot], sem_load[slot]).wait()

        @pl.when(wid < nch)
        def _():
            start_loads(wid, 0)

        @pl.loop(0, pl.cdiv(pl.cdiv(nch, SC_WORKERS), 2))
        def _(rr):
            for slot in range(2):
                c = (rr * 2 + slot) * SC_WORKERS + wid

                @pl.when(c < nch)
                def _(c=c, slot=slot):
                    wait_loads(slot)

                    @pl.when(c + SC_WORKERS < nch)
                    def _():
                        start_loads(c + SC_WORKERS, 1 - slot)

                    copies = [pltpu.async_copy(rows_v.at[slot], xs_ref.at[idx_v.at[slot, k]], sem_scatter)
                              for k in range(TOP_K)]
                    for cp in copies:
                        cp.wait()

    return run(*hps, dest3)


def _undispatch(ys, dest3, n_tokens, after=None):
    w = GATHER_ROWS
    width = ys.shape[1]
    nch = n_tokens // w

    @functools.partial(
        pl.kernel, mesh=_sc_mesh(),
        out_type=jax.ShapeDtypeStruct((TOP_K, n_tokens, width), U32),
        scratch_types=[pltpu.VMEM((2, SUBLANES, w), I32), pltpu.VMEM((2, TOP_K, w, width), U32),
                       pltpu.SemaphoreType.DMA, pltpu.SemaphoreType.DMA, pltpu.SemaphoreType.DMA],
        compiler_params=pltpu.CompilerParams(use_tc_tiling_on_sc=False),
        name="undispatch",
    )
    def run(*refs):
        ys_ref, dest_ref = refs[:2]
        z_ref, idx_v, bufs = refs[-6:-3]
        sem_gather = refs[-3:-1]
        sem_store = refs[-1]
        wid = _sc_worker_id()

        def start_gathers(c, slot):
            pltpu.sync_copy(_index_block(dest_ref, c, w), idx_v.at[slot])
            for k in range(TOP_K):
                pltpu.async_copy(ys_ref.at[idx_v.at[slot, k]], bufs.at[slot, k], sem_gather[slot])

        def wait_gathers(slot):
            for k in range(TOP_K):
                pltpu.make_async_copy(ys_ref.at[idx_v.at[slot, k]], bufs.at[slot, k], sem_gather[slot]).wait()

        @pl.when(wid < nch)
        def _():
            start_gathers(wid, 0)

        @pl.loop(0, pl.cdiv(pl.cdiv(nch, SC_WORKERS), 2))
        def _(rr):
            for slot in range(2):
                c = (rr * 2 + slot) * SC_WORKERS + wid

                @pl.when(c < nch)
                def _(c=c, slot=slot):
                    @pl.when(c + SC_WORKERS < nch)
                    def _():
                        start_gathers(c + SC_WORKERS, 1 - slot)

                    wait_gathers(slot)
                    stores = [pltpu.async_copy(bufs.at[slot, k], z_ref.at[k, pl.ds(c * w, w)], sem_store)
                              for k in range(TOP_K)]
                    for cp in stores:
                        cp.wait()

    return run(ys, dest3) if after is None else run(ys, dest3, after)


PART_SHIFT = 24


def _expert_body(*refs, tm, ne, nparts):
    cnt_refs = refs[:nparts]
    xs_refs = refs[nparts:2 * nparts]
    w1_ref, w3_ref, w2_ref = refs[2 * nparts:2 * nparts + 3]
    ys_refs = refs[2 * nparts + 3:3 * nparts + 3]
    (w1f, w3f, w2f, w1s, w3s, w2s, xbuf, ybuf, xlo, xhi, sched, sem_x, sem_y, sem_w) = refs[3 * nparts + 3:]
    blk_words = tm * ROW_WORDS
    half = ROW_WORDS * LANES
    shift = tm.bit_length() - 1

    def n_blocks_of(p, e):
        return lax.shift_right_logical(cnt_refs[p][e] + (tm - 1), shift)

    def n_all(e):
        n = n_blocks_of(0, e)
        for p in range(1, nparts):
            n = n + n_blocks_of(p, e)
        return n

    def next_nonempty(e):
        return lax.while_loop(
            lambda c: jnp.logical_and(c < ne, n_all(jnp.minimum(c, ne - 1)) == 0), lambda c: c + 1, e)

    def plan(e, carry):
        i, starts = carry[0], list(carry[1:])
        for p in range(nparts):
            n = n_blocks_of(p, e)

            def put(j, c, p=p, i=i, start=starts[p]):
                sched[i + j] = (start + j) + (p << PART_SHIFT)
                return c

            lax.fori_loop(0, n, put, 0)
            i = i + n
            starts[p] = starts[p] + n
        return (i, *starts)

    nu = lax.fori_loop(0, ne, plan, (jnp.int32(0),) * (nparts + 1))[0]

    def rows_of(code):
        blk = code & ((1 << PART_SHIFT) - 1)
        return pl.ds(pl.multiple_of(blk * blk_words, blk_words), blk_words)

    def x_start(i, slot):
        code = sched[i]
        for p in range(nparts):
            @pl.when(lax.shift_right_logical(code, PART_SHIFT) == p)
            def _(p=p):
                pltpu.make_async_copy(xs_refs[p].at[rows_of(code), :], xbuf.at[slot], sem_x.at[slot]).start()

    def x_wait(slot):
        pltpu.make_async_copy(xs_refs[0].at[pl.ds(0, blk_words), :], xbuf.at[slot], sem_x.at[slot]).wait()

    def y_start(i, slot):
        code = sched[i]
        for p in range(nparts):
            @pl.when(lax.shift_right_logical(code, PART_SHIFT) == p)
            def _(p=p):
                pltpu.make_async_copy(ybuf.at[slot], ys_refs[p].at[rows_of(code), :], sem_y.at[slot]).start()

    def y_wait(slot):
        pltpu.make_async_copy(ybuf.at[slot], ys_refs[0].at[pl.ds(0, blk_words), :], sem_y.at[slot]).wait()

    def w_copies(e, ws):
        return [pltpu.make_async_copy(src.at[e], dst.at[ws], sem_w.at[ws])
                for src, dst in ((w1_ref, w1f), (w3_ref, w3f), (w2_ref, w2f))]

    for q in range(X_AHEAD):
        @pl.when(q < nu)
        def _(q=q):
            x_start(q, q)

    e_first = next_nonempty(jnp.int32(0))

    @pl.when(e_first < ne)
    def _():
        for cp in w_copies(e_first, 0):
            cp.start()

    def blocks(i, nblk):
        for b in range(nblk):
            x_wait((i + b) % X_SLOTS)
        for b in range(nblk):
            nxt = i + b + X_AHEAD

            @pl.when(nxt < nu)
            def _(nxt=nxt):
                x_start(nxt, nxt % X_SLOTS)

        for b in range(nblk):
            @pl.when(i + b >= Y_SLOTS)
            def _(b=b):
                y_wait((i + b) % Y_SLOTS)

        for b in range(nblk):
            xin = xbuf.at[(i + b) % X_SLOTS]
            for w in range(ROW_WORDS):
                lo, hi = _unpack_halves(_load_row_word(xin, w, tm))
                xlo[b, :, LANES * w:LANES * (w + 1)] = lo.astype(BF16)
                xhi[b, :, LANES * w:LANES * (w + 1)] = hi.astype(BF16)

        def up(b, wsc):
            return (jnp.dot(xlo[b], wsc[0:half, :], preferred_element_type=F32)
                    + jnp.dot(xhi[b], wsc[half:2 * half, :], preferred_element_type=F32))

        for b in range(nblk):
            hid = (_silu(up(b, w1s)) * up(b, w3s)).astype(BF16)
            y = jnp.dot(hid, w2s[...], preferred_element_type=F32)
            _store_rows(ybuf.at[(i + b) % Y_SLOTS], _pack_halves(y[:, 0:half], y[:, half:2 * half]))
        for b in range(nblk):
            y_start(i + b, (i + b) % Y_SLOTS)

    def per_expert(e, carry):
        i0, ws = carry
        n = n_all(e)

        @pl.when(n > 0)
        def _():
            for cp in w_copies(e, ws):
                cp.wait()
            w1s[...] = w1f[ws].astype(BF16)
            w3s[...] = w3f[ws].astype(BF16)
            w2s[...] = w2f[ws].astype(BF16)
            e_next = next_nonempty(e + 1)

            @pl.when(e_next < ne)
            def _():
                for cp in w_copies(e_next, 1 - ws):
                    cp.start()

            def pair(j, c):
                blocks(i0 + 2 * j, 2)
                return c

            lax.fori_loop(0, lax.shift_right_logical(n, 1), pair, 0)

            @pl.when(n % 2 == 1)
            def _():
                blocks(i0 + n - 1, 1)

        return i0 + n, jnp.where(n > 0, 1 - ws, ws)

    lax.fori_loop(0, ne, per_expert, (jnp.int32(0), jnp.int32(0)))

    for q in range(Y_SLOTS):
        @pl.when(nu > q)
        def _(q=q):
            y_wait((nu - 1 - q) % Y_SLOTS)


def _experts(counts, xss, w1e, w3e, w2e):
    tm = EXPERT_ROWS
    ne, d, de = w1e.shape
    half = ROW_WORDS * LANES
    nparts = len(xss)
    cap = sum(x.shape[0] // (tm * ROW_WORDS) for x in xss)
    assert cap < (1 << PART_SHIFT)
    anyspec = pl.BlockSpec(memory_space=pl.ANY)
    blk_buf = lambda n: pltpu.VMEM((n, tm * ROW_WORDS, LANES), U32)
    return pl.pallas_call(
        functools.partial(_expert_body, tm=tm, ne=ne, nparts=nparts),
        grid_spec=pltpu.PrefetchScalarGridSpec(
            num_scalar_prefetch=nparts,
            grid=(1,),
            in_specs=[anyspec] * (nparts + 3),
            out_specs=[anyspec] * nparts,
            scratch_shapes=[pltpu.VMEM((2, d, de), F32), pltpu.VMEM((2, d, de), F32), pltpu.VMEM((2, de, d), F32),
                            pltpu.VMEM((d, de), BF16), pltpu.VMEM((d, de), BF16), pltpu.VMEM((de, d), BF16),
                            blk_buf(X_SLOTS), blk_buf(Y_SLOTS),
                            pltpu.VMEM((2, tm, half), BF16), pltpu.VMEM((2, tm, half), BF16),
                            pltpu.SMEM((cap,), I32),
                            pltpu.SemaphoreType.DMA((X_SLOTS,)), pltpu.SemaphoreType.DMA((Y_SLOTS,)),
                            pltpu.SemaphoreType.DMA((2,))]),
        out_shape=[jax.ShapeDtypeStruct(x.shape, U32) for x in xss],
        compiler_params=pltpu.CompilerParams(dimension_semantics=("arbitrary",), vmem_limit_bytes=VMEM_LIMIT),
        name="experts",
    )(*counts, *xss, w1e, w3e, w2e)


def _combine_body(z_ref, x1_ref, hp_ref, gtf_ref, wt_ref, gfin_ref, ws1_ref, ws3_ref, ws2_ref, y_ref, xo, hlo, hhi,
                  *, td, topk):
    half = ROW_WORDS * LANES
    for w in range(ROW_WORDS):
        lo, hi = _unpack_halves(_load_row_word(hp_ref, w, td))
        hlo[:, LANES * w:LANES * (w + 1)] = lo.astype(BF16)
        hhi[:, LANES * w:LANES * (w + 1)] = hi.astype(BF16)

    def up(w_ref):
        return (jnp.dot(hlo[...], w_ref[0:half, :], preferred_element_type=F32)
                + jnp.dot(hhi[...], w_ref[half:2 * half, :], preferred_element_type=F32))

    xo[...] = jnp.dot((_silu(up(ws1_ref)) * up(ws3_ref)).astype(BF16), ws2_ref[...], preferred_element_type=F32)
    wt = wt_ref[...].T
    ws = [wt[:, k:k + 1] for k in range(topk)]
    sq = jnp.zeros((td, 1), F32)
    for j in range(ROW_WORDS):
        acc_lo = jnp.zeros((td, LANES), F32)
        acc_hi = jnp.zeros((td, LANES), F32)
        for k in range(topk):
            lo, hi = _unpack_halves(_load_row_word(z_ref.at[k], j, td))
            acc_lo = acc_lo + ws[k] * lo
            acc_hi = acc_hi + ws[k] * hi
        for base, acc in ((0, acc_lo), (half, acc_hi)):
            cols = slice(base + LANES * j, base + LANES * (j + 1))
            x = x1_ref[:, cols] + gtf_ref[:, cols] * (acc + xo[:, cols])
            xo[:, cols] = x
            sq = sq + jnp.sum(x * x, axis=-1, keepdims=True)
    rs = lax.rsqrt(sq / (2 * half) + EPS)
    y_ref[...] = xo[...] * rs * gfin_ref[...]


def _combine_body_into(*refs, td, topk):
    _combine_body(*refs[:9], *refs[10:], td=td, topk=topk)


def _combine(z, token0, wts, x1, hp, mod, g_final, shared_w, *, rows_per_mod, per_row_mod, b0=0, out_rows=None,
             into=None):
    t, d = x1.shape
    td = min(t, COMBINE_TILE)
    tile0 = token0 // td
    out_rows = t if out_rows is None else out_rows
    if per_row_mod:
        gtf_spec = pl.BlockSpec((td, d), lambda i: (i, 5))
        out0 = 0
    else:
        tiles_per_mod = rows_per_mod // td
        gtf_spec = pl.BlockSpec((None, None, 1, d), lambda i: (i // tiles_per_mod + b0, 5, 0, 0))
        out0 = b0 * tiles_per_mod
    full = lambda a: pl.BlockSpec(a.shape, lambda i: (0,) * a.ndim)
    in_specs = [pl.BlockSpec((TOP_K, td * ROW_WORDS, LANES), lambda i: (0, i + tile0, 0)),
                pl.BlockSpec((td, d), lambda i: (i, 0)),
                pl.BlockSpec((td * ROW_WORDS, LANES), lambda i: (i, 0)),
                gtf_spec,
                pl.BlockSpec((SUBLANES, td), lambda i: (0, i + tile0)),
                pl.BlockSpec((1, d), lambda i: (0, 0))] + [full(w) for w in shared_w]
    args = [z, x1, hp, mod, wts, g_final, *shared_w]
    body, aliases = _combine_body, {}
    if into is not None:
        in_specs.append(pl.BlockSpec(memory_space=pl.ANY))
        args.append(into)
        body, aliases = _combine_body_into, {len(args) - 1: 0}
    return pl.pallas_call(
        functools.partial(body, td=td, topk=TOP_K),
        grid=(t // td,),
        in_specs=in_specs,
        out_specs=pl.BlockSpec((td, d), lambda i: (i + out0, 0)),
        scratch_shapes=[pltpu.VMEM((td, d), F32), pltpu.VMEM((td, d // 2), BF16), pltpu.VMEM((td, d // 2), BF16)],
        out_shape=jax.ShapeDtypeStruct((out_rows, d), F32),
        input_output_aliases=aliases,
        compiler_params=pltpu.CompilerParams(vmem_limit_bytes=VMEM_LIMIT),
        name="combine",
    )(*args)


def _log_gamma(nh):
    return np.log(1.0 - 2.0 ** (-5.0 - np.arange(nh, dtype=np.float32))).astype(np.float32)


def _retention_tables(length, nh, hd):
    c = math.gcd(length, RET_CHUNK)
    log_g = _log_gamma(nh)
    idx = np.arange(c, dtype=np.float32)
    rel = idx[:, None] - idx[None, :]
    mask = np.where(rel >= 0, np.exp(log_g[:, None, None] * np.maximum(rel, 0.0)), 0.0).astype(np.float32)
    q_decay = np.exp(log_g[None, :] * (idx[:, None] + 1.0)).astype(np.float32)
    k_decay = np.exp(log_g[None, :] * (c - 1.0 - idx[:, None])).astype(np.float32)
    chunk_decay = np.exp(log_g * np.float32(c)).astype(np.float32)
    qd = np.broadcast_to(q_decay.T[:, :, None], (nh, c, hd))
    kd = np.broadcast_to(k_decay.T[:, :, None], (nh, c, hd))
    cd = np.broadcast_to(chunk_decay[:, None, None], (nh, hd, hd))
    return tuple(jnp.asarray(t) for t in (mask, qd, kd, cd))


def kernel(x_prompt, x_sample, c_prompt, c_sample, state_conv, state_ret, w_ada, b_ada, g_mix, g_ffn, w_in,
           conv_w, conv_b, conv_norm_g, conv_norm_b, ret_norm_g, ret_norm_b, w_out, w_router, router_bias,
           w1, w3, w2, ws1, ws3, ws2, g_final):
    depth = w_ada.shape[0]
    assert depth == 1, "single-layer trunk"
    bp, lp, d = x_prompt.shape
    bs, ls, _ = x_sample.shape
    assert ls == 1
    dc = conv_w.shape[2]
    dr = ret_norm_g.shape[1]
    nh = RET_HEADS
    hd = dr // nh
    assert hd == LANES and lp % 256 == 0 and bs % TOKEN_TILE == 0 and d // 2 == ROW_WORDS * LANES
    ne = w_router.shape[2]
    row = lambda a: a.reshape(1, -1)

    mod_p, mod_s = _ada([c_prompt, c_sample], w_ada[0], row(b_ada[0]))
    mod_p = mod_p.reshape(bp, 6, 1, d)

    half = hd // 2
    inv = (np.float32(ROPE_BASE) ** (-np.arange(half, dtype=np.float32) / np.float32(half))).astype(np.float32)
    inv2 = np.concatenate([inv, inv]).reshape(1, hd)
    cos_p, sin_p = _rope_tables(inv2, lp, 0)
    cos_s, sin_s = _rope_tables(inv2, SUBLANES, PAST_LEN)

    w_in_b = w_in[0].astype(BF16)
    wo_b = w_out[0].astype(BF16)
    wr_t = w_router[0].T
    wrh = wr_t.astype(BF16)
    wrl = (wr_t - wrh.astype(F32)).astype(BF16)
    ws1_b, ws3_b, ws2_b = ws1[0].astype(BF16), ws3[0].astype(BF16), ws2[0].astype(BF16)
    dims = dict(dc=dc, dr=dr, hd=hd)

    tables = _retention_tables(lp, nh, hd)
    gam = jnp.asarray(np.broadcast_to(np.exp(_log_gamma(nh))[:, None, None], (nh, SUBLANES, hd)))
    norm_rows = (row(conv_b[0]), row(conv_norm_g[0]), row(conv_norm_b[0]), row(ret_norm_g[0]), row(ret_norm_b[0]))
    post_w = (row(g_ffn[0]), wo_b, wrh, wrl)
    shared_w = (ws1_b, ws3_b, ws2_b)
    bias_col = router_bias[0].reshape(ne, 1)
    hw = d // 2
    tm = EXPERT_ROWS

    def pre_prompt(b0, nb, after=None):
        glu, q, k, v, sg = _proj(x_prompt, mod_p, row(g_mix[0]), w_in_b, cos_p, sin_p,
                                 per_row_mod=False, b0=b0, nb=nb, after=after, **dims)
        x2, hp, lg, ret = _mix(glu, q, k, v, sg, conv_w[0], *norm_rows, tables, x_prompt, mod_p, *post_w,
                               b0=b0, nb=nb, **dims)
        return glu, ret, x2, hp, lg

    def pre_sample():
        xs3 = x_sample.reshape(1, bs, d)
        glu, q, k, v, sg = _proj(xs3, mod_s, row(g_mix[0]), w_in_b, cos_s, sin_s, per_row_mod=True, **dims)
        cat, ret, new_conv = _mix1(glu, q, k, v, sg, state_conv.transpose(0, 2, 1, 3), state_ret, conv_w[0],
                                   *norm_rows, gam, **dims)
        x2, hp, lg = _post(cat, x_sample.reshape(bs, d), mod_s, *post_w)
        return new_conv.transpose(0, 2, 1, 3), ret, x2, hp, lg

    def route_and_dispatch(hps, lgs):
        lg = lgs[0] if len(lgs) == 1 else jnp.concatenate(lgs, axis=1)
        tokens = lg.shape[1]
        eidx, wts, rank, cnt = _route(lg, bias_col)
        counts = cnt[:, 0].astype(I32)
        n_rows = -(-(tokens * TOP_K + ne * (tm - 1)) // tm) * tm
        dest3 = _dest_rows(counts, eidx, rank).reshape(tokens // TOKEN_TILE, SUBLANES, TOKEN_TILE)
        xs = _dispatch([h.reshape(-1, hw) for h in hps], dest3, n_rows)
        return counts, dest3, wts, xs.reshape(n_rows * ROW_WORDS, LANES)

    def undispatch(ys, dest3, after=None):
        tokens = dest3.shape[0] * TOKEN_TILE
        z = _undispatch(ys.reshape(-1, hw), dest3, tokens, after)
        return z.reshape(TOP_K, tokens * ROW_WORDS, LANES)

    nb0 = bp // 2
    nb1 = bp - nb0
    new_conv_s, ret_s, x2_s, hp_s, lg_s = pre_sample()
    glu_0, ret_0, x2_0, hp_0, lg_0 = pre_prompt(0, nb0, after=lg_s)
    counts_0, dest_0, wts_0, xs_0 = route_and_dispatch([hp_0], [lg_0])
    glu_1, ret_1, x2_1, hp_1, lg_1 = pre_prompt(nb0, nb1, after=dest_0)
    counts_1, dest_1, wts_1, xs_1 = route_and_dispatch([hp_1, hp_s], [lg_1, lg_s])
    ys_0, ys_1 = _experts((counts_0, counts_1), (xs_0, xs_1), w1[0], w3[0], w2[0])
    z_0 = undispatch(ys_0, dest_0)
    z_1 = undispatch(ys_1, dest_1, after=z_0[0, :SUBLANES])
    y_p = _combine(z_0, 0, wts_0, x2_0, hp_0, mod_p, row(g_final), shared_w, rows_per_mod=lp, per_row_mod=False,
                   out_rows=bp * lp)
    y_p = _combine(z_1, 0, wts_1, x2_1, hp_1, mod_p, row(g_final), shared_w, rows_per_mod=lp, per_row_mod=False,
                   b0=nb0, out_rows=bp * lp, into=y_p)
    y_s = _combine(z_1, nb1 * lp, wts_1, x2_s, hp_s, mod_s, row(g_final), shared_w, rows_per_mod=bs,
                   per_row_mod=True)
    ret_p = jnp.concatenate([ret_0, ret_1], axis=0)

    tail = lambda g, n: g.reshape(n, lp, dc)[:, lp - CONV_BUF:, :]
    new_conv_p = jnp.concatenate([tail(glu_0, nb0), tail(glu_1, nb1)], axis=0)
    return (y_p.reshape(bp, lp, d), y_s.reshape(bs, ls, d), new_conv_p[None], ret_p[None],
            new_conv_s, ret_s)
```

```python
import functools
import math

import jax
import jax.numpy as jnp
import numpy as np
from jax import lax
from jax.experimental import pallas as pl
from jax.experimental.pallas import tpu as pltpu
from jax.experimental.pallas import tpu_sc as plsc

F32 = jnp.float32
BF16 = jnp.bfloat16
U32 = jnp.uint32
I32 = jnp.int32

EPS = 1e-6
PAST_LEN = 16384
RET_HEADS = 4
RET_CHUNK = 128
CONV_WIDTH = 31
CONV_BUF = CONV_WIDTH - 1
ROPE_BASE = 10000.0
TOP_K = 6
N_GROUPS = 8
TOPK_GROUPS = 4
ROUTED_SCALE = 2.5

LANES = 128
SUBLANES = 8
CONV_PAD = 32
EXPERT_ROWS = 256
ROUTE_TILE = 1024
DEST_TILE = 4096
TOKEN_TILE = 128
COMBINE_TILE = 512
VMEM_LIMIT = 56 * 1024 * 1024
SC_CORES = 2
SC_SUBCORES = 16
SC_WORKERS = SC_CORES * SC_SUBCORES
SCATTER_ROWS = 64
GATHER_ROWS = 16
ROW_WORDS = 4
X_SLOTS = 6
X_AHEAD = 4
Y_SLOTS = 4

HI_MASK = 0xFFFF0000


def _sigmoid(x):
    return jax.nn.sigmoid(x)


def _silu(x):
    return x * jax.nn.sigmoid(x)


def _pack_halves(lo, hi):
    lo_u = lax.bitcast_convert_type(lo.astype(BF16).astype(F32), U32) >> 16
    hi_u = lax.bitcast_convert_type(hi.astype(BF16).astype(F32), U32) & jnp.uint32(HI_MASK)
    return hi_u | lo_u


def _unpack_halves(p):
    lo = lax.bitcast_convert_type(p << 16, F32)
    hi = lax.bitcast_convert_type(p & jnp.uint32(HI_MASK), F32)
    return lo, hi


def _store_rows(ref, x):
    rows = x.shape[0]
    for j in range(ROW_WORDS):
        ref[pl.ds(j, rows, stride=ROW_WORDS), :] = x[:, LANES * j:LANES * (j + 1)]


def _load_row_word(ref, j, rows):
    return ref[pl.ds(j, rows, stride=ROW_WORDS), :]


def _ada_body(w_ref, b_ref, *refs):
    w = w_ref[...].astype(BF16)
    groups = len(refs) // 2
    for c_ref, o_ref in zip(refs[:groups], refs[groups:]):
        o_ref[...] = jnp.dot(_silu(c_ref[...]).astype(BF16), w, preferred_element_type=F32) + b_ref[...]


def _ada(cs, w_ada, b_ada):
    d, n = w_ada.shape
    tn = 2048
    return pl.pallas_call(
        _ada_body,
        grid=(n // tn,),
        in_specs=[pl.BlockSpec((d, tn), lambda j: (0, j)), pl.BlockSpec((1, tn), lambda j: (0, j))]
        + [pl.BlockSpec(c.shape, lambda j: (0, 0)) for c in cs],
        out_specs=[pl.BlockSpec((c.shape[0], tn), lambda j: (0, j)) for c in cs],
        out_shape=[jax.ShapeDtypeStruct((c.shape[0], n), F32) for c in cs],
        compiler_params=pltpu.CompilerParams(vmem_limit_bytes=VMEM_LIMIT),
        name="ada",
    )(w_ada, b_ada, *cs)


def _rope_tables(inv2, rows, pos0):
    pos = np.arange(rows, dtype=np.float32) + np.float32(pos0)
    ang = (pos[:, None] * inv2).astype(np.float64)
    sign = np.where(np.arange(inv2.shape[1]) < inv2.shape[1] // 2, -1.0, 1.0)
    return jnp.asarray(np.cos(ang).astype(np.float32)), jnp.asarray((np.sin(ang) * sign).astype(np.float32))


def _modulated_rmsnorm(x, g, sc, sh):
    ms = jnp.mean(x * x, axis=-1, keepdims=True)
    h = x * lax.rsqrt(ms + EPS) * g
    return h * (1.0 + sc) + sh


def _proj_body(x_ref, sh_ref, sc_ref, g_ref, w_ref, cos_ref, sin_ref, *rest, dc, dr, hd, rope_rows):
    glu_ref, q_ref, k_ref, v_ref, sg_ref = rest[-5:]
    hb = _modulated_rmsnorm(x_ref[...], g_ref[...], sc_ref[...], sh_ref[...]).astype(BF16)

    def proj(lo, n):
        return jnp.dot(hb, w_ref[:, lo:lo + n], preferred_element_type=F32)

    glu_ref[...] = proj(0, dc) * _sigmoid(proj(dc, dc))
    cos = cos_ref[...] if rope_rows else cos_ref[0:1, :]
    sin = sin_ref[...] if rope_rows else sin_ref[0:1, :]
    for ref, lo, scale in ((q_ref, 2 * dc, hd ** -0.5), (k_ref, 2 * dc + dr, None)):
        t = proj(lo, dr)
        for hh in range(dr // hd):
            th = t[:, hh * hd:(hh + 1) * hd]
            r = th * cos + pltpu.roll(th, hd // 2, 1) * sin
            if scale is not None:
                r = r * scale
            ref[:, hh * hd:(hh + 1) * hd] = r.astype(BF16)
    v_ref[...] = proj(2 * dc + 2 * dr, dr).astype(BF16)
    sg_ref[...] = _silu(proj(2 * dc + 3 * dr, dr)).astype(BF16)


def _proj(x, mod, g_mix, w_in_b, cos2, sin2, *, dc, dr, hd, per_row_mod, b0=0, nb=None, after=None):
    nb_all, length, d = x.shape
    nb = nb_all if nb is None else nb
    tl = min(length, 512)
    x2 = x.reshape(nb_all * length, d)
    nl = length // tl
    if per_row_mod:
        mod_spec = lambda j: pl.BlockSpec((tl, d), lambda b, l: (l, j))
        rope_spec = pl.BlockSpec((SUBLANES, LANES), lambda b, l: (0, 0))
    else:
        mod_spec = lambda j: pl.BlockSpec((None, None, 1, d), lambda b, l: (b + b0, j, 0, 0))
        rope_spec = pl.BlockSpec((tl, LANES), lambda b, l: (l, 0))
    row_spec = lambda w: pl.BlockSpec((tl, w), lambda b, l: (b * nl + l, 0))
    t = nb * length
    outs = pl.pallas_call(
        functools.partial(_proj_body, dc=dc, dr=dr, hd=hd, rope_rows=not per_row_mod),
        grid=(nb, nl),
        in_specs=[
            pl.BlockSpec((tl, d), lambda b, l: ((b + b0) * nl + l, 0)), mod_spec(0), mod_spec(1),
            pl.BlockSpec((1, d), lambda b, l: (0, 0)),
            pl.BlockSpec(w_in_b.shape, lambda b, l: (0, 0)),
            rope_spec, rope_spec,
        ] + ([] if after is None else [pl.BlockSpec(memory_space=pl.ANY)]),
        out_specs=[row_spec(dc), row_spec(dr), row_spec(dr), row_spec(dr), row_spec(dr)],
        out_shape=[
            jax.ShapeDtypeStruct((t, dc), F32),
            jax.ShapeDtypeStruct((t, dr), BF16),
            jax.ShapeDtypeStruct((t, dr), BF16),
            jax.ShapeDtypeStruct((t, dr), BF16),
            jax.ShapeDtypeStruct((t, dr), BF16),
        ],
        compiler_params=pltpu.CompilerParams(vmem_limit_bytes=VMEM_LIMIT),
        name="proj",
    )(x2, mod, mod, g_mix, w_in_b, cos2, sin2, *(() if after is None else (after,)))
    return outs


def _layernorm_silu(c, g, b):
    mu = jnp.mean(c, axis=-1, keepdims=True)
    d = c - mu
    var = jnp.mean(d * d, axis=-1, keepdims=True)
    return _silu(d * lax.rsqrt(var + EPS) * g + b)


def _groupnorm(o, g, b):
    mu = jnp.mean(o, axis=-1, keepdims=True)
    d = o - mu
    var = jnp.mean(d * d, axis=-1, keepdims=True)
    return d * lax.rsqrt(var + EPS) * g + b


def _mix_body(glu_ref, q_ref, k_ref, v_ref, sg_ref, cw_ref, cb_ref, lng_ref, lnb_ref, rg_ref, rb_ref,
              mask_ref, qd_ref, kd_ref, cd_ref, cat_ref, st_ref, buf, cscr, *, tl, dc, hd, nh, chunk):
    nslab = dc // LANES

    @pl.when(pl.program_id(1) == 0)
    def _():
        buf[:, 0:CONV_PAD, :] = jnp.zeros((nslab, CONV_PAD, LANES), F32)
        st_ref[...] = jnp.zeros(st_ref.shape, F32)

    for j in range(nslab):
        buf[j, CONV_PAD:CONV_PAD + tl, :] = glu_ref[:, LANES * j:LANES * (j + 1)]
    first = CONV_PAD - CONV_BUF
    rows_per_iter = 8 * SUBLANES
    for j in range(nslab):
        cols = slice(LANES * j, LANES * (j + 1))
        wv = [jnp.broadcast_to(cw_ref[t:t + 1, cols], (SUBLANES, LANES)) for t in range(CONV_WIDTH)]
        bias = jnp.broadcast_to(cb_ref[0:1, cols], (SUBLANES, LANES))

        def body(r, carry, j=j, cols=cols, wv=wv, bias=bias):
            base = pl.multiple_of(r * rows_per_iter, rows_per_iter)
            for u in range(rows_per_iter // SUBLANES):
                acc = bias
                for t in range(CONV_WIDTH):
                    acc = acc + wv[t] * buf[j, pl.ds(base + (u * SUBLANES + first + t), SUBLANES), :]
                cscr[pl.ds(base + u * SUBLANES, SUBLANES), cols] = acc
            return carry

        lax.fori_loop(0, tl // rows_per_iter, body, 0)
    for j in range(nslab):
        buf[j, 0:CONV_PAD, :] = buf[j, tl:tl + CONV_PAD, :]
    cat_ref[:, 0:dc] = _layernorm_silu(cscr[...], lng_ref[...], lnb_ref[...]).astype(BF16)

    nt = (((1,), (1,)), ((), ()))
    tn = (((0,), (0,)), ((), ()))
    for c in range(tl // chunk):
        rows = slice(c * chunk, (c + 1) * chunk)
        for hh in range(nh):
            cols = slice(hh * hd, (hh + 1) * hd)
            qh = q_ref[rows, cols]
            kh = k_ref[rows, cols]
            vh = v_ref[rows, cols]
            s = st_ref[0, hh]
            scores = lax.dot_general(qh, kh, nt, preferred_element_type=F32) * mask_ref[hh]
            inner = jnp.dot(scores.astype(BF16), vh, preferred_element_type=F32)
            qd = (qh.astype(F32) * qd_ref[hh]).astype(BF16)
            cross = jnp.dot(qd, s.astype(BF16), preferred_element_type=F32)
            kd = (kh.astype(F32) * kd_ref[hh]).astype(BF16)
            st_ref[0, hh] = cd_ref[hh] * s + lax.dot_general(kd, vh, tn, preferred_element_type=F32)
            o = _groupnorm(inner + cross, rg_ref[0:1, cols], rb_ref[0:1, cols])
            cat_ref[rows, dc + hh * hd:dc + (hh + 1) * hd] = (o * sg_ref[rows, cols].astype(F32)).astype(BF16)


def _mix(glu, q, k, v, sg, conv_w, conv_b, ln_g, ln_b, rg, rb, tables, *, nb, length, dc, dr, hd):
    nh = dr // hd
    chunk = math.gcd(length, RET_CHUNK)
    tl = min(length, 256)
    nl = length // tl
    mask, qd, kd, cd = tables
    row_spec = lambda w: pl.BlockSpec((tl, w), lambda b, l: (b * nl + l, 0))
    full = lambda a: pl.BlockSpec(a.shape, lambda b, l: (0,) * a.ndim)
    cat, st = pl.pallas_call(
        functools.partial(_mix_body, tl=tl, dc=dc, hd=hd, nh=nh, chunk=chunk),
        grid=(nb, nl),
        in_specs=[row_spec(dc), row_spec(dr), row_spec(dr), row_spec(dr), row_spec(dr),
                  full(conv_w), full(conv_b), full(ln_g), full(ln_b), full(rg), full(rb),
                  full(mask), full(qd), full(kd), full(cd)],
        out_specs=[row_spec(dc + dr), pl.BlockSpec((1, nh, hd, hd), lambda b, l: (b, 0, 0, 0))],
        out_shape=[jax.ShapeDtypeStruct((nb * length, dc + dr), BF16),
                   jax.ShapeDtypeStruct((nb, nh, hd, hd), F32)],
        scratch_shapes=[pltpu.VMEM((dc // LANES, tl + CONV_PAD, LANES), F32),
                        pltpu.VMEM((tl, dc), F32)],
        compiler_params=pltpu.CompilerParams(dimension_semantics=("arbitrary", "arbitrary")),
        name="mix",
    )(glu, q, k, v, sg, conv_w, conv_b, ln_g, ln_b, rg, rb, mask, qd, kd, cd)
    return cat, st


def _mix1_body(glu_ref, q_ref, k_ref, v_ref, sg_ref, sc_ref, s0_ref, cw_ref, cb_ref, lng_ref, lnb_ref,
               rg_ref, rb_ref, gam_ref, cat_ref, st_ref, nc_ref, oscr, qf, kf, vf, *, tb, dc, hd, nh):
    glu = glu_ref[...]
    conv = glu * cw_ref[CONV_BUF:CONV_WIDTH, :] + cb_ref[...]
    for r in range(CONV_BUF):
        past = sc_ref[r]
        conv = conv + past * cw_ref[r:r + 1, :]
        if r > 0:
            nc_ref[r - 1] = past
    nc_ref[CONV_BUF - 1] = glu
    cat_ref[:, 0:dc] = _layernorm_silu(conv, lng_ref[...], lnb_ref[...]).astype(BF16)

    tn = (((0,), (0,)), ((), ()))
    rowid = lax.broadcasted_iota(I32, (tb, hd), 0)
    qf[...] = q_ref[...].astype(F32)
    kf[...] = k_ref[...].astype(F32)
    vf[...] = v_ref[...].astype(F32)
    for hh in range(nh):
        cols = slice(hh * hd, (hh + 1) * hd)
        qa = q_ref[:, cols]
        ka = k_ref[:, cols]
        gam = gam_ref[hh, 0:1, :]
        for bb in range(tb):
            onehot = (rowid == bb).astype(BF16)
            qcol = lax.dot_general(qa, onehot, tn, preferred_element_type=F32)
            kcol = lax.dot_general(ka, onehot, tn, preferred_element_type=F32)
            s0 = s0_ref[bb, hh]
            qrow = qf[bb:bb + 1, cols]
            krow = kf[bb:bb + 1, cols]
            vrow = vf[bb:bb + 1, cols]
            qk = jnp.sum(qrow * krow, axis=-1, keepdims=True)
            cross = gam * jnp.sum(qcol * s0, axis=0, keepdims=True)
            st_ref[bb, hh] = gam * s0 + kcol * vrow
            oscr[bb:bb + 1, cols] = qk * vrow + cross
    for hh in range(nh):
        cols = slice(hh * hd, (hh + 1) * hd)
        o = _groupnorm(oscr[:, cols], rg_ref[0:1, cols], rb_ref[0:1, cols])
        cat_ref[:, dc + hh * hd:dc + (hh + 1) * hd] = (o * sg_ref[:, cols].astype(F32)).astype(BF16)


def _mix1(glu, q, k, v, sg, state_conv, state_ret, conv_w, conv_b, ln_g, ln_b, rg, rb, gam, *, dc, dr, hd):
    nb = glu.shape[0]
    nh = dr // hd
    tb = 16
    row_spec = lambda w: pl.BlockSpec((tb, w), lambda i: (i, 0))
    full = lambda a: pl.BlockSpec(a.shape, lambda i: (0,) * a.ndim)
    st_spec = pl.BlockSpec((None, tb, nh, hd, hd), lambda i: (0, i, 0, 0, 0))
    conv_spec = pl.BlockSpec((None, CONV_BUF, tb, dc), lambda i: (0, 0, i, 0))
    cat, st, new_conv = pl.pallas_call(
        functools.partial(_mix1_body, tb=tb, dc=dc, hd=hd, nh=nh),
        grid=(nb // tb,),
        in_specs=[row_spec(dc), row_spec(dr), row_spec(dr), row_spec(dr), row_spec(dr), conv_spec, st_spec,
                  full(conv_w), full(conv_b), full(ln_g), full(ln_b), full(rg), full(rb), full(gam)],
        out_specs=[row_spec(dc + dr), st_spec, conv_spec],
        out_shape=[jax.ShapeDtypeStruct((nb, dc + dr), BF16),
                   jax.ShapeDtypeStruct((1, nb, nh, hd, hd), F32),
                   jax.ShapeDtypeStruct((1, CONV_BUF, nb, dc), F32)],
        scratch_shapes=[pltpu.VMEM((tb, dr), F32)] * 4,
        compiler_params=pltpu.CompilerParams(vmem_limit_bytes=VMEM_LIMIT),
        name="mix1",
    )(glu, q, k, v, sg, state_conv, state_ret, conv_w, conv_b, ln_g, ln_b, rg, rb, gam)
    return cat, st, new_conv


def _post_body(cat_ref, x_ref, gtm_ref, scf_ref, shf_ref, g_ref, wo_ref, wrh_ref, wrl_ref, x1_ref, hp_ref, lg_ref):
    d = x_ref.shape[1]
    y = jnp.dot(cat_ref[...], wo_ref[...], preferred_element_type=F32)
    x1 = x_ref[...] + gtm_ref[...] * y
    x1_ref[...] = x1
    h = _modulated_rmsnorm(x1, g_ref[...], scf_ref[...], shf_ref[...])
    hb = h.astype(BF16)
    _store_rows(hp_ref, _pack_halves(h[:, 0:d // 2], h[:, d // 2:d]))
    hl = (h - hb.astype(F32)).astype(BF16)
    nt = (((1,), (1,)), ((), ()))
    lg_ref[...] = (lax.dot_general(wrh_ref[...], hb, nt, preferred_element_type=F32)
                   + lax.dot_general(wrh_ref[...], hl, nt, preferred_element_type=F32)
                   + lax.dot_general(wrl_ref[...], hb, nt, preferred_element_type=F32))


def _post(cat, x, mod, g_ffn, wo_b, wrh, wrl, *, per_row_mod, b0=0, nb=None):
    nb_all, length, d = x.shape
    nb = nb_all if nb is None else nb
    tl = min(length, 1024)
    nl = length // tl
    t = nb * length
    ne = wrh.shape[0]
    x2d = x.reshape(nb_all * length, d)
    if per_row_mod:
        mod_spec = lambda j: pl.BlockSpec((tl, d), lambda b, l: (l, j))
    else:
        mod_spec = lambda j: pl.BlockSpec((None, None, 1, d), lambda b, l: (b + b0, j, 0, 0))
    row_spec = lambda w: pl.BlockSpec((tl, w), lambda b, l: (b * nl + l, 0))
    full = lambda a: pl.BlockSpec(a.shape, lambda b, l: (0,) * a.ndim)
    return pl.pallas_call(
        _post_body,
        grid=(nb, nl),
        in_specs=[row_spec(d), pl.BlockSpec((tl, d), lambda b, l: ((b + b0) * nl + l, 0)),
                  mod_spec(2), mod_spec(4), mod_spec(3),
                  full(g_ffn), full(wo_b), full(wrh), full(wrl)],
        out_specs=[row_spec(d),
                   pl.BlockSpec((tl * ROW_WORDS, LANES), lambda b, l: (b * nl + l, 0)),
                   pl.BlockSpec((ne, tl), lambda b, l: (0, b * nl + l))],
        out_shape=[jax.ShapeDtypeStruct((t, d), F32),
                   jax.ShapeDtypeStruct((t * ROW_WORDS, LANES), U32),
                   jax.ShapeDtypeStruct((ne, t), F32)],
        compiler_params=pltpu.CompilerParams(vmem_limit_bytes=VMEM_LIMIT),
        name="post",
    )(cat, x2d, mod, mod, mod, g_ffn, wo_b, wrh, wrl)


def _first_max(x, idx, sentinel):
    m = jnp.max(x, axis=0, keepdims=True)
    f = jnp.min(jnp.where(x == m, idx, sentinel), axis=0, keepdims=True)
    return m, f


def _route_body(lg_ref, bias_ref, e_ref, w_ref, r_ref, cnt_ref, cnt_scr, before, *, tr, ne, ng, topk, topg):
    @pl.when(pl.program_id(0) == 0)
    def _():
        cnt_scr[...] = jnp.zeros(cnt_scr.shape, F32)
        t_row = lax.broadcasted_iota(I32, (tr, tr), 0)
        t_col = lax.broadcasted_iota(I32, (tr, tr), 1)
        before[...] = (t_row < t_col).astype(BF16)

    per = ne // ng
    neg = -jnp.inf
    scores = _sigmoid(lg_ref[...])
    sel = scores + bias_ref[...]
    sub = lax.broadcasted_iota(I32, (per, tr), 0)
    gs = []
    for g in range(ng):
        s_g = sel[g * per:(g + 1) * per, :]
        m1, f1 = _first_max(s_g, sub, per)
        m2 = jnp.max(jnp.where(sub == f1, neg, s_g), axis=0, keepdims=True)
        gs.append(m1 + m2)
    gsc = jnp.concatenate(gs, axis=0)
    gi = lax.broadcasted_iota(I32, (ng, tr), 0)
    keep = jnp.zeros((ng, tr), F32)
    for _ in range(topg):
        _, f = _first_max(gsc, gi, ng)
        pick = gi == f
        keep = jnp.where(pick, 1.0, keep)
        gsc = jnp.where(pick, neg, gsc)
    work = jnp.concatenate(
        [jnp.where(keep[g:g + 1, :] > 0.5, sel[g * per:(g + 1) * per, :], neg) for g in range(ng)], axis=0)
    ei = lax.broadcasted_iota(I32, (ne, tr), 0)
    picks, es, ws = [], [], []
    for _ in range(topk):
        _, f = _first_max(work, ei, ne)
        pick = ei == f
        picks.append(pick)
        es.append(f)
        ws.append(jnp.sum(jnp.where(pick, scores, 0.0), axis=0, keepdims=True))
        work = jnp.where(pick, neg, work)
    wsum = ws[0]
    for w in ws[1:]:
        wsum = wsum + w
    scale = ROUTED_SCALE / wsum
    chosen = picks[0]
    for p in picks[1:]:
        chosen = jnp.logical_or(chosen, p)
    chosen_f = chosen.astype(F32)
    prior = cnt_scr[:, 0:1] + jnp.dot(chosen_f.astype(BF16), before[...], preferred_element_type=F32)
    rs = [jnp.sum(jnp.where(p, prior, 0.0), axis=0, keepdims=True).astype(I32) for p in picks]
    pad_i = jnp.zeros((SUBLANES - topk, tr), I32)
    pad_f = jnp.zeros((SUBLANES - topk, tr), F32)
    e_ref[...] = jnp.concatenate(es + [pad_i], axis=0)
    w_ref[...] = jnp.concatenate([w * scale for w in ws] + [pad_f], axis=0)
    r_ref[...] = jnp.concatenate(rs + [pad_i], axis=0)
    total = cnt_scr[:, 0:1] + jnp.sum(chosen_f, axis=1, keepdims=True)
    cnt_scr[...] = jnp.broadcast_to(total, cnt_scr.shape)
    cnt_ref[...] = jnp.broadcast_to(total, cnt_ref.shape)


def _token_tile(t, limit):
    return max(m for m in range(LANES, limit + 1, LANES) if t % m == 0)


def _route(logits_t, bias_col):
    ne, t = logits_t.shape
    tr = _token_tile(t, ROUTE_TILE)
    tok = lambda dt: jax.ShapeDtypeStruct((SUBLANES, t), dt)
    tok_spec = pl.BlockSpec((SUBLANES, tr), lambda i: (0, i))
    return pl.pallas_call(
        functools.partial(_route_body, tr=tr, ne=ne, ng=N_GROUPS, topk=TOP_K, topg=TOPK_GROUPS),
        grid=(t // tr,),
        in_specs=[pl.BlockSpec((ne, tr), lambda i: (0, i)), pl.BlockSpec((ne, 1), lambda i: (0, 0))],
        out_specs=[tok_spec, tok_spec, tok_spec, pl.BlockSpec((ne, LANES), lambda i: (0, 0))],
        out_shape=[tok(I32), tok(F32), tok(I32), jax.ShapeDtypeStruct((ne, LANES), F32)],
        scratch_shapes=[pltpu.VMEM((ne, LANES), F32), pltpu.VMEM((tr, tr), BF16)],
        compiler_params=pltpu.CompilerParams(dimension_semantics=("arbitrary",)),
        name="route",
    )(logits_t, bias_col)


def _dest_body(cnt_ref, e_ref, r_ref, d_ref, ps, *, ne, tr):
    shift = EXPERT_ROWS.bit_length() - 1

    @pl.when(pl.program_id(0) == 0)
    def _():
        def step(j, start):
            ps[j] = start
            return start + lax.shift_left(lax.shift_right_logical(cnt_ref[j] + (EXPERT_ROWS - 1), shift), shift)

        lax.fori_loop(0, ne, step, jnp.int32(0))

    e = e_ref[...]
    base = jnp.zeros(e.shape, I32)
    for j in range(ne):
        base = jnp.where(e == j, ps[j], base)
    dest = base + r_ref[...]
    for m in range(tr // TOKEN_TILE):
        d_ref[SUBLANES * m:SUBLANES * (m + 1), :] = dest[:, TOKEN_TILE * m:TOKEN_TILE * (m + 1)]


def _dest_rows(counts, eidx, rank):
    rows, t = eidx.shape
    ne = counts.shape[0]
    tr = _token_tile(t, DEST_TILE)
    spec = pl.BlockSpec((rows, tr), lambda i, cnt: (0, i))
    return pl.pallas_call(
        functools.partial(_dest_body, ne=ne, tr=tr),
        grid_spec=pltpu.PrefetchScalarGridSpec(
            num_scalar_prefetch=1, grid=(t // tr,), in_specs=[spec, spec],
            out_specs=pl.BlockSpec((tr // TOKEN_TILE * rows, TOKEN_TILE), lambda i, cnt: (i, 0)),
            scratch_shapes=[pltpu.SMEM((ne,), I32)]),
        out_shape=jax.ShapeDtypeStruct((t // TOKEN_TILE * rows, TOKEN_TILE), I32),
        compiler_params=pltpu.CompilerParams(dimension_semantics=("arbitrary",)),
        name="dest",
    )(counts, eidx, rank)


def _sc_mesh():
    return plsc.VectorSubcoreMesh(core_axis_name="c", subcore_axis_name="s")


def _sc_worker_id():
    return lax.axis_index("s") * SC_CORES + lax.axis_index("c")


def _index_block(dest_ref, chunk, width):
    per_tile = TOKEN_TILE // width
    return dest_ref.at[chunk // per_tile, :, pl.ds((chunk % per_tile) * width, width)]


def _dispatch(hps, dest3, n_rows):
    w = SCATTER_ROWS
    width = hps[0].shape[1]
    bounds = [0]
    for h in hps:
        bounds.append(bounds[-1] + h.shape[0] // w)
    nch = bounds[-1]
    nsrc = len(hps)

    @functools.partial(
        pl.kernel, mesh=_sc_mesh(),
        out_type=jax.ShapeDtypeStruct((n_rows, width), U32),
        scratch_types=[pltpu.VMEM((2, SUBLANES, w), I32), pltpu.VMEM((2, w, width), U32),
                       pltpu.SemaphoreType.DMA, pltpu.SemaphoreType.DMA, pltpu.SemaphoreType.DMA],
        compiler_params=pltpu.CompilerParams(use_tc_tiling_on_sc=False),
        name="dispatch",
    )
    def run(*refs):
        src_refs, dest_ref, xs_ref = refs[:nsrc], refs[nsrc], refs[nsrc + 1]
        idx_v, rows_v = refs[nsrc + 2:nsrc + 4]
        sem_load = refs[nsrc + 4:nsrc + 6]
        sem_scatter = refs[nsrc + 6]
        wid = _sc_worker_id()

        def start_loads(c, slot):
            pltpu.async_copy(_index_block(dest_ref, c, w), idx_v.at[slot], sem_load[slot])
            for i, src in enumerate(src_refs):
                @pl.when(jnp.logical_and(c >= bounds[i], c < bounds[i + 1]))
                def _(src=src, lo=bounds[i]):
                    pltpu.async_copy(src.at[pl.ds((c - lo) * w, w)], rows_v.at[slot], sem_load[slot])

        def wait_loads(slot):
            pltpu.make_async_copy(_index_block(dest_ref, 0, w), idx_v.at[slot], sem_load[slot]).wait()
            pltpu.make_async_copy(src_refs[0].at[pl.ds(0, w)], rows_v.at[slot], sem_load[slot]).wait()

        @pl.when(wid < nch)
        def _():
            start_loads(wid, 0)

        @pl.loop(0, pl.cdiv(pl.cdiv(nch, SC_WORKERS), 2))
        def _(rr):
            for slot in range(2):
                c = (rr * 2 + slot) * SC_WORKERS + wid

                @pl.when(c < nch)
                def _(c=c, slot=slot):
                    wait_loads(slot)

                    @pl.when(c + SC_WORKERS < nch)
                    def _():
                        start_loads(c + SC_WORKERS, 1 - slot)

                    copies = [pltpu.async_copy(rows_v.at[slot], xs_ref.at[idx_v.at[slot, k]], sem_scatter)
                              for k in range(TOP_K)]
                    for cp in copies:
                        cp.wait()

    return run(*hps, dest3)


def _undispatch(ys, dest3, n_tokens, after=None):
    w = GATHER_ROWS
    width = ys.shape[1]
    nch = n_tokens // w

    @functools.partial(
        pl.kernel, mesh=_sc_mesh(),
        out_type=jax.ShapeDtypeStruct((TOP_K, n_tokens, width), U32),
        scratch_types=[pltpu.VMEM((2, SUBLANES, w), I32), pltpu.VMEM((2, TOP_K, w, width), U32),
                       pltpu.SemaphoreType.DMA, pltpu.SemaphoreType.DMA, pltpu.SemaphoreType.DMA],
        compiler_params=pltpu.CompilerParams(use_tc_tiling_on_sc=False),
        name="undispatch",
    )
    def run(*refs):
        ys_ref, dest_ref = refs[:2]
        z_ref, idx_v, bufs = refs[-6:-3]
        sem_gather = refs[-3:-1]
        sem_store = refs[-1]
        wid = _sc_worker_id()

        def start_gathers(c, slot):
            pltpu.sync_copy(_index_block(dest_ref, c, w), idx_v.at[slot])
            for k in range(TOP_K):
                pltpu.async_copy(ys_ref.at[idx_v.at[slot, k]], bufs.at[slot, k], sem_gather[slot])

        def wait_gathers(slot):
            for k in range(TOP_K):
                pltpu.make_async_copy(ys_ref.at[idx_v.at[slot, k]], bufs.at[slot, k], sem_gather[slot]).wait()

        @pl.when(wid < nch)
        def _():
            start_gathers(wid, 0)

        @pl.loop(0, pl.cdiv(pl.cdiv(nch, SC_WORKERS), 2))
        def _(rr):
            for slot in range(2):
                c = (rr * 2 + slot) * SC_WORKERS + wid

                @pl.when(c < nch)
                def _(c=c, slot=slot):
                    @pl.when(c + SC_WORKERS < nch)
                    def _():
                        start_gathers(c + SC_WORKERS, 1 - slot)

                    wait_gathers(slot)
                    stores = [pltpu.async_copy(bufs.at[slot, k], z_ref.at[k, pl.ds(c * w, w)], sem_store)
                              for k in range(TOP_K)]
                    for cp in stores:
                        cp.wait()

    return run(ys, dest3) if after is None else run(ys, dest3, after)


PART_SHIFT = 24


def _expert_body(*refs, tm, ne, nparts):
    cnt_refs = refs[:nparts]
    xs_refs = refs[nparts:2 * nparts]
    w1_ref, w3_ref, w2_ref = refs[2 * nparts:2 * nparts + 3]
    ys_refs = refs[2 * nparts + 3:3 * nparts + 3]
    (w1f, w3f, w2f, w1s, w3s, w2s, xbuf, ybuf, xlo, xhi, sched, sem_x, sem_y, sem_w) = refs[3 * nparts + 3:]
    blk_words = tm * ROW_WORDS
    half = ROW_WORDS * LANES
    shift = tm.bit_length() - 1

    def n_blocks_of(p, e):
        return lax.shift_right_logical(cnt_refs[p][e] + (tm - 1), shift)

    def n_all(e):
        n = n_blocks_of(0, e)
        for p in range(1, nparts):
            n = n + n_blocks_of(p, e)
        return n

    def next_nonempty(e):
        return lax.while_loop(
            lambda c: jnp.logical_and(c < ne, n_all(jnp.minimum(c, ne - 1)) == 0), lambda c: c + 1, e)

    def plan(e, carry):
        i, starts = carry[0], list(carry[1:])
        for p in range(nparts):
            n = n_blocks_of(p, e)

            def put(j, c, p=p, i=i, start=starts[p]):
                sched[i + j] = (start + j) + (p << PART_SHIFT)
                return c

            lax.fori_loop(0, n, put, 0)
            i = i + n
            starts[p] = starts[p] + n
        return (i, *starts)

    nu = lax.fori_loop(0, ne, plan, (jnp.int32(0),) * (nparts + 1))[0]

    def rows_of(code):
        blk = code & ((1 << PART_SHIFT) - 1)
        return pl.ds(pl.multiple_of(blk * blk_words, blk_words), blk_words)

    def x_start(i, slot):
        code = sched[i]
        for p in range(nparts):
            @pl.when(lax.shift_right_logical(code, PART_SHIFT) == p)
            def _(p=p):
                pltpu.make_async_copy(xs_refs[p].at[rows_of(code), :], xbuf.at[slot], sem_x.at[slot]).start()

    def x_wait(slot):
        pltpu.make_async_copy(xs_refs[0].at[pl.ds(0, blk_words), :], xbuf.at[slot], sem_x.at[slot]).wait()

    def y_start(i, slot):
        code = sched[i]
        for p in range(nparts):
            @pl.when(lax.shift_right_logical(code, PART_SHIFT) == p)
            def _(p=p):
                pltpu.make_async_copy(ybuf.at[slot], ys_refs[p].at[rows_of(code), :], sem_y.at[slot]).start()

    def y_wait(slot):
        pltpu.make_async_copy(ybuf.at[slot], ys_refs[0].at[pl.ds(0, blk_words), :], sem_y.at[slot]).wait()

    def w_copies(e, ws):
        return [pltpu.make_async_copy(src.at[e], dst.at[ws], sem_w.at[ws])
                for src, dst in ((w1_ref, w1f), (w3_ref, w3f), (w2_ref, w2f))]

    for q in range(X_AHEAD):
        @pl.when(q < nu)
        def _(q=q):
            x_start(q, q)

    e_first = next_nonempty(jnp.int32(0))

    @pl.when(e_first < ne)
    def _():
        for cp in w_copies(e_first, 0):
            cp.start()

    def blocks(i, nblk):
        for b in range(nblk):
            x_wait((i + b) % X_SLOTS)
        for b in range(nblk):
            nxt = i + b + X_AHEAD

            @pl.when(nxt < nu)
            def _(nxt=nxt):
                x_start(nxt, nxt % X_SLOTS)

        for b in range(nblk):
            @pl.when(i + b >= Y_SLOTS)
            def _(b=b):
                y_wait((i + b) % Y_SLOTS)

        for b in range(nblk):
            xin = xbuf.at[(i + b) % X_SLOTS]
            for w in range(ROW_WORDS):
                lo, hi = _unpack_halves(_load_row_word(xin, w, tm))
                xlo[b, :, LANES * w:LANES * (w + 1)] = lo.astype(BF16)
                xhi[b, :, LANES * w:LANES * (w + 1)] = hi.astype(BF16)

        def up(b, wsc):
            return (jnp.dot(xlo[b], wsc[0:half, :], preferred_element_type=F32)
                    + jnp.dot(xhi[b], wsc[half:2 * half, :], preferred_element_type=F32))

        for b in range(nblk):
            hid = (_silu(up(b, w1s)) * up(b, w3s)).astype(BF16)
            y = jnp.dot(hid, w2s[...], preferred_element_type=F32)
            _store_rows(ybuf.at[(i + b) % Y_SLOTS], _pack_halves(y[:, 0:half], y[:, half:2 * half]))
        for b in range(nblk):
            y_start(i + b, (i + b) % Y_SLOTS)

    def per_expert(e, carry):
        i0, ws = carry
        n = n_all(e)

        @pl.when(n > 0)
        def _():
            for cp in w_copies(e, ws):
                cp.wait()
            w1s[...] = w1f[ws].astype(BF16)
            w3s[...] = w3f[ws].astype(BF16)
            w2s[...] = w2f[ws].astype(BF16)
            e_next = next_nonempty(e + 1)

            @pl.when(e_next < ne)
            def _():
                for cp in w_copies(e_next, 1 - ws):
                    cp.start()

            def pair(j, c):
                blocks(i0 + 2 * j, 2)
                return c

            lax.fori_loop(0, lax.shift_right_logical(n, 1), pair, 0)

            @pl.when(n % 2 == 1)
            def _():
                blocks(i0 + n - 1, 1)

        return i0 + n, jnp.where(n > 0, 1 - ws, ws)

    lax.fori_loop(0, ne, per_expert, (jnp.int32(0), jnp.int32(0)))

    for q in range(Y_SLOTS):
        @pl.when(nu > q)
        def _(q=q):
            y_wait((nu - 1 - q) % Y_SLOTS)


def _experts(counts, xss, w1e, w3e, w2e):
    tm = EXPERT_ROWS
    ne, d, de = w1e.shape
    half = ROW_WORDS * LANES
    nparts = len(xss)
    cap = sum(x.shape[0] // (tm * ROW_WORDS) for x in xss)
    assert cap < (1 << PART_SHIFT)
    anyspec = pl.BlockSpec(memory_space=pl.ANY)
    blk_buf = lambda n: pltpu.VMEM((n, tm * ROW_WORDS, LANES), U32)
    return pl.pallas_call(
        functools.partial(_expert_body, tm=tm, ne=ne, nparts=nparts),
        grid_spec=pltpu.PrefetchScalarGridSpec(
            num_scalar_prefetch=nparts,
            grid=(1,),
            in_specs=[anyspec] * (nparts + 3),
            out_specs=[anyspec] * nparts,
            scratch_shapes=[pltpu.VMEM((2, d, de), F32), pltpu.VMEM((2, d, de), F32), pltpu.VMEM((2, de, d), F32),
                            pltpu.VMEM((d, de), BF16), pltpu.VMEM((d, de), BF16), pltpu.VMEM((de, d), BF16),
                            blk_buf(X_SLOTS), blk_buf(Y_SLOTS),
                            pltpu.VMEM((2, tm, half), BF16), pltpu.VMEM((2, tm, half), BF16),
                            pltpu.SMEM((cap,), I32),
                            pltpu.SemaphoreType.DMA((X_SLOTS,)), pltpu.SemaphoreType.DMA((Y_SLOTS,)),
                            pltpu.SemaphoreType.DMA((2,))]),
        out_shape=[jax.ShapeDtypeStruct(x.shape, U32) for x in xss],
        compiler_params=pltpu.CompilerParams(dimension_semantics=("arbitrary",), vmem_limit_bytes=VMEM_LIMIT),
        name="experts",
    )(*counts, *xss, w1e, w3e, w2e)


def _combine_body(z_ref, x1_ref, scf_ref, shf_ref, gtf_ref, wt_ref, gffn_ref, gfin_ref, ws1_ref, ws3_ref, ws2_ref,
                  y_ref, xo, *, td, topk):
    half = ROW_WORDS * LANES
    hb = _modulated_rmsnorm(x1_ref[...], gffn_ref[...], scf_ref[...], shf_ref[...]).astype(BF16)
    up = lambda w_ref: jnp.dot(hb, w_ref[...], preferred_element_type=F32)
    xo[...] = jnp.dot((_silu(up(ws1_ref)) * up(ws3_ref)).astype(BF16), ws2_ref[...], preferred_element_type=F32)
    wt = wt_ref[...].T
    ws = [wt[:, k:k + 1] for k in range(topk)]
    sq = jnp.zeros((td, 1), F32)
    for j in range(ROW_WORDS):
        acc_lo = jnp.zeros((td, LANES), F32)
        acc_hi = jnp.zeros((td, LANES), F32)
        for k in range(topk):
            lo, hi = _unpack_halves(_load_row_word(z_ref.at[k], j, td))
            acc_lo = acc_lo + ws[k] * lo
            acc_hi = acc_hi + ws[k] * hi
        for base, acc in ((0, acc_lo), (half, acc_hi)):
            cols = slice(base + LANES * j, base + LANES * (j + 1))
            x = x1_ref[:, cols] + gtf_ref[:, cols] * (acc + xo[:, cols])
            xo[:, cols] = x
            sq = sq + jnp.sum(x * x, axis=-1, keepdims=True)
    rs = lax.rsqrt(sq / (2 * half) + EPS)
    y_ref[...] = xo[...] * rs * gfin_ref[...]


def _combine_body_into(*refs, td, topk):
    _combine_body(*refs[:11], *refs[12:], td=td, topk=topk)


def _combine(z, token0, wts, x1, mod, g_ffn, g_final, shared_w, *, rows_per_mod, per_row_mod, b0=0, out_rows=None,
             into=None):
    t, d = x1.shape
    td = min(t, COMBINE_TILE)
    tile0 = token0 // td
    out_rows = t if out_rows is None else out_rows
    if per_row_mod:
        mod_spec = lambda j: pl.BlockSpec((td, d), lambda i: (i, j))
        out0 = 0
    else:
        tiles_per_mod = rows_per_mod // td
        mod_spec = lambda j: pl.BlockSpec((None, None, 1, d), lambda i: (i // tiles_per_mod + b0, j, 0, 0))
        out0 = b0 * tiles_per_mod
    full = lambda a: pl.BlockSpec(a.shape, lambda i: (0,) * a.ndim)
    in_specs = [pl.BlockSpec((TOP_K, td * ROW_WORDS, LANES), lambda i: (0, i + tile0, 0)),
                pl.BlockSpec((td, d), lambda i: (i, 0)),
                mod_spec(4), mod_spec(3), mod_spec(5),
                pl.BlockSpec((SUBLANES, td), lambda i: (0, i + tile0)),
                full(g_ffn), full(g_final)] + [full(w) for w in shared_w]
    args = [z, x1, mod, mod, mod, wts, g_ffn, g_final, *shared_w]
    body, aliases = _combine_body, {}
    if into is not None:
        in_specs.append(pl.BlockSpec(memory_space=pl.ANY))
        args.append(into)
        body, aliases = _combine_body_into, {len(args) - 1: 0}
    return pl.pallas_call(
        functools.partial(body, td=td, topk=TOP_K),
        grid=(t // td,),
        in_specs=in_specs,
        out_specs=pl.BlockSpec((td, d), lambda i: (i + out0, 0)),
        scratch_shapes=[pltpu.VMEM((td, d), F32)],
        out_shape=jax.ShapeDtypeStruct((out_rows, d), F32),
        input_output_aliases=aliases,
        compiler_params=pltpu.CompilerParams(vmem_limit_bytes=VMEM_LIMIT),
        name="combine",
    )(*args)


def _log_gamma(nh):
    return np.log(1.0 - 2.0 ** (-5.0 - np.arange(nh, dtype=np.float32))).astype(np.float32)


def _retention_tables(length, nh, hd):
    c = math.gcd(length, RET_CHUNK)
    log_g = _log_gamma(nh)
    idx = np.arange(c, dtype=np.float32)
    rel = idx[:, None] - idx[None, :]
    mask = np.where(rel >= 0, np.exp(log_g[:, None, None] * np.maximum(rel, 0.0)), 0.0).astype(np.float32)
    q_decay = np.exp(log_g[None, :] * (idx[:, None] + 1.0)).astype(np.float32)
    k_decay = np.exp(log_g[None, :] * (c - 1.0 - idx[:, None])).astype(np.float32)
    chunk_decay = np.exp(log_g * np.float32(c)).astype(np.float32)
    qd = np.broadcast_to(q_decay.T[:, :, None], (nh, c, hd))
    kd = np.broadcast_to(k_decay.T[:, :, None], (nh, c, hd))
    cd = np.broadcast_to(chunk_decay[:, None, None], (nh, hd, hd))
    return tuple(jnp.asarray(t) for t in (mask, qd, kd, cd))


def kernel(x_prompt, x_sample, c_prompt, c_sample, state_conv, state_ret, w_ada, b_ada, g_mix, g_ffn, w_in,
           conv_w, conv_b, conv_norm_g, conv_norm_b, ret_norm_g, ret_norm_b, w_out, w_router, router_bias,
           w1, w3, w2, ws1, ws3, ws2, g_final):
    depth = w_ada.shape[0]
    assert depth == 1, "single-layer trunk"
    bp, lp, d = x_prompt.shape
    bs, ls, _ = x_sample.shape
    assert ls == 1
    dc = conv_w.shape[2]
    dr = ret_norm_g.shape[1]
    nh = RET_HEADS
    hd = dr // nh
    assert hd == LANES and lp % 256 == 0 and bs % TOKEN_TILE == 0 and d // 2 == ROW_WORDS * LANES
    ne = w_router.shape[2]
    row = lambda a: a.reshape(1, -1)

    mod_p, mod_s = _ada([c_prompt, c_sample], w_ada[0], row(b_ada[0]))
    mod_p = mod_p.reshape(bp, 6, 1, d)

    half = hd // 2
    inv = (np.float32(ROPE_BASE) ** (-np.arange(half, dtype=np.float32) / np.float32(half))).astype(np.float32)
    inv2 = np.concatenate([inv, inv]).reshape(1, hd)
    cos_p, sin_p = _rope_tables(inv2, lp, 0)
    cos_s, sin_s = _rope_tables(inv2, SUBLANES, PAST_LEN)

    w_in_b = w_in[0].astype(BF16)
    wo_b = w_out[0].astype(BF16)
    wr_t = w_router[0].T
    wrh = wr_t.astype(BF16)
    wrl = (wr_t - wrh.astype(F32)).astype(BF16)
    ws1_b, ws3_b, ws2_b = ws1[0].astype(BF16), ws3[0].astype(BF16), ws2[0].astype(BF16)
    dims = dict(dc=dc, dr=dr, hd=hd)

    tables = _retention_tables(lp, nh, hd)
    gam = jnp.asarray(np.broadcast_to(np.exp(_log_gamma(nh))[:, None, None], (nh, SUBLANES, hd)))
    norm_rows = (row(conv_b[0]), row(conv_norm_g[0]), row(conv_norm_b[0]), row(ret_norm_g[0]), row(ret_norm_b[0]))
    post_w = (row(g_ffn[0]), wo_b, wrh, wrl)
    shared_w = (ws1_b, ws3_b, ws2_b)
    bias_col = router_bias[0].reshape(ne, 1)
    hw = d // 2
    tm = EXPERT_ROWS

    def pre_prompt(b0, nb, after=None):
        glu, q, k, v, sg = _proj(x_prompt, mod_p, row(g_mix[0]), w_in_b, cos_p, sin_p,
                                 per_row_mod=False, b0=b0, nb=nb, after=after, **dims)
        cat, ret = _mix(glu, q, k, v, sg, conv_w[0], *norm_rows, tables, nb=nb, length=lp, **dims)
        x2, hp, lg = _post(cat, x_prompt, mod_p, *post_w, per_row_mod=False, b0=b0, nb=nb)
        return glu, ret, x2, hp, lg

    def pre_sample():
        xs3 = x_sample.reshape(1, bs, d)
        glu, q, k, v, sg = _proj(xs3, mod_s, row(g_mix[0]), w_in_b, cos_s, sin_s, per_row_mod=True, **dims)
        cat, ret, new_conv = _mix1(glu, q, k, v, sg, state_conv.transpose(0, 2, 1, 3), state_ret, conv_w[0],
                                   *norm_rows, gam, **dims)
        x2, hp, lg = _post(cat, xs3, mod_s, *post_w, per_row_mod=True)
        return new_conv.transpose(0, 2, 1, 3), ret, x2, hp, lg

    def route_and_dispatch(hps, lgs):
        lg = lgs[0] if len(lgs) == 1 else jnp.concatenate(lgs, axis=1)
        tokens = lg.shape[1]
        eidx, wts, rank, cnt = _route(lg, bias_col)
        counts = cnt[:, 0].astype(I32)
        n_rows = -(-(tokens * TOP_K + ne * (tm - 1)) // tm) * tm
        dest3 = _dest_rows(counts, eidx, rank).reshape(tokens // TOKEN_TILE, SUBLANES, TOKEN_TILE)
        xs = _dispatch([h.reshape(-1, hw) for h in hps], dest3, n_rows)
        return counts, dest3, wts, xs.reshape(n_rows * ROW_WORDS, LANES)

    def undispatch(ys, dest3, after=None):
        tokens = dest3.shape[0] * TOKEN_TILE
        z = _undispatch(ys.reshape(-1, hw), dest3, tokens, after)
        return z.reshape(TOP_K, tokens * ROW_WORDS, LANES)

    nb0 = bp // 2
    nb1 = bp - nb0
    new_conv_s, ret_s, x2_s, hp_s, lg_s = pre_sample()
    glu_0, ret_0, x2_0, hp_0, lg_0 = pre_prompt(0, nb0, after=lg_s)
    counts_0, dest_0, wts_0, xs_0 = route_and_dispatch([hp_0], [lg_0])
    glu_1, ret_1, x2_1, hp_1, lg_1 = pre_prompt(nb0, nb1, after=dest_0)
    counts_1, dest_1, wts_1, xs_1 = route_and_dispatch([hp_1, hp_s], [lg_1, lg_s])
    ys_0, ys_1 = _experts((counts_0, counts_1), (xs_0, xs_1), w1[0], w3[0], w2[0])
    z_0 = undispatch(ys_0, dest_0)
    z_1 = undispatch(ys_1, dest_1, after=z_0[0, :SUBLANES])
    y_p = _combine(z_0, 0, wts_0, x2_0, mod_p, row(g_ffn[0]), row(g_final), shared_w, rows_per_mod=lp, per_row_mod=False,
                   out_rows=bp * lp)
    y_p = _combine(z_1, 0, wts_1, x2_1, mod_p, row(g_ffn[0]), row(g_final), shared_w, rows_per_mod=lp, per_row_mod=False,
                   b0=nb0, out_rows=bp * lp, into=y_p)
    y_s = _combine(z_1, nb1 * lp, wts_1, x2_s, mod_s, row(g_ffn[0]), row(g_final), shared_w, rows_per_mod=bs,
                   per_row_mod=True)
    ret_p = jnp.concatenate([ret_0, ret_1], axis=0)

    tail = lambda g, n: g.reshape(n, lp, dc)[:, lp - CONV_BUF:, :]
    new_conv_p = jnp.concatenate([tail(glu_0, nb0), tail(glu_1, nb1)], axis=0)
    return (y_p.reshape(bp, lp, d), y_s.reshape(bs, ls, d), new_conv_p[None], ret_p[None],
            new_conv_s, ret_s)
```

```python
import functools
import math

import jax
import jax.numpy as jnp
import numpy as np
from jax import lax
from jax.experimental import pallas as pl
from jax.experimental.pallas import tpu as pltpu
from jax.experimental.pallas import tpu_sc as plsc

F32 = jnp.float32
BF16 = jnp.bfloat16
U32 = jnp.uint32
I32 = jnp.int32

EPS = 1e-6
PAST_LEN = 16384
RET_HEADS = 4
RET_CHUNK = 128
CONV_WIDTH = 31
CONV_BUF = CONV_WIDTH - 1
ROPE_BASE = 10000.0
TOP_K = 6
N_GROUPS = 8
TOPK_GROUPS = 4
ROUTED_SCALE = 2.5

LANES = 128
SUBLANES = 8
CONV_PAD = 32
EXPERT_ROWS = 256
ROUTE_TILE = 1024
DEST_TILE = 4096
TOKEN_TILE = 128
COMBINE_TILE = 512
VMEM_LIMIT = 56 * 1024 * 1024
SC_CORES = 2
SC_SUBCORES = 16
SC_WORKERS = SC_CORES * SC_SUBCORES
SCATTER_ROWS = 64
GATHER_ROWS = 16
ROW_WORDS = 4
POST_BUFFERS = 3
X_SLOTS = 6
X_AHEAD = 4
Y_SLOTS = 4

HI_MASK = 0xFFFF0000


def _sigmoid(x):
    return jax.nn.sigmoid(x)


def _silu(x):
    return x * jax.nn.sigmoid(x)


def _pack_halves(lo, hi):
    lo_u = lax.bitcast_convert_type(lo.astype(BF16).astype(F32), U32) >> 16
    hi_u = lax.bitcast_convert_type(hi.astype(BF16).astype(F32), U32) & jnp.uint32(HI_MASK)
    return hi_u | lo_u


def _unpack_halves(p):
    lo = lax.bitcast_convert_type(p << 16, F32)
    hi = lax.bitcast_convert_type(p & jnp.uint32(HI_MASK), F32)
    return lo, hi


def _store_rows(ref, x):
    rows = x.shape[0]
    for j in range(ROW_WORDS):
        ref[pl.ds(j, rows, stride=ROW_WORDS), :] = x[:, LANES * j:LANES * (j + 1)]


def _load_row_word(ref, j, rows):
    return ref[pl.ds(j, rows, stride=ROW_WORDS), :]


def _ada_body(w_ref, b_ref, *refs):
    w = w_ref[...].astype(BF16)
    groups = len(refs) // 2
    for c_ref, o_ref in zip(refs[:groups], refs[groups:]):
        o_ref[...] = jnp.dot(_silu(c_ref[...]).astype(BF16), w, preferred_element_type=F32) + b_ref[...]


def _ada(cs, w_ada, b_ada):
    d, n = w_ada.shape
    tn = 2048
    return pl.pallas_call(
        _ada_body,
        grid=(n // tn,),
        in_specs=[pl.BlockSpec((d, tn), lambda j: (0, j)), pl.BlockSpec((1, tn), lambda j: (0, j))]
        + [pl.BlockSpec(c.shape, lambda j: (0, 0)) for c in cs],
        out_specs=[pl.BlockSpec((c.shape[0], tn), lambda j: (0, j)) for c in cs],
        out_shape=[jax.ShapeDtypeStruct((c.shape[0], n), F32) for c in cs],
        compiler_params=pltpu.CompilerParams(vmem_limit_bytes=VMEM_LIMIT),
        name="ada",
    )(w_ada, b_ada, *cs)


def _rope_tables(inv2, rows, pos0):
    pos = np.arange(rows, dtype=np.float32) + np.float32(pos0)
    ang = (pos[:, None] * inv2).astype(np.float64)
    sign = np.where(np.arange(inv2.shape[1]) < inv2.shape[1] // 2, -1.0, 1.0)
    return jnp.asarray(np.cos(ang).astype(np.float32)), jnp.asarray((np.sin(ang) * sign).astype(np.float32))


def _modulated_rmsnorm(x, g, sc, sh):
    ms = jnp.mean(x * x, axis=-1, keepdims=True)
    h = x * lax.rsqrt(ms + EPS) * g
    return h * (1.0 + sc) + sh


def _proj_body(x_ref, sh_ref, sc_ref, g_ref, w_ref, cos_ref, sin_ref, *rest, dc, dr, hd, rope_rows):
    glu_ref, q_ref, k_ref, v_ref, sg_ref = rest[-5:]
    hb = _modulated_rmsnorm(x_ref[...], g_ref[...], sc_ref[...], sh_ref[...]).astype(BF16)

    def proj(lo, n):
        return jnp.dot(hb, w_ref[:, lo:lo + n], preferred_element_type=F32)

    glu_ref[...] = proj(0, dc) * _sigmoid(proj(dc, dc))
    cos = cos_ref[...] if rope_rows else cos_ref[0:1, :]
    sin = sin_ref[...] if rope_rows else sin_ref[0:1, :]
    for ref, lo, scale in ((q_ref, 2 * dc, hd ** -0.5), (k_ref, 2 * dc + dr, None)):
        t = proj(lo, dr)
        for hh in range(dr // hd):
            th = t[:, hh * hd:(hh + 1) * hd]
            r = th * cos + pltpu.roll(th, hd // 2, 1) * sin
            if scale is not None:
                r = r * scale
            ref[:, hh * hd:(hh + 1) * hd] = r.astype(BF16)
    v_ref[...] = proj(2 * dc + 2 * dr, dr).astype(BF16)
    sg_ref[...] = _silu(proj(2 * dc + 3 * dr, dr)).astype(BF16)


def _proj(x, mod, g_mix, w_in_b, cos2, sin2, *, dc, dr, hd, per_row_mod, b0=0, nb=None, after=None):
    nb_all, length, d = x.shape
    nb = nb_all if nb is None else nb
    tl = min(length, 512)
    x2 = x.reshape(nb_all * length, d)
    nl = length // tl
    if per_row_mod:
        mod_spec = lambda j: pl.BlockSpec((tl, d), lambda b, l: (l, j))
        rope_spec = pl.BlockSpec((SUBLANES, LANES), lambda b, l: (0, 0))
    else:
        mod_spec = lambda j: pl.BlockSpec((None, None, 1, d), lambda b, l: (b + b0, j, 0, 0))
        rope_spec = pl.BlockSpec((tl, LANES), lambda b, l: (l, 0))
    row_spec = lambda w: pl.BlockSpec((tl, w), lambda b, l: (b * nl + l, 0))
    t = nb * length
    outs = pl.pallas_call(
        functools.partial(_proj_body, dc=dc, dr=dr, hd=hd, rope_rows=not per_row_mod),
        grid=(nb, nl),
        in_specs=[
            pl.BlockSpec((tl, d), lambda b, l: ((b + b0) * nl + l, 0)), mod_spec(0), mod_spec(1),
            pl.BlockSpec((1, d), lambda b, l: (0, 0)),
            pl.BlockSpec(w_in_b.shape, lambda b, l: (0, 0)),
            rope_spec, rope_spec,
        ] + ([] if after is None else [pl.BlockSpec(memory_space=pl.ANY)]),
        out_specs=[row_spec(dc), row_spec(dr), row_spec(dr), row_spec(dr), row_spec(dr)],
        out_shape=[
            jax.ShapeDtypeStruct((t, dc), F32),
            jax.ShapeDtypeStruct((t, dr), BF16),
            jax.ShapeDtypeStruct((t, dr), BF16),
            jax.ShapeDtypeStruct((t, dr), BF16),
            jax.ShapeDtypeStruct((t, dr), BF16),
        ],
        compiler_params=pltpu.CompilerParams(vmem_limit_bytes=VMEM_LIMIT),
        name="proj",
    )(x2, mod, mod, g_mix, w_in_b, cos2, sin2, *(() if after is None else (after,)))
    return outs


def _layernorm_silu(c, g, b):
    mu = jnp.mean(c, axis=-1, keepdims=True)
    d = c - mu
    var = jnp.mean(d * d, axis=-1, keepdims=True)
    return _silu(d * lax.rsqrt(var + EPS) * g + b)


def _groupnorm(o, g, b):
    mu = jnp.mean(o, axis=-1, keepdims=True)
    d = o - mu
    var = jnp.mean(d * d, axis=-1, keepdims=True)
    return d * lax.rsqrt(var + EPS) * g + b


def _mix_body(glu_ref, q_ref, k_ref, v_ref, sg_ref, cw_ref, cb_ref, lng_ref, lnb_ref, rg_ref, rb_ref,
              mask_ref, qd_ref, kd_ref, cd_ref, cat_ref, st_ref, buf, cscr, *, tl, dc, hd, nh, chunk):
    nslab = dc // LANES

    @pl.when(pl.program_id(1) == 0)
    def _():
        buf[:, 0:CONV_PAD, :] = jnp.zeros((nslab, CONV_PAD, LANES), F32)
        st_ref[...] = jnp.zeros(st_ref.shape, F32)

    for j in range(nslab):
        buf[j, CONV_PAD:CONV_PAD + tl, :] = glu_ref[:, LANES * j:LANES * (j + 1)]
    first = CONV_PAD - CONV_BUF
    rows_per_iter = 8 * SUBLANES
    for j in range(nslab):
        cols = slice(LANES * j, LANES * (j + 1))
        wv = [jnp.broadcast_to(cw_ref[t:t + 1, cols], (SUBLANES, LANES)) for t in range(CONV_WIDTH)]
        bias = jnp.broadcast_to(cb_ref[0:1, cols], (SUBLANES, LANES))

        def body(r, carry, j=j, cols=cols, wv=wv, bias=bias):
            base = pl.multiple_of(r * rows_per_iter, rows_per_iter)
            for u in range(rows_per_iter // SUBLANES):
                acc = bias
                for t in range(CONV_WIDTH):
                    acc = acc + wv[t] * buf[j, pl.ds(base + (u * SUBLANES + first + t), SUBLANES), :]
                cscr[pl.ds(base + u * SUBLANES, SUBLANES), cols] = acc
            return carry

        lax.fori_loop(0, tl // rows_per_iter, body, 0)
    for j in range(nslab):
        buf[j, 0:CONV_PAD, :] = buf[j, tl:tl + CONV_PAD, :]
    cat_ref[:, 0:dc] = _layernorm_silu(cscr[...], lng_ref[...], lnb_ref[...]).astype(BF16)

    nt = (((1,), (1,)), ((), ()))
    tn = (((0,), (0,)), ((), ()))
    for c in range(tl // chunk):
        rows = slice(c * chunk, (c + 1) * chunk)
        for hh in range(nh):
            cols = slice(hh * hd, (hh + 1) * hd)
            qh = q_ref[rows, cols]
            kh = k_ref[rows, cols]
            vh = v_ref[rows, cols]
            s = st_ref[0, hh]
            scores = lax.dot_general(qh, kh, nt, preferred_element_type=F32) * mask_ref[hh]
            inner = jnp.dot(scores.astype(BF16), vh, preferred_element_type=F32)
            qd = (qh.astype(F32) * qd_ref[hh]).astype(BF16)
            cross = jnp.dot(qd, s.astype(BF16), preferred_element_type=F32)
            kd = (kh.astype(F32) * kd_ref[hh]).astype(BF16)
            st_ref[0, hh] = cd_ref[hh] * s + lax.dot_general(kd, vh, tn, preferred_element_type=F32)
            o = _groupnorm(inner + cross, rg_ref[0:1, cols], rb_ref[0:1, cols])
            cat_ref[rows, dc + hh * hd:dc + (hh + 1) * hd] = (o * sg_ref[rows, cols].astype(F32)).astype(BF16)


def _mix(glu, q, k, v, sg, conv_w, conv_b, ln_g, ln_b, rg, rb, tables, *, nb, length, dc, dr, hd):
    nh = dr // hd
    chunk = math.gcd(length, RET_CHUNK)
    tl = min(length, 256)
    nl = length // tl
    mask, qd, kd, cd = tables
    row_spec = lambda w: pl.BlockSpec((tl, w), lambda b, l: (b * nl + l, 0))
    full = lambda a: pl.BlockSpec(a.shape, lambda b, l: (0,) * a.ndim)
    cat, st = pl.pallas_call(
        functools.partial(_mix_body, tl=tl, dc=dc, hd=hd, nh=nh, chunk=chunk),
        grid=(nb, nl),
        in_specs=[row_spec(dc), row_spec(dr), row_spec(dr), row_spec(dr), row_spec(dr),
                  full(conv_w), full(conv_b), full(ln_g), full(ln_b), full(rg), full(rb),
                  full(mask), full(qd), full(kd), full(cd)],
        out_specs=[row_spec(dc + dr), pl.BlockSpec((1, nh, hd, hd), lambda b, l: (b, 0, 0, 0))],
        out_shape=[jax.ShapeDtypeStruct((nb * length, dc + dr), BF16),
                   jax.ShapeDtypeStruct((nb, nh, hd, hd), F32)],
        scratch_shapes=[pltpu.VMEM((dc // LANES, tl + CONV_PAD, LANES), F32),
                        pltpu.VMEM((tl, dc), F32)],
        compiler_params=pltpu.CompilerParams(dimension_semantics=("arbitrary", "arbitrary")),
        name="mix",
    )(glu, q, k, v, sg, conv_w, conv_b, ln_g, ln_b, rg, rb, mask, qd, kd, cd)
    return cat, st


def _mix1_body(glu_ref, q_ref, k_ref, v_ref, sg_ref, sc_ref, s0_ref, cw_ref, cb_ref, lng_ref, lnb_ref,
               rg_ref, rb_ref, gam_ref, cat_ref, st_ref, nc_ref, oscr, qf, kf, vf, *, tb, dc, hd, nh):
    glu = glu_ref[...]
    conv = glu * cw_ref[CONV_BUF:CONV_WIDTH, :] + cb_ref[...]
    for r in range(CONV_BUF):
        past = sc_ref[r]
        conv = conv + past * cw_ref[r:r + 1, :]
        if r > 0:
            nc_ref[r - 1] = past
    nc_ref[CONV_BUF - 1] = glu
    cat_ref[:, 0:dc] = _layernorm_silu(conv, lng_ref[...], lnb_ref[...]).astype(BF16)

    tn = (((0,), (0,)), ((), ()))
    rowid = lax.broadcasted_iota(I32, (tb, hd), 0)
    qf[...] = q_ref[...].astype(F32)
    kf[...] = k_ref[...].astype(F32)
    vf[...] = v_ref[...].astype(F32)
    for hh in range(nh):
        cols = slice(hh * hd, (hh + 1) * hd)
        qa = q_ref[:, cols]
        ka = k_ref[:, cols]
        gam = gam_ref[hh, 0:1, :]
        for bb in range(tb):
            onehot = (rowid == bb).astype(BF16)
            qcol = lax.dot_general(qa, onehot, tn, preferred_element_type=F32)
            kcol = lax.dot_general(ka, onehot, tn, preferred_element_type=F32)
            s0 = s0_ref[bb, hh]
            qrow = qf[bb:bb + 1, cols]
            krow = kf[bb:bb + 1, cols]
            vrow = vf[bb:bb + 1, cols]
            qk = jnp.sum(qrow * krow, axis=-1, keepdims=True)
            cross = gam * jnp.sum(qcol * s0, axis=0, keepdims=True)
            st_ref[bb, hh] = gam * s0 + kcol * vrow
            oscr[bb:bb + 1, cols] = qk * vrow + cross
    for hh in range(nh):
        cols = slice(hh * hd, (hh + 1) * hd)
        o = _groupnorm(oscr[:, cols], rg_ref[0:1, cols], rb_ref[0:1, cols])
        cat_ref[:, dc + hh * hd:dc + (hh + 1) * hd] = (o * sg_ref[:, cols].astype(F32)).astype(BF16)


def _mix1(glu, q, k, v, sg, state_conv, state_ret, conv_w, conv_b, ln_g, ln_b, rg, rb, gam, *, dc, dr, hd):
    nb = glu.shape[0]
    nh = dr // hd
    tb = 16
    row_spec = lambda w: pl.BlockSpec((tb, w), lambda i: (i, 0))
    full = lambda a: pl.BlockSpec(a.shape, lambda i: (0,) * a.ndim)
    st_spec = pl.BlockSpec((None, tb, nh, hd, hd), lambda i: (0, i, 0, 0, 0))
    conv_spec = pl.BlockSpec((None, CONV_BUF, tb, dc), lambda i: (0, 0, i, 0))
    cat, st, new_conv = pl.pallas_call(
        functools.partial(_mix1_body, tb=tb, dc=dc, hd=hd, nh=nh),
        grid=(nb // tb,),
        in_specs=[row_spec(dc), row_spec(dr), row_spec(dr), row_spec(dr), row_spec(dr), conv_spec, st_spec,
                  full(conv_w), full(conv_b), full(ln_g), full(ln_b), full(rg), full(rb), full(gam)],
        out_specs=[row_spec(dc + dr), st_spec, conv_spec],
        out_shape=[jax.ShapeDtypeStruct((nb, dc + dr), BF16),
                   jax.ShapeDtypeStruct((1, nb, nh, hd, hd), F32),
                   jax.ShapeDtypeStruct((1, CONV_BUF, nb, dc), F32)],
        scratch_shapes=[pltpu.VMEM((tb, dr), F32)] * 4,
        compiler_params=pltpu.CompilerParams(vmem_limit_bytes=VMEM_LIMIT),
        name="mix1",
    )(glu, q, k, v, sg, state_conv, state_ret, conv_w, conv_b, ln_g, ln_b, rg, rb, gam)
    return cat, st, new_conv


def _post_body(cat_ref, x_ref, gtm_ref, scf_ref, shf_ref, g_ref, wo_ref, wrh_ref, wrl_ref, x1_ref, hp_ref, lg_ref):
    d = x_ref.shape[1]
    y = jnp.dot(cat_ref[...], wo_ref[...], preferred_element_type=F32)
    x1 = x_ref[...] + gtm_ref[...] * y
    x1_ref[...] = x1
    h = _modulated_rmsnorm(x1, g_ref[...], scf_ref[...], shf_ref[...])
    hb = h.astype(BF16)
    _store_rows(hp_ref, _pack_halves(h[:, 0:d // 2], h[:, d // 2:d]))
    hl = (h - hb.astype(F32)).astype(BF16)
    nt = (((1,), (1,)), ((), ()))
    lg_ref[...] = (lax.dot_general(wrh_ref[...], hb, nt, preferred_element_type=F32)
                   + lax.dot_general(wrh_ref[...], hl, nt, preferred_element_type=F32)
                   + lax.dot_general(wrl_ref[...], hb, nt, preferred_element_type=F32))


def _post(cat, x, mod, g_ffn, wo_b, wrh, wrl, *, per_row_mod, b0=0, nb=None):
    nb_all, length, d = x.shape
    nb = nb_all if nb is None else nb
    tl = min(length, 1024)
    nl = length // tl
    t = nb * length
    ne = wrh.shape[0]
    x2d = x.reshape(nb_all * length, d)
    if per_row_mod:
        mod_spec = lambda j: pl.BlockSpec((tl, d), lambda b, l: (l, j))
    else:
        mod_spec = lambda j: pl.BlockSpec((None, None, 1, d), lambda b, l: (b + b0, j, 0, 0))
    row_spec = lambda w: pl.BlockSpec((tl, w), lambda b, l: (b * nl + l, 0))
    full = lambda a: pl.BlockSpec(a.shape, lambda b, l: (0,) * a.ndim)
    deep = pl.Buffered(POST_BUFFERS)
    steps = pltpu.emit_pipeline(
        _post_body,
        grid=(nb, nl),
        in_specs=[pl.BlockSpec((tl, d), lambda b, l: (b * nl + l, 0), pipeline_mode=deep),
                  pl.BlockSpec((tl, d), lambda b, l: ((b + b0) * nl + l, 0), pipeline_mode=deep),
                  mod_spec(2), mod_spec(4), mod_spec(3),
                  full(g_ffn), full(wo_b), full(wrh), full(wrl)],
        out_specs=[row_spec(d),
                   pl.BlockSpec((tl * ROW_WORDS, LANES), lambda b, l: (b * nl + l, 0)),
                   pl.BlockSpec((ne, tl), lambda b, l: (0, b * nl + l))],
        tiling=pltpu.Tiling.COMPACT)
    hbm = pl.BlockSpec(memory_space=pl.ANY)
    return pl.pallas_call(
        lambda *refs: steps(*refs),
        in_specs=[hbm] * 9,
        out_specs=[hbm] * 3,
        out_shape=[jax.ShapeDtypeStruct((t, d), F32),
                   jax.ShapeDtypeStruct((t * ROW_WORDS, LANES), U32),
                   jax.ShapeDtypeStruct((ne, t), F32)],
        compiler_params=pltpu.CompilerParams(vmem_limit_bytes=VMEM_LIMIT),
        name="post",
    )(cat, x2d, mod, mod, mod, g_ffn, wo_b, wrh, wrl)


def _first_max(x, idx, sentinel):
    m = jnp.max(x, axis=0, keepdims=True)
    f = jnp.min(jnp.where(x == m, idx, sentinel), axis=0, keepdims=True)
    return m, f


def _route_body(lg_ref, bias_ref, e_ref, w_ref, r_ref, cnt_ref, cnt_scr, before, *, tr, ne, ng, topk, topg):
    @pl.when(pl.program_id(0) == 0)
    def _():
        cnt_scr[...] = jnp.zeros(cnt_scr.shape, F32)
        t_row = lax.broadcasted_iota(I32, (tr, tr), 0)
        t_col = lax.broadcasted_iota(I32, (tr, tr), 1)
        before[...] = (t_row < t_col).astype(BF16)

    per = ne // ng
    neg = -jnp.inf
    scores = _sigmoid(lg_ref[...])
    sel = scores + bias_ref[...]
    sub = lax.broadcasted_iota(I32, (per, tr), 0)
    gs = []
    for g in range(ng):
        s_g = sel[g * per:(g + 1) * per, :]
        m1, f1 = _first_max(s_g, sub, per)
        m2 = jnp.max(jnp.where(sub == f1, neg, s_g), axis=0, keepdims=True)
        gs.append(m1 + m2)
    gsc = jnp.concatenate(gs, axis=0)
    gi = lax.broadcasted_iota(I32, (ng, tr), 0)
    keep = jnp.zeros((ng, tr), F32)
    for _ in range(topg):
        _, f = _first_max(gsc, gi, ng)
        pick = gi == f
        keep = jnp.where(pick, 1.0, keep)
        gsc = jnp.where(pick, neg, gsc)
    work = jnp.concatenate(
        [jnp.where(keep[g:g + 1, :] > 0.5, sel[g * per:(g + 1) * per, :], neg) for g in range(ng)], axis=0)
    ei = lax.broadcasted_iota(I32, (ne, tr), 0)
    picks, es, ws = [], [], []
    for _ in range(topk):
        _, f = _first_max(work, ei, ne)
        pick = ei == f
        picks.append(pick)
        es.append(f)
        ws.append(jnp.sum(jnp.where(pick, scores, 0.0), axis=0, keepdims=True))
        work = jnp.where(pick, neg, work)
    wsum = ws[0]
    for w in ws[1:]:
        wsum = wsum + w
    scale = ROUTED_SCALE / wsum
    chosen = picks[0]
    for p in picks[1:]:
        chosen = jnp.logical_or(chosen, p)
    chosen_f = chosen.astype(F32)
    prior = cnt_scr[:, 0:1] + jnp.dot(chosen_f.astype(BF16), before[...], preferred_element_type=F32)
    rs = [jnp.sum(jnp.where(p, prior, 0.0), axis=0, keepdims=True).astype(I32) for p in picks]
    pad_i = jnp.zeros((SUBLANES - topk, tr), I32)
    pad_f = jnp.zeros((SUBLANES - topk, tr), F32)
    e_ref[...] = jnp.concatenate(es + [pad_i], axis=0)
    w_ref[...] = jnp.concatenate([w * scale for w in ws] + [pad_f], axis=0)
    r_ref[...] = jnp.concatenate(rs + [pad_i], axis=0)
    total = cnt_scr[:, 0:1] + jnp.sum(chosen_f, axis=1, keepdims=True)
    cnt_scr[...] = jnp.broadcast_to(total, cnt_scr.shape)
    cnt_ref[...] = jnp.broadcast_to(total, cnt_ref.shape)


def _token_tile(t, limit):
    return max(m for m in range(LANES, limit + 1, LANES) if t % m == 0)


def _route(logits_t, bias_col):
    ne, t = logits_t.shape
    tr = _token_tile(t, ROUTE_TILE)
    tok = lambda dt: jax.ShapeDtypeStruct((SUBLANES, t), dt)
    tok_spec = pl.BlockSpec((SUBLANES, tr), lambda i: (0, i))
    return pl.pallas_call(
        functools.partial(_route_body, tr=tr, ne=ne, ng=N_GROUPS, topk=TOP_K, topg=TOPK_GROUPS),
        grid=(t // tr,),
        in_specs=[pl.BlockSpec((ne, tr), lambda i: (0, i)), pl.BlockSpec((ne, 1), lambda i: (0, 0))],
        out_specs=[tok_spec, tok_spec, tok_spec, pl.BlockSpec((ne, LANES), lambda i: (0, 0))],
        out_shape=[tok(I32), tok(F32), tok(I32), jax.ShapeDtypeStruct((ne, LANES), F32)],
        scratch_shapes=[pltpu.VMEM((ne, LANES), F32), pltpu.VMEM((tr, tr), BF16)],
        compiler_params=pltpu.CompilerParams(dimension_semantics=("arbitrary",)),
        name="route",
    )(logits_t, bias_col)


def _dest_body(cnt_ref, e_ref, r_ref, d_ref, ps, *, ne, tr):
    shift = EXPERT_ROWS.bit_length() - 1

    @pl.when(pl.program_id(0) == 0)
    def _():
        def step(j, start):
            ps[j] = start
            return start + lax.shift_left(lax.shift_right_logical(cnt_ref[j] + (EXPERT_ROWS - 1), shift), shift)

        lax.fori_loop(0, ne, step, jnp.int32(0))

    e = e_ref[...]
    base = jnp.zeros(e.shape, I32)
    for j in range(ne):
        base = jnp.where(e == j, ps[j], base)
    dest = base + r_ref[...]
    for m in range(tr // TOKEN_TILE):
        d_ref[SUBLANES * m:SUBLANES * (m + 1), :] = dest[:, TOKEN_TILE * m:TOKEN_TILE * (m + 1)]


def _dest_rows(counts, eidx, rank):
    rows, t = eidx.shape
    ne = counts.shape[0]
    tr = _token_tile(t, DEST_TILE)
    spec = pl.BlockSpec((rows, tr), lambda i, cnt: (0, i))
    return pl.pallas_call(
        functools.partial(_dest_body, ne=ne, tr=tr),
        grid_spec=pltpu.PrefetchScalarGridSpec(
            num_scalar_prefetch=1, grid=(t // tr,), in_specs=[spec, spec],
            out_specs=pl.BlockSpec((tr // TOKEN_TILE * rows, TOKEN_TILE), lambda i, cnt: (i, 0)),
            scratch_shapes=[pltpu.SMEM((ne,), I32)]),
        out_shape=jax.ShapeDtypeStruct((t // TOKEN_TILE * rows, TOKEN_TILE), I32),
        compiler_params=pltpu.CompilerParams(dimension_semantics=("arbitrary",)),
        name="dest",
    )(counts, eidx, rank)


def _sc_mesh():
    return plsc.VectorSubcoreMesh(core_axis_name="c", subcore_axis_name="s")


def _sc_worker_id():
    return lax.axis_index("s") * SC_CORES + lax.axis_index("c")


def _index_block(dest_ref, chunk, width):
    per_tile = TOKEN_TILE // width
    return dest_ref.at[chunk // per_tile, :, pl.ds((chunk % per_tile) * width, width)]


def _dispatch(hps, dest3, n_rows):
    w = SCATTER_ROWS
    width = hps[0].shape[1]
    bounds = [0]
    for h in hps:
        bounds.append(bounds[-1] + h.shape[0] // w)
    nch = bounds[-1]
    nsrc = len(hps)

    @functools.partial(
        pl.kernel, mesh=_sc_mesh(),
        out_type=jax.ShapeDtypeStruct((n_rows, width), U32),
        scratch_types=[pltpu.VMEM((2, SUBLANES, w), I32), pltpu.VMEM((2, w, width), U32),
                       pltpu.SemaphoreType.DMA, pltpu.SemaphoreType.DMA, pltpu.SemaphoreType.DMA],
        compiler_params=pltpu.CompilerParams(use_tc_tiling_on_sc=False),
        name="dispatch",
    )
    def run(*refs):
        src_refs, dest_ref, xs_ref = refs[:nsrc], refs[nsrc], refs[nsrc + 1]
        idx_v, rows_v = refs[nsrc + 2:nsrc + 4]
        sem_load = refs[nsrc + 4:nsrc + 6]
        sem_scatter = refs[nsrc + 6]
        wid = _sc_worker_id()

        def start_loads(c, slot):
            pltpu.async_copy(_index_block(dest_ref, c, w), idx_v.at[slot], sem_load[slot])
            for i, src in enumerate(src_refs):
                @pl.when(jnp.logical_and(c >= bounds[i], c < bounds[i + 1]))
                def _(src=src, lo=bounds[i]):
                    pltpu.async_copy(src.at[pl.ds((c - lo) * w, w)], rows_v.at[slot], sem_load[slot])

        def wait_loads(slot):
            pltpu.make_async_copy(_index_block(dest_ref, 0, w), idx_v.at[slot], sem_load[slot]).wait()
            pltpu.make_async_copy(src_refs[0].at[pl.ds(0, w)], rows_v.at[slot], sem_load[slot]).wait()

        @pl.when(wid < nch)
        def _():
            start_loads(wid, 0)

        @pl.loop(0, pl.cdiv(pl.cdiv(nch, SC_WORKERS), 2))
        def _(rr):
            for slot in range(2):
                c = (rr * 2 + slot) * SC_WORKERS + wid

                @pl.when(c < nch)
                def _(c=c, slot=slot):
                    wait_loads(slot)

                    @pl.when(c + SC_WORKERS < nch)
                    def _():
                        start_loads(c + SC_WORKERS, 1 - slot)

                    copies = [pltpu.async_copy(rows_v.at[slot], xs_ref.at[idx_v.at[slot, k]], sem_scatter)
                              for k in range(TOP_K)]
                    for cp in copies:
                        cp.wait()

    return run(*hps, dest3)


def _undispatch(ys, dest3, n_tokens, after=None):
    w = GATHER_ROWS
    width = ys.shape[1]
    nch = n_tokens // w

    @functools.partial(
        pl.kernel, mesh=_sc_mesh(),
        out_type=jax.ShapeDtypeStruct((TOP_K, n_tokens, width), U32),
        scratch_types=[pltpu.VMEM((2, SUBLANES, w), I32), pltpu.VMEM((2, TOP_K, w, width), U32),
                       pltpu.SemaphoreType.DMA, pltpu.SemaphoreType.DMA, pltpu.SemaphoreType.DMA],
        compiler_params=pltpu.CompilerParams(use_tc_tiling_on_sc=False),
        name="undispatch",
    )
    def run(*refs):
        ys_ref, dest_ref = refs[:2]
        z_ref, idx_v, bufs = refs[-6:-3]
        sem_gather = refs[-3:-1]
        sem_store = refs[-1]
        wid = _sc_worker_id()

        def start_gathers(c, slot):
            pltpu.sync_copy(_index_block(dest_ref, c, w), idx_v.at[slot])
            for k in range(TOP_K):
                pltpu.async_copy(ys_ref.at[idx_v.at[slot, k]], bufs.at[slot, k], sem_gather[slot])

        def wait_gathers(slot):
            for k in range(TOP_K):
                pltpu.make_async_copy(ys_ref.at[idx_v.at[slot, k]], bufs.at[slot, k], sem_gather[slot]).wait()

        @pl.when(wid < nch)
        def _():
            start_gathers(wid, 0)

        @pl.loop(0, pl.cdiv(pl.cdiv(nch, SC_WORKERS), 2))
        def _(rr):
            for slot in range(2):
                c = (rr * 2 + slot) * SC_WORKERS + wid

                @pl.when(c < nch)
                def _(c=c, slot=slot):
                    @pl.when(c + SC_WORKERS < nch)
                    def _():
                        start_gathers(c + SC_WORKERS, 1 - slot)

                    wait_gathers(slot)
                    stores = [pltpu.async_copy(bufs.at[slot, k], z_ref.at[k, pl.ds(c * w, w)], sem_store)
                              for k in range(TOP_K)]
                    for cp in stores:
                        cp.wait()

    return run(ys, dest3) if after is None else run(ys, dest3, after)


PART_SHIFT = 24


def _expert_body(*refs, tm, ne, nparts):
    cnt_refs = refs[:nparts]
    xs_refs = refs[nparts:2 * nparts]
    w1_ref, w3_ref, w2_ref = refs[2 * nparts:2 * nparts + 3]
    ys_refs = refs[2 * nparts + 3:3 * nparts + 3]
    (w1f, w3f, w2f, w1s, w3s, w2s, xbuf, ybuf, xlo, xhi, sched, sem_x, sem_y, sem_w) = refs[3 * nparts + 3:]
    blk_words = tm * ROW_WORDS
    half = ROW_WORDS * LANES
    shift = tm.bit_length() - 1

    def n_blocks_of(p, e):
        return lax.shift_right_logical(cnt_refs[p][e] + (tm - 1), shift)

    def n_all(e):
        n = n_blocks_of(0, e)
        for p in range(1, nparts):
            n = n + n_blocks_of(p, e)
        return n

    def next_nonempty(e):
        return lax.while_loop(
            lambda c: jnp.logical_and(c < ne, n_all(jnp.minimum(c, ne - 1)) == 0), lambda c: c + 1, e)

    def plan(e, carry):
        i, starts = carry[0], list(carry[1:])
        for p in range(nparts):
            n = n_blocks_of(p, e)

            def put(j, c, p=p, i=i, start=starts[p]):
                sched[i + j] = (start + j) + (p << PART_SHIFT)
                return c

            lax.fori_loop(0, n, put, 0)
            i = i + n
            starts[p] = starts[p] + n
        return (i, *starts)

    nu = lax.fori_loop(0, ne, plan, (jnp.int32(0),) * (nparts + 1))[0]

    def rows_of(code):
        blk = code & ((1 << PART_SHIFT) - 1)
        return pl.ds(pl.multiple_of(blk * blk_words, blk_words), blk_words)

    def x_start(i, slot):
        code = sched[i]
        for p in range(nparts):
            @pl.when(lax.shift_right_logical(code, PART_SHIFT) == p)
            def _(p=p):
                pltpu.make_async_copy(xs_refs[p].at[rows_of(code), :], xbuf.at[slot], sem_x.at[slot]).start()

    def x_wait(slot):
        pltpu.make_async_copy(xs_refs[0].at[pl.ds(0, blk_words), :], xbuf.at[slot], sem_x.at[slot]).wait()

    def y_start(i, slot):
        code = sched[i]
        for p in range(nparts):
            @pl.when(lax.shift_right_logical(code, PART_SHIFT) == p)
            def _(p=p):
                pltpu.make_async_copy(ybuf.at[slot], ys_refs[p].at[rows_of(code), :], sem_y.at[slot]).start()

    def y_wait(slot):
        pltpu.make_async_copy(ybuf.at[slot], ys_refs[0].at[pl.ds(0, blk_words), :], sem_y.at[slot]).wait()

    def w_copies(e, ws):
        return [pltpu.make_async_copy(src.at[e], dst.at[ws], sem_w.at[ws])
                for src, dst in ((w1_ref, w1f), (w3_ref, w3f), (w2_ref, w2f))]

    for q in range(X_AHEAD):
        @pl.when(q < nu)
        def _(q=q):
            x_start(q, q)

    e_first = next_nonempty(jnp.int32(0))

    @pl.when(e_first < ne)
    def _():
        for cp in w_copies(e_first, 0):
            cp.start()

    def blocks(i, nblk):
        for b in range(nblk):
            x_wait((i + b) % X_SLOTS)
        for b in range(nblk):
            nxt = i + b + X_AHEAD

            @pl.when(nxt < nu)
            def _(nxt=nxt):
                x_start(nxt, nxt % X_SLOTS)

        for b in range(nblk):
            @pl.when(i + b >= Y_SLOTS)
            def _(b=b):
                y_wait((i + b) % Y_SLOTS)

        for b in range(nblk):
            xin = xbuf.at[(i + b) % X_SLOTS]
            for w in range(ROW_WORDS):
                lo, hi = _unpack_halves(_load_row_word(xin, w, tm))
                xlo[b, :, LANES * w:LANES * (w + 1)] = lo.astype(BF16)
                xhi[b, :, LANES * w:LANES * (w + 1)] = hi.astype(BF16)

        def up(b, wsc):
            return (jnp.dot(xlo[b], wsc[0:half, :], preferred_element_type=F32)
                    + jnp.dot(xhi[b], wsc[half:2 * half, :], preferred_element_type=F32))

        for b in range(nblk):
            hid = (_silu(up(b, w1s)) * up(b, w3s)).astype(BF16)
            y = jnp.dot(hid, w2s[...], preferred_element_type=F32)
            _store_rows(ybuf.at[(i + b) % Y_SLOTS], _pack_halves(y[:, 0:half], y[:, half:2 * half]))
        for b in range(nblk):
            y_start(i + b, (i + b) % Y_SLOTS)

    def per_expert(e, carry):
        i0, ws = carry
        n = n_all(e)

        @pl.when(n > 0)
        def _():
            for cp in w_copies(e, ws):
                cp.wait()
            w1s[...] = w1f[ws].astype(BF16)
            w3s[...] = w3f[ws].astype(BF16)
            w2s[...] = w2f[ws].astype(BF16)
            e_next = next_nonempty(e + 1)

            @pl.when(e_next < ne)
            def _():
                for cp in w_copies(e_next, 1 - ws):
                    cp.start()

            def pair(j, c):
                blocks(i0 + 2 * j, 2)
                return c

            lax.fori_loop(0, lax.shift_right_logical(n, 1), pair, 0)

            @pl.when(n % 2 == 1)
            def _():
                blocks(i0 + n - 1, 1)

        return i0 + n, jnp.where(n > 0, 1 - ws, ws)

    lax.fori_loop(0, ne, per_expert, (jnp.int32(0), jnp.int32(0)))

    for q in range(Y_SLOTS):
        @pl.when(nu > q)
        def _(q=q):
            y_wait((nu - 1 - q) % Y_SLOTS)


def _experts(counts, xss, w1e, w3e, w2e):
    tm = EXPERT_ROWS
    ne, d, de = w1e.shape
    half = ROW_WORDS * LANES
    nparts = len(xss)
    cap = sum(x.shape[0] // (tm * ROW_WORDS) for x in xss)
    assert cap < (1 << PART_SHIFT)
    anyspec = pl.BlockSpec(memory_space=pl.ANY)
    blk_buf = lambda n: pltpu.VMEM((n, tm * ROW_WORDS, LANES), U32)
    return pl.pallas_call(
        functools.partial(_expert_body, tm=tm, ne=ne, nparts=nparts),
        grid_spec=pltpu.PrefetchScalarGridSpec(
            num_scalar_prefetch=nparts,
            grid=(1,),
            in_specs=[anyspec] * (nparts + 3),
            out_specs=[anyspec] * nparts,
            scratch_shapes=[pltpu.VMEM((2, d, de), F32), pltpu.VMEM((2, d, de), F32), pltpu.VMEM((2, de, d), F32),
                            pltpu.VMEM((d, de), BF16), pltpu.VMEM((d, de), BF16), pltpu.VMEM((de, d), BF16),
                            blk_buf(X_SLOTS), blk_buf(Y_SLOTS),
                            pltpu.VMEM((2, tm, half), BF16), pltpu.VMEM((2, tm, half), BF16),
                            pltpu.SMEM((cap,), I32),
                            pltpu.SemaphoreType.DMA((X_SLOTS,)), pltpu.SemaphoreType.DMA((Y_SLOTS,)),
                            pltpu.SemaphoreType.DMA((2,))]),
        out_shape=[jax.ShapeDtypeStruct(x.shape, U32) for x in xss],
        compiler_params=pltpu.CompilerParams(dimension_semantics=("arbitrary",), vmem_limit_bytes=VMEM_LIMIT),
        name="experts",
    )(*counts, *xss, w1e, w3e, w2e)


def _combine_body(z_ref, x1_ref, scf_ref, shf_ref, gtf_ref, wt_ref, gffn_ref, gfin_ref, ws1_ref, ws3_ref, ws2_ref,
                  y_ref, xo, *, td, topk):
    half = ROW_WORDS * LANES
    hb = _modulated_rmsnorm(x1_ref[...], gffn_ref[...], scf_ref[...], shf_ref[...]).astype(BF16)
    up = lambda w_ref: jnp.dot(hb, w_ref[...], preferred_element_type=F32)
    xo[...] = jnp.dot((_silu(up(ws1_ref)) * up(ws3_ref)).astype(BF16), ws2_ref[...], preferred_element_type=F32)
    wt = wt_ref[...].T
    ws = [wt[:, k:k + 1] for k in range(topk)]
    sq = jnp.zeros((td, 1), F32)
    for j in range(ROW_WORDS):
        acc_lo = jnp.zeros((td, LANES), F32)
        acc_hi = jnp.zeros((td, LANES), F32)
        for k in range(topk):
            lo, hi = _unpack_halves(_load_row_word(z_ref.at[k], j, td))
            acc_lo = acc_lo + ws[k] * lo
            acc_hi = acc_hi + ws[k] * hi
        for base, acc in ((0, acc_lo), (half, acc_hi)):
            cols = slice(base + LANES * j, base + LANES * (j + 1))
            x = x1_ref[:, cols] + gtf_ref[:, cols] * (acc + xo[:, cols])
            xo[:, cols] = x
            sq = sq + jnp.sum(x * x, axis=-1, keepdims=True)
    rs = lax.rsqrt(sq / (2 * half) + EPS)
    y_ref[...] = xo[...] * rs * gfin_ref[...]


def _combine_body_into(*refs, td, topk):
    _combine_body(*refs[:11], *refs[12:], td=td, topk=topk)


def _combine(z, token0, wts, x1, mod, g_ffn, g_final, shared_w, *, rows_per_mod, per_row_mod, b0=0, out_rows=None,
             into=None):
    t, d = x1.shape
    td = min(t, COMBINE_TILE)
    tile0 = token0 // td
    out_rows = t if out_rows is None else out_rows
    if per_row_mod:
        mod_spec = lambda j: pl.BlockSpec((td, d), lambda i: (i, j))
        out0 = 0
    else:
        tiles_per_mod = rows_per_mod // td
        mod_spec = lambda j: pl.BlockSpec((None, None, 1, d), lambda i: (i // tiles_per_mod + b0, j, 0, 0))
        out0 = b0 * tiles_per_mod
    full = lambda a: pl.BlockSpec(a.shape, lambda i: (0,) * a.ndim)
    in_specs = [pl.BlockSpec((TOP_K, td * ROW_WORDS, LANES), lambda i: (0, i + tile0, 0)),
                pl.BlockSpec((td, d), lambda i: (i, 0)),
                mod_spec(4), mod_spec(3), mod_spec(5),
                pl.BlockSpec((SUBLANES, td), lambda i: (0, i + tile0)),
                full(g_ffn), full(g_final)] + [full(w) for w in shared_w]
    args = [z, x1, mod, mod, mod, wts, g_ffn, g_final, *shared_w]
    body, aliases = _combine_body, {}
    if into is not None:
        in_specs.append(pl.BlockSpec(memory_space=pl.ANY))
        args.append(into)
        body, aliases = _combine_body_into, {len(args) - 1: 0}
    return pl.pallas_call(
        functools.partial(body, td=td, topk=TOP_K),
        grid=(t // td,),
        in_specs=in_specs,
        out_specs=pl.BlockSpec((td, d), lambda i: (i + out0, 0)),
        scratch_shapes=[pltpu.VMEM((td, d), F32)],
        out_shape=jax.ShapeDtypeStruct((out_rows, d), F32),
        input_output_aliases=aliases,
        compiler_params=pltpu.CompilerParams(vmem_limit_bytes=VMEM_LIMIT),
        name="combine",
    )(*args)


def _log_gamma(nh):
    return np.log(1.0 - 2.0 ** (-5.0 - np.arange(nh, dtype=np.float32))).astype(np.float32)


def _retention_tables(length, nh, hd):
    c = math.gcd(length, RET_CHUNK)
    log_g = _log_gamma(nh)
    idx = np.arange(c, dtype=np.float32)
    rel = idx[:, None] - idx[None, :]
    mask = np.where(rel >= 0, np.exp(log_g[:, None, None] * np.maximum(rel, 0.0)), 0.0).astype(np.float32)
    q_decay = np.exp(log_g[None, :] * (idx[:, None] + 1.0)).astype(np.float32)
    k_decay = np.exp(log_g[None, :] * (c - 1.0 - idx[:, None])).astype(np.float32)
    chunk_decay = np.exp(log_g * np.float32(c)).astype(np.float32)
    qd = np.broadcast_to(q_decay.T[:, :, None], (nh, c, hd))
    kd = np.broadcast_to(k_decay.T[:, :, None], (nh, c, hd))
    cd = np.broadcast_to(chunk_decay[:, None, None], (nh, hd, hd))
    return tuple(jnp.asarray(t) for t in (mask, qd, kd, cd))


def kernel(x_prompt, x_sample, c_prompt, c_sample, state_conv, state_ret, w_ada, b_ada, g_mix, g_ffn, w_in,
           conv_w, conv_b, conv_norm_g, conv_norm_b, ret_norm_g, ret_norm_b, w_out, w_router, router_bias,
           w1, w3, w2, ws1, ws3, ws2, g_final):
    depth = w_ada.shape[0]
    assert depth == 1, "single-layer trunk"
    bp, lp, d = x_prompt.shape
    bs, ls, _ = x_sample.shape
    assert ls == 1
    dc = conv_w.shape[2]
    dr = ret_norm_g.shape[1]
    nh = RET_HEADS
    hd = dr // nh
    assert hd == LANES and lp % 256 == 0 and bs % TOKEN_TILE == 0 and d // 2 == ROW_WORDS * LANES
    ne = w_router.shape[2]
    row = lambda a: a.reshape(1, -1)

    mod_p, mod_s = _ada([c_prompt, c_sample], w_ada[0], row(b_ada[0]))
    mod_p = mod_p.reshape(bp, 6, 1, d)

    half = hd // 2
    inv = (np.float32(ROPE_BASE) ** (-np.arange(half, dtype=np.float32) / np.float32(half))).astype(np.float32)
    inv2 = np.concatenate([inv, inv]).reshape(1, hd)
    cos_p, sin_p = _rope_tables(inv2, lp, 0)
    cos_s, sin_s = _rope_tables(inv2, SUBLANES, PAST_LEN)

    w_in_b = w_in[0].astype(BF16)
    wo_b = w_out[0].astype(BF16)
    wr_t = w_router[0].T
    wrh = wr_t.astype(BF16)
    wrl = (wr_t - wrh.astype(F32)).astype(BF16)
    ws1_b, ws3_b, ws2_b = ws1[0].astype(BF16), ws3[0].astype(BF16), ws2[0].astype(BF16)
    dims = dict(dc=dc, dr=dr, hd=hd)

    tables = _retention_tables(lp, nh, hd)
    gam = jnp.asarray(np.broadcast_to(np.exp(_log_gamma(nh))[:, None, None], (nh, SUBLANES, hd)))
    norm_rows = (row(conv_b[0]), row(conv_norm_g[0]), row(conv_norm_b[0]), row(ret_norm_g[0]), row(ret_norm_b[0]))
    post_w = (row(g_ffn[0]), wo_b, wrh, wrl)
    shared_w = (ws1_b, ws3_b, ws2_b)
    bias_col = router_bias[0].reshape(ne, 1)
    hw = d // 2
    tm = EXPERT_ROWS

    def pre_prompt(b0, nb, after=None):
        glu, q, k, v, sg = _proj(x_prompt, mod_p, row(g_mix[0]), w_in_b, cos_p, sin_p,
                                 per_row_mod=False, b0=b0, nb=nb, after=after, **dims)
        cat, ret = _mix(glu, q, k, v, sg, conv_w[0], *norm_rows, tables, nb=nb, length=lp, **dims)
        x2, hp, lg = _post(cat, x_prompt, mod_p, *post_w, per_row_mod=False, b0=b0, nb=nb)
        return glu, ret, x2, hp, lg

    def pre_sample():
        xs3 = x_sample.reshape(1, bs, d)
        glu, q, k, v, sg = _proj(xs3, mod_s, row(g_mix[0]), w_in_b, cos_s, sin_s, per_row_mod=True, **dims)
        cat, ret, new_conv = _mix1(glu, q, k, v, sg, state_conv.transpose(0, 2, 1, 3), state_ret, conv_w[0],
                                   *norm_rows, gam, **dims)
        x2, hp, lg = _post(cat, xs3, mod_s, *post_w, per_row_mod=True)
        return new_conv.transpose(0, 2, 1, 3), ret, x2, hp, lg

    def route_and_dispatch(hps, lgs):
        lg = lgs[0] if len(lgs) == 1 else jnp.concatenate(lgs, axis=1)
        tokens = lg.shape[1]
        eidx, wts, rank, cnt = _route(lg, bias_col)
        counts = cnt[:, 0].astype(I32)
        n_rows = -(-(tokens * TOP_K + ne * (tm - 1)) // tm) * tm
        dest3 = _dest_rows(counts, eidx, rank).reshape(tokens // TOKEN_TILE, SUBLANES, TOKEN_TILE)
        xs = _dispatch([h.reshape(-1, hw) for h in hps], dest3, n_rows)
        return counts, dest3, wts, xs.reshape(n_rows * ROW_WORDS, LANES)

    def undispatch(ys, dest3, after=None):
        tokens = dest3.shape[0] * TOKEN_TILE
        z = _undispatch(ys.reshape(-1, hw), dest3, tokens, after)
        return z.reshape(TOP_K, tokens * ROW_WORDS, LANES)

    nb0 = bp // 2
    nb1 = bp - nb0
    new_conv_s, ret_s, x2_s, hp_s, lg_s = pre_sample()
    glu_0, ret_0, x2_0, hp_0, lg_0 = pre_prompt(0, nb0, after=lg_s)
    counts_0, dest_0, wts_0, xs_0 = route_and_dispatch([hp_0], [lg_0])
    glu_1, ret_1, x2_1, hp_1, lg_1 = pre_prompt(nb0, nb1, after=dest_0)
    counts_1, dest_1, wts_1, xs_1 = route_and_dispatch([hp_1, hp_s], [lg_1, lg_s])
    ys_0, ys_1 = _experts((counts_0, counts_1), (xs_0, xs_1), w1[0], w3[0], w2[0])
    z_0 = undispatch(ys_0, dest_0)
    z_1 = undispatch(ys_1, dest_1, after=z_0[0, :SUBLANES])
    y_p = _combine(z_0, 0, wts_0, x2_0, mod_p, row(g_ffn[0]), row(g_final), shared_w, rows_per_mod=lp, per_row_mod=False,
                   out_rows=bp * lp)
    y_p = _combine(z_1, 0, wts_1, x2_1, mod_p, row(g_ffn[0]), row(g_final), shared_w, rows_per_mod=lp, per_row_mod=False,
                   b0=nb0, out_rows=bp * lp, into=y_p)
    y_s = _combine(z_1, nb1 * lp, wts_1, x2_s, mod_s, row(g_ffn[0]), row(g_final), shared_w, rows_per_mod=bs,
                   per_row_mod=True)
    ret_p = jnp.concatenate([ret_0, ret_1], axis=0)

    tail = lambda g, n: g.reshape(n, lp, dc)[:, lp - CONV_BUF:, :]
    new_conv_p = jnp.concatenate([tail(glu_0, nb0), tail(glu_1, nb1)], axis=0)
    return (y_p.reshape(bp, lp, d), y_s.reshape(bs, ls, d), new_conv_p[None], ret_p[None],
            new_conv_s, ret_s)
```

```python
import functools
import math

import jax
import jax.numpy as jnp
import numpy as np
from jax import lax
from jax.experimental import pallas as pl
from jax.experimental.pallas import tpu as pltpu
from jax.experimental.pallas import tpu_sc as plsc

F32 = jnp.float32
BF16 = jnp.bfloat16
U32 = jnp.uint32
I32 = jnp.int32

EPS = 1e-6
PAST_LEN = 16384
RET_HEADS = 4
RET_CHUNK = 128
CONV_WIDTH = 31
CONV_BUF = CONV_WIDTH - 1
ROPE_BASE = 10000.0
TOP_K = 6
N_GROUPS = 8
TOPK_GROUPS = 4
ROUTED_SCALE = 2.5

LANES = 128
SUBLANES = 8
CONV_PAD = 32
EXPERT_ROWS = 256
ROUTE_TILE = 1024
TOKEN_TILE = 128
COMBINE_TILE = 512
VMEM_LIMIT = 56 * 1024 * 1024
SC_CORES = 2
SC_SUBCORES = 16
SC_WORKERS = SC_CORES * SC_SUBCORES
SCATTER_ROWS = 64
GATHER_ROWS = 16
ROW_WORDS = 4
X_SLOTS = 6
X_AHEAD = 4
Y_SLOTS = 4

HI_MASK = 0xFFFF0000


def _sigmoid(x):
    return jax.nn.sigmoid(x)


def _silu(x):
    return x * jax.nn.sigmoid(x)


def _pack_halves(lo, hi):
    lo_u = lax.bitcast_convert_type(lo.astype(BF16).astype(F32), U32) >> 16
    hi_u = lax.bitcast_convert_type(hi.astype(BF16).astype(F32), U32) & jnp.uint32(HI_MASK)
    return hi_u | lo_u


def _unpack_halves(p):
    lo = lax.bitcast_convert_type(p << 16, F32)
    hi = lax.bitcast_convert_type(p & jnp.uint32(HI_MASK), F32)
    return lo, hi


def _store_rows(ref, x):
    rows = x.shape[0]
    for j in range(ROW_WORDS):
        ref[pl.ds(j, rows, stride=ROW_WORDS), :] = x[:, LANES * j:LANES * (j + 1)]


def _load_row_word(ref, j, rows):
    return ref[pl.ds(j, rows, stride=ROW_WORDS), :]


def _ada_body(w_ref, b_ref, *refs):
    w = w_ref[...].astype(BF16)
    groups = len(refs) // 2
    for c_ref, o_ref in zip(refs[:groups], refs[groups:]):
        o_ref[...] = jnp.dot(_silu(c_ref[...]).astype(BF16), w, preferred_element_type=F32) + b_ref[...]


def _ada(cs, w_ada, b_ada):
    d, n = w_ada.shape
    tn = 2048
    return pl.pallas_call(
        _ada_body,
        grid=(n // tn,),
        in_specs=[pl.BlockSpec((d, tn), lambda j: (0, j)), pl.BlockSpec((1, tn), lambda j: (0, j))]
        + [pl.BlockSpec(c.shape, lambda j: (0, 0)) for c in cs],
        out_specs=[pl.BlockSpec((c.shape[0], tn), lambda j: (0, j)) for c in cs],
        out_shape=[jax.ShapeDtypeStruct((c.shape[0], n), F32) for c in cs],
        compiler_params=pltpu.CompilerParams(vmem_limit_bytes=VMEM_LIMIT),
        name="ada",
    )(w_ada, b_ada, *cs)


def _rope_tables(inv2, rows, pos0):
    pos = np.arange(rows, dtype=np.float32) + np.float32(pos0)
    ang = (pos[:, None] * inv2).astype(np.float64)
    sign = np.where(np.arange(inv2.shape[1]) < inv2.shape[1] // 2, -1.0, 1.0)
    return jnp.asarray(np.cos(ang).astype(np.float32)), jnp.asarray((np.sin(ang) * sign).astype(np.float32))


def _modulated_rmsnorm(x, g, sc, sh):
    ms = jnp.mean(x * x, axis=-1, keepdims=True)
    h = x * lax.rsqrt(ms + EPS) * g
    return h * (1.0 + sc) + sh


def _proj_body(x_ref, sh_ref, sc_ref, g_ref, w_ref, cos_ref, sin_ref, *rest, dc, dr, hd, rope_rows):
    glu_ref, q_ref, k_ref, v_ref, sg_ref = rest[-5:]
    hb = _modulated_rmsnorm(x_ref[...], g_ref[...], sc_ref[...], sh_ref[...]).astype(BF16)

    def proj(lo, n):
        return jnp.dot(hb, w_ref[:, lo:lo + n], preferred_element_type=F32)

    glu_ref[...] = proj(0, dc) * _sigmoid(proj(dc, dc))
    cos = cos_ref[...] if rope_rows else cos_ref[0:1, :]
    sin = sin_ref[...] if rope_rows else sin_ref[0:1, :]
    for ref, lo, scale in ((q_ref, 2 * dc, hd ** -0.5), (k_ref, 2 * dc + dr, None)):
        t = proj(lo, dr)
        for hh in range(dr // hd):
            th = t[:, hh * hd:(hh + 1) * hd]
            r = th * cos + pltpu.roll(th, hd // 2, 1) * sin
            if scale is not None:
                r = r * scale
            ref[:, hh * hd:(hh + 1) * hd] = r.astype(BF16)
    v_ref[...] = proj(2 * dc + 2 * dr, dr).astype(BF16)
    sg_ref[...] = _silu(proj(2 * dc + 3 * dr, dr)).astype(BF16)


def _proj(x, mod, g_mix, w_in_b, cos2, sin2, *, dc, dr, hd, per_row_mod, b0=0, nb=None, after=None):
    nb_all, length, d = x.shape
    nb = nb_all if nb is None else nb
    tl = min(length, 512)
    x2 = x.reshape(nb_all * length, d)
    nl = length // tl
    if per_row_mod:
        mod_spec = lambda j: pl.BlockSpec((tl, d), lambda b, l: (l, j))
        rope_spec = pl.BlockSpec((SUBLANES, LANES), lambda b, l: (0, 0))
    else:
        mod_spec = lambda j: pl.BlockSpec((None, None, 1, d), lambda b, l: (b + b0, j, 0, 0))
        rope_spec = pl.BlockSpec((tl, LANES), lambda b, l: (l, 0))
    row_spec = lambda w: pl.BlockSpec((tl, w), lambda b, l: (b * nl + l, 0))
    t = nb * length
    outs = pl.pallas_call(
        functools.partial(_proj_body, dc=dc, dr=dr, hd=hd, rope_rows=not per_row_mod),
        grid=(nb, nl),
        in_specs=[
            pl.BlockSpec((tl, d), lambda b, l: ((b + b0) * nl + l, 0)), mod_spec(0), mod_spec(1),
            pl.BlockSpec((1, d), lambda b, l: (0, 0)),
            pl.BlockSpec(w_in_b.shape, lambda b, l: (0, 0)),
            rope_spec, rope_spec,
        ] + ([] if after is None else [pl.BlockSpec(memory_space=pl.ANY)]),
        out_specs=[row_spec(dc), row_spec(dr), row_spec(dr), row_spec(dr), row_spec(dr)],
        out_shape=[
            jax.ShapeDtypeStruct((t, dc), F32),
            jax.ShapeDtypeStruct((t, dr), BF16),
            jax.ShapeDtypeStruct((t, dr), BF16),
            jax.ShapeDtypeStruct((t, dr), BF16),
            jax.ShapeDtypeStruct((t, dr), BF16),
        ],
        compiler_params=pltpu.CompilerParams(vmem_limit_bytes=VMEM_LIMIT),
        name="proj",
    )(x2, mod, mod, g_mix, w_in_b, cos2, sin2, *(() if after is None else (after,)))
    return outs


def _layernorm_silu(c, g, b):
    mu = jnp.mean(c, axis=-1, keepdims=True)
    d = c - mu
    var = jnp.mean(d * d, axis=-1, keepdims=True)
    return _silu(d * lax.rsqrt(var + EPS) * g + b)


def _groupnorm(o, g, b):
    mu = jnp.mean(o, axis=-1, keepdims=True)
    d = o - mu
    var = jnp.mean(d * d, axis=-1, keepdims=True)
    return d * lax.rsqrt(var + EPS) * g + b


def _mix_body(glu_ref, q_ref, k_ref, v_ref, sg_ref, cw_ref, cb_ref, lng_ref, lnb_ref, rg_ref, rb_ref,
              mask_ref, qd_ref, kd_ref, cd_ref, cat_ref, st_ref, buf, cscr, *, tl, dc, hd, nh, chunk):
    nslab = dc // LANES

    @pl.when(pl.program_id(1) == 0)
    def _():
        buf[:, 0:CONV_PAD, :] = jnp.zeros((nslab, CONV_PAD, LANES), F32)
        st_ref[...] = jnp.zeros(st_ref.shape, F32)

    for j in range(nslab):
        buf[j, CONV_PAD:CONV_PAD + tl, :] = glu_ref[:, LANES * j:LANES * (j + 1)]
    first = CONV_PAD - CONV_BUF
    rows_per_iter = 8 * SUBLANES
    for j in range(nslab):
        cols = slice(LANES * j, LANES * (j + 1))
        wv = [jnp.broadcast_to(cw_ref[t:t + 1, cols], (SUBLANES, LANES)) for t in range(CONV_WIDTH)]
        bias = jnp.broadcast_to(cb_ref[0:1, cols], (SUBLANES, LANES))

        def body(r, carry, j=j, cols=cols, wv=wv, bias=bias):
            base = pl.multiple_of(r * rows_per_iter, rows_per_iter)
            for u in range(rows_per_iter // SUBLANES):
                acc = bias
                for t in range(CONV_WIDTH):
                    acc = acc + wv[t] * buf[j, pl.ds(base + (u * SUBLANES + first + t), SUBLANES), :]
                cscr[pl.ds(base + u * SUBLANES, SUBLANES), cols] = acc
            return carry

        lax.fori_loop(0, tl // rows_per_iter, body, 0)
    for j in range(nslab):
        buf[j, 0:CONV_PAD, :] = buf[j, tl:tl + CONV_PAD, :]
    cat_ref[:, 0:dc] = _layernorm_silu(cscr[...], lng_ref[...], lnb_ref[...]).astype(BF16)

    nt = (((1,), (1,)), ((), ()))
    tn = (((0,), (0,)), ((), ()))
    for c in range(tl // chunk):
        rows = slice(c * chunk, (c + 1) * chunk)
        for hh in range(nh):
            cols = slice(hh * hd, (hh + 1) * hd)
            qh = q_ref[rows, cols]
            kh = k_ref[rows, cols]
            vh = v_ref[rows, cols]
            s = st_ref[0, hh]
            scores = lax.dot_general(qh, kh, nt, preferred_element_type=F32) * mask_ref[hh]
            inner = jnp.dot(scores.astype(BF16), vh, preferred_element_type=F32)
            qd = (qh.astype(F32) * qd_ref[hh]).astype(BF16)
            cross = jnp.dot(qd, s.astype(BF16), preferred_element_type=F32)
            kd = (kh.astype(F32) * kd_ref[hh]).astype(BF16)
            st_ref[0, hh] = cd_ref[hh] * s + lax.dot_general(kd, vh, tn, preferred_element_type=F32)
            o = _groupnorm(inner + cross, rg_ref[0:1, cols], rb_ref[0:1, cols])
            cat_ref[rows, dc + hh * hd:dc + (hh + 1) * hd] = (o * sg_ref[rows, cols].astype(F32)).astype(BF16)


def _mix(glu, q, k, v, sg, conv_w, conv_b, ln_g, ln_b, rg, rb, tables, *, nb, length, dc, dr, hd):
    nh = dr // hd
    chunk = math.gcd(length, RET_CHUNK)
    tl = min(length, 256)
    nl = length // tl
    mask, qd, kd, cd = tables
    row_spec = lambda w: pl.BlockSpec((tl, w), lambda b, l: (b * nl + l, 0))
    full = lambda a: pl.BlockSpec(a.shape, lambda b, l: (0,) * a.ndim)
    cat, st = pl.pallas_call(
        functools.partial(_mix_body, tl=tl, dc=dc, hd=hd, nh=nh, chunk=chunk),
        grid=(nb, nl),
        in_specs=[row_spec(dc), row_spec(dr), row_spec(dr), row_spec(dr), row_spec(dr),
                  full(conv_w), full(conv_b), full(ln_g), full(ln_b), full(rg), full(rb),
                  full(mask), full(qd), full(kd), full(cd)],
        out_specs=[row_spec(dc + dr), pl.BlockSpec((1, nh, hd, hd), lambda b, l: (b, 0, 0, 0))],
        out_shape=[jax.ShapeDtypeStruct((nb * length, dc + dr), BF16),
                   jax.ShapeDtypeStruct((nb, nh, hd, hd), F32)],
        scratch_shapes=[pltpu.VMEM((dc // LANES, tl + CONV_PAD, LANES), F32),
                        pltpu.VMEM((tl, dc), F32)],
        compiler_params=pltpu.CompilerParams(dimension_semantics=("arbitrary", "arbitrary")),
        name="mix",
    )(glu, q, k, v, sg, conv_w, conv_b, ln_g, ln_b, rg, rb, mask, qd, kd, cd)
    return cat, st


def _mix1_body(glu_ref, q_ref, k_ref, v_ref, sg_ref, sc_ref, s0_ref, cw_ref, cb_ref, lng_ref, lnb_ref,
               rg_ref, rb_ref, gam_ref, cat_ref, st_ref, nc_ref, oscr, qf, kf, vf, *, tb, dc, hd, nh):
    glu = glu_ref[...]
    conv = glu * cw_ref[CONV_BUF:CONV_WIDTH, :] + cb_ref[...]
    for r in range(CONV_BUF):
        past = sc_ref[r]
        conv = conv + past * cw_ref[r:r + 1, :]
        if r > 0:
            nc_ref[r - 1] = past
    nc_ref[CONV_BUF - 1] = glu
    cat_ref[:, 0:dc] = _layernorm_silu(conv, lng_ref[...], lnb_ref[...]).astype(BF16)

    tn = (((0,), (0,)), ((), ()))
    rowid = lax.broadcasted_iota(I32, (tb, hd), 0)
    qf[...] = q_ref[...].astype(F32)
    kf[...] = k_ref[...].astype(F32)
    vf[...] = v_ref[...].astype(F32)
    for hh in range(nh):
        cols = slice(hh * hd, (hh + 1) * hd)
        qa = q_ref[:, cols]
        ka = k_ref[:, cols]
        gam = gam_ref[hh, 0:1, :]
        for bb in range(tb):
            onehot = (rowid == bb).astype(BF16)
            qcol = lax.dot_general(qa, onehot, tn, preferred_element_type=F32)
            kcol = lax.dot_general(ka, onehot, tn, preferred_element_type=F32)
            s0 = s0_ref[bb, hh]
            qrow = qf[bb:bb + 1, cols]
            krow = kf[bb:bb + 1, cols]
            vrow = vf[bb:bb + 1, cols]
            qk = jnp.sum(qrow * krow, axis=-1, keepdims=True)
            cross = gam * jnp.sum(qcol * s0, axis=0, keepdims=True)
            st_ref[bb, hh] = gam * s0 + kcol * vrow
            oscr[bb:bb + 1, cols] = qk * vrow + cross
    for hh in range(nh):
        cols = slice(hh * hd, (hh + 1) * hd)
        o = _groupnorm(oscr[:, cols], rg_ref[0:1, cols], rb_ref[0:1, cols])
        cat_ref[:, dc + hh * hd:dc + (hh + 1) * hd] = (o * sg_ref[:, cols].astype(F32)).astype(BF16)


def _mix1(glu, q, k, v, sg, state_conv, state_ret, conv_w, conv_b, ln_g, ln_b, rg, rb, gam, *, dc, dr, hd):
    nb = glu.shape[0]
    nh = dr // hd
    tb = 16
    row_spec = lambda w: pl.BlockSpec((tb, w), lambda i: (i, 0))
    full = lambda a: pl.BlockSpec(a.shape, lambda i: (0,) * a.ndim)
    st_spec = pl.BlockSpec((None, tb, nh, hd, hd), lambda i: (0, i, 0, 0, 0))
    conv_spec = pl.BlockSpec((None, CONV_BUF, tb, dc), lambda i: (0, 0, i, 0))
    cat, st, new_conv = pl.pallas_call(
        functools.partial(_mix1_body, tb=tb, dc=dc, hd=hd, nh=nh),
        grid=(nb // tb,),
        in_specs=[row_spec(dc), row_spec(dr), row_spec(dr), row_spec(dr), row_spec(dr), conv_spec, st_spec,
                  full(conv_w), full(conv_b), full(ln_g), full(ln_b), full(rg), full(rb), full(gam)],
        out_specs=[row_spec(dc + dr), st_spec, conv_spec],
        out_shape=[jax.ShapeDtypeStruct((nb, dc + dr), BF16),
                   jax.ShapeDtypeStruct((1, nb, nh, hd, hd), F32),
                   jax.ShapeDtypeStruct((1, CONV_BUF, nb, dc), F32)],
        scratch_shapes=[pltpu.VMEM((tb, dr), F32)] * 4,
        compiler_params=pltpu.CompilerParams(vmem_limit_bytes=VMEM_LIMIT),
        name="mix1",
    )(glu, q, k, v, sg, state_conv, state_ret, conv_w, conv_b, ln_g, ln_b, rg, rb, gam)
    return cat, st, new_conv


def _post_body(cat_ref, x_ref, gtm_ref, scf_ref, shf_ref, g_ref, wo_ref, wrh_ref, wrl_ref, x1_ref, hp_ref, lg_ref):
    d = x_ref.shape[1]
    y = jnp.dot(cat_ref[...], wo_ref[...], preferred_element_type=F32)
    x1 = x_ref[...] + gtm_ref[...] * y
    x1_ref[...] = x1
    h = _modulated_rmsnorm(x1, g_ref[...], scf_ref[...], shf_ref[...])
    hb = h.astype(BF16)
    _store_rows(hp_ref, _pack_halves(h[:, 0:d // 2], h[:, d // 2:d]))
    hl = (h - hb.astype(F32)).astype(BF16)
    nt = (((1,), (1,)), ((), ()))
    lg_ref[...] = (lax.dot_general(wrh_ref[...], hb, nt, preferred_element_type=F32)
                   + lax.dot_general(wrh_ref[...], hl, nt, preferred_element_type=F32)
                   + lax.dot_general(wrl_ref[...], hb, nt, preferred_element_type=F32))


def _post(cat, x, mod, g_ffn, wo_b, wrh, wrl, *, per_row_mod, b0=0, nb=None):
    nb_all, length, d = x.shape
    nb = nb_all if nb is None else nb
    tl = min(length, 1024)
    nl = length // tl
    t = nb * length
    ne = wrh.shape[0]
    x2d = x.reshape(nb_all * length, d)
    if per_row_mod:
        mod_spec = lambda j: pl.BlockSpec((tl, d), lambda b, l: (l, j))
    else:
        mod_spec = lambda j: pl.BlockSpec((None, None, 1, d), lambda b, l: (b + b0, j, 0, 0))
    row_spec = lambda w: pl.BlockSpec((tl, w), lambda b, l: (b * nl + l, 0))
    full = lambda a: pl.BlockSpec(a.shape, lambda b, l: (0,) * a.ndim)
    return pl.pallas_call(
        _post_body,
        grid=(nb, nl),
        in_specs=[row_spec(d), pl.BlockSpec((tl, d), lambda b, l: ((b + b0) * nl + l, 0)),
                  mod_spec(2), mod_spec(4), mod_spec(3),
                  full(g_ffn), full(wo_b), full(wrh), full(wrl)],
        out_specs=[row_spec(d),
                   pl.BlockSpec((tl * ROW_WORDS, LANES), lambda b, l: (b * nl + l, 0)),
                   pl.BlockSpec((ne, tl), lambda b, l: (0, b * nl + l))],
        out_shape=[jax.ShapeDtypeStruct((t, d), F32),
                   jax.ShapeDtypeStruct((t * ROW_WORDS, LANES), U32),
                   jax.ShapeDtypeStruct((ne, t), F32)],
        compiler_params=pltpu.CompilerParams(vmem_limit_bytes=VMEM_LIMIT),
        name="post",
    )(cat, x2d, mod, mod, mod, g_ffn, wo_b, wrh, wrl)


def _first_max(x, idx, sentinel):
    m = jnp.max(x, axis=0, keepdims=True)
    f = jnp.min(jnp.where(x == m, idx, sentinel), axis=0, keepdims=True)
    return m, f


def _route_body(lg_ref, bias_ref, w_ref, cnt_ref, d_ref, cnt_scr, before, e_all, r_all, *, tr, nt, ne, ng, topk, topg):
    @pl.when(pl.program_id(0) == 0)
    def _():
        cnt_scr[...] = jnp.zeros(cnt_scr.shape, F32)
        t_row = lax.broadcasted_iota(I32, (tr, tr), 0)
        t_col = lax.broadcasted_iota(I32, (tr, tr), 1)
        before[...] = (t_row < t_col).astype(BF16)

    per = ne // ng
    neg = -jnp.inf
    scores = _sigmoid(lg_ref[...])
    sel = scores + bias_ref[...]
    sub = lax.broadcasted_iota(I32, (per, tr), 0)
    gs = []
    for g in range(ng):
        s_g = sel[g * per:(g + 1) * per, :]
        m1, f1 = _first_max(s_g, sub, per)
        m2 = jnp.max(jnp.where(sub == f1, neg, s_g), axis=0, keepdims=True)
        gs.append(m1 + m2)
    gsc = jnp.concatenate(gs, axis=0)
    gi = lax.broadcasted_iota(I32, (ng, tr), 0)
    keep = jnp.zeros((ng, tr), F32)
    for _ in range(topg):
        _, f = _first_max(gsc, gi, ng)
        pick = gi == f
        keep = jnp.where(pick, 1.0, keep)
        gsc = jnp.where(pick, neg, gsc)
    work = jnp.concatenate(
        [jnp.where(keep[g:g + 1, :] > 0.5, sel[g * per:(g + 1) * per, :], neg) for g in range(ng)], axis=0)
    ei = lax.broadcasted_iota(I32, (ne, tr), 0)
    picks, es, ws = [], [], []
    for _ in range(topk):
        _, f = _first_max(work, ei, ne)
        pick = ei == f
        picks.append(pick)
        es.append(f)
        ws.append(jnp.sum(jnp.where(pick, scores, 0.0), axis=0, keepdims=True))
        work = jnp.where(pick, neg, work)
    wsum = ws[0]
    for w in ws[1:]:
        wsum = wsum + w
    scale = ROUTED_SCALE / wsum
    chosen = picks[0]
    for p in picks[1:]:
        chosen = jnp.logical_or(chosen, p)
    chosen_f = chosen.astype(F32)
    prior = cnt_scr[:, 0:1] + jnp.dot(chosen_f.astype(BF16), before[...], preferred_element_type=F32)
    rs = [jnp.sum(jnp.where(p, prior, 0.0), axis=0, keepdims=True).astype(I32) for p in picks]
    pad_i = jnp.zeros((SUBLANES - topk, tr), I32)
    pad_f = jnp.zeros((SUBLANES - topk, tr), F32)
    step = pl.program_id(0)
    e_all[step] = jnp.concatenate(es + [pad_i], axis=0)
    r_all[step] = jnp.concatenate(rs + [pad_i], axis=0)
    w_ref[...] = jnp.concatenate([w * scale for w in ws] + [pad_f], axis=0)
    total = cnt_scr[:, 0:1] + jnp.sum(chosen_f, axis=1, keepdims=True)
    cnt_scr[...] = jnp.broadcast_to(total, cnt_scr.shape)
    cnt_ref[...] = jnp.broadcast_to(total, cnt_ref.shape)

    @pl.when(step == nt - 1)
    def _():
        padded = jnp.ceil(total * (1.0 / EXPERT_ROWS)) * EXPERT_ROWS
        on_lanes = jnp.transpose(jnp.broadcast_to(padded, (ne, LANES)))[0:SUBLANES, 0:ne]
        e_row = lax.broadcasted_iota(I32, (ne, ne), 0)
        e_col = lax.broadcasted_iota(I32, (ne, ne), 1)
        starts = jnp.sum(jnp.where(e_col < e_row, jnp.broadcast_to(on_lanes[0:1, :], (ne, ne)), 0.0),
                         axis=1, keepdims=True).astype(I32)
        tiles = tr // TOKEN_TILE
        for c in range(nt):
            e = e_all[c]
            base = jnp.zeros(e.shape, I32)
            for j in range(ne):
                base = jnp.where(e == j, starts[j:j + 1, 0:1], base)
            dest = base + r_all[c]
            for m in range(tiles):
                row0 = SUBLANES * (c * tiles + m)
                d_ref[row0:row0 + SUBLANES, :] = dest[:, TOKEN_TILE * m:TOKEN_TILE * (m + 1)]


def _token_tile(t, limit):
    return max(m for m in range(LANES, limit + 1, LANES) if t % m == 0)


def _route(logits_t, bias_col):
    ne, t = logits_t.shape
    tr = _token_tile(t, ROUTE_TILE)
    nt = t // tr
    dest_shape = (t // TOKEN_TILE * SUBLANES, TOKEN_TILE)
    return pl.pallas_call(
        functools.partial(_route_body, tr=tr, nt=nt, ne=ne, ng=N_GROUPS, topk=TOP_K, topg=TOPK_GROUPS),
        grid=(nt,),
        in_specs=[pl.BlockSpec((ne, tr), lambda i: (0, i)), pl.BlockSpec((ne, 1), lambda i: (0, 0))],
        out_specs=[pl.BlockSpec((SUBLANES, tr), lambda i: (0, i)), pl.BlockSpec((ne, LANES), lambda i: (0, 0)),
                   pl.BlockSpec(dest_shape, lambda i: (0, 0))],
        out_shape=[jax.ShapeDtypeStruct((SUBLANES, t), F32), jax.ShapeDtypeStruct((ne, LANES), F32),
                   jax.ShapeDtypeStruct(dest_shape, I32)],
        scratch_shapes=[pltpu.VMEM((ne, LANES), F32), pltpu.VMEM((tr, tr), BF16),
                        pltpu.VMEM((nt, SUBLANES, tr), I32), pltpu.VMEM((nt, SUBLANES, tr), I32)],
        compiler_params=pltpu.CompilerParams(dimension_semantics=("arbitrary",)),
        name="route",
    )(logits_t, bias_col)


def _sc_mesh():
    return plsc.VectorSubcoreMesh(core_axis_name="c", subcore_axis_name="s")


def _sc_worker_id():
    return lax.axis_index("s") * SC_CORES + lax.axis_index("c")


def _index_block(dest_ref, chunk, width):
    per_tile = TOKEN_TILE // width
    return dest_ref.at[chunk // per_tile, :, pl.ds((chunk % per_tile) * width, width)]


def _dispatch(hps, dest3, n_rows):
    w = SCATTER_ROWS
    width = hps[0].shape[1]
    bounds = [0]
    for h in hps:
        bounds.append(bounds[-1] + h.shape[0] // w)
    nch = bounds[-1]
    nsrc = len(hps)

    @functools.partial(
        pl.kernel, mesh=_sc_mesh(),
        out_type=jax.ShapeDtypeStruct((n_rows, width), U32),
        scratch_types=[pltpu.VMEM((2, SUBLANES, w), I32), pltpu.VMEM((2, w, width), U32),
                       pltpu.SemaphoreType.DMA, pltpu.SemaphoreType.DMA, pltpu.SemaphoreType.DMA],
        compiler_params=pltpu.CompilerParams(use_tc_tiling_on_sc=False),
        name="dispatch",
    )
    def run(*refs):
        src_refs, dest_ref, xs_ref = refs[:nsrc], refs[nsrc], refs[nsrc + 1]
        idx_v, rows_v = refs[nsrc + 2:nsrc + 4]
        sem_load = refs[nsrc + 4:nsrc + 6]
        sem_scatter = refs[nsrc + 6]
        wid = _sc_worker_id()

        def start_loads(c, slot):
            pltpu.async_copy(_index_block(dest_ref, c, w), idx_v.at[slot], sem_load[slot])
            for i, src in enumerate(src_refs):
                @pl.when(jnp.logical_and(c >= bounds[i], c < bounds[i + 1]))
                def _(src=src, lo=bounds[i]):
                    pltpu.async_copy(src.at[pl.ds((c - lo) * w, w)], rows_v.at[slot], sem_load[slot])

        def wait_loads(slot):
            pltpu.make_async_copy(_index_block(dest_ref, 0, w), idx_v.at[slot], sem_load[slot]).wait()
            pltpu.make_async_copy(src_refs[0].at[pl.ds(0, w)], rows_v.at[slot], sem_load[slot]).wait()

        @pl.when(wid < nch)
        def _():
            start_loads(wid, 0)

        @pl.loop(0, pl.cdiv(pl.cdiv(nch, SC_WORKERS), 2))
        def _(rr):
            for slot in range(2):
                c = (rr * 2 + slot) * SC_WORKERS + wid

                @pl.when(c < nch)
                def _(c=c, slot=slot):
                    wait_loads(slot)

                    @pl.when(c + SC_WORKERS < nch)
                    def _():
                        start_loads(c + SC_WORKERS, 1 - slot)

                    copies = [pltpu.async_copy(rows_v.at[slot], xs_ref.at[idx_v.at[slot, k]], sem_scatter)
                              for k in range(TOP_K)]
                    for cp in copies:
                        cp.wait()

    return run(*hps, dest3)


def _undispatch(ys, dest3, n_tokens, after=None):
    w = GATHER_ROWS
    width = ys.shape[1]
    nch = n_tokens // w

    @functools.partial(
        pl.kernel, mesh=_sc_mesh(),
        out_type=jax.ShapeDtypeStruct((TOP_K, n_tokens, width), U32),
        scratch_types=[pltpu.VMEM((2, SUBLANES, w), I32), pltpu.VMEM((2, TOP_K, w, width), U32),
                       pltpu.SemaphoreType.DMA, pltpu.SemaphoreType.DMA, pltpu.SemaphoreType.DMA],
        compiler_params=pltpu.CompilerParams(use_tc_tiling_on_sc=False),
        name="undispatch",
    )
    def run(*refs):
        ys_ref, dest_ref = refs[:2]
        z_ref, idx_v, bufs = refs[-6:-3]
        sem_gather = refs[-3:-1]
        sem_store = refs[-1]
        wid = _sc_worker_id()

        def start_gathers(c, slot):
            pltpu.sync_copy(_index_block(dest_ref, c, w), idx_v.at[slot])
            for k in range(TOP_K):
                pltpu.async_copy(ys_ref.at[idx_v.at[slot, k]], bufs.at[slot, k], sem_gather[slot])

        def wait_gathers(slot):
            for k in range(TOP_K):
                pltpu.make_async_copy(ys_ref.at[idx_v.at[slot, k]], bufs.at[slot, k], sem_gather[slot]).wait()

        @pl.when(wid < nch)
        def _():
            start_gathers(wid, 0)

        @pl.loop(0, pl.cdiv(pl.cdiv(nch, SC_WORKERS), 2))
        def _(rr):
            for slot in range(2):
                c = (rr * 2 + slot) * SC_WORKERS + wid

                @pl.when(c < nch)
                def _(c=c, slot=slot):
                    @pl.when(c + SC_WORKERS < nch)
                    def _():
                        start_gathers(c + SC_WORKERS, 1 - slot)

                    wait_gathers(slot)
                    stores = [pltpu.async_copy(bufs.at[slot, k], z_ref.at[k, pl.ds(c * w, w)], sem_store)
                              for k in range(TOP_K)]
                    for cp in stores:
                        cp.wait()

    return run(ys, dest3) if after is None else run(ys, dest3, after)


PART_SHIFT = 24


def _expert_body(*refs, tm, ne, nparts):
    cnt_refs = refs[:nparts]
    xs_refs = refs[nparts:2 * nparts]
    w1_ref, w3_ref, w2_ref = refs[2 * nparts:2 * nparts + 3]
    ys_refs = refs[2 * nparts + 3:3 * nparts + 3]
    (w1f, w3f, w2f, w1s, w3s, w2s, xbuf, ybuf, xlo, xhi, sched, sem_x, sem_y, sem_w) = refs[3 * nparts + 3:]
    blk_words = tm * ROW_WORDS
    half = ROW_WORDS * LANES
    shift = tm.bit_length() - 1

    def n_blocks_of(p, e):
        return lax.shift_right_logical(cnt_refs[p][e] + (tm - 1), shift)

    def n_all(e):
        n = n_blocks_of(0, e)
        for p in range(1, nparts):
            n = n + n_blocks_of(p, e)
        return n

    def next_nonempty(e):
        return lax.while_loop(
            lambda c: jnp.logical_and(c < ne, n_all(jnp.minimum(c, ne - 1)) == 0), lambda c: c + 1, e)

    def plan(e, carry):
        i, starts = carry[0], list(carry[1:])
        for p in range(nparts):
            n = n_blocks_of(p, e)

            def put(j, c, p=p, i=i, start=starts[p]):
                sched[i + j] = (start + j) + (p << PART_SHIFT)
                return c

            lax.fori_loop(0, n, put, 0)
            i = i + n
            starts[p] = starts[p] + n
        return (i, *starts)

    nu = lax.fori_loop(0, ne, plan, (jnp.int32(0),) * (nparts + 1))[0]

    def rows_of(code):
        blk = code & ((1 << PART_SHIFT) - 1)
        return pl.ds(pl.multiple_of(blk * blk_words, blk_words), blk_words)

    def x_start(i, slot):
        code = sched[i]
        for p in range(nparts):
            @pl.when(lax.shift_right_logical(code, PART_SHIFT) == p)
            def _(p=p):
                pltpu.make_async_copy(xs_refs[p].at[rows_of(code), :], xbuf.at[slot], sem_x.at[slot]).start()

    def x_wait(slot):
        pltpu.make_async_copy(xs_refs[0].at[pl.ds(0, blk_words), :], xbuf.at[slot], sem_x.at[slot]).wait()

    def y_start(i, slot):
        code = sched[i]
        for p in range(nparts):
            @pl.when(lax.shift_right_logical(code, PART_SHIFT) == p)
            def _(p=p):
                pltpu.make_async_copy(ybuf.at[slot], ys_refs[p].at[rows_of(code), :], sem_y.at[slot]).start()

    def y_wait(slot):
        pltpu.make_async_copy(ybuf.at[slot], ys_refs[0].at[pl.ds(0, blk_words), :], sem_y.at[slot]).wait()

    def w_copies(e, ws):
        return [pltpu.make_async_copy(src.at[e], dst.at[ws], sem_w.at[ws])
                for src, dst in ((w1_ref, w1f), (w3_ref, w3f), (w2_ref, w2f))]

    for q in range(X_AHEAD):
        @pl.when(q < nu)
        def _(q=q):
            x_start(q, q)

    e_first = next_nonempty(jnp.int32(0))

    @pl.when(e_first < ne)
    def _():
        for cp in w_copies(e_first, 0):
            cp.start()

    def blocks(i, nblk):
        for b in range(nblk):
            x_wait((i + b) % X_SLOTS)
        for b in range(nblk):
            nxt = i + b + X_AHEAD

            @pl.when(nxt < nu)
            def _(nxt=nxt):
                x_start(nxt, nxt % X_SLOTS)

        for b in range(nblk):
            @pl.when(i + b >= Y_SLOTS)
            def _(b=b):
                y_wait((i + b) % Y_SLOTS)

        for b in range(nblk):
            xin = xbuf.at[(i + b) % X_SLOTS]
            for w in range(ROW_WORDS):
                lo, hi = _unpack_halves(_load_row_word(xin, w, tm))
                xlo[b, :, LANES * w:LANES * (w + 1)] = lo.astype(BF16)
                xhi[b, :, LANES * w:LANES * (w + 1)] = hi.astype(BF16)

        def up(b, wsc):
            return (jnp.dot(xlo[b], wsc[0:half, :], preferred_element_type=F32)
                    + jnp.dot(xhi[b], wsc[half:2 * half, :], preferred_element_type=F32))

        for b in range(nblk):
            hid = (_silu(up(b, w1s)) * up(b, w3s)).astype(BF16)
            y = jnp.dot(hid, w2s[...], preferred_element_type=F32)
            _store_rows(ybuf.at[(i + b) % Y_SLOTS], _pack_halves(y[:, 0:half], y[:, half:2 * half]))
        for b in range(nblk):
            y_start(i + b, (i + b) % Y_SLOTS)

    def per_expert(e, carry):
        i0, ws = carry
        n = n_all(e)

        @pl.when(n > 0)
        def _():
            for cp in w_copies(e, ws):
                cp.wait()
            w1s[...] = w1f[ws].astype(BF16)
            w3s[...] = w3f[ws].astype(BF16)
            w2s[...] = w2f[ws].astype(BF16)
            e_next = next_nonempty(e + 1)

            @pl.when(e_next < ne)
            def _():
                for cp in w_copies(e_next, 1 - ws):
                    cp.start()

            def pair(j, c):
                blocks(i0 + 2 * j, 2)
                return c

            lax.fori_loop(0, lax.shift_right_logical(n, 1), pair, 0)

            @pl.when(n % 2 == 1)
            def _():
                blocks(i0 + n - 1, 1)

        return i0 + n, jnp.where(n > 0, 1 - ws, ws)

    lax.fori_loop(0, ne, per_expert, (jnp.int32(0), jnp.int32(0)))

    for q in range(Y_SLOTS):
        @pl.when(nu > q)
        def _(q=q):
            y_wait((nu - 1 - q) % Y_SLOTS)


def _experts(counts, xss, w1e, w3e, w2e):
    tm = EXPERT_ROWS
    ne, d, de = w1e.shape
    half = ROW_WORDS * LANES
    nparts = len(xss)
    cap = sum(x.shape[0] // (tm * ROW_WORDS) for x in xss)
    assert cap < (1 << PART_SHIFT)
    anyspec = pl.BlockSpec(memory_space=pl.ANY)
    blk_buf = lambda n: pltpu.VMEM((n, tm * ROW_WORDS, LANES), U32)
    return pl.pallas_call(
        functools.partial(_expert_body, tm=tm, ne=ne, nparts=nparts),
        grid_spec=pltpu.PrefetchScalarGridSpec(
            num_scalar_prefetch=nparts,
            grid=(1,),
            in_specs=[anyspec] * (nparts + 3),
            out_specs=[anyspec] * nparts,
            scratch_shapes=[pltpu.VMEM((2, d, de), F32), pltpu.VMEM((2, d, de), F32), pltpu.VMEM((2, de, d), F32),
                            pltpu.VMEM((d, de), BF16), pltpu.VMEM((d, de), BF16), pltpu.VMEM((de, d), BF16),
                            blk_buf(X_SLOTS), blk_buf(Y_SLOTS),
                            pltpu.VMEM((2, tm, half), BF16), pltpu.VMEM((2, tm, half), BF16),
                            pltpu.SMEM((cap,), I32),
                            pltpu.SemaphoreType.DMA((X_SLOTS,)), pltpu.SemaphoreType.DMA((Y_SLOTS,)),
                            pltpu.SemaphoreType.DMA((2,))]),
        out_shape=[jax.ShapeDtypeStruct(x.shape, U32) for x in xss],
        compiler_params=pltpu.CompilerParams(dimension_semantics=("arbitrary",), vmem_limit_bytes=VMEM_LIMIT),
        name="experts",
    )(*counts, *xss, w1e, w3e, w2e)


def _combine_body(z_ref, x1_ref, scf_ref, shf_ref, gtf_ref, wt_ref, gffn_ref, gfin_ref, ws1_ref, ws3_ref, ws2_ref,
                  y_ref, xo, *, td, topk):
    half = ROW_WORDS * LANES
    hb = _modulated_rmsnorm(x1_ref[...], gffn_ref[...], scf_ref[...], shf_ref[...]).astype(BF16)
    up = lambda w_ref: jnp.dot(hb, w_ref[...], preferred_element_type=F32)
    xo[...] = jnp.dot((_silu(up(ws1_ref)) * up(ws3_ref)).astype(BF16), ws2_ref[...], preferred_element_type=F32)
    wt = wt_ref[...].T
    ws = [wt[:, k:k + 1] for k in range(topk)]
    sq = jnp.zeros((td, 1), F32)
    for j in range(ROW_WORDS):
        acc_lo = jnp.zeros((td, LANES), F32)
        acc_hi = jnp.zeros((td, LANES), F32)
        for k in range(topk):
            lo, hi = _unpack_halves(_load_row_word(z_ref.at[k], j, td))
            acc_lo = acc_lo + ws[k] * lo
            acc_hi = acc_hi + ws[k] * hi
        for base, acc in ((0, acc_lo), (half, acc_hi)):
            cols = slice(base + LANES * j, base + LANES * (j + 1))
            x = x1_ref[:, cols] + gtf_ref[:, cols] * (acc + xo[:, cols])
            xo[:, cols] = x
            sq = sq + jnp.sum(x * x, axis=-1, keepdims=True)
    rs = lax.rsqrt(sq / (2 * half) + EPS)
    y_ref[...] = xo[...] * rs * gfin_ref[...]


def _combine_body_into(*refs, td, topk):
    _combine_body(*refs[:11], *refs[12:], td=td, topk=topk)


def _combine(z, token0, wts, x1, mod, g_ffn, g_final, shared_w, *, rows_per_mod, per_row_mod, b0=0, out_rows=None,
             into=None):
    t, d = x1.shape
    td = min(t, COMBINE_TILE)
    tile0 = token0 // td
    out_rows = t if out_rows is None else out_rows
    if per_row_mod:
        mod_spec = lambda j: pl.BlockSpec((td, d), lambda i: (i, j))
        out0 = 0
    else:
        tiles_per_mod = rows_per_mod // td
        mod_spec = lambda j: pl.BlockSpec((None, None, 1, d), lambda i: (i // tiles_per_mod + b0, j, 0, 0))
        out0 = b0 * tiles_per_mod
    full = lambda a: pl.BlockSpec(a.shape, lambda i: (0,) * a.ndim)
    in_specs = [pl.BlockSpec((TOP_K, td * ROW_WORDS, LANES), lambda i: (0, i + tile0, 0)),
                pl.BlockSpec((td, d), lambda i: (i, 0)),
                mod_spec(4), mod_spec(3), mod_spec(5),
                pl.BlockSpec((SUBLANES, td), lambda i: (0, i + tile0)),
                full(g_ffn), full(g_final)] + [full(w) for w in shared_w]
    args = [z, x1, mod, mod, mod, wts, g_ffn, g_final, *shared_w]
    body, aliases = _combine_body, {}
    if into is not None:
        in_specs.append(pl.BlockSpec(memory_space=pl.ANY))
        args.append(into)
        body, aliases = _combine_body_into, {len(args) - 1: 0}
    return pl.pallas_call(
        functools.partial(body, td=td, topk=TOP_K),
        grid=(t // td,),
        in_specs=in_specs,
        out_specs=pl.BlockSpec((td, d), lambda i: (i + out0, 0)),
        scratch_shapes=[pltpu.VMEM((td, d), F32)],
        out_shape=jax.ShapeDtypeStruct((out_rows, d), F32),
        input_output_aliases=aliases,
        compiler_params=pltpu.CompilerParams(vmem_limit_bytes=VMEM_LIMIT),
        name="combine",
    )(*args)


def _log_gamma(nh):
    return np.log(1.0 - 2.0 ** (-5.0 - np.arange(nh, dtype=np.float32))).astype(np.float32)


def _retention_tables(length, nh, hd):
    c = math.gcd(length, RET_CHUNK)
    log_g = _log_gamma(nh)
    idx = np.arange(c, dtype=np.float32)
    rel = idx[:, None] - idx[None, :]
    mask = np.where(rel >= 0, np.exp(log_g[:, None, None] * np.maximum(rel, 0.0)), 0.0).astype(np.float32)
    q_decay = np.exp(log_g[None, :] * (idx[:, None] + 1.0)).astype(np.float32)
    k_decay = np.exp(log_g[None, :] * (c - 1.0 - idx[:, None])).astype(np.float32)
    chunk_decay = np.exp(log_g * np.float32(c)).astype(np.float32)
    qd = np.broadcast_to(q_decay.T[:, :, None], (nh, c, hd))
    kd = np.broadcast_to(k_decay.T[:, :, None], (nh, c, hd))
    cd = np.broadcast_to(chunk_decay[:, None, None], (nh, hd, hd))
    return tuple(jnp.asarray(t) for t in (mask, qd, kd, cd))


def kernel(x_prompt, x_sample, c_prompt, c_sample, state_conv, state_ret, w_ada, b_ada, g_mix, g_ffn, w_in,
           conv_w, conv_b, conv_norm_g, conv_norm_b, ret_norm_g, ret_norm_b, w_out, w_router, router_bias,
           w1, w3, w2, ws1, ws3, ws2, g_final):
    depth = w_ada.shape[0]
    assert depth == 1, "single-layer trunk"
    bp, lp, d = x_prompt.shape
    bs, ls, _ = x_sample.shape
    assert ls == 1
    dc = conv_w.shape[2]
    dr = ret_norm_g.shape[1]
    nh = RET_HEADS
    hd = dr // nh
    assert hd == LANES and lp % 256 == 0 and bs % TOKEN_TILE == 0 and d // 2 == ROW_WORDS * LANES
    ne = w_router.shape[2]
    row = lambda a: a.reshape(1, -1)

    mod_p, mod_s = _ada([c_prompt, c_sample], w_ada[0], row(b_ada[0]))
    mod_p = mod_p.reshape(bp, 6, 1, d)

    half = hd // 2
    inv = (np.float32(ROPE_BASE) ** (-np.arange(half, dtype=np.float32) / np.float32(half))).astype(np.float32)
    inv2 = np.concatenate([inv, inv]).reshape(1, hd)
    cos_p, sin_p = _rope_tables(inv2, lp, 0)
    cos_s, sin_s = _rope_tables(inv2, SUBLANES, PAST_LEN)

    w_in_b = w_in[0].astype(BF16)
    wo_b = w_out[0].astype(BF16)
    wr_t = w_router[0].T
    wrh = wr_t.astype(BF16)
    wrl = (wr_t - wrh.astype(F32)).astype(BF16)
    ws1_b, ws3_b, ws2_b = ws1[0].astype(BF16), ws3[0].astype(BF16), ws2[0].astype(BF16)
    dims = dict(dc=dc, dr=dr, hd=hd)

    tables = _retention_tables(lp, nh, hd)
    gam = jnp.asarray(np.broadcast_to(np.exp(_log_gamma(nh))[:, None, None], (nh, SUBLANES, hd)))
    norm_rows = (row(conv_b[0]), row(conv_norm_g[0]), row(conv_norm_b[0]), row(ret_norm_g[0]), row(ret_norm_b[0]))
    post_w = (row(g_ffn[0]), wo_b, wrh, wrl)
    shared_w = (ws1_b, ws3_b, ws2_b)
    bias_col = router_bias[0].reshape(ne, 1)
    hw = d // 2
    tm = EXPERT_ROWS

    def pre_prompt(b0, nb, after=None):
        glu, q, k, v, sg = _proj(x_prompt, mod_p, row(g_mix[0]), w_in_b, cos_p, sin_p,
                                 per_row_mod=False, b0=b0, nb=nb, after=after, **dims)
        cat, ret = _mix(glu, q, k, v, sg, conv_w[0], *norm_rows, tables, nb=nb, length=lp, **dims)
        x2, hp, lg = _post(cat, x_prompt, mod_p, *post_w, per_row_mod=False, b0=b0, nb=nb)
        return glu, ret, x2, hp, lg

    def pre_sample():
        xs3 = x_sample.reshape(1, bs, d)
        glu, q, k, v, sg = _proj(xs3, mod_s, row(g_mix[0]), w_in_b, cos_s, sin_s, per_row_mod=True, **dims)
        cat, ret, new_conv = _mix1(glu, q, k, v, sg, state_conv.transpose(0, 2, 1, 3), state_ret, conv_w[0],
                                   *norm_rows, gam, **dims)
        x2, hp, lg = _post(cat, xs3, mod_s, *post_w, per_row_mod=True)
        return new_conv.transpose(0, 2, 1, 3), ret, x2, hp, lg

    def route_and_dispatch(hps, lgs):
        lg = lgs[0] if len(lgs) == 1 else jnp.concatenate(lgs, axis=1)
        tokens = lg.shape[1]
        wts, cnt, dest = _route(lg, bias_col)
        counts = cnt[:, 0].astype(I32)
        n_rows = -(-(tokens * TOP_K + ne * (tm - 1)) // tm) * tm
        dest3 = dest.reshape(tokens // TOKEN_TILE, SUBLANES, TOKEN_TILE)
        xs = _dispatch([h.reshape(-1, hw) for h in hps], dest3, n_rows)
        return counts, dest3, wts, xs.reshape(n_rows * ROW_WORDS, LANES)

    def undispatch(ys, dest3, after=None):
        tokens = dest3.shape[0] * TOKEN_TILE
        z = _undispatch(ys.reshape(-1, hw), dest3, tokens, after)
        return z.reshape(TOP_K, tokens * ROW_WORDS, LANES)

    nb0 = bp // 2
    nb1 = bp - nb0
    new_conv_s, ret_s, x2_s, hp_s, lg_s = pre_sample()
    glu_0, ret_0, x2_0, hp_0, lg_0 = pre_prompt(0, nb0, after=lg_s)
    counts_0, dest_0, wts_0, xs_0 = route_and_dispatch([hp_0], [lg_0])
    glu_1, ret_1, x2_1, hp_1, lg_1 = pre_prompt(nb0, nb1, after=dest_0)
    counts_1, dest_1, wts_1, xs_1 = route_and_dispatch([hp_1, hp_s], [lg_1, lg_s])
    ys_0, ys_1 = _experts((counts_0, counts_1), (xs_0, xs_1), w1[0], w3[0], w2[0])
    z_0 = undispatch(ys_0, dest_0)
    z_1 = undispatch(ys_1, dest_1, after=z_0[0, :SUBLANES])
    y_p = _combine(z_0, 0, wts_0, x2_0, mod_p, row(g_ffn[0]), row(g_final), shared_w, rows_per_mod=lp, per_row_mod=False,
                   out_rows=bp * lp)
    y_p = _combine(z_1, 0, wts_1, x2_1, mod_p, row(g_ffn[0]), row(g_final), shared_w, rows_per_mod=lp, per_row_mod=False,
                   b0=nb0, out_rows=bp * lp, into=y_p)
    y_s = _combine(z_1, nb1 * lp, wts_1, x2_s, mod_s, row(g_ffn[0]), row(g_final), shared_w, rows_per_mod=bs,
                   per_row_mod=True)
    ret_p = jnp.concatenate([ret_0, ret_1], axis=0)

    tail = lambda g, n: g.reshape(n, lp, dc)[:, lp - CONV_BUF:, :]
    new_conv_p = jnp.concatenate([tail(glu_0, nb0), tail(glu_1, nb1)], axis=0)
    return (y_p.reshape(bp, lp, d), y_s.reshape(bs, ls, d), new_conv_p[None], ret_p[None],
            new_conv_s, ret_s)
```

```python
import functools
import math

import jax
import jax.numpy as jnp
import numpy as np
from jax import lax
from jax.experimental import pallas as pl
from jax.experimental.pallas import tpu as pltpu
from jax.experimental.pallas import tpu_sc as plsc

F32 = jnp.float32
BF16 = jnp.bfloat16
U32 = jnp.uint32
I32 = jnp.int32

EPS = 1e-6
PAST_LEN = 16384
RET_HEADS = 4
RET_CHUNK = 128
CONV_WIDTH = 31
CONV_BUF = CONV_WIDTH - 1
ROPE_BASE = 10000.0
TOP_K = 6
N_GROUPS = 8
TOPK_GROUPS = 4
ROUTED_SCALE = 2.5

LANES = 128
SUBLANES = 8
CONV_PAD = 32
EXPERT_ROWS = 256
ROUTE_TILE = 1024
TOKEN_TILE = 128
COMBINE_TILE = 512
VMEM_LIMIT = 56 * 1024 * 1024
SC_CORES = 2
SC_SUBCORES = 16
SC_WORKERS = SC_CORES * SC_SUBCORES
SCATTER_ROWS = 64
GATHER_ROWS = 16
ROW_WORDS = 4
X_SLOTS = 6
X_AHEAD = 4
Y_SLOTS = 4

HI_MASK = 0xFFFF0000


def _sigmoid(x):
    return jax.nn.sigmoid(x)


def _silu(x):
    return x * jax.nn.sigmoid(x)


def _pack_halves(lo, hi):
    lo_u = lax.bitcast_convert_type(lo.astype(BF16).astype(F32), U32) >> 16
    hi_u = lax.bitcast_convert_type(hi.astype(BF16).astype(F32), U32) & jnp.uint32(HI_MASK)
    return hi_u | lo_u


def _unpack_halves(p):
    lo = lax.bitcast_convert_type(p << 16, F32)
    hi = lax.bitcast_convert_type(p & jnp.uint32(HI_MASK), F32)
    return lo, hi


def _store_rows(ref, x):
    rows = x.shape[0]
    for j in range(ROW_WORDS):
        ref[pl.ds(j, rows, stride=ROW_WORDS), :] = x[:, LANES * j:LANES * (j + 1)]


def _load_row_word(ref, j, rows):
    return ref[pl.ds(j, rows, stride=ROW_WORDS), :]


def _ada_body(w_ref, b_ref, *refs):
    w = w_ref[...].astype(BF16)
    groups = len(refs) // 2
    for c_ref, o_ref in zip(refs[:groups], refs[groups:]):
        o_ref[...] = jnp.dot(_silu(c_ref[...]).astype(BF16), w, preferred_element_type=F32) + b_ref[...]


def _ada(cs, w_ada, b_ada):
    d, n = w_ada.shape
    tn = 2048
    return pl.pallas_call(
        _ada_body,
        grid=(n // tn,),
        in_specs=[pl.BlockSpec((d, tn), lambda j: (0, j)), pl.BlockSpec((1, tn), lambda j: (0, j))]
        + [pl.BlockSpec(c.shape, lambda j: (0, 0)) for c in cs],
        out_specs=[pl.BlockSpec((c.shape[0], tn), lambda j: (0, j)) for c in cs],
        out_shape=[jax.ShapeDtypeStruct((c.shape[0], n), F32) for c in cs],
        compiler_params=pltpu.CompilerParams(vmem_limit_bytes=VMEM_LIMIT),
        name="ada",
    )(w_ada, b_ada, *cs)


def _rope_tables(inv2, rows, pos0):
    pos = np.arange(rows, dtype=np.float32) + np.float32(pos0)
    ang = (pos[:, None] * inv2).astype(np.float64)
    sign = np.where(np.arange(inv2.shape[1]) < inv2.shape[1] // 2, -1.0, 1.0)
    return jnp.asarray(np.cos(ang).astype(np.float32)), jnp.asarray((np.sin(ang) * sign).astype(np.float32))


def _modulated_rmsnorm(x, g, sc, sh):
    ms = jnp.mean(x * x, axis=-1, keepdims=True)
    h = x * lax.rsqrt(ms + EPS) * g
    return h * (1.0 + sc) + sh


def _proj_body(x_ref, sh_ref, sc_ref, g_ref, w_ref, cos_ref, sin_ref, *rest, dc, dr, hd, rope_rows):
    glu_ref, q_ref, k_ref, v_ref, sg_ref = rest[-5:]
    hb = _modulated_rmsnorm(x_ref[...], g_ref[...], sc_ref[...], sh_ref[...]).astype(BF16)

    def proj(lo, n):
        return jnp.dot(hb, w_ref[:, lo:lo + n], preferred_element_type=F32)

    glu_ref[...] = proj(0, dc) * _sigmoid(proj(dc, dc))
    cos = cos_ref[...] if rope_rows else cos_ref[0:1, :]
    sin = sin_ref[...] if rope_rows else sin_ref[0:1, :]
    for ref, lo, scale in ((q_ref, 2 * dc, hd ** -0.5), (k_ref, 2 * dc + dr, None)):
        t = proj(lo, dr)
        for hh in range(dr // hd):
            th = t[:, hh * hd:(hh + 1) * hd]
            r = th * cos + pltpu.roll(th, hd // 2, 1) * sin
            if scale is not None:
                r = r * scale
            ref[:, hh * hd:(hh + 1) * hd] = r.astype(BF16)
    v_ref[...] = proj(2 * dc + 2 * dr, dr).astype(BF16)
    sg_ref[...] = _silu(proj(2 * dc + 3 * dr, dr)).astype(BF16)


def _proj(x, mod, g_mix, w_in_b, cos2, sin2, *, dc, dr, hd, per_row_mod, b0=0, nb=None, after=None):
    nb_all, length, d = x.shape
    nb = nb_all if nb is None else nb
    tl = min(length, 512)
    x2 = x.reshape(nb_all * length, d)
    nl = length // tl
    if per_row_mod:
        mod_spec = lambda j: pl.BlockSpec((tl, d), lambda b, l: (l, j))
        rope_spec = pl.BlockSpec((SUBLANES, LANES), lambda b, l: (0, 0))
    else:
        mod_spec = lambda j: pl.BlockSpec((None, None, 1, d), lambda b, l: (b + b0, j, 0, 0))
        rope_spec = pl.BlockSpec((tl, LANES), lambda b, l: (l, 0))
    row_spec = lambda w: pl.BlockSpec((tl, w), lambda b, l: (b * nl + l, 0))
    t = nb * length
    outs = pl.pallas_call(
        functools.partial(_proj_body, dc=dc, dr=dr, hd=hd, rope_rows=not per_row_mod),
        grid=(nb, nl),
        in_specs=[
            pl.BlockSpec((tl, d), lambda b, l: ((b + b0) * nl + l, 0)), mod_spec(0), mod_spec(1),
            pl.BlockSpec((1, d), lambda b, l: (0, 0)),
            pl.BlockSpec(w_in_b.shape, lambda b, l: (0, 0)),
            rope_spec, rope_spec,
        ] + ([] if after is None else [pl.BlockSpec(memory_space=pl.ANY)]),
        out_specs=[row_spec(dc), row_spec(dr), row_spec(dr), row_spec(dr), row_spec(dr)],
        out_shape=[
            jax.ShapeDtypeStruct((t, dc), F32),
            jax.ShapeDtypeStruct((t, dr), BF16),
            jax.ShapeDtypeStruct((t, dr), BF16),
            jax.ShapeDtypeStruct((t, dr), BF16),
            jax.ShapeDtypeStruct((t, dr), BF16),
        ],
        compiler_params=pltpu.CompilerParams(vmem_limit_bytes=VMEM_LIMIT),
        name="proj",
    )(x2, mod, mod, g_mix, w_in_b, cos2, sin2, *(() if after is None else (after,)))
    return outs


def _layernorm_silu(c, g, b):
    mu = jnp.mean(c, axis=-1, keepdims=True)
    d = c - mu
    var = jnp.mean(d * d, axis=-1, keepdims=True)
    return _silu(d * lax.rsqrt(var + EPS) * g + b)


def _groupnorm(o, g, b):
    mu = jnp.mean(o, axis=-1, keepdims=True)
    d = o - mu
    var = jnp.mean(d * d, axis=-1, keepdims=True)
    return d * lax.rsqrt(var + EPS) * g + b


def _mix_body(glu_ref, q_ref, k_ref, v_ref, sg_ref, cw_ref, cb_ref, lng_ref, lnb_ref, rg_ref, rb_ref,
              mask_ref, qd_ref, kd_ref, cd_ref, cat_ref, st_ref, buf, cscr, *, tl, dc, hd, nh, chunk):
    nslab = dc // LANES

    @pl.when(pl.program_id(1) == 0)
    def _():
        buf[:, 0:CONV_PAD, :] = jnp.zeros((nslab, CONV_PAD, LANES), F32)
        st_ref[...] = jnp.zeros(st_ref.shape, F32)

    for j in range(nslab):
        buf[j, CONV_PAD:CONV_PAD + tl, :] = glu_ref[:, LANES * j:LANES * (j + 1)]
    first = CONV_PAD - CONV_BUF
    rows_per_iter = 8 * SUBLANES
    for j in range(nslab):
        cols = slice(LANES * j, LANES * (j + 1))
        wv = [jnp.broadcast_to(cw_ref[t:t + 1, cols], (SUBLANES, LANES)) for t in range(CONV_WIDTH)]
        bias = jnp.broadcast_to(cb_ref[0:1, cols], (SUBLANES, LANES))

        def body(r, carry, j=j, cols=cols, wv=wv, bias=bias):
            base = pl.multiple_of(r * rows_per_iter, rows_per_iter)
            for u in range(rows_per_iter // SUBLANES):
                acc = bias
                for t in range(CONV_WIDTH):
                    acc = acc + wv[t] * buf[j, pl.ds(base + (u * SUBLANES + first + t), SUBLANES), :]
                cscr[pl.ds(base + u * SUBLANES, SUBLANES), cols] = acc
            return carry

        lax.fori_loop(0, tl // rows_per_iter, body, 0)
    for j in range(nslab):
        buf[j, 0:CONV_PAD, :] = buf[j, tl:tl + CONV_PAD, :]
    cat_ref[:, 0:dc] = _layernorm_silu(cscr[...], lng_ref[...], lnb_ref[...]).astype(BF16)

    nt = (((1,), (1,)), ((), ()))
    tn = (((0,), (0,)), ((), ()))
    for c in range(tl // chunk):
        rows = slice(c * chunk, (c + 1) * chunk)
        for hh in range(nh):
            cols = slice(hh * hd, (hh + 1) * hd)
            qh = q_ref[rows, cols]
            kh = k_ref[rows, cols]
            vh = v_ref[rows, cols]
            s = st_ref[0, hh]
            scores = lax.dot_general(qh, kh, nt, preferred_element_type=F32) * mask_ref[hh]
            inner = jnp.dot(scores.astype(BF16), vh, preferred_element_type=F32)
            qd = (qh.astype(F32) * qd_ref[hh]).astype(BF16)
            cross = jnp.dot(qd, s.astype(BF16), preferred_element_type=F32)
            kd = (kh.astype(F32) * kd_ref[hh]).astype(BF16)
            st_ref[0, hh] = cd_ref[hh] * s + lax.dot_general(kd, vh, tn, preferred_element_type=F32)
            o = _groupnorm(inner + cross, rg_ref[0:1, cols], rb_ref[0:1, cols])
            cat_ref[rows, dc + hh * hd:dc + (hh + 1) * hd] = (o * sg_ref[rows, cols].astype(F32)).astype(BF16)


def _mix(glu, q, k, v, sg, conv_w, conv_b, ln_g, ln_b, rg, rb, tables, *, nb, length, dc, dr, hd):
    nh = dr // hd
    chunk = math.gcd(length, RET_CHUNK)
    tl = min(length, 256)
    nl = length // tl
    mask, qd, kd, cd = tables
    row_spec = lambda w: pl.BlockSpec((tl, w), lambda b, l: (b * nl + l, 0))
    full = lambda a: pl.BlockSpec(a.shape, lambda b, l: (0,) * a.ndim)
    cat, st = pl.pallas_call(
        functools.partial(_mix_body, tl=tl, dc=dc, hd=hd, nh=nh, chunk=chunk),
        grid=(nb, nl),
        in_specs=[row_spec(dc), row_spec(dr), row_spec(dr), row_spec(dr), row_spec(dr),
                  full(conv_w), full(conv_b), full(ln_g), full(ln_b), full(rg), full(rb),
                  full(mask), full(qd), full(kd), full(cd)],
        out_specs=[row_spec(dc + dr), pl.BlockSpec((1, nh, hd, hd), lambda b, l: (b, 0, 0, 0))],
        out_shape=[jax.ShapeDtypeStruct((nb * length, dc + dr), BF16),
                   jax.ShapeDtypeStruct((nb, nh, hd, hd), F32)],
        scratch_shapes=[pltpu.VMEM((dc // LANES, tl + CONV_PAD, LANES), F32),
                        pltpu.VMEM((tl, dc), F32)],
        compiler_params=pltpu.CompilerParams(dimension_semantics=("arbitrary", "arbitrary")),
        name="mix",
    )(glu, q, k, v, sg, conv_w, conv_b, ln_g, ln_b, rg, rb, mask, qd, kd, cd)
    return cat, st


def _mix1_body(glu_ref, q_ref, k_ref, v_ref, sg_ref, sc_ref, s0_ref, cw_ref, cb_ref, lng_ref, lnb_ref,
               rg_ref, rb_ref, gam_ref, cat_ref, st_ref, nc_ref, oscr, qf, kf, vf, *, tb, dc, hd, nh):
    glu = glu_ref[...]
    conv = glu * cw_ref[CONV_BUF:CONV_WIDTH, :] + cb_ref[...]
    for r in range(CONV_BUF):
        past = sc_ref[r]
        conv = conv + past * cw_ref[r:r + 1, :]
        if r > 0:
            nc_ref[r - 1] = past
    nc_ref[CONV_BUF - 1] = glu
    cat_ref[:, 0:dc] = _layernorm_silu(conv, lng_ref[...], lnb_ref[...]).astype(BF16)

    tn = (((0,), (0,)), ((), ()))
    rowid = lax.broadcasted_iota(I32, (tb, hd), 0)
    qf[...] = q_ref[...].astype(F32)
    kf[...] = k_ref[...].astype(F32)
    vf[...] = v_ref[...].astype(F32)
    for hh in range(nh):
        cols = slice(hh * hd, (hh + 1) * hd)
        qa = q_ref[:, cols]
        ka = k_ref[:, cols]
        gam = gam_ref[hh, 0:1, :]
        for bb in range(tb):
            onehot = (rowid == bb).astype(BF16)
            qcol = lax.dot_general(qa, onehot, tn, preferred_element_type=F32)
            kcol = lax.dot_general(ka, onehot, tn, preferred_element_type=F32)
            s0 = s0_ref[bb, hh]
            qrow = qf[bb:bb + 1, cols]
            krow = kf[bb:bb + 1, cols]
            vrow = vf[bb:bb + 1, cols]
            qk = jnp.sum(qrow * krow, axis=-1, keepdims=True)
            cross = gam * jnp.sum(qcol * s0, axis=0, keepdims=True)
            st_ref[bb, hh] = gam * s0 + kcol * vrow
            oscr[bb:bb + 1, cols] = qk * vrow + cross
    for hh in range(nh):
        cols = slice(hh * hd, (hh + 1) * hd)
        o = _groupnorm(oscr[:, cols], rg_ref[0:1, cols], rb_ref[0:1, cols])
        cat_ref[:, dc + hh * hd:dc + (hh + 1) * hd] = (o * sg_ref[:, cols].astype(F32)).astype(BF16)


def _mix1(glu, q, k, v, sg, state_conv, state_ret, conv_w, conv_b, ln_g, ln_b, rg, rb, gam, *, dc, dr, hd):
    nb = glu.shape[0]
    nh = dr // hd
    tb = 16
    row_spec = lambda w: pl.BlockSpec((tb, w), lambda i: (i, 0))
    full = lambda a: pl.BlockSpec(a.shape, lambda i: (0,) * a.ndim)
    st_spec = pl.BlockSpec((None, tb, nh, hd, hd), lambda i: (0, i, 0, 0, 0))
    conv_spec = pl.BlockSpec((None, CONV_BUF, tb, dc), lambda i: (0, 0, i, 0))
    cat, st, new_conv = pl.pallas_call(
        functools.partial(_mix1_body, tb=tb, dc=dc, hd=hd, nh=nh),
        grid=(nb // tb,),
        in_specs=[row_spec(dc), row_spec(dr), row_spec(dr), row_spec(dr), row_spec(dr), conv_spec, st_spec,
                  full(conv_w), full(conv_b), full(ln_g), full(ln_b), full(rg), full(rb), full(gam)],
        out_specs=[row_spec(dc + dr), st_spec, conv_spec],
        out_shape=[jax.ShapeDtypeStruct((nb, dc + dr), BF16),
                   jax.ShapeDtypeStruct((1, nb, nh, hd, hd), F32),
                   jax.ShapeDtypeStruct((1, CONV_BUF, nb, dc), F32)],
        scratch_shapes=[pltpu.VMEM((tb, dr), F32)] * 4,
        compiler_params=pltpu.CompilerParams(vmem_limit_bytes=VMEM_LIMIT),
        name="mix1",
    )(glu, q, k, v, sg, state_conv, state_ret, conv_w, conv_b, ln_g, ln_b, rg, rb, gam)
    return cat, st, new_conv


def _post_body(cat_ref, x_ref, gtm_ref, scf_ref, shf_ref, g_ref, wo_ref, wrh_ref, wrl_ref, x1_ref, hp_ref, lg_ref):
    d = x_ref.shape[1]
    y = jnp.dot(cat_ref[...], wo_ref[...], preferred_element_type=F32)
    x1 = x_ref[...] + gtm_ref[...] * y
    x1_ref[...] = x1
    h = _modulated_rmsnorm(x1, g_ref[...], scf_ref[...], shf_ref[...])
    hb = h.astype(BF16)
    _store_rows(hp_ref, _pack_halves(h[:, 0:d // 2], h[:, d // 2:d]))
    hl = (h - hb.astype(F32)).astype(BF16)
    nt = (((1,), (1,)), ((), ()))
    lg_ref[...] = (lax.dot_general(wrh_ref[...], hb, nt, preferred_element_type=F32)
                   + lax.dot_general(wrh_ref[...], hl, nt, preferred_element_type=F32)
                   + lax.dot_general(wrl_ref[...], hb, nt, preferred_element_type=F32))


def _post(cat, x, mod, g_ffn, wo_b, wrh, wrl, *, per_row_mod, b0=0, nb=None):
    nb_all, length, d = x.shape
    nb = nb_all if nb is None else nb
    tl = min(length, 1024)
    nl = length // tl
    t = nb * length
    ne = wrh.shape[0]
    x2d = x.reshape(nb_all * length, d)
    if per_row_mod:
        mod_spec = lambda j: pl.BlockSpec((tl, d), lambda b, l: (l, j))
    else:
        mod_spec = lambda j: pl.BlockSpec((None, None, 1, d), lambda b, l: (b + b0, j, 0, 0))
    row_spec = lambda w: pl.BlockSpec((tl, w), lambda b, l: (b * nl + l, 0))
    full = lambda a: pl.BlockSpec(a.shape, lambda b, l: (0,) * a.ndim)
    return pl.pallas_call(
        _post_body,
        grid=(nb, nl),
        in_specs=[row_spec(d), pl.BlockSpec((tl, d), lambda b, l: ((b + b0) * nl + l, 0)),
                  mod_spec(2), mod_spec(4), mod_spec(3),
                  full(g_ffn), full(wo_b), full(wrh), full(wrl)],
        out_specs=[row_spec(d),
                   pl.BlockSpec((tl * ROW_WORDS, LANES), lambda b, l: (b * nl + l, 0)),
                   pl.BlockSpec((ne, tl), lambda b, l: (0, b * nl + l))],
        out_shape=[jax.ShapeDtypeStruct((t, d), F32),
                   jax.ShapeDtypeStruct((t * ROW_WORDS, LANES), U32),
                   jax.ShapeDtypeStruct((ne, t), F32)],
        compiler_params=pltpu.CompilerParams(vmem_limit_bytes=VMEM_LIMIT),
        name="post",
    )(cat, x2d, mod, mod, mod, g_ffn, wo_b, wrh, wrl)


def _first_max(x, idx, sentinel):
    m = jnp.max(x, axis=0, keepdims=True)
    f = jnp.min(jnp.where(x == m, idx, sentinel), axis=0, keepdims=True)
    return m, f


def _route_body(lg_ref, bias_ref, w_ref, cnt_ref, d_ref, cnt_scr, before, e_all, r_all, *, tr, nt, ne, ng, topk, topg):
    @pl.when(pl.program_id(0) == 0)
    def _():
        cnt_scr[...] = jnp.zeros(cnt_scr.shape, F32)
        t_row = lax.broadcasted_iota(I32, (tr, tr), 0)
        t_col = lax.broadcasted_iota(I32, (tr, tr), 1)
        before[...] = (t_row < t_col).astype(BF16)

    per = ne // ng
    neg = -jnp.inf
    scores = _sigmoid(lg_ref[...])
    sel = scores + bias_ref[...]
    sub = lax.broadcasted_iota(I32, (per, tr), 0)
    gs = []
    for g in range(ng):
        s_g = sel[g * per:(g + 1) * per, :]
        m1, f1 = _first_max(s_g, sub, per)
        m2 = jnp.max(jnp.where(sub == f1, neg, s_g), axis=0, keepdims=True)
        gs.append(m1 + m2)
    gsc = jnp.concatenate(gs, axis=0)
    gi = lax.broadcasted_iota(I32, (ng, tr), 0)
    keep = jnp.zeros((ng, tr), F32)
    for _ in range(topg):
        _, f = _first_max(gsc, gi, ng)
        pick = gi == f
        keep = jnp.where(pick, 1.0, keep)
        gsc = jnp.where(pick, neg, gsc)
    work = jnp.concatenate(
        [jnp.where(keep[g:g + 1, :] > 0.5, sel[g * per:(g + 1) * per, :], neg) for g in range(ng)], axis=0)
    ei = lax.broadcasted_iota(I32, (ne, tr), 0)
    picks, es, ws = [], [], []
    for _ in range(topk):
        _, f = _first_max(work, ei, ne)
        pick = ei == f
        picks.append(pick)
        es.append(f)
        ws.append(jnp.sum(jnp.where(pick, scores, 0.0), axis=0, keepdims=True))
        work = jnp.where(pick, neg, work)
    wsum = ws[0]
    for w in ws[1:]:
        wsum = wsum + w
    scale = ROUTED_SCALE / wsum
    chosen = picks[0]
    for p in picks[1:]:
        chosen = jnp.logical_or(chosen, p)
    chosen_f = chosen.astype(F32)
    prior = cnt_scr[:, 0:1] + jnp.dot(chosen_f.astype(BF16), before[...], preferred_element_type=F32)
    rs = [jnp.sum(jnp.where(p, prior, 0.0), axis=0, keepdims=True).astype(I32) for p in picks]
    pad_i = jnp.zeros((SUBLANES - topk, tr), I32)
    pad_f = jnp.zeros((SUBLANES - topk, tr), F32)
    step = pl.program_id(0)
    e_all[step] = jnp.concatenate(es + [pad_i], axis=0)
    r_all[step] = jnp.concatenate(rs + [pad_i], axis=0)
    w_ref[...] = jnp.concatenate([w * scale for w in ws] + [pad_f], axis=0)
    total = cnt_scr[:, 0:1] + jnp.sum(chosen_f, axis=1, keepdims=True)
    cnt_scr[...] = jnp.broadcast_to(total, cnt_scr.shape)
    cnt_ref[...] = jnp.broadcast_to(total, cnt_ref.shape).astype(I32)

    @pl.when(step == nt - 1)
    def _():
        padded = jnp.ceil(total * (1.0 / EXPERT_ROWS)) * EXPERT_ROWS
        on_lanes = jnp.transpose(jnp.broadcast_to(padded, (ne, LANES)))[0:SUBLANES, 0:ne]
        e_row = lax.broadcasted_iota(I32, (ne, ne), 0)
        e_col = lax.broadcasted_iota(I32, (ne, ne), 1)
        starts = jnp.sum(jnp.where(e_col < e_row, jnp.broadcast_to(on_lanes[0:1, :], (ne, ne)), 0.0),
                         axis=1, keepdims=True).astype(I32)
        tiles = tr // TOKEN_TILE
        for c in range(nt):
            e = e_all[c]
            base = jnp.zeros(e.shape, I32)
            for j in range(ne):
                base = jnp.where(e == j, starts[j:j + 1, 0:1], base)
            dest = base + r_all[c]
            for m in range(tiles):
                row0 = SUBLANES * (c * tiles + m)
                d_ref[row0:row0 + SUBLANES, :] = dest[:, TOKEN_TILE * m:TOKEN_TILE * (m + 1)]


def _token_tile(t, limit):
    return max(m for m in range(LANES, limit + 1, LANES) if t % m == 0)


def _route(logits_t, bias_col):
    ne, t = logits_t.shape
    tr = _token_tile(t, ROUTE_TILE)
    nt = t // tr
    dest_shape = (t // TOKEN_TILE * SUBLANES, TOKEN_TILE)
    return pl.pallas_call(
        functools.partial(_route_body, tr=tr, nt=nt, ne=ne, ng=N_GROUPS, topk=TOP_K, topg=TOPK_GROUPS),
        grid=(nt,),
        in_specs=[pl.BlockSpec((ne, tr), lambda i: (0, i)), pl.BlockSpec((ne, 1), lambda i: (0, 0))],
        out_specs=[pl.BlockSpec((SUBLANES, tr), lambda i: (0, i)), pl.BlockSpec((ne, LANES), lambda i: (0, 0)),
                   pl.BlockSpec(dest_shape, lambda i: (0, 0))],
        out_shape=[jax.ShapeDtypeStruct((SUBLANES, t), F32), jax.ShapeDtypeStruct((ne, LANES), I32),
                   jax.ShapeDtypeStruct(dest_shape, I32)],
        scratch_shapes=[pltpu.VMEM((ne, LANES), F32), pltpu.VMEM((tr, tr), BF16),
                        pltpu.VMEM((nt, SUBLANES, tr), I32), pltpu.VMEM((nt, SUBLANES, tr), I32)],
        compiler_params=pltpu.CompilerParams(dimension_semantics=("arbitrary",)),
        name="route",
    )(logits_t, bias_col)


def _sc_mesh():
    return plsc.VectorSubcoreMesh(core_axis_name="c", subcore_axis_name="s")


def _sc_worker_id():
    return lax.axis_index("s") * SC_CORES + lax.axis_index("c")


def _index_block(dest_ref, chunk, width):
    per_tile = TOKEN_TILE // width
    return dest_ref.at[chunk // per_tile, :, pl.ds((chunk % per_tile) * width, width)]


def _dispatch(hps, dest3, n_rows):
    w = SCATTER_ROWS
    width = hps[0].shape[1]
    bounds = [0]
    for h in hps:
        bounds.append(bounds[-1] + h.shape[0] // w)
    nch = bounds[-1]
    nsrc = len(hps)

    @functools.partial(
        pl.kernel, mesh=_sc_mesh(),
        out_type=jax.ShapeDtypeStruct((n_rows, width), U32),
        scratch_types=[pltpu.VMEM((2, SUBLANES, w), I32), pltpu.VMEM((2, w, width), U32),
                       pltpu.SemaphoreType.DMA, pltpu.SemaphoreType.DMA, pltpu.SemaphoreType.DMA],
        compiler_params=pltpu.CompilerParams(use_tc_tiling_on_sc=False),
        name="dispatch",
    )
    def run(*refs):
        src_refs, dest_ref, xs_ref = refs[:nsrc], refs[nsrc], refs[nsrc + 1]
        idx_v, rows_v = refs[nsrc + 2:nsrc + 4]
        sem_load = refs[nsrc + 4:nsrc + 6]
        sem_scatter = refs[nsrc + 6]
        wid = _sc_worker_id()

        def start_loads(c, slot):
            pltpu.async_copy(_index_block(dest_ref, c, w), idx_v.at[slot], sem_load[slot])
            for i, src in enumerate(src_refs):
                @pl.when(jnp.logical_and(c >= bounds[i], c < bounds[i + 1]))
                def _(src=src, lo=bounds[i]):
                    pltpu.async_copy(src.at[pl.ds((c - lo) * w, w)], rows_v.at[slot], sem_load[slot])

        def wait_loads(slot):
            pltpu.make_async_copy(_index_block(dest_ref, 0, w), idx_v.at[slot], sem_load[slot]).wait()
            pltpu.make_async_copy(src_refs[0].at[pl.ds(0, w)], rows_v.at[slot], sem_load[slot]).wait()

        @pl.when(wid < nch)
        def _():
            start_loads(wid, 0)

        @pl.loop(0, pl.cdiv(pl.cdiv(nch, SC_WORKERS), 2))
        def _(rr):
            for slot in range(2):
                c = (rr * 2 + slot) * SC_WORKERS + wid

                @pl.when(c < nch)
                def _(c=c, slot=slot):
                    wait_loads(slot)

                    @pl.when(c + SC_WORKERS < nch)
                    def _():
                        start_loads(c + SC_WORKERS, 1 - slot)

                    copies = [pltpu.async_copy(rows_v.at[slot], xs_ref.at[idx_v.at[slot, k]], sem_scatter)
                              for k in range(TOP_K)]
                    for cp in copies:
                        cp.wait()

    return run(*hps, dest3)


def _undispatch(ys, dest3, n_tokens, after=None):
    w = GATHER_ROWS
    width = ys.shape[1]
    nch = n_tokens // w

    @functools.partial(
        pl.kernel, mesh=_sc_mesh(),
        out_type=jax.ShapeDtypeStruct((TOP_K, n_tokens, width), U32),
        scratch_types=[pltpu.VMEM((2, SUBLANES, w), I32), pltpu.VMEM((2, TOP_K, w, width), U32),
                       pltpu.SemaphoreType.DMA, pltpu.SemaphoreType.DMA, pltpu.SemaphoreType.DMA],
        compiler_params=pltpu.CompilerParams(use_tc_tiling_on_sc=False),
        name="undispatch",
    )
    def run(*refs):
        ys_ref, dest_ref = refs[:2]
        z_ref, idx_v, bufs = refs[-6:-3]
        sem_gather = refs[-3:-1]
        sem_store = refs[-1]
        wid = _sc_worker_id()

        def start_gathers(c, slot):
            pltpu.sync_copy(_index_block(dest_ref, c, w), idx_v.at[slot])
            for k in range(TOP_K):
                pltpu.async_copy(ys_ref.at[idx_v.at[slot, k]], bufs.at[slot, k], sem_gather[slot])

        def wait_gathers(slot):
            for k in range(TOP_K):
                pltpu.make_async_copy(ys_ref.at[idx_v.at[slot, k]], bufs.at[slot, k], sem_gather[slot]).wait()

        @pl.when(wid < nch)
        def _():
            start_gathers(wid, 0)

        @pl.loop(0, pl.cdiv(pl.cdiv(nch, SC_WORKERS), 2))
        def _(rr):
            for slot in range(2):
                c = (rr * 2 + slot) * SC_WORKERS + wid

                @pl.when(c < nch)
                def _(c=c, slot=slot):
                    @pl.when(c + SC_WORKERS < nch)
                    def _():
                        start_gathers(c + SC_WORKERS, 1 - slot)

                    wait_gathers(slot)
                    stores = [pltpu.async_copy(bufs.at[slot, k], z_ref.at[k, pl.ds(c * w, w)], sem_store)
                              for k in range(TOP_K)]
                    for cp in stores:
                        cp.wait()

    return run(ys, dest3) if after is None else run(ys, dest3, after)


PART_SHIFT = 24


def _expert_body(*refs, tm, ne, nparts):
    cnt_refs = refs[:nparts]
    xs_refs = refs[nparts:2 * nparts]
    w1_ref, w3_ref, w2_ref = refs[2 * nparts:2 * nparts + 3]
    ys_refs = refs[2 * nparts + 3:3 * nparts + 3]
    (w1f, w3f, w2f, w1s, w3s, w2s, xbuf, ybuf, xlo, xhi, sched, sem_x, sem_y, sem_w) = refs[3 * nparts + 3:]
    blk_words = tm * ROW_WORDS
    half = ROW_WORDS * LANES
    shift = tm.bit_length() - 1

    def n_blocks_of(p, e):
        return lax.shift_right_logical(cnt_refs[p][e, 0] + (tm - 1), shift)

    def n_all(e):
        n = n_blocks_of(0, e)
        for p in range(1, nparts):
            n = n + n_blocks_of(p, e)
        return n

    def next_nonempty(e):
        return lax.while_loop(
            lambda c: jnp.logical_and(c < ne, n_all(jnp.minimum(c, ne - 1)) == 0), lambda c: c + 1, e)

    def plan(e, carry):
        i, starts = carry[0], list(carry[1:])
        for p in range(nparts):
            n = n_blocks_of(p, e)

            def put(j, c, p=p, i=i, start=starts[p]):
                sched[i + j] = (start + j) + (p << PART_SHIFT)
                return c

            lax.fori_loop(0, n, put, 0)
            i = i + n
            starts[p] = starts[p] + n
        return (i, *starts)

    nu = lax.fori_loop(0, ne, plan, (jnp.int32(0),) * (nparts + 1))[0]

    def rows_of(code):
        blk = code & ((1 << PART_SHIFT) - 1)
        return pl.ds(pl.multiple_of(blk * blk_words, blk_words), blk_words)

    def x_start(i, slot):
        code = sched[i]
        for p in range(nparts):
            @pl.when(lax.shift_right_logical(code, PART_SHIFT) == p)
            def _(p=p):
                pltpu.make_async_copy(xs_refs[p].at[rows_of(code), :], xbuf.at[slot], sem_x.at[slot]).start()

    def x_wait(slot):
        pltpu.make_async_copy(xs_refs[0].at[pl.ds(0, blk_words), :], xbuf.at[slot], sem_x.at[slot]).wait()

    def y_start(i, slot):
        code = sched[i]
        for p in range(nparts):
            @pl.when(lax.shift_right_logical(code, PART_SHIFT) == p)
            def _(p=p):
                pltpu.make_async_copy(ybuf.at[slot], ys_refs[p].at[rows_of(code), :], sem_y.at[slot]).start()

    def y_wait(slot):
        pltpu.make_async_copy(ybuf.at[slot], ys_refs[0].at[pl.ds(0, blk_words), :], sem_y.at[slot]).wait()

    def w_copies(e, ws):
        return [pltpu.make_async_copy(src.at[e], dst.at[ws], sem_w.at[ws])
                for src, dst in ((w1_ref, w1f), (w3_ref, w3f), (w2_ref, w2f))]

    for q in range(X_AHEAD):
        @pl.when(q < nu)
        def _(q=q):
            x_start(q, q)

    e_first = next_nonempty(jnp.int32(0))

    @pl.when(e_first < ne)
    def _():
        for cp in w_copies(e_first, 0):
            cp.start()

    def blocks(i, nblk):
        for b in range(nblk):
            x_wait((i + b) % X_SLOTS)
        for b in range(nblk):
            nxt = i + b + X_AHEAD

            @pl.when(nxt < nu)
            def _(nxt=nxt):
                x_start(nxt, nxt % X_SLOTS)

        for b in range(nblk):
            @pl.when(i + b >= Y_SLOTS)
            def _(b=b):
                y_wait((i + b) % Y_SLOTS)

        for b in range(nblk):
            xin = xbuf.at[(i + b) % X_SLOTS]
            for w in range(ROW_WORDS):
                lo, hi = _unpack_halves(_load_row_word(xin, w, tm))
                xlo[b, :, LANES * w:LANES * (w + 1)] = lo.astype(BF16)
                xhi[b, :, LANES * w:LANES * (w + 1)] = hi.astype(BF16)

        def up(b, wsc):
            return (jnp.dot(xlo[b], wsc[0:half, :], preferred_element_type=F32)
                    + jnp.dot(xhi[b], wsc[half:2 * half, :], preferred_element_type=F32))

        for b in range(nblk):
            hid = (_silu(up(b, w1s)) * up(b, w3s)).astype(BF16)
            y = jnp.dot(hid, w2s[...], preferred_element_type=F32)
            _store_rows(ybuf.at[(i + b) % Y_SLOTS], _pack_halves(y[:, 0:half], y[:, half:2 * half]))
        for b in range(nblk):
            y_start(i + b, (i + b) % Y_SLOTS)

    def per_expert(e, carry):
        i0, ws = carry
        n = n_all(e)

        @pl.when(n > 0)
        def _():
            for cp in w_copies(e, ws):
                cp.wait()
            w1s[...] = w1f[ws].astype(BF16)
            w3s[...] = w3f[ws].astype(BF16)
            w2s[...] = w2f[ws].astype(BF16)
            e_next = next_nonempty(e + 1)

            @pl.when(e_next < ne)
            def _():
                for cp in w_copies(e_next, 1 - ws):
                    cp.start()

            def pair(j, c):
                blocks(i0 + 2 * j, 2)
                return c

            lax.fori_loop(0, lax.shift_right_logical(n, 1), pair, 0)

            @pl.when(n % 2 == 1)
            def _():
                blocks(i0 + n - 1, 1)

        return i0 + n, jnp.where(n > 0, 1 - ws, ws)

    lax.fori_loop(0, ne, per_expert, (jnp.int32(0), jnp.int32(0)))

    for q in range(Y_SLOTS):
        @pl.when(nu > q)
        def _(q=q):
            y_wait((nu - 1 - q) % Y_SLOTS)


def _experts(counts, xss, w1e, w3e, w2e):
    tm = EXPERT_ROWS
    ne, d, de = w1e.shape
    half = ROW_WORDS * LANES
    nparts = len(xss)
    cap = sum(x.shape[0] // (tm * ROW_WORDS) for x in xss)
    assert cap < (1 << PART_SHIFT)
    anyspec = pl.BlockSpec(memory_space=pl.ANY)
    blk_buf = lambda n: pltpu.VMEM((n, tm * ROW_WORDS, LANES), U32)
    return pl.pallas_call(
        functools.partial(_expert_body, tm=tm, ne=ne, nparts=nparts),
        grid_spec=pltpu.PrefetchScalarGridSpec(
            num_scalar_prefetch=nparts,
            grid=(1,),
            in_specs=[anyspec] * (nparts + 3),
            out_specs=[anyspec] * nparts,
            scratch_shapes=[pltpu.VMEM((2, d, de), F32), pltpu.VMEM((2, d, de), F32), pltpu.VMEM((2, de, d), F32),
                            pltpu.VMEM((d, de), BF16), pltpu.VMEM((d, de), BF16), pltpu.VMEM((de, d), BF16),
                            blk_buf(X_SLOTS), blk_buf(Y_SLOTS),
                            pltpu.VMEM((2, tm, half), BF16), pltpu.VMEM((2, tm, half), BF16),
                            pltpu.SMEM((cap,), I32),
                            pltpu.SemaphoreType.DMA((X_SLOTS,)), pltpu.SemaphoreType.DMA((Y_SLOTS,)),
                            pltpu.SemaphoreType.DMA((2,))]),
        out_shape=[jax.ShapeDtypeStruct(x.shape, U32) for x in xss],
        compiler_params=pltpu.CompilerParams(dimension_semantics=("arbitrary",), vmem_limit_bytes=VMEM_LIMIT),
        name="experts",
    )(*counts, *xss, w1e, w3e, w2e)


def _combine_body(z_ref, x1_ref, scf_ref, shf_ref, gtf_ref, wt_ref, gffn_ref, gfin_ref, ws1_ref, ws3_ref, ws2_ref,
                  y_ref, xo, *, td, topk):
    half = ROW_WORDS * LANES
    hb = _modulated_rmsnorm(x1_ref[...], gffn_ref[...], scf_ref[...], shf_ref[...]).astype(BF16)
    up = lambda w_ref: jnp.dot(hb, w_ref[...], preferred_element_type=F32)
    xo[...] = jnp.dot((_silu(up(ws1_ref)) * up(ws3_ref)).astype(BF16), ws2_ref[...], preferred_element_type=F32)
    wt = wt_ref[...].T
    ws = [wt[:, k:k + 1] for k in range(topk)]
    sq = jnp.zeros((td, 1), F32)
    for j in range(ROW_WORDS):
        acc_lo = jnp.zeros((td, LANES), F32)
        acc_hi = jnp.zeros((td, LANES), F32)
        for k in range(topk):
            lo, hi = _unpack_halves(_load_row_word(z_ref.at[k], j, td))
            acc_lo = acc_lo + ws[k] * lo
            acc_hi = acc_hi + ws[k] * hi
        for base, acc in ((0, acc_lo), (half, acc_hi)):
            cols = slice(base + LANES * j, base + LANES * (j + 1))
            x = x1_ref[:, cols] + gtf_ref[:, cols] * (acc + xo[:, cols])
            xo[:, cols] = x
            sq = sq + jnp.sum(x * x, axis=-1, keepdims=True)
    rs = lax.rsqrt(sq / (2 * half) + EPS)
    y_ref[...] = xo[...] * rs * gfin_ref[...]


def _combine_body_into(*refs, td, topk):
    _combine_body(*refs[:11], *refs[12:], td=td, topk=topk)


def _combine(z, token0, wts, x1, mod, g_ffn, g_final, shared_w, *, rows_per_mod, per_row_mod, b0=0, out_rows=None,
             into=None):
    t, d = x1.shape
    td = min(t, COMBINE_TILE)
    tile0 = token0 // td
    out_rows = t if out_rows is None else out_rows
    if per_row_mod:
        mod_spec = lambda j: pl.BlockSpec((td, d), lambda i: (i, j))
        out0 = 0
    else:
        tiles_per_mod = rows_per_mod // td
        mod_spec = lambda j: pl.BlockSpec((None, None, 1, d), lambda i: (i // tiles_per_mod + b0, j, 0, 0))
        out0 = b0 * tiles_per_mod
    full = lambda a: pl.BlockSpec(a.shape, lambda i: (0,) * a.ndim)
    in_specs = [pl.BlockSpec((TOP_K, td * ROW_WORDS, LANES), lambda i: (0, i + tile0, 0)),
                pl.BlockSpec((td, d), lambda i: (i, 0)),
                mod_spec(4), mod_spec(3), mod_spec(5),
                pl.BlockSpec((SUBLANES, td), lambda i: (0, i + tile0)),
                full(g_ffn), full(g_final)] + [full(w) for w in shared_w]
    args = [z, x1, mod, mod, mod, wts, g_ffn, g_final, *shared_w]
    body, aliases = _combine_body, {}
    if into is not None:
        in_specs.append(pl.BlockSpec(memory_space=pl.ANY))
        args.append(into)
        body, aliases = _combine_body_into, {len(args) - 1: 0}
    return pl.pallas_call(
        functools.partial(body, td=td, topk=TOP_K),
        grid=(t // td,),
        in_specs=in_specs,
        out_specs=pl.BlockSpec((td, d), lambda i: (i + out0, 0)),
        scratch_shapes=[pltpu.VMEM((td, d), F32)],
        out_shape=jax.ShapeDtypeStruct((out_rows, d), F32),
        input_output_aliases=aliases,
        compiler_params=pltpu.CompilerParams(vmem_limit_bytes=VMEM_LIMIT),
        name="combine",
    )(*args)


def _log_gamma(nh):
    return np.log(1.0 - 2.0 ** (-5.0 - np.arange(nh, dtype=np.float32))).astype(np.float32)


def _retention_tables(length, nh, hd):
    c = math.gcd(length, RET_CHUNK)
    log_g = _log_gamma(nh)
    idx = np.arange(c, dtype=np.float32)
    rel = idx[:, None] - idx[None, :]
    mask = np.where(rel >= 0, np.exp(log_g[:, None, None] * np.maximum(rel, 0.0)), 0.0).astype(np.float32)
    q_decay = np.exp(log_g[None, :] * (idx[:, None] + 1.0)).astype(np.float32)
    k_decay = np.exp(log_g[None, :] * (c - 1.0 - idx[:, None])).astype(np.float32)
    chunk_decay = np.exp(log_g * np.float32(c)).astype(np.float32)
    qd = np.broadcast_to(q_decay.T[:, :, None], (nh, c, hd))
    kd = np.broadcast_to(k_decay.T[:, :, None], (nh, c, hd))
    cd = np.broadcast_to(chunk_decay[:, None, None], (nh, hd, hd))
    return tuple(jnp.asarray(t) for t in (mask, qd, kd, cd))


def kernel(x_prompt, x_sample, c_prompt, c_sample, state_conv, state_ret, w_ada, b_ada, g_mix, g_ffn, w_in,
           conv_w, conv_b, conv_norm_g, conv_norm_b, ret_norm_g, ret_norm_b, w_out, w_router, router_bias,
           w1, w3, w2, ws1, ws3, ws2, g_final):
    depth = w_ada.shape[0]
    assert depth == 1, "single-layer trunk"
    bp, lp, d = x_prompt.shape
    bs, ls, _ = x_sample.shape
    assert ls == 1
    dc = conv_w.shape[2]
    dr = ret_norm_g.shape[1]
    nh = RET_HEADS
    hd = dr // nh
    assert hd == LANES and lp % 256 == 0 and bs % TOKEN_TILE == 0 and d // 2 == ROW_WORDS * LANES
    ne = w_router.shape[2]
    row = lambda a: a.reshape(1, -1)

    mod_p, mod_s = _ada([c_prompt, c_sample], w_ada[0], row(b_ada[0]))
    mod_p = mod_p.reshape(bp, 6, 1, d)

    half = hd // 2
    inv = (np.float32(ROPE_BASE) ** (-np.arange(half, dtype=np.float32) / np.float32(half))).astype(np.float32)
    inv2 = np.concatenate([inv, inv]).reshape(1, hd)
    cos_p, sin_p = _rope_tables(inv2, lp, 0)
    cos_s, sin_s = _rope_tables(inv2, SUBLANES, PAST_LEN)

    w_in_b = w_in[0].astype(BF16)
    wo_b = w_out[0].astype(BF16)
    wr_t = w_router[0].T
    wrh = wr_t.astype(BF16)
    wrl = (wr_t - wrh.astype(F32)).astype(BF16)
    ws1_b, ws3_b, ws2_b = ws1[0].astype(BF16), ws3[0].astype(BF16), ws2[0].astype(BF16)
    dims = dict(dc=dc, dr=dr, hd=hd)

    tables = _retention_tables(lp, nh, hd)
    gam = jnp.asarray(np.broadcast_to(np.exp(_log_gamma(nh))[:, None, None], (nh, SUBLANES, hd)))
    norm_rows = (row(conv_b[0]), row(conv_norm_g[0]), row(conv_norm_b[0]), row(ret_norm_g[0]), row(ret_norm_b[0]))
    post_w = (row(g_ffn[0]), wo_b, wrh, wrl)
    shared_w = (ws1_b, ws3_b, ws2_b)
    bias_col = router_bias[0].reshape(ne, 1)
    hw = d // 2
    tm = EXPERT_ROWS

    def pre_prompt(b0, nb, after=None):
        glu, q, k, v, sg = _proj(x_prompt, mod_p, row(g_mix[0]), w_in_b, cos_p, sin_p,
                                 per_row_mod=False, b0=b0, nb=nb, after=after, **dims)
        cat, ret = _mix(glu, q, k, v, sg, conv_w[0], *norm_rows, tables, nb=nb, length=lp, **dims)
        x2, hp, lg = _post(cat, x_prompt, mod_p, *post_w, per_row_mod=False, b0=b0, nb=nb)
        return glu, ret, x2, hp, lg

    def pre_sample():
        xs3 = x_sample.reshape(1, bs, d)
        glu, q, k, v, sg = _proj(xs3, mod_s, row(g_mix[0]), w_in_b, cos_s, sin_s, per_row_mod=True, **dims)
        cat, ret, new_conv = _mix1(glu, q, k, v, sg, state_conv.transpose(0, 2, 1, 3), state_ret, conv_w[0],
                                   *norm_rows, gam, **dims)
        x2, hp, lg = _post(cat, xs3, mod_s, *post_w, per_row_mod=True)
        return new_conv.transpose(0, 2, 1, 3), ret, x2, hp, lg

    def route_and_dispatch(hps, lgs):
        lg = lgs[0] if len(lgs) == 1 else jnp.concatenate(lgs, axis=1)
        tokens = lg.shape[1]
        wts, counts, dest = _route(lg, bias_col)
        n_rows = -(-(tokens * TOP_K + ne * (tm - 1)) // tm) * tm
        dest3 = dest.reshape(tokens // TOKEN_TILE, SUBLANES, TOKEN_TILE)
        xs = _dispatch([h.reshape(-1, hw) for h in hps], dest3, n_rows)
        return counts, dest3, wts, xs.reshape(n_rows * ROW_WORDS, LANES)

    def undispatch(ys, dest3, after=None):
        tokens = dest3.shape[0] * TOKEN_TILE
        z = _undispatch(ys.reshape(-1, hw), dest3, tokens, after)
        return z.reshape(TOP_K, tokens * ROW_WORDS, LANES)

    nb0 = bp // 2
    nb1 = bp - nb0
    new_conv_s, ret_s, x2_s, hp_s, lg_s = pre_sample()
    glu_0, ret_0, x2_0, hp_0, lg_0 = pre_prompt(0, nb0, after=lg_s)
    counts_0, dest_0, wts_0, xs_0 = route_and_dispatch([hp_0], [lg_0])
    glu_1, ret_1, x2_1, hp_1, lg_1 = pre_prompt(nb0, nb1, after=dest_0)
    counts_1, dest_1, wts_1, xs_1 = route_and_dispatch([hp_1, hp_s], [lg_1, lg_s])
    ys_0, ys_1 = _experts((counts_0, counts_1), (xs_0, xs_1), w1[0], w3[0], w2[0])
    z_0 = undispatch(ys_0, dest_0)
    z_1 = undispatch(ys_1, dest_1, after=z_0[0, :SUBLANES])
    y_p = _combine(z_0, 0, wts_0, x2_0, mod_p, row(g_ffn[0]), row(g_final), shared_w, rows_per_mod=lp, per_row_mod=False,
                   out_rows=bp * lp)
    y_p = _combine(z_1, 0, wts_1, x2_1, mod_p, row(g_ffn[0]), row(g_final), shared_w, rows_per_mod=lp, per_row_mod=False,
                   b0=nb0, out_rows=bp * lp, into=y_p)
    y_s = _combine(z_1, nb1 * lp, wts_1, x2_s, mod_s, row(g_ffn[0]), row(g_final), shared_w, rows_per_mod=bs,
                   per_row_mod=True)
    ret_p = jnp.concatenate([ret_0, ret_1], axis=0)

    tail = lambda g, n: g.reshape(n, lp, dc)[:, lp - CONV_BUF:, :]
    new_conv_p = jnp.concatenate([tail(glu_0, nb0), tail(glu_1, nb1)], axis=0)
    return (y_p.reshape(bp, lp, d), y_s.reshape(bs, ls, d), new_conv_p[None], ret_p[None],
            new_conv_s, ret_s)
```

```python
import functools
import math

import jax
import jax.numpy as jnp
import numpy as np
from jax import lax
from jax.experimental import pallas as pl
from jax.experimental.pallas import tpu as pltpu
from jax.experimental.pallas import tpu_sc as plsc

F32 = jnp.float32
BF16 = jnp.bfloat16
U32 = jnp.uint32
I32 = jnp.int32

EPS = 1e-6
PAST_LEN = 16384
RET_HEADS = 4
RET_CHUNK = 128
CONV_WIDTH = 31
CONV_BUF = CONV_WIDTH - 1
ROPE_BASE = 10000.0
TOP_K = 6
N_GROUPS = 8
TOPK_GROUPS = 4
ROUTED_SCALE = 2.5

LANES = 128
SUBLANES = 8
CONV_PAD = 32
EXPERT_ROWS = 256
ROUTE_TILE = 1024
TOKEN_TILE = 128
COMBINE_TILE = 512
VMEM_LIMIT = 56 * 1024 * 1024
SC_CORES = 2
SC_SUBCORES = 16
SC_WORKERS = SC_CORES * SC_SUBCORES
SCATTER_ROWS = 64
GATHER_ROWS = 16
ROW_WORDS = 4
X_SLOTS = 6
X_AHEAD = 4
Y_SLOTS = 4

HI_MASK = 0xFFFF0000


def _sigmoid(x):
    return jax.nn.sigmoid(x)


def _silu(x):
    return x * jax.nn.sigmoid(x)


def _pack_halves(lo, hi):
    lo_u = lax.bitcast_convert_type(lo.astype(BF16).astype(F32), U32) >> 16
    hi_u = lax.bitcast_convert_type(hi.astype(BF16).astype(F32), U32) & jnp.uint32(HI_MASK)
    return hi_u | lo_u


def _unpack_halves(p):
    lo = lax.bitcast_convert_type(p << 16, F32)
    hi = lax.bitcast_convert_type(p & jnp.uint32(HI_MASK), F32)
    return lo, hi


def _store_rows(ref, x):
    rows = x.shape[0]
    for j in range(ROW_WORDS):
        ref[pl.ds(j, rows, stride=ROW_WORDS), :] = x[:, LANES * j:LANES * (j + 1)]


def _load_row_word(ref, j, rows):
    return ref[pl.ds(j, rows, stride=ROW_WORDS), :]


def _ada_body(w_ref, b_ref, *refs):
    w = w_ref[...].astype(BF16)
    groups = len(refs) // 2
    for c_ref, o_ref in zip(refs[:groups], refs[groups:]):
        o_ref[...] = jnp.dot(_silu(c_ref[...]).astype(BF16), w, preferred_element_type=F32) + b_ref[...]


def _ada(cs, w_ada, b_ada):
    d, n = w_ada.shape
    tn = 2048
    return pl.pallas_call(
        _ada_body,
        grid=(n // tn,),
        in_specs=[pl.BlockSpec((d, tn), lambda j: (0, j)), pl.BlockSpec((1, tn), lambda j: (0, j))]
        + [pl.BlockSpec(c.shape, lambda j: (0, 0)) for c in cs],
        out_specs=[pl.BlockSpec((c.shape[0], tn), lambda j: (0, j)) for c in cs],
        out_shape=[jax.ShapeDtypeStruct((c.shape[0], n), F32) for c in cs],
        compiler_params=pltpu.CompilerParams(vmem_limit_bytes=VMEM_LIMIT),
        name="ada",
    )(w_ada, b_ada, *cs)


def _rope_tables(inv2, rows, pos0):
    pos = np.arange(rows, dtype=np.float32) + np.float32(pos0)
    ang = (pos[:, None] * inv2).astype(np.float64)
    sign = np.where(np.arange(inv2.shape[1]) < inv2.shape[1] // 2, -1.0, 1.0)
    return jnp.asarray(np.cos(ang).astype(np.float32)), jnp.asarray((np.sin(ang) * sign).astype(np.float32))


def _modulated_rmsnorm(x, g, sc, sh):
    ms = jnp.mean(x * x, axis=-1, keepdims=True)
    h = x * lax.rsqrt(ms + EPS) * g
    return h * (1.0 + sc) + sh


def _proj_body(x_ref, sh_ref, sc_ref, g_ref, w_ref, cos_ref, sin_ref, *rest, dc, dr, hd, rope_rows):
    glu_ref, q_ref, k_ref, v_ref, sg_ref = rest[-5:]
    hb = _modulated_rmsnorm(x_ref[...], g_ref[...], sc_ref[...], sh_ref[...]).astype(BF16)

    def proj(lo, n):
        return jnp.dot(hb, w_ref[:, lo:lo + n], preferred_element_type=F32)

    glu_ref[...] = proj(0, dc) * _sigmoid(proj(dc, dc))
    cos = cos_ref[...] if rope_rows else cos_ref[0:1, :]
    sin = sin_ref[...] if rope_rows else sin_ref[0:1, :]
    for ref, lo, scale in ((q_ref, 2 * dc, hd ** -0.5), (k_ref, 2 * dc + dr, None)):
        t = proj(lo, dr)
        for hh in range(dr // hd):
            th = t[:, hh * hd:(hh + 1) * hd]
            r = th * cos + pltpu.roll(th, hd // 2, 1) * sin
            if scale is not None:
                r = r * scale
            ref[:, hh * hd:(hh + 1) * hd] = r.astype(BF16)
    v_ref[...] = proj(2 * dc + 2 * dr, dr).astype(BF16)
    sg_ref[...] = _silu(proj(2 * dc + 3 * dr, dr)).astype(BF16)


def _proj(x, mod, g_mix, w_in_b, cos2, sin2, *, dc, dr, hd, per_row_mod, b0=0, nb=None, after=None):
    nb_all, length, d = x.shape
    nb = nb_all if nb is None else nb
    tl = min(length, 512)
    x2 = x.reshape(nb_all * length, d)
    nl = length // tl
    if per_row_mod:
        mod_spec = lambda j: pl.BlockSpec((tl, d), lambda b, l: (l, j))
        rope_spec = pl.BlockSpec((SUBLANES, LANES), lambda b, l: (0, 0))
    else:
        mod_spec = lambda j: pl.BlockSpec((None, None, 1, d), lambda b, l: (b + b0, j, 0, 0))
        rope_spec = pl.BlockSpec((tl, LANES), lambda b, l: (l, 0))
    row_spec = lambda w: pl.BlockSpec((tl, w), lambda b, l: (b * nl + l, 0))
    t = nb * length
    outs = pl.pallas_call(
        functools.partial(_proj_body, dc=dc, dr=dr, hd=hd, rope_rows=not per_row_mod),
        grid=(nb, nl),
        in_specs=[
            pl.BlockSpec((tl, d), lambda b, l: ((b + b0) * nl + l, 0)), mod_spec(0), mod_spec(1),
            pl.BlockSpec((1, d), lambda b, l: (0, 0)),
            pl.BlockSpec(w_in_b.shape, lambda b, l: (0, 0)),
            rope_spec, rope_spec,
        ] + ([] if after is None else [pl.BlockSpec(memory_space=pl.ANY)]),
        out_specs=[row_spec(dc), row_spec(dr), row_spec(dr), row_spec(dr), row_spec(dr)],
        out_shape=[
            jax.ShapeDtypeStruct((t, dc), F32),
            jax.ShapeDtypeStruct((t, dr), BF16),
            jax.ShapeDtypeStruct((t, dr), BF16),
            jax.ShapeDtypeStruct((t, dr), BF16),
            jax.ShapeDtypeStruct((t, dr), BF16),
        ],
        compiler_params=pltpu.CompilerParams(vmem_limit_bytes=VMEM_LIMIT),
        name="proj",
    )(x2, mod, mod, g_mix, w_in_b, cos2, sin2, *(() if after is None else (after,)))
    return outs


def _layernorm_silu(c, g, b):
    mu = jnp.mean(c, axis=-1, keepdims=True)
    d = c - mu
    var = jnp.mean(d * d, axis=-1, keepdims=True)
    return _silu(d * lax.rsqrt(var + EPS) * g + b)


def _groupnorm(o, g, b):
    mu = jnp.mean(o, axis=-1, keepdims=True)
    d = o - mu
    var = jnp.mean(d * d, axis=-1, keepdims=True)
    return d * lax.rsqrt(var + EPS) * g + b


def _mix_body(glu_ref, q_ref, k_ref, v_ref, sg_ref, cw_ref, cb_ref, lng_ref, lnb_ref, rg_ref, rb_ref,
              mask_ref, qd_ref, kd_ref, cd_ref, cat_ref, st_ref, buf, cscr, *, tl, dc, hd, nh, chunk):
    nslab = dc // LANES

    @pl.when(pl.program_id(1) == 0)
    def _():
        buf[:, 0:CONV_PAD, :] = jnp.zeros((nslab, CONV_PAD, LANES), F32)
        st_ref[...] = jnp.zeros(st_ref.shape, F32)

    for j in range(nslab):
        buf[j, CONV_PAD:CONV_PAD + tl, :] = glu_ref[:, LANES * j:LANES * (j + 1)]
    first = CONV_PAD - CONV_BUF
    rows_per_iter = 8 * SUBLANES
    for j in range(nslab):
        cols = slice(LANES * j, LANES * (j + 1))
        wv = [jnp.broadcast_to(cw_ref[t:t + 1, cols], (SUBLANES, LANES)) for t in range(CONV_WIDTH)]
        bias = jnp.broadcast_to(cb_ref[0:1, cols], (SUBLANES, LANES))

        def body(r, carry, j=j, cols=cols, wv=wv, bias=bias):
            base = pl.multiple_of(r * rows_per_iter, rows_per_iter)
            for u in range(rows_per_iter // SUBLANES):
                acc = bias
                for t in range(CONV_WIDTH):
                    acc = acc + wv[t] * buf[j, pl.ds(base + (u * SUBLANES + first + t), SUBLANES), :]
                cscr[pl.ds(base + u * SUBLANES, SUBLANES), cols] = acc
            return carry

        lax.fori_loop(0, tl // rows_per_iter, body, 0)
    for j in range(nslab):
        buf[j, 0:CONV_PAD, :] = buf[j, tl:tl + CONV_PAD, :]
    cat_ref[:, 0:dc] = _layernorm_silu(cscr[...], lng_ref[...], lnb_ref[...]).astype(BF16)

    nt = (((1,), (1,)), ((), ()))
    tn = (((0,), (0,)), ((), ()))
    for c in range(tl // chunk):
        rows = slice(c * chunk, (c + 1) * chunk)
        for hh in range(nh):
            cols = slice(hh * hd, (hh + 1) * hd)
            qh = q_ref[rows, cols]
            kh = k_ref[rows, cols]
            vh = v_ref[rows, cols]
            s = st_ref[0, hh]
            scores = lax.dot_general(qh, kh, nt, preferred_element_type=F32) * mask_ref[hh]
            inner = jnp.dot(scores.astype(BF16), vh, preferred_element_type=F32)
            qd = (qh.astype(F32) * qd_ref[hh]).astype(BF16)
            cross = jnp.dot(qd, s.astype(BF16), preferred_element_type=F32)
            kd = (kh.astype(F32) * kd_ref[hh]).astype(BF16)
            st_ref[0, hh] = cd_ref[hh] * s + lax.dot_general(kd, vh, tn, preferred_element_type=F32)
            o = _groupnorm(inner + cross, rg_ref[0:1, cols], rb_ref[0:1, cols])
            cat_ref[rows, dc + hh * hd:dc + (hh + 1) * hd] = (o * sg_ref[rows, cols].astype(F32)).astype(BF16)


def _mix(glu, q, k, v, sg, conv_w, conv_b, ln_g, ln_b, rg, rb, tables, *, nb, length, dc, dr, hd):
    nh = dr // hd
    chunk = math.gcd(length, RET_CHUNK)
    tl = min(length, 256)
    nl = length // tl
    mask, qd, kd, cd = tables
    row_spec = lambda w: pl.BlockSpec((tl, w), lambda b, l: (b * nl + l, 0))
    full = lambda a: pl.BlockSpec(a.shape, lambda b, l: (0,) * a.ndim)
    cat, st = pl.pallas_call(
        functools.partial(_mix_body, tl=tl, dc=dc, hd=hd, nh=nh, chunk=chunk),
        grid=(nb, nl),
        in_specs=[row_spec(dc), row_spec(dr), row_spec(dr), row_spec(dr), row_spec(dr),
                  full(conv_w), full(conv_b), full(ln_g), full(ln_b), full(rg), full(rb),
                  full(mask), full(qd), full(kd), full(cd)],
        out_specs=[row_spec(dc + dr), pl.BlockSpec((1, nh, hd, hd), lambda b, l: (b, 0, 0, 0))],
        out_shape=[jax.ShapeDtypeStruct((nb * length, dc + dr), BF16),
                   jax.ShapeDtypeStruct((nb, nh, hd, hd), F32)],
        scratch_shapes=[pltpu.VMEM((dc // LANES, tl + CONV_PAD, LANES), F32),
                        pltpu.VMEM((tl, dc), F32)],
        compiler_params=pltpu.CompilerParams(dimension_semantics=("arbitrary", "arbitrary")),
        name="mix",
    )(glu, q, k, v, sg, conv_w, conv_b, ln_g, ln_b, rg, rb, mask, qd, kd, cd)
    return cat, st


def _mix1_body(glu_ref, q_ref, k_ref, v_ref, sg_ref, sc_ref, s0_ref, cw_ref, cb_ref, lng_ref, lnb_ref,
               rg_ref, rb_ref, gam_ref, cat_ref, st_ref, nc_ref, oscr, qf, kf, vf, *, tb, dc, hd, nh):
    glu = glu_ref[...]
    conv = glu * cw_ref[CONV_BUF:CONV_WIDTH, :] + cb_ref[...]
    for r in range(CONV_BUF):
        past = sc_ref[r]
        conv = conv + past * cw_ref[r:r + 1, :]
        if r > 0:
            nc_ref[r - 1] = past
    nc_ref[CONV_BUF - 1] = glu
    cat_ref[:, 0:dc] = _layernorm_silu(conv, lng_ref[...], lnb_ref[...]).astype(BF16)

    tn = (((0,), (0,)), ((), ()))
    rowid = lax.broadcasted_iota(I32, (tb, hd), 0)
    qf[...] = q_ref[...].astype(F32)
    kf[...] = k_ref[...].astype(F32)
    vf[...] = v_ref[...].astype(F32)
    for hh in range(nh):
        cols = slice(hh * hd, (hh + 1) * hd)
        qa = q_ref[:, cols]
        ka = k_ref[:, cols]
        gam = gam_ref[hh, 0:1, :]
        for bb in range(tb):
            onehot = (rowid == bb).astype(BF16)
            qcol = lax.dot_general(qa, onehot, tn, preferred_element_type=F32)
            kcol = lax.dot_general(ka, onehot, tn, preferred_element_type=F32)
            s0 = s0_ref[bb, hh]
            qrow = qf[bb:bb + 1, cols]
            krow = kf[bb:bb + 1, cols]
            vrow = vf[bb:bb + 1, cols]
            qk = jnp.sum(qrow * krow, axis=-1, keepdims=True)
            cross = gam * jnp.sum(qcol * s0, axis=0, keepdims=True)
            st_ref[bb, hh] = gam * s0 + kcol * vrow
            oscr[bb:bb + 1, cols] = qk * vrow + cross
    for hh in range(nh):
        cols = slice(hh * hd, (hh + 1) * hd)
        o = _groupnorm(oscr[:, cols], rg_ref[0:1, cols], rb_ref[0:1, cols])
        cat_ref[:, dc + hh * hd:dc + (hh + 1) * hd] = (o * sg_ref[:, cols].astype(F32)).astype(BF16)


def _mix1(glu, q, k, v, sg, state_conv, state_ret, conv_w, conv_b, ln_g, ln_b, rg, rb, gam, *, dc, dr, hd):
    nb = glu.shape[0]
    nh = dr // hd
    tb = 16
    row_spec = lambda w: pl.BlockSpec((tb, w), lambda i: (i, 0))
    full = lambda a: pl.BlockSpec(a.shape, lambda i: (0,) * a.ndim)
    st_spec = pl.BlockSpec((None, tb, nh, hd, hd), lambda i: (0, i, 0, 0, 0))
    conv_spec = pl.BlockSpec((None, CONV_BUF, tb, dc), lambda i: (0, 0, i, 0))
    cat, st, new_conv = pl.pallas_call(
        functools.partial(_mix1_body, tb=tb, dc=dc, hd=hd, nh=nh),
        grid=(nb // tb,),
        in_specs=[row_spec(dc), row_spec(dr), row_spec(dr), row_spec(dr), row_spec(dr), conv_spec, st_spec,
                  full(conv_w), full(conv_b), full(ln_g), full(ln_b), full(rg), full(rb), full(gam)],
        out_specs=[row_spec(dc + dr), st_spec, conv_spec],
        out_shape=[jax.ShapeDtypeStruct((nb, dc + dr), BF16),
                   jax.ShapeDtypeStruct((1, nb, nh, hd, hd), F32),
                   jax.ShapeDtypeStruct((1, CONV_BUF, nb, dc), F32)],
        scratch_shapes=[pltpu.VMEM((tb, dr), F32)] * 4,
        compiler_params=pltpu.CompilerParams(vmem_limit_bytes=VMEM_LIMIT),
        name="mix1",
    )(glu, q, k, v, sg, state_conv, state_ret, conv_w, conv_b, ln_g, ln_b, rg, rb, gam)
    return cat, st, new_conv


def _post_body(cat_ref, x_ref, gtm_ref, scf_ref, shf_ref, g_ref, wo_ref, wrh_ref, wrl_ref, x1_ref, hp_ref, lg_ref):
    d = x_ref.shape[1]
    y = jnp.dot(cat_ref[...], wo_ref[...], preferred_element_type=F32)
    x1 = x_ref[...] + gtm_ref[...] * y
    x1_ref[...] = x1
    h = _modulated_rmsnorm(x1, g_ref[...], scf_ref[...], shf_ref[...])
    hb = h.astype(BF16)
    _store_rows(hp_ref, _pack_halves(h[:, 0:d // 2], h[:, d // 2:d]))
    hl = (h - hb.astype(F32)).astype(BF16)
    nt = (((1,), (1,)), ((), ()))
    lg_ref[...] = (lax.dot_general(wrh_ref[...], hb, nt, preferred_element_type=F32)
                   + lax.dot_general(wrh_ref[...], hl, nt, preferred_element_type=F32)
                   + lax.dot_general(wrl_ref[...], hb, nt, preferred_element_type=F32))


def _post(cat, x, mod, g_ffn, wo_b, wrh, wrl, *, per_row_mod, b0=0, nb=None):
    nb_all, length, d = x.shape
    nb = nb_all if nb is None else nb
    tl = min(length, 1024)
    nl = length // tl
    t = nb * length
    ne = wrh.shape[0]
    x2d = x.reshape(nb_all * length, d)
    if per_row_mod:
        mod_spec = lambda j: pl.BlockSpec((tl, d), lambda b, l: (l, j))
    else:
        mod_spec = lambda j: pl.BlockSpec((None, None, 1, d), lambda b, l: (b + b0, j, 0, 0))
    row_spec = lambda w: pl.BlockSpec((tl, w), lambda b, l: (b * nl + l, 0))
    full = lambda a: pl.BlockSpec(a.shape, lambda b, l: (0,) * a.ndim)
    return pl.pallas_call(
        _post_body,
        grid=(nb, nl),
        in_specs=[row_spec(d), pl.BlockSpec((tl, d), lambda b, l: ((b + b0) * nl + l, 0)),
                  mod_spec(2), mod_spec(4), mod_spec(3),
                  full(g_ffn), full(wo_b), full(wrh), full(wrl)],
        out_specs=[row_spec(d),
                   pl.BlockSpec((tl * ROW_WORDS, LANES), lambda b, l: (b * nl + l, 0)),
                   pl.BlockSpec((ne, tl), lambda b, l: (0, b * nl + l))],
        out_shape=[jax.ShapeDtypeStruct((t, d), F32),
                   jax.ShapeDtypeStruct((t * ROW_WORDS, LANES), U32),
                   jax.ShapeDtypeStruct((ne, t), F32)],
        compiler_params=pltpu.CompilerParams(vmem_limit_bytes=VMEM_LIMIT,
                                             allow_input_fusion=[False] * 6 + [True] * 3),
        name="post",
    )(cat, x2d, mod, mod, mod, g_ffn, wo_b, wrh, wrl)


def _first_max(x, idx, sentinel):
    m = jnp.max(x, axis=0, keepdims=True)
    f = jnp.min(jnp.where(x == m, idx, sentinel), axis=0, keepdims=True)
    return m, f


def _route_body(lg_ref, bias_ref, w_ref, cnt_ref, d_ref, cnt_scr, before, e_all, r_all, *, tr, nt, ne, ng, topk, topg):
    @pl.when(pl.program_id(0) == 0)
    def _():
        cnt_scr[...] = jnp.zeros(cnt_scr.shape, F32)
        t_row = lax.broadcasted_iota(I32, (tr, tr), 0)
        t_col = lax.broadcasted_iota(I32, (tr, tr), 1)
        before[...] = (t_row < t_col).astype(BF16)

    per = ne // ng
    neg = -jnp.inf
    scores = _sigmoid(lg_ref[...])
    sel = scores + bias_ref[...]
    sub = lax.broadcasted_iota(I32, (per, tr), 0)
    gs = []
    for g in range(ng):
        s_g = sel[g * per:(g + 1) * per, :]
        m1, f1 = _first_max(s_g, sub, per)
        m2 = jnp.max(jnp.where(sub == f1, neg, s_g), axis=0, keepdims=True)
        gs.append(m1 + m2)
    gsc = jnp.concatenate(gs, axis=0)
    gi = lax.broadcasted_iota(I32, (ng, tr), 0)
    keep = jnp.zeros((ng, tr), F32)
    for _ in range(topg):
        _, f = _first_max(gsc, gi, ng)
        pick = gi == f
        keep = jnp.where(pick, 1.0, keep)
        gsc = jnp.where(pick, neg, gsc)
    work = jnp.concatenate(
        [jnp.where(keep[g:g + 1, :] > 0.5, sel[g * per:(g + 1) * per, :], neg) for g in range(ng)], axis=0)
    ei = lax.broadcasted_iota(I32, (ne, tr), 0)
    picks, es, ws = [], [], []
    for _ in range(topk):
        _, f = _first_max(work, ei, ne)
        pick = ei == f
        picks.append(pick)
        es.append(f)
        ws.append(jnp.sum(jnp.where(pick, scores, 0.0), axis=0, keepdims=True))
        work = jnp.where(pick, neg, work)
    wsum = ws[0]
    for w in ws[1:]:
        wsum = wsum + w
    scale = ROUTED_SCALE / wsum
    chosen = picks[0]
    for p in picks[1:]:
        chosen = jnp.logical_or(chosen, p)
    chosen_f = chosen.astype(F32)
    prior = cnt_scr[:, 0:1] + jnp.dot(chosen_f.astype(BF16), before[...], preferred_element_type=F32)
    rs = [jnp.sum(jnp.where(p, prior, 0.0), axis=0, keepdims=True).astype(I32) for p in picks]
    pad_i = jnp.zeros((SUBLANES - topk, tr), I32)
    pad_f = jnp.zeros((SUBLANES - topk, tr), F32)
    step = pl.program_id(0)
    e_all[step] = jnp.concatenate(es + [pad_i], axis=0)
    r_all[step] = jnp.concatenate(rs + [pad_i], axis=0)
    w_ref[...] = jnp.concatenate([w * scale for w in ws] + [pad_f], axis=0)
    total = cnt_scr[:, 0:1] + jnp.sum(chosen_f, axis=1, keepdims=True)
    cnt_scr[...] = jnp.broadcast_to(total, cnt_scr.shape)
    cnt_ref[...] = jnp.broadcast_to(total, cnt_ref.shape)

    @pl.when(step == nt - 1)
    def _():
        padded = jnp.ceil(total * (1.0 / EXPERT_ROWS)) * EXPERT_ROWS
        on_lanes = jnp.transpose(jnp.broadcast_to(padded, (ne, LANES)))[0:SUBLANES, 0:ne]
        e_row = lax.broadcasted_iota(I32, (ne, ne), 0)
        e_col = lax.broadcasted_iota(I32, (ne, ne), 1)
        starts = jnp.sum(jnp.where(e_col < e_row, jnp.broadcast_to(on_lanes[0:1, :], (ne, ne)), 0.0),
                         axis=1, keepdims=True).astype(I32)
        tiles = tr // TOKEN_TILE
        for c in range(nt):
            e = e_all[c]
            base = jnp.zeros(e.shape, I32)
            for j in range(ne):
                base = jnp.where(e == j, starts[j:j + 1, 0:1], base)
            dest = base + r_all[c]
            for m in range(tiles):
                row0 = SUBLANES * (c * tiles + m)
                d_ref[row0:row0 + SUBLANES, :] = dest[:, TOKEN_TILE * m:TOKEN_TILE * (m + 1)]


def _token_tile(t, limit):
    return max(m for m in range(LANES, limit + 1, LANES) if t % m == 0)


def _route(logits_t, bias_col):
    ne, t = logits_t.shape
    tr = _token_tile(t, ROUTE_TILE)
    nt = t // tr
    dest_shape = (t // TOKEN_TILE * SUBLANES, TOKEN_TILE)
    return pl.pallas_call(
        functools.partial(_route_body, tr=tr, nt=nt, ne=ne, ng=N_GROUPS, topk=TOP_K, topg=TOPK_GROUPS),
        grid=(nt,),
        in_specs=[pl.BlockSpec((ne, tr), lambda i: (0, i)), pl.BlockSpec((ne, 1), lambda i: (0, 0))],
        out_specs=[pl.BlockSpec((SUBLANES, tr), lambda i: (0, i)), pl.BlockSpec((ne, LANES), lambda i: (0, 0)),
                   pl.BlockSpec(dest_shape, lambda i: (0, 0))],
        out_shape=[jax.ShapeDtypeStruct((SUBLANES, t), F32), jax.ShapeDtypeStruct((ne, LANES), F32),
                   jax.ShapeDtypeStruct(dest_shape, I32)],
        scratch_shapes=[pltpu.VMEM((ne, LANES), F32), pltpu.VMEM((tr, tr), BF16),
                        pltpu.VMEM((nt, SUBLANES, tr), I32), pltpu.VMEM((nt, SUBLANES, tr), I32)],
        compiler_params=pltpu.CompilerParams(dimension_semantics=("arbitrary",), allow_input_fusion=[True, False]),
        name="route",
    )(logits_t, bias_col)


def _sc_mesh():
    return plsc.VectorSubcoreMesh(core_axis_name="c", subcore_axis_name="s")


def _sc_worker_id():
    return lax.axis_index("s") * SC_CORES + lax.axis_index("c")


def _index_block(dest_ref, chunk, width):
    per_tile = TOKEN_TILE // width
    return dest_ref.at[chunk // per_tile, :, pl.ds((chunk % per_tile) * width, width)]


def _dispatch(hps, dest3, n_rows):
    w = SCATTER_ROWS
    width = hps[0].shape[1]
    bounds = [0]
    for h in hps:
        bounds.append(bounds[-1] + h.shape[0] // w)
    nch = bounds[-1]
    nsrc = len(hps)

    @functools.partial(
        pl.kernel, mesh=_sc_mesh(),
        out_type=jax.ShapeDtypeStruct((n_rows, width), U32),
        scratch_types=[pltpu.VMEM((2, SUBLANES, w), I32), pltpu.VMEM((2, w, width), U32),
                       pltpu.SemaphoreType.DMA, pltpu.SemaphoreType.DMA, pltpu.SemaphoreType.DMA],
        compiler_params=pltpu.CompilerParams(use_tc_tiling_on_sc=False),
        name="dispatch",
    )
    def run(*refs):
        src_refs, dest_ref, xs_ref = refs[:nsrc], refs[nsrc], refs[nsrc + 1]
        idx_v, rows_v = refs[nsrc + 2:nsrc + 4]
        sem_load = refs[nsrc + 4:nsrc + 6]
        sem_scatter = refs[nsrc + 6]
        wid = _sc_worker_id()

        def start_loads(c, slot):
            pltpu.async_copy(_index_block(dest_ref, c, w), idx_v.at[slot], sem_load[slot])
            for i, src in enumerate(src_refs):
                @pl.when(jnp.logical_and(c >= bounds[i], c < bounds[i + 1]))
                def _(src=src, lo=bounds[i]):
                    pltpu.async_copy(src.at[pl.ds((c - lo) * w, w)], rows_v.at[slot], sem_load[slot])

        def wait_loads(slot):
            pltpu.make_async_copy(_index_block(dest_ref, 0, w), idx_v.at[slot], sem_load[slot]).wait()
            pltpu.make_async_copy(src_refs[0].at[pl.ds(0, w)], rows_v.at[slot], sem_load[slot]).wait()

        @pl.when(wid < nch)
        def _():
            start_loads(wid, 0)

        @pl.loop(0, pl.cdiv(pl.cdiv(nch, SC_WORKERS), 2))
        def _(rr):
            for slot in range(2):
                c = (rr * 2 + slot) * SC_WORKERS + wid

                @pl.when(c < nch)
                def _(c=c, slot=slot):
                    wait_loads(slot)

                    @pl.when(c + SC_WORKERS < nch)
                    def _():
                        start_loads(c + SC_WORKERS, 1 - slot)

                    copies = [pltpu.async_copy(rows_v.at[slot], xs_ref.at[idx_v.at[slot, k]], sem_scatter)
                              for k in range(TOP_K)]
                    for cp in copies:
                        cp.wait()

    return run(*hps, dest3)


def _undispatch(ys, dest3, n_tokens, after=None):
    w = GATHER_ROWS
    width = ys.shape[1]
    nch = n_tokens // w

    @functools.partial(
        pl.kernel, mesh=_sc_mesh(),
        out_type=jax.ShapeDtypeStruct((TOP_K, n_tokens, width), U32),
        scratch_types=[pltpu.VMEM((2, SUBLANES, w), I32), pltpu.VMEM((2, TOP_K, w, width), U32),
                       pltpu.SemaphoreType.DMA, pltpu.SemaphoreType.DMA, pltpu.SemaphoreType.DMA],
        compiler_params=pltpu.CompilerParams(use_tc_tiling_on_sc=False),
        name="undispatch",
    )
    def run(*refs):
        ys_ref, dest_ref = refs[:2]
        z_ref, idx_v, bufs = refs[-6:-3]
        sem_gather = refs[-3:-1]
        sem_store = refs[-1]
        wid = _sc_worker_id()

        def start_gathers(c, slot):
            pltpu.sync_copy(_index_block(dest_ref, c, w), idx_v.at[slot])
            for k in range(TOP_K):
                pltpu.async_copy(ys_ref.at[idx_v.at[slot, k]], bufs.at[slot, k], sem_gather[slot])

        def wait_gathers(slot):
            for k in range(TOP_K):
                pltpu.make_async_copy(ys_ref.at[idx_v.at[slot, k]], bufs.at[slot, k], sem_gather[slot]).wait()

        @pl.when(wid < nch)
        def _():
            start_gathers(wid, 0)

        @pl.loop(0, pl.cdiv(pl.cdiv(nch, SC_WORKERS), 2))
        def _(rr):
            for slot in range(2):
                c = (rr * 2 + slot) * SC_WORKERS + wid

                @pl.when(c < nch)
                def _(c=c, slot=slot):
                    @pl.when(c + SC_WORKERS < nch)
                    def _():
                        start_gathers(c + SC_WORKERS, 1 - slot)

                    wait_gathers(slot)
                    stores = [pltpu.async_copy(bufs.at[slot, k], z_ref.at[k, pl.ds(c * w, w)], sem_store)
                              for k in range(TOP_K)]
                    for cp in stores:
                        cp.wait()

    return run(ys, dest3) if after is None else run(ys, dest3, after)


PART_SHIFT = 24


def _expert_body(*refs, tm, ne, nparts):
    cnt_refs = refs[:nparts]
    xs_refs = refs[nparts:2 * nparts]
    w1_ref, w3_ref, w2_ref = refs[2 * nparts:2 * nparts + 3]
    ys_refs = refs[2 * nparts + 3:3 * nparts + 3]
    (w1f, w3f, w2f, w1s, w3s, w2s, xbuf, ybuf, xlo, xhi, sched, sem_x, sem_y, sem_w) = refs[3 * nparts + 3:]
    blk_words = tm * ROW_WORDS
    half = ROW_WORDS * LANES
    shift = tm.bit_length() - 1

    def n_blocks_of(p, e):
        return lax.shift_right_logical(cnt_refs[p][e] + (tm - 1), shift)

    def n_all(e):
        n = n_blocks_of(0, e)
        for p in range(1, nparts):
            n = n + n_blocks_of(p, e)
        return n

    def next_nonempty(e):
        return lax.while_loop(
            lambda c: jnp.logical_and(c < ne, n_all(jnp.minimum(c, ne - 1)) == 0), lambda c: c + 1, e)

    def plan(e, carry):
        i, starts = carry[0], list(carry[1:])
        for p in range(nparts):
            n = n_blocks_of(p, e)

            def put(j, c, p=p, i=i, start=starts[p]):
                sched[i + j] = (start + j) + (p << PART_SHIFT)
                return c

            lax.fori_loop(0, n, put, 0)
            i = i + n
            starts[p] = starts[p] + n
        return (i, *starts)

    nu = lax.fori_loop(0, ne, plan, (jnp.int32(0),) * (nparts + 1))[0]

    def rows_of(code):
        blk = code & ((1 << PART_SHIFT) - 1)
        return pl.ds(pl.multiple_of(blk * blk_words, blk_words), blk_words)

    def x_start(i, slot):
        code = sched[i]
        for p in range(nparts):
            @pl.when(lax.shift_right_logical(code, PART_SHIFT) == p)
            def _(p=p):
                pltpu.make_async_copy(xs_refs[p].at[rows_of(code), :], xbuf.at[slot], sem_x.at[slot]).start()

    def x_wait(slot):
        pltpu.make_async_copy(xs_refs[0].at[pl.ds(0, blk_words), :], xbuf.at[slot], sem_x.at[slot]).wait()

    def y_start(i, slot):
        code = sched[i]
        for p in range(nparts):
            @pl.when(lax.shift_right_logical(code, PART_SHIFT) == p)
            def _(p=p):
                pltpu.make_async_copy(ybuf.at[slot], ys_refs[p].at[rows_of(code), :], sem_y.at[slot]).start()

    def y_wait(slot):
        pltpu.make_async_copy(ybuf.at[slot], ys_refs[0].at[pl.ds(0, blk_words), :], sem_y.at[slot]).wait()

    def w_copies(e, ws):
        return [pltpu.make_async_copy(src.at[e], dst.at[ws], sem_w.at[ws])
                for src, dst in ((w1_ref, w1f), (w3_ref, w3f), (w2_ref, w2f))]

    for q in range(X_AHEAD):
        @pl.when(q < nu)
        def _(q=q):
            x_start(q, q)

    e_first = next_nonempty(jnp.int32(0))

    @pl.when(e_first < ne)
    def _():
        for cp in w_copies(e_first, 0):
            cp.start()

    def blocks(i, nblk):
        for b in range(nblk):
            x_wait((i + b) % X_SLOTS)
        for b in range(nblk):
            nxt = i + b + X_AHEAD

            @pl.when(nxt < nu)
            def _(nxt=nxt):
                x_start(nxt, nxt % X_SLOTS)

        for b in range(nblk):
            @pl.when(i + b >= Y_SLOTS)
            def _(b=b):
                y_wait((i + b) % Y_SLOTS)

        for b in range(nblk):
            xin = xbuf.at[(i + b) % X_SLOTS]
            for w in range(ROW_WORDS):
                lo, hi = _unpack_halves(_load_row_word(xin, w, tm))
                xlo[b, :, LANES * w:LANES * (w + 1)] = lo.astype(BF16)
                xhi[b, :, LANES * w:LANES * (w + 1)] = hi.astype(BF16)

        def up(b, wsc):
            return (jnp.dot(xlo[b], wsc[0:half, :], preferred_element_type=F32)
                    + jnp.dot(xhi[b], wsc[half:2 * half, :], preferred_element_type=F32))

        for b in range(nblk):
            hid = (_silu(up(b, w1s)) * up(b, w3s)).astype(BF16)
            y = jnp.dot(hid, w2s[...], preferred_element_type=F32)
            _store_rows(ybuf.at[(i + b) % Y_SLOTS], _pack_halves(y[:, 0:half], y[:, half:2 * half]))
        for b in range(nblk):
            y_start(i + b, (i + b) % Y_SLOTS)

    def per_expert(e, carry):
        i0, ws = carry
        n = n_all(e)

        @pl.when(n > 0)
        def _():
            for cp in w_copies(e, ws):
                cp.wait()
            w1s[...] = w1f[ws].astype(BF16)
            w3s[...] = w3f[ws].astype(BF16)
            w2s[...] = w2f[ws].astype(BF16)
            e_next = next_nonempty(e + 1)

            @pl.when(e_next < ne)
            def _():
                for cp in w_copies(e_next, 1 - ws):
                    cp.start()

            def pair(j, c):
                blocks(i0 + 2 * j, 2)
                return c

            lax.fori_loop(0, lax.shift_right_logical(n, 1), pair, 0)

            @pl.when(n % 2 == 1)
            def _():
                blocks(i0 + n - 1, 1)

        return i0 + n, jnp.where(n > 0, 1 - ws, ws)

    lax.fori_loop(0, ne, per_expert, (jnp.int32(0), jnp.int32(0)))

    for q in range(Y_SLOTS):
        @pl.when(nu > q)
        def _(q=q):
            y_wait((nu - 1 - q) % Y_SLOTS)


def _experts(counts, xss, w1e, w3e, w2e):
    tm = EXPERT_ROWS
    ne, d, de = w1e.shape
    half = ROW_WORDS * LANES
    nparts = len(xss)
    cap = sum(x.shape[0] // (tm * ROW_WORDS) for x in xss)
    assert cap < (1 << PART_SHIFT)
    anyspec = pl.BlockSpec(memory_space=pl.ANY)
    blk_buf = lambda n: pltpu.VMEM((n, tm * ROW_WORDS, LANES), U32)
    return pl.pallas_call(
        functools.partial(_expert_body, tm=tm, ne=ne, nparts=nparts),
        grid_spec=pltpu.PrefetchScalarGridSpec(
            num_scalar_prefetch=nparts,
            grid=(1,),
            in_specs=[anyspec] * (nparts + 3),
            out_specs=[anyspec] * nparts,
            scratch_shapes=[pltpu.VMEM((2, d, de), F32), pltpu.VMEM((2, d, de), F32), pltpu.VMEM((2, de, d), F32),
                            pltpu.VMEM((d, de), BF16), pltpu.VMEM((d, de), BF16), pltpu.VMEM((de, d), BF16),
                            blk_buf(X_SLOTS), blk_buf(Y_SLOTS),
                            pltpu.VMEM((2, tm, half), BF16), pltpu.VMEM((2, tm, half), BF16),
                            pltpu.SMEM((cap,), I32),
                            pltpu.SemaphoreType.DMA((X_SLOTS,)), pltpu.SemaphoreType.DMA((Y_SLOTS,)),
                            pltpu.SemaphoreType.DMA((2,))]),
        out_shape=[jax.ShapeDtypeStruct(x.shape, U32) for x in xss],
        compiler_params=pltpu.CompilerParams(dimension_semantics=("arbitrary",), vmem_limit_bytes=VMEM_LIMIT),
        name="experts",
    )(*counts, *xss, w1e, w3e, w2e)


def _combine_body(z_ref, x1_ref, scf_ref, shf_ref, gtf_ref, wt_ref, gffn_ref, gfin_ref, ws1_ref, ws3_ref, ws2_ref,
                  y_ref, xo, *, td, topk):
    half = ROW_WORDS * LANES
    hb = _modulated_rmsnorm(x1_ref[...], gffn_ref[...], scf_ref[...], shf_ref[...]).astype(BF16)
    up = lambda w_ref: jnp.dot(hb, w_ref[...], preferred_element_type=F32)
    xo[...] = jnp.dot((_silu(up(ws1_ref)) * up(ws3_ref)).astype(BF16), ws2_ref[...], preferred_element_type=F32)
    wt = wt_ref[...].T
    ws = [wt[:, k:k + 1] for k in range(topk)]
    sq = jnp.zeros((td, 1), F32)
    for j in range(ROW_WORDS):
        acc_lo = jnp.zeros((td, LANES), F32)
        acc_hi = jnp.zeros((td, LANES), F32)
        for k in range(topk):
            lo, hi = _unpack_halves(_load_row_word(z_ref.at[k], j, td))
            acc_lo = acc_lo + ws[k] * lo
            acc_hi = acc_hi + ws[k] * hi
        for base, acc in ((0, acc_lo), (half, acc_hi)):
            cols = slice(base + LANES * j, base + LANES * (j + 1))
            x = x1_ref[:, cols] + gtf_ref[:, cols] * (acc + xo[:, cols])
            xo[:, cols] = x
            sq = sq + jnp.sum(x * x, axis=-1, keepdims=True)
    rs = lax.rsqrt(sq / (2 * half) + EPS)
    y_ref[...] = xo[...] * rs * gfin_ref[...]


def _combine_body_into(*refs, td, topk):
    _combine_body(*refs[:11], *refs[12:], td=td, topk=topk)


def _combine(z, token0, wts, x1, mod, g_ffn, g_final, shared_w, *, rows_per_mod, per_row_mod, b0=0, out_rows=None,
             into=None):
    t, d = x1.shape
    td = min(t, COMBINE_TILE)
    tile0 = token0 // td
    out_rows = t if out_rows is None else out_rows
    if per_row_mod:
        mod_spec = lambda j: pl.BlockSpec((td, d), lambda i: (i, j))
        out0 = 0
    else:
        tiles_per_mod = rows_per_mod // td
        mod_spec = lambda j: pl.BlockSpec((None, None, 1, d), lambda i: (i // tiles_per_mod + b0, j, 0, 0))
        out0 = b0 * tiles_per_mod
    full = lambda a: pl.BlockSpec(a.shape, lambda i: (0,) * a.ndim)
    in_specs = [pl.BlockSpec((TOP_K, td * ROW_WORDS, LANES), lambda i: (0, i + tile0, 0)),
                pl.BlockSpec((td, d), lambda i: (i, 0)),
                mod_spec(4), mod_spec(3), mod_spec(5),
                pl.BlockSpec((SUBLANES, td), lambda i: (0, i + tile0)),
                full(g_ffn), full(g_final)] + [full(w) for w in shared_w]
    args = [z, x1, mod, mod, mod, wts, g_ffn, g_final, *shared_w]
    body, aliases = _combine_body, {}
    if into is not None:
        in_specs.append(pl.BlockSpec(memory_space=pl.ANY))
        args.append(into)
        body, aliases = _combine_body_into, {len(args) - 1: 0}
    return pl.pallas_call(
        functools.partial(body, td=td, topk=TOP_K),
        grid=(t // td,),
        in_specs=in_specs,
        out_specs=pl.BlockSpec((td, d), lambda i: (i + out0, 0)),
        scratch_shapes=[pltpu.VMEM((td, d), F32)],
        out_shape=jax.ShapeDtypeStruct((out_rows, d), F32),
        input_output_aliases=aliases,
        compiler_params=pltpu.CompilerParams(vmem_limit_bytes=VMEM_LIMIT),
        name="combine",
    )(*args)


def _log_gamma(nh):
    return np.log(1.0 - 2.0 ** (-5.0 - np.arange(nh, dtype=np.float32))).astype(np.float32)


def _retention_tables(length, nh, hd):
    c = math.gcd(length, RET_CHUNK)
    log_g = _log_gamma(nh)
    idx = np.arange(c, dtype=np.float32)
    rel = idx[:, None] - idx[None, :]
    mask = np.where(rel >= 0, np.exp(log_g[:, None, None] * np.maximum(rel, 0.0)), 0.0).astype(np.float32)
    q_decay = np.exp(log_g[None, :] * (idx[:, None] + 1.0)).astype(np.float32)
    k_decay = np.exp(log_g[None, :] * (c - 1.0 - idx[:, None])).astype(np.float32)
    chunk_decay = np.exp(log_g * np.float32(c)).astype(np.float32)
    qd = np.broadcast_to(q_decay.T[:, :, None], (nh, c, hd))
    kd = np.broadcast_to(k_decay.T[:, :, None], (nh, c, hd))
    cd = np.broadcast_to(chunk_decay[:, None, None], (nh, hd, hd))
    return tuple(jnp.asarray(t) for t in (mask, qd, kd, cd))


def kernel(x_prompt, x_sample, c_prompt, c_sample, state_conv, state_ret, w_ada, b_ada, g_mix, g_ffn, w_in,
           conv_w, conv_b, conv_norm_g, conv_norm_b, ret_norm_g, ret_norm_b, w_out, w_router, router_bias,
           w1, w3, w2, ws1, ws3, ws2, g_final):
    depth = w_ada.shape[0]
    assert depth == 1, "single-layer trunk"
    bp, lp, d = x_prompt.shape
    bs, ls, _ = x_sample.shape
    assert ls == 1
    dc = conv_w.shape[2]
    dr = ret_norm_g.shape[1]
    nh = RET_HEADS
    hd = dr // nh
    assert hd == LANES and lp % 256 == 0 and bs % TOKEN_TILE == 0 and d // 2 == ROW_WORDS * LANES
    ne = w_router.shape[2]
    row = lambda a: a.reshape(1, -1)

    mod_p, mod_s = _ada([c_prompt, c_sample], w_ada[0], row(b_ada[0]))
    mod_p = mod_p.reshape(bp, 6, 1, d)

    half = hd // 2
    inv = (np.float32(ROPE_BASE) ** (-np.arange(half, dtype=np.float32) / np.float32(half))).astype(np.float32)
    inv2 = np.concatenate([inv, inv]).reshape(1, hd)
    cos_p, sin_p = _rope_tables(inv2, lp, 0)
    cos_s, sin_s = _rope_tables(inv2, SUBLANES, PAST_LEN)

    w_in_b = w_in[0].astype(BF16)
    wo_b = w_out[0].astype(BF16)
    wr_t = w_router[0].T
    wrh = wr_t.astype(BF16)
    wrl = (wr_t - wrh.astype(F32)).astype(BF16)
    ws1_b, ws3_b, ws2_b = ws1[0].astype(BF16), ws3[0].astype(BF16), ws2[0].astype(BF16)
    dims = dict(dc=dc, dr=dr, hd=hd)

    tables = _retention_tables(lp, nh, hd)
    gam = jnp.asarray(np.broadcast_to(np.exp(_log_gamma(nh))[:, None, None], (nh, SUBLANES, hd)))
    norm_rows = (row(conv_b[0]), row(conv_norm_g[0]), row(conv_norm_b[0]), row(ret_norm_g[0]), row(ret_norm_b[0]))
    post_w = (row(g_ffn[0]), wo_b, wrh, wrl)
    shared_w = (ws1_b, ws3_b, ws2_b)
    bias_col = router_bias[0].reshape(ne, 1)
    hw = d // 2
    tm = EXPERT_ROWS

    def pre_prompt(b0, nb, after=None):
        glu, q, k, v, sg = _proj(x_prompt, mod_p, row(g_mix[0]), w_in_b, cos_p, sin_p,
                                 per_row_mod=False, b0=b0, nb=nb, after=after, **dims)
        cat, ret = _mix(glu, q, k, v, sg, conv_w[0], *norm_rows, tables, nb=nb, length=lp, **dims)
        x2, hp, lg = _post(cat, x_prompt, mod_p, *post_w, per_row_mod=False, b0=b0, nb=nb)
        return glu, ret, x2, hp, lg

    def pre_sample():
        xs3 = x_sample.reshape(1, bs, d)
        glu, q, k, v, sg = _proj(xs3, mod_s, row(g_mix[0]), w_in_b, cos_s, sin_s, per_row_mod=True, **dims)
        cat, ret, new_conv = _mix1(glu, q, k, v, sg, state_conv.transpose(0, 2, 1, 3), state_ret, conv_w[0],
                                   *norm_rows, gam, **dims)
        x2, hp, lg = _post(cat, xs3, mod_s, *post_w, per_row_mod=True)
        return new_conv.transpose(0, 2, 1, 3), ret, x2, hp, lg

    def route_and_dispatch(hps, lgs):
        lg = lgs[0] if len(lgs) == 1 else jnp.concatenate(lgs, axis=1)
        tokens = lg.shape[1]
        wts, cnt, dest = _route(lg, bias_col)
        counts = cnt[:, 0].astype(I32)
        n_rows = -(-(tokens * TOP_K + ne * (tm - 1)) // tm) * tm
        dest3 = dest.reshape(tokens // TOKEN_TILE, SUBLANES, TOKEN_TILE)
        xs = _dispatch([h.reshape(-1, hw) for h in hps], dest3, n_rows)
        return counts, dest3, wts, xs.reshape(n_rows * ROW_WORDS, LANES)

    def undispatch(ys, dest3, after=None):
        tokens = dest3.shape[0] * TOKEN_TILE
        z = _undispatch(ys.reshape(-1, hw), dest3, tokens, after)
        return z.reshape(TOP_K, tokens * ROW_WORDS, LANES)

    nb0 = bp // 2
    nb1 = bp - nb0
    new_conv_s, ret_s, x2_s, hp_s, lg_s = pre_sample()
    glu_0, ret_0, x2_0, hp_0, lg_0 = pre_prompt(0, nb0, after=lg_s)
    counts_0, dest_0, wts_0, xs_0 = route_and_dispatch([hp_0], [lg_0])
    glu_1, ret_1, x2_1, hp_1, lg_1 = pre_prompt(nb0, nb1, after=dest_0)
    counts_1, dest_1, wts_1, xs_1 = route_and_dispatch([hp_1, hp_s], [lg_1, lg_s])
    ys_0, ys_1 = _experts((counts_0, counts_1), (xs_0, xs_1), w1[0], w3[0], w2[0])
    z_0 = undispatch(ys_0, dest_0)
    z_1 = undispatch(ys_1, dest_1, after=z_0[0, :SUBLANES])
    y_p = _combine(z_0, 0, wts_0, x2_0, mod_p, row(g_ffn[0]), row(g_final), shared_w, rows_per_mod=lp, per_row_mod=False,
                   out_rows=bp * lp)
    y_p = _combine(z_1, 0, wts_1, x2_1, mod_p, row(g_ffn[0]), row(g_final), shared_w, rows_per_mod=lp, per_row_mod=False,
                   b0=nb0, out_rows=bp * lp, into=y_p)
    y_s = _combine(z_1, nb1 * lp, wts_1, x2_s, mod_s, row(g_ffn[0]), row(g_final), shared_w, rows_per_mod=bs,
                   per_row_mod=True)
    ret_p = jnp.concatenate([ret_0, ret_1], axis=0)

    tail = lambda g, n: g.reshape(n, lp, dc)[:, lp - CONV_BUF:, :]
    new_conv_p = jnp.concatenate([tail(glu_0, nb0), tail(glu_1, nb1)], axis=0)
    return (y_p.reshape(bp, lp, d), y_s.reshape(bs, ls, d), new_conv_p[None], ret_p[None],
            new_conv_s, ret_s)
```
